```python
import math
import jax, jax.numpy as jnp
from jax import lax
import numpy as np

D_MODEL = 1024
BATCH = 8
SEQ = 16384
DEPTH = 2

N_META = 16
N_MIXERS = 2
N_A_LAYERS = (DEPTH + 1) // 2
N_B_LAYERS = DEPTH // 2
SC_WIDTH = 3
D_RNN = 1280
RG_BLOCKS = 10
RG_BLOCK_DIM = D_RNN // RG_BLOCKS
RG_CONV_WIDTH = 4
RG_C = 8.0
D_FF = 2816
FFN_CONV_WIDTH = 3
RMS_EPS = 1e-6

kernel_name = "hybrid_shortconv_rglru_convffn"


def rms_norm(x, g):
    xf = x.astype(jnp.float32)
    var = jnp.mean(xf * xf, axis=-1, keepdims=True)
    return (xf * lax.rsqrt(var + RMS_EPS) * g.astype(jnp.float32)).astype(x.dtype)


def causal_dwconv(x, w):
    k_width = w.shape[0]
    t_len = x.shape[1]
    xp = jnp.pad(x, ((0, 0), (k_width - 1, 0), (0, 0)))
    y = xp[:, 0:t_len] * w[0]
    for k in range(1, k_width):
        y = y + xp[:, k:k + t_len] * w[k]
    return y


def short_conv_mixer(x, w_in, conv_w, w_out):
    h = jnp.einsum('btd,de->bte', x, w_in)
    b_gate, c_gate, v = jnp.split(h, 3, axis=-1)
    u = causal_dwconv(c_gate * v, conv_w)
    return jnp.einsum('btd,de->bte', b_gate * u, w_out)


def _lin_rec_combine(left, right):
    a_l, b_l = left
    a_r, b_r = right
    return a_l * a_r, a_r * b_l + b_r


def rglru_block(x, w_in, conv_w, conv_b, w_gate_a, b_gate_a, w_gate_x, b_gate_x, lam, w_out):
    bsz, t_len, _ = x.shape
    h = jnp.einsum('btd,de->bte', x, w_in)
    g_branch, r_branch = jnp.split(h, 2, axis=-1)
    gate = jax.nn.gelu(g_branch, approximate=True)
    u = causal_dwconv(r_branch, conv_w) + conv_b
    ub = u.reshape(bsz, t_len, RG_BLOCKS, RG_BLOCK_DIM)
    r = jax.nn.sigmoid(jnp.einsum('btki,kij->btkj', ub, w_gate_a).reshape(bsz, t_len, D_RNN) + b_gate_a)
    i = jax.nn.sigmoid(jnp.einsum('btki,kij->btkj', ub, w_gate_x).reshape(bsz, t_len, D_RNN) + b_gate_x)
    log_a = -RG_C * r.astype(jnp.float32) * jax.nn.softplus(-lam.astype(jnp.float32))
    a = jnp.exp(log_a)
    mult = jnp.sqrt(-jnp.expm1(2.0 * log_a))
    b = mult * (i * u).astype(jnp.float32)
    _, hs = lax.associative_scan(_lin_rec_combine, (a, b), axis=1)
    y = hs.astype(x.dtype) * gate
    return jnp.einsum('bte,ed->btd', y, w_out)


def conv_gated_mlp(x, w_up, conv_w, w_down):
    h = jnp.einsum('btd,df->btf', x, w_up)
    h = causal_dwconv(h, conv_w)
    g, v = jnp.split(h, 2, axis=-1)
    return jnp.einsum('btf,fd->btd', jax.nn.silu(g) * v, w_down)


def _fwd_setup_inputs(seed: int = 0) -> dict:
    key = jax.random.key(seed)
    ks = jax.random.split(key, 24)
    f32 = jnp.float32
    D = D_MODEL

    def nrm(k, shape, scale):
        return jax.random.normal(k, shape, f32) * scale

    x = jax.random.normal(ks[0], (BATCH, SEQ, D), f32)
    meta_tokens = nrm(ks[1], (N_META, D), 1.0)
    norm_mix_g = 1.0 + nrm(ks[2], (DEPTH, D), 0.01)
    norm_ffn_g = 1.0 + nrm(ks[3], (DEPTH, D), 0.01)
    final_norm_g = 1.0 + nrm(ks[4], (D,), 0.01)

    sc_w_in = nrm(ks[5], (N_A_LAYERS, D, 3 * D), D ** -0.5)
    sc_conv_w = nrm(ks[6], (N_A_LAYERS, SC_WIDTH, D), SC_WIDTH ** -0.5)
    sc_w_out = nrm(ks[7], (N_A_LAYERS, D, D), D ** -0.5)

    rg_w_in = nrm(ks[8], (N_B_LAYERS, D, 2 * D_RNN), D ** -0.5)
    rg_conv_w = nrm(ks[9], (N_B_LAYERS, RG_CONV_WIDTH, D_RNN), RG_CONV_WIDTH ** -0.5)
    rg_conv_b = nrm(ks[10], (N_B_LAYERS, D_RNN), 0.01)
    rg_w_gate_a = nrm(ks[11], (N_B_LAYERS, RG_BLOCKS, RG_BLOCK_DIM, RG_BLOCK_DIM), RG_BLOCK_DIM ** -0.5)
    rg_b_gate_a = nrm(ks[12], (N_B_LAYERS, D_RNN), 0.01)
    rg_w_gate_x = nrm(ks[13], (N_B_LAYERS, RG_BLOCKS, RG_BLOCK_DIM, RG_BLOCK_DIM), RG_BLOCK_DIM ** -0.5)
    rg_b_gate_x = nrm(ks[14], (N_B_LAYERS, D_RNN), 0.01)
    a_c = jax.random.uniform(ks[15], (N_B_LAYERS, D_RNN), f32, 0.9, 0.999)
    a_base = a_c ** (1.0 / RG_C)
    rg_lambda = jnp.log(a_base) - jnp.log1p(-a_base)
    rg_w_out = nrm(ks[16], (N_B_LAYERS, D_RNN, D), D_RNN ** -0.5)

    ffn_w_up = nrm(ks[17], (DEPTH, D, 2 * D_FF), D ** -0.5)
    ffn_conv_w = nrm(ks[18], (DEPTH, FFN_CONV_WIDTH, 2 * D_FF), FFN_CONV_WIDTH ** -0.5)
    ffn_w_down = nrm(ks[19], (DEPTH, D_FF, D), D_FF ** -0.5)

    return {"x": x, "meta_tokens": meta_tokens, "norm_mix_g": norm_mix_g,
            "norm_ffn_g": norm_ffn_g, "final_norm_g": final_norm_g,
            "sc_w_in": sc_w_in, "sc_conv_w": sc_conv_w, "sc_w_out": sc_w_out,
            "rg_w_in": rg_w_in, "rg_conv_w": rg_conv_w, "rg_conv_b": rg_conv_b,
            "rg_w_gate_a": rg_w_gate_a, "rg_b_gate_a": rg_b_gate_a,
            "rg_w_gate_x": rg_w_gate_x, "rg_b_gate_x": rg_b_gate_x,
            "rg_lambda": rg_lambda, "rg_w_out": rg_w_out,
            "ffn_w_up": ffn_w_up, "ffn_conv_w": ffn_conv_w, "ffn_w_down": ffn_w_down}


def _fwd_reference(x, meta_tokens, norm_mix_g, norm_ffn_g, final_norm_g,
              sc_w_in, sc_conv_w, sc_w_out,
              rg_w_in, rg_conv_w, rg_conv_b, rg_w_gate_a, rg_b_gate_a,
              rg_w_gate_x, rg_b_gate_x, rg_lambda, rg_w_out,
              ffn_w_up, ffn_conv_w, ffn_w_down):
    bsz = x.shape[0]
    meta = jnp.broadcast_to(meta_tokens.astype(x.dtype)[None], (bsz, N_META, x.shape[-1]))
    h = jnp.concatenate([meta, x], axis=1)
    for layer in range(DEPTH):
        hn = rms_norm(h, norm_mix_g[layer])
        j = layer // N_MIXERS
        if layer % N_MIXERS == 0:
            mix = short_conv_mixer(hn, sc_w_in[j], sc_conv_w[j], sc_w_out[j])
        else:
            mix = rglru_block(hn, rg_w_in[j], rg_conv_w[j], rg_conv_b[j],
                              rg_w_gate_a[j], rg_b_gate_a[j], rg_w_gate_x[j], rg_b_gate_x[j],
                              rg_lambda[j], rg_w_out[j])
        h = h + mix
        h = h + conv_gated_mlp(rms_norm(h, norm_ffn_g[layer]), ffn_w_up[layer],
                               ffn_conv_w[layer], ffn_w_down[layer])
    out = rms_norm(h, final_norm_g)
    return out[:, N_META:]


import jax as _jax
import jax.numpy as _jnp

TWIN_FORMAT = 'train_step'
FWD_PARAMS = ['x', 'meta_tokens', 'norm_mix_g', 'norm_ffn_g', 'final_norm_g', 'sc_w_in', 'sc_conv_w', 'sc_w_out', 'rg_w_in', 'rg_conv_w', 'rg_conv_b', 'rg_w_gate_a', 'rg_b_gate_a', 'rg_w_gate_x', 'rg_b_gate_x', 'rg_lambda', 'rg_w_out', 'ffn_w_up', 'ffn_conv_w', 'ffn_w_down']
TWIN_WEIGHTS = ['meta_tokens', 'norm_mix_g', 'norm_ffn_g', 'final_norm_g', 'sc_w_in', 'sc_conv_w', 'sc_w_out', 'rg_w_in', 'rg_conv_w', 'rg_conv_b', 'rg_w_gate_a', 'rg_b_gate_a', 'rg_w_gate_x', 'rg_b_gate_x', 'rg_lambda', 'rg_w_out', 'ffn_w_up', 'ffn_conv_w', 'ffn_w_down']
TWIN_DIFF_INPUT = 'x'
TWIN_INPUTS = ['x', 'meta_tokens', 'norm_mix_g', 'norm_ffn_g', 'final_norm_g', 'sc_w_in', 'sc_conv_w', 'sc_w_out', 'rg_w_in', 'rg_conv_w', 'rg_conv_b', 'rg_w_gate_a', 'rg_b_gate_a', 'rg_w_gate_x', 'rg_b_gate_x', 'rg_lambda', 'rg_w_out', 'ffn_w_up', 'ffn_conv_w', 'ffn_w_down', 'loss_target', 'm_meta_tokens', 'm_norm_mix_g', 'm_norm_ffn_g', 'm_final_norm_g', 'm_sc_w_in', 'm_sc_conv_w', 'm_sc_w_out', 'm_rg_w_in', 'm_rg_conv_w', 'm_rg_conv_b', 'm_rg_w_gate_a', 'm_rg_b_gate_a', 'm_rg_w_gate_x', 'm_rg_b_gate_x', 'm_rg_lambda', 'm_rg_w_out', 'm_ffn_w_up', 'm_ffn_conv_w', 'm_ffn_w_down', 'v_meta_tokens', 'v_norm_mix_g', 'v_norm_ffn_g', 'v_final_norm_g', 'v_sc_w_in', 'v_sc_conv_w', 'v_sc_w_out', 'v_rg_w_in', 'v_rg_conv_w', 'v_rg_conv_b', 'v_rg_w_gate_a', 'v_rg_b_gate_a', 'v_rg_w_gate_x', 'v_rg_b_gate_x', 'v_rg_lambda', 'v_rg_w_out', 'v_ffn_w_up', 'v_ffn_conv_w', 'v_ffn_w_down']
TWIN_OUTPUTS = ['loss', 'grad_x', 'grad_meta_tokens', 'grad_norm_mix_g', 'grad_norm_ffn_g', 'grad_final_norm_g', 'grad_sc_w_in', 'grad_sc_conv_w', 'grad_sc_w_out', 'grad_rg_w_in', 'grad_rg_conv_w', 'grad_rg_conv_b', 'grad_rg_w_gate_a', 'grad_rg_b_gate_a', 'grad_rg_w_gate_x', 'grad_rg_b_gate_x', 'grad_rg_lambda', 'grad_rg_w_out', 'grad_ffn_w_up', 'grad_ffn_conv_w', 'grad_ffn_w_down', 'delta_meta_tokens', 'delta_norm_mix_g', 'delta_norm_ffn_g', 'delta_final_norm_g', 'delta_sc_w_in', 'delta_sc_conv_w', 'delta_sc_w_out', 'delta_rg_w_in', 'delta_rg_conv_w', 'delta_rg_conv_b', 'delta_rg_w_gate_a', 'delta_rg_b_gate_a', 'delta_rg_w_gate_x', 'delta_rg_b_gate_x', 'delta_rg_lambda', 'delta_rg_w_out', 'delta_ffn_w_up', 'delta_ffn_conv_w', 'delta_ffn_w_down', 'new_m_meta_tokens', 'new_m_norm_mix_g', 'new_m_norm_ffn_g', 'new_m_final_norm_g', 'new_m_sc_w_in', 'new_m_sc_conv_w', 'new_m_sc_w_out', 'new_m_rg_w_in', 'new_m_rg_conv_w', 'new_m_rg_conv_b', 'new_m_rg_w_gate_a', 'new_m_rg_b_gate_a', 'new_m_rg_w_gate_x', 'new_m_rg_b_gate_x', 'new_m_rg_lambda', 'new_m_rg_w_out', 'new_m_ffn_w_up', 'new_m_ffn_conv_w', 'new_m_ffn_w_down', 'new_v_meta_tokens', 'new_v_norm_mix_g', 'new_v_norm_ffn_g', 'new_v_final_norm_g', 'new_v_sc_w_in', 'new_v_sc_conv_w', 'new_v_sc_w_out', 'new_v_rg_w_in', 'new_v_rg_conv_w', 'new_v_rg_conv_b', 'new_v_rg_w_gate_a', 'new_v_rg_b_gate_a', 'new_v_rg_w_gate_x', 'new_v_rg_b_gate_x', 'new_v_rg_lambda', 'new_v_rg_w_out', 'new_v_ffn_w_up', 'new_v_ffn_conv_w', 'new_v_ffn_w_down']
TWIN_LEAF_KINDS = {'loss': 'loss', 'grad_x': 'grad_x', 'grad_meta_tokens': 'grad_w', 'grad_norm_mix_g': 'grad_w', 'grad_norm_ffn_g': 'grad_w', 'grad_final_norm_g': 'grad_w', 'grad_sc_w_in': 'grad_w', 'grad_sc_conv_w': 'grad_w', 'grad_sc_w_out': 'grad_w', 'grad_rg_w_in': 'grad_w', 'grad_rg_conv_w': 'grad_w', 'grad_rg_conv_b': 'grad_w', 'grad_rg_w_gate_a': 'grad_w', 'grad_rg_b_gate_a': 'grad_w', 'grad_rg_w_gate_x': 'grad_w', 'grad_rg_b_gate_x': 'grad_w', 'grad_rg_lambda': 'grad_w', 'grad_rg_w_out': 'grad_w', 'grad_ffn_w_up': 'grad_w', 'grad_ffn_conv_w': 'grad_w', 'grad_ffn_w_down': 'grad_w', 'delta_meta_tokens': 'delta_w', 'delta_norm_mix_g': 'delta_w', 'delta_norm_ffn_g': 'delta_w', 'delta_final_norm_g': 'delta_w', 'delta_sc_w_in': 'delta_w', 'delta_sc_conv_w': 'delta_w', 'delta_sc_w_out': 'delta_w', 'delta_rg_w_in': 'delta_w', 'delta_rg_conv_w': 'delta_w', 'delta_rg_conv_b': 'delta_w', 'delta_rg_w_gate_a': 'delta_w', 'delta_rg_b_gate_a': 'delta_w', 'delta_rg_w_gate_x': 'delta_w', 'delta_rg_b_gate_x': 'delta_w', 'delta_rg_lambda': 'delta_w', 'delta_rg_w_out': 'delta_w', 'delta_ffn_w_up': 'delta_w', 'delta_ffn_conv_w': 'delta_w', 'delta_ffn_w_down': 'delta_w', 'new_m_meta_tokens': 'new_m', 'new_m_norm_mix_g': 'new_m', 'new_m_norm_ffn_g': 'new_m', 'new_m_final_norm_g': 'new_m', 'new_m_sc_w_in': 'new_m', 'new_m_sc_conv_w': 'new_m', 'new_m_sc_w_out': 'new_m', 'new_m_rg_w_in': 'new_m', 'new_m_rg_conv_w': 'new_m', 'new_m_rg_conv_b': 'new_m', 'new_m_rg_w_gate_a': 'new_m', 'new_m_rg_b_gate_a': 'new_m', 'new_m_rg_w_gate_x': 'new_m', 'new_m_rg_b_gate_x': 'new_m', 'new_m_rg_lambda': 'new_m', 'new_m_rg_w_out': 'new_m', 'new_m_ffn_w_up': 'new_m', 'new_m_ffn_conv_w': 'new_m', 'new_m_ffn_w_down': 'new_m', 'new_v_meta_tokens': 'new_v', 'new_v_norm_mix_g': 'new_v', 'new_v_norm_ffn_g': 'new_v', 'new_v_final_norm_g': 'new_v', 'new_v_sc_w_in': 'new_v', 'new_v_sc_conv_w': 'new_v', 'new_v_sc_w_out': 'new_v', 'new_v_rg_w_in': 'new_v', 'new_v_rg_conv_w': 'new_v', 'new_v_rg_conv_b': 'new_v', 'new_v_rg_w_gate_a': 'new_v', 'new_v_rg_b_gate_a': 'new_v', 'new_v_rg_w_gate_x': 'new_v', 'new_v_rg_b_gate_x': 'new_v', 'new_v_rg_lambda': 'new_v', 'new_v_rg_w_out': 'new_v', 'new_v_ffn_w_up': 'new_v', 'new_v_ffn_conv_w': 'new_v', 'new_v_ffn_w_down': 'new_v'}


def _forward(args):
    return _fwd_reference(*[args[k] for k in FWD_PARAMS])


def _output_shape():
    def fwd():
        inp = _fwd_setup_inputs(0)
        return _fwd_reference(*[inp[k] for k in FWD_PARAMS])
    out = _jax.eval_shape(fwd)
    return out.shape, out.dtype

N_MICROBATCH = 1
ADAM_LR = 0.001
ADAM_B1 = 0.9
ADAM_B2 = 0.999
ADAM_EPS = 1e-08
ADAM_WD = 0.01
ADAM_STEP = 10
PER_EXAMPLE_BATCH_AXIS = {'x': 0, 'loss_target': 0}
SHARED_INPUTS = []
_WEIGHT_DTYPES = {'meta_tokens': _jnp.float32, 'norm_mix_g': _jnp.float32, 'norm_ffn_g': _jnp.float32, 'final_norm_g': _jnp.float32, 'sc_w_in': _jnp.float32, 'sc_conv_w': _jnp.float32, 'sc_w_out': _jnp.float32, 'rg_w_in': _jnp.float32, 'rg_conv_w': _jnp.float32, 'rg_conv_b': _jnp.float32, 'rg_w_gate_a': _jnp.float32, 'rg_b_gate_a': _jnp.float32, 'rg_w_gate_x': _jnp.float32, 'rg_b_gate_x': _jnp.float32, 'rg_lambda': _jnp.float32, 'rg_w_out': _jnp.float32, 'ffn_w_up': _jnp.float32, 'ffn_conv_w': _jnp.float32, 'ffn_w_down': _jnp.float32}
MOMENT_SCALE = {'meta_tokens': 1.204723e-02, 'norm_mix_g': 4.349933e-01, 'norm_ffn_g': 2.009556e-01, 'final_norm_g': 1.278659e+02, 'sc_w_in': 2.957523e-01, 'sc_conv_w': 2.979141e-01, 'sc_w_out': 2.972296e-01, 'rg_w_in': 9.451037e-02, 'rg_conv_w': 9.736555e-02, 'rg_conv_b': 1.307992e+00, 'rg_w_gate_a': 3.345986e-02, 'rg_b_gate_a': 2.729981e-02, 'rg_w_gate_x': 6.055281e-02, 'rg_b_gate_x': 2.904729e-02, 'rg_lambda': 5.176107e-02, 'rg_w_out': 1.099865e-01, 'ffn_w_up': 8.587234e-02, 'ffn_conv_w': 8.688155e-02, 'ffn_w_down': 1.400662e-01}


def _to_microbatches(a, axis):
    t = _jnp.moveaxis(a, axis, 0)
    t = t.reshape((N_MICROBATCH, t.shape[0] // N_MICROBATCH) + t.shape[1:])
    return _jnp.moveaxis(t, 1, axis + 1)


def setup_inputs(seed: int = 0) -> dict:
    inp = _fwd_setup_inputs(seed)
    key = _jax.random.fold_in(_jax.random.key(seed), 7919)
    shape, _ = _output_shape()
    out = dict(inp)
    out["loss_target"] = _jax.random.normal(_jax.random.fold_in(key, 0), shape, _jnp.float32)
    for i, name in enumerate(TWIN_WEIGHTS):
        w = inp[name].astype(_jnp.float32)
        if MOMENT_SCALE is None:
            s = _jnp.sqrt(_jnp.mean(_jnp.square(w)) + 1e-30)
        else:
            s = MOMENT_SCALE[name]
        km, kv = _jax.random.split(_jax.random.fold_in(key, i + 1))
        out[name] = w
        out["m_" + name] = s * _jax.random.normal(km, w.shape, _jnp.float32)
        out["v_" + name] = (s * s) * _jax.random.uniform(kv, w.shape, _jnp.float32, 0.5, 1.5)
    if N_MICROBATCH > 1:
        for name, axis in PER_EXAMPLE_BATCH_AXIS.items():
            out[name] = _to_microbatches(out[name], axis)
    return {'x': out['x'], 'meta_tokens': out['meta_tokens'], 'norm_mix_g': out['norm_mix_g'], 'norm_ffn_g': out['norm_ffn_g'], 'final_norm_g': out['final_norm_g'], 'sc_w_in': out['sc_w_in'], 'sc_conv_w': out['sc_conv_w'], 'sc_w_out': out['sc_w_out'], 'rg_w_in': out['rg_w_in'], 'rg_conv_w': out['rg_conv_w'], 'rg_conv_b': out['rg_conv_b'], 'rg_w_gate_a': out['rg_w_gate_a'], 'rg_b_gate_a': out['rg_b_gate_a'], 'rg_w_gate_x': out['rg_w_gate_x'], 'rg_b_gate_x': out['rg_b_gate_x'], 'rg_lambda': out['rg_lambda'], 'rg_w_out': out['rg_w_out'], 'ffn_w_up': out['ffn_w_up'], 'ffn_conv_w': out['ffn_conv_w'], 'ffn_w_down': out['ffn_w_down'], 'loss_target': out['loss_target'], 'm_meta_tokens': out['m_meta_tokens'], 'm_norm_mix_g': out['m_norm_mix_g'], 'm_norm_ffn_g': out['m_norm_ffn_g'], 'm_final_norm_g': out['m_final_norm_g'], 'm_sc_w_in': out['m_sc_w_in'], 'm_sc_conv_w': out['m_sc_conv_w'], 'm_sc_w_out': out['m_sc_w_out'], 'm_rg_w_in': out['m_rg_w_in'], 'm_rg_conv_w': out['m_rg_conv_w'], 'm_rg_conv_b': out['m_rg_conv_b'], 'm_rg_w_gate_a': out['m_rg_w_gate_a'], 'm_rg_b_gate_a': out['m_rg_b_gate_a'], 'm_rg_w_gate_x': out['m_rg_w_gate_x'], 'm_rg_b_gate_x': out['m_rg_b_gate_x'], 'm_rg_lambda': out['m_rg_lambda'], 'm_rg_w_out': out['m_rg_w_out'], 'm_ffn_w_up': out['m_ffn_w_up'], 'm_ffn_conv_w': out['m_ffn_conv_w'], 'm_ffn_w_down': out['m_ffn_w_down'], 'v_meta_tokens': out['v_meta_tokens'], 'v_norm_mix_g': out['v_norm_mix_g'], 'v_norm_ffn_g': out['v_norm_ffn_g'], 'v_final_norm_g': out['v_final_norm_g'], 'v_sc_w_in': out['v_sc_w_in'], 'v_sc_conv_w': out['v_sc_conv_w'], 'v_sc_w_out': out['v_sc_w_out'], 'v_rg_w_in': out['v_rg_w_in'], 'v_rg_conv_w': out['v_rg_conv_w'], 'v_rg_conv_b': out['v_rg_conv_b'], 'v_rg_w_gate_a': out['v_rg_w_gate_a'], 'v_rg_b_gate_a': out['v_rg_b_gate_a'], 'v_rg_w_gate_x': out['v_rg_w_gate_x'], 'v_rg_b_gate_x': out['v_rg_b_gate_x'], 'v_rg_lambda': out['v_rg_lambda'], 'v_rg_w_out': out['v_rg_w_out'], 'v_ffn_w_up': out['v_ffn_w_up'], 'v_ffn_conv_w': out['v_ffn_conv_w'], 'v_ffn_w_down': out['v_ffn_w_down']}


def _loss(weights, diff, rest, loss_target):
    with _jax.named_scope("forward"):
        args = {**rest, TWIN_DIFF_INPUT: diff, **{k: w.astype(_WEIGHT_DTYPES[k]) for k, w in weights.items()}}
        y = _forward(args)
    with _jax.named_scope("loss_head"):
        err = _jnp.square(y.astype(_jnp.float32) - loss_target)
        return 0.5 * _jnp.sum(_jnp.mean(err, axis=-1)) if err.ndim else 0.5 * err


def _adamw(w, g, m, v):
    m = ADAM_B1 * m + (1.0 - ADAM_B1) * g
    v = ADAM_B2 * v + (1.0 - ADAM_B2) * _jnp.square(g)
    m_hat = m / (1.0 - ADAM_B1 ** ADAM_STEP)
    v_hat = v / (1.0 - ADAM_B2 ** ADAM_STEP)
    delta = -ADAM_LR * (m_hat / (_jnp.sqrt(v_hat) + ADAM_EPS) + ADAM_WD * w)
    return delta, m, v


def reference(x, meta_tokens, norm_mix_g, norm_ffn_g, final_norm_g, sc_w_in, sc_conv_w, sc_w_out, rg_w_in, rg_conv_w, rg_conv_b, rg_w_gate_a, rg_b_gate_a, rg_w_gate_x, rg_b_gate_x, rg_lambda, rg_w_out, ffn_w_up, ffn_conv_w, ffn_w_down, loss_target, m_meta_tokens, m_norm_mix_g, m_norm_ffn_g, m_final_norm_g, m_sc_w_in, m_sc_conv_w, m_sc_w_out, m_rg_w_in, m_rg_conv_w, m_rg_conv_b, m_rg_w_gate_a, m_rg_b_gate_a, m_rg_w_gate_x, m_rg_b_gate_x, m_rg_lambda, m_rg_w_out, m_ffn_w_up, m_ffn_conv_w, m_ffn_w_down, v_meta_tokens, v_norm_mix_g, v_norm_ffn_g, v_final_norm_g, v_sc_w_in, v_sc_conv_w, v_sc_w_out, v_rg_w_in, v_rg_conv_w, v_rg_conv_b, v_rg_w_gate_a, v_rg_b_gate_a, v_rg_w_gate_x, v_rg_b_gate_x, v_rg_lambda, v_rg_w_out, v_ffn_w_up, v_ffn_conv_w, v_ffn_w_down):
    given = dict(x=x, meta_tokens=meta_tokens, norm_mix_g=norm_mix_g, norm_ffn_g=norm_ffn_g, final_norm_g=final_norm_g, sc_w_in=sc_w_in, sc_conv_w=sc_conv_w, sc_w_out=sc_w_out, rg_w_in=rg_w_in, rg_conv_w=rg_conv_w, rg_conv_b=rg_conv_b, rg_w_gate_a=rg_w_gate_a, rg_b_gate_a=rg_b_gate_a, rg_w_gate_x=rg_w_gate_x, rg_b_gate_x=rg_b_gate_x, rg_lambda=rg_lambda, rg_w_out=rg_w_out, ffn_w_up=ffn_w_up, ffn_conv_w=ffn_conv_w, ffn_w_down=ffn_w_down, loss_target=loss_target, m_meta_tokens=m_meta_tokens, m_norm_mix_g=m_norm_mix_g, m_norm_ffn_g=m_norm_ffn_g, m_final_norm_g=m_final_norm_g, m_sc_w_in=m_sc_w_in, m_sc_conv_w=m_sc_conv_w, m_sc_w_out=m_sc_w_out, m_rg_w_in=m_rg_w_in, m_rg_conv_w=m_rg_conv_w, m_rg_conv_b=m_rg_conv_b, m_rg_w_gate_a=m_rg_w_gate_a, m_rg_b_gate_a=m_rg_b_gate_a, m_rg_w_gate_x=m_rg_w_gate_x, m_rg_b_gate_x=m_rg_b_gate_x, m_rg_lambda=m_rg_lambda, m_rg_w_out=m_rg_w_out, m_ffn_w_up=m_ffn_w_up, m_ffn_conv_w=m_ffn_conv_w, m_ffn_w_down=m_ffn_w_down, v_meta_tokens=v_meta_tokens, v_norm_mix_g=v_norm_mix_g, v_norm_ffn_g=v_norm_ffn_g, v_final_norm_g=v_final_norm_g, v_sc_w_in=v_sc_w_in, v_sc_conv_w=v_sc_conv_w, v_sc_w_out=v_sc_w_out, v_rg_w_in=v_rg_w_in, v_rg_conv_w=v_rg_conv_w, v_rg_conv_b=v_rg_conv_b, v_rg_w_gate_a=v_rg_w_gate_a, v_rg_b_gate_a=v_rg_b_gate_a, v_rg_w_gate_x=v_rg_w_gate_x, v_rg_b_gate_x=v_rg_b_gate_x, v_rg_lambda=v_rg_lambda, v_rg_w_out=v_rg_w_out, v_ffn_w_up=v_ffn_w_up, v_ffn_conv_w=v_ffn_conv_w, v_ffn_w_down=v_ffn_w_down)
    weights = {n: given[n] for n in TWIN_WEIGHTS}
    shared = {n: given[n] for n in SHARED_INPUTS}
    per_example = {n: given[n] for n in ['x']}
    grad_fn = _jax.value_and_grad(_loss, argnums=(0, 1))

    def one_microbatch(ex, loss_target):
        ex = dict(ex)
        diff = ex.pop(TWIN_DIFF_INPUT)
        return grad_fn(weights, diff, {**shared, **ex}, loss_target)

    if N_MICROBATCH == 1:
        loss, (grad_w, grad_x) = one_microbatch(per_example, given["loss_target"])
    else:
        def body(carry, xs):
            loss_sum, grad_sum = carry
            l_k, (gw_k, gx_k) = one_microbatch(xs[0], xs[1])
            with _jax.named_scope("update"):
                return (loss_sum + l_k, _jax.tree.map(_jnp.add, grad_sum, gw_k)), gx_k

        init = (_jnp.zeros((), _jnp.float32), _jax.tree.map(_jnp.zeros_like, weights))
        (loss, grad_w), grad_x = _jax.lax.scan(body, init, (per_example, given["loss_target"]))
    with _jax.named_scope("update"):
        delta_w, new_m, new_v = {}, {}, {}
        for n in TWIN_WEIGHTS:
            delta_w[n], new_m[n], new_v[n] = _adamw(weights[n], grad_w[n], given["m_" + n], given["v_" + n])
    return (loss, grad_x, *[grad_w[n] for n in TWIN_WEIGHTS], *[delta_w[n] for n in TWIN_WEIGHTS],
            *[new_m[n] for n in TWIN_WEIGHTS], *[new_v[n] for n in TWIN_WEIGHTS])
```

```python
import jax
import jax.numpy as jnp
from jax import lax
from jax.experimental import pallas as pl
from jax.experimental.pallas import tpu as pltpu

F32 = jnp.float32
MXU_DTYPE = jnp.bfloat16
RMS_EPS = 1e-6
RG_C = 8.0
ADAM_LR = 0.001
ADAM_B1 = 0.9
ADAM_B2 = 0.999
ADAM_EPS = 1e-08
ADAM_WD = 0.01
ADAM_STEP = 10

N_DEV = 8
AXES = ("x", "y", "c")
SUBLANES = 8
LANES = 128
VMEM_LIMIT_BYTES = 48 * 1024 * 1024
ROW_TILE_MATMUL = 700
ROW_TILE_MID = 128
STRIP = 256

_NT = (((1,), (1,)), ((), ()))
_TN = (((0,), (0,)), ((), ()))
_NN = (((1,), (0,)), ((), ()))


def _row_tile(t, target):
    best = None
    for tm in range(16, t + 1, 16):
        if t % tm == 0 and tm <= target:
            best = tm
    return best if best is not None else t


def _col_tile(n, target):
    best = None
    for tn in range(LANES, n + 1, LANES):
        if n % tn == 0 and tn <= target:
            best = tn
    return best if best is not None else n


def _params(*sem):
    return pltpu.CompilerParams(dimension_semantics=sem, vmem_limit_bytes=VMEM_LIMIT_BYTES)


def _dot(a, b, dims):
    return lax.dot_general(a, b, dims, preferred_element_type=F32)


def _sigmoid(x):
    return 1.0 / (1.0 + jnp.exp(-x))


def _gelu(x):
    c = 0.7978845608028654
    t = jnp.tanh(c * (x + 0.044715 * (x * x * x)))
    return 0.5 * x * (1.0 + t)


def _gelu_grad(x):
    c = 0.7978845608028654
    t = jnp.tanh(c * (x + 0.044715 * (x * x * x)))
    return 0.5 * (1.0 + t) + 0.5 * x * (1.0 - t * t) * c * (1.0 + 3.0 * 0.044715 * x * x)


def _softplus(x):
    return jnp.maximum(x, 0.0) + jnp.log1p(jnp.exp(-jnp.abs(x)))


def _shift_down(x, halo, s):
    if s == 0:
        return x
    n = x.shape[0]
    r = pltpu.roll(x, s, axis=0)
    hr = pltpu.roll(halo, s, axis=0)
    row = lax.broadcasted_iota(jnp.int32, (SUBLANES, x.shape[1]), 0)
    first = jnp.where(row < s, hr, r[0:SUBLANES])
    if n == SUBLANES:
        return first
    return jnp.concatenate([first, r[SUBLANES:]], axis=0)


def _shift_up(x, halo, s):
    if s == 0:
        return x
    n = x.shape[0]
    r = pltpu.roll(x, n - s, axis=0)
    hr = pltpu.roll(halo, SUBLANES - s, axis=0)
    row = lax.broadcasted_iota(jnp.int32, (SUBLANES, x.shape[1]), 0)
    last = jnp.where(row >= SUBLANES - s, hr, r[n - SUBLANES:n])
    if n == SUBLANES:
        return last
    return jnp.concatenate([r[:n - SUBLANES], last], axis=0)


def _conv(x, halo, w):
    k_width = len(w)
    y = w[k_width - 1] * x
    for k in range(k_width - 1):
        y = y + w[k] * _shift_down(x, halo, k_width - 1 - k)
    return y


def _conv_t(dy, halo_after, w):
    k_width = len(w)
    dx = w[k_width - 1] * dy
    for k in range(k_width - 1):
        dx = dx + w[k] * _shift_up(dy, halo_after, k_width - 1 - k)
    return dx


def _rows(ref, sl):
    return [ref[k:k + 1, sl] for k in range(ref.shape[0])]


def _halo_before_spec(tm, cols):
    return pl.BlockSpec((SUBLANES, cols), lambda i: (jnp.maximum(i * (tm // SUBLANES) - 1, 0), 0))


def _halo_after_spec(tm, cols, t):
    last = t // SUBLANES - 1
    return pl.BlockSpec((SUBLANES, cols), lambda i: (jnp.minimum((i + 1) * (tm // SUBLANES), last), 0))


def _norm_matmul(h, g, wt, name):
    t, d = h.shape
    n = wt.shape[0]
    tm = _row_tile(t, ROW_TILE_MATMUL)
    tn = _col_tile(n, 512)

    def body(h_ref, g_ref, wt_ref, n_ref, p_ref):
        @pl.when(pl.program_id(1) == 0)
        def _():
            x = h_ref[...]
            ms = jnp.mean(x * x, axis=-1, keepdims=True)
            n_ref[...] = (x * lax.rsqrt(ms + RMS_EPS) * g_ref[...]).astype(n_ref.dtype)
        p_ref[...] = _dot(n_ref[...], wt_ref[...], _NT)

    return pl.pallas_call(
        body, name=name, grid=(t // tm, n // tn),
        in_specs=[pl.BlockSpec((tm, d), lambda i, j: (i, 0)),
                  pl.BlockSpec((1, d), lambda i, j: (0, 0)),
                  pl.BlockSpec((tn, d), lambda i, j: (j, 0))],
        out_specs=[pl.BlockSpec((tm, d), lambda i, j: (i, 0)),
                   pl.BlockSpec((tm, tn), lambda i, j: (i, j))],
        out_shape=[jax.ShapeDtypeStruct((t, d), MXU_DTYPE), jax.ShapeDtypeStruct((t, n), F32)],
        compiler_params=_params("parallel", "arbitrary"),
    )(h, g, wt)


def _matmul_residual(q, w, h, name):
    t, k = q.shape
    d = w.shape[1]
    tm = _row_tile(t, ROW_TILE_MATMUL)

    def body(q_ref, w_ref, h_ref, o_ref):
        o_ref[...] = h_ref[...] + _dot(q_ref[...], w_ref[...], _NN)

    return pl.pallas_call(
        body, name=name, grid=(t // tm,),
        in_specs=[pl.BlockSpec((tm, k), lambda i: (i, 0)),
                  pl.BlockSpec((k, d), lambda i: (0, 0)),
                  pl.BlockSpec((tm, d), lambda i: (i, 0))],
        out_specs=pl.BlockSpec((tm, d), lambda i: (i, 0)),
        out_shape=jax.ShapeDtypeStruct((t, d), F32),
        compiler_params=_params("parallel"),
    )(q, w, h)


def _dgrad_out(dh, w, name):
    t, d = dh.shape
    k = w.shape[0]
    tm = _row_tile(t, ROW_TILE_MATMUL)
    tk = _col_tile(k, 768)

    def body(dh_ref, w_ref, dq_ref, dhb_ref):
        @pl.when(pl.program_id(1) == 0)
        def _():
            dhb_ref[...] = dh_ref[...].astype(dhb_ref.dtype)
        dq_ref[...] = _dot(dhb_ref[...], w_ref[...], _NT)

    return pl.pallas_call(
        body, name=name, grid=(t // tm, k // tk),
        in_specs=[pl.BlockSpec((tm, d), lambda i, j: (i, 0)),
                  pl.BlockSpec((tk, d), lambda i, j: (j, 0))],
        out_specs=[pl.BlockSpec((tm, tk), lambda i, j: (i, j)),
                   pl.BlockSpec((tm, d), lambda i, j: (i, 0))],
        out_shape=[jax.ShapeDtypeStruct((t, k), F32), jax.ShapeDtypeStruct((t, d), MXU_DTYPE)],
        compiler_params=_params("parallel", "arbitrary"),
    )(dh, w)


def _dgrad_in_norm(dp, wt, h, g, dh_next, name):
    t, n = dp.shape
    d = wt.shape[1]
    tm = _row_tile(t, ROW_TILE_MATMUL)
    tn = _col_tile(n, 1536)
    nj = n // tn

    def body(dp_ref, wt_ref, h_ref, g_ref, dhn_ref, dh_ref, dg_ref, acc_ref):
        i, j = pl.program_id(0), pl.program_id(1)

        @pl.when(j == 0)
        def _():
            acc_ref[...] = jnp.zeros_like(acc_ref)

        acc_ref[...] += _dot(dp_ref[...], wt_ref[...], _NN)

        @pl.when(j == nj - 1)
        def _():
            x = h_ref[...]
            ms = jnp.mean(x * x, axis=-1, keepdims=True)
            r = lax.rsqrt(ms + RMS_EPS)
            xhat = x * r
            dn = acc_ref[...]
            dng = dn * g_ref[...]
            c = jnp.mean(dng * xhat, axis=-1, keepdims=True)
            dh_ref[...] = dhn_ref[...] + r * (dng - xhat * c)
            part = jnp.sum(dn * xhat, axis=0, keepdims=True)

            @pl.when(i == 0)
            def _():
                dg_ref[...] = part

            @pl.when(i > 0)
            def _():
                dg_ref[...] += part

    return pl.pallas_call(
        body, name=name, grid=(t // tm, nj),
        in_specs=[pl.BlockSpec((tm, tn), lambda i, j: (i, j)),
                  pl.BlockSpec((tn, d), lambda i, j: (j, 0)),
                  pl.BlockSpec((tm, d), lambda i, j: (i, 0)),
                  pl.BlockSpec((1, d), lambda i, j: (0, 0)),
                  pl.BlockSpec((tm, d), lambda i, j: (i, 0))],
        out_specs=[pl.BlockSpec((tm, d), lambda i, j: (i, 0)),
                   pl.BlockSpec((1, d), lambda i, j: (0, 0))],
        out_shape=[jax.ShapeDtypeStruct((t, d), F32), jax.ShapeDtypeStruct((1, d), F32)],
        scratch_shapes=[pltpu.VMEM((tm, d), F32)],
        compiler_params=_params("arbitrary", "arbitrary"),
    )(dp, wt, h, g, dh_next)


def _wgrad(a, b, name):
    t, m = a.shape
    d = b.shape[1]
    tmm = _col_tile(m, 1536)
    tk = _row_tile(t, ROW_TILE_MATMUL)

    def body(a_ref, b_ref, o_ref):
        @pl.when(pl.program_id(1) == 0)
        def _():
            o_ref[...] = jnp.zeros_like(o_ref)
        o_ref[...] += _dot(a_ref[...], b_ref[...], _TN)

    return pl.pallas_call(
        body, name=name, grid=(m // tmm, t // tk),
        in_specs=[pl.BlockSpec((tk, tmm), lambda i, k: (k, i)),
                  pl.BlockSpec((tk, d), lambda i, k: (k, 0))],
        out_specs=pl.BlockSpec((tmm, d), lambda i, k: (i, 0)),
        out_shape=jax.ShapeDtypeStruct((m, d), F32),
        compiler_params=_params("parallel", "arbitrary"),
    )(a, b)


def _sc_mid_fwd(p, conv_w, name):
    t, d3 = p.shape
    d = d3 // 3
    tm = _row_tile(t, ROW_TILE_MID)
    cw = min(STRIP, d)

    def body(p_ref, ph_ref, w_ref, q_ref):
        first = pl.program_id(0) == 0
        for c in range(0, d, cw):
            sl = slice(c, c + cw)
            bg = p_ref[:, c:c + cw]
            cv = p_ref[:, d + c:d + c + cw] * p_ref[:, 2 * d + c:2 * d + c + cw]
            hcv = jnp.where(first, 0.0, ph_ref[:, d + c:d + c + cw] * ph_ref[:, 2 * d + c:2 * d + c + cw])
            u = _conv(cv, hcv, _rows(w_ref, sl))
            q_ref[:, sl] = (bg * u).astype(q_ref.dtype)

    return pl.pallas_call(
        body, name=name, grid=(t // tm,),
        in_specs=[pl.BlockSpec((tm, d3), lambda i: (i, 0)),
                  _halo_before_spec(tm, d3),
                  pl.BlockSpec(conv_w.shape, lambda i: (0, 0))],
        out_specs=pl.BlockSpec((tm, d), lambda i: (i, 0)),
        out_shape=jax.ShapeDtypeStruct((t, d), MXU_DTYPE),
        compiler_params=_params("parallel"),
    )(p, p, conv_w)


def _sc_mid_bwd(p, dq, conv_w, name):
    t, d3 = p.shape
    d = d3 // 3
    tm = _row_tile(t, ROW_TILE_MID)
    nt = t // tm
    cw = min(STRIP, d)
    kw = conv_w.shape[0]

    def body(p_ref, phb_ref, pha_ref, dq_ref, dqa_ref, w_ref, dp_ref, dw_ref):
        i = pl.program_id(0)
        first, last = i == 0, i == nt - 1

        @pl.when(first)
        def _():
            dw_ref[...] = jnp.zeros_like(dw_ref)

        for c in range(0, d, cw):
            sl = slice(c, c + cw)
            w = _rows(w_ref, sl)
            bg = p_ref[:, c:c + cw]
            cg = p_ref[:, d + c:d + c + cw]
            v = p_ref[:, 2 * d + c:2 * d + c + cw]
            cv = cg * v
            hcv = jnp.where(first, 0.0, phb_ref[:, d + c:d + c + cw] * phb_ref[:, 2 * d + c:2 * d + c + cw])
            shifted = [_shift_down(cv, hcv, kw - 1 - k) for k in range(kw)]
            u = w[0] * shifted[0]
            for k in range(1, kw):
                u = u + w[k] * shifted[k]
            dq = dq_ref[:, sl]
            du = dq * bg
            du_after = jnp.where(last, 0.0, dqa_ref[:, sl] * pha_ref[:, c:c + cw])
            dcv = _conv_t(du, du_after, w)
            dp_ref[:, c:c + cw] = (dq * u).astype(dp_ref.dtype)
            dp_ref[:, d + c:d + c + cw] = (dcv * v).astype(dp_ref.dtype)
            dp_ref[:, 2 * d + c:2 * d + c + cw] = (dcv * cg).astype(dp_ref.dtype)
            for k in range(kw):
                dw_ref[k:k + 1, sl] += jnp.sum(du * shifted[k], axis=0, keepdims=True)

    return pl.pallas_call(
        body, name=name, grid=(nt,),
        in_specs=[pl.BlockSpec((tm, d3), lambda i: (i, 0)),
                  _halo_before_spec(tm, d3),
                  _halo_after_spec(tm, d3, t),
                  pl.BlockSpec((tm, d), lambda i: (i, 0)),
                  _halo_after_spec(tm, d, t),
                  pl.BlockSpec(conv_w.shape, lambda i: (0, 0))],
        out_specs=[pl.BlockSpec((tm, d3), lambda i: (i, 0)),
                   pl.BlockSpec(conv_w.shape, lambda i: (0, 0))],
        out_shape=[jax.ShapeDtypeStruct((t, d3), MXU_DTYPE), jax.ShapeDtypeStruct(conv_w.shape, F32)],
        compiler_params=_params("arbitrary"),
    )(p, p, p, dq, dq, conv_w)


def _ffn_mid_fwd(a, conv_w, name):
    t, f2 = a.shape
    f = f2 // 2
    tm = _row_tile(t, ROW_TILE_MID)
    cw = min(STRIP, f)

    def body(a_ref, ah_ref, w_ref, z_ref):
        first = pl.program_id(0) == 0
        for c in range(0, f, cw):
            gsl, vsl = slice(c, c + cw), slice(f + c, f + c + cw)
            gg = _conv(a_ref[:, gsl], jnp.where(first, 0.0, ah_ref[:, gsl]), _rows(w_ref, gsl))
            vv = _conv(a_ref[:, vsl], jnp.where(first, 0.0, ah_ref[:, vsl]), _rows(w_ref, vsl))
            z_ref[:, gsl] = (gg * _sigmoid(gg) * vv).astype(z_ref.dtype)

    return pl.pallas_call(
        body, name=name, grid=(t // tm,),
        in_specs=[pl.BlockSpec((tm, f2), lambda i: (i, 0)),
                  _halo_before_spec(tm, f2),
                  pl.BlockSpec(conv_w.shape, lambda i: (0, 0))],
        out_specs=pl.BlockSpec((tm, f), lambda i: (i, 0)),
        out_shape=jax.ShapeDtypeStruct((t, f), MXU_DTYPE),
        compiler_params=_params("parallel"),
    )(a, a, conv_w)


def _ffn_mid_bwd(a, dz, conv_w, name):
    t, f2 = a.shape
    f = f2 // 2
    tm = _row_tile(t, ROW_TILE_MID)
    nt = t // tm
    cw = min(STRIP, f)
    kw = conv_w.shape[0]

    def body(a_ref, ahb_ref, aha_ref, dz_ref, dza_ref, w_ref, da_ref, dw_ref):
        i = pl.program_id(0)
        first, last = i == 0, i == nt - 1

        @pl.when(first)
        def _():
            dw_ref[...] = jnp.zeros_like(dw_ref)

        for c in range(0, f, cw):
            gsl, vsl = slice(c, c + cw), slice(f + c, f + c + cw)
            wg, wv = _rows(w_ref, gsl), _rows(w_ref, vsl)
            ag, av = a_ref[:, gsl], a_ref[:, vsl]
            hg = jnp.where(first, 0.0, ahb_ref[:, gsl])
            hv = jnp.where(first, 0.0, ahb_ref[:, vsl])
            sg = [_shift_down(ag, hg, kw - 1 - k) for k in range(kw)]
            sv = [_shift_down(av, hv, kw - 1 - k) for k in range(kw)]
            gg = wg[0] * sg[0]
            vv = wv[0] * sv[0]
            for k in range(1, kw):
                gg = gg + wg[k] * sg[k]
                vv = vv + wv[k] * sv[k]
            gg_a = _conv(aha_ref[:, gsl], ag[tm - SUBLANES:tm], wg)
            vv_a = _conv(aha_ref[:, vsl], av[tm - SUBLANES:tm], wv)

            def grads(dz, gg, vv):
                s = _sigmoid(gg)
                return dz * vv * s * (1.0 + gg * (1.0 - s)), dz * gg * s

            dgg, dvv = grads(dz_ref[:, gsl], gg, vv)
            dgg_a, dvv_a = grads(dza_ref[:, gsl], gg_a, vv_a)
            dgg_a = jnp.where(last, 0.0, dgg_a)
            dvv_a = jnp.where(last, 0.0, dvv_a)
            da_ref[:, gsl] = _conv_t(dgg, dgg_a, wg).astype(da_ref.dtype)
            da_ref[:, vsl] = _conv_t(dvv, dvv_a, wv).astype(da_ref.dtype)
            for k in range(kw):
                dw_ref[k:k + 1, gsl] += jnp.sum(dgg * sg[k], axis=0, keepdims=True)
                dw_ref[k:k + 1, vsl] += jnp.sum(dvv * sv[k], axis=0, keepdims=True)

    return pl.pallas_call(
        body, name=name, grid=(nt,),
        in_specs=[pl.BlockSpec((tm, f2), lambda i: (i, 0)),
                  _halo_before_spec(tm, f2),
                  _halo_after_spec(tm, f2, t),
                  pl.BlockSpec((tm, f), lambda i: (i, 0)),
                  _halo_after_spec(tm, f, t),
                  pl.BlockSpec(conv_w.shape, lambda i: (0, 0))],
        out_specs=[pl.BlockSpec((tm, f2), lambda i: (i, 0)),
                   pl.BlockSpec(conv_w.shape, lambda i: (0, 0))],
        out_shape=[jax.ShapeDtypeStruct((t, f2), MXU_DTYPE), jax.ShapeDtypeStruct(conv_w.shape, F32)],
        compiler_params=_params("arbitrary"),
    )(a, a, a, dz, dz, conv_w)


def _rg_gates(u, k, wa_ref, ba, wx_ref, bx, lam):
    ub = u.astype(MXU_DTYPE)
    r = _sigmoid(_dot(ub, wa_ref[k], _NN) + ba)
    ig = _sigmoid(_dot(ub, wx_ref[k], _NN) + bx)
    sp = _softplus(-lam)
    la = -RG_C * r * sp
    a = jnp.exp(la)
    th = jnp.tanh(la)
    mult = jnp.sqrt(-2.0 * th / (1.0 - th))
    return ub, r, ig, sp, a, mult


def _rg_mid_fwd(p, conv_w, conv_b, wa, ba, wx, bx, lam, name):
    t, r2 = p.shape
    rr = r2 // 2
    nb, bd, _ = wa.shape
    tm = _row_tile(t, ROW_TILE_MID)

    def body(p_ref, ph_ref, cw_ref, cb_ref, wa_ref, ba_ref, wx_ref, bx_ref, lam_ref,
             y_ref, hs_ref, a_scr, b_scr, carry_ref):
        first = pl.program_id(0) == 0

        @pl.when(first)
        def _():
            carry_ref[...] = jnp.zeros_like(carry_ref)

        for k in range(nb):
            sl = slice(k * bd, (k + 1) * bd)
            rsl = slice(rr + k * bd, rr + (k + 1) * bd)
            u = _conv(p_ref[:, rsl], jnp.where(first, 0.0, ph_ref[:, rsl]), _rows(cw_ref, sl)) + cb_ref[:, sl]
            _, _, ig, _, a, mult = _rg_gates(u, k, wa_ref, ba_ref[:, sl], wx_ref, bx_ref[:, sl], lam_ref[:, sl])
            a_scr[:, sl] = a
            b_scr[:, sl] = mult * (ig * u)

        row = lax.broadcasted_iota(jnp.int32, (SUBLANES, rr), 0)

        def group(gi, carry):
            r0 = pl.multiple_of(gi * SUBLANES, SUBLANES)
            a = a_scr[pl.ds(r0, SUBLANES), :]
            b = b_scr[pl.ds(r0, SUBLANES), :]
            for s in (1, 2, 4):
                a_s = jnp.where(row >= s, pltpu.roll(a, s, axis=0), 1.0)
                b_s = jnp.where(row >= s, pltpu.roll(b, s, axis=0), 0.0)
                b = a * b_s + b
                a = a * a_s
            h = a * carry + b
            hs_ref[pl.ds(r0, SUBLANES), :] = h
            return jnp.broadcast_to(h[SUBLANES - 1:SUBLANES, :], (SUBLANES, rr))

        carry_ref[...] = lax.fori_loop(0, tm // SUBLANES, group, carry_ref[...])

        for k in range(nb):
            sl = slice(k * bd, (k + 1) * bd)
            y_ref[:, sl] = (hs_ref[:, sl] * _gelu(p_ref[:, sl])).astype(y_ref.dtype)

    full = lambda s: pl.BlockSpec(s, lambda i: (0,) * len(s))
    return pl.pallas_call(
        body, name=name, grid=(t // tm,),
        in_specs=[pl.BlockSpec((tm, r2), lambda i: (i, 0)),
                  _halo_before_spec(tm, r2),
                  full(conv_w.shape), full(conv_b.shape), full(wa.shape), full(ba.shape),
                  full(wx.shape), full(bx.shape), full(lam.shape)],
        out_specs=[pl.BlockSpec((tm, rr), lambda i: (i, 0)),
                   pl.BlockSpec((tm, rr), lambda i: (i, 0))],
        out_shape=[jax.ShapeDtypeStruct((t, rr), MXU_DTYPE), jax.ShapeDtypeStruct((t, rr), F32)],
        scratch_shapes=[pltpu.VMEM((tm, rr), F32), pltpu.VMEM((tm, rr), F32), pltpu.VMEM((SUBLANES, rr), F32)],
        compiler_params=_params("arbitrary"),
    )(p, p, conv_w, conv_b, wa, ba, wx, bx, lam)


def _rg_mid_bwd(p, dy, hs, conv_w, conv_b, wa, ba, wx, bx, lam, name):
    t, r2 = p.shape
    rr = r2 // 2
    nb, bd, _ = wa.shape
    tm = _row_tile(t, ROW_TILE_MID)
    nt = t // tm
    kw = conv_w.shape[0]
    blocks_per_tile = tm // SUBLANES

    def body(p_ref, ph_ref, dy_ref, hs_ref, hsh_ref, cw_ref, cb_ref, wa_ref, ba_ref, wx_ref, bx_ref, lam_ref,
             dp_ref, dcw_ref, dcb_ref, dwa_ref, dba_ref, dwx_ref, dbx_ref, dlam_ref,
             a_scr, as_scr, u_scr, r_scr, ig_scr, mult_scr, g_scr, carry_g, carry_du):
        i = pl.program_id(0)
        newest = i == 0
        oldest = i == nt - 1

        @pl.when(newest)
        def _():
            carry_g[...] = jnp.zeros_like(carry_g)
            carry_du[...] = jnp.zeros_like(carry_du)
            for ref in (dcw_ref, dcb_ref, dwa_ref, dba_ref, dwx_ref, dbx_ref, dlam_ref):
                ref[...] = jnp.zeros_like(ref)

        ones = jnp.ones((SUBLANES, bd), F32)
        for k in range(nb):
            sl = slice(k * bd, (k + 1) * bd)
            rsl = slice(rr + k * bd, rr + (k + 1) * bd)
            u = _conv(p_ref[:, rsl], jnp.where(oldest, 0.0, ph_ref[:, rsl]), _rows(cw_ref, sl)) + cb_ref[:, sl]
            _, r, ig, _, a, mult = _rg_gates(u, k, wa_ref, ba_ref[:, sl], wx_ref, bx_ref[:, sl], lam_ref[:, sl])
            gb = p_ref[:, sl]
            dy = dy_ref[:, sl]
            g_scr[:, sl] = dy * _gelu(gb)
            dp_ref[:, sl] = (dy * hs_ref[:, sl] * _gelu_grad(gb)).astype(dp_ref.dtype)
            a_scr[:, sl] = a
            as_scr[:, sl] = _shift_up(a, ones, 1)
            u_scr[:, sl] = u
            r_scr[:, sl] = r
            ig_scr[:, sl] = ig
            mult_scr[:, sl] = mult

        row = lax.broadcasted_iota(jnp.int32, (SUBLANES, rr), 0)

        def group(j, carry):
            r0 = pl.multiple_of((blocks_per_tile - 1 - j) * SUBLANES, SUBLANES)
            a = as_scr[pl.ds(r0, SUBLANES), :]
            d = g_scr[pl.ds(r0, SUBLANES), :]
            for s in (1, 2, 4):
                a_s = jnp.where(row < SUBLANES - s, pltpu.roll(a, SUBLANES - s, axis=0), 1.0)
                d_s = jnp.where(row < SUBLANES - s, pltpu.roll(d, SUBLANES - s, axis=0), 0.0)
                d = a * d_s + d
                a = a * a_s
            g = a * carry + d
            g_scr[pl.ds(r0, SUBLANES), :] = g
            return jnp.broadcast_to(g[0:1, :], (SUBLANES, rr))

        g_first = lax.fori_loop(0, blocks_per_tile, group, carry_g[...])
        carry_g[...] = jnp.broadcast_to(a_scr[0:1, :], (SUBLANES, rr)) * g_first

        for k in range(nb):
            sl = slice(k * bd, (k + 1) * bd)
            rsl = slice(rr + k * bd, rr + (k + 1) * bd)
            cw = _rows(cw_ref, sl)
            lam_k = lam_ref[:, sl]
            sp = _softplus(-lam_k)
            g = g_scr[:, sl]
            a, u, r, ig, mult = a_scr[:, sl], u_scr[:, sl], r_scr[:, sl], ig_scr[:, sl], mult_scr[:, sl]
            h_prev = _shift_down(hs_ref[:, sl], jnp.where(oldest, 0.0, hsh_ref[:, sl]), 1)
            da = g * h_prev
            dmult = g * (ig * u)
            d_iu = g * mult
            dla = da * a - dmult * (a * a) / mult
            dr = dla * (-RG_C * sp)
            dsp = jnp.sum(dla * (-RG_C * r), axis=0, keepdims=True)
            dlam_ref[:, sl] += dsp * (-_sigmoid(-lam_k))
            dza = dr * r * (1.0 - r)
            dzx = (d_iu * u) * ig * (1.0 - ig)
            dba_ref[:, sl] += jnp.sum(dza, axis=0, keepdims=True)
            dbx_ref[:, sl] += jnp.sum(dzx, axis=0, keepdims=True)
            ub = u.astype(MXU_DTYPE)
            dzab, dzxb = dza.astype(MXU_DTYPE), dzx.astype(MXU_DTYPE)
            dwa_ref[k] += _dot(ub, dzab, _TN)
            dwx_ref[k] += _dot(ub, dzxb, _TN)
            du = d_iu * ig + _dot(dzab, wa_ref[k], _NT) + _dot(dzxb, wx_ref[k], _NT)
            dcb_ref[:, sl] += jnp.sum(du, axis=0, keepdims=True)
            rb = p_ref[:, rsl]
            hrb = jnp.where(oldest, 0.0, ph_ref[:, rsl])
            for kk in range(kw):
                dcw_ref[kk:kk + 1, sl] += jnp.sum(du * _shift_down(rb, hrb, kw - 1 - kk), axis=0, keepdims=True)
            dp_ref[:, rsl] = _conv_t(du, carry_du[:, sl], cw).astype(dp_ref.dtype)
            carry_du[:, sl] = du[0:SUBLANES]

    rev = lambda i: nt - 1 - i
    full = lambda s: pl.BlockSpec(s, lambda i: (0,) * len(s))
    halo = lambda cols: pl.BlockSpec(
        (SUBLANES, cols), lambda i: (jnp.maximum(rev(i) * (tm // SUBLANES) - 1, 0), 0))
    tile = lambda cols: pl.BlockSpec((tm, cols), lambda i: (rev(i), 0))
    vm = lambda: pltpu.VMEM((tm, rr), F32)
    return pl.pallas_call(
        body, name=name, grid=(nt,),
        in_specs=[tile(r2), halo(r2), tile(rr), tile(rr), halo(rr),
                  full(conv_w.shape), full(conv_b.shape), full(wa.shape), full(ba.shape),
                  full(wx.shape), full(bx.shape), full(lam.shape)],
        out_specs=[tile(r2), full(conv_w.shape), full(conv_b.shape), full(wa.shape), full(ba.shape),
                   full(wx.shape), full(bx.shape), full(lam.shape)],
        out_shape=[jax.ShapeDtypeStruct((t, r2), MXU_DTYPE),
                   jax.ShapeDtypeStruct(conv_w.shape, F32), jax.ShapeDtypeStruct(conv_b.shape, F32),
                   jax.ShapeDtypeStruct(wa.shape, F32), jax.ShapeDtypeStruct(ba.shape, F32),
                   jax.ShapeDtypeStruct(wx.shape, F32), jax.ShapeDtypeStruct(bx.shape, F32),
                   jax.ShapeDtypeStruct(lam.shape, F32)],
        scratch_shapes=[vm(), vm(), vm(), vm(), vm(), vm(), vm(),
                        pltpu.VMEM((SUBLANES, rr), F32), pltpu.VMEM((SUBLANES, rr), F32)],
        compiler_params=_params("arbitrary"),
    )(p, p, dy, hs, hs, conv_w, conv_b, wa, ba, wx, bx, lam)


def _loss_head(h, g, target, n_meta, name):
    t, d = h.shape
    tm = _row_tile(t, ROW_TILE_MATMUL)

    def body(h_ref, g_ref, t_ref, dh_ref, loss_ref, dg_ref):
        i = pl.program_id(0)
        x = h_ref[...]
        ms = jnp.mean(x * x, axis=-1, keepdims=True)
        r = lax.rsqrt(ms + RMS_EPS)
        xhat = x * r
        gg = g_ref[...]
        row = i * tm + lax.broadcasted_iota(jnp.int32, (tm, 1), 0)
        err = jnp.where(row >= n_meta, xhat * gg - t_ref[...], 0.0)
        dout = err * (1.0 / d)
        dng = dout * gg
        c = jnp.mean(dng * xhat, axis=-1, keepdims=True)
        dh_ref[...] = r * (dng - xhat * c)
        part_dg = jnp.sum(dout * xhat, axis=0, keepdims=True)
        part_loss = jnp.broadcast_to(0.5 * jnp.sum(err * dout, keepdims=True), loss_ref.shape)

        @pl.when(i == 0)
        def _():
            dg_ref[...] = part_dg
            loss_ref[...] = part_loss

        @pl.when(i > 0)
        def _():
            dg_ref[...] += part_dg
            loss_ref[...] += part_loss

    return pl.pallas_call(
        body, name=name, grid=(t // tm,),
        in_specs=[pl.BlockSpec((tm, d), lambda i: (i, 0)),
                  pl.BlockSpec((1, d), lambda i: (0, 0)),
                  pl.BlockSpec((tm, d), lambda i: (i, 0))],
        out_specs=[pl.BlockSpec((tm, d), lambda i: (i, 0)),
                   pl.BlockSpec((SUBLANES, LANES), lambda i: (0, 0)),
                   pl.BlockSpec((1, d), lambda i: (0, 0))],
        out_shape=[jax.ShapeDtypeStruct((t, d), F32), jax.ShapeDtypeStruct((SUBLANES, LANES), F32),
                   jax.ShapeDtypeStruct((1, d), F32)],
        compiler_params=_params("arbitrary"),
    )(h, g, target)


def _adamw(w, g, m, v, name):
    rows, cols = w.shape
    tr = rows
    if rows > 512:
        for cand in range(8, 513, 8):
            if rows % cand == 0:
                tr = cand

    def body(w_ref, g_ref, m_ref, v_ref, d_ref, nm_ref, nv_ref):
        g_ = g_ref[...]
        m_ = ADAM_B1 * m_ref[...] + (1.0 - ADAM_B1) * g_
        v_ = ADAM_B2 * v_ref[...] + (1.0 - ADAM_B2) * (g_ * g_)
        m_hat = m_ / (1.0 - ADAM_B1 ** ADAM_STEP)
        v_hat = v_ / (1.0 - ADAM_B2 ** ADAM_STEP)
        d_ref[...] = -ADAM_LR * (m_hat / (jnp.sqrt(v_hat) + ADAM_EPS) + ADAM_WD * w_ref[...])
        nm_ref[...] = m_
        nv_ref[...] = v_

    spec = pl.BlockSpec((tr, cols), lambda i: (i, 0))
    shape = jax.ShapeDtypeStruct((rows, cols), F32)
    return pl.pallas_call(
        body, name=name, grid=(rows // tr,),
        in_specs=[spec] * 4, out_specs=[spec] * 3, out_shape=[shape] * 3,
        compiler_params=_params("parallel"),
    )(w, g, m, v)


def _adamw_nd(w, g, m, v, name):
    shape = w.shape
    two_d = (-1, shape[-1]) if w.ndim > 1 else (1, -1)
    outs = _adamw(w.reshape(two_d), g.reshape(two_d), m.reshape(two_d), v.reshape(two_d), name)
    return tuple(o.reshape(shape) for o in outs)


_ANY = pl.BlockSpec(memory_space=pl.ANY)
_MESH = pl.DeviceIdType.MESH


def _all_gather(blk, name):
    rows, cols = blk.shape

    def body(x_ref, out_ref, send_sems, recv_sems, local_sem):
        x, y, c = lax.axis_index("x"), lax.axis_index("y"), lax.axis_index("c")
        me, sibling = (x, y, c), (x, y, 1 - c)
        chips = [(1 - x, y), (x, 1 - y), (1 - x, 1 - y)]

        def slot(px, py, pc):
            return out_ref.at[4 * px + 2 * py + pc]

        def copy(k, block, to, src=None):
            return pltpu.make_async_remote_copy(
                src_ref=slot(*block) if src is None else src, dst_ref=slot(*block),
                send_sem=send_sems.at[k], recv_sem=recv_sems.at[k], device_id=to, device_id_type=_MESH)

        mine = pltpu.make_async_copy(x_ref, slot(*me), local_sem)
        mine.start()
        first = [copy(0, me, sibling, src=x_ref)]
        first += [copy(1 + j, me, (*chip, c), src=x_ref) for j, chip in enumerate(chips)]
        for cp in first:
            cp.start()
        passed = [copy(4 + j, (*chip, c), sibling) for j, chip in enumerate(chips)]
        for j, chip in enumerate(chips):
            copy(1 + j, (*chip, c), me).wait_recv()
            passed[j].start()
        copy(0, sibling, me).wait_recv()
        for j, chip in enumerate(chips):
            copy(4 + j, (*chip, 1 - c), me).wait_recv()
        for cp in first + passed:
            cp.wait_send()
        mine.wait()

    return pl.pallas_call(
        body, name=name,
        in_specs=[_ANY], out_specs=_ANY,
        out_shape=jax.ShapeDtypeStruct((N_DEV, rows, cols), blk.dtype),
        scratch_shapes=[pltpu.SemaphoreType.DMA((7,)), pltpu.SemaphoreType.DMA((7,)), pltpu.SemaphoreType.DMA(())],
    )(blk)


def _rs_sibling_exchange(buf, name):
    _, rows, cols = buf.shape

    def body(buf_ref, out_ref, send_sems, recv_sems):
        x, y, c = lax.axis_index("x"), lax.axis_index("y"), lax.axis_index("c")
        copies = [pltpu.make_async_remote_copy(
            src_ref=buf_ref.at[2 * k + (1 - c)], dst_ref=out_ref.at[k],
            send_sem=send_sems.at[k], recv_sem=recv_sems.at[k],
            device_id=(x, y, 1 - c), device_id_type=_MESH) for k in range(4)]
        for cp in copies:
            cp.start()
        for cp in copies:
            cp.wait()

    return pl.pallas_call(
        body, name=name,
        in_specs=[_ANY], out_specs=_ANY,
        out_shape=jax.ShapeDtypeStruct((4, rows, cols), buf.dtype),
        scratch_shapes=[pltpu.SemaphoreType.DMA((4,)), pltpu.SemaphoreType.DMA((4,))],
    )(buf)


def _rs_chip_sum(buf, recv, core, name):
    _, rows, cols = buf.shape
    tr = _row_tile(rows, 512) if rows % 16 == 0 else rows

    def body(c_ref, a_ref, b_ref, o_ref):
        o_ref[...] = a_ref[...] + b_ref[...]

    grid_spec = pltpu.PrefetchScalarGridSpec(
        num_scalar_prefetch=1, grid=(4, rows // tr),
        in_specs=[pl.BlockSpec((None, tr, cols), lambda k, j, c: (2 * k + c[0], j, 0)),
                  pl.BlockSpec((None, tr, cols), lambda k, j, c: (k, j, 0))],
        out_specs=pl.BlockSpec((None, tr, cols), lambda k, j, c: (k, j, 0)))
    return pl.pallas_call(
        body, name=name, grid_spec=grid_spec,
        out_shape=jax.ShapeDtypeStruct((4, rows, cols), buf.dtype),
        compiler_params=_params("parallel", "parallel"),
    )(core, buf, recv)


def _rs_chip_exchange(chip_sum, name):
    _, rows, cols = chip_sum.shape

    def body(cs_ref, out_ref, send_sems, recv_sems, local_sem):
        x, y, c = lax.axis_index("x"), lax.axis_index("y"), lax.axis_index("c")
        my_chip = 2 * x + y
        mine = pltpu.make_async_copy(cs_ref.at[my_chip], out_ref.at[my_chip], local_sem)
        mine.start()
        chips = [(1 - x, y), (x, 1 - y), (1 - x, 1 - y)]
        copies = [pltpu.make_async_remote_copy(
            src_ref=cs_ref.at[2 * px + py], dst_ref=out_ref.at[my_chip],
            send_sem=send_sems.at[j], recv_sem=recv_sems.at[j],
            device_id=(px, py, c), device_id_type=_MESH) for j, (px, py) in enumerate(chips)]
        for cp in copies:
            cp.start()
        for cp in copies:
            cp.wait()
        mine.wait()

    return pl.pallas_call(
        body, name=name,
        in_specs=[_ANY], out_specs=_ANY,
        out_shape=jax.ShapeDtypeStruct((4, rows, cols), chip_sum.dtype),
        scratch_shapes=[pltpu.SemaphoreType.DMA((3,)), pltpu.SemaphoreType.DMA((3,)), pltpu.SemaphoreType.DMA(())],
    )(chip_sum)


def _sum_slots(parts, name):
    _, rows, cols = parts.shape
    tr = _row_tile(rows, 512) if rows % 16 == 0 else rows

    def body(p_ref, o_ref):
        o_ref[...] = ((p_ref[0] + p_ref[1]) + p_ref[2]) + p_ref[3]

    return pl.pallas_call(
        body, name=name, grid=(rows // tr,),
        in_specs=[pl.BlockSpec((4, tr, cols), lambda i: (0, i, 0))],
        out_specs=pl.BlockSpec((tr, cols), lambda i: (i, 0)),
        out_shape=jax.ShapeDtypeStruct((rows, cols), parts.dtype),
        compiler_params=_params("parallel"),
    )(parts)


def _reduce_scatter(buf, core):
    recv = _rs_sibling_exchange(buf, "rs_sibling_exchange")
    chip_sum = _rs_chip_sum(buf, recv, core, "rs_chip_sum")
    parts = _rs_chip_exchange(chip_sum, "rs_chip_exchange")
    return _sum_slots(parts, "rs_sum_slots")


def _pad_rows(flat, cols, multiple):
    n = flat.shape[0]
    rows = -(-n // cols)
    rows = -(-rows // multiple) * multiple
    return jnp.pad(flat, (0, rows * cols - n)).reshape(rows, cols)


def _cols_to_chunks(full):
    lead = full.shape[:-1]
    c = full.shape[-1] // N_DEV
    x = full.reshape(-1, N_DEV, c)
    return jnp.transpose(x, (1, 0, 2)).reshape(N_DEV, -1)


def _chunks_to_cols(chunks, lead):
    n = 1
    for s in lead:
        n *= s
    c = chunks.shape[1] // n
    x = chunks.reshape(N_DEV, n, c)
    return jnp.transpose(x, (1, 0, 2)).reshape(tuple(lead) + (N_DEV * c,))


def kernel(x, meta_tokens, norm_mix_g, norm_ffn_g, final_norm_g, sc_w_in, sc_conv_w, sc_w_out, rg_w_in, rg_conv_w, rg_conv_b, rg_w_gate_a, rg_b_gate_a, rg_w_gate_x, rg_b_gate_x, rg_lambda, rg_w_out, ffn_w_up, ffn_conv_w, ffn_w_down, loss_target, m_meta_tokens, m_norm_mix_g, m_norm_ffn_g, m_final_norm_g, m_sc_w_in, m_sc_conv_w, m_sc_w_out, m_rg_w_in, m_rg_conv_w, m_rg_conv_b, m_rg_w_gate_a, m_rg_b_gate_a, m_rg_w_gate_x, m_rg_b_gate_x, m_rg_lambda, m_rg_w_out, m_ffn_w_up, m_ffn_conv_w, m_ffn_w_down, v_meta_tokens, v_norm_mix_g, v_norm_ffn_g, v_final_norm_g, v_sc_w_in, v_sc_conv_w, v_sc_w_out, v_rg_w_in, v_rg_conv_w, v_rg_conv_b, v_rg_w_gate_a, v_rg_b_gate_a, v_rg_w_gate_x, v_rg_b_gate_x, v_rg_lambda, v_rg_w_out, v_ffn_w_up, v_ffn_conv_w, v_ffn_w_down):
    weights = dict(meta_tokens=meta_tokens, norm_mix_g=norm_mix_g, norm_ffn_g=norm_ffn_g, final_norm_g=final_norm_g,
                   sc_w_in=sc_w_in, sc_conv_w=sc_conv_w, sc_w_out=sc_w_out, rg_w_in=rg_w_in, rg_conv_w=rg_conv_w,
                   rg_conv_b=rg_conv_b, rg_w_gate_a=rg_w_gate_a, rg_b_gate_a=rg_b_gate_a, rg_w_gate_x=rg_w_gate_x,
                   rg_b_gate_x=rg_b_gate_x, rg_lambda=rg_lambda, rg_w_out=rg_w_out, ffn_w_up=ffn_w_up,
                   ffn_conv_w=ffn_conv_w, ffn_w_down=ffn_w_down)
    m_in = dict(meta_tokens=m_meta_tokens, norm_mix_g=m_norm_mix_g, norm_ffn_g=m_norm_ffn_g, final_norm_g=m_final_norm_g,
                sc_w_in=m_sc_w_in, sc_conv_w=m_sc_conv_w, sc_w_out=m_sc_w_out, rg_w_in=m_rg_w_in, rg_conv_w=m_rg_conv_w,
                rg_conv_b=m_rg_conv_b, rg_w_gate_a=m_rg_w_gate_a, rg_b_gate_a=m_rg_b_gate_a, rg_w_gate_x=m_rg_w_gate_x,
                rg_b_gate_x=m_rg_b_gate_x, rg_lambda=m_rg_lambda, rg_w_out=m_rg_w_out, ffn_w_up=m_ffn_w_up,
                ffn_conv_w=m_ffn_conv_w, ffn_w_down=m_ffn_w_down)
    v_in = dict(meta_tokens=v_meta_tokens, norm_mix_g=v_norm_mix_g, norm_ffn_g=v_norm_ffn_g, final_norm_g=v_final_norm_g,
                sc_w_in=v_sc_w_in, sc_conv_w=v_sc_conv_w, sc_w_out=v_sc_w_out, rg_w_in=v_rg_w_in, rg_conv_w=v_rg_conv_w,
                rg_conv_b=v_rg_conv_b, rg_w_gate_a=v_rg_w_gate_a, rg_b_gate_a=v_rg_b_gate_a, rg_w_gate_x=v_rg_w_gate_x,
                rg_b_gate_x=v_rg_b_gate_x, rg_lambda=v_rg_lambda, rg_w_out=v_rg_w_out, ffn_w_up=v_ffn_w_up,
                ffn_conv_w=v_ffn_conv_w, ffn_w_down=v_ffn_w_down)
    names = list(weights)

    seq, d = x.shape[1], x.shape[2]
    n_meta = meta_tokens.shape[0]
    n_ffn = ffn_w_up.shape[0]
    rr = rg_w_gate_a.shape[1] * rg_w_gate_a.shape[2]
    core = lax.axis_index("c").astype(jnp.int32).reshape(1)

    big_parts = [sc_w_in[0].T, sc_w_out[0], rg_w_in[0].T, rg_w_out[0]]
    big_parts += [ffn_w_up[l].T for l in range(n_ffn)] + [ffn_w_down[l] for l in range(n_ffn)]
    big_rows = [p.shape[0] for p in big_parts]
    big = _all_gather(jnp.concatenate(big_parts, axis=0).astype(MXU_DTYPE), "gather_matrices")
    offs = [0]
    for r_ in big_rows:
        offs.append(offs[-1] + r_)
    full_big = [big[:, offs[k]:offs[k + 1], :].reshape(N_DEV * big_rows[k], d) for k in range(len(big_parts))]
    sc_w_in_t, sc_w_out_f, rg_w_in_t, rg_w_out_f = full_big[:4]
    ffn_w_up_t = full_big[4:4 + n_ffn]
    ffn_w_down_f = full_big[4 + n_ffn:4 + 2 * n_ffn]

    small_names = ["meta_tokens", "sc_conv_w", "rg_conv_w", "rg_conv_b", "rg_b_gate_a", "rg_b_gate_x", "rg_lambda",
                   "ffn_conv_w"]
    small_lead = {n: weights[n].shape[:-1] for n in small_names}
    small_sizes = [weights[n].size for n in small_names]
    small_flat = jnp.concatenate([weights[n].reshape(-1) for n in small_names])
    small_rows = _pad_rows(small_flat, d, SUBLANES)
    small_g = _all_gather(small_rows, "gather_small").reshape(N_DEV, -1)
    small_full = {}
    o = 0
    for n, sz in zip(small_names, small_sizes):
        small_full[n] = _chunks_to_cols(small_g[:, o:o + sz], small_lead[n])
        o += sz

    wa = rg_w_gate_a[0].astype(MXU_DTYPE)
    wx = rg_w_gate_x[0].astype(MXU_DTYPE)
    rg_cw, rg_cb = small_full["rg_conv_w"][0], small_full["rg_conv_b"]
    rg_ba, rg_bx, rg_lam = small_full["rg_b_gate_a"], small_full["rg_b_gate_x"], small_full["rg_lambda"]
    sc_cw = small_full["sc_conv_w"][0]
    ffn_cw = small_full["ffn_conv_w"]

    h0 = jnp.concatenate([small_full["meta_tokens"], x[0]], axis=0)
    target = jnp.concatenate([jnp.zeros((n_meta, d), F32), loss_target[0]], axis=0)

    def ffn_fwd(h, l):
        n, a = _norm_matmul(h, norm_ffn_g[l:l + 1], ffn_w_up_t[l], f"ffn{l}_up")
        z = _ffn_mid_fwd(a, ffn_cw[l], f"ffn{l}_mid")
        return _matmul_residual(z, ffn_w_down_f[l], h, f"ffn{l}_down"), (n, a, z)

    n0, p0 = _norm_matmul(h0, norm_mix_g[0:1], sc_w_in_t, "sc_in")
    q0 = _sc_mid_fwd(p0, sc_cw, "sc_mid")
    h1 = _matmul_residual(q0, sc_w_out_f, h0, "sc_out")
    h2, ffn0_saved = ffn_fwd(h1, 0)
    n2, p2 = _norm_matmul(h2, norm_mix_g[1:2], rg_w_in_t, "rg_in")
    y2, hs2 = _rg_mid_fwd(p2, rg_cw, rg_cb, wa, rg_ba, wx, rg_bx, rg_lam, "rg_mid")
    h3 = _matmul_residual(y2, rg_w_out_f, h2, "rg_out")
    h4, ffn1_saved = ffn_fwd(h3, 1)

    dh4, loss_tile, d_final_g = _loss_head(h4, final_norm_g.reshape(1, d), target, n_meta, "loss_head")
    loss = lax.psum(loss_tile[0, 0], AXES)

    def ffn_bwd(dh_out, h_in, saved, l):
        n, a, z = saved
        dz, dhb = _dgrad_out(dh_out, ffn_w_down_f[l], f"ffn{l}_down_dgrad")
        d_w_down = _wgrad(z, dhb, f"ffn{l}_down_wgrad")
        da, d_cw = _ffn_mid_bwd(a, dz, ffn_cw[l], f"ffn{l}_mid_bwd")
        d_w_up_t = _wgrad(da, n, f"ffn{l}_up_wgrad")
        dh_in, d_g = _dgrad_in_norm(da, ffn_w_up_t[l], h_in, norm_ffn_g[l:l + 1], dh_out, f"ffn{l}_up_dgrad")
        return dh_in, d_w_up_t, d_cw, d_w_down, d_g

    dh3, d_up1, d_fcw1, d_down1, d_fg1 = ffn_bwd(dh4, h3, ffn1_saved, 1)

    dy2, dhb3 = _dgrad_out(dh3, rg_w_out_f, "rg_out_dgrad")
    d_rg_w_out = _wgrad(y2, dhb3, "rg_out_wgrad")
    dp2, d_rg_cw, d_rg_cb, d_wa, d_rg_ba, d_wx, d_rg_bx, d_rg_lam = _rg_mid_bwd(
        p2, dy2, hs2, rg_cw, rg_cb, wa, rg_ba, wx, rg_bx, rg_lam, "rg_mid_bwd")
    d_rg_w_in_t = _wgrad(dp2, n2, "rg_in_wgrad")
    dh2, d_mg1 = _dgrad_in_norm(dp2, rg_w_in_t, h2, norm_mix_g[1:2], dh3, "rg_in_dgrad")

    dh1, d_up0, d_fcw0, d_down0, d_fg0 = ffn_bwd(dh2, h1, ffn0_saved, 0)

    dq0, dhb1 = _dgrad_out(dh1, sc_w_out_f, "sc_out_dgrad")
    d_sc_w_out = _wgrad(q0, dhb1, "sc_out_wgrad")
    dp0, d_sc_cw = _sc_mid_bwd(p0, dq0, sc_cw, "sc_mid_bwd")
    d_sc_w_in_t = _wgrad(dp0, n0, "sc_in_wgrad")
    dh0, d_mg0 = _dgrad_in_norm(dp0, sc_w_in_t, h0, norm_mix_g[0:1], dh1, "sc_in_dgrad")

    grad_x = dh0[n_meta:][None]

    big_grads = [d_sc_w_in_t, d_sc_w_out, d_rg_w_in_t, d_rg_w_out, d_up0, d_up1, d_down0, d_down1]
    big_chunks = [g.reshape(N_DEV, r_, d) for g, r_ in zip(big_grads, big_rows)]
    small_grads = {"meta_tokens": dh0[:n_meta], "sc_conv_w": d_sc_cw[None], "rg_conv_w": d_rg_cw[None],
                   "rg_conv_b": d_rg_cb, "rg_b_gate_a": d_rg_ba, "rg_b_gate_x": d_rg_bx, "rg_lambda": d_rg_lam,
                   "ffn_conv_w": jnp.stack([d_fcw0, d_fcw1])}
    small_chunks = jnp.concatenate([_cols_to_chunks(small_grads[n]) for n in small_names], axis=1)
    pad = small_rows.shape[0] * d - small_chunks.shape[1]
    small_chunks = jnp.pad(small_chunks, ((0, 0), (0, pad))).reshape(N_DEV, small_rows.shape[0], d)
    rep_names = ["norm_mix_g", "norm_ffn_g", "final_norm_g", "rg_w_gate_a", "rg_w_gate_x"]
    rep_grads = {"norm_mix_g": jnp.concatenate([d_mg0, d_mg1], axis=0),
                 "norm_ffn_g": jnp.concatenate([d_fg0, d_fg1], axis=0),
                 "final_norm_g": d_final_g.reshape(-1), "rg_w_gate_a": d_wa[None], "rg_w_gate_x": d_wx[None]}
    rep_flat = jnp.concatenate([rep_grads[n].reshape(-1) for n in rep_names])
    rep_chunk_rows = -(-rep_flat.shape[0] // (N_DEV * d))
    rep_chunk_rows += -(sum(big_rows) + small_rows.shape[0] + rep_chunk_rows) % 64
    rep_chunks = jnp.pad(rep_flat, (0, N_DEV * rep_chunk_rows * d - rep_flat.shape[0])).reshape(N_DEV, rep_chunk_rows, d)

    rs_buf = jnp.concatenate(big_chunks + [small_chunks, rep_chunks], axis=1)
    reduced = _reduce_scatter(rs_buf, core)

    grads = {}
    o = 0
    big_names = ["sc_w_in", "sc_w_out", "rg_w_in", "rg_w_out"]
    for n, r_ in zip(big_names, big_rows[:4]):
        g = reduced[o:o + r_]
        grads[n] = (g.T if n.endswith("_in") else g)[None]
        o += r_
    up = [reduced[o + l * big_rows[4]:o + (l + 1) * big_rows[4]].T for l in range(n_ffn)]
    o += n_ffn * big_rows[4]
    down = [reduced[o + l * big_rows[4 + n_ffn]:o + (l + 1) * big_rows[4 + n_ffn]] for l in range(n_ffn)]
    o += n_ffn * big_rows[4 + n_ffn]
    grads["ffn_w_up"] = jnp.stack(up)
    grads["ffn_w_down"] = jnp.stack(down)
    small_red = reduced[o:o + small_rows.shape[0]].reshape(-1)
    o += small_rows.shape[0]
    so = 0
    for n, sz in zip(small_names, small_sizes):
        grads[n] = small_red[so:so + sz].reshape(weights[n].shape)
        so += sz
    rep_red = _all_gather(reduced[o:o + rep_chunks.shape[1]], "gather_replicated_grads").reshape(-1)
    ro = 0
    for n in rep_names:
        sz = weights[n].size
        grads[n] = rep_red[ro:ro + sz].reshape(weights[n].shape)
        ro += sz

    delta, new_m, new_v = {}, {}, {}
    for n in names:
        delta[n], new_m[n], new_v[n] = _adamw_nd(weights[n], grads[n], m_in[n], v_in[n], f"adamw_{n}")

    return (loss, grad_x, *[grads[n] for n in names], *[delta[n] for n in names],
            *[new_m[n] for n in names], *[new_v[n] for n in names])
```

```python
import jax
import jax.numpy as jnp
from jax import lax
from jax.experimental import pallas as pl
from jax.experimental.pallas import tpu as pltpu

F32 = jnp.float32
MXU_DTYPE = jnp.bfloat16
RMS_EPS = 1e-6
RG_C = 8.0
ADAM_LR = 0.001
ADAM_B1 = 0.9
ADAM_B2 = 0.999
ADAM_EPS = 1e-08
ADAM_WD = 0.01
ADAM_STEP = 10

N_DEV = 8
AXES = ("x", "y", "c")
SUBLANES = 8
LANES = 128
VMEM_LIMIT_BYTES = 48 * 1024 * 1024
ROW_TILE_MATMUL = 700
ROW_TILE_MID = 128
STRIP = 256

_NT = (((1,), (1,)), ((), ()))
_TN = (((0,), (0,)), ((), ()))
_NN = (((1,), (0,)), ((), ()))


def _row_tile(t, target):
    best = None
    for tm in range(16, t + 1, 16):
        if t % tm == 0 and tm <= target:
            best = tm
    return best if best is not None else t


def _col_tile(n, target):
    best = None
    for tn in range(LANES, n + 1, LANES):
        if n % tn == 0 and tn <= target:
            best = tn
    return best if best is not None else n


def _params(*sem):
    return pltpu.CompilerParams(dimension_semantics=sem, vmem_limit_bytes=VMEM_LIMIT_BYTES)


def _dot(a, b, dims):
    return lax.dot_general(a, b, dims, preferred_element_type=F32)


def _sigmoid(x):
    return 1.0 / (1.0 + jnp.exp(-x))


def _gelu(x):
    c = 0.7978845608028654
    t = jnp.tanh(c * (x + 0.044715 * (x * x * x)))
    return 0.5 * x * (1.0 + t)


def _gelu_grad(x):
    c = 0.7978845608028654
    t = jnp.tanh(c * (x + 0.044715 * (x * x * x)))
    return 0.5 * (1.0 + t) + 0.5 * x * (1.0 - t * t) * c * (1.0 + 3.0 * 0.044715 * x * x)


def _softplus(x):
    return jnp.maximum(x, 0.0) + jnp.log1p(jnp.exp(-jnp.abs(x)))


def _rows(ref, sl):
    return [ref[k:k + 1, sl] for k in range(ref.shape[0])]


def _windows(ext, sl, n, k_width, base=SUBLANES):
    return [ext[pl.ds(base - (k_width - 1 - k), n), sl] for k in range(k_width)]


def _weighted(w, windows):
    y = w[0] * windows[0]
    for k in range(1, len(w)):
        y = y + w[k] * windows[k]
    return y


def _conv_t_ext(ext, sl, n, w):
    k_width = len(w)
    return _weighted(w, [ext[pl.ds(k_width - 1 - k, n), sl] for k in range(k_width)])


def _halo_before_spec(tm, cols):
    return pl.BlockSpec((SUBLANES, cols), lambda i: (jnp.maximum(i * (tm // SUBLANES) - 1, 0), 0))


def _halo_after_spec(tm, cols, t):
    last = t // SUBLANES - 1
    return pl.BlockSpec((SUBLANES, cols), lambda i: (jnp.minimum((i + 1) * (tm // SUBLANES), last), 0))


def _resident(shape):
    return pl.BlockSpec(shape, lambda *_: (0,) * len(shape), pipeline_mode=pl.Buffered(1))


def _norm_matmul(h, g, w, name):
    t, d = h.shape
    n = w.shape[1]
    tm = _row_tile(t, ROW_TILE_MATMUL)
    tn = _col_tile(n, 1536)

    def body(h_ref, g_ref, w_ref, n_ref, p_ref):
        @pl.when(pl.program_id(1) == 0)
        def _():
            x = h_ref[...]
            ms = jnp.mean(x * x, axis=-1, keepdims=True)
            n_ref[...] = (x * lax.rsqrt(ms + RMS_EPS) * g_ref[...]).astype(n_ref.dtype)
        p_ref[...] = _dot(n_ref[...], w_ref[...], _NN)

    return pl.pallas_call(
        body, name=name, grid=(t // tm, n // tn),
        in_specs=[pl.BlockSpec((tm, d), lambda i, j: (i, 0)),
                  pl.BlockSpec((1, d), lambda i, j: (0, 0)),
                  pl.BlockSpec((d, tn), lambda i, j: (0, j))],
        out_specs=[pl.BlockSpec((tm, d), lambda i, j: (i, 0)),
                   pl.BlockSpec((tm, tn), lambda i, j: (i, j))],
        out_shape=[jax.ShapeDtypeStruct((t, d), MXU_DTYPE), jax.ShapeDtypeStruct((t, n), F32)],
        compiler_params=_params("parallel", "arbitrary"),
    )(h, g, w)


def _matmul_residual(q, w, h, name):
    t, k = q.shape
    d = w.shape[1]
    tm = _row_tile(t, ROW_TILE_MATMUL)

    def body(q_ref, w_ref, h_ref, o_ref):
        o_ref[...] = h_ref[...] + _dot(q_ref[...], w_ref[...], _NN)

    return pl.pallas_call(
        body, name=name, grid=(t // tm,),
        in_specs=[pl.BlockSpec((tm, k), lambda i: (i, 0)),
                  _resident((k, d)),
                  pl.BlockSpec((tm, d), lambda i: (i, 0))],
        out_specs=pl.BlockSpec((tm, d), lambda i: (i, 0)),
        out_shape=jax.ShapeDtypeStruct((t, d), F32),
        compiler_params=_params("parallel"),
    )(q, w, h)


def _dgrad_out(dh, wt, name):
    t, d = dh.shape
    k = wt.shape[1]
    tm = _row_tile(t, ROW_TILE_MATMUL)

    def body(dh_ref, wt_ref, dq_ref, dhb_ref):
        dhb = dh_ref[...].astype(dhb_ref.dtype)
        dhb_ref[...] = dhb
        dq_ref[...] = _dot(dhb, wt_ref[...], _NN)

    return pl.pallas_call(
        body, name=name, grid=(t // tm,),
        in_specs=[pl.BlockSpec((tm, d), lambda i: (i, 0)),
                  _resident((d, k))],
        out_specs=[pl.BlockSpec((tm, k), lambda i: (i, 0)),
                   pl.BlockSpec((tm, d), lambda i: (i, 0))],
        out_shape=[jax.ShapeDtypeStruct((t, k), F32), jax.ShapeDtypeStruct((t, d), MXU_DTYPE)],
        compiler_params=_params("parallel"),
    )(dh, wt)


def _dgrad_in_norm(dp, wt, h, g, dh_next, name):
    t, n = dp.shape
    d = wt.shape[1]
    tm = _row_tile(t, ROW_TILE_MATMUL)

    def body(dp_ref, wt_ref, h_ref, g_ref, dhn_ref, dh_ref, dg_ref):
        i = pl.program_id(0)
        dn = _dot(dp_ref[...], wt_ref[...], _NN)
        x = h_ref[...]
        ms = jnp.mean(x * x, axis=-1, keepdims=True)
        r = lax.rsqrt(ms + RMS_EPS)
        xhat = x * r
        dng = dn * g_ref[...]
        c = jnp.mean(dng * xhat, axis=-1, keepdims=True)
        dh_ref[...] = dhn_ref[...] + r * (dng - xhat * c)
        part = jnp.sum(dn * xhat, axis=0, keepdims=True)

        @pl.when(i == 0)
        def _():
            dg_ref[...] = part

        @pl.when(i > 0)
        def _():
            dg_ref[...] += part

    return pl.pallas_call(
        body, name=name, grid=(t // tm,),
        in_specs=[pl.BlockSpec((tm, n), lambda i: (i, 0)),
                  _resident((n, d)),
                  pl.BlockSpec((tm, d), lambda i: (i, 0)),
                  pl.BlockSpec((1, d), lambda i: (0, 0)),
                  pl.BlockSpec((tm, d), lambda i: (i, 0))],
        out_specs=[pl.BlockSpec((tm, d), lambda i: (i, 0)),
                   pl.BlockSpec((1, d), lambda i: (0, 0))],
        out_shape=[jax.ShapeDtypeStruct((t, d), F32), jax.ShapeDtypeStruct((1, d), F32)],
        compiler_params=_params("arbitrary"),
    )(dp, wt, h, g, dh_next)


def _wgrad(a, b, name):
    t, m = a.shape
    d = b.shape[1]
    tmm = _col_tile(m, 1536)
    tk = _row_tile(t, ROW_TILE_MATMUL)

    def body(a_ref, b_ref, o_ref):
        @pl.when(pl.program_id(1) == 0)
        def _():
            o_ref[...] = jnp.zeros_like(o_ref)
        o_ref[...] += _dot(a_ref[...], b_ref[...], _TN)

    return pl.pallas_call(
        body, name=name, grid=(m // tmm, t // tk),
        in_specs=[pl.BlockSpec((tk, tmm), lambda i, k: (k, i)),
                  pl.BlockSpec((tk, d), lambda i, k: (k, 0))],
        out_specs=pl.BlockSpec((tmm, d), lambda i, k: (i, 0)),
        out_shape=jax.ShapeDtypeStruct((m, d), F32),
        compiler_params=_params("parallel", "arbitrary"),
    )(a, b)


def _sc_mid_fwd(p, conv_w, name):
    t, d3 = p.shape
    d = d3 // 3
    tm = _row_tile(t, ROW_TILE_MID)
    cw = min(STRIP, d)
    kw = conv_w.shape[0]

    def body(p_ref, ph_ref, w_ref, q_ref, ext):
        first = pl.program_id(0) == 0
        ext[0:SUBLANES, :] = jnp.where(first, 0.0, ph_ref[:, d:2 * d] * ph_ref[:, 2 * d:3 * d])
        for c in range(0, d, cw):
            sl = slice(c, c + cw)
            ext[SUBLANES:SUBLANES + tm, sl] = p_ref[:, d + c:d + c + cw] * p_ref[:, 2 * d + c:2 * d + c + cw]
            u = _weighted(_rows(w_ref, sl), _windows(ext, sl, tm, kw))
            q_ref[:, sl] = (p_ref[:, sl] * u).astype(q_ref.dtype)

    return pl.pallas_call(
        body, name=name, grid=(t // tm,),
        in_specs=[pl.BlockSpec((tm, d3), lambda i: (i, 0)),
                  _halo_before_spec(tm, d3),
                  pl.BlockSpec(conv_w.shape, lambda i: (0, 0))],
        out_specs=pl.BlockSpec((tm, d), lambda i: (i, 0)),
        out_shape=jax.ShapeDtypeStruct((t, d), MXU_DTYPE),
        scratch_shapes=[pltpu.VMEM((tm + SUBLANES, d), F32)],
        compiler_params=_params("parallel"),
    )(p, p, conv_w)


def _sc_mid_bwd(p, dq, conv_w, name):
    t, d3 = p.shape
    d = d3 // 3
    tm = _row_tile(t, ROW_TILE_MID)
    nt = t // tm
    cw = min(STRIP, d)
    kw = conv_w.shape[0]

    def body(p_ref, phb_ref, pha_ref, dq_ref, dqa_ref, w_ref, dp_ref, dw_ref, ext_cv, ext_du):
        i = pl.program_id(0)
        first, last = i == 0, i == nt - 1

        @pl.when(first)
        def _():
            dw_ref[...] = jnp.zeros_like(dw_ref)

        ext_cv[0:SUBLANES, :] = jnp.where(first, 0.0, phb_ref[:, d:2 * d] * phb_ref[:, 2 * d:3 * d])
        ext_du[tm:tm + SUBLANES, :] = jnp.where(last, 0.0, dqa_ref[...] * pha_ref[:, 0:d])
        for c in range(0, d, cw):
            sl = slice(c, c + cw)
            w = _rows(w_ref, sl)
            bg = p_ref[:, sl]
            cg = p_ref[:, d + c:d + c + cw]
            v = p_ref[:, 2 * d + c:2 * d + c + cw]
            ext_cv[SUBLANES:SUBLANES + tm, sl] = cg * v
            windows = _windows(ext_cv, sl, tm, kw)
            u = _weighted(w, windows)
            dq = dq_ref[:, sl]
            du = dq * bg
            ext_du[0:tm, sl] = du
            dcv = _conv_t_ext(ext_du, sl, tm, w)
            dp_ref[:, sl] = (dq * u).astype(dp_ref.dtype)
            dp_ref[:, d + c:d + c + cw] = (dcv * v).astype(dp_ref.dtype)
            dp_ref[:, 2 * d + c:2 * d + c + cw] = (dcv * cg).astype(dp_ref.dtype)
            for k in range(kw):
                dw_ref[k:k + 1, sl] += jnp.sum(du * windows[k], axis=0, keepdims=True)

    return pl.pallas_call(
        body, name=name, grid=(nt,),
        in_specs=[pl.BlockSpec((tm, d3), lambda i: (i, 0)),
                  _halo_before_spec(tm, d3),
                  _halo_after_spec(tm, d3, t),
                  pl.BlockSpec((tm, d), lambda i: (i, 0)),
                  _halo_after_spec(tm, d, t),
                  pl.BlockSpec(conv_w.shape, lambda i: (0, 0))],
        out_specs=[pl.BlockSpec((tm, d3), lambda i: (i, 0)),
                   pl.BlockSpec(conv_w.shape, lambda i: (0, 0))],
        out_shape=[jax.ShapeDtypeStruct((t, d3), MXU_DTYPE), jax.ShapeDtypeStruct(conv_w.shape, F32)],
        scratch_shapes=[pltpu.VMEM((tm + SUBLANES, d), F32), pltpu.VMEM((tm + SUBLANES, d), F32)],
        compiler_params=_params("arbitrary"),
    )(p, p, p, dq, dq, conv_w)


def _ffn_mid_fwd(a, conv_w, name):
    t, f2 = a.shape
    f = f2 // 2
    tm = _row_tile(t, ROW_TILE_MID)
    cw = min(STRIP, f)
    kw = conv_w.shape[0]

    def body(a_ref, ah_ref, w_ref, z_ref, ext):
        first = pl.program_id(0) == 0
        ext[0:SUBLANES, :] = jnp.where(first, 0.0, ah_ref[...])
        ext[SUBLANES:SUBLANES + tm, :] = a_ref[...]
        for c in range(0, f, cw):
            gsl, vsl = slice(c, c + cw), slice(f + c, f + c + cw)
            gg = _weighted(_rows(w_ref, gsl), _windows(ext, gsl, tm, kw))
            vv = _weighted(_rows(w_ref, vsl), _windows(ext, vsl, tm, kw))
            z_ref[:, gsl] = (gg * _sigmoid(gg) * vv).astype(z_ref.dtype)

    return pl.pallas_call(
        body, name=name, grid=(t // tm,),
        in_specs=[pl.BlockSpec((tm, f2), lambda i: (i, 0)),
                  _halo_before_spec(tm, f2),
                  pl.BlockSpec(conv_w.shape, lambda i: (0, 0))],
        out_specs=pl.BlockSpec((tm, f), lambda i: (i, 0)),
        out_shape=jax.ShapeDtypeStruct((t, f), MXU_DTYPE),
        scratch_shapes=[pltpu.VMEM((tm + SUBLANES, f2), F32)],
        compiler_params=_params("parallel"),
    )(a, a, conv_w)


def _ffn_mid_bwd(a, dz, conv_w, name):
    t, f2 = a.shape
    f = f2 // 2
    tm = _row_tile(t, ROW_TILE_MID)
    nt = t // tm
    cw = min(STRIP, f)
    kw = conv_w.shape[0]

    def body(a_ref, ahb_ref, aha_ref, dz_ref, dza_ref, w_ref, da_ref, dw_ref, ext_a, ext_d):
        i = pl.program_id(0)
        first, last = i == 0, i == nt - 1

        @pl.when(first)
        def _():
            dw_ref[...] = jnp.zeros_like(dw_ref)

        ext_a[0:SUBLANES, :] = jnp.where(first, 0.0, ahb_ref[...])
        ext_a[SUBLANES:SUBLANES + tm, :] = a_ref[...]
        ext_a[SUBLANES + tm:2 * SUBLANES + tm, :] = aha_ref[...]

        def grads(dz, gg, vv):
            s = _sigmoid(gg)
            return dz * vv * s * (1.0 + gg * (1.0 - s)), dz * gg * s

        for c in range(0, f, cw):
            gsl, vsl = slice(c, c + cw), slice(f + c, f + c + cw)
            wg, wv = _rows(w_ref, gsl), _rows(w_ref, vsl)
            sg, sv = _windows(ext_a, gsl, tm, kw), _windows(ext_a, vsl, tm, kw)
            dgg, dvv = grads(dz_ref[:, gsl], _weighted(wg, sg), _weighted(wv, sv))
            gg_a = _weighted(wg, _windows(ext_a, gsl, SUBLANES, kw, base=SUBLANES + tm))
            vv_a = _weighted(wv, _windows(ext_a, vsl, SUBLANES, kw, base=SUBLANES + tm))
            dgg_a, dvv_a = grads(dza_ref[:, gsl], gg_a, vv_a)
            ext_d[0:tm, gsl] = dgg
            ext_d[0:tm, vsl] = dvv
            ext_d[tm:tm + SUBLANES, gsl] = jnp.where(last, 0.0, dgg_a)
            ext_d[tm:tm + SUBLANES, vsl] = jnp.where(last, 0.0, dvv_a)
            da_ref[:, gsl] = _conv_t_ext(ext_d, gsl, tm, wg).astype(da_ref.dtype)
            da_ref[:, vsl] = _conv_t_ext(ext_d, vsl, tm, wv).astype(da_ref.dtype)
            for k in range(kw):
                dw_ref[k:k + 1, gsl] += jnp.sum(dgg * sg[k], axis=0, keepdims=True)
                dw_ref[k:k + 1, vsl] += jnp.sum(dvv * sv[k], axis=0, keepdims=True)

    return pl.pallas_call(
        body, name=name, grid=(nt,),
        in_specs=[pl.BlockSpec((tm, f2), lambda i: (i, 0)),
                  _halo_before_spec(tm, f2),
                  _halo_after_spec(tm, f2, t),
                  pl.BlockSpec((tm, f), lambda i: (i, 0)),
                  _halo_after_spec(tm, f, t),
                  pl.BlockSpec(conv_w.shape, lambda i: (0, 0))],
        out_specs=[pl.BlockSpec((tm, f2), lambda i: (i, 0)),
                   pl.BlockSpec(conv_w.shape, lambda i: (0, 0))],
        out_shape=[jax.ShapeDtypeStruct((t, f2), MXU_DTYPE), jax.ShapeDtypeStruct(conv_w.shape, F32)],
        scratch_shapes=[pltpu.VMEM((tm + 2 * SUBLANES, f2), F32), pltpu.VMEM((tm + SUBLANES, f2), F32)],
        compiler_params=_params("arbitrary"),
    )(a, a, a, dz, dz, conv_w)


def _rg_gates(u, k, wa_ref, ba, wx_ref, bx, lam):
    ub = u.astype(MXU_DTYPE)
    r = _sigmoid(_dot(ub, wa_ref[k], _NN) + ba)
    ig = _sigmoid(_dot(ub, wx_ref[k], _NN) + bx)
    sp = _softplus(-lam)
    la = -RG_C * r * sp
    a = jnp.exp(la)
    th = jnp.tanh(la)
    mult = jnp.sqrt(-2.0 * th / (1.0 - th))
    return ub, r, ig, sp, a, mult


def _rg_mid_fwd(p, conv_w, conv_b, wa, ba, wx, bx, lam, name):
    t, r2 = p.shape
    rr = r2 // 2
    nb, bd, _ = wa.shape
    tm = _row_tile(t, ROW_TILE_MID)
    kw = conv_w.shape[0]

    def body(p_ref, ph_ref, cw_ref, cb_ref, wa_ref, ba_ref, wx_ref, bx_ref, lam_ref,
             y_ref, hs_ref, ext_rb, a_scr, b_scr, carry_ref):
        first = pl.program_id(0) == 0

        @pl.when(first)
        def _():
            carry_ref[...] = jnp.zeros_like(carry_ref)

        ext_rb[0:SUBLANES, :] = jnp.where(first, 0.0, ph_ref[:, rr:r2])
        ext_rb[SUBLANES:SUBLANES + tm, :] = p_ref[:, rr:r2]
        for k in range(nb):
            sl = slice(k * bd, (k + 1) * bd)
            u = _weighted(_rows(cw_ref, sl), _windows(ext_rb, sl, tm, kw)) + cb_ref[:, sl]
            _, _, ig, _, a, mult = _rg_gates(u, k, wa_ref, ba_ref[:, sl], wx_ref, bx_ref[:, sl], lam_ref[:, sl])
            a_scr[:, sl] = a
            b_scr[:, sl] = mult * (ig * u)

        row = lax.broadcasted_iota(jnp.int32, (SUBLANES, rr), 0)

        def group(gi, carry):
            r0 = pl.multiple_of(gi * SUBLANES, SUBLANES)
            a = a_scr[pl.ds(r0, SUBLANES), :]
            b = b_scr[pl.ds(r0, SUBLANES), :]
            for s in (1, 2, 4):
                a_s = jnp.where(row >= s, pltpu.roll(a, s, axis=0), 1.0)
                b_s = jnp.where(row >= s, pltpu.roll(b, s, axis=0), 0.0)
                b = a * b_s + b
                a = a * a_s
            h = a * carry + b
            hs_ref[pl.ds(r0, SUBLANES), :] = h
            return jnp.broadcast_to(h[SUBLANES - 1:SUBLANES, :], (SUBLANES, rr))

        carry_ref[...] = lax.fori_loop(0, tm // SUBLANES, group, carry_ref[...])

        for k in range(nb):
            sl = slice(k * bd, (k + 1) * bd)
            y_ref[:, sl] = (hs_ref[:, sl] * _gelu(p_ref[:, sl])).astype(y_ref.dtype)

    full = lambda s: pl.BlockSpec(s, lambda i: (0,) * len(s))
    vm = lambda rows: pltpu.VMEM((rows, rr), F32)
    return pl.pallas_call(
        body, name=name, grid=(t // tm,),
        in_specs=[pl.BlockSpec((tm, r2), lambda i: (i, 0)),
                  _halo_before_spec(tm, r2),
                  full(conv_w.shape), full(conv_b.shape), full(wa.shape), full(ba.shape),
                  full(wx.shape), full(bx.shape), full(lam.shape)],
        out_specs=[pl.BlockSpec((tm, rr), lambda i: (i, 0)),
                   pl.BlockSpec((tm, rr), lambda i: (i, 0))],
        out_shape=[jax.ShapeDtypeStruct((t, rr), MXU_DTYPE), jax.ShapeDtypeStruct((t, rr), F32)],
        scratch_shapes=[vm(tm + SUBLANES), vm(tm), vm(tm), vm(SUBLANES)],
        compiler_params=_params("arbitrary"),
    )(p, p, conv_w, conv_b, wa, ba, wx, bx, lam)


def _rg_mid_bwd(p, dy, hs, conv_w, conv_b, wa, ba, wx, bx, lam, name):
    t, r2 = p.shape
    rr = r2 // 2
    nb, bd, _ = wa.shape
    tm = _row_tile(t, ROW_TILE_MID)
    nt = t // tm
    kw = conv_w.shape[0]
    blocks_per_tile = tm // SUBLANES

    def body(p_ref, ph_ref, dy_ref, hs_ref, hsh_ref, cw_ref, cb_ref, wa_ref, ba_ref, wx_ref, bx_ref, lam_ref,
             dp_ref, dcw_ref, dcb_ref, dwa_ref, dba_ref, dwx_ref, dbx_ref, dlam_ref,
             ext_rb, ext_hs, ext_a, ext_du, as_scr, u_scr, r_scr, ig_scr, mult_scr, g_scr, carry_g):
        i = pl.program_id(0)
        newest = i == 0
        oldest = i == nt - 1

        @pl.when(newest)
        def _():
            carry_g[...] = jnp.zeros_like(carry_g)
            ext_du[tm:tm + SUBLANES, :] = jnp.zeros((SUBLANES, rr), F32)
            for ref in (dcw_ref, dcb_ref, dwa_ref, dba_ref, dwx_ref, dbx_ref, dlam_ref):
                ref[...] = jnp.zeros_like(ref)

        ext_rb[0:SUBLANES, :] = jnp.where(oldest, 0.0, ph_ref[:, rr:r2])
        ext_rb[SUBLANES:SUBLANES + tm, :] = p_ref[:, rr:r2]
        ext_hs[0:SUBLANES, :] = jnp.where(oldest, 0.0, hsh_ref[...])
        ext_hs[SUBLANES:SUBLANES + tm, :] = hs_ref[...]
        ext_a[tm:tm + SUBLANES, :] = jnp.ones((SUBLANES, rr), F32)
        for k in range(nb):
            sl = slice(k * bd, (k + 1) * bd)
            u = _weighted(_rows(cw_ref, sl), _windows(ext_rb, sl, tm, kw)) + cb_ref[:, sl]
            _, r, ig, _, a, mult = _rg_gates(u, k, wa_ref, ba_ref[:, sl], wx_ref, bx_ref[:, sl], lam_ref[:, sl])
            gb = p_ref[:, sl]
            dy = dy_ref[:, sl]
            g_scr[:, sl] = dy * _gelu(gb)
            dp_ref[:, sl] = (dy * hs_ref[:, sl] * _gelu_grad(gb)).astype(dp_ref.dtype)
            ext_a[0:tm, sl] = a
            as_scr[:, sl] = ext_a[pl.ds(1, tm), sl]
            u_scr[:, sl] = u
            r_scr[:, sl] = r
            ig_scr[:, sl] = ig
            mult_scr[:, sl] = mult

        row = lax.broadcasted_iota(jnp.int32, (SUBLANES, rr), 0)

        def group(j, carry):
            r0 = pl.multiple_of((blocks_per_tile - 1 - j) * SUBLANES, SUBLANES)
            a = as_scr[pl.ds(r0, SUBLANES), :]
            d = g_scr[pl.ds(r0, SUBLANES), :]
            for s in (1, 2, 4):
                a_s = jnp.where(row < SUBLANES - s, pltpu.roll(a, SUBLANES - s, axis=0), 1.0)
                d_s = jnp.where(row < SUBLANES - s, pltpu.roll(d, SUBLANES - s, axis=0), 0.0)
                d = a * d_s + d
                a = a * a_s
            g = a * carry + d
            g_scr[pl.ds(r0, SUBLANES), :] = g
            return jnp.broadcast_to(g[0:1, :], (SUBLANES, rr))

        g_first = lax.fori_loop(0, blocks_per_tile, group, carry_g[...])
        carry_g[...] = jnp.broadcast_to(ext_a[0:1, :], (SUBLANES, rr)) * g_first

        for k in range(nb):
            sl = slice(k * bd, (k + 1) * bd)
            rsl = slice(rr + k * bd, rr + (k + 1) * bd)
            cw = _rows(cw_ref, sl)
            lam_k = lam_ref[:, sl]
            sp = _softplus(-lam_k)
            g = g_scr[:, sl]
            a, u, r, ig, mult = ext_a[0:tm, sl], u_scr[:, sl], r_scr[:, sl], ig_scr[:, sl], mult_scr[:, sl]
            da = g * ext_hs[pl.ds(SUBLANES - 1, tm), sl]
            dmult = g * (ig * u)
            d_iu = g * mult
            dla = da * a - dmult * (a * a) / mult
            dr = dla * (-RG_C * sp)
            dsp = jnp.sum(dla * (-RG_C * r), axis=0, keepdims=True)
            dlam_ref[:, sl] += dsp * (-_sigmoid(-lam_k))
            dza = dr * r * (1.0 - r)
            dzx = (d_iu * u) * ig * (1.0 - ig)
            dba_ref[:, sl] += jnp.sum(dza, axis=0, keepdims=True)
            dbx_ref[:, sl] += jnp.sum(dzx, axis=0, keepdims=True)
            ub = u.astype(MXU_DTYPE)
            dzab, dzxb = dza.astype(MXU_DTYPE), dzx.astype(MXU_DTYPE)
            dwa_ref[k] += _dot(ub, dzab, _TN)
            dwx_ref[k] += _dot(ub, dzxb, _TN)
            du = d_iu * ig + _dot(dzab, wa_ref[k], _NT) + _dot(dzxb, wx_ref[k], _NT)
            dcb_ref[:, sl] += jnp.sum(du, axis=0, keepdims=True)
            windows = _windows(ext_rb, sl, tm, kw)
            for kk in range(kw):
                dcw_ref[kk:kk + 1, sl] += jnp.sum(du * windows[kk], axis=0, keepdims=True)
            ext_du[0:tm, sl] = du
            dp_ref[:, rsl] = _conv_t_ext(ext_du, sl, tm, cw).astype(dp_ref.dtype)
            ext_du[tm:tm + SUBLANES, sl] = du[0:SUBLANES]

    rev = lambda i: nt - 1 - i
    full = lambda s: pl.BlockSpec(s, lambda i: (0,) * len(s))
    halo = lambda cols: pl.BlockSpec(
        (SUBLANES, cols), lambda i: (jnp.maximum(rev(i) * (tm // SUBLANES) - 1, 0), 0))
    tile = lambda cols: pl.BlockSpec((tm, cols), lambda i: (rev(i), 0))
    vm = lambda rows: pltpu.VMEM((rows, rr), F32)
    return pl.pallas_call(
        body, name=name, grid=(nt,),
        in_specs=[tile(r2), halo(r2), tile(rr), tile(rr), halo(rr),
                  full(conv_w.shape), full(conv_b.shape), full(wa.shape), full(ba.shape),
                  full(wx.shape), full(bx.shape), full(lam.shape)],
        out_specs=[tile(r2), full(conv_w.shape), full(conv_b.shape), full(wa.shape), full(ba.shape),
                   full(wx.shape), full(bx.shape), full(lam.shape)],
        out_shape=[jax.ShapeDtypeStruct((t, r2), MXU_DTYPE),
                   jax.ShapeDtypeStruct(conv_w.shape, F32), jax.ShapeDtypeStruct(conv_b.shape, F32),
                   jax.ShapeDtypeStruct(wa.shape, F32), jax.ShapeDtypeStruct(ba.shape, F32),
                   jax.ShapeDtypeStruct(wx.shape, F32), jax.ShapeDtypeStruct(bx.shape, F32),
                   jax.ShapeDtypeStruct(lam.shape, F32)],
        scratch_shapes=[vm(tm + SUBLANES), vm(tm + SUBLANES), vm(tm + SUBLANES), vm(tm + SUBLANES),
                        vm(tm), vm(tm), vm(tm), vm(tm), vm(tm), vm(tm), vm(SUBLANES)],
        compiler_params=_params("arbitrary"),
    )(p, p, dy, hs, hs, conv_w, conv_b, wa, ba, wx, bx, lam)


def _loss_head(h, g, target, n_meta, name):
    t, d = h.shape
    tm = _row_tile(t, ROW_TILE_MATMUL)

    def body(h_ref, g_ref, t_ref, dh_ref, loss_ref, dg_ref):
        i = pl.program_id(0)
        x = h_ref[...]
        ms = jnp.mean(x * x, axis=-1, keepdims=True)
        r = lax.rsqrt(ms + RMS_EPS)
        xhat = x * r
        gg = g_ref[...]
        row = i * tm + lax.broadcasted_iota(jnp.int32, (tm, 1), 0)
        err = jnp.where(row >= n_meta, xhat * gg - t_ref[...], 0.0)
        dout = err * (1.0 / d)
        dng = dout * gg
        c = jnp.mean(dng * xhat, axis=-1, keepdims=True)
        dh_ref[...] = r * (dng - xhat * c)
        part_dg = jnp.sum(dout * xhat, axis=0, keepdims=True)
        part_loss = jnp.broadcast_to(0.5 * jnp.sum(err * dout, keepdims=True), loss_ref.shape)

        @pl.when(i == 0)
        def _():
            dg_ref[...] = part_dg
            loss_ref[...] = part_loss

        @pl.when(i > 0)
        def _():
            dg_ref[...] += part_dg
            loss_ref[...] += part_loss

    return pl.pallas_call(
        body, name=name, grid=(t // tm,),
        in_specs=[pl.BlockSpec((tm, d), lambda i: (i, 0)),
                  pl.BlockSpec((1, d), lambda i: (0, 0)),
                  pl.BlockSpec((tm, d), lambda i: (i, 0))],
        out_specs=[pl.BlockSpec((tm, d), lambda i: (i, 0)),
                   pl.BlockSpec((SUBLANES, LANES), lambda i: (0, 0)),
                   pl.BlockSpec((1, d), lambda i: (0, 0))],
        out_shape=[jax.ShapeDtypeStruct((t, d), F32), jax.ShapeDtypeStruct((SUBLANES, LANES), F32),
                   jax.ShapeDtypeStruct((1, d), F32)],
        compiler_params=_params("arbitrary"),
    )(h, g, target)


def _adamw(w, g, m, v, name):
    rows, cols = w.shape
    tr = rows
    if rows > 512:
        for cand in range(8, 513, 8):
            if rows % cand == 0:
                tr = cand

    def body(w_ref, g_ref, m_ref, v_ref, d_ref, nm_ref, nv_ref):
        g_ = g_ref[...]
        m_ = ADAM_B1 * m_ref[...] + (1.0 - ADAM_B1) * g_
        v_ = ADAM_B2 * v_ref[...] + (1.0 - ADAM_B2) * (g_ * g_)
        m_hat = m_ / (1.0 - ADAM_B1 ** ADAM_STEP)
        v_hat = v_ / (1.0 - ADAM_B2 ** ADAM_STEP)
        d_ref[...] = -ADAM_LR * (m_hat / (jnp.sqrt(v_hat) + ADAM_EPS) + ADAM_WD * w_ref[...])
        nm_ref[...] = m_
        nv_ref[...] = v_

    spec = pl.BlockSpec((tr, cols), lambda i: (i, 0))
    shape = jax.ShapeDtypeStruct((rows, cols), F32)
    return pl.pallas_call(
        body, name=name, grid=(rows // tr,),
        in_specs=[spec] * 4, out_specs=[spec] * 3, out_shape=[shape] * 3,
        compiler_params=_params("parallel"),
    )(w, g, m, v)


def _adamw_nd(w, g, m, v, name):
    shape = w.shape
    two_d = (-1, shape[-1]) if w.ndim > 1 else (1, -1)
    outs = _adamw(w.reshape(two_d), g.reshape(two_d), m.reshape(two_d), v.reshape(two_d), name)
    return tuple(o.reshape(shape) for o in outs)


_ANY = pl.BlockSpec(memory_space=pl.ANY)
_MESH = pl.DeviceIdType.MESH


def _all_gather(blk, name):
    rows, cols = blk.shape

    def body(x_ref, out_ref, send_sems, recv_sems, local_sem):
        x, y, c = lax.axis_index("x"), lax.axis_index("y"), lax.axis_index("c")
        me, sibling = (x, y, c), (x, y, 1 - c)
        chips = [(1 - x, y), (x, 1 - y), (1 - x, 1 - y)]

        def slot(px, py, pc):
            return out_ref.at[4 * px + 2 * py + pc]

        def copy(k, block, to, src=None):
            return pltpu.make_async_remote_copy(
                src_ref=slot(*block) if src is None else src, dst_ref=slot(*block),
                send_sem=send_sems.at[k], recv_sem=recv_sems.at[k], device_id=to, device_id_type=_MESH)

        mine = pltpu.make_async_copy(x_ref, slot(*me), local_sem)
        mine.start()
        first = [copy(0, me, sibling, src=x_ref)]
        first += [copy(1 + j, me, (*chip, c), src=x_ref) for j, chip in enumerate(chips)]
        for cp in first:
            cp.start()
        passed = [copy(4 + j, (*chip, c), sibling) for j, chip in enumerate(chips)]
        for j, chip in enumerate(chips):
            copy(1 + j, (*chip, c), me).wait_recv()
            passed[j].start()
        copy(0, sibling, me).wait_recv()
        for j, chip in enumerate(chips):
            copy(4 + j, (*chip, 1 - c), me).wait_recv()
        for cp in first + passed:
            cp.wait_send()
        mine.wait()

    return pl.pallas_call(
        body, name=name,
        in_specs=[_ANY], out_specs=_ANY,
        out_shape=jax.ShapeDtypeStruct((N_DEV, rows, cols), blk.dtype),
        scratch_shapes=[pltpu.SemaphoreType.DMA((7,)), pltpu.SemaphoreType.DMA((7,)), pltpu.SemaphoreType.DMA(())],
    )(blk)


def _rs_sibling_exchange(buf, name):
    _, rows, cols = buf.shape

    def body(buf_ref, out_ref, send_sems, recv_sems):
        x, y, c = lax.axis_index("x"), lax.axis_index("y"), lax.axis_index("c")
        copies = [pltpu.make_async_remote_copy(
            src_ref=buf_ref.at[2 * k + (1 - c)], dst_ref=out_ref.at[k],
            send_sem=send_sems.at[k], recv_sem=recv_sems.at[k],
            device_id=(x, y, 1 - c), device_id_type=_MESH) for k in range(4)]
        for cp in copies:
            cp.start()
        for cp in copies:
            cp.wait()

    return pl.pallas_call(
        body, name=name,
        in_specs=[_ANY], out_specs=_ANY,
        out_shape=jax.ShapeDtypeStruct((4, rows, cols), buf.dtype),
        scratch_shapes=[pltpu.SemaphoreType.DMA((4,)), pltpu.SemaphoreType.DMA((4,))],
    )(buf)


def _rs_chip_sum(buf, recv, core, name):
    _, rows, cols = buf.shape
    tr = _row_tile(rows, 512) if rows % 16 == 0 else rows

    def body(c_ref, a_ref, b_ref, o_ref):
        o_ref[...] = a_ref[...] + b_ref[...]

    grid_spec = pltpu.PrefetchScalarGridSpec(
        num_scalar_prefetch=1, grid=(4, rows // tr),
        in_specs=[pl.BlockSpec((None, tr, cols), lambda k, j, c: (2 * k + c[0], j, 0)),
                  pl.BlockSpec((None, tr, cols), lambda k, j, c: (k, j, 0))],
        out_specs=pl.BlockSpec((None, tr, cols), lambda k, j, c: (k, j, 0)))
    return pl.pallas_call(
        body, name=name, grid_spec=grid_spec,
        out_shape=jax.ShapeDtypeStruct((4, rows, cols), buf.dtype),
        compiler_params=_params("parallel", "parallel"),
    )(core, buf, recv)


def _rs_chip_exchange(chip_sum, name):
    _, rows, cols = chip_sum.shape

    def body(cs_ref, out_ref, send_sems, recv_sems, local_sem):
        x, y, c = lax.axis_index("x"), lax.axis_index("y"), lax.axis_index("c")
        my_chip = 2 * x + y
        mine = pltpu.make_async_copy(cs_ref.at[my_chip], out_ref.at[my_chip], local_sem)
        mine.start()
        chips = [(1 - x, y), (x, 1 - y), (1 - x, 1 - y)]
        copies = [pltpu.make_async_remote_copy(
            src_ref=cs_ref.at[2 * px + py], dst_ref=out_ref.at[my_chip],
            send_sem=send_sems.at[j], recv_sem=recv_sems.at[j],
            device_id=(px, py, c), device_id_type=_MESH) for j, (px, py) in enumerate(chips)]
        for cp in copies:
            cp.start()
        for cp in copies:
            cp.wait()
        mine.wait()

    return pl.pallas_call(
        body, name=name,
        in_specs=[_ANY], out_specs=_ANY,
        out_shape=jax.ShapeDtypeStruct((4, rows, cols), chip_sum.dtype),
        scratch_shapes=[pltpu.SemaphoreType.DMA((3,)), pltpu.SemaphoreType.DMA((3,)), pltpu.SemaphoreType.DMA(())],
    )(chip_sum)


def _sum_slots(parts, name):
    _, rows, cols = parts.shape
    tr = _row_tile(rows, 512) if rows % 16 == 0 else rows

    def body(p_ref, o_ref):
        o_ref[...] = ((p_ref[0] + p_ref[1]) + p_ref[2]) + p_ref[3]

    return pl.pallas_call(
        body, name=name, grid=(rows // tr,),
        in_specs=[pl.BlockSpec((4, tr, cols), lambda i: (0, i, 0))],
        out_specs=pl.BlockSpec((tr, cols), lambda i: (i, 0)),
        out_shape=jax.ShapeDtypeStruct((rows, cols), parts.dtype),
        compiler_params=_params("parallel"),
    )(parts)


def _reduce_scatter(buf, core):
    recv = _rs_sibling_exchange(buf, "rs_sibling_exchange")
    chip_sum = _rs_chip_sum(buf, recv, core, "rs_chip_sum")
    parts = _rs_chip_exchange(chip_sum, "rs_chip_exchange")
    return _sum_slots(parts, "rs_sum_slots")


def _pad_rows(flat, cols, multiple):
    n = flat.shape[0]
    rows = -(-n // cols)
    rows = -(-rows // multiple) * multiple
    return jnp.pad(flat, (0, rows * cols - n)).reshape(rows, cols)


def _cols_to_chunks(full):
    lead = full.shape[:-1]
    c = full.shape[-1] // N_DEV
    x = full.reshape(-1, N_DEV, c)
    return jnp.transpose(x, (1, 0, 2)).reshape(N_DEV, -1)


def _chunks_to_cols(chunks, lead):
    n = 1
    for s in lead:
        n *= s
    c = chunks.shape[1] // n
    x = chunks.reshape(N_DEV, n, c)
    return jnp.transpose(x, (1, 0, 2)).reshape(tuple(lead) + (N_DEV * c,))


def kernel(x, meta_tokens, norm_mix_g, norm_ffn_g, final_norm_g, sc_w_in, sc_conv_w, sc_w_out, rg_w_in, rg_conv_w, rg_conv_b, rg_w_gate_a, rg_b_gate_a, rg_w_gate_x, rg_b_gate_x, rg_lambda, rg_w_out, ffn_w_up, ffn_conv_w, ffn_w_down, loss_target, m_meta_tokens, m_norm_mix_g, m_norm_ffn_g, m_final_norm_g, m_sc_w_in, m_sc_conv_w, m_sc_w_out, m_rg_w_in, m_rg_conv_w, m_rg_conv_b, m_rg_w_gate_a, m_rg_b_gate_a, m_rg_w_gate_x, m_rg_b_gate_x, m_rg_lambda, m_rg_w_out, m_ffn_w_up, m_ffn_conv_w, m_ffn_w_down, v_meta_tokens, v_norm_mix_g, v_norm_ffn_g, v_final_norm_g, v_sc_w_in, v_sc_conv_w, v_sc_w_out, v_rg_w_in, v_rg_conv_w, v_rg_conv_b, v_rg_w_gate_a, v_rg_b_gate_a, v_rg_w_gate_x, v_rg_b_gate_x, v_rg_lambda, v_rg_w_out, v_ffn_w_up, v_ffn_conv_w, v_ffn_w_down):
    weights = dict(meta_tokens=meta_tokens, norm_mix_g=norm_mix_g, norm_ffn_g=norm_ffn_g, final_norm_g=final_norm_g,
                   sc_w_in=sc_w_in, sc_conv_w=sc_conv_w, sc_w_out=sc_w_out, rg_w_in=rg_w_in, rg_conv_w=rg_conv_w,
                   rg_conv_b=rg_conv_b, rg_w_gate_a=rg_w_gate_a, rg_b_gate_a=rg_b_gate_a, rg_w_gate_x=rg_w_gate_x,
                   rg_b_gate_x=rg_b_gate_x, rg_lambda=rg_lambda, rg_w_out=rg_w_out, ffn_w_up=ffn_w_up,
                   ffn_conv_w=ffn_conv_w, ffn_w_down=ffn_w_down)
    m_in = dict(meta_tokens=m_meta_tokens, norm_mix_g=m_norm_mix_g, norm_ffn_g=m_norm_ffn_g, final_norm_g=m_final_norm_g,
                sc_w_in=m_sc_w_in, sc_conv_w=m_sc_conv_w, sc_w_out=m_sc_w_out, rg_w_in=m_rg_w_in, rg_conv_w=m_rg_conv_w,
                rg_conv_b=m_rg_conv_b, rg_w_gate_a=m_rg_w_gate_a, rg_b_gate_a=m_rg_b_gate_a, rg_w_gate_x=m_rg_w_gate_x,
                rg_b_gate_x=m_rg_b_gate_x, rg_lambda=m_rg_lambda, rg_w_out=m_rg_w_out, ffn_w_up=m_ffn_w_up,
                ffn_conv_w=m_ffn_conv_w, ffn_w_down=m_ffn_w_down)
    v_in = dict(meta_tokens=v_meta_tokens, norm_mix_g=v_norm_mix_g, norm_ffn_g=v_norm_ffn_g, final_norm_g=v_final_norm_g,
                sc_w_in=v_sc_w_in, sc_conv_w=v_sc_conv_w, sc_w_out=v_sc_w_out, rg_w_in=v_rg_w_in, rg_conv_w=v_rg_conv_w,
                rg_conv_b=v_rg_conv_b, rg_w_gate_a=v_rg_w_gate_a, rg_b_gate_a=v_rg_b_gate_a, rg_w_gate_x=v_rg_w_gate_x,
                rg_b_gate_x=v_rg_b_gate_x, rg_lambda=v_rg_lambda, rg_w_out=v_rg_w_out, ffn_w_up=v_ffn_w_up,
                ffn_conv_w=v_ffn_conv_w, ffn_w_down=v_ffn_w_down)
    names = list(weights)

    seq, d = x.shape[1], x.shape[2]
    n_meta = meta_tokens.shape[0]
    n_ffn = ffn_w_up.shape[0]
    rr = rg_w_gate_a.shape[1] * rg_w_gate_a.shape[2]
    core = lax.axis_index("c").astype(jnp.int32).reshape(1)

    big_parts = [sc_w_in[0].T, sc_w_out[0], rg_w_in[0].T, rg_w_out[0]]
    big_parts += [ffn_w_up[l].T for l in range(n_ffn)] + [ffn_w_down[l] for l in range(n_ffn)]
    big_rows = [p.shape[0] for p in big_parts]
    big = _all_gather(jnp.concatenate(big_parts, axis=0).astype(MXU_DTYPE), "gather_matrices")
    offs = [0]
    for r_ in big_rows:
        offs.append(offs[-1] + r_)
    full_big = [big[:, offs[k]:offs[k + 1], :].reshape(N_DEV * big_rows[k], d) for k in range(len(big_parts))]
    sc_w_in_t, sc_w_out_f, rg_w_in_t, rg_w_out_f = full_big[:4]
    ffn_w_up_t = full_big[4:4 + n_ffn]
    ffn_w_down_f = full_big[4 + n_ffn:4 + 2 * n_ffn]
    sc_w_in_f, rg_w_in_f, ffn_w_up_f = sc_w_in_t.T, rg_w_in_t.T, [w.T for w in ffn_w_up_t]
    sc_w_out_t, rg_w_out_t, ffn_w_down_t = sc_w_out_f.T, rg_w_out_f.T, [w.T for w in ffn_w_down_f]

    small_names = ["meta_tokens", "sc_conv_w", "rg_conv_w", "rg_conv_b", "rg_b_gate_a", "rg_b_gate_x", "rg_lambda",
                   "ffn_conv_w"]
    small_lead = {n: weights[n].shape[:-1] for n in small_names}
    small_sizes = [weights[n].size for n in small_names]
    small_flat = jnp.concatenate([weights[n].reshape(-1) for n in small_names])
    small_rows = _pad_rows(small_flat, d, SUBLANES)
    small_g = _all_gather(small_rows, "gather_small").reshape(N_DEV, -1)
    small_full = {}
    o = 0
    for n, sz in zip(small_names, small_sizes):
        small_full[n] = _chunks_to_cols(small_g[:, o:o + sz], small_lead[n])
        o += sz

    wa = rg_w_gate_a[0].astype(MXU_DTYPE)
    wx = rg_w_gate_x[0].astype(MXU_DTYPE)
    rg_cw, rg_cb = small_full["rg_conv_w"][0], small_full["rg_conv_b"]
    rg_ba, rg_bx, rg_lam = small_full["rg_b_gate_a"], small_full["rg_b_gate_x"], small_full["rg_lambda"]
    sc_cw = small_full["sc_conv_w"][0]
    ffn_cw = small_full["ffn_conv_w"]

    h0 = jnp.concatenate([small_full["meta_tokens"], x[0]], axis=0)
    target = jnp.concatenate([jnp.zeros((n_meta, d), F32), loss_target[0]], axis=0)

    def ffn_fwd(h, l):
        n, a = _norm_matmul(h, norm_ffn_g[l:l + 1], ffn_w_up_f[l], f"ffn{l}_up")
        z = _ffn_mid_fwd(a, ffn_cw[l], f"ffn{l}_mid")
        return _matmul_residual(z, ffn_w_down_f[l], h, f"ffn{l}_down"), (n, a, z)

    n0, p0 = _norm_matmul(h0, norm_mix_g[0:1], sc_w_in_f, "sc_in")
    q0 = _sc_mid_fwd(p0, sc_cw, "sc_mid")
    h1 = _matmul_residual(q0, sc_w_out_f, h0, "sc_out")
    h2, ffn0_saved = ffn_fwd(h1, 0)
    n2, p2 = _norm_matmul(h2, norm_mix_g[1:2], rg_w_in_f, "rg_in")
    y2, hs2 = _rg_mid_fwd(p2, rg_cw, rg_cb, wa, rg_ba, wx, rg_bx, rg_lam, "rg_mid")
    h3 = _matmul_residual(y2, rg_w_out_f, h2, "rg_out")
    h4, ffn1_saved = ffn_fwd(h3, 1)

    dh4, loss_tile, d_final_g = _loss_head(h4, final_norm_g.reshape(1, d), target, n_meta, "loss_head")
    loss = lax.psum(loss_tile[0, 0], AXES)

    def ffn_bwd(dh_out, h_in, saved, l):
        n, a, z = saved
        dz, dhb = _dgrad_out(dh_out, ffn_w_down_t[l], f"ffn{l}_down_dgrad")
        d_w_down = _wgrad(z, dhb, f"ffn{l}_down_wgrad")
        da, d_cw = _ffn_mid_bwd(a, dz, ffn_cw[l], f"ffn{l}_mid_bwd")
        d_w_up_t = _wgrad(da, n, f"ffn{l}_up_wgrad")
        dh_in, d_g = _dgrad_in_norm(da, ffn_w_up_t[l], h_in, norm_ffn_g[l:l + 1], dh_out, f"ffn{l}_up_dgrad")
        return dh_in, d_w_up_t, d_cw, d_w_down, d_g

    dh3, d_up1, d_fcw1, d_down1, d_fg1 = ffn_bwd(dh4, h3, ffn1_saved, 1)

    dy2, dhb3 = _dgrad_out(dh3, rg_w_out_t, "rg_out_dgrad")
    d_rg_w_out = _wgrad(y2, dhb3, "rg_out_wgrad")
    dp2, d_rg_cw, d_rg_cb, d_wa, d_rg_ba, d_wx, d_rg_bx, d_rg_lam = _rg_mid_bwd(
        p2, dy2, hs2, rg_cw, rg_cb, wa, rg_ba, wx, rg_bx, rg_lam, "rg_mid_bwd")
    d_rg_w_in_t = _wgrad(dp2, n2, "rg_in_wgrad")
    dh2, d_mg1 = _dgrad_in_norm(dp2, rg_w_in_t, h2, norm_mix_g[1:2], dh3, "rg_in_dgrad")

    dh1, d_up0, d_fcw0, d_down0, d_fg0 = ffn_bwd(dh2, h1, ffn0_saved, 0)

    dq0, dhb1 = _dgrad_out(dh1, sc_w_out_t, "sc_out_dgrad")
    d_sc_w_out = _wgrad(q0, dhb1, "sc_out_wgrad")
    dp0, d_sc_cw = _sc_mid_bwd(p0, dq0, sc_cw, "sc_mid_bwd")
    d_sc_w_in_t = _wgrad(dp0, n0, "sc_in_wgrad")
    dh0, d_mg0 = _dgrad_in_norm(dp0, sc_w_in_t, h0, norm_mix_g[0:1], dh1, "sc_in_dgrad")

    grad_x = dh0[n_meta:][None]

    big_grads = [d_sc_w_in_t, d_sc_w_out, d_rg_w_in_t, d_rg_w_out, d_up0, d_up1, d_down0, d_down1]
    big_chunks = [g.reshape(N_DEV, r_, d) for g, r_ in zip(big_grads, big_rows)]
    small_grads = {"meta_tokens": dh0[:n_meta], "sc_conv_w": d_sc_cw[None], "rg_conv_w": d_rg_cw[None],
                   "rg_conv_b": d_rg_cb, "rg_b_gate_a": d_rg_ba, "rg_b_gate_x": d_rg_bx, "rg_lambda": d_rg_lam,
                   "ffn_conv_w": jnp.stack([d_fcw0, d_fcw1])}
    small_chunks = jnp.concatenate([_cols_to_chunks(small_grads[n]) for n in small_names], axis=1)
    pad = small_rows.shape[0] * d - small_chunks.shape[1]
    small_chunks = jnp.pad(small_chunks, ((0, 0), (0, pad))).reshape(N_DEV, small_rows.shape[0], d)
    rep_names = ["norm_mix_g", "norm_ffn_g", "final_norm_g", "rg_w_gate_a", "rg_w_gate_x"]
    rep_grads = {"norm_mix_g": jnp.concatenate([d_mg0, d_mg1], axis=0),
                 "norm_ffn_g": jnp.concatenate([d_fg0, d_fg1], axis=0),
                 "final_norm_g": d_final_g.reshape(-1), "rg_w_gate_a": d_wa[None], "rg_w_gate_x": d_wx[None]}
    rep_flat = jnp.concatenate([rep_grads[n].reshape(-1) for n in rep_names])
    rep_chunk_rows = -(-rep_flat.shape[0] // (N_DEV * d))
    rep_chunk_rows += -(sum(big_rows) + small_rows.shape[0] + rep_chunk_rows) % 64
    rep_chunks = jnp.pad(rep_flat, (0, N_DEV * rep_chunk_rows * d - rep_flat.shape[0])).reshape(N_DEV, rep_chunk_rows, d)

    rs_buf = jnp.concatenate(big_chunks + [small_chunks, rep_chunks], axis=1)
    reduced = _reduce_scatter(rs_buf, core)

    grads = {}
    o = 0
    big_names = ["sc_w_in", "sc_w_out", "rg_w_in", "rg_w_out"]
    for n, r_ in zip(big_names, big_rows[:4]):
        g = reduced[o:o + r_]
        grads[n] = (g.T if n.endswith("_in") else g)[None]
        o += r_
    up = [reduced[o + l * big_rows[4]:o + (l + 1) * big_rows[4]].T for l in range(n_ffn)]
    o += n_ffn * big_rows[4]
    down = [reduced[o + l * big_rows[4 + n_ffn]:o + (l + 1) * big_rows[4 + n_ffn]] for l in range(n_ffn)]
    o += n_ffn * big_rows[4 + n_ffn]
    grads["ffn_w_up"] = jnp.stack(up)
    grads["ffn_w_down"] = jnp.stack(down)
    small_red = reduced[o:o + small_rows.shape[0]].reshape(-1)
    o += small_rows.shape[0]
    so = 0
    for n, sz in zip(small_names, small_sizes):
        grads[n] = small_red[so:so + sz].reshape(weights[n].shape)
        so += sz
    rep_red = _all_gather(reduced[o:o + rep_chunks.shape[1]], "gather_replicated_grads").reshape(-1)
    ro = 0
    for n in rep_names:
        sz = weights[n].size
        grads[n] = rep_red[ro:ro + sz].reshape(weights[n].shape)
        ro += sz

    delta, new_m, new_v = {}, {}, {}
    for n in names:
        delta[n], new_m[n], new_v[n] = _adamw_nd(weights[n], grads[n], m_in[n], v_in[n], f"adamw_{n}")

    return (loss, grad_x, *[grads[n] for n in names], *[delta[n] for n in names],
            *[new_m[n] for n in names], *[new_v[n] for n in names])
```

```python
import jax
import jax.numpy as jnp
from jax import lax
from jax.experimental import pallas as pl
from jax.experimental.pallas import tpu as pltpu

F32 = jnp.float32
MXU_DTYPE = jnp.bfloat16
RMS_EPS = 1e-6
RG_C = 8.0
ADAM_LR = 0.001
ADAM_B1 = 0.9
ADAM_B2 = 0.999
ADAM_EPS = 1e-08
ADAM_WD = 0.01
ADAM_STEP = 10

N_DEV = 8
AXES = ("x", "y", "c")
SUBLANES = 8
LANES = 128
VMEM_LIMIT_BYTES = 48 * 1024 * 1024
ROW_TILE_MATMUL = 700
ROW_TILE_PERM = 400
STRIP = 256

_NT = (((1,), (1,)), ((), ()))
_TN = (((0,), (0,)), ((), ()))
_NN = (((1,), (0,)), ((), ()))


def _row_tile(t, target):
    best = None
    for tm in range(16, t + 1, 16):
        if t % tm == 0 and tm <= target:
            best = tm
    return best if best is not None else t


def _col_tile(n, target):
    best = None
    for tn in range(LANES, n + 1, LANES):
        if n % tn == 0 and tn <= target:
            best = tn
    return best if best is not None else n


def _params(*sem):
    return pltpu.CompilerParams(dimension_semantics=sem, vmem_limit_bytes=VMEM_LIMIT_BYTES)


def _dot(a, b, dims):
    return lax.dot_general(a, b, dims, preferred_element_type=F32)


def _sigmoid(x):
    return 1.0 / (1.0 + jnp.exp(-x))


def _gelu(x):
    c = 0.7978845608028654
    t = jnp.tanh(c * (x + 0.044715 * (x * x * x)))
    return 0.5 * x * (1.0 + t)


def _gelu_grad(x):
    c = 0.7978845608028654
    t = jnp.tanh(c * (x + 0.044715 * (x * x * x)))
    return 0.5 * (1.0 + t) + 0.5 * x * (1.0 - t * t) * c * (1.0 + 3.0 * 0.044715 * x * x)


def _softplus(x):
    return jnp.maximum(x, 0.0) + jnp.log1p(jnp.exp(-jnp.abs(x)))


def _to_tiles(a, tm):
    t, d = a.shape
    return a.reshape(t // tm, SUBLANES, tm // SUBLANES, d).transpose(0, 2, 1, 3).reshape(t, d)


def _from_tiles(a, tm):
    t, d = a.shape
    return a.reshape(t // tm, tm // SUBLANES, SUBLANES, d).transpose(0, 2, 1, 3).reshape(t, d)


def _rows(ref, sl):
    return [ref[k:k + 1, sl] for k in range(ref.shape[0])]


def _shift_down(x, before, s):
    if s == 0:
        return x
    n = x.shape[0]
    row = lax.broadcasted_iota(jnp.int32, (SUBLANES, x.shape[1]), 0)
    heads = []
    for g in range(s):
        v = x[n - (s - g) * SUBLANES:n - (s - g - 1) * SUBLANES]
        heads.append(pltpu.roll(jnp.where(row == SUBLANES - 1, before[s - g - 1], v), 1, axis=0))
    return jnp.concatenate(heads + [x[0:n - s * SUBLANES]], axis=0)


def _shift_up(x, after, s):
    if s == 0:
        return x
    row = lax.broadcasted_iota(jnp.int32, (SUBLANES, x.shape[1]), 0)
    tails = []
    for m in range(s):
        v = x[m * SUBLANES:(m + 1) * SUBLANES]
        tails.append(pltpu.roll(jnp.where(row == 0, after[m], v), SUBLANES - 1, axis=0))
    return jnp.concatenate([x[s * SUBLANES:]] + tails, axis=0)


def _weighted(w, windows):
    y = w[0] * windows[0]
    for k in range(1, len(w)):
        y = y + w[k] * windows[k]
    return y


def _windows(x, before, k_width):
    return [_shift_down(x, before, k_width - 1 - k) for k in range(k_width)]


def _conv_t(dy, after, w):
    k_width = len(w)
    return _weighted(w, [_shift_up(dy, after, k_width - 1 - k) for k in range(k_width)])


def _blocks(ref, sl, count, newest_first):
    n = ref.shape[0] // SUBLANES
    order = range(n - 1, n - 1 - count, -1) if newest_first else range(count)
    return [ref[b * SUBLANES:(b + 1) * SUBLANES, sl] for b in order]


def _halo_specs(tm, cols, count, tile_of):
    def spec(k):
        return pl.BlockSpec((SUBLANES, cols), lambda i: (jnp.maximum(tile_of(i) * (tm // SUBLANES) - k, 0), 0))
    return [spec(k) for k in range(1, count + 1)]


def _scan_tile(coef, val, out, carry, groups, reverse):
    cols = coef.shape[1]
    row = lax.broadcasted_iota(jnp.int32, (SUBLANES, cols), 0)

    def blk(i):
        g = groups - 1 - i if reverse else i
        return pl.ds(pl.multiple_of(g * SUBLANES, SUBLANES), SUBLANES)

    def local(i, pl_):
        p_prev, l_prev = pl_
        a = coef[blk(i), :]
        p = a * p_prev
        l = a * l_prev + val[blk(i), :]
        coef[blk(i), :] = p
        val[blk(i), :] = l
        return p, l

    pf, lf = lax.fori_loop(0, groups, local, (jnp.ones((SUBLANES, cols), F32), jnp.zeros((SUBLANES, cols), F32)))
    for s in (1, 2, 4):
        keep, sh = (row < SUBLANES - s, SUBLANES - s) if reverse else (row >= s, s)
        p_s = jnp.where(keep, pltpu.roll(pf, sh, axis=0), 1.0)
        l_s = jnp.where(keep, pltpu.roll(lf, sh, axis=0), 0.0)
        lf = pf * l_s + lf
        pf = pf * p_s
    end = lf + pf * carry
    if reverse:
        init = jnp.where(row == SUBLANES - 1, carry, pltpu.roll(end, SUBLANES - 1, axis=0))
        leaving = jnp.broadcast_to(end[0:1, :], (SUBLANES, cols))
    else:
        init = jnp.where(row == 0, carry, pltpu.roll(end, 1, axis=0))
        leaving = jnp.broadcast_to(end[SUBLANES - 1:SUBLANES, :], (SUBLANES, cols))

    def fix(i, _):
        out[blk(i), :] = val[blk(i), :] + coef[blk(i), :] * init
        return 0

    lax.fori_loop(0, groups, fix, 0)
    return leaving


def _resident(shape):
    return pl.BlockSpec(shape, lambda *_: (0,) * len(shape), pipeline_mode=pl.Buffered(1))


def _full(shape):
    return pl.BlockSpec(shape, lambda *_: (0,) * len(shape))


def _rmsnorm_to(h_ref, g_ref, n_ref):
    x = h_ref[...]
    ms = jnp.mean(x * x, axis=-1, keepdims=True)
    n_ref[...] = (x * lax.rsqrt(ms + RMS_EPS) * g_ref[...]).astype(n_ref.dtype)


def _matmul_residual(q, w, h, name):
    t, k = q.shape
    d = w.shape[1]
    tm = _row_tile(t, ROW_TILE_MATMUL)

    def body(q_ref, w_ref, h_ref, o_ref):
        o_ref[...] = h_ref[...] + _dot(q_ref[...], w_ref[...], _NN)

    return pl.pallas_call(
        body, name=name, grid=(t // tm,),
        in_specs=[pl.BlockSpec((tm, k), lambda i: (i, 0)),
                  _resident((k, d)),
                  pl.BlockSpec((tm, d), lambda i: (i, 0))],
        out_specs=pl.BlockSpec((tm, d), lambda i: (i, 0)),
        out_shape=jax.ShapeDtypeStruct((t, d), F32),
        compiler_params=_params("parallel"),
    )(q, w, h)


def _dgrad_in_norm(dp, wt, h, g, dh_next, name):
    t, n = dp.shape
    d = wt.shape[1]
    tm = _row_tile(t, ROW_TILE_MATMUL)

    def body(dp_ref, wt_ref, h_ref, g_ref, dhn_ref, dh_ref, dg_ref):
        i = pl.program_id(0)
        dn = _dot(dp_ref[...], wt_ref[...], _NN)
        x = h_ref[...]
        ms = jnp.mean(x * x, axis=-1, keepdims=True)
        r = lax.rsqrt(ms + RMS_EPS)
        xhat = x * r
        dng = dn * g_ref[...]
        c = jnp.mean(dng * xhat, axis=-1, keepdims=True)
        dh_ref[...] = dhn_ref[...] + r * (dng - xhat * c)
        part = jnp.sum(dn * xhat, axis=0, keepdims=True)

        @pl.when(i == 0)
        def _():
            dg_ref[...] = part

        @pl.when(i > 0)
        def _():
            dg_ref[...] += part

    return pl.pallas_call(
        body, name=name, grid=(t // tm,),
        in_specs=[pl.BlockSpec((tm, n), lambda i: (i, 0)),
                  _resident((n, d)),
                  pl.BlockSpec((tm, d), lambda i: (i, 0)),
                  pl.BlockSpec((1, d), lambda i: (0, 0)),
                  pl.BlockSpec((tm, d), lambda i: (i, 0))],
        out_specs=[pl.BlockSpec((tm, d), lambda i: (i, 0)),
                   pl.BlockSpec((1, d), lambda i: (0, 0))],
        out_shape=[jax.ShapeDtypeStruct((t, d), F32), jax.ShapeDtypeStruct((1, d), F32)],
        compiler_params=_params("arbitrary"),
    )(dp, wt, h, g, dh_next)


def _wgrad(a, b, name):
    t, m = a.shape
    d = b.shape[1]
    tmm = _col_tile(m, 1536)
    tk = _row_tile(t, ROW_TILE_MATMUL)

    def body(a_ref, b_ref, o_ref):
        @pl.when(pl.program_id(1) == 0)
        def _():
            o_ref[...] = jnp.zeros_like(o_ref)
        o_ref[...] += _dot(a_ref[...], b_ref[...], _TN)

    return pl.pallas_call(
        body, name=name, grid=(m // tmm, t // tk),
        in_specs=[pl.BlockSpec((tk, tmm), lambda i, k: (k, i)),
                  pl.BlockSpec((tk, d), lambda i, k: (k, 0))],
        out_specs=pl.BlockSpec((tmm, d), lambda i, k: (i, 0)),
        out_shape=jax.ShapeDtypeStruct((m, d), F32),
        compiler_params=_params("parallel", "arbitrary"),
    )(a, b)


def _ffn_up_fused(h, g, w, conv_w, tm, name):
    t, d = h.shape
    f2 = w.shape[1]
    f = f2 // 2
    cw = min(STRIP, f)
    kw = conv_w.shape[0]
    nh = kw - 1

    def body(h_ref, g_ref, w_ref, cw_ref, n_ref, a_ref, z_ref, carry):
        @pl.when(pl.program_id(0) == 0)
        def _():
            carry[...] = jnp.zeros_like(carry)

        _rmsnorm_to(h_ref, g_ref, n_ref)
        for c in range(0, f, cw):
            conv = []
            for sl in (slice(c, c + cw), slice(f + c, f + c + cw)):
                a = _dot(n_ref[...], w_ref[:, sl], _NN)
                a_ref[:, sl] = a
                conv.append(_weighted(_rows(cw_ref, sl), _windows(a, _blocks(carry, sl, nh, True), kw)))
                carry[:, sl] = a[tm - nh * SUBLANES:tm]
            gg, vv = conv
            z_ref[:, c:c + cw] = (gg * _sigmoid(gg) * vv).astype(z_ref.dtype)

    return pl.pallas_call(
        body, name=name, grid=(t // tm,),
        in_specs=[pl.BlockSpec((tm, d), lambda i: (i, 0)), pl.BlockSpec((1, d), lambda i: (0, 0)),
                  _resident((d, f2)), _full(conv_w.shape)],
        out_specs=[pl.BlockSpec((tm, d), lambda i: (i, 0)), pl.BlockSpec((tm, f2), lambda i: (i, 0)),
                   pl.BlockSpec((tm, f), lambda i: (i, 0))],
        out_shape=[jax.ShapeDtypeStruct((t, d), MXU_DTYPE), jax.ShapeDtypeStruct((t, f2), F32),
                   jax.ShapeDtypeStruct((t, f), MXU_DTYPE)],
        scratch_shapes=[pltpu.VMEM((nh * SUBLANES, f2), F32)],
        compiler_params=_params("arbitrary"),
    )(h, g, w, conv_w)


def _ffn_down_bwd_fused(dh, wdt, a, conv_w, tm, name):
    t, d = dh.shape
    f = wdt.shape[1]
    f2 = 2 * f
    nt = t // tm
    cw = min(STRIP, f)
    kw = conv_w.shape[0]
    nh = kw - 1
    rev = lambda i: nt - 1 - i

    def body(dh_ref, wdt_ref, a_ref, *rest):
        halo_refs = rest[:nh]
        cw_ref, da_ref, dhb_ref, dw_ref, carry = rest[nh:]
        i = pl.program_id(0)
        newest, oldest = i == 0, i == nt - 1

        @pl.when(newest)
        def _():
            carry[...] = jnp.zeros_like(carry)
            dw_ref[...] = jnp.zeros_like(dw_ref)

        dhb_ref[...] = dh_ref[...].astype(dhb_ref.dtype)
        for c in range(0, f, cw):
            gsl, vsl = slice(c, c + cw), slice(f + c, f + c + cw)
            dz = _dot(dhb_ref[...], wdt_ref[:, gsl], _NN)
            win, conv = {}, {}
            for sl in (gsl, vsl):
                before = [jnp.where(oldest, 0.0, r[:, sl]) for r in halo_refs]
                win[sl.start] = _windows(a_ref[:, sl], before, kw)
                conv[sl.start] = _weighted(_rows(cw_ref, sl), win[sl.start])
            gg, vv = conv[gsl.start], conv[vsl.start]
            s = _sigmoid(gg)
            grads = {gsl.start: dz * vv * s * (1.0 + gg * (1.0 - s)), vsl.start: dz * gg * s}
            for sl in (gsl, vsl):
                dconv = grads[sl.start]
                da_ref[:, sl] = _conv_t(dconv, _blocks(carry, sl, nh, False), _rows(cw_ref, sl)).astype(da_ref.dtype)
                for k in range(kw):
                    dw_ref[k:k + 1, sl] += jnp.sum(dconv * win[sl.start][k], axis=0, keepdims=True)
                carry[:, sl] = dconv[0:nh * SUBLANES]

    return pl.pallas_call(
        body, name=name, grid=(nt,),
        in_specs=[pl.BlockSpec((tm, d), lambda i: (rev(i), 0)), _resident((d, f)),
                  pl.BlockSpec((tm, f2), lambda i: (rev(i), 0))] + _halo_specs(tm, f2, nh, rev) + [_full(conv_w.shape)],
        out_specs=[pl.BlockSpec((tm, f2), lambda i: (rev(i), 0)), pl.BlockSpec((tm, d), lambda i: (rev(i), 0)),
                   _full(conv_w.shape)],
        out_shape=[jax.ShapeDtypeStruct((t, f2), MXU_DTYPE), jax.ShapeDtypeStruct((t, d), MXU_DTYPE),
                   jax.ShapeDtypeStruct(conv_w.shape, F32)],
        scratch_shapes=[pltpu.VMEM((nh * SUBLANES, f2), F32)],
        compiler_params=_params("arbitrary"),
    )(dh, wdt, a, *([a] * nh), conv_w)


def _sc_in_fused(h, g, w, conv_w, tm, name):
    t, d = h.shape
    cw = min(STRIP, d)
    kw = conv_w.shape[0]
    nh = kw - 1

    def body(h_ref, g_ref, w_ref, cw_ref, n_ref, p_ref, q_ref, carry):
        @pl.when(pl.program_id(0) == 0)
        def _():
            carry[...] = jnp.zeros_like(carry)

        _rmsnorm_to(h_ref, g_ref, n_ref)
        for c in range(0, d, cw):
            sl = slice(c, c + cw)
            parts = []
            for base in (0, d, 2 * d):
                psl = slice(base + c, base + c + cw)
                parts.append(_dot(n_ref[...], w_ref[:, psl], _NN))
                p_ref[:, psl] = parts[-1]
            bg, cg, v = parts
            cv = cg * v
            u = _weighted(_rows(cw_ref, sl), _windows(cv, _blocks(carry, sl, nh, True), kw))
            carry[:, sl] = cv[tm - nh * SUBLANES:tm]
            q_ref[:, sl] = (bg * u).astype(q_ref.dtype)

    return pl.pallas_call(
        body, name=name, grid=(t // tm,),
        in_specs=[pl.BlockSpec((tm, d), lambda i: (i, 0)), pl.BlockSpec((1, d), lambda i: (0, 0)),
                  _resident((d, 3 * d)), _full(conv_w.shape)],
        out_specs=[pl.BlockSpec((tm, d), lambda i: (i, 0)), pl.BlockSpec((tm, 3 * d), lambda i: (i, 0)),
                   pl.BlockSpec((tm, d), lambda i: (i, 0))],
        out_shape=[jax.ShapeDtypeStruct((t, d), MXU_DTYPE), jax.ShapeDtypeStruct((t, 3 * d), F32),
                   jax.ShapeDtypeStruct((t, d), MXU_DTYPE)],
        scratch_shapes=[pltpu.VMEM((nh * SUBLANES, d), F32)],
        compiler_params=_params("arbitrary"),
    )(h, g, w, conv_w)


def _sc_out_bwd_fused(dh, wot, p, conv_w, tm, name):
    t, d = dh.shape
    nt = t // tm
    cw = min(STRIP, d)
    kw = conv_w.shape[0]
    nh = kw - 1
    rev = lambda i: nt - 1 - i

    def body(dh_ref, wot_ref, p_ref, *rest):
        halo_refs = rest[:nh]
        cw_ref, dp_ref, dhb_ref, dw_ref, carry = rest[nh:]
        i = pl.program_id(0)
        newest, oldest = i == 0, i == nt - 1

        @pl.when(newest)
        def _():
            carry[...] = jnp.zeros_like(carry)
            dw_ref[...] = jnp.zeros_like(dw_ref)

        dhb_ref[...] = dh_ref[...].astype(dhb_ref.dtype)
        for c in range(0, d, cw):
            sl, csl, vsl = slice(c, c + cw), slice(d + c, d + c + cw), slice(2 * d + c, 2 * d + c + cw)
            w = _rows(cw_ref, sl)
            dq = _dot(dhb_ref[...], wot_ref[:, sl], _NN)
            bg, cg, v = p_ref[:, sl], p_ref[:, csl], p_ref[:, vsl]
            before = [jnp.where(oldest, 0.0, r[:, csl] * r[:, vsl]) for r in halo_refs]
            win = _windows(cg * v, before, kw)
            du = dq * bg
            dcv = _conv_t(du, _blocks(carry, sl, nh, False), w)
            dp_ref[:, sl] = (dq * _weighted(w, win)).astype(dp_ref.dtype)
            dp_ref[:, csl] = (dcv * v).astype(dp_ref.dtype)
            dp_ref[:, vsl] = (dcv * cg).astype(dp_ref.dtype)
            for k in range(kw):
                dw_ref[k:k + 1, sl] += jnp.sum(du * win[k], axis=0, keepdims=True)
            carry[:, sl] = du[0:nh * SUBLANES]

    return pl.pallas_call(
        body, name=name, grid=(nt,),
        in_specs=[pl.BlockSpec((tm, d), lambda i: (rev(i), 0)), _resident((d, d)),
                  pl.BlockSpec((tm, 3 * d), lambda i: (rev(i), 0))] + _halo_specs(tm, 3 * d, nh, rev)
                 + [_full(conv_w.shape)],
        out_specs=[pl.BlockSpec((tm, 3 * d), lambda i: (rev(i), 0)), pl.BlockSpec((tm, d), lambda i: (rev(i), 0)),
                   _full(conv_w.shape)],
        out_shape=[jax.ShapeDtypeStruct((t, 3 * d), MXU_DTYPE), jax.ShapeDtypeStruct((t, d), MXU_DTYPE),
                   jax.ShapeDtypeStruct(conv_w.shape, F32)],
        scratch_shapes=[pltpu.VMEM((nh * SUBLANES, d), F32)],
        compiler_params=_params("arbitrary"),
    )(dh, wot, p, *([p] * nh), conv_w)


def _rg_gates(u, k, wa_ref, ba, wx_ref, bx, lam):
    ub = u.astype(MXU_DTYPE)
    r = _sigmoid(_dot(ub, wa_ref[k], _NN) + ba)
    ig = _sigmoid(_dot(ub, wx_ref[k], _NN) + bx)
    sp = _softplus(-lam)
    la = -RG_C * r * sp
    a = jnp.exp(la)
    th = jnp.tanh(la)
    mult = jnp.sqrt(-2.0 * th / (1.0 - th))
    return r, ig, a, mult


def _rg_in_fused(h, g, w, conv_w, conv_b, wa, ba, wx, bx, lam, tm, name):
    t, d = h.shape
    r2 = w.shape[1]
    rr = r2 // 2
    nb, bd, _ = wa.shape
    kw = conv_w.shape[0]
    nh = kw - 1
    groups = tm // SUBLANES

    def body(h_ref, g_ref, w_ref, cw_ref, cb_ref, wa_ref, ba_ref, wx_ref, bx_ref, lam_ref,
             n_ref, p_ref, y_ref, hs_ref, a_scr, b_scr, carry_rb, carry_h):
        @pl.when(pl.program_id(0) == 0)
        def _():
            carry_rb[...] = jnp.zeros_like(carry_rb)
            carry_h[...] = jnp.zeros_like(carry_h)

        _rmsnorm_to(h_ref, g_ref, n_ref)
        for k in range(nb):
            sl = slice(k * bd, (k + 1) * bd)
            rsl = slice(rr + k * bd, rr + (k + 1) * bd)
            p_ref[:, sl] = _dot(n_ref[...], w_ref[:, sl], _NN)
            rb = _dot(n_ref[...], w_ref[:, rsl], _NN)
            p_ref[:, rsl] = rb
            u = _weighted(_rows(cw_ref, sl), _windows(rb, _blocks(carry_rb, sl, nh, True), kw)) + cb_ref[:, sl]
            carry_rb[:, sl] = rb[tm - nh * SUBLANES:tm]
            _, ig, a, mult = _rg_gates(u, k, wa_ref, ba_ref[:, sl], wx_ref, bx_ref[:, sl], lam_ref[:, sl])
            a_scr[:, sl] = a
            b_scr[:, sl] = mult * (ig * u)

        carry_h[...] = _scan_tile(a_scr, b_scr, hs_ref, carry_h[...], groups, reverse=False)

        for k in range(nb):
            sl = slice(k * bd, (k + 1) * bd)
            y_ref[:, sl] = (hs_ref[:, sl] * _gelu(p_ref[:, sl])).astype(y_ref.dtype)

    vm = lambda rows: pltpu.VMEM((rows, rr), F32)
    return pl.pallas_call(
        body, name=name, grid=(t // tm,),
        in_specs=[pl.BlockSpec((tm, d), lambda i: (i, 0)), pl.BlockSpec((1, d), lambda i: (0, 0)),
                  _resident((d, r2)), _full(conv_w.shape), _full(conv_b.shape), _full(wa.shape), _full(ba.shape),
                  _full(wx.shape), _full(bx.shape), _full(lam.shape)],
        out_specs=[pl.BlockSpec((tm, d), lambda i: (i, 0)), pl.BlockSpec((tm, r2), lambda i: (i, 0)),
                   pl.BlockSpec((tm, rr), lambda i: (i, 0)), pl.BlockSpec((tm, rr), lambda i: (i, 0))],
        out_shape=[jax.ShapeDtypeStruct((t, d), MXU_DTYPE), jax.ShapeDtypeStruct((t, r2), F32),
                   jax.ShapeDtypeStruct((t, rr), MXU_DTYPE), jax.ShapeDtypeStruct((t, rr), F32)],
        scratch_shapes=[vm(tm), vm(tm), vm(nh * SUBLANES), vm(SUBLANES)],
        compiler_params=_params("arbitrary"),
    )(h, g, w, conv_w, conv_b, wa, ba, wx, bx, lam)


def _rg_out_bwd_fused(dh, wot, p, hs, conv_w, conv_b, wa, ba, wx, bx, lam, tm, name):
    t, d = dh.shape
    r2 = p.shape[1]
    rr = r2 // 2
    nb, bd, _ = wa.shape
    nt = t // tm
    kw = conv_w.shape[0]
    nh = kw - 1
    groups = tm // SUBLANES
    rev = lambda i: nt - 1 - i

    def body(dh_ref, wot_ref, p_ref, hs_ref, hsh_ref, *rest):
        halo_refs = rest[:nh]
        (cw_ref, cb_ref, wa_ref, ba_ref, wx_ref, bx_ref, lam_ref,
         dp_ref, dhb_ref, dcw_ref, dcb_ref, dwa_ref, dba_ref, dwx_ref, dbx_ref, dlam_ref,
         a_scr, as_scr, u_scr, r_scr, ig_scr, mult_scr, g_scr, carry_g, carry_du) = rest[nh:]
        i = pl.program_id(0)
        newest, oldest = i == 0, i == nt - 1

        @pl.when(newest)
        def _():
            carry_g[...] = jnp.zeros_like(carry_g)
            carry_du[...] = jnp.zeros_like(carry_du)
            for ref in (dcw_ref, dcb_ref, dwa_ref, dba_ref, dwx_ref, dbx_ref, dlam_ref):
                ref[...] = jnp.zeros_like(ref)

        dhb_ref[...] = dh_ref[...].astype(dhb_ref.dtype)
        ones = jnp.ones((SUBLANES, bd), F32)

        def conv_windows(sl, rsl):
            before = [jnp.where(oldest, 0.0, r[:, rsl]) for r in halo_refs]
            return _windows(p_ref[:, rsl], before, kw)

        for k in range(nb):
            sl = slice(k * bd, (k + 1) * bd)
            rsl = slice(rr + k * bd, rr + (k + 1) * bd)
            dy = _dot(dhb_ref[...], wot_ref[:, sl], _NN)
            u = _weighted(_rows(cw_ref, sl), conv_windows(sl, rsl)) + cb_ref[:, sl]
            r, ig, a, mult = _rg_gates(u, k, wa_ref, ba_ref[:, sl], wx_ref, bx_ref[:, sl], lam_ref[:, sl])
            gb = p_ref[:, sl]
            g_scr[:, sl] = dy * _gelu(gb)
            dp_ref[:, sl] = (dy * hs_ref[:, sl] * _gelu_grad(gb)).astype(dp_ref.dtype)
            a_scr[:, sl] = a
            as_scr[:, sl] = _shift_up(a, [ones], 1)
            u_scr[:, sl] = u
            r_scr[:, sl] = r
            ig_scr[:, sl] = ig
            mult_scr[:, sl] = mult

        g_first = _scan_tile(as_scr, g_scr, g_scr, carry_g[...], groups, reverse=True)
        carry_g[...] = jnp.broadcast_to(a_scr[0:1, :], (SUBLANES, rr)) * g_first

        for k in range(nb):
            sl = slice(k * bd, (k + 1) * bd)
            rsl = slice(rr + k * bd, rr + (k + 1) * bd)
            cw = _rows(cw_ref, sl)
            lam_k = lam_ref[:, sl]
            sp = _softplus(-lam_k)
            g = g_scr[:, sl]
            a, u, r, ig, mult = a_scr[:, sl], u_scr[:, sl], r_scr[:, sl], ig_scr[:, sl], mult_scr[:, sl]
            da = g * _shift_down(hs_ref[:, sl], [jnp.where(oldest, 0.0, hsh_ref[:, sl])], 1)
            dmult = g * (ig * u)
            d_iu = g * mult
            dla = da * a - dmult * (a * a) / mult
            dr = dla * (-RG_C * sp)
            dsp = jnp.sum(dla * (-RG_C * r), axis=0, keepdims=True)
            dlam_ref[:, sl] += dsp * (-_sigmoid(-lam_k))
            dza = dr * r * (1.0 - r)
            dzx = (d_iu * u) * ig * (1.0 - ig)
            dba_ref[:, sl] += jnp.sum(dza, axis=0, keepdims=True)
            dbx_ref[:, sl] += jnp.sum(dzx, axis=0, keepdims=True)
            ub = u.astype(MXU_DTYPE)
            dzab, dzxb = dza.astype(MXU_DTYPE), dzx.astype(MXU_DTYPE)
            dwa_ref[k] += _dot(ub, dzab, _TN)
            dwx_ref[k] += _dot(ub, dzxb, _TN)
            du = d_iu * ig + _dot(dzab, wa_ref[k], _NT) + _dot(dzxb, wx_ref[k], _NT)
            dcb_ref[:, sl] += jnp.sum(du, axis=0, keepdims=True)
            win = conv_windows(sl, rsl)
            for kk in range(kw):
                dcw_ref[kk:kk + 1, sl] += jnp.sum(du * win[kk], axis=0, keepdims=True)
            dp_ref[:, rsl] = _conv_t(du, _blocks(carry_du, sl, nh, False), cw).astype(dp_ref.dtype)
            carry_du[:, sl] = du[0:nh * SUBLANES]

    tile = lambda cols: pl.BlockSpec((tm, cols), lambda i: (rev(i), 0))
    vm = lambda rows: pltpu.VMEM((rows, rr), F32)
    grads = [conv_w.shape, conv_b.shape, wa.shape, ba.shape, wx.shape, bx.shape, lam.shape]
    return pl.pallas_call(
        body, name=name, grid=(nt,),
        in_specs=[tile(d), _resident((d, rr)), tile(r2), tile(rr)] + _halo_specs(tm, rr, 1, rev)
                 + _halo_specs(tm, r2, nh, rev)
                 + [_full(conv_w.shape), _full(conv_b.shape), _full(wa.shape), _full(ba.shape),
                    _full(wx.shape), _full(bx.shape), _full(lam.shape)],
        out_specs=[tile(r2), tile(d)] + [_full(s) for s in grads],
        out_shape=[jax.ShapeDtypeStruct((t, r2), MXU_DTYPE), jax.ShapeDtypeStruct((t, d), MXU_DTYPE)]
                  + [jax.ShapeDtypeStruct(s, F32) for s in grads],
        scratch_shapes=[vm(tm)] * 7 + [vm(SUBLANES), vm(nh * SUBLANES)],
        compiler_params=_params("arbitrary"),
    )(dh, wot, p, hs, hs, *([p] * nh), conv_w, conv_b, wa, ba, wx, bx, lam)


def _loss_head(h, g, target, n_meta, tm, name):
    t, d = h.shape
    groups = tm // SUBLANES

    def body(h_ref, g_ref, t_ref, dh_ref, loss_ref, dg_ref):
        i = pl.program_id(0)
        x = h_ref[...]
        ms = jnp.mean(x * x, axis=-1, keepdims=True)
        r = lax.rsqrt(ms + RMS_EPS)
        xhat = x * r
        gg = g_ref[...]
        row = lax.broadcasted_iota(jnp.int32, (tm, 1), 0)
        time = i * tm + jnp.right_shift(row, 3) + jnp.bitwise_and(row, SUBLANES - 1) * groups
        err = jnp.where(time >= n_meta, xhat * gg - t_ref[...], 0.0)
        dout = err * (1.0 / d)
        dng = dout * gg
        c = jnp.mean(dng * xhat, axis=-1, keepdims=True)
        dh_ref[...] = r * (dng - xhat * c)
        part_dg = jnp.sum(dout * xhat, axis=0, keepdims=True)
        part_loss = jnp.broadcast_to(0.5 * jnp.sum(err * dout, keepdims=True), loss_ref.shape)

        @pl.when(i == 0)
        def _():
            dg_ref[...] = part_dg
            loss_ref[...] = part_loss

        @pl.when(i > 0)
        def _():
            dg_ref[...] += part_dg
            loss_ref[...] += part_loss

    return pl.pallas_call(
        body, name=name, grid=(t // tm,),
        in_specs=[pl.BlockSpec((tm, d), lambda i: (i, 0)),
                  pl.BlockSpec((1, d), lambda i: (0, 0)),
                  pl.BlockSpec((tm, d), lambda i: (i, 0))],
        out_specs=[pl.BlockSpec((tm, d), lambda i: (i, 0)),
                   pl.BlockSpec((SUBLANES, LANES), lambda i: (0, 0)),
                   pl.BlockSpec((1, d), lambda i: (0, 0))],
        out_shape=[jax.ShapeDtypeStruct((t, d), F32), jax.ShapeDtypeStruct((SUBLANES, LANES), F32),
                   jax.ShapeDtypeStruct((1, d), F32)],
        compiler_params=_params("arbitrary"),
    )(h, g, target)


def _adamw(w, g, m, v, name):
    rows, cols = w.shape
    tr = rows
    if rows > 512:
        for cand in range(8, 513, 8):
            if rows % cand == 0:
                tr = cand

    def body(w_ref, g_ref, m_ref, v_ref, d_ref, nm_ref, nv_ref):
        g_ = g_ref[...]
        m_ = ADAM_B1 * m_ref[...] + (1.0 - ADAM_B1) * g_
        v_ = ADAM_B2 * v_ref[...] + (1.0 - ADAM_B2) * (g_ * g_)
        m_hat = m_ / (1.0 - ADAM_B1 ** ADAM_STEP)
        v_hat = v_ / (1.0 - ADAM_B2 ** ADAM_STEP)
        d_ref[...] = -ADAM_LR * (m_hat / (jnp.sqrt(v_hat) + ADAM_EPS) + ADAM_WD * w_ref[...])
        nm_ref[...] = m_
        nv_ref[...] = v_

    spec = pl.BlockSpec((tr, cols), lambda i: (i, 0))
    shape = jax.ShapeDtypeStruct((rows, cols), F32)
    return pl.pallas_call(
        body, name=name, grid=(rows // tr,),
        in_specs=[spec] * 4, out_specs=[spec] * 3, out_shape=[shape] * 3,
        compiler_params=_params("parallel"),
    )(w, g, m, v)


def _adamw_nd(w, g, m, v, name):
    shape = w.shape
    two_d = (-1, shape[-1]) if w.ndim > 1 else (1, -1)
    outs = _adamw(w.reshape(two_d), g.reshape(two_d), m.reshape(two_d), v.reshape(two_d), name)
    return tuple(o.reshape(shape) for o in outs)


_ANY = pl.BlockSpec(memory_space=pl.ANY)
_MESH = pl.DeviceIdType.MESH


def _all_gather(blk, name):
    rows, cols = blk.shape

    def body(x_ref, out_ref, send_sems, recv_sems, local_sem):
        x, y, c = lax.axis_index("x"), lax.axis_index("y"), lax.axis_index("c")
        me, sibling = (x, y, c), (x, y, 1 - c)
        chips = [(1 - x, y), (x, 1 - y), (1 - x, 1 - y)]

        def slot(px, py, pc):
            return out_ref.at[4 * px + 2 * py + pc]

        def copy(k, block, to, src=None):
            return pltpu.make_async_remote_copy(
                src_ref=slot(*block) if src is None else src, dst_ref=slot(*block),
                send_sem=send_sems.at[k], recv_sem=recv_sems.at[k], device_id=to, device_id_type=_MESH)

        mine = pltpu.make_async_copy(x_ref, slot(*me), local_sem)
        mine.start()
        first = [copy(0, me, sibling, src=x_ref)]
        first += [copy(1 + j, me, (*chip, c), src=x_ref) for j, chip in enumerate(chips)]
        for cp in first:
            cp.start()
        passed = [copy(4 + j, (*chip, c), sibling) for j, chip in enumerate(chips)]
        for j, chip in enumerate(chips):
            copy(1 + j, (*chip, c), me).wait_recv()
            passed[j].start()
        copy(0, sibling, me).wait_recv()
        for j, chip in enumerate(chips):
            copy(4 + j, (*chip, 1 - c), me).wait_recv()
        for cp in first + passed:
            cp.wait_send()
        mine.wait()

    return pl.pallas_call(
        body, name=name,
        in_specs=[_ANY], out_specs=_ANY,
        out_shape=jax.ShapeDtypeStruct((N_DEV, rows, cols), blk.dtype),
        scratch_shapes=[pltpu.SemaphoreType.DMA((7,)), pltpu.SemaphoreType.DMA((7,)), pltpu.SemaphoreType.DMA(())],
    )(blk)


def _rs_sibling_exchange(buf, name):
    _, rows, cols = buf.shape

    def body(buf_ref, out_ref, send_sems, recv_sems):
        x, y, c = lax.axis_index("x"), lax.axis_index("y"), lax.axis_index("c")
        copies = [pltpu.make_async_remote_copy(
            src_ref=buf_ref.at[2 * k + (1 - c)], dst_ref=out_ref.at[k],
            send_sem=send_sems.at[k], recv_sem=recv_sems.at[k],
            device_id=(x, y, 1 - c), device_id_type=_MESH) for k in range(4)]
        for cp in copies:
            cp.start()
        for cp in copies:
            cp.wait()

    return pl.pallas_call(
        body, name=name,
        in_specs=[_ANY], out_specs=_ANY,
        out_shape=jax.ShapeDtypeStruct((4, rows, cols), buf.dtype),
        scratch_shapes=[pltpu.SemaphoreType.DMA((4,)), pltpu.SemaphoreType.DMA((4,))],
    )(buf)


def _rs_chip_sum(buf, recv, core, name):
    _, rows, cols = buf.shape
    tr = _row_tile(rows, 512) if rows % 16 == 0 else rows

    def body(c_ref, a_ref, b_ref, o_ref):
        o_ref[...] = a_ref[...] + b_ref[...]

    grid_spec = pltpu.PrefetchScalarGridSpec(
        num_scalar_prefetch=1, grid=(4, rows // tr),
        in_specs=[pl.BlockSpec((None, tr, cols), lambda k, j, c: (2 * k + c[0], j, 0)),
                  pl.BlockSpec((None, tr, cols), lambda k, j, c: (k, j, 0))],
        out_specs=pl.BlockSpec((None, tr, cols), lambda k, j, c: (k, j, 0)))
    return pl.pallas_call(
        body, name=name, grid_spec=grid_spec,
        out_shape=jax.ShapeDtypeStruct((4, rows, cols), buf.dtype),
        compiler_params=_params("parallel", "parallel"),
    )(core, buf, recv)


def _rs_chip_exchange(chip_sum, name):
    _, rows, cols = chip_sum.shape

    def body(cs_ref, out_ref, send_sems, recv_sems, local_sem):
        x, y, c = lax.axis_index("x"), lax.axis_index("y"), lax.axis_index("c")
        my_chip = 2 * x + y
        mine = pltpu.make_async_copy(cs_ref.at[my_chip], out_ref.at[my_chip], local_sem)
        mine.start()
        chips = [(1 - x, y), (x, 1 - y), (1 - x, 1 - y)]
        copies = [pltpu.make_async_remote_copy(
            src_ref=cs_ref.at[2 * px + py], dst_ref=out_ref.at[my_chip],
            send_sem=send_sems.at[j], recv_sem=recv_sems.at[j],
            device_id=(px, py, c), device_id_type=_MESH) for j, (px, py) in enumerate(chips)]
        for cp in copies:
            cp.start()
        for cp in copies:
            cp.wait()
        mine.wait()

    return pl.pallas_call(
        body, name=name,
        in_specs=[_ANY], out_specs=_ANY,
        out_shape=jax.ShapeDtypeStruct((4, rows, cols), chip_sum.dtype),
        scratch_shapes=[pltpu.SemaphoreType.DMA((3,)), pltpu.SemaphoreType.DMA((3,)), pltpu.SemaphoreType.DMA(())],
    )(chip_sum)


def _sum_slots(parts, name):
    _, rows, cols = parts.shape
    tr = _row_tile(rows, 512) if rows % 16 == 0 else rows

    def body(p_ref, o_ref):
        o_ref[...] = ((p_ref[0] + p_ref[1]) + p_ref[2]) + p_ref[3]

    return pl.pallas_call(
        body, name=name, grid=(rows // tr,),
        in_specs=[pl.BlockSpec((4, tr, cols), lambda i: (0, i, 0))],
        out_specs=pl.BlockSpec((tr, cols), lambda i: (i, 0)),
        out_shape=jax.ShapeDtypeStruct((rows, cols), parts.dtype),
        compiler_params=_params("parallel"),
    )(parts)


def _reduce_scatter(buf, core):
    recv = _rs_sibling_exchange(buf, "rs_sibling_exchange")
    chip_sum = _rs_chip_sum(buf, recv, core, "rs_chip_sum")
    parts = _rs_chip_exchange(chip_sum, "rs_chip_exchange")
    return _sum_slots(parts, "rs_sum_slots")


def _pad_rows(flat, cols, multiple):
    n = flat.shape[0]
    rows = -(-n // cols)
    rows = -(-rows // multiple) * multiple
    return jnp.pad(flat, (0, rows * cols - n)).reshape(rows, cols)


def _cols_to_chunks(full):
    lead = full.shape[:-1]
    c = full.shape[-1] // N_DEV
    x = full.reshape(-1, N_DEV, c)
    return jnp.transpose(x, (1, 0, 2)).reshape(N_DEV, -1)


def _chunks_to_cols(chunks, lead):
    n = 1
    for s in lead:
        n *= s
    c = chunks.shape[1] // n
    x = chunks.reshape(N_DEV, n, c)
    return jnp.transpose(x, (1, 0, 2)).reshape(tuple(lead) + (N_DEV * c,))


def kernel(x, meta_tokens, norm_mix_g, norm_ffn_g, final_norm_g, sc_w_in, sc_conv_w, sc_w_out, rg_w_in, rg_conv_w, rg_conv_b, rg_w_gate_a, rg_b_gate_a, rg_w_gate_x, rg_b_gate_x, rg_lambda, rg_w_out, ffn_w_up, ffn_conv_w, ffn_w_down, loss_target, m_meta_tokens, m_norm_mix_g, m_norm_ffn_g, m_final_norm_g, m_sc_w_in, m_sc_conv_w, m_sc_w_out, m_rg_w_in, m_rg_conv_w, m_rg_conv_b, m_rg_w_gate_a, m_rg_b_gate_a, m_rg_w_gate_x, m_rg_b_gate_x, m_rg_lambda, m_rg_w_out, m_ffn_w_up, m_ffn_conv_w, m_ffn_w_down, v_meta_tokens, v_norm_mix_g, v_norm_ffn_g, v_final_norm_g, v_sc_w_in, v_sc_conv_w, v_sc_w_out, v_rg_w_in, v_rg_conv_w, v_rg_conv_b, v_rg_w_gate_a, v_rg_b_gate_a, v_rg_w_gate_x, v_rg_b_gate_x, v_rg_lambda, v_rg_w_out, v_ffn_w_up, v_ffn_conv_w, v_ffn_w_down):
    weights = dict(meta_tokens=meta_tokens, norm_mix_g=norm_mix_g, norm_ffn_g=norm_ffn_g, final_norm_g=final_norm_g,
                   sc_w_in=sc_w_in, sc_conv_w=sc_conv_w, sc_w_out=sc_w_out, rg_w_in=rg_w_in, rg_conv_w=rg_conv_w,
                   rg_conv_b=rg_conv_b, rg_w_gate_a=rg_w_gate_a, rg_b_gate_a=rg_b_gate_a, rg_w_gate_x=rg_w_gate_x,
                   rg_b_gate_x=rg_b_gate_x, rg_lambda=rg_lambda, rg_w_out=rg_w_out, ffn_w_up=ffn_w_up,
                   ffn_conv_w=ffn_conv_w, ffn_w_down=ffn_w_down)
    m_in = dict(meta_tokens=m_meta_tokens, norm_mix_g=m_norm_mix_g, norm_ffn_g=m_norm_ffn_g, final_norm_g=m_final_norm_g,
                sc_w_in=m_sc_w_in, sc_conv_w=m_sc_conv_w, sc_w_out=m_sc_w_out, rg_w_in=m_rg_w_in, rg_conv_w=m_rg_conv_w,
                rg_conv_b=m_rg_conv_b, rg_w_gate_a=m_rg_w_gate_a, rg_b_gate_a=m_rg_b_gate_a, rg_w_gate_x=m_rg_w_gate_x,
                rg_b_gate_x=m_rg_b_gate_x, rg_lambda=m_rg_lambda, rg_w_out=m_rg_w_out, ffn_w_up=m_ffn_w_up,
                ffn_conv_w=m_ffn_conv_w, ffn_w_down=m_ffn_w_down)
    v_in = dict(meta_tokens=v_meta_tokens, norm_mix_g=v_norm_mix_g, norm_ffn_g=v_norm_ffn_g, final_norm_g=v_final_norm_g,
                sc_w_in=v_sc_w_in, sc_conv_w=v_sc_conv_w, sc_w_out=v_sc_w_out, rg_w_in=v_rg_w_in, rg_conv_w=v_rg_conv_w,
                rg_conv_b=v_rg_conv_b, rg_w_gate_a=v_rg_w_gate_a, rg_b_gate_a=v_rg_b_gate_a, rg_w_gate_x=v_rg_w_gate_x,
                rg_b_gate_x=v_rg_b_gate_x, rg_lambda=v_rg_lambda, rg_w_out=v_rg_w_out, ffn_w_up=v_ffn_w_up,
                ffn_conv_w=v_ffn_conv_w, ffn_w_down=v_ffn_w_down)
    names = list(weights)

    seq, d = x.shape[1], x.shape[2]
    n_meta = meta_tokens.shape[0]
    n_ffn = ffn_w_up.shape[0]
    rr = rg_w_gate_a.shape[1] * rg_w_gate_a.shape[2]
    core = lax.axis_index("c").astype(jnp.int32).reshape(1)

    big_parts = [sc_w_in[0].T, sc_w_out[0], rg_w_in[0].T, rg_w_out[0]]
    big_parts += [ffn_w_up[l].T for l in range(n_ffn)] + [ffn_w_down[l] for l in range(n_ffn)]
    big_rows = [p.shape[0] for p in big_parts]
    big = _all_gather(jnp.concatenate(big_parts, axis=0).astype(MXU_DTYPE), "gather_matrices")
    offs = [0]
    for r_ in big_rows:
        offs.append(offs[-1] + r_)
    full_big = [big[:, offs[k]:offs[k + 1], :].reshape(N_DEV * big_rows[k], d) for k in range(len(big_parts))]
    sc_w_in_t, sc_w_out_f, rg_w_in_t, rg_w_out_f = full_big[:4]
    ffn_w_up_t = full_big[4:4 + n_ffn]
    ffn_w_down_f = full_big[4 + n_ffn:4 + 2 * n_ffn]
    sc_w_in_f, rg_w_in_f, ffn_w_up_f = sc_w_in_t.T, rg_w_in_t.T, [w.T for w in ffn_w_up_t]
    sc_w_out_t, rg_w_out_t, ffn_w_down_t = sc_w_out_f.T, rg_w_out_f.T, [w.T for w in ffn_w_down_f]

    small_names = ["meta_tokens", "sc_conv_w", "rg_conv_w", "rg_conv_b", "rg_b_gate_a", "rg_b_gate_x", "rg_lambda",
                   "ffn_conv_w"]
    small_lead = {n: weights[n].shape[:-1] for n in small_names}
    small_sizes = [weights[n].size for n in small_names]
    small_flat = jnp.concatenate([weights[n].reshape(-1) for n in small_names])
    small_rows = _pad_rows(small_flat, d, SUBLANES)
    small_g = _all_gather(small_rows, "gather_small").reshape(N_DEV, -1)
    small_full = {}
    o = 0
    for n, sz in zip(small_names, small_sizes):
        small_full[n] = _chunks_to_cols(small_g[:, o:o + sz], small_lead[n])
        o += sz

    wa = rg_w_gate_a[0].astype(MXU_DTYPE)
    wx = rg_w_gate_x[0].astype(MXU_DTYPE)
    rg_cw, rg_cb = small_full["rg_conv_w"][0], small_full["rg_conv_b"]
    rg_ba, rg_bx, rg_lam = small_full["rg_b_gate_a"], small_full["rg_b_gate_x"], small_full["rg_lambda"]
    sc_cw = small_full["sc_conv_w"][0]
    ffn_cw = small_full["ffn_conv_w"]

    tp = _row_tile(n_meta + seq, ROW_TILE_PERM)
    h0 = _to_tiles(jnp.concatenate([small_full["meta_tokens"], x[0]], axis=0), tp)
    target = _to_tiles(jnp.concatenate([jnp.zeros((n_meta, d), F32), loss_target[0]], axis=0), tp)

    def ffn_fwd(h, l):
        n, a, z = _ffn_up_fused(h, norm_ffn_g[l:l + 1], ffn_w_up_f[l], ffn_cw[l], tp, f"ffn{l}_up")
        return _matmul_residual(z, ffn_w_down_f[l], h, f"ffn{l}_down"), (n, a, z)

    n0, p0, q0 = _sc_in_fused(h0, norm_mix_g[0:1], sc_w_in_f, sc_cw, tp, "sc_in")
    h1 = _matmul_residual(q0, sc_w_out_f, h0, "sc_out")
    h2, ffn0_saved = ffn_fwd(h1, 0)
    n2, p2, y2, hs2 = _rg_in_fused(h2, norm_mix_g[1:2], rg_w_in_f, rg_cw, rg_cb, wa, rg_ba, wx, rg_bx, rg_lam, tp, "rg_in")
    h3 = _matmul_residual(y2, rg_w_out_f, h2, "rg_out")
    h4, ffn1_saved = ffn_fwd(h3, 1)

    dh4, loss_tile, d_final_g = _loss_head(h4, final_norm_g.reshape(1, d), target, n_meta, tp, "loss_head")
    loss = lax.psum(loss_tile[0, 0], AXES)

    def ffn_bwd(dh_out, h_in, saved, l):
        n, a, z = saved
        da, dhb, d_cw = _ffn_down_bwd_fused(dh_out, ffn_w_down_t[l], a, ffn_cw[l], tp, f"ffn{l}_down_bwd")
        d_w_down = _wgrad(z, dhb, f"ffn{l}_down_wgrad")
        d_w_up_t = _wgrad(da, n, f"ffn{l}_up_wgrad")
        dh_in, d_g = _dgrad_in_norm(da, ffn_w_up_t[l], h_in, norm_ffn_g[l:l + 1], dh_out, f"ffn{l}_up_dgrad")
        return dh_in, d_w_up_t, d_cw, d_w_down, d_g

    dh3, d_up1, d_fcw1, d_down1, d_fg1 = ffn_bwd(dh4, h3, ffn1_saved, 1)

    dp2, dhb3, d_rg_cw, d_rg_cb, d_wa, d_rg_ba, d_wx, d_rg_bx, d_rg_lam = _rg_out_bwd_fused(
        dh3, rg_w_out_t, p2, hs2, rg_cw, rg_cb, wa, rg_ba, wx, rg_bx, rg_lam, tp, "rg_out_bwd")
    d_rg_w_out = _wgrad(y2, dhb3, "rg_out_wgrad")
    d_rg_w_in_t = _wgrad(dp2, n2, "rg_in_wgrad")
    dh2, d_mg1 = _dgrad_in_norm(dp2, rg_w_in_t, h2, norm_mix_g[1:2], dh3, "rg_in_dgrad")

    dh1, d_up0, d_fcw0, d_down0, d_fg0 = ffn_bwd(dh2, h1, ffn0_saved, 0)

    dp0, dhb1, d_sc_cw = _sc_out_bwd_fused(dh1, sc_w_out_t, p0, sc_cw, tp, "sc_out_bwd")
    d_sc_w_out = _wgrad(q0, dhb1, "sc_out_wgrad")
    d_sc_w_in_t = _wgrad(dp0, n0, "sc_in_wgrad")
    dh0, d_mg0 = _dgrad_in_norm(dp0, sc_w_in_t, h0, norm_mix_g[0:1], dh1, "sc_in_dgrad")

    dh0 = _from_tiles(dh0, tp)
    grad_x = dh0[n_meta:][None]

    big_grads = [d_sc_w_in_t, d_sc_w_out, d_rg_w_in_t, d_rg_w_out, d_up0, d_up1, d_down0, d_down1]
    big_chunks = [g.reshape(N_DEV, r_, d) for g, r_ in zip(big_grads, big_rows)]
    small_grads = {"meta_tokens": dh0[:n_meta], "sc_conv_w": d_sc_cw[None], "rg_conv_w": d_rg_cw[None],
                   "rg_conv_b": d_rg_cb, "rg_b_gate_a": d_rg_ba, "rg_b_gate_x": d_rg_bx, "rg_lambda": d_rg_lam,
                   "ffn_conv_w": jnp.stack([d_fcw0, d_fcw1])}
    small_chunks = jnp.concatenate([_cols_to_chunks(small_grads[n]) for n in small_names], axis=1)
    pad = small_rows.shape[0] * d - small_chunks.shape[1]
    small_chunks = jnp.pad(small_chunks, ((0, 0), (0, pad))).reshape(N_DEV, small_rows.shape[0], d)
    rep_names = ["norm_mix_g", "norm_ffn_g", "final_norm_g", "rg_w_gate_a", "rg_w_gate_x"]
    rep_grads = {"norm_mix_g": jnp.concatenate([d_mg0, d_mg1], axis=0),
                 "norm_ffn_g": jnp.concatenate([d_fg0, d_fg1], axis=0),
                 "final_norm_g": d_final_g.reshape(-1), "rg_w_gate_a": d_wa[None], "rg_w_gate_x": d_wx[None]}
    rep_flat = jnp.concatenate([rep_grads[n].reshape(-1) for n in rep_names])
    rep_chunk_rows = -(-rep_flat.shape[0] // (N_DEV * d))
    rep_chunk_rows += -(sum(big_rows) + small_rows.shape[0] + rep_chunk_rows) % 64
    rep_chunks = jnp.pad(rep_flat, (0, N_DEV * rep_chunk_rows * d - rep_flat.shape[0])).reshape(N_DEV, rep_chunk_rows, d)

    rs_buf = jnp.concatenate(big_chunks + [small_chunks, rep_chunks], axis=1)
    reduced = _reduce_scatter(rs_buf, core)

    grads = {}
    o = 0
    big_names = ["sc_w_in", "sc_w_out", "rg_w_in", "rg_w_out"]
    for n, r_ in zip(big_names, big_rows[:4]):
        g = reduced[o:o + r_]
        grads[n] = (g.T if n.endswith("_in") else g)[None]
        o += r_
    up = [reduced[o + l * big_rows[4]:o + (l + 1) * big_rows[4]].T for l in range(n_ffn)]
    o += n_ffn * big_rows[4]
    down = [reduced[o + l * big_rows[4 + n_ffn]:o + (l + 1) * big_rows[4 + n_ffn]] for l in range(n_ffn)]
    o += n_ffn * big_rows[4 + n_ffn]
    grads["ffn_w_up"] = jnp.stack(up)
    grads["ffn_w_down"] = jnp.stack(down)
    small_red = reduced[o:o + small_rows.shape[0]].reshape(-1)
    o += small_rows.shape[0]
    so = 0
    for n, sz in zip(small_names, small_sizes):
        grads[n] = small_red[so:so + sz].reshape(weights[n].shape)
        so += sz
    rep_red = _all_gather(reduced[o:o + rep_chunks.shape[1]], "gather_replicated_grads").reshape(-1)
    ro = 0
    for n in rep_names:
        sz = weights[n].size
        grads[n] = rep_red[ro:ro + sz].reshape(weights[n].shape)
        ro += sz

    delta, new_m, new_v = {}, {}, {}
    for n in names:
        delta[n], new_m[n], new_v[n] = _adamw_nd(weights[n], grads[n], m_in[n], v_in[n], f"adamw_{n}")

    return (loss, grad_x, *[grads[n] for n in names], *[delta[n] for n in names],
            *[new_m[n] for n in names], *[new_v[n] for n in names])
```

```python
import jax
import jax.numpy as jnp
from jax import lax
from jax.experimental import pallas as pl
from jax.experimental.pallas import tpu as pltpu

F32 = jnp.float32
MXU_DTYPE = jnp.bfloat16
RMS_EPS = 1e-6
RG_C = 8.0
ADAM_LR = 0.001
ADAM_B1 = 0.9
ADAM_B2 = 0.999
ADAM_EPS = 1e-08
ADAM_WD = 0.01
ADAM_STEP = 10

N_DEV = 8
AXES = ("x", "y", "c")
SUBLANES = 8
LANES = 128
VMEM_LIMIT_BYTES = 48 * 1024 * 1024
ROW_TILE_MATMUL = 700
ROW_TILE_PERM = 400
STRIP = 256

_NT = (((1,), (1,)), ((), ()))
_TN = (((0,), (0,)), ((), ()))
_NN = (((1,), (0,)), ((), ()))


def _row_tile(t, target):
    best = None
    for tm in range(16, t + 1, 16):
        if t % tm == 0 and tm <= target:
            best = tm
    return best if best is not None else t


def _col_tile(n, target):
    best = None
    for tn in range(LANES, n + 1, LANES):
        if n % tn == 0 and tn <= target:
            best = tn
    return best if best is not None else n


def _params(*sem):
    return pltpu.CompilerParams(dimension_semantics=sem, vmem_limit_bytes=VMEM_LIMIT_BYTES)


def _dot(a, b, dims):
    return lax.dot_general(a, b, dims, preferred_element_type=F32)


def _sigmoid(x):
    return 1.0 / (1.0 + jnp.exp(-x))


def _gelu(x):
    c = 0.7978845608028654
    t = jnp.tanh(c * (x + 0.044715 * (x * x * x)))
    return 0.5 * x * (1.0 + t)


def _gelu_grad(x):
    c = 0.7978845608028654
    t = jnp.tanh(c * (x + 0.044715 * (x * x * x)))
    return 0.5 * (1.0 + t) + 0.5 * x * (1.0 - t * t) * c * (1.0 + 3.0 * 0.044715 * x * x)


def _softplus(x):
    return jnp.maximum(x, 0.0) + jnp.log1p(jnp.exp(-jnp.abs(x)))


def _to_tiles(a, tm):
    t, d = a.shape
    return a.reshape(t // tm, SUBLANES, tm // SUBLANES, d).transpose(0, 2, 1, 3).reshape(t, d)


def _from_tiles(a, tm):
    t, d = a.shape
    return a.reshape(t // tm, tm // SUBLANES, SUBLANES, d).transpose(0, 2, 1, 3).reshape(t, d)


def _rows(ref, sl):
    return [ref[k:k + 1, sl] for k in range(ref.shape[0])]


def _shift_down(x, before, s):
    if s == 0:
        return x
    n = x.shape[0]
    row = lax.broadcasted_iota(jnp.int32, (SUBLANES, x.shape[1]), 0)
    heads = []
    for g in range(s):
        v = x[n - (s - g) * SUBLANES:n - (s - g - 1) * SUBLANES]
        heads.append(pltpu.roll(jnp.where(row == SUBLANES - 1, before[s - g - 1], v), 1, axis=0))
    return jnp.concatenate(heads + [x[0:n - s * SUBLANES]], axis=0)


def _shift_up(x, after, s):
    if s == 0:
        return x
    row = lax.broadcasted_iota(jnp.int32, (SUBLANES, x.shape[1]), 0)
    tails = []
    for m in range(s):
        v = x[m * SUBLANES:(m + 1) * SUBLANES]
        tails.append(pltpu.roll(jnp.where(row == 0, after[m], v), SUBLANES - 1, axis=0))
    return jnp.concatenate([x[s * SUBLANES:]] + tails, axis=0)


def _weighted(w, windows):
    y = w[0] * windows[0]
    for k in range(1, len(w)):
        y = y + w[k] * windows[k]
    return y


def _windows(x, before, k_width):
    return [_shift_down(x, before, k_width - 1 - k) for k in range(k_width)]


def _conv_t(dy, after, w):
    k_width = len(w)
    return _weighted(w, [_shift_up(dy, after, k_width - 1 - k) for k in range(k_width)])


def _blocks(ref, sl, count, newest_first):
    n = ref.shape[0] // SUBLANES
    order = range(n - 1, n - 1 - count, -1) if newest_first else range(count)
    return [ref[b * SUBLANES:(b + 1) * SUBLANES, sl] for b in order]


def _halo_specs(tm, cols, count, tile_of):
    def spec(k):
        return pl.BlockSpec((SUBLANES, cols), lambda i: (jnp.maximum(tile_of(i) * (tm // SUBLANES) - k, 0), 0))
    return [spec(k) for k in range(1, count + 1)]


def _scan_tile(coef, val, out, carry, groups, reverse):
    cols = coef.shape[1]
    row = lax.broadcasted_iota(jnp.int32, (SUBLANES, cols), 0)

    def blk(i):
        g = groups - 1 - i if reverse else i
        return pl.ds(pl.multiple_of(g * SUBLANES, SUBLANES), SUBLANES)

    def local(i, pl_):
        p_prev, l_prev = pl_
        a = coef[blk(i), :]
        p = a * p_prev
        l = a * l_prev + val[blk(i), :]
        coef[blk(i), :] = p
        val[blk(i), :] = l
        return p, l

    pf, lf = lax.fori_loop(0, groups, local, (jnp.ones((SUBLANES, cols), F32), jnp.zeros((SUBLANES, cols), F32)))
    for s in (1, 2, 4):
        keep, sh = (row < SUBLANES - s, SUBLANES - s) if reverse else (row >= s, s)
        p_s = jnp.where(keep, pltpu.roll(pf, sh, axis=0), 1.0)
        l_s = jnp.where(keep, pltpu.roll(lf, sh, axis=0), 0.0)
        lf = pf * l_s + lf
        pf = pf * p_s
    end = lf + pf * carry
    if reverse:
        init = jnp.where(row == SUBLANES - 1, carry, pltpu.roll(end, SUBLANES - 1, axis=0))
        leaving = jnp.broadcast_to(end[0:1, :], (SUBLANES, cols))
    else:
        init = jnp.where(row == 0, carry, pltpu.roll(end, 1, axis=0))
        leaving = jnp.broadcast_to(end[SUBLANES - 1:SUBLANES, :], (SUBLANES, cols))

    def fix(i, _):
        out[blk(i), :] = val[blk(i), :] + coef[blk(i), :] * init
        return 0

    lax.fori_loop(0, groups, fix, 0)
    return leaving


def _resident(shape):
    return pl.BlockSpec(shape, lambda *_: (0,) * len(shape), pipeline_mode=pl.Buffered(1))


def _full(shape):
    return pl.BlockSpec(shape, lambda *_: (0,) * len(shape))


def _rmsnorm_to(h_ref, g_ref, n_ref):
    x = h_ref[...]
    ms = jnp.mean(x * x, axis=-1, keepdims=True)
    n_ref[...] = (x * lax.rsqrt(ms + RMS_EPS) * g_ref[...]).astype(n_ref.dtype)


_ANY = pl.BlockSpec(memory_space=pl.ANY)
_MESH = pl.DeviceIdType.MESH


def _dma_sems(n):
    return pltpu.SemaphoreType.DMA((n,))


def _when_each(*phases):
    for cond, fn in phases:
        if cond is not None:
            pl.when(cond)(fn)


class _NoRider:
    inputs = in_specs = out_shape = out_specs = scratch_shapes = ()

    def __call__(self, first, middle, last, ins, outs, scratch):
        pass


_NO_RIDER = _NoRider()


def _split_refs(rider, n_out, n_scratch, rest):
    a = len(rider.inputs)
    b = a + n_out
    c = b + len(rider.out_shape)
    e = c + n_scratch
    return rest[:a], rest[a:b], rest[b:c], rest[c:e], rest[e:]


class _GatherRider:
    def __init__(self, blk):
        self.inputs = (blk,)
        self.in_specs = (_ANY,)
        self.out_shape = (jax.ShapeDtypeStruct((N_DEV,) + blk.shape, blk.dtype),)
        self.out_specs = (_ANY,)
        self.scratch_shapes = (_dma_sems(7), _dma_sems(7), pltpu.SemaphoreType.DMA(()))

    def __call__(self, first, middle, last, ins, outs, scratch):
        (x_ref,), (out_ref,) = ins, outs
        send_sems, recv_sems, local_sem = scratch
        x, y, c = lax.axis_index("x"), lax.axis_index("y"), lax.axis_index("c")
        me, sibling = (x, y, c), (x, y, 1 - c)
        chips = [(1 - x, y), (x, 1 - y), (1 - x, 1 - y)]

        def slot(px, py, pc):
            return out_ref.at[4 * px + 2 * py + pc]

        def copy(k, block, to, src=None):
            return pltpu.make_async_remote_copy(
                src_ref=slot(*block) if src is None else src, dst_ref=slot(*block),
                send_sem=send_sems.at[k], recv_sem=recv_sems.at[k], device_id=to, device_id_type=_MESH)

        mine = pltpu.make_async_copy(x_ref, slot(*me), local_sem)
        own = [copy(0, me, sibling, src=x_ref)]
        own += [copy(1 + j, me, (*chip, c), src=x_ref) for j, chip in enumerate(chips)]
        passed = [copy(4 + j, (*chip, c), sibling) for j, chip in enumerate(chips)]

        def at_first():
            mine.start()
            for cp in own:
                cp.start()

        def at_middle():
            for j, chip in enumerate(chips):
                copy(1 + j, (*chip, c), me).wait_recv()
                passed[j].start()

        def at_last():
            copy(0, sibling, me).wait_recv()
            for j, chip in enumerate(chips):
                copy(4 + j, (*chip, 1 - c), me).wait_recv()
            for cp in own + passed:
                cp.wait_send()
            mine.wait()

        _when_each((first, at_first), (middle, at_middle), (last, at_last))


class _ScatterRider:
    def __init__(self, grad):
        r = grad.shape[0] // N_DEV
        chunks = grad.reshape(N_DEV, r, grad.shape[1])
        self.inputs = (chunks,)
        self.in_specs = (_ANY,)
        self.out_shape = (jax.ShapeDtypeStruct(chunks.shape, chunks.dtype),)
        self.out_specs = (_ANY,)
        self.scratch_shapes = (_dma_sems(N_DEV - 1), _dma_sems(N_DEV - 1), pltpu.SemaphoreType.DMA(()))

    def __call__(self, first, middle, last, ins, outs, scratch):
        (g_ref,), (r_ref,) = ins, outs
        send_sems, recv_sems, local_sem = scratch
        x, y, c = lax.axis_index("x"), lax.axis_index("y"), lax.axis_index("c")
        me = 4 * x + 2 * y + c
        mine = pltpu.make_async_copy(g_ref.at[me], r_ref.at[me], local_sem)
        copies = []
        for k in range(1, N_DEV):
            px, py, pc = (1 - x if k & 4 else x), (1 - y if k & 2 else y), (1 - c if k & 1 else c)
            copies.append(pltpu.make_async_remote_copy(
                src_ref=g_ref.at[4 * px + 2 * py + pc], dst_ref=r_ref.at[me],
                send_sem=send_sems.at[k - 1], recv_sem=recv_sems.at[k - 1],
                device_id=(px, py, pc), device_id_type=_MESH))

        def at_first():
            mine.start()
            for cp in copies:
                cp.start()

        def at_last():
            for cp in copies:
                cp.wait()
            mine.wait()

        _when_each((first, at_first), (last, at_last))


def _matmul_residual(q, w, h, name):
    t, k = q.shape
    d = w.shape[1]
    tm = _row_tile(t, ROW_TILE_MATMUL)

    def body(q_ref, w_ref, h_ref, o_ref):
        o_ref[...] = h_ref[...] + _dot(q_ref[...], w_ref[...], _NN)

    return pl.pallas_call(
        body, name=name, grid=(t // tm,),
        in_specs=[pl.BlockSpec((tm, k), lambda i: (i, 0)),
                  _resident((k, d)),
                  pl.BlockSpec((tm, d), lambda i: (i, 0))],
        out_specs=pl.BlockSpec((tm, d), lambda i: (i, 0)),
        out_shape=jax.ShapeDtypeStruct((t, d), F32),
        compiler_params=_params("parallel"),
    )(q, w, h)


def _dgrad_in_norm(dp, wt, h, g, dh_next, name, rider=_NO_RIDER):
    t, n = dp.shape
    d = wt.shape[1]
    tm = _row_tile(t, ROW_TILE_MATMUL)
    nt = t // tm

    def body(dp_ref, wt_ref, h_ref, g_ref, dhn_ref, *rest):
        r_in, (dh_ref, dg_ref), r_out, _, r_scratch = _split_refs(rider, 2, 0, rest)
        i = pl.program_id(0)
        rider(i == 0, i == nt // 2, None, r_in, r_out, r_scratch)
        dn = _dot(dp_ref[...], wt_ref[...], _NN)
        x = h_ref[...]
        ms = jnp.mean(x * x, axis=-1, keepdims=True)
        r = lax.rsqrt(ms + RMS_EPS)
        xhat = x * r
        dng = dn * g_ref[...]
        c = jnp.mean(dng * xhat, axis=-1, keepdims=True)
        dh_ref[...] = dhn_ref[...] + r * (dng - xhat * c)
        part = jnp.sum(dn * xhat, axis=0, keepdims=True)

        @pl.when(i == 0)
        def _():
            dg_ref[...] = part

        @pl.when(i > 0)
        def _():
            dg_ref[...] += part

        rider(None, None, i == nt - 1, r_in, r_out, r_scratch)

    return pl.pallas_call(
        body, name=name, grid=(nt,),
        in_specs=[pl.BlockSpec((tm, n), lambda i: (i, 0)),
                  _resident((n, d)),
                  pl.BlockSpec((tm, d), lambda i: (i, 0)),
                  pl.BlockSpec((1, d), lambda i: (0, 0)),
                  pl.BlockSpec((tm, d), lambda i: (i, 0))] + list(rider.in_specs),
        out_specs=[pl.BlockSpec((tm, d), lambda i: (i, 0)),
                   pl.BlockSpec((1, d), lambda i: (0, 0))] + list(rider.out_specs),
        out_shape=[jax.ShapeDtypeStruct((t, d), F32), jax.ShapeDtypeStruct((1, d), F32)] + list(rider.out_shape),
        scratch_shapes=list(rider.scratch_shapes),
        compiler_params=_params("arbitrary"),
    )(dp, wt, h, g, dh_next, *rider.inputs)


def _wgrad(a, b, name, rider=_NO_RIDER):
    t, m = a.shape
    d = b.shape[1]
    tmm = _col_tile(m, 1536)
    tk = _row_tile(t, ROW_TILE_MATMUL)
    nm, nk = m // tmm, t // tk

    def body(a_ref, b_ref, *rest):
        r_in, (o_ref,), r_out, _, r_scratch = _split_refs(rider, 1, 0, rest)
        i, k = pl.program_id(0), pl.program_id(1)
        rider(jnp.logical_and(i == 0, k == 0), None, None, r_in, r_out, r_scratch)

        @pl.when(k == 0)
        def _():
            o_ref[...] = jnp.zeros_like(o_ref)

        o_ref[...] += _dot(a_ref[...], b_ref[...], _TN)
        rider(None, None, jnp.logical_and(i == nm - 1, k == nk - 1), r_in, r_out, r_scratch)

    outs = pl.pallas_call(
        body, name=name, grid=(nm, nk),
        in_specs=[pl.BlockSpec((tk, tmm), lambda i, k: (k, i)),
                  pl.BlockSpec((tk, d), lambda i, k: (k, 0))] + list(rider.in_specs),
        out_specs=[pl.BlockSpec((tmm, d), lambda i, k: (i, 0))] + list(rider.out_specs),
        out_shape=[jax.ShapeDtypeStruct((m, d), F32)] + list(rider.out_shape),
        scratch_shapes=list(rider.scratch_shapes),
        compiler_params=_params("arbitrary", "arbitrary"),
    )(a, b, *rider.inputs)
    return outs if rider.out_shape else outs[0]


def _ffn_up_fused(h, g, w, conv_w, tm, name, rider=_NO_RIDER):
    t, d = h.shape
    f2 = w.shape[1]
    f = f2 // 2
    cw = min(STRIP, f)
    kw = conv_w.shape[0]
    nh = kw - 1
    nt = t // tm

    def body(h_ref, g_ref, w_ref, cw_ref, *rest):
        r_in, (n_ref, a_ref, z_ref), r_out, (carry,), r_scratch = _split_refs(rider, 3, 1, rest)
        i = pl.program_id(0)
        rider(i == 0, i == nt // 2, None, r_in, r_out, r_scratch)

        @pl.when(i == 0)
        def _():
            carry[...] = jnp.zeros_like(carry)

        _rmsnorm_to(h_ref, g_ref, n_ref)
        for c in range(0, f, cw):
            conv = []
            for sl in (slice(c, c + cw), slice(f + c, f + c + cw)):
                a = _dot(n_ref[...], w_ref[:, sl], _NN)
                a_ref[:, sl] = a
                conv.append(_weighted(_rows(cw_ref, sl), _windows(a, _blocks(carry, sl, nh, True), kw)))
                carry[:, sl] = a[tm - nh * SUBLANES:tm]
            gg, vv = conv
            z_ref[:, c:c + cw] = (gg * _sigmoid(gg) * vv).astype(z_ref.dtype)
        rider(None, None, i == nt - 1, r_in, r_out, r_scratch)

    return pl.pallas_call(
        body, name=name, grid=(nt,),
        in_specs=[pl.BlockSpec((tm, d), lambda i: (i, 0)), pl.BlockSpec((1, d), lambda i: (0, 0)),
                  _resident((d, f2)), _full(conv_w.shape)] + list(rider.in_specs),
        out_specs=[pl.BlockSpec((tm, d), lambda i: (i, 0)), pl.BlockSpec((tm, f2), lambda i: (i, 0)),
                   pl.BlockSpec((tm, f), lambda i: (i, 0))] + list(rider.out_specs),
        out_shape=[jax.ShapeDtypeStruct((t, d), MXU_DTYPE), jax.ShapeDtypeStruct((t, f2), F32),
                   jax.ShapeDtypeStruct((t, f), MXU_DTYPE)] + list(rider.out_shape),
        scratch_shapes=[pltpu.VMEM((nh * SUBLANES, f2), F32)] + list(rider.scratch_shapes),
        compiler_params=_params("arbitrary"),
    )(h, g, w, conv_w, *rider.inputs)


def _ffn_down_bwd_fused(dh, wdt, a, conv_w, tm, name):
    t, d = dh.shape
    f = wdt.shape[1]
    f2 = 2 * f
    nt = t // tm
    cw = min(STRIP, f)
    kw = conv_w.shape[0]
    nh = kw - 1
    rev = lambda i: nt - 1 - i

    def body(dh_ref, wdt_ref, a_ref, *rest):
        halo_refs = rest[:nh]
        cw_ref, da_ref, dhb_ref, dw_ref, carry = rest[nh:]
        i = pl.program_id(0)
        newest, oldest = i == 0, i == nt - 1

        @pl.when(newest)
        def _():
            carry[...] = jnp.zeros_like(carry)
            dw_ref[...] = jnp.zeros_like(dw_ref)

        dhb_ref[...] = dh_ref[...].astype(dhb_ref.dtype)
        for c in range(0, f, cw):
            gsl, vsl = slice(c, c + cw), slice(f + c, f + c + cw)
            dz = _dot(dhb_ref[...], wdt_ref[:, gsl], _NN)
            win, conv = {}, {}
            for sl in (gsl, vsl):
                before = [jnp.where(oldest, 0.0, r[:, sl]) for r in halo_refs]
                win[sl.start] = _windows(a_ref[:, sl], before, kw)
                conv[sl.start] = _weighted(_rows(cw_ref, sl), win[sl.start])
            gg, vv = conv[gsl.start], conv[vsl.start]
            s = _sigmoid(gg)
            grads = {gsl.start: dz * vv * s * (1.0 + gg * (1.0 - s)), vsl.start: dz * gg * s}
            for sl in (gsl, vsl):
                dconv = grads[sl.start]
                da_ref[:, sl] = _conv_t(dconv, _blocks(carry, sl, nh, False), _rows(cw_ref, sl)).astype(da_ref.dtype)
                for k in range(kw):
                    dw_ref[k:k + 1, sl] += jnp.sum(dconv * win[sl.start][k], axis=0, keepdims=True)
                carry[:, sl] = dconv[0:nh * SUBLANES]

    return pl.pallas_call(
        body, name=name, grid=(nt,),
        in_specs=[pl.BlockSpec((tm, d), lambda i: (rev(i), 0)), _resident((d, f)),
                  pl.BlockSpec((tm, f2), lambda i: (rev(i), 0))] + _halo_specs(tm, f2, nh, rev) + [_full(conv_w.shape)],
        out_specs=[pl.BlockSpec((tm, f2), lambda i: (rev(i), 0)), pl.BlockSpec((tm, d), lambda i: (rev(i), 0)),
                   _full(conv_w.shape)],
        out_shape=[jax.ShapeDtypeStruct((t, f2), MXU_DTYPE), jax.ShapeDtypeStruct((t, d), MXU_DTYPE),
                   jax.ShapeDtypeStruct(conv_w.shape, F32)],
        scratch_shapes=[pltpu.VMEM((nh * SUBLANES, f2), F32)],
        compiler_params=_params("arbitrary"),
    )(dh, wdt, a, *([a] * nh), conv_w)


def _sc_in_fused(h, g, w, conv_w, tm, name, rider=_NO_RIDER):
    t, d = h.shape
    cw = min(STRIP, d)
    kw = conv_w.shape[0]
    nh = kw - 1
    nt = t // tm

    def body(h_ref, g_ref, w_ref, cw_ref, *rest):
        r_in, (n_ref, p_ref, q_ref), r_out, (carry,), r_scratch = _split_refs(rider, 3, 1, rest)
        i = pl.program_id(0)
        rider(i == 0, i == nt // 2, None, r_in, r_out, r_scratch)

        @pl.when(i == 0)
        def _():
            carry[...] = jnp.zeros_like(carry)

        _rmsnorm_to(h_ref, g_ref, n_ref)
        for c in range(0, d, cw):
            sl = slice(c, c + cw)
            parts = []
            for base in (0, d, 2 * d):
                psl = slice(base + c, base + c + cw)
                parts.append(_dot(n_ref[...], w_ref[:, psl], _NN))
                p_ref[:, psl] = parts[-1]
            bg, cg, v = parts
            cv = cg * v
            u = _weighted(_rows(cw_ref, sl), _windows(cv, _blocks(carry, sl, nh, True), kw))
            carry[:, sl] = cv[tm - nh * SUBLANES:tm]
            q_ref[:, sl] = (bg * u).astype(q_ref.dtype)
        rider(None, None, i == nt - 1, r_in, r_out, r_scratch)

    return pl.pallas_call(
        body, name=name, grid=(nt,),
        in_specs=[pl.BlockSpec((tm, d), lambda i: (i, 0)), pl.BlockSpec((1, d), lambda i: (0, 0)),
                  _resident((d, 3 * d)), _full(conv_w.shape)] + list(rider.in_specs),
        out_specs=[pl.BlockSpec((tm, d), lambda i: (i, 0)), pl.BlockSpec((tm, 3 * d), lambda i: (i, 0)),
                   pl.BlockSpec((tm, d), lambda i: (i, 0))] + list(rider.out_specs),
        out_shape=[jax.ShapeDtypeStruct((t, d), MXU_DTYPE), jax.ShapeDtypeStruct((t, 3 * d), F32),
                   jax.ShapeDtypeStruct((t, d), MXU_DTYPE)] + list(rider.out_shape),
        scratch_shapes=[pltpu.VMEM((nh * SUBLANES, d), F32)] + list(rider.scratch_shapes),
        compiler_params=_params("arbitrary"),
    )(h, g, w, conv_w, *rider.inputs)


def _sc_out_bwd_fused(dh, wot, p, conv_w, tm, name):
    t, d = dh.shape
    nt = t // tm
    cw = min(STRIP, d)
    kw = conv_w.shape[0]
    nh = kw - 1
    rev = lambda i: nt - 1 - i

    def body(dh_ref, wot_ref, p_ref, *rest):
        halo_refs = rest[:nh]
        cw_ref, dp_ref, dhb_ref, dw_ref, carry = rest[nh:]
        i = pl.program_id(0)
        newest, oldest = i == 0, i == nt - 1

        @pl.when(newest)
        def _():
            carry[...] = jnp.zeros_like(carry)
            dw_ref[...] = jnp.zeros_like(dw_ref)

        dhb_ref[...] = dh_ref[...].astype(dhb_ref.dtype)
        for c in range(0, d, cw):
            sl, csl, vsl = slice(c, c + cw), slice(d + c, d + c + cw), slice(2 * d + c, 2 * d + c + cw)
            w = _rows(cw_ref, sl)
            dq = _dot(dhb_ref[...], wot_ref[:, sl], _NN)
            bg, cg, v = p_ref[:, sl], p_ref[:, csl], p_ref[:, vsl]
            before = [jnp.where(oldest, 0.0, r[:, csl] * r[:, vsl]) for r in halo_refs]
            win = _windows(cg * v, before, kw)
            du = dq * bg
            dcv = _conv_t(du, _blocks(carry, sl, nh, False), w)
            dp_ref[:, sl] = (dq * _weighted(w, win)).astype(dp_ref.dtype)
            dp_ref[:, csl] = (dcv * v).astype(dp_ref.dtype)
            dp_ref[:, vsl] = (dcv * cg).astype(dp_ref.dtype)
            for k in range(kw):
                dw_ref[k:k + 1, sl] += jnp.sum(du * win[k], axis=0, keepdims=True)
            carry[:, sl] = du[0:nh * SUBLANES]

    return pl.pallas_call(
        body, name=name, grid=(nt,),
        in_specs=[pl.BlockSpec((tm, d), lambda i: (rev(i), 0)), _resident((d, d)),
                  pl.BlockSpec((tm, 3 * d), lambda i: (rev(i), 0))] + _halo_specs(tm, 3 * d, nh, rev)
                 + [_full(conv_w.shape)],
        out_specs=[pl.BlockSpec((tm, 3 * d), lambda i: (rev(i), 0)), pl.BlockSpec((tm, d), lambda i: (rev(i), 0)),
                   _full(conv_w.shape)],
        out_shape=[jax.ShapeDtypeStruct((t, 3 * d), MXU_DTYPE), jax.ShapeDtypeStruct((t, d), MXU_DTYPE),
                   jax.ShapeDtypeStruct(conv_w.shape, F32)],
        scratch_shapes=[pltpu.VMEM((nh * SUBLANES, d), F32)],
        compiler_params=_params("arbitrary"),
    )(dh, wot, p, *([p] * nh), conv_w)


def _rg_gates(u, k, wa_ref, ba, wx_ref, bx, lam):
    ub = u.astype(MXU_DTYPE)
    r = _sigmoid(_dot(ub, wa_ref[k], _NN) + ba)
    ig = _sigmoid(_dot(ub, wx_ref[k], _NN) + bx)
    sp = _softplus(-lam)
    la = -RG_C * r * sp
    a = jnp.exp(la)
    th = jnp.tanh(la)
    mult = jnp.sqrt(-2.0 * th / (1.0 - th))
    return r, ig, a, mult


def _rg_in_fused(h, g, w, conv_w, conv_b, wa, ba, wx, bx, lam, tm, name, rider=_NO_RIDER):
    t, d = h.shape
    r2 = w.shape[1]
    rr = r2 // 2
    nb, bd, _ = wa.shape
    kw = conv_w.shape[0]
    nh = kw - 1
    groups = tm // SUBLANES
    nt = t // tm

    def body(h_ref, g_ref, w_ref, cw_ref, cb_ref, wa_ref, ba_ref, wx_ref, bx_ref, lam_ref, *rest):
        r_in, (n_ref, p_ref, y_ref, hs_ref), r_out, (a_scr, b_scr, carry_rb, carry_h), r_scratch = _split_refs(
            rider, 4, 4, rest)
        i = pl.program_id(0)
        rider(i == 0, i == nt // 2, None, r_in, r_out, r_scratch)

        @pl.when(i == 0)
        def _():
            carry_rb[...] = jnp.zeros_like(carry_rb)
            carry_h[...] = jnp.zeros_like(carry_h)

        _rmsnorm_to(h_ref, g_ref, n_ref)
        for k in range(nb):
            sl = slice(k * bd, (k + 1) * bd)
            rsl = slice(rr + k * bd, rr + (k + 1) * bd)
            p_ref[:, sl] = _dot(n_ref[...], w_ref[:, sl], _NN)
            rb = _dot(n_ref[...], w_ref[:, rsl], _NN)
            p_ref[:, rsl] = rb
            u = _weighted(_rows(cw_ref, sl), _windows(rb, _blocks(carry_rb, sl, nh, True), kw)) + cb_ref[:, sl]
            carry_rb[:, sl] = rb[tm - nh * SUBLANES:tm]
            _, ig, a, mult = _rg_gates(u, k, wa_ref, ba_ref[:, sl], wx_ref, bx_ref[:, sl], lam_ref[:, sl])
            a_scr[:, sl] = a
            b_scr[:, sl] = mult * (ig * u)

        carry_h[...] = _scan_tile(a_scr, b_scr, hs_ref, carry_h[...], groups, reverse=False)

        for k in range(nb):
            sl = slice(k * bd, (k + 1) * bd)
            y_ref[:, sl] = (hs_ref[:, sl] * _gelu(p_ref[:, sl])).astype(y_ref.dtype)
        rider(None, None, i == nt - 1, r_in, r_out, r_scratch)

    vm = lambda rows: pltpu.VMEM((rows, rr), F32)
    return pl.pallas_call(
        body, name=name, grid=(nt,),
        in_specs=[pl.BlockSpec((tm, d), lambda i: (i, 0)), pl.BlockSpec((1, d), lambda i: (0, 0)),
                  _resident((d, r2)), _full(conv_w.shape), _full(conv_b.shape), _full(wa.shape), _full(ba.shape),
                  _full(wx.shape), _full(bx.shape), _full(lam.shape)] + list(rider.in_specs),
        out_specs=[pl.BlockSpec((tm, d), lambda i: (i, 0)), pl.BlockSpec((tm, r2), lambda i: (i, 0)),
                   pl.BlockSpec((tm, rr), lambda i: (i, 0)), pl.BlockSpec((tm, rr), lambda i: (i, 0))]
                  + list(rider.out_specs),
        out_shape=[jax.ShapeDtypeStruct((t, d), MXU_DTYPE), jax.ShapeDtypeStruct((t, r2), F32),
                   jax.ShapeDtypeStruct((t, rr), MXU_DTYPE), jax.ShapeDtypeStruct((t, rr), F32)]
                  + list(rider.out_shape),
        scratch_shapes=[vm(tm), vm(tm), vm(nh * SUBLANES), vm(SUBLANES)] + list(rider.scratch_shapes),
        compiler_params=_params("arbitrary"),
    )(h, g, w, conv_w, conv_b, wa, ba, wx, bx, lam, *rider.inputs)


def _rg_out_bwd_fused(dh, wot, p, hs, conv_w, conv_b, wa, ba, wx, bx, lam, tm, name):
    t, d = dh.shape
    r2 = p.shape[1]
    rr = r2 // 2
    nb, bd, _ = wa.shape
    nt = t // tm
    kw = conv_w.shape[0]
    nh = kw - 1
    groups = tm // SUBLANES
    rev = lambda i: nt - 1 - i

    def body(dh_ref, wot_ref, p_ref, hs_ref, hsh_ref, *rest):
        halo_refs = rest[:nh]
        (cw_ref, cb_ref, wa_ref, ba_ref, wx_ref, bx_ref, lam_ref,
         dp_ref, dhb_ref, dcw_ref, dcb_ref, dwa_ref, dba_ref, dwx_ref, dbx_ref, dlam_ref,
         a_scr, as_scr, u_scr, r_scr, ig_scr, mult_scr, g_scr, carry_g, carry_du) = rest[nh:]
        i = pl.program_id(0)
        newest, oldest = i == 0, i == nt - 1

        @pl.when(newest)
        def _():
            carry_g[...] = jnp.zeros_like(carry_g)
            carry_du[...] = jnp.zeros_like(carry_du)
            for ref in (dcw_ref, dcb_ref, dwa_ref, dba_ref, dwx_ref, dbx_ref, dlam_ref):
                ref[...] = jnp.zeros_like(ref)

        dhb_ref[...] = dh_ref[...].astype(dhb_ref.dtype)
        ones = jnp.ones((SUBLANES, bd), F32)

        def conv_windows(sl, rsl):
            before = [jnp.where(oldest, 0.0, r[:, rsl]) for r in halo_refs]
            return _windows(p_ref[:, rsl], before, kw)

        for k in range(nb):
            sl = slice(k * bd, (k + 1) * bd)
            rsl = slice(rr + k * bd, rr + (k + 1) * bd)
            dy = _dot(dhb_ref[...], wot_ref[:, sl], _NN)
            u = _weighted(_rows(cw_ref, sl), conv_windows(sl, rsl)) + cb_ref[:, sl]
            r, ig, a, mult = _rg_gates(u, k, wa_ref, ba_ref[:, sl], wx_ref, bx_ref[:, sl], lam_ref[:, sl])
            gb = p_ref[:, sl]
            g_scr[:, sl] = dy * _gelu(gb)
            dp_ref[:, sl] = (dy * hs_ref[:, sl] * _gelu_grad(gb)).astype(dp_ref.dtype)
            a_scr[:, sl] = a
            as_scr[:, sl] = _shift_up(a, [ones], 1)
            u_scr[:, sl] = u
            r_scr[:, sl] = r
            ig_scr[:, sl] = ig
            mult_scr[:, sl] = mult

        g_first = _scan_tile(as_scr, g_scr, g_scr, carry_g[...], groups, reverse=True)
        carry_g[...] = jnp.broadcast_to(a_scr[0:1, :], (SUBLANES, rr)) * g_first

        for k in range(nb):
            sl = slice(k * bd, (k + 1) * bd)
            rsl = slice(rr + k * bd, rr + (k + 1) * bd)
            cw = _rows(cw_ref, sl)
            lam_k = lam_ref[:, sl]
            sp = _softplus(-lam_k)
            g = g_scr[:, sl]
            a, u, r, ig, mult = a_scr[:, sl], u_scr[:, sl], r_scr[:, sl], ig_scr[:, sl], mult_scr[:, sl]
            da = g * _shift_down(hs_ref[:, sl], [jnp.where(oldest, 0.0, hsh_ref[:, sl])], 1)
            dmult = g * (ig * u)
            d_iu = g * mult
            dla = da * a - dmult * (a * a) / mult
            dr = dla * (-RG_C * sp)
            dsp = jnp.sum(dla * (-RG_C * r), axis=0, keepdims=True)
            dlam_ref[:, sl] += dsp * (-_sigmoid(-lam_k))
            dza = dr * r * (1.0 - r)
            dzx = (d_iu * u) * ig * (1.0 - ig)
            dba_ref[:, sl] += jnp.sum(dza, axis=0, keepdims=True)
            dbx_ref[:, sl] += jnp.sum(dzx, axis=0, keepdims=True)
            ub = u.astype(MXU_DTYPE)
            dzab, dzxb = dza.astype(MXU_DTYPE), dzx.astype(MXU_DTYPE)
            dwa_ref[k] += _dot(ub, dzab, _TN)
            dwx_ref[k] += _dot(ub, dzxb, _TN)
            du = d_iu * ig + _dot(dzab, wa_ref[k], _NT) + _dot(dzxb, wx_ref[k], _NT)
            dcb_ref[:, sl] += jnp.sum(du, axis=0, keepdims=True)
            win = conv_windows(sl, rsl)
            for kk in range(kw):
                dcw_ref[kk:kk + 1, sl] += jnp.sum(du * win[kk], axis=0, keepdims=True)
            dp_ref[:, rsl] = _conv_t(du, _blocks(carry_du, sl, nh, False), cw).astype(dp_ref.dtype)
            carry_du[:, sl] = du[0:nh * SUBLANES]

    tile = lambda cols: pl.BlockSpec((tm, cols), lambda i: (rev(i), 0))
    vm = lambda rows: pltpu.VMEM((rows, rr), F32)
    grads = [conv_w.shape, conv_b.shape, wa.shape, ba.shape, wx.shape, bx.shape, lam.shape]
    return pl.pallas_call(
        body, name=name, grid=(nt,),
        in_specs=[tile(d), _resident((d, rr)), tile(r2), tile(rr)] + _halo_specs(tm, rr, 1, rev)
                 + _halo_specs(tm, r2, nh, rev)
                 + [_full(conv_w.shape), _full(conv_b.shape), _full(wa.shape), _full(ba.shape),
                    _full(wx.shape), _full(bx.shape), _full(lam.shape)],
        out_specs=[tile(r2), tile(d)] + [_full(s) for s in grads],
        out_shape=[jax.ShapeDtypeStruct((t, r2), MXU_DTYPE), jax.ShapeDtypeStruct((t, d), MXU_DTYPE)]
                  + [jax.ShapeDtypeStruct(s, F32) for s in grads],
        scratch_shapes=[vm(tm)] * 7 + [vm(SUBLANES), vm(nh * SUBLANES)],
        compiler_params=_params("arbitrary"),
    )(dh, wot, p, hs, hs, *([p] * nh), conv_w, conv_b, wa, ba, wx, bx, lam)


def _loss_head(h, g, target, n_meta, tm, name):
    t, d = h.shape
    groups = tm // SUBLANES

    def body(h_ref, g_ref, t_ref, dh_ref, loss_ref, dg_ref):
        i = pl.program_id(0)
        x = h_ref[...]
        ms = jnp.mean(x * x, axis=-1, keepdims=True)
        r = lax.rsqrt(ms + RMS_EPS)
        xhat = x * r
        gg = g_ref[...]
        row = lax.broadcasted_iota(jnp.int32, (tm, 1), 0)
        time = i * tm + jnp.right_shift(row, 3) + jnp.bitwise_and(row, SUBLANES - 1) * groups
        err = jnp.where(time >= n_meta, xhat * gg - t_ref[...], 0.0)
        dout = err * (1.0 / d)
        dng = dout * gg
        c = jnp.mean(dng * xhat, axis=-1, keepdims=True)
        dh_ref[...] = r * (dng - xhat * c)
        part_dg = jnp.sum(dout * xhat, axis=0, keepdims=True)
        part_loss = jnp.broadcast_to(0.5 * jnp.sum(err * dout, keepdims=True), loss_ref.shape)

        @pl.when(i == 0)
        def _():
            dg_ref[...] = part_dg
            loss_ref[...] = part_loss

        @pl.when(i > 0)
        def _():
            dg_ref[...] += part_dg
            loss_ref[...] += part_loss

    return pl.pallas_call(
        body, name=name, grid=(t // tm,),
        in_specs=[pl.BlockSpec((tm, d), lambda i: (i, 0)),
                  pl.BlockSpec((1, d), lambda i: (0, 0)),
                  pl.BlockSpec((tm, d), lambda i: (i, 0))],
        out_specs=[pl.BlockSpec((tm, d), lambda i: (i, 0)),
                   pl.BlockSpec((SUBLANES, LANES), lambda i: (0, 0)),
                   pl.BlockSpec((1, d), lambda i: (0, 0))],
        out_shape=[jax.ShapeDtypeStruct((t, d), F32), jax.ShapeDtypeStruct((SUBLANES, LANES), F32),
                   jax.ShapeDtypeStruct((1, d), F32)],
        compiler_params=_params("arbitrary"),
    )(h, g, target)


def _adamw(w, g, m, v, name):
    rows, cols = w.shape
    tr = rows
    if rows > 512:
        for cand in range(8, 513, 8):
            if rows % cand == 0:
                tr = cand

    def body(w_ref, g_ref, m_ref, v_ref, d_ref, nm_ref, nv_ref):
        g_ = g_ref[...]
        m_ = ADAM_B1 * m_ref[...] + (1.0 - ADAM_B1) * g_
        v_ = ADAM_B2 * v_ref[...] + (1.0 - ADAM_B2) * (g_ * g_)
        m_hat = m_ / (1.0 - ADAM_B1 ** ADAM_STEP)
        v_hat = v_ / (1.0 - ADAM_B2 ** ADAM_STEP)
        d_ref[...] = -ADAM_LR * (m_hat / (jnp.sqrt(v_hat) + ADAM_EPS) + ADAM_WD * w_ref[...])
        nm_ref[...] = m_
        nv_ref[...] = v_

    spec = pl.BlockSpec((tr, cols), lambda i: (i, 0))
    shape = jax.ShapeDtypeStruct((rows, cols), F32)
    return pl.pallas_call(
        body, name=name, grid=(rows // tr,),
        in_specs=[spec] * 4, out_specs=[spec] * 3, out_shape=[shape] * 3,
        compiler_params=_params("parallel"),
    )(w, g, m, v)


def _adamw_nd(w, g, m, v, name):
    shape = w.shape
    two_d = (-1, shape[-1]) if w.ndim > 1 else (1, -1)
    outs = _adamw(w.reshape(two_d), g.reshape(two_d), m.reshape(two_d), v.reshape(two_d), name)
    return tuple(o.reshape(shape) for o in outs)


def _all_gather(blk, name):
    rows, cols = blk.shape

    def body(x_ref, out_ref, send_sems, recv_sems, local_sem):
        x, y, c = lax.axis_index("x"), lax.axis_index("y"), lax.axis_index("c")
        me, sibling = (x, y, c), (x, y, 1 - c)
        chips = [(1 - x, y), (x, 1 - y), (1 - x, 1 - y)]

        def slot(px, py, pc):
            return out_ref.at[4 * px + 2 * py + pc]

        def copy(k, block, to, src=None):
            return pltpu.make_async_remote_copy(
                src_ref=slot(*block) if src is None else src, dst_ref=slot(*block),
                send_sem=send_sems.at[k], recv_sem=recv_sems.at[k], device_id=to, device_id_type=_MESH)

        mine = pltpu.make_async_copy(x_ref, slot(*me), local_sem)
        mine.start()
        first = [copy(0, me, sibling, src=x_ref)]
        first += [copy(1 + j, me, (*chip, c), src=x_ref) for j, chip in enumerate(chips)]
        for cp in first:
            cp.start()
        passed = [copy(4 + j, (*chip, c), sibling) for j, chip in enumerate(chips)]
        for j, chip in enumerate(chips):
            copy(1 + j, (*chip, c), me).wait_recv()
            passed[j].start()
        copy(0, sibling, me).wait_recv()
        for j, chip in enumerate(chips):
            copy(4 + j, (*chip, 1 - c), me).wait_recv()
        for cp in first + passed:
            cp.wait_send()
        mine.wait()

    return pl.pallas_call(
        body, name=name,
        in_specs=[_ANY], out_specs=_ANY,
        out_shape=jax.ShapeDtypeStruct((N_DEV, rows, cols), blk.dtype),
        scratch_shapes=[pltpu.SemaphoreType.DMA((7,)), pltpu.SemaphoreType.DMA((7,)), pltpu.SemaphoreType.DMA(())],
    )(blk)


def _rs_sibling_exchange(buf, name):
    _, rows, cols = buf.shape

    def body(buf_ref, out_ref, send_sems, recv_sems):
        x, y, c = lax.axis_index("x"), lax.axis_index("y"), lax.axis_index("c")
        copies = [pltpu.make_async_remote_copy(
            src_ref=buf_ref.at[2 * k + (1 - c)], dst_ref=out_ref.at[k],
            send_sem=send_sems.at[k], recv_sem=recv_sems.at[k],
            device_id=(x, y, 1 - c), device_id_type=_MESH) for k in range(4)]
        for cp in copies:
            cp.start()
        for cp in copies:
            cp.wait()

    return pl.pallas_call(
        body, name=name,
        in_specs=[_ANY], out_specs=_ANY,
        out_shape=jax.ShapeDtypeStruct((4, rows, cols), buf.dtype),
        scratch_shapes=[pltpu.SemaphoreType.DMA((4,)), pltpu.SemaphoreType.DMA((4,))],
    )(buf)


def _rs_chip_sum(buf, recv, core, name):
    _, rows, cols = buf.shape
    tr = _row_tile(rows, 512) if rows % 16 == 0 else rows

    def body(c_ref, a_ref, b_ref, o_ref):
        o_ref[...] = a_ref[...] + b_ref[...]

    grid_spec = pltpu.PrefetchScalarGridSpec(
        num_scalar_prefetch=1, grid=(4, rows // tr),
        in_specs=[pl.BlockSpec((None, tr, cols), lambda k, j, c: (2 * k + c[0], j, 0)),
                  pl.BlockSpec((None, tr, cols), lambda k, j, c: (k, j, 0))],
        out_specs=pl.BlockSpec((None, tr, cols), lambda k, j, c: (k, j, 0)))
    return pl.pallas_call(
        body, name=name, grid_spec=grid_spec,
        out_shape=jax.ShapeDtypeStruct((4, rows, cols), buf.dtype),
        compiler_params=_params("parallel", "parallel"),
    )(core, buf, recv)


def _rs_chip_exchange(chip_sum, name):
    _, rows, cols = chip_sum.shape

    def body(cs_ref, out_ref, send_sems, recv_sems, local_sem):
        x, y, c = lax.axis_index("x"), lax.axis_index("y"), lax.axis_index("c")
        my_chip = 2 * x + y
        mine = pltpu.make_async_copy(cs_ref.at[my_chip], out_ref.at[my_chip], local_sem)
        mine.start()
        chips = [(1 - x, y), (x, 1 - y), (1 - x, 1 - y)]
        copies = [pltpu.make_async_remote_copy(
            src_ref=cs_ref.at[2 * px + py], dst_ref=out_ref.at[my_chip],
            send_sem=send_sems.at[j], recv_sem=recv_sems.at[j],
            device_id=(px, py, c), device_id_type=_MESH) for j, (px, py) in enumerate(chips)]
        for cp in copies:
            cp.start()
        for cp in copies:
            cp.wait()
        mine.wait()

    return pl.pallas_call(
        body, name=name,
        in_specs=[_ANY], out_specs=_ANY,
        out_shape=jax.ShapeDtypeStruct((4, rows, cols), chip_sum.dtype),
        scratch_shapes=[pltpu.SemaphoreType.DMA((3,)), pltpu.SemaphoreType.DMA((3,)), pltpu.SemaphoreType.DMA(())],
    )(chip_sum)


def _sum_slots(parts, name):
    slots, rows, cols = parts.shape
    tr = _row_tile(rows, 256) if rows % 16 == 0 else rows

    def body(p_ref, o_ref):
        acc = p_ref[0]
        for s in range(1, slots):
            acc = acc + p_ref[s]
        o_ref[...] = acc

    return pl.pallas_call(
        body, name=name, grid=(rows // tr,),
        in_specs=[pl.BlockSpec((slots, tr, cols), lambda i: (0, i, 0))],
        out_specs=pl.BlockSpec((tr, cols), lambda i: (i, 0)),
        out_shape=jax.ShapeDtypeStruct((rows, cols), parts.dtype),
        compiler_params=_params("parallel"),
    )(parts)


def _reduce_scatter(buf, core):
    recv = _rs_sibling_exchange(buf, "rs_sibling_exchange")
    chip_sum = _rs_chip_sum(buf, recv, core, "rs_chip_sum")
    parts = _rs_chip_exchange(chip_sum, "rs_chip_exchange")
    return _sum_slots(parts, "rs_sum_slots")


def _pad_rows(flat, cols, multiple):
    n = flat.shape[0]
    rows = -(-n // cols)
    rows = -(-rows // multiple) * multiple
    return jnp.pad(flat, (0, rows * cols - n)).reshape(rows, cols)


def _cols_to_chunks(full):
    lead = full.shape[:-1]
    c = full.shape[-1] // N_DEV
    x = full.reshape(-1, N_DEV, c)
    return jnp.transpose(x, (1, 0, 2)).reshape(N_DEV, -1)


def _chunks_to_cols(chunks, lead):
    n = 1
    for s in lead:
        n *= s
    c = chunks.shape[1] // n
    x = chunks.reshape(N_DEV, n, c)
    return jnp.transpose(x, (1, 0, 2)).reshape(tuple(lead) + (N_DEV * c,))


def kernel(x, meta_tokens, norm_mix_g, norm_ffn_g, final_norm_g, sc_w_in, sc_conv_w, sc_w_out, rg_w_in, rg_conv_w, rg_conv_b, rg_w_gate_a, rg_b_gate_a, rg_w_gate_x, rg_b_gate_x, rg_lambda, rg_w_out, ffn_w_up, ffn_conv_w, ffn_w_down, loss_target, m_meta_tokens, m_norm_mix_g, m_norm_ffn_g, m_final_norm_g, m_sc_w_in, m_sc_conv_w, m_sc_w_out, m_rg_w_in, m_rg_conv_w, m_rg_conv_b, m_rg_w_gate_a, m_rg_b_gate_a, m_rg_w_gate_x, m_rg_b_gate_x, m_rg_lambda, m_rg_w_out, m_ffn_w_up, m_ffn_conv_w, m_ffn_w_down, v_meta_tokens, v_norm_mix_g, v_norm_ffn_g, v_final_norm_g, v_sc_w_in, v_sc_conv_w, v_sc_w_out, v_rg_w_in, v_rg_conv_w, v_rg_conv_b, v_rg_w_gate_a, v_rg_b_gate_a, v_rg_w_gate_x, v_rg_b_gate_x, v_rg_lambda, v_rg_w_out, v_ffn_w_up, v_ffn_conv_w, v_ffn_w_down):
    weights = dict(meta_tokens=meta_tokens, norm_mix_g=norm_mix_g, norm_ffn_g=norm_ffn_g, final_norm_g=final_norm_g,
                   sc_w_in=sc_w_in, sc_conv_w=sc_conv_w, sc_w_out=sc_w_out, rg_w_in=rg_w_in, rg_conv_w=rg_conv_w,
                   rg_conv_b=rg_conv_b, rg_w_gate_a=rg_w_gate_a, rg_b_gate_a=rg_b_gate_a, rg_w_gate_x=rg_w_gate_x,
                   rg_b_gate_x=rg_b_gate_x, rg_lambda=rg_lambda, rg_w_out=rg_w_out, ffn_w_up=ffn_w_up,
                   ffn_conv_w=ffn_conv_w, ffn_w_down=ffn_w_down)
    m_in = dict(meta_tokens=m_meta_tokens, norm_mix_g=m_norm_mix_g, norm_ffn_g=m_norm_ffn_g, final_norm_g=m_final_norm_g,
                sc_w_in=m_sc_w_in, sc_conv_w=m_sc_conv_w, sc_w_out=m_sc_w_out, rg_w_in=m_rg_w_in, rg_conv_w=m_rg_conv_w,
                rg_conv_b=m_rg_conv_b, rg_w_gate_a=m_rg_w_gate_a, rg_b_gate_a=m_rg_b_gate_a, rg_w_gate_x=m_rg_w_gate_x,
                rg_b_gate_x=m_rg_b_gate_x, rg_lambda=m_rg_lambda, rg_w_out=m_rg_w_out, ffn_w_up=m_ffn_w_up,
                ffn_conv_w=m_ffn_conv_w, ffn_w_down=m_ffn_w_down)
    v_in = dict(meta_tokens=v_meta_tokens, norm_mix_g=v_norm_mix_g, norm_ffn_g=v_norm_ffn_g, final_norm_g=v_final_norm_g,
                sc_w_in=v_sc_w_in, sc_conv_w=v_sc_conv_w, sc_w_out=v_sc_w_out, rg_w_in=v_rg_w_in, rg_conv_w=v_rg_conv_w,
                rg_conv_b=v_rg_conv_b, rg_w_gate_a=v_rg_w_gate_a, rg_b_gate_a=v_rg_b_gate_a, rg_w_gate_x=v_rg_w_gate_x,
                rg_b_gate_x=v_rg_b_gate_x, rg_lambda=v_rg_lambda, rg_w_out=v_rg_w_out, ffn_w_up=v_ffn_w_up,
                ffn_conv_w=v_ffn_conv_w, ffn_w_down=v_ffn_w_down)
    names = list(weights)

    seq, d = x.shape[1], x.shape[2]
    n_meta = meta_tokens.shape[0]
    n_ffn = ffn_w_up.shape[0]
    rr = rg_w_gate_a.shape[1] * rg_w_gate_a.shape[2]
    core = lax.axis_index("c").astype(jnp.int32).reshape(1)

    assert n_ffn == 2
    def shard_rows(w_in, w_out):
        return jnp.concatenate([w_in.T, w_out], axis=0).astype(MXU_DTYPE)

    def both_orientations(gathered, w_in):
        r_in = w_in.shape[1]
        w_in_t = gathered[:, :r_in, :].reshape(N_DEV * r_in, d)
        w_out = gathered[:, r_in:, :].reshape(-1, d)
        return w_in_t.T, w_in_t, w_out, w_out.T

    shards = {"ffn0": shard_rows(ffn_w_up[0], ffn_w_down[0]), "rg": shard_rows(rg_w_in[0], rg_w_out[0]),
              "ffn1": shard_rows(ffn_w_up[1], ffn_w_down[1])}
    sc_w_in_f, sc_w_in_t, sc_w_out_f, sc_w_out_t = both_orientations(
        _all_gather(shard_rows(sc_w_in[0], sc_w_out[0]), "gather_sc"), sc_w_in[0])
    ffn_w_up_f, ffn_w_up_t, ffn_w_down_f, ffn_w_down_t = [None] * 2, [None] * 2, [None] * 2, [None] * 2

    small_names = ["meta_tokens", "sc_conv_w", "rg_conv_w", "rg_conv_b", "rg_b_gate_a", "rg_b_gate_x", "rg_lambda",
                   "ffn_conv_w"]
    small_lead = {n: weights[n].shape[:-1] for n in small_names}
    small_sizes = [weights[n].size for n in small_names]
    small_flat = jnp.concatenate([weights[n].reshape(-1) for n in small_names])
    small_rows = _pad_rows(small_flat, d, SUBLANES)
    small_g = _all_gather(small_rows, "gather_small").reshape(N_DEV, -1)
    small_full = {}
    o = 0
    for n, sz in zip(small_names, small_sizes):
        small_full[n] = _chunks_to_cols(small_g[:, o:o + sz], small_lead[n])
        o += sz

    wa = rg_w_gate_a[0].astype(MXU_DTYPE)
    wx = rg_w_gate_x[0].astype(MXU_DTYPE)
    rg_cw, rg_cb = small_full["rg_conv_w"][0], small_full["rg_conv_b"]
    rg_ba, rg_bx, rg_lam = small_full["rg_b_gate_a"], small_full["rg_b_gate_x"], small_full["rg_lambda"]
    sc_cw = small_full["sc_conv_w"][0]
    ffn_cw = small_full["ffn_conv_w"]

    tp = _row_tile(n_meta + seq, ROW_TILE_PERM)
    h0 = _to_tiles(jnp.concatenate([small_full["meta_tokens"], x[0]], axis=0), tp)
    target = _to_tiles(jnp.concatenate([jnp.zeros((n_meta, d), F32), loss_target[0]], axis=0), tp)

    def ffn_fwd(h, l, rider):
        n, a, z, *gathered = _ffn_up_fused(h, norm_ffn_g[l:l + 1], ffn_w_up_f[l], ffn_cw[l], tp, f"ffn{l}_up", rider)
        return _matmul_residual(z, ffn_w_down_f[l], h, f"ffn{l}_down"), (n, a, z), gathered

    n0, p0, q0, ffn0_w = _sc_in_fused(h0, norm_mix_g[0:1], sc_w_in_f, sc_cw, tp, "sc_in", _GatherRider(shards["ffn0"]))
    ffn_w_up_f[0], ffn_w_up_t[0], ffn_w_down_f[0], ffn_w_down_t[0] = both_orientations(ffn0_w, ffn_w_up[0])
    h1 = _matmul_residual(q0, sc_w_out_f, h0, "sc_out")
    h2, ffn0_saved, (rg_w,) = ffn_fwd(h1, 0, _GatherRider(shards["rg"]))
    rg_w_in_f, rg_w_in_t, rg_w_out_f, rg_w_out_t = both_orientations(rg_w, rg_w_in[0])
    n2, p2, y2, hs2, ffn1_w = _rg_in_fused(h2, norm_mix_g[1:2], rg_w_in_f, rg_cw, rg_cb, wa, rg_ba, wx, rg_bx, rg_lam,
                                           tp, "rg_in", _GatherRider(shards["ffn1"]))
    ffn_w_up_f[1], ffn_w_up_t[1], ffn_w_down_f[1], ffn_w_down_t[1] = both_orientations(ffn1_w, ffn_w_up[1])
    h3 = _matmul_residual(y2, rg_w_out_f, h2, "rg_out")
    h4, ffn1_saved, _ = ffn_fwd(h3, 1, _NO_RIDER)

    dh4, loss_tile, d_final_g = _loss_head(h4, final_norm_g.reshape(1, d), target, n_meta, tp, "loss_head")
    loss = lax.psum(loss_tile[0, 0], AXES)

    arrived = {}

    def ffn_bwd(dh_out, h_in, saved, l):
        n, a, z = saved
        da, dhb, d_cw = _ffn_down_bwd_fused(dh_out, ffn_w_down_t[l], a, ffn_cw[l], tp, f"ffn{l}_down_bwd")
        d_w_down = _wgrad(z, dhb, f"ffn{l}_down_wgrad")
        d_w_up_t, arrived[f"ffn_w_down{l}"] = _wgrad(da, n, f"ffn{l}_up_wgrad", _ScatterRider(d_w_down))
        dh_in, d_g, arrived[f"ffn_w_up{l}"] = _dgrad_in_norm(
            da, ffn_w_up_t[l], h_in, norm_ffn_g[l:l + 1], dh_out, f"ffn{l}_up_dgrad", _ScatterRider(d_w_up_t))
        return dh_in, d_cw, d_g

    dh3, d_fcw1, d_fg1 = ffn_bwd(dh4, h3, ffn1_saved, 1)

    dp2, dhb3, d_rg_cw, d_rg_cb, d_wa, d_rg_ba, d_wx, d_rg_bx, d_rg_lam = _rg_out_bwd_fused(
        dh3, rg_w_out_t, p2, hs2, rg_cw, rg_cb, wa, rg_ba, wx, rg_bx, rg_lam, tp, "rg_out_bwd")
    d_rg_w_out = _wgrad(y2, dhb3, "rg_out_wgrad")
    d_rg_w_in_t, arrived["rg_w_out"] = _wgrad(dp2, n2, "rg_in_wgrad", _ScatterRider(d_rg_w_out))
    dh2, d_mg1, arrived["rg_w_in"] = _dgrad_in_norm(
        dp2, rg_w_in_t, h2, norm_mix_g[1:2], dh3, "rg_in_dgrad", _ScatterRider(d_rg_w_in_t))

    dh1, d_fcw0, d_fg0 = ffn_bwd(dh2, h1, ffn0_saved, 0)

    dp0, dhb1, d_sc_cw = _sc_out_bwd_fused(dh1, sc_w_out_t, p0, sc_cw, tp, "sc_out_bwd")
    d_sc_w_out = _wgrad(q0, dhb1, "sc_out_wgrad")
    d_sc_w_in_t, arrived["sc_w_out"] = _wgrad(dp0, n0, "sc_in_wgrad", _ScatterRider(d_sc_w_out))
    dh0, d_mg0, arrived["sc_w_in"] = _dgrad_in_norm(
        dp0, sc_w_in_t, h0, norm_mix_g[0:1], dh1, "sc_in_dgrad", _ScatterRider(d_sc_w_in_t))

    dh0 = _from_tiles(dh0, tp)
    grad_x = dh0[n_meta:][None]

    grads = {}
    for n in ("sc_w_in", "rg_w_in"):
        grads[n] = _sum_slots(arrived[n], f"sum_{n}").T[None]
    for n in ("sc_w_out", "rg_w_out"):
        grads[n] = _sum_slots(arrived[n], f"sum_{n}")[None]
    grads["ffn_w_up"] = jnp.stack([_sum_slots(arrived[f"ffn_w_up{l}"], f"sum_ffn_w_up{l}").T for l in range(n_ffn)])
    grads["ffn_w_down"] = jnp.stack([_sum_slots(arrived[f"ffn_w_down{l}"], f"sum_ffn_w_down{l}") for l in range(n_ffn)])

    small_grads = {"meta_tokens": dh0[:n_meta], "sc_conv_w": d_sc_cw[None], "rg_conv_w": d_rg_cw[None],
                   "rg_conv_b": d_rg_cb, "rg_b_gate_a": d_rg_ba, "rg_b_gate_x": d_rg_bx, "rg_lambda": d_rg_lam,
                   "ffn_conv_w": jnp.stack([d_fcw0, d_fcw1])}
    small_chunks = jnp.concatenate([_cols_to_chunks(small_grads[n]) for n in small_names], axis=1)
    pad = small_rows.shape[0] * d - small_chunks.shape[1]
    small_chunks = jnp.pad(small_chunks, ((0, 0), (0, pad))).reshape(N_DEV, small_rows.shape[0], d)
    rep_names = ["norm_mix_g", "norm_ffn_g", "final_norm_g", "rg_w_gate_a", "rg_w_gate_x"]
    rep_grads = {"norm_mix_g": jnp.concatenate([d_mg0, d_mg1], axis=0),
                 "norm_ffn_g": jnp.concatenate([d_fg0, d_fg1], axis=0),
                 "final_norm_g": d_final_g.reshape(-1), "rg_w_gate_a": d_wa[None], "rg_w_gate_x": d_wx[None]}
    rep_flat = jnp.concatenate([rep_grads[n].reshape(-1) for n in rep_names])
    rep_chunk_rows = -(-rep_flat.shape[0] // (N_DEV * d))
    rep_chunk_rows += -(small_rows.shape[0] + rep_chunk_rows) % 16
    rep_chunks = jnp.pad(rep_flat, (0, N_DEV * rep_chunk_rows * d - rep_flat.shape[0])).reshape(N_DEV, rep_chunk_rows, d)

    reduced = _reduce_scatter(jnp.concatenate([small_chunks, rep_chunks], axis=1), core)

    small_red = reduced[0:small_rows.shape[0]].reshape(-1)
    o = small_rows.shape[0]
    so = 0
    for n, sz in zip(small_names, small_sizes):
        grads[n] = small_red[so:so + sz].reshape(weights[n].shape)
        so += sz
    rep_red = _all_gather(reduced[o:o + rep_chunks.shape[1]], "gather_replicated_grads").reshape(-1)
    ro = 0
    for n in rep_names:
        sz = weights[n].size
        grads[n] = rep_red[ro:ro + sz].reshape(weights[n].shape)
        ro += sz

    delta, new_m, new_v = {}, {}, {}
    for n in names:
        delta[n], new_m[n], new_v[n] = _adamw_nd(weights[n], grads[n], m_in[n], v_in[n], f"adamw_{n}")

    return (loss, grad_x, *[grads[n] for n in names], *[delta[n] for n in names],
            *[new_m[n] for n in names], *[new_v[n] for n in names])
```

```python
import jax
import jax.numpy as jnp
from jax import lax
from jax.experimental import pallas as pl
from jax.experimental.pallas import tpu as pltpu

F32 = jnp.float32
MXU_DTYPE = jnp.bfloat16
RMS_EPS = 1e-6
RG_C = 8.0
ADAM_LR = 0.001
ADAM_B1 = 0.9
ADAM_B2 = 0.999
ADAM_EPS = 1e-08
ADAM_WD = 0.01
ADAM_STEP = 10

N_DEV = 8
AXES = ("x", "y", "c")
SUBLANES = 8
LANES = 128
VMEM_LIMIT_BYTES = 48 * 1024 * 1024
ROW_TILE_MATMUL = 700
ROW_TILE_WGRAD = 3300
ROW_TILE_PERM = 400
STRIP = 256

_NT = (((1,), (1,)), ((), ()))
_TN = (((0,), (0,)), ((), ()))
_NN = (((1,), (0,)), ((), ()))


def _row_tile(t, target):
    best = None
    for tm in range(16, t + 1, 16):
        if t % tm == 0 and tm <= target:
            best = tm
    return best if best is not None else t


def _col_tile(n, target):
    best = None
    for tn in range(LANES, n + 1, LANES):
        if n % tn == 0 and tn <= target:
            best = tn
    return best if best is not None else n


def _params(*sem):
    return pltpu.CompilerParams(dimension_semantics=sem, vmem_limit_bytes=VMEM_LIMIT_BYTES)


def _dot(a, b, dims):
    return lax.dot_general(a, b, dims, preferred_element_type=F32)


def _sigmoid(x):
    return 1.0 / (1.0 + jnp.exp(-x))


def _sigmoid_tanh(x):
    return 0.5 * jnp.tanh(0.5 * x) + 0.5


def _gelu(x):
    c = 0.7978845608028654
    t = jnp.tanh(c * (x + 0.044715 * (x * x * x)))
    return 0.5 * x * (1.0 + t)


def _gelu_grad(x):
    c = 0.7978845608028654
    t = jnp.tanh(c * (x + 0.044715 * (x * x * x)))
    return 0.5 * (1.0 + t) + 0.5 * x * (1.0 - t * t) * c * (1.0 + 3.0 * 0.044715 * x * x)


def _softplus(x):
    return jnp.maximum(x, 0.0) + jnp.log1p(jnp.exp(-jnp.abs(x)))


def _to_tiles(a, tm):
    t, d = a.shape
    return a.reshape(t // tm, SUBLANES, tm // SUBLANES, d).transpose(0, 2, 1, 3).reshape(t, d)


def _from_tiles(a, tm):
    t, d = a.shape
    return a.reshape(t // tm, tm // SUBLANES, SUBLANES, d).transpose(0, 2, 1, 3).reshape(t, d)


def _rows(ref, sl):
    return [ref[k:k + 1, sl] for k in range(ref.shape[0])]


def _shift_down(x, before, s):
    if s == 0:
        return x
    n = x.shape[0]
    row = lax.broadcasted_iota(jnp.int32, (SUBLANES, x.shape[1]), 0)
    heads = []
    for g in range(s):
        v = x[n - (s - g) * SUBLANES:n - (s - g - 1) * SUBLANES]
        heads.append(pltpu.roll(jnp.where(row == SUBLANES - 1, before[s - g - 1], v), 1, axis=0))
    return jnp.concatenate(heads + [x[0:n - s * SUBLANES]], axis=0)


def _shift_up(x, after, s):
    if s == 0:
        return x
    row = lax.broadcasted_iota(jnp.int32, (SUBLANES, x.shape[1]), 0)
    tails = []
    for m in range(s):
        v = x[m * SUBLANES:(m + 1) * SUBLANES]
        tails.append(pltpu.roll(jnp.where(row == 0, after[m], v), SUBLANES - 1, axis=0))
    return jnp.concatenate([x[s * SUBLANES:]] + tails, axis=0)


def _weighted(w, windows):
    y = w[0] * windows[0]
    for k in range(1, len(w)):
        y = y + w[k] * windows[k]
    return y


def _windows(x, before, k_width):
    return [_shift_down(x, before, k_width - 1 - k) for k in range(k_width)]


def _conv_t(dy, after, w):
    k_width = len(w)
    return _weighted(w, [_shift_up(dy, after, k_width - 1 - k) for k in range(k_width)])


def _blocks(ref, sl, count, newest_first):
    n = ref.shape[0] // SUBLANES
    order = range(n - 1, n - 1 - count, -1) if newest_first else range(count)
    return [ref[b * SUBLANES:(b + 1) * SUBLANES, sl] for b in order]


def _halo_specs(tm, cols, count, tile_of):
    def spec(k):
        return pl.BlockSpec((SUBLANES, cols), lambda i: (jnp.maximum(tile_of(i) * (tm // SUBLANES) - k, 0), 0))
    return [spec(k) for k in range(1, count + 1)]


def _scan_tile(coef, val, out, carry, groups, reverse):
    cols = coef.shape[1]
    row = lax.broadcasted_iota(jnp.int32, (SUBLANES, cols), 0)

    def blk(i):
        g = groups - 1 - i if reverse else i
        return pl.ds(pl.multiple_of(g * SUBLANES, SUBLANES), SUBLANES)

    def local(i, pl_):
        p_prev, l_prev = pl_
        a = coef[blk(i), :]
        p = a * p_prev
        l = a * l_prev + val[blk(i), :]
        coef[blk(i), :] = p
        val[blk(i), :] = l
        return p, l

    pf, lf = lax.fori_loop(0, groups, local, (jnp.ones((SUBLANES, cols), F32), jnp.zeros((SUBLANES, cols), F32)))
    for s in (1, 2, 4):
        keep, sh = (row < SUBLANES - s, SUBLANES - s) if reverse else (row >= s, s)
        p_s = jnp.where(keep, pltpu.roll(pf, sh, axis=0), 1.0)
        l_s = jnp.where(keep, pltpu.roll(lf, sh, axis=0), 0.0)
        lf = pf * l_s + lf
        pf = pf * p_s
    end = lf + pf * carry
    if reverse:
        init = jnp.where(row == SUBLANES - 1, carry, pltpu.roll(end, SUBLANES - 1, axis=0))
        leaving = jnp.broadcast_to(end[0:1, :], (SUBLANES, cols))
    else:
        init = jnp.where(row == 0, carry, pltpu.roll(end, 1, axis=0))
        leaving = jnp.broadcast_to(end[SUBLANES - 1:SUBLANES, :], (SUBLANES, cols))

    def fix(i, _):
        out[blk(i), :] = val[blk(i), :] + coef[blk(i), :] * init
        return 0

    lax.fori_loop(0, groups, fix, 0)
    return leaving


def _resident(shape):
    return pl.BlockSpec(shape, lambda *_: (0,) * len(shape), pipeline_mode=pl.Buffered(1))


def _full(shape):
    return pl.BlockSpec(shape, lambda *_: (0,) * len(shape))


def _rmsnorm_to(h_ref, g_ref, n_ref):
    x = h_ref[...]
    ms = jnp.mean(x * x, axis=-1, keepdims=True)
    n_ref[...] = (x * lax.rsqrt(ms + RMS_EPS) * g_ref[...]).astype(n_ref.dtype)


_ANY = pl.BlockSpec(memory_space=pl.ANY)
_MESH = pl.DeviceIdType.MESH


def _dma_sems(n):
    return pltpu.SemaphoreType.DMA((n,))


def _when_each(*phases):
    for cond, fn in phases:
        if cond is not None:
            pl.when(cond)(fn)


class _NoRider:
    inputs = in_specs = out_shape = out_specs = scratch_shapes = ()

    def __call__(self, first, middle, last, ins, outs, scratch):
        pass


_NO_RIDER = _NoRider()


def _split_refs(rider, n_out, n_scratch, rest):
    a = len(rider.inputs)
    b = a + n_out
    c = b + len(rider.out_shape)
    e = c + n_scratch
    return rest[:a], rest[a:b], rest[b:c], rest[c:e], rest[e:]


class _GatherRider:
    def __init__(self, blk):
        self.inputs = (blk,)
        self.in_specs = (_ANY,)
        self.out_shape = (jax.ShapeDtypeStruct((N_DEV,) + blk.shape, blk.dtype),)
        self.out_specs = (_ANY,)
        self.scratch_shapes = (_dma_sems(7), _dma_sems(7), pltpu.SemaphoreType.DMA(()))

    def __call__(self, first, middle, last, ins, outs, scratch):
        (x_ref,), (out_ref,) = ins, outs
        send_sems, recv_sems, local_sem = scratch
        x, y, c = lax.axis_index("x"), lax.axis_index("y"), lax.axis_index("c")
        me, sibling = (x, y, c), (x, y, 1 - c)
        chips = [(1 - x, y), (x, 1 - y), (1 - x, 1 - y)]

        def slot(px, py, pc):
            return out_ref.at[4 * px + 2 * py + pc]

        def copy(k, block, to, src=None):
            return pltpu.make_async_remote_copy(
                src_ref=slot(*block) if src is None else src, dst_ref=slot(*block),
                send_sem=send_sems.at[k], recv_sem=recv_sems.at[k], device_id=to, device_id_type=_MESH)

        mine = pltpu.make_async_copy(x_ref, slot(*me), local_sem)
        own = [copy(0, me, sibling, src=x_ref)]
        own += [copy(1 + j, me, (*chip, c), src=x_ref) for j, chip in enumerate(chips)]
        passed = [copy(4 + j, (*chip, c), sibling) for j, chip in enumerate(chips)]

        def at_first():
            mine.start()
            for cp in own:
                cp.start()

        def at_middle():
            for j, chip in enumerate(chips):
                copy(1 + j, (*chip, c), me).wait_recv()
                passed[j].start()

        def at_last():
            copy(0, sibling, me).wait_recv()
            for j, chip in enumerate(chips):
                copy(4 + j, (*chip, 1 - c), me).wait_recv()
            for cp in own + passed:
                cp.wait_send()
            mine.wait()

        _when_each((first, at_first), (middle, at_middle), (last, at_last))


class _ScatterRider:
    def __init__(self, grad):
        r = grad.shape[0] // N_DEV
        chunks = grad.reshape(N_DEV, r, grad.shape[1])
        self.inputs = (chunks,)
        self.in_specs = (_ANY,)
        self.out_shape = (jax.ShapeDtypeStruct(chunks.shape, chunks.dtype),)
        self.out_specs = (_ANY,)
        self.scratch_shapes = (_dma_sems(N_DEV - 1), _dma_sems(N_DEV - 1), pltpu.SemaphoreType.DMA(()))

    def __call__(self, first, middle, last, ins, outs, scratch):
        (g_ref,), (r_ref,) = ins, outs
        send_sems, recv_sems, local_sem = scratch
        x, y, c = lax.axis_index("x"), lax.axis_index("y"), lax.axis_index("c")
        me = 4 * x + 2 * y + c
        mine = pltpu.make_async_copy(g_ref.at[me], r_ref.at[me], local_sem)
        copies = []
        for k in range(1, N_DEV):
            px, py, pc = (1 - x if k & 4 else x), (1 - y if k & 2 else y), (1 - c if k & 1 else c)
            copies.append(pltpu.make_async_remote_copy(
                src_ref=g_ref.at[4 * px + 2 * py + pc], dst_ref=r_ref.at[me],
                send_sem=send_sems.at[k - 1], recv_sem=recv_sems.at[k - 1],
                device_id=(px, py, pc), device_id_type=_MESH))

        def at_first():
            mine.start()
            for cp in copies:
                cp.start()

        def at_last():
            for cp in copies:
                cp.wait()
            mine.wait()

        _when_each((first, at_first), (last, at_last))


def _matmul_residual(q, w, h, name):
    t, k = q.shape
    d = w.shape[1]
    tm = _row_tile(t, ROW_TILE_MATMUL)

    def body(q_ref, w_ref, h_ref, o_ref):
        o_ref[...] = h_ref[...] + _dot(q_ref[...], w_ref[...], _NN)

    return pl.pallas_call(
        body, name=name, grid=(t // tm,),
        in_specs=[pl.BlockSpec((tm, k), lambda i: (i, 0)),
                  _resident((k, d)),
                  pl.BlockSpec((tm, d), lambda i: (i, 0))],
        out_specs=pl.BlockSpec((tm, d), lambda i: (i, 0)),
        out_shape=jax.ShapeDtypeStruct((t, d), F32),
        compiler_params=_params("parallel"),
    )(q, w, h)


def _dgrad_in_norm(dp, wt, h, g, dh_next, name, rider=_NO_RIDER):
    t, n = dp.shape
    d = wt.shape[1]
    tm = _row_tile(t, ROW_TILE_MATMUL)
    nt = t // tm

    def body(dp_ref, wt_ref, h_ref, g_ref, dhn_ref, *rest):
        r_in, (dh_ref, dg_ref), r_out, _, r_scratch = _split_refs(rider, 2, 0, rest)
        i = pl.program_id(0)
        rider(i == 0, i == nt // 2, None, r_in, r_out, r_scratch)
        dn = _dot(dp_ref[...], wt_ref[...], _NN)
        x = h_ref[...]
        ms = jnp.mean(x * x, axis=-1, keepdims=True)
        r = lax.rsqrt(ms + RMS_EPS)
        xhat = x * r
        dng = dn * g_ref[...]
        c = jnp.mean(dng * xhat, axis=-1, keepdims=True)
        dh_ref[...] = dhn_ref[...] + r * (dng - xhat * c)
        part = jnp.sum(dn * xhat, axis=0, keepdims=True)

        @pl.when(i == 0)
        def _():
            dg_ref[...] = part

        @pl.when(i > 0)
        def _():
            dg_ref[...] += part

        rider(None, None, i == nt - 1, r_in, r_out, r_scratch)

    return pl.pallas_call(
        body, name=name, grid=(nt,),
        in_specs=[pl.BlockSpec((tm, n), lambda i: (i, 0)),
                  _resident((n, d)),
                  pl.BlockSpec((tm, d), lambda i: (i, 0)),
                  pl.BlockSpec((1, d), lambda i: (0, 0)),
                  pl.BlockSpec((tm, d), lambda i: (i, 0))] + list(rider.in_specs),
        out_specs=[pl.BlockSpec((tm, d), lambda i: (i, 0)),
                   pl.BlockSpec((1, d), lambda i: (0, 0))] + list(rider.out_specs),
        out_shape=[jax.ShapeDtypeStruct((t, d), F32), jax.ShapeDtypeStruct((1, d), F32)] + list(rider.out_shape),
        scratch_shapes=list(rider.scratch_shapes),
        compiler_params=_params("arbitrary"),
    )(dp, wt, h, g, dh_next, *rider.inputs)


def _wgrad(a, b, name, rider=_NO_RIDER):
    t, m = a.shape
    d = b.shape[1]
    tmm = _col_tile(m, 1024)
    tk = _row_tile(t, ROW_TILE_WGRAD)
    nm, nk = m // tmm, t // tk

    def body(a_ref, b_ref, *rest):
        r_in, (o_ref,), r_out, _, r_scratch = _split_refs(rider, 1, 0, rest)
        i, k = pl.program_id(0), pl.program_id(1)
        rider(jnp.logical_and(i == 0, k == 0), None, None, r_in, r_out, r_scratch)

        @pl.when(k == 0)
        def _():
            o_ref[...] = jnp.zeros_like(o_ref)

        o_ref[...] += _dot(a_ref[...], b_ref[...], _TN)
        rider(None, None, jnp.logical_and(i == nm - 1, k == nk - 1), r_in, r_out, r_scratch)

    outs = pl.pallas_call(
        body, name=name, grid=(nm, nk),
        in_specs=[pl.BlockSpec((tk, tmm), lambda i, k: (k, i)),
                  pl.BlockSpec((tk, d), lambda i, k: (k, 0))] + list(rider.in_specs),
        out_specs=[pl.BlockSpec((tmm, d), lambda i, k: (i, 0))] + list(rider.out_specs),
        out_shape=[jax.ShapeDtypeStruct((m, d), F32)] + list(rider.out_shape),
        scratch_shapes=list(rider.scratch_shapes),
        compiler_params=_params("arbitrary", "arbitrary"),
    )(a, b, *rider.inputs)
    return outs if rider.out_shape else outs[0]


def _ffn_up_fused(h, g, w, conv_w, tm, name, rider=_NO_RIDER):
    t, d = h.shape
    f2 = w.shape[1]
    f = f2 // 2
    cw = min(STRIP, f)
    kw = conv_w.shape[0]
    nh = kw - 1
    nt = t // tm

    def body(h_ref, g_ref, w_ref, cw_ref, *rest):
        r_in, (n_ref, a_ref, z_ref), r_out, (carry,), r_scratch = _split_refs(rider, 3, 1, rest)
        i = pl.program_id(0)
        rider(i == 0, i == nt // 2, None, r_in, r_out, r_scratch)

        @pl.when(i == 0)
        def _():
            carry[...] = jnp.zeros_like(carry)

        _rmsnorm_to(h_ref, g_ref, n_ref)
        for c in range(0, f, cw):
            conv = []
            for sl in (slice(c, c + cw), slice(f + c, f + c + cw)):
                a = _dot(n_ref[...], w_ref[:, sl], _NN)
                a_ref[:, sl] = a
                conv.append(_weighted(_rows(cw_ref, sl), _windows(a, _blocks(carry, sl, nh, True), kw)))
                carry[:, sl] = a[tm - nh * SUBLANES:tm]
            gg, vv = conv
            z_ref[:, c:c + cw] = (gg * _sigmoid(gg) * vv).astype(z_ref.dtype)
        rider(None, None, i == nt - 1, r_in, r_out, r_scratch)

    return pl.pallas_call(
        body, name=name, grid=(nt,),
        in_specs=[pl.BlockSpec((tm, d), lambda i: (i, 0)), pl.BlockSpec((1, d), lambda i: (0, 0)),
                  _resident((d, f2)), _full(conv_w.shape)] + list(rider.in_specs),
        out_specs=[pl.BlockSpec((tm, d), lambda i: (i, 0)), pl.BlockSpec((tm, f2), lambda i: (i, 0)),
                   pl.BlockSpec((tm, f), lambda i: (i, 0))] + list(rider.out_specs),
        out_shape=[jax.ShapeDtypeStruct((t, d), MXU_DTYPE), jax.ShapeDtypeStruct((t, f2), F32),
                   jax.ShapeDtypeStruct((t, f), MXU_DTYPE)] + list(rider.out_shape),
        scratch_shapes=[pltpu.VMEM((nh * SUBLANES, f2), F32)] + list(rider.scratch_shapes),
        compiler_params=_params("arbitrary"),
    )(h, g, w, conv_w, *rider.inputs)


def _ffn_down_bwd_fused(dh, wdt, a, conv_w, tm, name):
    t, d = dh.shape
    f = wdt.shape[1]
    f2 = 2 * f
    nt = t // tm
    cw = min(STRIP, f)
    kw = conv_w.shape[0]
    nh = kw - 1
    rev = lambda i: nt - 1 - i

    def body(dh_ref, wdt_ref, a_ref, *rest):
        halo_refs = rest[:nh]
        cw_ref, da_ref, dhb_ref, dw_ref, carry = rest[nh:]
        i = pl.program_id(0)
        newest, oldest = i == 0, i == nt - 1

        @pl.when(newest)
        def _():
            carry[...] = jnp.zeros_like(carry)
            dw_ref[...] = jnp.zeros_like(dw_ref)

        dhb_ref[...] = dh_ref[...].astype(dhb_ref.dtype)
        for c in range(0, f, cw):
            gsl, vsl = slice(c, c + cw), slice(f + c, f + c + cw)
            dz = _dot(dhb_ref[...], wdt_ref[:, gsl], _NN)
            win, conv = {}, {}
            for sl in (gsl, vsl):
                before = [jnp.where(oldest, 0.0, r[:, sl]) for r in halo_refs]
                win[sl.start] = _windows(a_ref[:, sl], before, kw)
                conv[sl.start] = _weighted(_rows(cw_ref, sl), win[sl.start])
            gg, vv = conv[gsl.start], conv[vsl.start]
            s = _sigmoid(gg)
            grads = {gsl.start: dz * vv * s * (1.0 + gg * (1.0 - s)), vsl.start: dz * gg * s}
            for sl in (gsl, vsl):
                dconv = grads[sl.start]
                da_ref[:, sl] = _conv_t(dconv, _blocks(carry, sl, nh, False), _rows(cw_ref, sl)).astype(da_ref.dtype)
                for k in range(kw):
                    dw_ref[k:k + 1, sl] += jnp.sum(dconv * win[sl.start][k], axis=0, keepdims=True)
                carry[:, sl] = dconv[0:nh * SUBLANES]

    return pl.pallas_call(
        body, name=name, grid=(nt,),
        in_specs=[pl.BlockSpec((tm, d), lambda i: (rev(i), 0)), _resident((d, f)),
                  pl.BlockSpec((tm, f2), lambda i: (rev(i), 0))] + _halo_specs(tm, f2, nh, rev) + [_full(conv_w.shape)],
        out_specs=[pl.BlockSpec((tm, f2), lambda i: (rev(i), 0)), pl.BlockSpec((tm, d), lambda i: (rev(i), 0)),
                   _full(conv_w.shape)],
        out_shape=[jax.ShapeDtypeStruct((t, f2), MXU_DTYPE), jax.ShapeDtypeStruct((t, d), MXU_DTYPE),
                   jax.ShapeDtypeStruct(conv_w.shape, F32)],
        scratch_shapes=[pltpu.VMEM((nh * SUBLANES, f2), F32)],
        compiler_params=_params("arbitrary"),
    )(dh, wdt, a, *([a] * nh), conv_w)


def _sc_in_fused(h, g, w, conv_w, tm, name, rider=_NO_RIDER):
    t, d = h.shape
    cw = min(STRIP, d)
    kw = conv_w.shape[0]
    nh = kw - 1
    nt = t // tm

    def body(h_ref, g_ref, w_ref, cw_ref, *rest):
        r_in, (n_ref, p_ref, q_ref), r_out, (carry,), r_scratch = _split_refs(rider, 3, 1, rest)
        i = pl.program_id(0)
        rider(i == 0, i == nt // 2, None, r_in, r_out, r_scratch)

        @pl.when(i == 0)
        def _():
            carry[...] = jnp.zeros_like(carry)

        _rmsnorm_to(h_ref, g_ref, n_ref)
        for c in range(0, d, cw):
            sl = slice(c, c + cw)
            parts = []
            for base in (0, d, 2 * d):
                psl = slice(base + c, base + c + cw)
                parts.append(_dot(n_ref[...], w_ref[:, psl], _NN))
                p_ref[:, psl] = parts[-1]
            bg, cg, v = parts
            cv = cg * v
            u = _weighted(_rows(cw_ref, sl), _windows(cv, _blocks(carry, sl, nh, True), kw))
            carry[:, sl] = cv[tm - nh * SUBLANES:tm]
            q_ref[:, sl] = (bg * u).astype(q_ref.dtype)
        rider(None, None, i == nt - 1, r_in, r_out, r_scratch)

    return pl.pallas_call(
        body, name=name, grid=(nt,),
        in_specs=[pl.BlockSpec((tm, d), lambda i: (i, 0)), pl.BlockSpec((1, d), lambda i: (0, 0)),
                  _resident((d, 3 * d)), _full(conv_w.shape)] + list(rider.in_specs),
        out_specs=[pl.BlockSpec((tm, d), lambda i: (i, 0)), pl.BlockSpec((tm, 3 * d), lambda i: (i, 0)),
                   pl.BlockSpec((tm, d), lambda i: (i, 0))] + list(rider.out_specs),
        out_shape=[jax.ShapeDtypeStruct((t, d), MXU_DTYPE), jax.ShapeDtypeStruct((t, 3 * d), F32),
                   jax.ShapeDtypeStruct((t, d), MXU_DTYPE)] + list(rider.out_shape),
        scratch_shapes=[pltpu.VMEM((nh * SUBLANES, d), F32)] + list(rider.scratch_shapes),
        compiler_params=_params("arbitrary"),
    )(h, g, w, conv_w, *rider.inputs)


def _sc_out_bwd_fused(dh, wot, p, conv_w, tm, name):
    t, d = dh.shape
    nt = t // tm
    cw = min(STRIP, d)
    kw = conv_w.shape[0]
    nh = kw - 1
    rev = lambda i: nt - 1 - i

    def body(dh_ref, wot_ref, p_ref, *rest):
        halo_refs = rest[:nh]
        cw_ref, dp_ref, dhb_ref, dw_ref, carry = rest[nh:]
        i = pl.program_id(0)
        newest, oldest = i == 0, i == nt - 1

        @pl.when(newest)
        def _():
            carry[...] = jnp.zeros_like(carry)
            dw_ref[...] = jnp.zeros_like(dw_ref)

        dhb_ref[...] = dh_ref[...].astype(dhb_ref.dtype)
        for c in range(0, d, cw):
            sl, csl, vsl = slice(c, c + cw), slice(d + c, d + c + cw), slice(2 * d + c, 2 * d + c + cw)
            w = _rows(cw_ref, sl)
            dq = _dot(dhb_ref[...], wot_ref[:, sl], _NN)
            bg, cg, v = p_ref[:, sl], p_ref[:, csl], p_ref[:, vsl]
            before = [jnp.where(oldest, 0.0, r[:, csl] * r[:, vsl]) for r in halo_refs]
            win = _windows(cg * v, before, kw)
            du = dq * bg
            dcv = _conv_t(du, _blocks(carry, sl, nh, False), w)
            dp_ref[:, sl] = (dq * _weighted(w, win)).astype(dp_ref.dtype)
            dp_ref[:, csl] = (dcv * v).astype(dp_ref.dtype)
            dp_ref[:, vsl] = (dcv * cg).astype(dp_ref.dtype)
            for k in range(kw):
                dw_ref[k:k + 1, sl] += jnp.sum(du * win[k], axis=0, keepdims=True)
            carry[:, sl] = du[0:nh * SUBLANES]

    return pl.pallas_call(
        body, name=name, grid=(nt,),
        in_specs=[pl.BlockSpec((tm, d), lambda i: (rev(i), 0)), _resident((d, d)),
                  pl.BlockSpec((tm, 3 * d), lambda i: (rev(i), 0))] + _halo_specs(tm, 3 * d, nh, rev)
                 + [_full(conv_w.shape)],
        out_specs=[pl.BlockSpec((tm, 3 * d), lambda i: (rev(i), 0)), pl.BlockSpec((tm, d), lambda i: (rev(i), 0)),
                   _full(conv_w.shape)],
        out_shape=[jax.ShapeDtypeStruct((t, 3 * d), MXU_DTYPE), jax.ShapeDtypeStruct((t, d), MXU_DTYPE),
                   jax.ShapeDtypeStruct(conv_w.shape, F32)],
        scratch_shapes=[pltpu.VMEM((nh * SUBLANES, d), F32)],
        compiler_params=_params("arbitrary"),
    )(dh, wot, p, *([p] * nh), conv_w)


def _rg_gates(u, k, wa_ref, ba, wx_ref, bx, lam):
    ub = u.astype(MXU_DTYPE)
    r = _sigmoid_tanh(_dot(ub, wa_ref[k], _NN) + ba)
    ig = _sigmoid_tanh(_dot(ub, wx_ref[k], _NN) + bx)
    sp = _softplus(-lam)
    la = -RG_C * r * sp
    a = jnp.exp(la)
    th = jnp.tanh(la)
    mult = jnp.sqrt(-2.0 * th / (1.0 - th))
    return r, ig, a, mult


def _rg_in_fused(h, g, w, conv_w, conv_b, wa, ba, wx, bx, lam, tm, name, rider=_NO_RIDER):
    t, d = h.shape
    r2 = w.shape[1]
    rr = r2 // 2
    nb, bd, _ = wa.shape
    kw = conv_w.shape[0]
    nh = kw - 1
    groups = tm // SUBLANES
    nt = t // tm

    def body(h_ref, g_ref, w_ref, cw_ref, cb_ref, wa_ref, ba_ref, wx_ref, bx_ref, lam_ref, *rest):
        r_in, (n_ref, p_ref, y_ref, hs_ref), r_out, (a_scr, b_scr, carry_rb, carry_h), r_scratch = _split_refs(
            rider, 4, 4, rest)
        i = pl.program_id(0)
        rider(i == 0, i == nt // 2, None, r_in, r_out, r_scratch)

        @pl.when(i == 0)
        def _():
            carry_rb[...] = jnp.zeros_like(carry_rb)
            carry_h[...] = jnp.zeros_like(carry_h)

        _rmsnorm_to(h_ref, g_ref, n_ref)
        for k in range(nb):
            sl = slice(k * bd, (k + 1) * bd)
            rsl = slice(rr + k * bd, rr + (k + 1) * bd)
            p_ref[:, sl] = _dot(n_ref[...], w_ref[:, sl], _NN)
            rb = _dot(n_ref[...], w_ref[:, rsl], _NN)
            p_ref[:, rsl] = rb
            u = _weighted(_rows(cw_ref, sl), _windows(rb, _blocks(carry_rb, sl, nh, True), kw)) + cb_ref[:, sl]
            carry_rb[:, sl] = rb[tm - nh * SUBLANES:tm]
            _, ig, a, mult = _rg_gates(u, k, wa_ref, ba_ref[:, sl], wx_ref, bx_ref[:, sl], lam_ref[:, sl])
            a_scr[:, sl] = a
            b_scr[:, sl] = mult * (ig * u)

        carry_h[...] = _scan_tile(a_scr, b_scr, hs_ref, carry_h[...], groups, reverse=False)

        for k in range(nb):
            sl = slice(k * bd, (k + 1) * bd)
            y_ref[:, sl] = (hs_ref[:, sl] * _gelu(p_ref[:, sl])).astype(y_ref.dtype)
        rider(None, None, i == nt - 1, r_in, r_out, r_scratch)

    vm = lambda rows: pltpu.VMEM((rows, rr), F32)
    return pl.pallas_call(
        body, name=name, grid=(nt,),
        in_specs=[pl.BlockSpec((tm, d), lambda i: (i, 0)), pl.BlockSpec((1, d), lambda i: (0, 0)),
                  _resident((d, r2)), _full(conv_w.shape), _full(conv_b.shape), _full(wa.shape), _full(ba.shape),
                  _full(wx.shape), _full(bx.shape), _full(lam.shape)] + list(rider.in_specs),
        out_specs=[pl.BlockSpec((tm, d), lambda i: (i, 0)), pl.BlockSpec((tm, r2), lambda i: (i, 0)),
                   pl.BlockSpec((tm, rr), lambda i: (i, 0)), pl.BlockSpec((tm, rr), lambda i: (i, 0))]
                  + list(rider.out_specs),
        out_shape=[jax.ShapeDtypeStruct((t, d), MXU_DTYPE), jax.ShapeDtypeStruct((t, r2), F32),
                   jax.ShapeDtypeStruct((t, rr), MXU_DTYPE), jax.ShapeDtypeStruct((t, rr), F32)]
                  + list(rider.out_shape),
        scratch_shapes=[vm(tm), vm(tm), vm(nh * SUBLANES), vm(SUBLANES)] + list(rider.scratch_shapes),
        compiler_params=_params("arbitrary"),
    )(h, g, w, conv_w, conv_b, wa, ba, wx, bx, lam, *rider.inputs)


def _rg_out_bwd_fused(dh, wot, p, hs, conv_w, conv_b, wa, ba, wx, bx, lam, tm, name):
    t, d = dh.shape
    r2 = p.shape[1]
    rr = r2 // 2
    nb, bd, _ = wa.shape
    nt = t // tm
    kw = conv_w.shape[0]
    nh = kw - 1
    groups = tm // SUBLANES
    rev = lambda i: nt - 1 - i

    def body(dh_ref, wot_ref, p_ref, hs_ref, hsh_ref, *rest):
        halo_refs = rest[:nh]
        (cw_ref, cb_ref, wa_ref, ba_ref, wx_ref, bx_ref, lam_ref,
         dp_ref, dhb_ref, dcw_ref, dcb_ref, dwa_ref, dba_ref, dwx_ref, dbx_ref, dlam_ref,
         a_scr, as_scr, u_scr, r_scr, ig_scr, mult_scr, g_scr, carry_g, carry_du) = rest[nh:]
        i = pl.program_id(0)
        newest, oldest = i == 0, i == nt - 1

        @pl.when(newest)
        def _():
            carry_g[...] = jnp.zeros_like(carry_g)
            carry_du[...] = jnp.zeros_like(carry_du)
            for ref in (dcw_ref, dcb_ref, dwa_ref, dba_ref, dwx_ref, dbx_ref, dlam_ref):
                ref[...] = jnp.zeros_like(ref)

        dhb_ref[...] = dh_ref[...].astype(dhb_ref.dtype)
        ones = jnp.ones((SUBLANES, bd), F32)

        def conv_windows(sl, rsl):
            before = [jnp.where(oldest, 0.0, r[:, rsl]) for r in halo_refs]
            return _windows(p_ref[:, rsl], before, kw)

        for k in range(nb):
            sl = slice(k * bd, (k + 1) * bd)
            rsl = slice(rr + k * bd, rr + (k + 1) * bd)
            dy = _dot(dhb_ref[...], wot_ref[:, sl], _NN)
            u = _weighted(_rows(cw_ref, sl), conv_windows(sl, rsl)) + cb_ref[:, sl]
            r, ig, a, mult = _rg_gates(u, k, wa_ref, ba_ref[:, sl], wx_ref, bx_ref[:, sl], lam_ref[:, sl])
            gb = p_ref[:, sl]
            g_scr[:, sl] = dy * _gelu(gb)
            dp_ref[:, sl] = (dy * hs_ref[:, sl] * _gelu_grad(gb)).astype(dp_ref.dtype)
            a_scr[:, sl] = a
            as_scr[:, sl] = _shift_up(a, [ones], 1)
            u_scr[:, sl] = u
            r_scr[:, sl] = r
            ig_scr[:, sl] = ig
            mult_scr[:, sl] = mult

        g_first = _scan_tile(as_scr, g_scr, g_scr, carry_g[...], groups, reverse=True)
        carry_g[...] = jnp.broadcast_to(a_scr[0:1, :], (SUBLANES, rr)) * g_first

        for k in range(nb):
            sl = slice(k * bd, (k + 1) * bd)
            rsl = slice(rr + k * bd, rr + (k + 1) * bd)
            cw = _rows(cw_ref, sl)
            lam_k = lam_ref[:, sl]
            sp = _softplus(-lam_k)
            g = g_scr[:, sl]
            a, u, r, ig, mult = a_scr[:, sl], u_scr[:, sl], r_scr[:, sl], ig_scr[:, sl], mult_scr[:, sl]
            da = g * _shift_down(hs_ref[:, sl], [jnp.where(oldest, 0.0, hsh_ref[:, sl])], 1)
            dmult = g * (ig * u)
            d_iu = g * mult
            dla = da * a - dmult * (a * a) / mult
            dr = dla * (-RG_C * sp)
            dsp = jnp.sum(dla * (-RG_C * r), axis=0, keepdims=True)
            dlam_ref[:, sl] += dsp * (-_sigmoid(-lam_k))
            dza = dr * r * (1.0 - r)
            dzx = (d_iu * u) * ig * (1.0 - ig)
            dba_ref[:, sl] += jnp.sum(dza, axis=0, keepdims=True)
            dbx_ref[:, sl] += jnp.sum(dzx, axis=0, keepdims=True)
            ub = u.astype(MXU_DTYPE)
            dzab, dzxb = dza.astype(MXU_DTYPE), dzx.astype(MXU_DTYPE)
            dwa_ref[k] += _dot(ub, dzab, _TN)
            dwx_ref[k] += _dot(ub, dzxb, _TN)
            du = d_iu * ig + _dot(dzab, wa_ref[k], _NT) + _dot(dzxb, wx_ref[k], _NT)
            dcb_ref[:, sl] += jnp.sum(du, axis=0, keepdims=True)
            win = conv_windows(sl, rsl)
            for kk in range(kw):
                dcw_ref[kk:kk + 1, sl] += jnp.sum(du * win[kk], axis=0, keepdims=True)
            dp_ref[:, rsl] = _conv_t(du, _blocks(carry_du, sl, nh, False), cw).astype(dp_ref.dtype)
            carry_du[:, sl] = du[0:nh * SUBLANES]

    tile = lambda cols: pl.BlockSpec((tm, cols), lambda i: (rev(i), 0))
    vm = lambda rows: pltpu.VMEM((rows, rr), F32)
    grads = [conv_w.shape, conv_b.shape, wa.shape, ba.shape, wx.shape, bx.shape, lam.shape]
    return pl.pallas_call(
        body, name=name, grid=(nt,),
        in_specs=[tile(d), _resident((d, rr)), tile(r2), tile(rr)] + _halo_specs(tm, rr, 1, rev)
                 + _halo_specs(tm, r2, nh, rev)
                 + [_full(conv_w.shape), _full(conv_b.shape), _full(wa.shape), _full(ba.shape),
                    _full(wx.shape), _full(bx.shape), _full(lam.shape)],
        out_specs=[tile(r2), tile(d)] + [_full(s) for s in grads],
        out_shape=[jax.ShapeDtypeStruct((t, r2), MXU_DTYPE), jax.ShapeDtypeStruct((t, d), MXU_DTYPE)]
                  + [jax.ShapeDtypeStruct(s, F32) for s in grads],
        scratch_shapes=[vm(tm)] * 7 + [vm(SUBLANES), vm(nh * SUBLANES)],
        compiler_params=_params("arbitrary"),
    )(dh, wot, p, hs, hs, *([p] * nh), conv_w, conv_b, wa, ba, wx, bx, lam)


def _loss_head(h, g, target, n_meta, tm, name):
    t, d = h.shape
    groups = tm // SUBLANES

    def body(h_ref, g_ref, t_ref, dh_ref, loss_ref, dg_ref):
        i = pl.program_id(0)
        x = h_ref[...]
        ms = jnp.mean(x * x, axis=-1, keepdims=True)
        r = lax.rsqrt(ms + RMS_EPS)
        xhat = x * r
        gg = g_ref[...]
        row = lax.broadcasted_iota(jnp.int32, (tm, 1), 0)
        time = i * tm + jnp.right_shift(row, 3) + jnp.bitwise_and(row, SUBLANES - 1) * groups
        err = jnp.where(time >= n_meta, xhat * gg - t_ref[...], 0.0)
        dout = err * (1.0 / d)
        dng = dout * gg
        c = jnp.mean(dng * xhat, axis=-1, keepdims=True)
        dh_ref[...] = r * (dng - xhat * c)
        part_dg = jnp.sum(dout * xhat, axis=0, keepdims=True)
        part_loss = jnp.broadcast_to(0.5 * jnp.sum(err * dout, keepdims=True), loss_ref.shape)

        @pl.when(i == 0)
        def _():
            dg_ref[...] = part_dg
            loss_ref[...] = part_loss

        @pl.when(i > 0)
        def _():
            dg_ref[...] += part_dg
            loss_ref[...] += part_loss

    return pl.pallas_call(
        body, name=name, grid=(t // tm,),
        in_specs=[pl.BlockSpec((tm, d), lambda i: (i, 0)),
                  pl.BlockSpec((1, d), lambda i: (0, 0)),
                  pl.BlockSpec((tm, d), lambda i: (i, 0))],
        out_specs=[pl.BlockSpec((tm, d), lambda i: (i, 0)),
                   pl.BlockSpec((SUBLANES, LANES), lambda i: (0, 0)),
                   pl.BlockSpec((1, d), lambda i: (0, 0))],
        out_shape=[jax.ShapeDtypeStruct((t, d), F32), jax.ShapeDtypeStruct((SUBLANES, LANES), F32),
                   jax.ShapeDtypeStruct((1, d), F32)],
        compiler_params=_params("arbitrary"),
    )(h, g, target)


def _adamw(w, g, m, v, name):
    rows, cols = w.shape
    tr = rows
    if rows > 512:
        for cand in range(8, 513, 8):
            if rows % cand == 0:
                tr = cand

    def body(w_ref, g_ref, m_ref, v_ref, d_ref, nm_ref, nv_ref):
        g_ = g_ref[...]
        m_ = ADAM_B1 * m_ref[...] + (1.0 - ADAM_B1) * g_
        v_ = ADAM_B2 * v_ref[...] + (1.0 - ADAM_B2) * (g_ * g_)
        m_hat = m_ / (1.0 - ADAM_B1 ** ADAM_STEP)
        v_hat = v_ / (1.0 - ADAM_B2 ** ADAM_STEP)
        d_ref[...] = -ADAM_LR * (m_hat / (jnp.sqrt(v_hat) + ADAM_EPS) + ADAM_WD * w_ref[...])
        nm_ref[...] = m_
        nv_ref[...] = v_

    spec = pl.BlockSpec((tr, cols), lambda i: (i, 0))
    shape = jax.ShapeDtypeStruct((rows, cols), F32)
    return pl.pallas_call(
        body, name=name, grid=(rows // tr,),
        in_specs=[spec] * 4, out_specs=[spec] * 3, out_shape=[shape] * 3,
        compiler_params=_params("parallel"),
    )(w, g, m, v)


def _adamw_nd(w, g, m, v, name):
    shape = w.shape
    two_d = (-1, shape[-1]) if w.ndim > 1 else (1, -1)
    outs = _adamw(w.reshape(two_d), g.reshape(two_d), m.reshape(two_d), v.reshape(two_d), name)
    return tuple(o.reshape(shape) for o in outs)


def _all_gather(blk, name):
    rows, cols = blk.shape

    def body(x_ref, out_ref, send_sems, recv_sems, local_sem):
        x, y, c = lax.axis_index("x"), lax.axis_index("y"), lax.axis_index("c")
        me, sibling = (x, y, c), (x, y, 1 - c)
        chips = [(1 - x, y), (x, 1 - y), (1 - x, 1 - y)]

        def slot(px, py, pc):
            return out_ref.at[4 * px + 2 * py + pc]

        def copy(k, block, to, src=None):
            return pltpu.make_async_remote_copy(
                src_ref=slot(*block) if src is None else src, dst_ref=slot(*block),
                send_sem=send_sems.at[k], recv_sem=recv_sems.at[k], device_id=to, device_id_type=_MESH)

        mine = pltpu.make_async_copy(x_ref, slot(*me), local_sem)
        mine.start()
        first = [copy(0, me, sibling, src=x_ref)]
        first += [copy(1 + j, me, (*chip, c), src=x_ref) for j, chip in enumerate(chips)]
        for cp in first:
            cp.start()
        passed = [copy(4 + j, (*chip, c), sibling) for j, chip in enumerate(chips)]
        for j, chip in enumerate(chips):
            copy(1 + j, (*chip, c), me).wait_recv()
            passed[j].start()
        copy(0, sibling, me).wait_recv()
        for j, chip in enumerate(chips):
            copy(4 + j, (*chip, 1 - c), me).wait_recv()
        for cp in first + passed:
            cp.wait_send()
        mine.wait()

    return pl.pallas_call(
        body, name=name,
        in_specs=[_ANY], out_specs=_ANY,
        out_shape=jax.ShapeDtypeStruct((N_DEV, rows, cols), blk.dtype),
        scratch_shapes=[pltpu.SemaphoreType.DMA((7,)), pltpu.SemaphoreType.DMA((7,)), pltpu.SemaphoreType.DMA(())],
    )(blk)


def _rs_sibling_exchange(buf, name):
    _, rows, cols = buf.shape

    def body(buf_ref, out_ref, send_sems, recv_sems):
        x, y, c = lax.axis_index("x"), lax.axis_index("y"), lax.axis_index("c")
        copies = [pltpu.make_async_remote_copy(
            src_ref=buf_ref.at[2 * k + (1 - c)], dst_ref=out_ref.at[k],
            send_sem=send_sems.at[k], recv_sem=recv_sems.at[k],
            device_id=(x, y, 1 - c), device_id_type=_MESH) for k in range(4)]
        for cp in copies:
            cp.start()
        for cp in copies:
            cp.wait()

    return pl.pallas_call(
        body, name=name,
        in_specs=[_ANY], out_specs=_ANY,
        out_shape=jax.ShapeDtypeStruct((4, rows, cols), buf.dtype),
        scratch_shapes=[pltpu.SemaphoreType.DMA((4,)), pltpu.SemaphoreType.DMA((4,))],
    )(buf)


def _rs_chip_sum(buf, recv, core, name):
    _, rows, cols = buf.shape
    tr = _row_tile(rows, 512) if rows % 16 == 0 else rows

    def body(c_ref, a_ref, b_ref, o_ref):
        o_ref[...] = a_ref[...] + b_ref[...]

    grid_spec = pltpu.PrefetchScalarGridSpec(
        num_scalar_prefetch=1, grid=(4, rows // tr),
        in_specs=[pl.BlockSpec((None, tr, cols), lambda k, j, c: (2 * k + c[0], j, 0)),
                  pl.BlockSpec((None, tr, cols), lambda k, j, c: (k, j, 0))],
        out_specs=pl.BlockSpec((None, tr, cols), lambda k, j, c: (k, j, 0)))
    return pl.pallas_call(
        body, name=name, grid_spec=grid_spec,
        out_shape=jax.ShapeDtypeStruct((4, rows, cols), buf.dtype),
        compiler_params=_params("parallel", "parallel"),
    )(core, buf, recv)


def _rs_chip_exchange(chip_sum, name):
    _, rows, cols = chip_sum.shape

    def body(cs_ref, out_ref, send_sems, recv_sems, local_sem):
        x, y, c = lax.axis_index("x"), lax.axis_index("y"), lax.axis_index("c")
        my_chip = 2 * x + y
        mine = pltpu.make_async_copy(cs_ref.at[my_chip], out_ref.at[my_chip], local_sem)
        mine.start()
        chips = [(1 - x, y), (x, 1 - y), (1 - x, 1 - y)]
        copies = [pltpu.make_async_remote_copy(
            src_ref=cs_ref.at[2 * px + py], dst_ref=out_ref.at[my_chip],
            send_sem=send_sems.at[j], recv_sem=recv_sems.at[j],
            device_id=(px, py, c), device_id_type=_MESH) for j, (px, py) in enumerate(chips)]
        for cp in copies:
            cp.start()
        for cp in copies:
            cp.wait()
        mine.wait()

    return pl.pallas_call(
        body, name=name,
        in_specs=[_ANY], out_specs=_ANY,
        out_shape=jax.ShapeDtypeStruct((4, rows, cols), chip_sum.dtype),
        scratch_shapes=[pltpu.SemaphoreType.DMA((3,)), pltpu.SemaphoreType.DMA((3,)), pltpu.SemaphoreType.DMA(())],
    )(chip_sum)


def _sum_slots(parts, name):
    slots, rows, cols = parts.shape
    tr = _row_tile(rows, 256) if rows % 16 == 0 else rows

    def body(p_ref, o_ref):
        acc = p_ref[0]
        for s in range(1, slots):
            acc = acc + p_ref[s]
        o_ref[...] = acc

    return pl.pallas_call(
        body, name=name, grid=(rows // tr,),
        in_specs=[pl.BlockSpec((slots, tr, cols), lambda i: (0, i, 0))],
        out_specs=pl.BlockSpec((tr, cols), lambda i: (i, 0)),
        out_shape=jax.ShapeDtypeStruct((rows, cols), parts.dtype),
        compiler_params=_params("parallel"),
    )(parts)


def _reduce_scatter(buf, core):
    recv = _rs_sibling_exchange(buf, "rs_sibling_exchange")
    chip_sum = _rs_chip_sum(buf, recv, core, "rs_chip_sum")
    parts = _rs_chip_exchange(chip_sum, "rs_chip_exchange")
    return _sum_slots(parts, "rs_sum_slots")


def _pad_rows(flat, cols, multiple):
    n = flat.shape[0]
    rows = -(-n // cols)
    rows = -(-rows // multiple) * multiple
    return jnp.pad(flat, (0, rows * cols - n)).reshape(rows, cols)


def _cols_to_chunks(full):
    lead = full.shape[:-1]
    c = full.shape[-1] // N_DEV
    x = full.reshape(-1, N_DEV, c)
    return jnp.transpose(x, (1, 0, 2)).reshape(N_DEV, -1)


def _chunks_to_cols(chunks, lead):
    n = 1
    for s in lead:
        n *= s
    c = chunks.shape[1] // n
    x = chunks.reshape(N_DEV, n, c)
    return jnp.transpose(x, (1, 0, 2)).reshape(tuple(lead) + (N_DEV * c,))


def kernel(x, meta_tokens, norm_mix_g, norm_ffn_g, final_norm_g, sc_w_in, sc_conv_w, sc_w_out, rg_w_in, rg_conv_w, rg_conv_b, rg_w_gate_a, rg_b_gate_a, rg_w_gate_x, rg_b_gate_x, rg_lambda, rg_w_out, ffn_w_up, ffn_conv_w, ffn_w_down, loss_target, m_meta_tokens, m_norm_mix_g, m_norm_ffn_g, m_final_norm_g, m_sc_w_in, m_sc_conv_w, m_sc_w_out, m_rg_w_in, m_rg_conv_w, m_rg_conv_b, m_rg_w_gate_a, m_rg_b_gate_a, m_rg_w_gate_x, m_rg_b_gate_x, m_rg_lambda, m_rg_w_out, m_ffn_w_up, m_ffn_conv_w, m_ffn_w_down, v_meta_tokens, v_norm_mix_g, v_norm_ffn_g, v_final_norm_g, v_sc_w_in, v_sc_conv_w, v_sc_w_out, v_rg_w_in, v_rg_conv_w, v_rg_conv_b, v_rg_w_gate_a, v_rg_b_gate_a, v_rg_w_gate_x, v_rg_b_gate_x, v_rg_lambda, v_rg_w_out, v_ffn_w_up, v_ffn_conv_w, v_ffn_w_down):
    weights = dict(meta_tokens=meta_tokens, norm_mix_g=norm_mix_g, norm_ffn_g=norm_ffn_g, final_norm_g=final_norm_g,
                   sc_w_in=sc_w_in, sc_conv_w=sc_conv_w, sc_w_out=sc_w_out, rg_w_in=rg_w_in, rg_conv_w=rg_conv_w,
                   rg_conv_b=rg_conv_b, rg_w_gate_a=rg_w_gate_a, rg_b_gate_a=rg_b_gate_a, rg_w_gate_x=rg_w_gate_x,
                   rg_b_gate_x=rg_b_gate_x, rg_lambda=rg_lambda, rg_w_out=rg_w_out, ffn_w_up=ffn_w_up,
                   ffn_conv_w=ffn_conv_w, ffn_w_down=ffn_w_down)
    m_in = dict(meta_tokens=m_meta_tokens, norm_mix_g=m_norm_mix_g, norm_ffn_g=m_norm_ffn_g, final_norm_g=m_final_norm_g,
                sc_w_in=m_sc_w_in, sc_conv_w=m_sc_conv_w, sc_w_out=m_sc_w_out, rg_w_in=m_rg_w_in, rg_conv_w=m_rg_conv_w,
                rg_conv_b=m_rg_conv_b, rg_w_gate_a=m_rg_w_gate_a, rg_b_gate_a=m_rg_b_gate_a, rg_w_gate_x=m_rg_w_gate_x,
                rg_b_gate_x=m_rg_b_gate_x, rg_lambda=m_rg_lambda, rg_w_out=m_rg_w_out, ffn_w_up=m_ffn_w_up,
                ffn_conv_w=m_ffn_conv_w, ffn_w_down=m_ffn_w_down)
    v_in = dict(meta_tokens=v_meta_tokens, norm_mix_g=v_norm_mix_g, norm_ffn_g=v_norm_ffn_g, final_norm_g=v_final_norm_g,
                sc_w_in=v_sc_w_in, sc_conv_w=v_sc_conv_w, sc_w_out=v_sc_w_out, rg_w_in=v_rg_w_in, rg_conv_w=v_rg_conv_w,
                rg_conv_b=v_rg_conv_b, rg_w_gate_a=v_rg_w_gate_a, rg_b_gate_a=v_rg_b_gate_a, rg_w_gate_x=v_rg_w_gate_x,
                rg_b_gate_x=v_rg_b_gate_x, rg_lambda=v_rg_lambda, rg_w_out=v_rg_w_out, ffn_w_up=v_ffn_w_up,
                ffn_conv_w=v_ffn_conv_w, ffn_w_down=v_ffn_w_down)
    names = list(weights)

    seq, d = x.shape[1], x.shape[2]
    n_meta = meta_tokens.shape[0]
    n_ffn = ffn_w_up.shape[0]
    rr = rg_w_gate_a.shape[1] * rg_w_gate_a.shape[2]
    core = lax.axis_index("c").astype(jnp.int32).reshape(1)

    assert n_ffn == 2
    def shard_rows(w_in, w_out):
        return jnp.concatenate([w_in.T, w_out], axis=0).astype(MXU_DTYPE)

    def both_orientations(gathered, w_in):
        r_in = w_in.shape[1]
        w_in_t = gathered[:, :r_in, :].reshape(N_DEV * r_in, d)
        w_out = gathered[:, r_in:, :].reshape(-1, d)
        return w_in_t.T, w_in_t, w_out, w_out.T

    shards = {"ffn0": shard_rows(ffn_w_up[0], ffn_w_down[0]), "rg": shard_rows(rg_w_in[0], rg_w_out[0]),
              "ffn1": shard_rows(ffn_w_up[1], ffn_w_down[1])}
    sc_w_in_f, sc_w_in_t, sc_w_out_f, sc_w_out_t = both_orientations(
        _all_gather(shard_rows(sc_w_in[0], sc_w_out[0]), "gather_sc"), sc_w_in[0])
    ffn_w_up_f, ffn_w_up_t, ffn_w_down_f, ffn_w_down_t = [None] * 2, [None] * 2, [None] * 2, [None] * 2

    small_names = ["meta_tokens", "sc_conv_w", "rg_conv_w", "rg_conv_b", "rg_b_gate_a", "rg_b_gate_x", "rg_lambda",
                   "ffn_conv_w"]
    small_lead = {n: weights[n].shape[:-1] for n in small_names}
    small_sizes = [weights[n].size for n in small_names]
    small_flat = jnp.concatenate([weights[n].reshape(-1) for n in small_names])
    small_rows = _pad_rows(small_flat, d, SUBLANES)
    small_g = _all_gather(small_rows, "gather_small").reshape(N_DEV, -1)
    small_full = {}
    o = 0
    for n, sz in zip(small_names, small_sizes):
        small_full[n] = _chunks_to_cols(small_g[:, o:o + sz], small_lead[n])
        o += sz

    wa = rg_w_gate_a[0].astype(MXU_DTYPE)
    wx = rg_w_gate_x[0].astype(MXU_DTYPE)
    rg_cw, rg_cb = small_full["rg_conv_w"][0], small_full["rg_conv_b"]
    rg_ba, rg_bx, rg_lam = small_full["rg_b_gate_a"], small_full["rg_b_gate_x"], small_full["rg_lambda"]
    sc_cw = small_full["sc_conv_w"][0]
    ffn_cw = small_full["ffn_conv_w"]

    tp = _row_tile(n_meta + seq, ROW_TILE_PERM)
    h0 = _to_tiles(jnp.concatenate([small_full["meta_tokens"], x[0]], axis=0), tp)
    target = _to_tiles(jnp.concatenate([jnp.zeros((n_meta, d), F32), loss_target[0]], axis=0), tp)

    def ffn_fwd(h, l, rider):
        n, a, z, *gathered = _ffn_up_fused(h, norm_ffn_g[l:l + 1], ffn_w_up_f[l], ffn_cw[l], tp, f"ffn{l}_up", rider)
        return _matmul_residual(z, ffn_w_down_f[l], h, f"ffn{l}_down"), (n, a, z), gathered

    n0, p0, q0, ffn0_w = _sc_in_fused(h0, norm_mix_g[0:1], sc_w_in_f, sc_cw, tp, "sc_in", _GatherRider(shards["ffn0"]))
    ffn_w_up_f[0], ffn_w_up_t[0], ffn_w_down_f[0], ffn_w_down_t[0] = both_orientations(ffn0_w, ffn_w_up[0])
    h1 = _matmul_residual(q0, sc_w_out_f, h0, "sc_out")
    h2, ffn0_saved, (rg_w,) = ffn_fwd(h1, 0, _GatherRider(shards["rg"]))
    rg_w_in_f, rg_w_in_t, rg_w_out_f, rg_w_out_t = both_orientations(rg_w, rg_w_in[0])
    n2, p2, y2, hs2, ffn1_w = _rg_in_fused(h2, norm_mix_g[1:2], rg_w_in_f, rg_cw, rg_cb, wa, rg_ba, wx, rg_bx, rg_lam,
                                           tp, "rg_in", _GatherRider(shards["ffn1"]))
    ffn_w_up_f[1], ffn_w_up_t[1], ffn_w_down_f[1], ffn_w_down_t[1] = both_orientations(ffn1_w, ffn_w_up[1])
    h3 = _matmul_residual(y2, rg_w_out_f, h2, "rg_out")
    h4, ffn1_saved, _ = ffn_fwd(h3, 1, _NO_RIDER)

    dh4, loss_tile, d_final_g = _loss_head(h4, final_norm_g.reshape(1, d), target, n_meta, tp, "loss_head")
    loss = lax.psum(loss_tile[0, 0], AXES)

    arrived = {}

    def ffn_bwd(dh_out, h_in, saved, l):
        n, a, z = saved
        da, dhb, d_cw = _ffn_down_bwd_fused(dh_out, ffn_w_down_t[l], a, ffn_cw[l], tp, f"ffn{l}_down_bwd")
        d_w_down = _wgrad(z, dhb, f"ffn{l}_down_wgrad")
        d_w_up_t, arrived[f"ffn_w_down{l}"] = _wgrad(da, n, f"ffn{l}_up_wgrad", _ScatterRider(d_w_down))
        dh_in, d_g, arrived[f"ffn_w_up{l}"] = _dgrad_in_norm(
            da, ffn_w_up_t[l], h_in, norm_ffn_g[l:l + 1], dh_out, f"ffn{l}_up_dgrad", _ScatterRider(d_w_up_t))
        return dh_in, d_cw, d_g

    dh3, d_fcw1, d_fg1 = ffn_bwd(dh4, h3, ffn1_saved, 1)

    dp2, dhb3, d_rg_cw, d_rg_cb, d_wa, d_rg_ba, d_wx, d_rg_bx, d_rg_lam = _rg_out_bwd_fused(
        dh3, rg_w_out_t, p2, hs2, rg_cw, rg_cb, wa, rg_ba, wx, rg_bx, rg_lam, tp, "rg_out_bwd")
    d_rg_w_out = _wgrad(y2, dhb3, "rg_out_wgrad")
    d_rg_w_in_t, arrived["rg_w_out"] = _wgrad(dp2, n2, "rg_in_wgrad", _ScatterRider(d_rg_w_out))
    dh2, d_mg1, arrived["rg_w_in"] = _dgrad_in_norm(
        dp2, rg_w_in_t, h2, norm_mix_g[1:2], dh3, "rg_in_dgrad", _ScatterRider(d_rg_w_in_t))

    dh1, d_fcw0, d_fg0 = ffn_bwd(dh2, h1, ffn0_saved, 0)

    dp0, dhb1, d_sc_cw = _sc_out_bwd_fused(dh1, sc_w_out_t, p0, sc_cw, tp, "sc_out_bwd")
    d_sc_w_out = _wgrad(q0, dhb1, "sc_out_wgrad")
    d_sc_w_in_t, arrived["sc_w_out"] = _wgrad(dp0, n0, "sc_in_wgrad", _ScatterRider(d_sc_w_out))
    dh0, d_mg0, arrived["sc_w_in"] = _dgrad_in_norm(
        dp0, sc_w_in_t, h0, norm_mix_g[0:1], dh1, "sc_in_dgrad", _ScatterRider(d_sc_w_in_t))

    dh0 = _from_tiles(dh0, tp)
    grad_x = dh0[n_meta:][None]

    grads = {}
    for n in ("sc_w_in", "rg_w_in"):
        grads[n] = _sum_slots(arrived[n], f"sum_{n}").T[None]
    for n in ("sc_w_out", "rg_w_out"):
        grads[n] = _sum_slots(arrived[n], f"sum_{n}")[None]
    grads["ffn_w_up"] = jnp.stack([_sum_slots(arrived[f"ffn_w_up{l}"], f"sum_ffn_w_up{l}").T for l in range(n_ffn)])
    grads["ffn_w_down"] = jnp.stack([_sum_slots(arrived[f"ffn_w_down{l}"], f"sum_ffn_w_down{l}") for l in range(n_ffn)])

    small_grads = {"meta_tokens": dh0[:n_meta], "sc_conv_w": d_sc_cw[None], "rg_conv_w": d_rg_cw[None],
                   "rg_conv_b": d_rg_cb, "rg_b_gate_a": d_rg_ba, "rg_b_gate_x": d_rg_bx, "rg_lambda": d_rg_lam,
                   "ffn_conv_w": jnp.stack([d_fcw0, d_fcw1])}
    small_chunks = jnp.concatenate([_cols_to_chunks(small_grads[n]) for n in small_names], axis=1)
    pad = small_rows.shape[0] * d - small_chunks.shape[1]
    small_chunks = jnp.pad(small_chunks, ((0, 0), (0, pad))).reshape(N_DEV, small_rows.shape[0], d)
    rep_names = ["norm_mix_g", "norm_ffn_g", "final_norm_g", "rg_w_gate_a", "rg_w_gate_x"]
    rep_grads = {"norm_mix_g": jnp.concatenate([d_mg0, d_mg1], axis=0),
                 "norm_ffn_g": jnp.concatenate([d_fg0, d_fg1], axis=0),
                 "final_norm_g": d_final_g.reshape(-1), "rg_w_gate_a": d_wa[None], "rg_w_gate_x": d_wx[None]}
    rep_flat = jnp.concatenate([rep_grads[n].reshape(-1) for n in rep_names])
    rep_chunk_rows = -(-rep_flat.shape[0] // (N_DEV * d))
    rep_chunk_rows += -(small_rows.shape[0] + rep_chunk_rows) % 16
    rep_chunks = jnp.pad(rep_flat, (0, N_DEV * rep_chunk_rows * d - rep_flat.shape[0])).reshape(N_DEV, rep_chunk_rows, d)

    reduced = _reduce_scatter(jnp.concatenate([small_chunks, rep_chunks], axis=1), core)

    small_red = reduced[0:small_rows.shape[0]].reshape(-1)
    o = small_rows.shape[0]
    so = 0
    for n, sz in zip(small_names, small_sizes):
        grads[n] = small_red[so:so + sz].reshape(weights[n].shape)
        so += sz
    rep_red = _all_gather(reduced[o:o + rep_chunks.shape[1]], "gather_replicated_grads").reshape(-1)
    ro = 0
    for n in rep_names:
        sz = weights[n].size
        grads[n] = rep_red[ro:ro + sz].reshape(weights[n].shape)
        ro += sz

    delta, new_m, new_v = {}, {}, {}
    for n in names:
        delta[n], new_m[n], new_v[n] = _adamw_nd(weights[n], grads[n], m_in[n], v_in[n], f"adamw_{n}")

    return (loss, grad_x, *[grads[n] for n in names], *[delta[n] for n in names],
            *[new_m[n] for n in names], *[new_v[n] for n in names])
```

```python
import jax
import jax.numpy as jnp
from jax import lax
from jax.experimental import pallas as pl
from jax.experimental.pallas import tpu as pltpu

F32 = jnp.float32
MXU_DTYPE = jnp.bfloat16
RMS_EPS = 1e-6
RG_C = 8.0
ADAM_LR = 0.001
ADAM_B1 = 0.9
ADAM_B2 = 0.999
ADAM_EPS = 1e-08
ADAM_WD = 0.01
ADAM_STEP = 10

N_DEV = 8
AXES = ("x", "y", "c")
SUBLANES = 8
LANES = 128
VMEM_LIMIT_BYTES = 48 * 1024 * 1024
ROW_TILE_MATMUL = 700
ROW_TILE_WGRAD = 3300
ROW_TILE_PERM = 400
STRIP = 256

_NT = (((1,), (1,)), ((), ()))
_TN = (((0,), (0,)), ((), ()))
_NN = (((1,), (0,)), ((), ()))


def _row_tile(t, target):
    best = None
    for tm in range(16, t + 1, 16):
        if t % tm == 0 and tm <= target:
            best = tm
    return best if best is not None else t


def _col_tile(n, target):
    best = None
    for tn in range(LANES, n + 1, LANES):
        if n % tn == 0 and tn <= target:
            best = tn
    return best if best is not None else n


def _params(*sem):
    return pltpu.CompilerParams(dimension_semantics=sem, vmem_limit_bytes=VMEM_LIMIT_BYTES)


def _dot(a, b, dims):
    return lax.dot_general(a, b, dims, preferred_element_type=F32)


def _sigmoid(x):
    return 1.0 / (1.0 + jnp.exp(-x))


def _sigmoid_tanh(x):
    return 0.5 * jnp.tanh(0.5 * x) + 0.5


def _gelu(x):
    c = 0.7978845608028654
    t = jnp.tanh(c * (x + 0.044715 * (x * x * x)))
    return 0.5 * x * (1.0 + t)


def _gelu_grad(x):
    c = 0.7978845608028654
    t = jnp.tanh(c * (x + 0.044715 * (x * x * x)))
    return 0.5 * (1.0 + t) + 0.5 * x * (1.0 - t * t) * c * (1.0 + 3.0 * 0.044715 * x * x)


def _softplus(x):
    return jnp.maximum(x, 0.0) + jnp.log1p(jnp.exp(-jnp.abs(x)))


def _to_tiles(a, tm):
    t, d = a.shape
    return a.reshape(t // tm, SUBLANES, tm // SUBLANES, d).transpose(0, 2, 1, 3).reshape(t, d)


def _from_tiles(a, tm):
    t, d = a.shape
    return a.reshape(t // tm, tm // SUBLANES, SUBLANES, d).transpose(0, 2, 1, 3).reshape(t, d)


def _time_scratch(tm, d):
    return [pltpu.VMEM((2, tm, d), F32), pltpu.VMEM((d // LANES, tm, LANES), F32), pltpu.SemaphoreType.DMA((2,))]


def _fetch_time_tile(i, nt, src_ref, head, tbuf, sems, tm):
    n_head = head.shape[0]

    def tile(j, slot):
        start = pl.multiple_of(j * tm - n_head, SUBLANES)
        return pltpu.make_async_copy(src_ref.at[pl.ds(start, tm)], tbuf.at[slot], sems.at[slot])

    first = pltpu.make_async_copy(src_ref.at[pl.ds(0, tm - n_head)], tbuf.at[0, pl.ds(n_head, tm - n_head)], sems.at[0])

    @pl.when(i == 0)
    def _():
        tbuf[0, 0:n_head, :] = head
        first.start()

    @pl.when(i + 1 < nt)
    def _():
        tile(i + 1, (i + 1) % 2).start()

    @pl.when(i == 0)
    def _():
        first.wait()

    @pl.when(i > 0)
    def _():
        tile(i, i % 2).wait()


def _time_to_tile_order(t_ref, slabs, out_ref):
    tm, d = out_ref.shape
    groups = tm // SUBLANES
    for k in range(d // LANES):
        slabs[k] = t_ref[:, k * LANES:(k + 1) * LANES]
    for k in range(d // LANES):
        for g in range(groups):
            out_ref[g * SUBLANES:(g + 1) * SUBLANES, k * LANES:(k + 1) * LANES] = slabs[k, pl.ds(g, SUBLANES, stride=groups), :]


def _tile_to_time_order(p_ref, slabs, t_ref):
    tm, d = p_ref.shape
    groups = tm // SUBLANES
    for k in range(d // LANES):
        for g in range(groups):
            slabs[k, pl.ds(g, SUBLANES, stride=groups), :] = p_ref[g * SUBLANES:(g + 1) * SUBLANES, k * LANES:(k + 1) * LANES]
    for k in range(d // LANES):
        t_ref[:, k * LANES:(k + 1) * LANES] = slabs[k]


def _rows(ref, sl):
    return [ref[k:k + 1, sl] for k in range(ref.shape[0])]


def _shift_down(x, before, s):
    if s == 0:
        return x
    n = x.shape[0]
    row = lax.broadcasted_iota(jnp.int32, (SUBLANES, x.shape[1]), 0)
    heads = []
    for g in range(s):
        v = x[n - (s - g) * SUBLANES:n - (s - g - 1) * SUBLANES]
        heads.append(pltpu.roll(jnp.where(row == SUBLANES - 1, before[s - g - 1], v), 1, axis=0))
    return jnp.concatenate(heads + [x[0:n - s * SUBLANES]], axis=0)


def _shift_up(x, after, s):
    if s == 0:
        return x
    row = lax.broadcasted_iota(jnp.int32, (SUBLANES, x.shape[1]), 0)
    tails = []
    for m in range(s):
        v = x[m * SUBLANES:(m + 1) * SUBLANES]
        tails.append(pltpu.roll(jnp.where(row == 0, after[m], v), SUBLANES - 1, axis=0))
    return jnp.concatenate([x[s * SUBLANES:]] + tails, axis=0)


def _weighted(w, windows):
    y = w[0] * windows[0]
    for k in range(1, len(w)):
        y = y + w[k] * windows[k]
    return y


def _windows(x, before, k_width):
    return [_shift_down(x, before, k_width - 1 - k) for k in range(k_width)]


def _conv_t(dy, after, w):
    k_width = len(w)
    return _weighted(w, [_shift_up(dy, after, k_width - 1 - k) for k in range(k_width)])


def _blocks(ref, sl, count, newest_first):
    n = ref.shape[0] // SUBLANES
    order = range(n - 1, n - 1 - count, -1) if newest_first else range(count)
    return [ref[b * SUBLANES:(b + 1) * SUBLANES, sl] for b in order]


def _halo_specs(tm, cols, count, tile_of):
    def spec(k):
        return pl.BlockSpec((SUBLANES, cols), lambda i: (jnp.maximum(tile_of(i) * (tm // SUBLANES) - k, 0), 0))
    return [spec(k) for k in range(1, count + 1)]


def _scan_tile(coef, val, out, carry, groups, reverse):
    cols = coef.shape[1]
    row = lax.broadcasted_iota(jnp.int32, (SUBLANES, cols), 0)

    def blk(i):
        g = groups - 1 - i if reverse else i
        return pl.ds(pl.multiple_of(g * SUBLANES, SUBLANES), SUBLANES)

    def local(i, pl_):
        p_prev, l_prev = pl_
        a = coef[blk(i), :]
        p = a * p_prev
        l = a * l_prev + val[blk(i), :]
        coef[blk(i), :] = p
        val[blk(i), :] = l
        return p, l

    pf, lf = lax.fori_loop(0, groups, local, (jnp.ones((SUBLANES, cols), F32), jnp.zeros((SUBLANES, cols), F32)))
    for s in (1, 2, 4):
        keep, sh = (row < SUBLANES - s, SUBLANES - s) if reverse else (row >= s, s)
        p_s = jnp.where(keep, pltpu.roll(pf, sh, axis=0), 1.0)
        l_s = jnp.where(keep, pltpu.roll(lf, sh, axis=0), 0.0)
        lf = pf * l_s + lf
        pf = pf * p_s
    end = lf + pf * carry
    if reverse:
        init = jnp.where(row == SUBLANES - 1, carry, pltpu.roll(end, SUBLANES - 1, axis=0))
        leaving = jnp.broadcast_to(end[0:1, :], (SUBLANES, cols))
    else:
        init = jnp.where(row == 0, carry, pltpu.roll(end, 1, axis=0))
        leaving = jnp.broadcast_to(end[SUBLANES - 1:SUBLANES, :], (SUBLANES, cols))

    def fix(i, _):
        out[blk(i), :] = val[blk(i), :] + coef[blk(i), :] * init
        return 0

    lax.fori_loop(0, groups, fix, 0)
    return leaving


def _resident(shape):
    return pl.BlockSpec(shape, lambda *_: (0,) * len(shape), pipeline_mode=pl.Buffered(1))


def _full(shape):
    return pl.BlockSpec(shape, lambda *_: (0,) * len(shape))


def _rmsnorm_to(h_ref, g_ref, n_ref):
    x = h_ref[...]
    ms = jnp.mean(x * x, axis=-1, keepdims=True)
    n_ref[...] = (x * lax.rsqrt(ms + RMS_EPS) * g_ref[...]).astype(n_ref.dtype)


_ANY = pl.BlockSpec(memory_space=pl.ANY)
_MESH = pl.DeviceIdType.MESH


def _dma_sems(n):
    return pltpu.SemaphoreType.DMA((n,))


def _when_each(*phases):
    for cond, fn in phases:
        if cond is not None:
            pl.when(cond)(fn)


class _NoRider:
    inputs = in_specs = out_shape = out_specs = scratch_shapes = ()

    def __call__(self, first, middle, last, ins, outs, scratch):
        pass


_NO_RIDER = _NoRider()


def _split_refs(rider, n_out, n_scratch, rest):
    a = len(rider.inputs)
    b = a + n_out
    c = b + len(rider.out_shape)
    e = c + n_scratch
    return rest[:a], rest[a:b], rest[b:c], rest[c:e], rest[e:]


class _GatherRider:
    def __init__(self, blk):
        self.inputs = (blk,)
        self.in_specs = (_ANY,)
        self.out_shape = (jax.ShapeDtypeStruct((N_DEV,) + blk.shape, blk.dtype),)
        self.out_specs = (_ANY,)
        self.scratch_shapes = (_dma_sems(7), _dma_sems(7), pltpu.SemaphoreType.DMA(()))

    def __call__(self, first, middle, last, ins, outs, scratch):
        (x_ref,), (out_ref,) = ins, outs
        send_sems, recv_sems, local_sem = scratch
        x, y, c = lax.axis_index("x"), lax.axis_index("y"), lax.axis_index("c")
        me, sibling = (x, y, c), (x, y, 1 - c)
        chips = [(1 - x, y), (x, 1 - y), (1 - x, 1 - y)]

        def slot(px, py, pc):
            return out_ref.at[4 * px + 2 * py + pc]

        def copy(k, block, to, src=None):
            return pltpu.make_async_remote_copy(
                src_ref=slot(*block) if src is None else src, dst_ref=slot(*block),
                send_sem=send_sems.at[k], recv_sem=recv_sems.at[k], device_id=to, device_id_type=_MESH)

        mine = pltpu.make_async_copy(x_ref, slot(*me), local_sem)
        own = [copy(0, me, sibling, src=x_ref)]
        own += [copy(1 + j, me, (*chip, c), src=x_ref) for j, chip in enumerate(chips)]
        passed = [copy(4 + j, (*chip, c), sibling) for j, chip in enumerate(chips)]

        def at_first():
            mine.start()
            for cp in own:
                cp.start()

        def at_middle():
            for j, chip in enumerate(chips):
                copy(1 + j, (*chip, c), me).wait_recv()
                passed[j].start()

        def at_last():
            copy(0, sibling, me).wait_recv()
            for j, chip in enumerate(chips):
                copy(4 + j, (*chip, 1 - c), me).wait_recv()
            for cp in own + passed:
                cp.wait_send()
            mine.wait()

        _when_each((first, at_first), (middle, at_middle), (last, at_last))


class _ScatterRider:
    def __init__(self, grad):
        r = grad.shape[0] // N_DEV
        chunks = grad.reshape(N_DEV, r, grad.shape[1])
        self.inputs = (chunks,)
        self.in_specs = (_ANY,)
        self.out_shape = (jax.ShapeDtypeStruct(chunks.shape, chunks.dtype),)
        self.out_specs = (_ANY,)
        self.scratch_shapes = (_dma_sems(N_DEV - 1), _dma_sems(N_DEV - 1), pltpu.SemaphoreType.DMA(()))

    def __call__(self, first, middle, last, ins, outs, scratch):
        (g_ref,), (r_ref,) = ins, outs
        send_sems, recv_sems, local_sem = scratch
        x, y, c = lax.axis_index("x"), lax.axis_index("y"), lax.axis_index("c")
        me = 4 * x + 2 * y + c
        mine = pltpu.make_async_copy(g_ref.at[me], r_ref.at[me], local_sem)
        copies = []
        for k in range(1, N_DEV):
            px, py, pc = (1 - x if k & 4 else x), (1 - y if k & 2 else y), (1 - c if k & 1 else c)
            copies.append(pltpu.make_async_remote_copy(
                src_ref=g_ref.at[4 * px + 2 * py + pc], dst_ref=r_ref.at[me],
                send_sem=send_sems.at[k - 1], recv_sem=recv_sems.at[k - 1],
                device_id=(px, py, pc), device_id_type=_MESH))

        def at_first():
            mine.start()
            for cp in copies:
                cp.start()

        def at_last():
            for cp in copies:
                cp.wait()
            mine.wait()

        _when_each((first, at_first), (last, at_last))


def _matmul_residual(q, w, h, name):
    t, k = q.shape
    d = w.shape[1]
    tm = _row_tile(t, ROW_TILE_MATMUL)

    def body(q_ref, w_ref, h_ref, o_ref):
        o_ref[...] = h_ref[...] + _dot(q_ref[...], w_ref[...], _NN)

    return pl.pallas_call(
        body, name=name, grid=(t // tm,),
        in_specs=[pl.BlockSpec((tm, k), lambda i: (i, 0)),
                  _resident((k, d)),
                  pl.BlockSpec((tm, d), lambda i: (i, 0))],
        out_specs=pl.BlockSpec((tm, d), lambda i: (i, 0)),
        out_shape=jax.ShapeDtypeStruct((t, d), F32),
        compiler_params=_params("parallel"),
    )(q, w, h)


def _dgrad_in_norm(dp, wt, h, g, dh_next, name, rider=_NO_RIDER):
    t, n = dp.shape
    d = wt.shape[1]
    tm = _row_tile(t, ROW_TILE_MATMUL)
    nt = t // tm

    def body(dp_ref, wt_ref, h_ref, g_ref, dhn_ref, *rest):
        r_in, (dh_ref, dg_ref), r_out, _, r_scratch = _split_refs(rider, 2, 0, rest)
        i = pl.program_id(0)
        rider(i == 0, i == nt // 2, None, r_in, r_out, r_scratch)
        dn = _dot(dp_ref[...], wt_ref[...], _NN)
        x = h_ref[...]
        ms = jnp.mean(x * x, axis=-1, keepdims=True)
        r = lax.rsqrt(ms + RMS_EPS)
        xhat = x * r
        dng = dn * g_ref[...]
        c = jnp.mean(dng * xhat, axis=-1, keepdims=True)
        dh_ref[...] = dhn_ref[...] + r * (dng - xhat * c)
        part = jnp.sum(dn * xhat, axis=0, keepdims=True)

        @pl.when(i == 0)
        def _():
            dg_ref[...] = part

        @pl.when(i > 0)
        def _():
            dg_ref[...] += part

        rider(None, None, i == nt - 1, r_in, r_out, r_scratch)

    return pl.pallas_call(
        body, name=name, grid=(nt,),
        in_specs=[pl.BlockSpec((tm, n), lambda i: (i, 0)),
                  _resident((n, d)),
                  pl.BlockSpec((tm, d), lambda i: (i, 0)),
                  pl.BlockSpec((1, d), lambda i: (0, 0)),
                  pl.BlockSpec((tm, d), lambda i: (i, 0))] + list(rider.in_specs),
        out_specs=[pl.BlockSpec((tm, d), lambda i: (i, 0)),
                   pl.BlockSpec((1, d), lambda i: (0, 0))] + list(rider.out_specs),
        out_shape=[jax.ShapeDtypeStruct((t, d), F32), jax.ShapeDtypeStruct((1, d), F32)] + list(rider.out_shape),
        scratch_shapes=list(rider.scratch_shapes),
        compiler_params=_params("arbitrary"),
    )(dp, wt, h, g, dh_next, *rider.inputs)


def _dgrad_in_to_input(dp, wt, h, g, dh_next, n_head, tm, name, rider=_NO_RIDER):
    t, n = dp.shape
    d = wt.shape[1]
    nt = t // tm

    def body(dp_ref, wt_ref, h_ref, g_ref, dhn_ref, *rest):
        r_in, (dg_ref, head_ref, rest_ref), r_out, (pbuf, tout, slabs, sems), r_scratch = _split_refs(rider, 3, 4, rest)
        i = pl.program_id(0)
        rider(i == 0, i == nt // 2, None, r_in, r_out, r_scratch)
        dn = _dot(dp_ref[...], wt_ref[...], _NN)
        x = h_ref[...]
        ms = jnp.mean(x * x, axis=-1, keepdims=True)
        r = lax.rsqrt(ms + RMS_EPS)
        xhat = x * r
        dng = dn * g_ref[...]
        c = jnp.mean(dng * xhat, axis=-1, keepdims=True)
        pbuf[...] = dhn_ref[...] + r * (dng - xhat * c)
        part = jnp.sum(dn * xhat, axis=0, keepdims=True)

        @pl.when(i == 0)
        def _():
            dg_ref[...] = part

        @pl.when(i > 0)
        def _():
            dg_ref[...] += part

        def store(j):
            if isinstance(j, int) and j == 0:
                return pltpu.make_async_copy(tout.at[0, pl.ds(n_head, tm - n_head)],
                                             rest_ref.at[pl.ds(0, tm - n_head)], sems.at[0])
            start = j * tm - n_head
            start = start if isinstance(start, int) else pl.multiple_of(start, SUBLANES)
            return pltpu.make_async_copy(tout.at[j % 2], rest_ref.at[pl.ds(start, tm)], sems.at[j % 2])

        @pl.when(i == 2)
        def _():
            store(0).wait()

        @pl.when(i > 2)
        def _():
            store(i - 2).wait()

        _tile_to_time_order(pbuf, slabs, tout.at[i % 2])

        @pl.when(i == 0)
        def _():
            head_ref[...] = tout[0, 0:n_head, :]
            store(0).start()

        @pl.when(i > 0)
        def _():
            store(i).start()

        @pl.when(i == nt - 1)
        def _():
            for j in (nt - 2, nt - 1):
                if j >= 0:
                    store(j).wait()

        rider(None, None, i == nt - 1, r_in, r_out, r_scratch)

    return pl.pallas_call(
        body, name=name, grid=(nt,),
        in_specs=[pl.BlockSpec((tm, n), lambda i: (i, 0)),
                  _resident((n, d)),
                  pl.BlockSpec((tm, d), lambda i: (i, 0)),
                  pl.BlockSpec((1, d), lambda i: (0, 0)),
                  pl.BlockSpec((tm, d), lambda i: (i, 0))] + list(rider.in_specs),
        out_specs=[pl.BlockSpec((1, d), lambda i: (0, 0)), _full((n_head, d)), _ANY] + list(rider.out_specs),
        out_shape=[jax.ShapeDtypeStruct((1, d), F32), jax.ShapeDtypeStruct((n_head, d), F32),
                   jax.ShapeDtypeStruct((t - n_head, d), F32)] + list(rider.out_shape),
        scratch_shapes=[pltpu.VMEM((tm, d), F32)] + _time_scratch(tm, d) + list(rider.scratch_shapes),
        compiler_params=_params("arbitrary"),
    )(dp, wt, h, g, dh_next, *rider.inputs)


def _wgrad(a, b, name, rider=_NO_RIDER):
    t, m = a.shape
    d = b.shape[1]
    tmm, tk = _col_tile(m, 1024), _row_tile(t, ROW_TILE_WGRAD)
    if tmm < 512:
        tmm, tk = _col_tile(m, 1536), _row_tile(t, ROW_TILE_MATMUL)
    nm, nk = m // tmm, t // tk

    def body(a_ref, b_ref, *rest):
        r_in, (o_ref,), r_out, _, r_scratch = _split_refs(rider, 1, 0, rest)
        i, k = pl.program_id(0), pl.program_id(1)
        rider(jnp.logical_and(i == 0, k == 0), None, None, r_in, r_out, r_scratch)

        @pl.when(k == 0)
        def _():
            o_ref[...] = jnp.zeros_like(o_ref)

        o_ref[...] += _dot(a_ref[...], b_ref[...], _TN)
        rider(None, None, jnp.logical_and(i == nm - 1, k == nk - 1), r_in, r_out, r_scratch)

    outs = pl.pallas_call(
        body, name=name, grid=(nm, nk),
        in_specs=[pl.BlockSpec((tk, tmm), lambda i, k: (k, i)),
                  pl.BlockSpec((tk, d), lambda i, k: (k, 0))] + list(rider.in_specs),
        out_specs=[pl.BlockSpec((tmm, d), lambda i, k: (i, 0))] + list(rider.out_specs),
        out_shape=[jax.ShapeDtypeStruct((m, d), F32)] + list(rider.out_shape),
        scratch_shapes=list(rider.scratch_shapes),
        compiler_params=_params("arbitrary", "arbitrary"),
    )(a, b, *rider.inputs)
    return outs if rider.out_shape else outs[0]


def _ffn_up_fused(h, g, w, conv_w, tm, name, rider=_NO_RIDER):
    t, d = h.shape
    f2 = w.shape[1]
    f = f2 // 2
    cw = min(STRIP, f)
    kw = conv_w.shape[0]
    nh = kw - 1
    nt = t // tm

    def body(h_ref, g_ref, w_ref, cw_ref, *rest):
        r_in, (n_ref, a_ref, z_ref), r_out, (carry,), r_scratch = _split_refs(rider, 3, 1, rest)
        i = pl.program_id(0)
        rider(i == 0, i == nt // 2, None, r_in, r_out, r_scratch)

        @pl.when(i == 0)
        def _():
            carry[...] = jnp.zeros_like(carry)

        _rmsnorm_to(h_ref, g_ref, n_ref)
        for c in range(0, f, cw):
            conv = []
            for sl in (slice(c, c + cw), slice(f + c, f + c + cw)):
                a = _dot(n_ref[...], w_ref[:, sl], _NN)
                a_ref[:, sl] = a
                conv.append(_weighted(_rows(cw_ref, sl), _windows(a, _blocks(carry, sl, nh, True), kw)))
                carry[:, sl] = a[tm - nh * SUBLANES:tm]
            gg, vv = conv
            z_ref[:, c:c + cw] = (gg * _sigmoid(gg) * vv).astype(z_ref.dtype)
        rider(None, None, i == nt - 1, r_in, r_out, r_scratch)

    return pl.pallas_call(
        body, name=name, grid=(nt,),
        in_specs=[pl.BlockSpec((tm, d), lambda i: (i, 0)), pl.BlockSpec((1, d), lambda i: (0, 0)),
                  _resident((d, f2)), _full(conv_w.shape)] + list(rider.in_specs),
        out_specs=[pl.BlockSpec((tm, d), lambda i: (i, 0)), pl.BlockSpec((tm, f2), lambda i: (i, 0)),
                   pl.BlockSpec((tm, f), lambda i: (i, 0))] + list(rider.out_specs),
        out_shape=[jax.ShapeDtypeStruct((t, d), MXU_DTYPE), jax.ShapeDtypeStruct((t, f2), F32),
                   jax.ShapeDtypeStruct((t, f), MXU_DTYPE)] + list(rider.out_shape),
        scratch_shapes=[pltpu.VMEM((nh * SUBLANES, f2), F32)] + list(rider.scratch_shapes),
        compiler_params=_params("arbitrary"),
    )(h, g, w, conv_w, *rider.inputs)


def _ffn_down_bwd_fused(dh, wdt, a, conv_w, tm, name):
    t, d = dh.shape
    f = wdt.shape[1]
    f2 = 2 * f
    nt = t // tm
    cw = min(STRIP, f)
    kw = conv_w.shape[0]
    nh = kw - 1
    rev = lambda i: nt - 1 - i

    def body(dh_ref, wdt_ref, a_ref, *rest):
        halo_refs = rest[:nh]
        cw_ref, da_ref, dhb_ref, dw_ref, carry = rest[nh:]
        i = pl.program_id(0)
        newest, oldest = i == 0, i == nt - 1

        @pl.when(newest)
        def _():
            carry[...] = jnp.zeros_like(carry)
            dw_ref[...] = jnp.zeros_like(dw_ref)

        dhb_ref[...] = dh_ref[...].astype(dhb_ref.dtype)
        for c in range(0, f, cw):
            gsl, vsl = slice(c, c + cw), slice(f + c, f + c + cw)
            dz = _dot(dhb_ref[...], wdt_ref[:, gsl], _NN)
            win, conv = {}, {}
            for sl in (gsl, vsl):
                before = [jnp.where(oldest, 0.0, r[:, sl]) for r in halo_refs]
                win[sl.start] = _windows(a_ref[:, sl], before, kw)
                conv[sl.start] = _weighted(_rows(cw_ref, sl), win[sl.start])
            gg, vv = conv[gsl.start], conv[vsl.start]
            s = _sigmoid(gg)
            grads = {gsl.start: dz * vv * s * (1.0 + gg * (1.0 - s)), vsl.start: dz * gg * s}
            for sl in (gsl, vsl):
                dconv = grads[sl.start]
                da_ref[:, sl] = _conv_t(dconv, _blocks(carry, sl, nh, False), _rows(cw_ref, sl)).astype(da_ref.dtype)
                for k in range(kw):
                    dw_ref[k:k + 1, sl] += jnp.sum(dconv * win[sl.start][k], axis=0, keepdims=True)
                carry[:, sl] = dconv[0:nh * SUBLANES]

    return pl.pallas_call(
        body, name=name, grid=(nt,),
        in_specs=[pl.BlockSpec((tm, d), lambda i: (rev(i), 0)), _resident((d, f)),
                  pl.BlockSpec((tm, f2), lambda i: (rev(i), 0))] + _halo_specs(tm, f2, nh, rev) + [_full(conv_w.shape)],
        out_specs=[pl.BlockSpec((tm, f2), lambda i: (rev(i), 0)), pl.BlockSpec((tm, d), lambda i: (rev(i), 0)),
                   _full(conv_w.shape)],
        out_shape=[jax.ShapeDtypeStruct((t, f2), MXU_DTYPE), jax.ShapeDtypeStruct((t, d), MXU_DTYPE),
                   jax.ShapeDtypeStruct(conv_w.shape, F32)],
        scratch_shapes=[pltpu.VMEM((nh * SUBLANES, f2), F32)],
        compiler_params=_params("arbitrary"),
    )(dh, wdt, a, *([a] * nh), conv_w)


def _sc_in_fused(x, meta, g, w, conv_w, tm, name, rider=_NO_RIDER):
    d = x.shape[1]
    t = x.shape[0] + meta.shape[0]
    cw = min(STRIP, d)
    kw = conv_w.shape[0]
    nh = kw - 1
    nt = t // tm

    def body(x_ref, meta_ref, g_ref, w_ref, cw_ref, *rest):
        r_in, (h_ref, n_ref, p_ref, q_ref), r_out, (carry, tbuf, slabs, sems), r_scratch = _split_refs(rider, 4, 4, rest)
        i = pl.program_id(0)
        rider(i == 0, i == nt // 2, None, r_in, r_out, r_scratch)

        @pl.when(i == 0)
        def _():
            carry[...] = jnp.zeros_like(carry)

        _fetch_time_tile(i, nt, x_ref, meta_ref[...], tbuf, sems, tm)
        _time_to_tile_order(tbuf.at[i % 2], slabs, h_ref)
        _rmsnorm_to(h_ref, g_ref, n_ref)
        for c in range(0, d, cw):
            sl = slice(c, c + cw)
            parts = []
            for base in (0, d, 2 * d):
                psl = slice(base + c, base + c + cw)
                parts.append(_dot(n_ref[...], w_ref[:, psl], _NN))
                p_ref[:, psl] = parts[-1]
            bg, cg, v = parts
            cv = cg * v
            u = _weighted(_rows(cw_ref, sl), _windows(cv, _blocks(carry, sl, nh, True), kw))
            carry[:, sl] = cv[tm - nh * SUBLANES:tm]
            q_ref[:, sl] = (bg * u).astype(q_ref.dtype)
        rider(None, None, i == nt - 1, r_in, r_out, r_scratch)

    row_tile = lambda cols: pl.BlockSpec((tm, cols), lambda i: (i, 0))
    return pl.pallas_call(
        body, name=name, grid=(nt,),
        in_specs=[_ANY, _full(meta.shape), pl.BlockSpec((1, d), lambda i: (0, 0)),
                  _resident((d, 3 * d)), _full(conv_w.shape)] + list(rider.in_specs),
        out_specs=[row_tile(d), row_tile(d), row_tile(3 * d), row_tile(d)] + list(rider.out_specs),
        out_shape=[jax.ShapeDtypeStruct((t, d), F32), jax.ShapeDtypeStruct((t, d), MXU_DTYPE),
                   jax.ShapeDtypeStruct((t, 3 * d), F32), jax.ShapeDtypeStruct((t, d), MXU_DTYPE)]
                  + list(rider.out_shape),
        scratch_shapes=[pltpu.VMEM((nh * SUBLANES, d), F32)] + _time_scratch(tm, d) + list(rider.scratch_shapes),
        compiler_params=_params("arbitrary"),
    )(x, meta, g, w, conv_w, *rider.inputs)


def _sc_out_bwd_fused(dh, wot, p, conv_w, tm, name):
    t, d = dh.shape
    nt = t // tm
    cw = min(STRIP, d)
    kw = conv_w.shape[0]
    nh = kw - 1
    rev = lambda i: nt - 1 - i

    def body(dh_ref, wot_ref, p_ref, *rest):
        halo_refs = rest[:nh]
        cw_ref, dp_ref, dhb_ref, dw_ref, carry = rest[nh:]
        i = pl.program_id(0)
        newest, oldest = i == 0, i == nt - 1

        @pl.when(newest)
        def _():
            carry[...] = jnp.zeros_like(carry)
            dw_ref[...] = jnp.zeros_like(dw_ref)

        dhb_ref[...] = dh_ref[...].astype(dhb_ref.dtype)
        for c in range(0, d, cw):
            sl, csl, vsl = slice(c, c + cw), slice(d + c, d + c + cw), slice(2 * d + c, 2 * d + c + cw)
            w = _rows(cw_ref, sl)
            dq = _dot(dhb_ref[...], wot_ref[:, sl], _NN)
            bg, cg, v = p_ref[:, sl], p_ref[:, csl], p_ref[:, vsl]
            before = [jnp.where(oldest, 0.0, r[:, csl] * r[:, vsl]) for r in halo_refs]
            win = _windows(cg * v, before, kw)
            du = dq * bg
            dcv = _conv_t(du, _blocks(carry, sl, nh, False), w)
            dp_ref[:, sl] = (dq * _weighted(w, win)).astype(dp_ref.dtype)
            dp_ref[:, csl] = (dcv * v).astype(dp_ref.dtype)
            dp_ref[:, vsl] = (dcv * cg).astype(dp_ref.dtype)
            for k in range(kw):
                dw_ref[k:k + 1, sl] += jnp.sum(du * win[k], axis=0, keepdims=True)
            carry[:, sl] = du[0:nh * SUBLANES]

    return pl.pallas_call(
        body, name=name, grid=(nt,),
        in_specs=[pl.BlockSpec((tm, d), lambda i: (rev(i), 0)), _resident((d, d)),
                  pl.BlockSpec((tm, 3 * d), lambda i: (rev(i), 0))] + _halo_specs(tm, 3 * d, nh, rev)
                 + [_full(conv_w.shape)],
        out_specs=[pl.BlockSpec((tm, 3 * d), lambda i: (rev(i), 0)), pl.BlockSpec((tm, d), lambda i: (rev(i), 0)),
                   _full(conv_w.shape)],
        out_shape=[jax.ShapeDtypeStruct((t, 3 * d), MXU_DTYPE), jax.ShapeDtypeStruct((t, d), MXU_DTYPE),
                   jax.ShapeDtypeStruct(conv_w.shape, F32)],
        scratch_shapes=[pltpu.VMEM((nh * SUBLANES, d), F32)],
        compiler_params=_params("arbitrary"),
    )(dh, wot, p, *([p] * nh), conv_w)


def _rg_gates(u, k, wa_ref, ba, wx_ref, bx, lam):
    ub = u.astype(MXU_DTYPE)
    r = _sigmoid_tanh(_dot(ub, wa_ref[k], _NN) + ba)
    ig = _sigmoid_tanh(_dot(ub, wx_ref[k], _NN) + bx)
    sp = _softplus(-lam)
    la = -RG_C * r * sp
    a = jnp.exp(la)
    th = jnp.tanh(la)
    mult = jnp.sqrt(-2.0 * th / (1.0 - th))
    return r, ig, a, mult


def _rg_in_fused(h, g, w, conv_w, conv_b, wa, ba, wx, bx, lam, tm, name, rider=_NO_RIDER):
    t, d = h.shape
    r2 = w.shape[1]
    rr = r2 // 2
    nb, bd, _ = wa.shape
    kw = conv_w.shape[0]
    nh = kw - 1
    groups = tm // SUBLANES
    nt = t // tm

    def body(h_ref, g_ref, w_ref, cw_ref, cb_ref, wa_ref, ba_ref, wx_ref, bx_ref, lam_ref, *rest):
        r_in, (n_ref, p_ref, y_ref, hs_ref), r_out, (a_scr, b_scr, carry_rb, carry_h), r_scratch = _split_refs(
            rider, 4, 4, rest)
        i = pl.program_id(0)
        rider(i == 0, i == nt // 2, None, r_in, r_out, r_scratch)

        @pl.when(i == 0)
        def _():
            carry_rb[...] = jnp.zeros_like(carry_rb)
            carry_h[...] = jnp.zeros_like(carry_h)

        _rmsnorm_to(h_ref, g_ref, n_ref)
        for k in range(nb):
            sl = slice(k * bd, (k + 1) * bd)
            rsl = slice(rr + k * bd, rr + (k + 1) * bd)
            p_ref[:, sl] = _dot(n_ref[...], w_ref[:, sl], _NN)
            rb = _dot(n_ref[...], w_ref[:, rsl], _NN)
            p_ref[:, rsl] = rb
            u = _weighted(_rows(cw_ref, sl), _windows(rb, _blocks(carry_rb, sl, nh, True), kw)) + cb_ref[:, sl]
            carry_rb[:, sl] = rb[tm - nh * SUBLANES:tm]
            _, ig, a, mult = _rg_gates(u, k, wa_ref, ba_ref[:, sl], wx_ref, bx_ref[:, sl], lam_ref[:, sl])
            a_scr[:, sl] = a
            b_scr[:, sl] = mult * (ig * u)

        carry_h[...] = _scan_tile(a_scr, b_scr, hs_ref, carry_h[...], groups, reverse=False)

        for k in range(nb):
            sl = slice(k * bd, (k + 1) * bd)
            y_ref[:, sl] = (hs_ref[:, sl] * _gelu(p_ref[:, sl])).astype(y_ref.dtype)
        rider(None, None, i == nt - 1, r_in, r_out, r_scratch)

    vm = lambda rows: pltpu.VMEM((rows, rr), F32)
    return pl.pallas_call(
        body, name=name, grid=(nt,),
        in_specs=[pl.BlockSpec((tm, d), lambda i: (i, 0)), pl.BlockSpec((1, d), lambda i: (0, 0)),
                  _resident((d, r2)), _full(conv_w.shape), _full(conv_b.shape), _full(wa.shape), _full(ba.shape),
                  _full(wx.shape), _full(bx.shape), _full(lam.shape)] + list(rider.in_specs),
        out_specs=[pl.BlockSpec((tm, d), lambda i: (i, 0)), pl.BlockSpec((tm, r2), lambda i: (i, 0)),
                   pl.BlockSpec((tm, rr), lambda i: (i, 0)), pl.BlockSpec((tm, rr), lambda i: (i, 0))]
                  + list(rider.out_specs),
        out_shape=[jax.ShapeDtypeStruct((t, d), MXU_DTYPE), jax.ShapeDtypeStruct((t, r2), F32),
                   jax.ShapeDtypeStruct((t, rr), MXU_DTYPE), jax.ShapeDtypeStruct((t, rr), F32)]
                  + list(rider.out_shape),
        scratch_shapes=[vm(tm), vm(tm), vm(nh * SUBLANES), vm(SUBLANES)] + list(rider.scratch_shapes),
        compiler_params=_params("arbitrary"),
    )(h, g, w, conv_w, conv_b, wa, ba, wx, bx, lam, *rider.inputs)


def _rg_out_bwd_fused(dh, wot, p, hs, conv_w, conv_b, wa, ba, wx, bx, lam, tm, name):
    t, d = dh.shape
    r2 = p.shape[1]
    rr = r2 // 2
    nb, bd, _ = wa.shape
    nt = t // tm
    kw = conv_w.shape[0]
    nh = kw - 1
    groups = tm // SUBLANES
    rev = lambda i: nt - 1 - i

    def body(dh_ref, wot_ref, p_ref, hs_ref, hsh_ref, *rest):
        halo_refs = rest[:nh]
        (cw_ref, cb_ref, wa_ref, ba_ref, wx_ref, bx_ref, lam_ref,
         dp_ref, dhb_ref, dcw_ref, dcb_ref, dwa_ref, dba_ref, dwx_ref, dbx_ref, dlam_ref,
         a_scr, as_scr, u_scr, r_scr, ig_scr, mult_scr, g_scr, carry_g, carry_du) = rest[nh:]
        i = pl.program_id(0)
        newest, oldest = i == 0, i == nt - 1

        @pl.when(newest)
        def _():
            carry_g[...] = jnp.zeros_like(carry_g)
            carry_du[...] = jnp.zeros_like(carry_du)
            for ref in (dcw_ref, dcb_ref, dwa_ref, dba_ref, dwx_ref, dbx_ref, dlam_ref):
                ref[...] = jnp.zeros_like(ref)

        dhb_ref[...] = dh_ref[...].astype(dhb_ref.dtype)
        ones = jnp.ones((SUBLANES, bd), F32)

        def conv_windows(sl, rsl):
            before = [jnp.where(oldest, 0.0, r[:, rsl]) for r in halo_refs]
            return _windows(p_ref[:, rsl], before, kw)

        for k in range(nb):
            sl = slice(k * bd, (k + 1) * bd)
            rsl = slice(rr + k * bd, rr + (k + 1) * bd)
            dy = _dot(dhb_ref[...], wot_ref[:, sl], _NN)
            u = _weighted(_rows(cw_ref, sl), conv_windows(sl, rsl)) + cb_ref[:, sl]
            r, ig, a, mult = _rg_gates(u, k, wa_ref, ba_ref[:, sl], wx_ref, bx_ref[:, sl], lam_ref[:, sl])
            gb = p_ref[:, sl]
            g_scr[:, sl] = dy * _gelu(gb)
            dp_ref[:, sl] = (dy * hs_ref[:, sl] * _gelu_grad(gb)).astype(dp_ref.dtype)
            a_scr[:, sl] = a
            as_scr[:, sl] = _shift_up(a, [ones], 1)
            u_scr[:, sl] = u
            r_scr[:, sl] = r
            ig_scr[:, sl] = ig
            mult_scr[:, sl] = mult

        g_first = _scan_tile(as_scr, g_scr, g_scr, carry_g[...], groups, reverse=True)
        carry_g[...] = jnp.broadcast_to(a_scr[0:1, :], (SUBLANES, rr)) * g_first

        for k in range(nb):
            sl = slice(k * bd, (k + 1) * bd)
            rsl = slice(rr + k * bd, rr + (k + 1) * bd)
            cw = _rows(cw_ref, sl)
            lam_k = lam_ref[:, sl]
            sp = _softplus(-lam_k)
            g = g_scr[:, sl]
            a, u, r, ig, mult = a_scr[:, sl], u_scr[:, sl], r_scr[:, sl], ig_scr[:, sl], mult_scr[:, sl]
            da = g * _shift_down(hs_ref[:, sl], [jnp.where(oldest, 0.0, hsh_ref[:, sl])], 1)
            dmult = g * (ig * u)
            d_iu = g * mult
            dla = da * a - dmult * (a * a) / mult
            dr = dla * (-RG_C * sp)
            dsp = jnp.sum(dla * (-RG_C * r), axis=0, keepdims=True)
            dlam_ref[:, sl] += dsp * (-_sigmoid(-lam_k))
            dza = dr * r * (1.0 - r)
            dzx = (d_iu * u) * ig * (1.0 - ig)
            dba_ref[:, sl] += jnp.sum(dza, axis=0, keepdims=True)
            dbx_ref[:, sl] += jnp.sum(dzx, axis=0, keepdims=True)
            ub = u.astype(MXU_DTYPE)
            dzab, dzxb = dza.astype(MXU_DTYPE), dzx.astype(MXU_DTYPE)
            dwa_ref[k] += _dot(ub, dzab, _TN)
            dwx_ref[k] += _dot(ub, dzxb, _TN)
            du = d_iu * ig + _dot(dzab, wa_ref[k], _NT) + _dot(dzxb, wx_ref[k], _NT)
            dcb_ref[:, sl] += jnp.sum(du, axis=0, keepdims=True)
            win = conv_windows(sl, rsl)
            for kk in range(kw):
                dcw_ref[kk:kk + 1, sl] += jnp.sum(du * win[kk], axis=0, keepdims=True)
            dp_ref[:, rsl] = _conv_t(du, _blocks(carry_du, sl, nh, False), cw).astype(dp_ref.dtype)
            carry_du[:, sl] = du[0:nh * SUBLANES]

    tile = lambda cols: pl.BlockSpec((tm, cols), lambda i: (rev(i), 0))
    vm = lambda rows: pltpu.VMEM((rows, rr), F32)
    grads = [conv_w.shape, conv_b.shape, wa.shape, ba.shape, wx.shape, bx.shape, lam.shape]
    return pl.pallas_call(
        body, name=name, grid=(nt,),
        in_specs=[tile(d), _resident((d, rr)), tile(r2), tile(rr)] + _halo_specs(tm, rr, 1, rev)
                 + _halo_specs(tm, r2, nh, rev)
                 + [_full(conv_w.shape), _full(conv_b.shape), _full(wa.shape), _full(ba.shape),
                    _full(wx.shape), _full(bx.shape), _full(lam.shape)],
        out_specs=[tile(r2), tile(d)] + [_full(s) for s in grads],
        out_shape=[jax.ShapeDtypeStruct((t, r2), MXU_DTYPE), jax.ShapeDtypeStruct((t, d), MXU_DTYPE)]
                  + [jax.ShapeDtypeStruct(s, F32) for s in grads],
        scratch_shapes=[vm(tm)] * 7 + [vm(SUBLANES), vm(nh * SUBLANES)],
        compiler_params=_params("arbitrary"),
    )(dh, wot, p, hs, hs, *([p] * nh), conv_w, conv_b, wa, ba, wx, bx, lam)


def _loss_head(h, g, target, n_meta, tm, name):
    t, d = h.shape
    groups = tm // SUBLANES
    nt = t // tm

    def body(h_ref, g_ref, t_ref, dh_ref, loss_ref, dg_ref, tgt, tbuf, slabs, sems):
        i = pl.program_id(0)
        _fetch_time_tile(i, nt, t_ref, jnp.zeros((n_meta, d), F32), tbuf, sems, tm)
        _time_to_tile_order(tbuf.at[i % 2], slabs, tgt)
        x = h_ref[...]
        ms = jnp.mean(x * x, axis=-1, keepdims=True)
        r = lax.rsqrt(ms + RMS_EPS)
        xhat = x * r
        gg = g_ref[...]
        row = lax.broadcasted_iota(jnp.int32, (tm, 1), 0)
        time = i * tm + jnp.right_shift(row, 3) + jnp.bitwise_and(row, SUBLANES - 1) * groups
        err = jnp.where(time >= n_meta, xhat * gg - tgt[...], 0.0)
        dout = err * (1.0 / d)
        dng = dout * gg
        c = jnp.mean(dng * xhat, axis=-1, keepdims=True)
        dh_ref[...] = r * (dng - xhat * c)
        part_dg = jnp.sum(dout * xhat, axis=0, keepdims=True)
        part_loss = jnp.broadcast_to(0.5 * jnp.sum(err * dout, keepdims=True), loss_ref.shape)

        @pl.when(i == 0)
        def _():
            dg_ref[...] = part_dg
            loss_ref[...] = part_loss

        @pl.when(i > 0)
        def _():
            dg_ref[...] += part_dg
            loss_ref[...] += part_loss

    return pl.pallas_call(
        body, name=name, grid=(nt,),
        in_specs=[pl.BlockSpec((tm, d), lambda i: (i, 0)),
                  pl.BlockSpec((1, d), lambda i: (0, 0)),
                  _ANY],
        out_specs=[pl.BlockSpec((tm, d), lambda i: (i, 0)),
                   pl.BlockSpec((SUBLANES, LANES), lambda i: (0, 0)),
                   pl.BlockSpec((1, d), lambda i: (0, 0))],
        out_shape=[jax.ShapeDtypeStruct((t, d), F32), jax.ShapeDtypeStruct((SUBLANES, LANES), F32),
                   jax.ShapeDtypeStruct((1, d), F32)],
        scratch_shapes=[pltpu.VMEM((tm, d), F32)] + _time_scratch(tm, d),
        compiler_params=_params("arbitrary"),
    )(h, g, target)


def _adamw(w, g, m, v, name):
    rows, cols = w.shape
    tr = rows
    if rows > 512:
        for cand in range(8, 513, 8):
            if rows % cand == 0:
                tr = cand

    def body(w_ref, g_ref, m_ref, v_ref, d_ref, nm_ref, nv_ref):
        g_ = g_ref[...]
        m_ = ADAM_B1 * m_ref[...] + (1.0 - ADAM_B1) * g_
        v_ = ADAM_B2 * v_ref[...] + (1.0 - ADAM_B2) * (g_ * g_)
        m_hat = m_ / (1.0 - ADAM_B1 ** ADAM_STEP)
        v_hat = v_ / (1.0 - ADAM_B2 ** ADAM_STEP)
        d_ref[...] = -ADAM_LR * (m_hat / (jnp.sqrt(v_hat) + ADAM_EPS) + ADAM_WD * w_ref[...])
        nm_ref[...] = m_
        nv_ref[...] = v_

    spec = pl.BlockSpec((tr, cols), lambda i: (i, 0))
    shape = jax.ShapeDtypeStruct((rows, cols), F32)
    return pl.pallas_call(
        body, name=name, grid=(rows // tr,),
        in_specs=[spec] * 4, out_specs=[spec] * 3, out_shape=[shape] * 3,
        compiler_params=_params("parallel"),
    )(w, g, m, v)


def _adamw_nd(w, g, m, v, name):
    shape = w.shape
    two_d = (-1, shape[-1]) if w.ndim > 1 else (1, -1)
    outs = _adamw(w.reshape(two_d), g.reshape(two_d), m.reshape(two_d), v.reshape(two_d), name)
    return tuple(o.reshape(shape) for o in outs)


def _all_gather(blk, name):
    rows, cols = blk.shape

    def body(x_ref, out_ref, send_sems, recv_sems, local_sem):
        x, y, c = lax.axis_index("x"), lax.axis_index("y"), lax.axis_index("c")
        me, sibling = (x, y, c), (x, y, 1 - c)
        chips = [(1 - x, y), (x, 1 - y), (1 - x, 1 - y)]

        def slot(px, py, pc):
            return out_ref.at[4 * px + 2 * py + pc]

        def copy(k, block, to, src=None):
            return pltpu.make_async_remote_copy(
                src_ref=slot(*block) if src is None else src, dst_ref=slot(*block),
                send_sem=send_sems.at[k], recv_sem=recv_sems.at[k], device_id=to, device_id_type=_MESH)

        mine = pltpu.make_async_copy(x_ref, slot(*me), local_sem)
        mine.start()
        first = [copy(0, me, sibling, src=x_ref)]
        first += [copy(1 + j, me, (*chip, c), src=x_ref) for j, chip in enumerate(chips)]
        for cp in first:
            cp.start()
        passed = [copy(4 + j, (*chip, c), sibling) for j, chip in enumerate(chips)]
        for j, chip in enumerate(chips):
            copy(1 + j, (*chip, c), me).wait_recv()
            passed[j].start()
        copy(0, sibling, me).wait_recv()
        for j, chip in enumerate(chips):
            copy(4 + j, (*chip, 1 - c), me).wait_recv()
        for cp in first + passed:
            cp.wait_send()
        mine.wait()

    return pl.pallas_call(
        body, name=name,
        in_specs=[_ANY], out_specs=_ANY,
        out_shape=jax.ShapeDtypeStruct((N_DEV, rows, cols), blk.dtype),
        scratch_shapes=[pltpu.SemaphoreType.DMA((7,)), pltpu.SemaphoreType.DMA((7,)), pltpu.SemaphoreType.DMA(())],
    )(blk)


def _rs_sibling_exchange(buf, name):
    _, rows, cols = buf.shape

    def body(buf_ref, out_ref, send_sems, recv_sems):
        x, y, c = lax.axis_index("x"), lax.axis_index("y"), lax.axis_index("c")
        copies = [pltpu.make_async_remote_copy(
            src_ref=buf_ref.at[2 * k + (1 - c)], dst_ref=out_ref.at[k],
            send_sem=send_sems.at[k], recv_sem=recv_sems.at[k],
            device_id=(x, y, 1 - c), device_id_type=_MESH) for k in range(4)]
        for cp in copies:
            cp.start()
        for cp in copies:
            cp.wait()

    return pl.pallas_call(
        body, name=name,
        in_specs=[_ANY], out_specs=_ANY,
        out_shape=jax.ShapeDtypeStruct((4, rows, cols), buf.dtype),
        scratch_shapes=[pltpu.SemaphoreType.DMA((4,)), pltpu.SemaphoreType.DMA((4,))],
    )(buf)


def _rs_chip_sum(buf, recv, core, name):
    _, rows, cols = buf.shape
    tr = _row_tile(rows, 512) if rows % 16 == 0 else rows

    def body(c_ref, a_ref, b_ref, o_ref):
        o_ref[...] = a_ref[...] + b_ref[...]

    grid_spec = pltpu.PrefetchScalarGridSpec(
        num_scalar_prefetch=1, grid=(4, rows // tr),
        in_specs=[pl.BlockSpec((None, tr, cols), lambda k, j, c: (2 * k + c[0], j, 0)),
                  pl.BlockSpec((None, tr, cols), lambda k, j, c: (k, j, 0))],
        out_specs=pl.BlockSpec((None, tr, cols), lambda k, j, c: (k, j, 0)))
    return pl.pallas_call(
        body, name=name, grid_spec=grid_spec,
        out_shape=jax.ShapeDtypeStruct((4, rows, cols), buf.dtype),
        compiler_params=_params("parallel", "parallel"),
    )(core, buf, recv)


def _rs_chip_exchange(chip_sum, name):
    _, rows, cols = chip_sum.shape

    def body(cs_ref, out_ref, send_sems, recv_sems, local_sem):
        x, y, c = lax.axis_index("x"), lax.axis_index("y"), lax.axis_index("c")
        my_chip = 2 * x + y
        mine = pltpu.make_async_copy(cs_ref.at[my_chip], out_ref.at[my_chip], local_sem)
        mine.start()
        chips = [(1 - x, y), (x, 1 - y), (1 - x, 1 - y)]
        copies = [pltpu.make_async_remote_copy(
            src_ref=cs_ref.at[2 * px + py], dst_ref=out_ref.at[my_chip],
            send_sem=send_sems.at[j], recv_sem=recv_sems.at[j],
            device_id=(px, py, c), device_id_type=_MESH) for j, (px, py) in enumerate(chips)]
        for cp in copies:
            cp.start()
        for cp in copies:
            cp.wait()
        mine.wait()

    return pl.pallas_call(
        body, name=name,
        in_specs=[_ANY], out_specs=_ANY,
        out_shape=jax.ShapeDtypeStruct((4, rows, cols), chip_sum.dtype),
        scratch_shapes=[pltpu.SemaphoreType.DMA((3,)), pltpu.SemaphoreType.DMA((3,)), pltpu.SemaphoreType.DMA(())],
    )(chip_sum)


def _sum_slots(parts, name):
    slots, rows, cols = parts.shape
    tr = _row_tile(rows, 256) if rows % 16 == 0 else rows

    def body(p_ref, o_ref):
        acc = p_ref[0]
        for s in range(1, slots):
            acc = acc + p_ref[s]
        o_ref[...] = acc

    return pl.pallas_call(
        body, name=name, grid=(rows // tr,),
        in_specs=[pl.BlockSpec((slots, tr, cols), lambda i: (0, i, 0))],
        out_specs=pl.BlockSpec((tr, cols), lambda i: (i, 0)),
        out_shape=jax.ShapeDtypeStruct((rows, cols), parts.dtype),
        compiler_params=_params("parallel"),
    )(parts)


def _reduce_scatter(buf, core):
    recv = _rs_sibling_exchange(buf, "rs_sibling_exchange")
    chip_sum = _rs_chip_sum(buf, recv, core, "rs_chip_sum")
    parts = _rs_chip_exchange(chip_sum, "rs_chip_exchange")
    return _sum_slots(parts, "rs_sum_slots")


def _pad_rows(flat, cols, multiple):
    n = flat.shape[0]
    rows = -(-n // cols)
    rows = -(-rows // multiple) * multiple
    return jnp.pad(flat, (0, rows * cols - n)).reshape(rows, cols)


def _cols_to_chunks(full):
    lead = full.shape[:-1]
    c = full.shape[-1] // N_DEV
    x = full.reshape(-1, N_DEV, c)
    return jnp.transpose(x, (1, 0, 2)).reshape(N_DEV, -1)


def _chunks_to_cols(chunks, lead):
    n = 1
    for s in lead:
        n *= s
    c = chunks.shape[1] // n
    x = chunks.reshape(N_DEV, n, c)
    return jnp.transpose(x, (1, 0, 2)).reshape(tuple(lead) + (N_DEV * c,))


def kernel(x, meta_tokens, norm_mix_g, norm_ffn_g, final_norm_g, sc_w_in, sc_conv_w, sc_w_out, rg_w_in, rg_conv_w, rg_conv_b, rg_w_gate_a, rg_b_gate_a, rg_w_gate_x, rg_b_gate_x, rg_lambda, rg_w_out, ffn_w_up, ffn_conv_w, ffn_w_down, loss_target, m_meta_tokens, m_norm_mix_g, m_norm_ffn_g, m_final_norm_g, m_sc_w_in, m_sc_conv_w, m_sc_w_out, m_rg_w_in, m_rg_conv_w, m_rg_conv_b, m_rg_w_gate_a, m_rg_b_gate_a, m_rg_w_gate_x, m_rg_b_gate_x, m_rg_lambda, m_rg_w_out, m_ffn_w_up, m_ffn_conv_w, m_ffn_w_down, v_meta_tokens, v_norm_mix_g, v_norm_ffn_g, v_final_norm_g, v_sc_w_in, v_sc_conv_w, v_sc_w_out, v_rg_w_in, v_rg_conv_w, v_rg_conv_b, v_rg_w_gate_a, v_rg_b_gate_a, v_rg_w_gate_x, v_rg_b_gate_x, v_rg_lambda, v_rg_w_out, v_ffn_w_up, v_ffn_conv_w, v_ffn_w_down):
    weights = dict(meta_tokens=meta_tokens, norm_mix_g=norm_mix_g, norm_ffn_g=norm_ffn_g, final_norm_g=final_norm_g,
                   sc_w_in=sc_w_in, sc_conv_w=sc_conv_w, sc_w_out=sc_w_out, rg_w_in=rg_w_in, rg_conv_w=rg_conv_w,
                   rg_conv_b=rg_conv_b, rg_w_gate_a=rg_w_gate_a, rg_b_gate_a=rg_b_gate_a, rg_w_gate_x=rg_w_gate_x,
                   rg_b_gate_x=rg_b_gate_x, rg_lambda=rg_lambda, rg_w_out=rg_w_out, ffn_w_up=ffn_w_up,
                   ffn_conv_w=ffn_conv_w, ffn_w_down=ffn_w_down)
    m_in = dict(meta_tokens=m_meta_tokens, norm_mix_g=m_norm_mix_g, norm_ffn_g=m_norm_ffn_g, final_norm_g=m_final_norm_g,
                sc_w_in=m_sc_w_in, sc_conv_w=m_sc_conv_w, sc_w_out=m_sc_w_out, rg_w_in=m_rg_w_in, rg_conv_w=m_rg_conv_w,
                rg_conv_b=m_rg_conv_b, rg_w_gate_a=m_rg_w_gate_a, rg_b_gate_a=m_rg_b_gate_a, rg_w_gate_x=m_rg_w_gate_x,
                rg_b_gate_x=m_rg_b_gate_x, rg_lambda=m_rg_lambda, rg_w_out=m_rg_w_out, ffn_w_up=m_ffn_w_up,
                ffn_conv_w=m_ffn_conv_w, ffn_w_down=m_ffn_w_down)
    v_in = dict(meta_tokens=v_meta_tokens, norm_mix_g=v_norm_mix_g, norm_ffn_g=v_norm_ffn_g, final_norm_g=v_final_norm_g,
                sc_w_in=v_sc_w_in, sc_conv_w=v_sc_conv_w, sc_w_out=v_sc_w_out, rg_w_in=v_rg_w_in, rg_conv_w=v_rg_conv_w,
                rg_conv_b=v_rg_conv_b, rg_w_gate_a=v_rg_w_gate_a, rg_b_gate_a=v_rg_b_gate_a, rg_w_gate_x=v_rg_w_gate_x,
                rg_b_gate_x=v_rg_b_gate_x, rg_lambda=v_rg_lambda, rg_w_out=v_rg_w_out, ffn_w_up=v_ffn_w_up,
                ffn_conv_w=v_ffn_conv_w, ffn_w_down=v_ffn_w_down)
    names = list(weights)

    seq, d = x.shape[1], x.shape[2]
    n_meta = meta_tokens.shape[0]
    n_ffn = ffn_w_up.shape[0]
    rr = rg_w_gate_a.shape[1] * rg_w_gate_a.shape[2]
    core = lax.axis_index("c").astype(jnp.int32).reshape(1)

    assert n_ffn == 2
    def shard_rows(w_in, w_out):
        return jnp.concatenate([w_in.T, w_out], axis=0).astype(MXU_DTYPE)

    def both_orientations(gathered, w_in):
        r_in = w_in.shape[1]
        w_in_t = gathered[:, :r_in, :].reshape(N_DEV * r_in, d)
        w_out = gathered[:, r_in:, :].reshape(-1, d)
        return w_in_t.T, w_in_t, w_out, w_out.T

    shards = {"ffn0": shard_rows(ffn_w_up[0], ffn_w_down[0]), "rg": shard_rows(rg_w_in[0], rg_w_out[0]),
              "ffn1": shard_rows(ffn_w_up[1], ffn_w_down[1])}
    sc_w_in_f, sc_w_in_t, sc_w_out_f, sc_w_out_t = both_orientations(
        _all_gather(shard_rows(sc_w_in[0], sc_w_out[0]), "gather_sc"), sc_w_in[0])
    ffn_w_up_f, ffn_w_up_t, ffn_w_down_f, ffn_w_down_t = [None] * 2, [None] * 2, [None] * 2, [None] * 2

    small_names = ["meta_tokens", "sc_conv_w", "rg_conv_w", "rg_conv_b", "rg_b_gate_a", "rg_b_gate_x", "rg_lambda",
                   "ffn_conv_w"]
    small_lead = {n: weights[n].shape[:-1] for n in small_names}
    small_sizes = [weights[n].size for n in small_names]
    small_flat = jnp.concatenate([weights[n].reshape(-1) for n in small_names])
    small_rows = _pad_rows(small_flat, d, SUBLANES)
    small_g = _all_gather(small_rows, "gather_small").reshape(N_DEV, -1)
    small_full = {}
    o = 0
    for n, sz in zip(small_names, small_sizes):
        small_full[n] = _chunks_to_cols(small_g[:, o:o + sz], small_lead[n])
        o += sz

    wa = rg_w_gate_a[0].astype(MXU_DTYPE)
    wx = rg_w_gate_x[0].astype(MXU_DTYPE)
    rg_cw, rg_cb = small_full["rg_conv_w"][0], small_full["rg_conv_b"]
    rg_ba, rg_bx, rg_lam = small_full["rg_b_gate_a"], small_full["rg_b_gate_x"], small_full["rg_lambda"]
    sc_cw = small_full["sc_conv_w"][0]
    ffn_cw = small_full["ffn_conv_w"]

    tp = _row_tile(n_meta + seq, ROW_TILE_PERM)

    def ffn_fwd(h, l, rider):
        n, a, z, *gathered = _ffn_up_fused(h, norm_ffn_g[l:l + 1], ffn_w_up_f[l], ffn_cw[l], tp, f"ffn{l}_up", rider)
        return _matmul_residual(z, ffn_w_down_f[l], h, f"ffn{l}_down"), (n, a, z), gathered

    h0, n0, p0, q0, ffn0_w = _sc_in_fused(x[0], small_full["meta_tokens"], norm_mix_g[0:1], sc_w_in_f, sc_cw, tp, "sc_in",
                                          _GatherRider(shards["ffn0"]))
    ffn_w_up_f[0], ffn_w_up_t[0], ffn_w_down_f[0], ffn_w_down_t[0] = both_orientations(ffn0_w, ffn_w_up[0])
    h1 = _matmul_residual(q0, sc_w_out_f, h0, "sc_out")
    h2, ffn0_saved, (rg_w,) = ffn_fwd(h1, 0, _GatherRider(shards["rg"]))
    rg_w_in_f, rg_w_in_t, rg_w_out_f, rg_w_out_t = both_orientations(rg_w, rg_w_in[0])
    n2, p2, y2, hs2, ffn1_w = _rg_in_fused(h2, norm_mix_g[1:2], rg_w_in_f, rg_cw, rg_cb, wa, rg_ba, wx, rg_bx, rg_lam,
                                           tp, "rg_in", _GatherRider(shards["ffn1"]))
    ffn_w_up_f[1], ffn_w_up_t[1], ffn_w_down_f[1], ffn_w_down_t[1] = both_orientations(ffn1_w, ffn_w_up[1])
    h3 = _matmul_residual(y2, rg_w_out_f, h2, "rg_out")
    h4, ffn1_saved, _ = ffn_fwd(h3, 1, _NO_RIDER)

    dh4, loss_tile, d_final_g = _loss_head(h4, final_norm_g.reshape(1, d), loss_target[0], n_meta, tp, "loss_head")
    loss = lax.psum(loss_tile[0, 0], AXES)

    arrived = {}

    def ffn_bwd(dh_out, h_in, saved, l):
        n, a, z = saved
        da, dhb, d_cw = _ffn_down_bwd_fused(dh_out, ffn_w_down_t[l], a, ffn_cw[l], tp, f"ffn{l}_down_bwd")
        d_w_down = _wgrad(z, dhb, f"ffn{l}_down_wgrad")
        d_w_up_t, arrived[f"ffn_w_down{l}"] = _wgrad(da, n, f"ffn{l}_up_wgrad", _ScatterRider(d_w_down))
        dh_in, d_g, arrived[f"ffn_w_up{l}"] = _dgrad_in_norm(
            da, ffn_w_up_t[l], h_in, norm_ffn_g[l:l + 1], dh_out, f"ffn{l}_up_dgrad", _ScatterRider(d_w_up_t))
        return dh_in, d_cw, d_g

    dh3, d_fcw1, d_fg1 = ffn_bwd(dh4, h3, ffn1_saved, 1)

    dp2, dhb3, d_rg_cw, d_rg_cb, d_wa, d_rg_ba, d_wx, d_rg_bx, d_rg_lam = _rg_out_bwd_fused(
        dh3, rg_w_out_t, p2, hs2, rg_cw, rg_cb, wa, rg_ba, wx, rg_bx, rg_lam, tp, "rg_out_bwd")
    d_rg_w_out = _wgrad(y2, dhb3, "rg_out_wgrad")
    d_rg_w_in_t, arrived["rg_w_out"] = _wgrad(dp2, n2, "rg_in_wgrad", _ScatterRider(d_rg_w_out))
    dh2, d_mg1, arrived["rg_w_in"] = _dgrad_in_norm(
        dp2, rg_w_in_t, h2, norm_mix_g[1:2], dh3, "rg_in_dgrad", _ScatterRider(d_rg_w_in_t))

    dh1, d_fcw0, d_fg0 = ffn_bwd(dh2, h1, ffn0_saved, 0)

    dp0, dhb1, d_sc_cw = _sc_out_bwd_fused(dh1, sc_w_out_t, p0, sc_cw, tp, "sc_out_bwd")
    d_sc_w_out = _wgrad(q0, dhb1, "sc_out_wgrad")
    d_sc_w_in_t, arrived["sc_w_out"] = _wgrad(dp0, n0, "sc_in_wgrad", _ScatterRider(d_sc_w_out))
    d_mg0, d_meta, d_x, arrived["sc_w_in"] = _dgrad_in_to_input(
        dp0, sc_w_in_t, h0, norm_mix_g[0:1], dh1, n_meta, tp, "sc_in_dgrad", _ScatterRider(d_sc_w_in_t))
    grad_x = d_x[None]

    grads = {}
    for n in ("sc_w_in", "rg_w_in"):
        grads[n] = _sum_slots(arrived[n], f"sum_{n}").T[None]
    for n in ("sc_w_out", "rg_w_out"):
        grads[n] = _sum_slots(arrived[n], f"sum_{n}")[None]
    grads["ffn_w_up"] = jnp.stack([_sum_slots(arrived[f"ffn_w_up{l}"], f"sum_ffn_w_up{l}").T for l in range(n_ffn)])
    grads["ffn_w_down"] = jnp.stack([_sum_slots(arrived[f"ffn_w_down{l}"], f"sum_ffn_w_down{l}") for l in range(n_ffn)])

    small_grads = {"meta_tokens": d_meta,"sc_conv_w": d_sc_cw[None], "rg_conv_w": d_rg_cw[None],
                   "rg_conv_b": d_rg_cb, "rg_b_gate_a": d_rg_ba, "rg_b_gate_x": d_rg_bx, "rg_lambda": d_rg_lam,
                   "ffn_conv_w": jnp.stack([d_fcw0, d_fcw1])}
    small_chunks = jnp.concatenate([_cols_to_chunks(small_grads[n]) for n in small_names], axis=1)
    pad = small_rows.shape[0] * d - small_chunks.shape[1]
    small_chunks = jnp.pad(small_chunks, ((0, 0), (0, pad))).reshape(N_DEV, small_rows.shape[0], d)
    rep_names = ["norm_mix_g", "norm_ffn_g", "final_norm_g", "rg_w_gate_a", "rg_w_gate_x"]
    rep_grads = {"norm_mix_g": jnp.concatenate([d_mg0, d_mg1], axis=0),
                 "norm_ffn_g": jnp.concatenate([d_fg0, d_fg1], axis=0),
                 "final_norm_g": d_final_g.reshape(-1), "rg_w_gate_a": d_wa[None], "rg_w_gate_x": d_wx[None]}
    rep_flat = jnp.concatenate([rep_grads[n].reshape(-1) for n in rep_names])
    rep_chunk_rows = -(-rep_flat.shape[0] // (N_DEV * d))
    rep_chunk_rows += -(small_rows.shape[0] + rep_chunk_rows) % 16
    rep_chunks = jnp.pad(rep_flat, (0, N_DEV * rep_chunk_rows * d - rep_flat.shape[0])).reshape(N_DEV, rep_chunk_rows, d)

    reduced = _reduce_scatter(jnp.concatenate([small_chunks, rep_chunks], axis=1), core)

    small_red = reduced[0:small_rows.shape[0]].reshape(-1)
    o = small_rows.shape[0]
    so = 0
    for n, sz in zip(small_names, small_sizes):
        grads[n] = small_red[so:so + sz].reshape(weights[n].shape)
        so += sz
    rep_red = _all_gather(reduced[o:o + rep_chunks.shape[1]], "gather_replicated_grads").reshape(-1)
    ro = 0
    for n in rep_names:
        sz = weights[n].size
        grads[n] = rep_red[ro:ro + sz].reshape(weights[n].shape)
        ro += sz

    delta, new_m, new_v = {}, {}, {}
    for n in names:
        delta[n], new_m[n], new_v[n] = _adamw_nd(weights[n], grads[n], m_in[n], v_in[n], f"adamw_{n}")

    return (loss, grad_x, *[grads[n] for n in names], *[delta[n] for n in names],
            *[new_m[n] for n in names], *[new_v[n] for n in names])
```

```python
import jax
import jax.numpy as jnp
from jax import lax
from jax.experimental import pallas as pl
from jax.experimental.pallas import tpu as pltpu

F32 = jnp.float32
MXU_DTYPE = jnp.bfloat16
RMS_EPS = 1e-6
RG_C = 8.0
ADAM_LR = 0.001
ADAM_B1 = 0.9
ADAM_B2 = 0.999
ADAM_EPS = 1e-08
ADAM_WD = 0.01
ADAM_STEP = 10

N_DEV = 8
AXES = ("x", "y", "c")
SUBLANES = 8
LANES = 128
VMEM_LIMIT_BYTES = 48 * 1024 * 1024
ROW_TILE_MATMUL = 700
ROW_TILE_WGRAD = 3300
ROW_TILE_PERM = 400
STRIP = 256

_TN = (((0,), (0,)), ((), ()))
_NN = (((1,), (0,)), ((), ()))


def _row_tile(t, target):
    best = None
    for tm in range(16, t + 1, 16):
        if t % tm == 0 and tm <= target:
            best = tm
    return best if best is not None else t


def _col_tile(n, target):
    best = None
    for tn in range(LANES, n + 1, LANES):
        if n % tn == 0 and tn <= target:
            best = tn
    return best if best is not None else n


def _params(*sem):
    return pltpu.CompilerParams(dimension_semantics=sem, vmem_limit_bytes=VMEM_LIMIT_BYTES)


def _dot(a, b, dims):
    return lax.dot_general(a, b, dims, preferred_element_type=F32)


def _sigmoid(x):
    return 1.0 / (1.0 + jnp.exp(-x))


def _sigmoid_tanh(x):
    return 0.5 * jnp.tanh(0.5 * x) + 0.5


def _gelu(x):
    c = 0.7978845608028654
    t = jnp.tanh(c * (x + 0.044715 * (x * x * x)))
    return 0.5 * x * (1.0 + t)


def _gelu_and_grad(x):
    c = 0.7978845608028654
    x2 = x * x
    t = jnp.tanh(c * (x + 0.044715 * (x2 * x)))
    half = 0.5 * (1.0 + t)
    return x * half, half + 0.5 * x * (1.0 - t * t) * c * (1.0 + 3.0 * 0.044715 * x2)


def _softplus(x):
    return jnp.maximum(x, 0.0) + jnp.log1p(jnp.exp(-jnp.abs(x)))


def _time_scratch(tm, d):
    return [pltpu.VMEM((2, tm, d), F32), pltpu.VMEM((d // LANES, tm, LANES), F32), pltpu.SemaphoreType.DMA((2,))]


def _fetch_time_tile(i, nt, src_ref, head, tbuf, sems, tm):
    n_head = head.shape[0]

    def tile(j, slot):
        start = pl.multiple_of(j * tm - n_head, SUBLANES)
        return pltpu.make_async_copy(src_ref.at[pl.ds(start, tm)], tbuf.at[slot], sems.at[slot])

    first = pltpu.make_async_copy(src_ref.at[pl.ds(0, tm - n_head)], tbuf.at[0, pl.ds(n_head, tm - n_head)], sems.at[0])

    @pl.when(i == 0)
    def _():
        tbuf[0, 0:n_head, :] = head
        first.start()

    @pl.when(i + 1 < nt)
    def _():
        tile(i + 1, (i + 1) % 2).start()

    @pl.when(i == 0)
    def _():
        first.wait()

    @pl.when(i > 0)
    def _():
        tile(i, i % 2).wait()


def _time_to_tile_order(t_ref, slabs, out_ref):
    tm, d = out_ref.shape
    groups = tm // SUBLANES
    for k in range(d // LANES):
        slabs[k] = t_ref[:, k * LANES:(k + 1) * LANES]
    for k in range(d // LANES):
        for g in range(groups):
            out_ref[g * SUBLANES:(g + 1) * SUBLANES, k * LANES:(k + 1) * LANES] = slabs[k, pl.ds(g, SUBLANES, stride=groups), :]


def _tile_to_time_order(p_ref, slabs, t_ref):
    tm, d = p_ref.shape
    groups = tm // SUBLANES
    for k in range(d // LANES):
        for g in range(groups):
            slabs[k, pl.ds(g, SUBLANES, stride=groups), :] = p_ref[g * SUBLANES:(g + 1) * SUBLANES, k * LANES:(k + 1) * LANES]
    for k in range(d // LANES):
        t_ref[:, k * LANES:(k + 1) * LANES] = slabs[k]


def _rows(ref, sl):
    return [ref[k:k + 1, sl] for k in range(ref.shape[0])]


def _shift_down(x, before, s):
    if s == 0:
        return x
    n = x.shape[0]
    row = lax.broadcasted_iota(jnp.int32, (SUBLANES, x.shape[1]), 0)
    heads = []
    for g in range(s):
        v = x[n - (s - g) * SUBLANES:n - (s - g - 1) * SUBLANES]
        heads.append(pltpu.roll(jnp.where(row == SUBLANES - 1, before[s - g - 1], v), 1, axis=0))
    return jnp.concatenate(heads + [x[0:n - s * SUBLANES]], axis=0)


def _shift_up(x, after, s):
    if s == 0:
        return x
    row = lax.broadcasted_iota(jnp.int32, (SUBLANES, x.shape[1]), 0)
    tails = []
    for m in range(s):
        v = x[m * SUBLANES:(m + 1) * SUBLANES]
        tails.append(pltpu.roll(jnp.where(row == 0, after[m], v), SUBLANES - 1, axis=0))
    return jnp.concatenate([x[s * SUBLANES:]] + tails, axis=0)


def _weighted(w, windows):
    y = w[0] * windows[0]
    for k in range(1, len(w)):
        y = y + w[k] * windows[k]
    return y


def _windows(x, before, k_width):
    return [_shift_down(x, before, k_width - 1 - k) for k in range(k_width)]


def _conv_t(dy, after, w):
    k_width = len(w)
    return _weighted(w, [_shift_up(dy, after, k_width - 1 - k) for k in range(k_width)])


def _blocks(ref, sl, count, newest_first):
    n = ref.shape[0] // SUBLANES
    order = range(n - 1, n - 1 - count, -1) if newest_first else range(count)
    return [ref[b * SUBLANES:(b + 1) * SUBLANES, sl] for b in order]


def _halo_specs(tm, cols, count, tile_of):
    def spec(k):
        return pl.BlockSpec((SUBLANES, cols), lambda i: (jnp.maximum(tile_of(i) * (tm // SUBLANES) - k, 0), 0))
    return [spec(k) for k in range(1, count + 1)]


def _scan_tile(coef, val, out, carry, groups, reverse):
    cols = coef.shape[1]
    row = lax.broadcasted_iota(jnp.int32, (SUBLANES, cols), 0)

    def blk(i):
        g = groups - 1 - i if reverse else i
        return pl.ds(pl.multiple_of(g * SUBLANES, SUBLANES), SUBLANES)

    def local(i, pl_):
        p_prev, l_prev = pl_
        a = coef[blk(i), :]
        p = a * p_prev
        l = a * l_prev + val[blk(i), :]
        coef[blk(i), :] = p
        val[blk(i), :] = l
        return p, l

    pf, lf = lax.fori_loop(0, groups, local, (jnp.ones((SUBLANES, cols), F32), jnp.zeros((SUBLANES, cols), F32)))
    for s in (1, 2, 4):
        keep, sh = (row < SUBLANES - s, SUBLANES - s) if reverse else (row >= s, s)
        p_s = jnp.where(keep, pltpu.roll(pf, sh, axis=0), 1.0)
        l_s = jnp.where(keep, pltpu.roll(lf, sh, axis=0), 0.0)
        lf = pf * l_s + lf
        pf = pf * p_s
    end = lf + pf * carry
    if reverse:
        init = jnp.where(row == SUBLANES - 1, carry, pltpu.roll(end, SUBLANES - 1, axis=0))
        leaving = jnp.broadcast_to(end[0:1, :], (SUBLANES, cols))
    else:
        init = jnp.where(row == 0, carry, pltpu.roll(end, 1, axis=0))
        leaving = jnp.broadcast_to(end[SUBLANES - 1:SUBLANES, :], (SUBLANES, cols))

    def fix(i, _):
        out[blk(i), :] = val[blk(i), :] + coef[blk(i), :] * init
        return 0

    lax.fori_loop(0, groups, fix, 0)
    return leaving


def _resident(shape):
    return pl.BlockSpec(shape, lambda *_: (0,) * len(shape), pipeline_mode=pl.Buffered(1))


def _full(shape):
    return pl.BlockSpec(shape, lambda *_: (0,) * len(shape))


def _rmsnorm_to(h_ref, g_ref, n_ref):
    x = h_ref[...]
    ms = jnp.mean(x * x, axis=-1, keepdims=True)
    n_ref[...] = (x * lax.rsqrt(ms + RMS_EPS) * g_ref[...]).astype(n_ref.dtype)


_ANY = pl.BlockSpec(memory_space=pl.ANY)
_MESH = pl.DeviceIdType.MESH


def _dma_sems(n):
    return pltpu.SemaphoreType.DMA((n,))


def _when_each(*phases):
    for cond, fn in phases:
        if cond is not None:
            pl.when(cond)(fn)


class _NoRider:
    inputs = in_specs = out_shape = out_specs = scratch_shapes = ()

    def __call__(self, first, middle, last, ins, outs, scratch):
        pass


_NO_RIDER = _NoRider()


def _split_refs(rider, n_out, n_scratch, rest):
    a = len(rider.inputs)
    b = a + n_out
    c = b + len(rider.out_shape)
    e = c + n_scratch
    return rest[:a], rest[a:b], rest[b:c], rest[c:e], rest[e:]


class _GatherRider:
    def __init__(self, blk):
        self.inputs = (blk,)
        self.in_specs = (_ANY,)
        self.out_shape = (jax.ShapeDtypeStruct((N_DEV,) + blk.shape, blk.dtype),)
        self.out_specs = (_ANY,)
        self.scratch_shapes = (_dma_sems(7), _dma_sems(7), pltpu.SemaphoreType.DMA(()))

    def __call__(self, first, middle, last, ins, outs, scratch):
        (x_ref,), (out_ref,) = ins, outs
        send_sems, recv_sems, local_sem = scratch
        x, y, c = lax.axis_index("x"), lax.axis_index("y"), lax.axis_index("c")
        me, sibling = (x, y, c), (x, y, 1 - c)
        chips = [(1 - x, y), (x, 1 - y), (1 - x, 1 - y)]

        def slot(px, py, pc):
            return out_ref.at[4 * px + 2 * py + pc]

        def copy(k, block, to, src=None):
            return pltpu.make_async_remote_copy(
                src_ref=slot(*block) if src is None else src, dst_ref=slot(*block),
                send_sem=send_sems.at[k], recv_sem=recv_sems.at[k], device_id=to, device_id_type=_MESH)

        mine = pltpu.make_async_copy(x_ref, slot(*me), local_sem)
        own = [copy(0, me, sibling, src=x_ref)]
        own += [copy(1 + j, me, (*chip, c), src=x_ref) for j, chip in enumerate(chips)]
        passed = [copy(4 + j, (*chip, c), sibling) for j, chip in enumerate(chips)]

        def at_first():
            mine.start()
            for cp in own:
                cp.start()

        def at_middle():
            for j, chip in enumerate(chips):
                copy(1 + j, (*chip, c), me).wait_recv()
                passed[j].start()

        def at_last():
            copy(0, sibling, me).wait_recv()
            for j, chip in enumerate(chips):
                copy(4 + j, (*chip, 1 - c), me).wait_recv()
            for cp in own + passed:
                cp.wait_send()
            mine.wait()

        _when_each((first, at_first), (middle, at_middle), (last, at_last))


class _ScatterRider:
    def __init__(self, grad):
        r = grad.shape[0] // N_DEV
        chunks = grad.reshape(N_DEV, r, grad.shape[1])
        self.inputs = (chunks,)
        self.in_specs = (_ANY,)
        self.out_shape = (jax.ShapeDtypeStruct(chunks.shape, chunks.dtype),)
        self.out_specs = (_ANY,)
        self.scratch_shapes = (_dma_sems(N_DEV - 1), _dma_sems(N_DEV - 1), pltpu.SemaphoreType.DMA(()))

    def __call__(self, first, middle, last, ins, outs, scratch):
        (g_ref,), (r_ref,) = ins, outs
        send_sems, recv_sems, local_sem = scratch
        x, y, c = lax.axis_index("x"), lax.axis_index("y"), lax.axis_index("c")
        me = 4 * x + 2 * y + c
        mine = pltpu.make_async_copy(g_ref.at[me], r_ref.at[me], local_sem)
        copies = []
        for k in range(1, N_DEV):
            px, py, pc = (1 - x if k & 4 else x), (1 - y if k & 2 else y), (1 - c if k & 1 else c)
            copies.append(pltpu.make_async_remote_copy(
                src_ref=g_ref.at[4 * px + 2 * py + pc], dst_ref=r_ref.at[me],
                send_sem=send_sems.at[k - 1], recv_sem=recv_sems.at[k - 1],
                device_id=(px, py, pc), device_id_type=_MESH))

        def at_first():
            mine.start()
            for cp in copies:
                cp.start()

        def at_last():
            for cp in copies:
                cp.wait()
            mine.wait()

        _when_each((first, at_first), (last, at_last))


def _matmul_residual(q, w, h, name):
    t, k = q.shape
    d = w.shape[1]
    tm = _row_tile(t, ROW_TILE_MATMUL)

    def body(q_ref, w_ref, h_ref, o_ref):
        o_ref[...] = h_ref[...] + _dot(q_ref[...], w_ref[...], _NN)

    return pl.pallas_call(
        body, name=name, grid=(t // tm,),
        in_specs=[pl.BlockSpec((tm, k), lambda i: (i, 0)),
                  _resident((k, d)),
                  pl.BlockSpec((tm, d), lambda i: (i, 0))],
        out_specs=pl.BlockSpec((tm, d), lambda i: (i, 0)),
        out_shape=jax.ShapeDtypeStruct((t, d), F32),
        compiler_params=_params("parallel"),
    )(q, w, h)


def _dgrad_in_norm(dp, wt, h, g, dh_next, name, rider=_NO_RIDER):
    t, n = dp.shape
    d = wt.shape[1]
    tm = _row_tile(t, ROW_TILE_MATMUL)
    nt = t // tm

    def body(dp_ref, wt_ref, h_ref, g_ref, dhn_ref, *rest):
        r_in, (dh_ref, dg_ref), r_out, _, r_scratch = _split_refs(rider, 2, 0, rest)
        i = pl.program_id(0)
        rider(i == 0, i == nt // 2, None, r_in, r_out, r_scratch)
        dn = _dot(dp_ref[...], wt_ref[...], _NN)
        x = h_ref[...]
        ms = jnp.mean(x * x, axis=-1, keepdims=True)
        r = lax.rsqrt(ms + RMS_EPS)
        xhat = x * r
        dng = dn * g_ref[...]
        c = jnp.mean(dng * xhat, axis=-1, keepdims=True)
        dh_ref[...] = dhn_ref[...] + r * (dng - xhat * c)
        part = jnp.sum(dn * xhat, axis=0, keepdims=True)

        @pl.when(i == 0)
        def _():
            dg_ref[...] = part

        @pl.when(i > 0)
        def _():
            dg_ref[...] += part

        rider(None, None, i == nt - 1, r_in, r_out, r_scratch)

    return pl.pallas_call(
        body, name=name, grid=(nt,),
        in_specs=[pl.BlockSpec((tm, n), lambda i: (i, 0)),
                  _resident((n, d)),
                  pl.BlockSpec((tm, d), lambda i: (i, 0)),
                  pl.BlockSpec((1, d), lambda i: (0, 0)),
                  pl.BlockSpec((tm, d), lambda i: (i, 0))] + list(rider.in_specs),
        out_specs=[pl.BlockSpec((tm, d), lambda i: (i, 0)),
                   pl.BlockSpec((1, d), lambda i: (0, 0))] + list(rider.out_specs),
        out_shape=[jax.ShapeDtypeStruct((t, d), F32), jax.ShapeDtypeStruct((1, d), F32)] + list(rider.out_shape),
        scratch_shapes=list(rider.scratch_shapes),
        compiler_params=_params("arbitrary"),
    )(dp, wt, h, g, dh_next, *rider.inputs)


def _dgrad_in_to_input(dp, wt, h, g, dh_next, n_head, tm, name, rider=_NO_RIDER):
    t, n = dp.shape
    d = wt.shape[1]
    nt = t // tm

    def body(dp_ref, wt_ref, h_ref, g_ref, dhn_ref, *rest):
        r_in, (dg_ref, head_ref, rest_ref), r_out, (pbuf, tout, slabs, sems), r_scratch = _split_refs(rider, 3, 4, rest)
        i = pl.program_id(0)
        rider(i == 0, i == nt // 2, None, r_in, r_out, r_scratch)
        dn = _dot(dp_ref[...], wt_ref[...], _NN)
        x = h_ref[...]
        ms = jnp.mean(x * x, axis=-1, keepdims=True)
        r = lax.rsqrt(ms + RMS_EPS)
        xhat = x * r
        dng = dn * g_ref[...]
        c = jnp.mean(dng * xhat, axis=-1, keepdims=True)
        pbuf[...] = dhn_ref[...] + r * (dng - xhat * c)
        part = jnp.sum(dn * xhat, axis=0, keepdims=True)

        @pl.when(i == 0)
        def _():
            dg_ref[...] = part

        @pl.when(i > 0)
        def _():
            dg_ref[...] += part

        def store(j):
            if isinstance(j, int) and j == 0:
                return pltpu.make_async_copy(tout.at[0, pl.ds(n_head, tm - n_head)],
                                             rest_ref.at[pl.ds(0, tm - n_head)], sems.at[0])
            start = j * tm - n_head
            start = start if isinstance(start, int) else pl.multiple_of(start, SUBLANES)
            return pltpu.make_async_copy(tout.at[j % 2], rest_ref.at[pl.ds(start, tm)], sems.at[j % 2])

        @pl.when(i == 2)
        def _():
            store(0).wait()

        @pl.when(i > 2)
        def _():
            store(i - 2).wait()

        _tile_to_time_order(pbuf, slabs, tout.at[i % 2])

        @pl.when(i == 0)
        def _():
            head_ref[...] = tout[0, 0:n_head, :]
            store(0).start()

        @pl.when(i > 0)
        def _():
            store(i).start()

        @pl.when(i == nt - 1)
        def _():
            for j in (nt - 2, nt - 1):
                if j >= 0:
                    store(j).wait()

        rider(None, None, i == nt - 1, r_in, r_out, r_scratch)

    return pl.pallas_call(
        body, name=name, grid=(nt,),
        in_specs=[pl.BlockSpec((tm, n), lambda i: (i, 0)),
                  _resident((n, d)),
                  pl.BlockSpec((tm, d), lambda i: (i, 0)),
                  pl.BlockSpec((1, d), lambda i: (0, 0)),
                  pl.BlockSpec((tm, d), lambda i: (i, 0))] + list(rider.in_specs),
        out_specs=[pl.BlockSpec((1, d), lambda i: (0, 0)), _full((n_head, d)), _ANY] + list(rider.out_specs),
        out_shape=[jax.ShapeDtypeStruct((1, d), F32), jax.ShapeDtypeStruct((n_head, d), F32),
                   jax.ShapeDtypeStruct((t - n_head, d), F32)] + list(rider.out_shape),
        scratch_shapes=[pltpu.VMEM((tm, d), F32)] + _time_scratch(tm, d) + list(rider.scratch_shapes),
        compiler_params=_params("arbitrary"),
    )(dp, wt, h, g, dh_next, *rider.inputs)


def _wgrad(a, b, name, rider=_NO_RIDER):
    t, m = a.shape
    d = b.shape[1]
    tmm, tk = _col_tile(m, 1024), _row_tile(t, ROW_TILE_WGRAD)
    if tmm < 512:
        tmm, tk = _col_tile(m, 1536), _row_tile(t, ROW_TILE_MATMUL)
    nm, nk = m // tmm, t // tk

    def body(a_ref, b_ref, *rest):
        r_in, (o_ref,), r_out, _, r_scratch = _split_refs(rider, 1, 0, rest)
        i, k = pl.program_id(0), pl.program_id(1)
        rider(jnp.logical_and(i == 0, k == 0), None, None, r_in, r_out, r_scratch)

        @pl.when(k == 0)
        def _():
            o_ref[...] = jnp.zeros_like(o_ref)

        o_ref[...] += _dot(a_ref[...], b_ref[...], _TN)
        rider(None, None, jnp.logical_and(i == nm - 1, k == nk - 1), r_in, r_out, r_scratch)

    outs = pl.pallas_call(
        body, name=name, grid=(nm, nk),
        in_specs=[pl.BlockSpec((tk, tmm), lambda i, k: (k, i)),
                  pl.BlockSpec((tk, d), lambda i, k: (k, 0))] + list(rider.in_specs),
        out_specs=[pl.BlockSpec((tmm, d), lambda i, k: (i, 0))] + list(rider.out_specs),
        out_shape=[jax.ShapeDtypeStruct((m, d), F32)] + list(rider.out_shape),
        scratch_shapes=list(rider.scratch_shapes),
        compiler_params=_params("arbitrary", "arbitrary"),
    )(a, b, *rider.inputs)
    return outs if rider.out_shape else outs[0]


def _ffn_up_fused(h, g, w, conv_w, tm, name, rider=_NO_RIDER):
    t, d = h.shape
    f2 = w.shape[1]
    f = f2 // 2
    cw = min(STRIP, f)
    kw = conv_w.shape[0]
    nh = kw - 1
    nt = t // tm

    def body(h_ref, g_ref, w_ref, cw_ref, *rest):
        r_in, (n_ref, a_ref, z_ref), r_out, (carry,), r_scratch = _split_refs(rider, 3, 1, rest)
        i = pl.program_id(0)
        rider(i == 0, i == nt // 2, None, r_in, r_out, r_scratch)

        @pl.when(i == 0)
        def _():
            carry[...] = jnp.zeros_like(carry)

        _rmsnorm_to(h_ref, g_ref, n_ref)
        for c in range(0, f, cw):
            conv = []
            for sl in (slice(c, c + cw), slice(f + c, f + c + cw)):
                a = _dot(n_ref[...], w_ref[:, sl], _NN)
                a_ref[:, sl] = a
                conv.append(_weighted(_rows(cw_ref, sl), _windows(a, _blocks(carry, sl, nh, True), kw)))
                carry[:, sl] = a[tm - nh * SUBLANES:tm]
            gg, vv = conv
            z_ref[:, c:c + cw] = (gg * _sigmoid(gg) * vv).astype(z_ref.dtype)
        rider(None, None, i == nt - 1, r_in, r_out, r_scratch)

    return pl.pallas_call(
        body, name=name, grid=(nt,),
        in_specs=[pl.BlockSpec((tm, d), lambda i: (i, 0)), pl.BlockSpec((1, d), lambda i: (0, 0)),
                  _resident((d, f2)), _full(conv_w.shape)] + list(rider.in_specs),
        out_specs=[pl.BlockSpec((tm, d), lambda i: (i, 0)), pl.BlockSpec((tm, f2), lambda i: (i, 0)),
                   pl.BlockSpec((tm, f), lambda i: (i, 0))] + list(rider.out_specs),
        out_shape=[jax.ShapeDtypeStruct((t, d), MXU_DTYPE), jax.ShapeDtypeStruct((t, f2), F32),
                   jax.ShapeDtypeStruct((t, f), MXU_DTYPE)] + list(rider.out_shape),
        scratch_shapes=[pltpu.VMEM((nh * SUBLANES, f2), F32)] + list(rider.scratch_shapes),
        compiler_params=_params("arbitrary"),
    )(h, g, w, conv_w, *rider.inputs)


def _ffn_down_bwd_fused(dh, wdt, a, conv_w, tm, name):
    t, d = dh.shape
    f = wdt.shape[1]
    f2 = 2 * f
    nt = t // tm
    cw = min(STRIP, f)
    kw = conv_w.shape[0]
    nh = kw - 1
    rev = lambda i: nt - 1 - i

    def body(dh_ref, wdt_ref, a_ref, *rest):
        halo_refs = rest[:nh]
        cw_ref, da_ref, dhb_ref, dw_ref, carry = rest[nh:]
        i = pl.program_id(0)
        newest, oldest = i == 0, i == nt - 1

        @pl.when(newest)
        def _():
            carry[...] = jnp.zeros_like(carry)
            dw_ref[...] = jnp.zeros_like(dw_ref)

        dhb_ref[...] = dh_ref[...].astype(dhb_ref.dtype)
        for c in range(0, f, cw):
            gsl, vsl = slice(c, c + cw), slice(f + c, f + c + cw)
            dz = _dot(dhb_ref[...], wdt_ref[:, gsl], _NN)
            win, conv = {}, {}
            for sl in (gsl, vsl):
                before = [jnp.where(oldest, 0.0, r[:, sl]) for r in halo_refs]
                win[sl.start] = _windows(a_ref[:, sl], before, kw)
                conv[sl.start] = _weighted(_rows(cw_ref, sl), win[sl.start])
            gg, vv = conv[gsl.start], conv[vsl.start]
            s = _sigmoid(gg)
            grads = {gsl.start: dz * vv * s * (1.0 + gg * (1.0 - s)), vsl.start: dz * gg * s}
            for sl in (gsl, vsl):
                dconv = grads[sl.start]
                da_ref[:, sl] = _conv_t(dconv, _blocks(carry, sl, nh, False), _rows(cw_ref, sl)).astype(da_ref.dtype)
                for k in range(kw):
                    dw_ref[k:k + 1, sl] += jnp.sum(dconv * win[sl.start][k], axis=0, keepdims=True)
                carry[:, sl] = dconv[0:nh * SUBLANES]

    return pl.pallas_call(
        body, name=name, grid=(nt,),
        in_specs=[pl.BlockSpec((tm, d), lambda i: (rev(i), 0)), _resident((d, f)),
                  pl.BlockSpec((tm, f2), lambda i: (rev(i), 0))] + _halo_specs(tm, f2, nh, rev) + [_full(conv_w.shape)],
        out_specs=[pl.BlockSpec((tm, f2), lambda i: (rev(i), 0)), pl.BlockSpec((tm, d), lambda i: (rev(i), 0)),
                   _full(conv_w.shape)],
        out_shape=[jax.ShapeDtypeStruct((t, f2), MXU_DTYPE), jax.ShapeDtypeStruct((t, d), MXU_DTYPE),
                   jax.ShapeDtypeStruct(conv_w.shape, F32)],
        scratch_shapes=[pltpu.VMEM((nh * SUBLANES, f2), F32)],
        compiler_params=_params("arbitrary"),
    )(dh, wdt, a, *([a] * nh), conv_w)


def _sc_in_fused(x, meta, g, w, conv_w, tm, name, rider=_NO_RIDER):
    d = x.shape[1]
    t = x.shape[0] + meta.shape[0]
    cw = min(STRIP, d)
    kw = conv_w.shape[0]
    nh = kw - 1
    nt = t // tm

    def body(x_ref, meta_ref, g_ref, w_ref, cw_ref, *rest):
        r_in, (h_ref, n_ref, p_ref, q_ref), r_out, (carry, tbuf, slabs, sems), r_scratch = _split_refs(rider, 4, 4, rest)
        i = pl.program_id(0)
        rider(i == 0, i == nt // 2, None, r_in, r_out, r_scratch)

        @pl.when(i == 0)
        def _():
            carry[...] = jnp.zeros_like(carry)

        _fetch_time_tile(i, nt, x_ref, meta_ref[...], tbuf, sems, tm)
        _time_to_tile_order(tbuf.at[i % 2], slabs, h_ref)
        _rmsnorm_to(h_ref, g_ref, n_ref)
        for c in range(0, d, cw):
            sl = slice(c, c + cw)
            parts = []
            for base in (0, d, 2 * d):
                psl = slice(base + c, base + c + cw)
                parts.append(_dot(n_ref[...], w_ref[:, psl], _NN))
                p_ref[:, psl] = parts[-1]
            bg, cg, v = parts
            cv = cg * v
            u = _weighted(_rows(cw_ref, sl), _windows(cv, _blocks(carry, sl, nh, True), kw))
            carry[:, sl] = cv[tm - nh * SUBLANES:tm]
            q_ref[:, sl] = (bg * u).astype(q_ref.dtype)
        rider(None, None, i == nt - 1, r_in, r_out, r_scratch)

    row_tile = lambda cols: pl.BlockSpec((tm, cols), lambda i: (i, 0))
    return pl.pallas_call(
        body, name=name, grid=(nt,),
        in_specs=[_ANY, _full(meta.shape), pl.BlockSpec((1, d), lambda i: (0, 0)),
                  _resident((d, 3 * d)), _full(conv_w.shape)] + list(rider.in_specs),
        out_specs=[row_tile(d), row_tile(d), row_tile(3 * d), row_tile(d)] + list(rider.out_specs),
        out_shape=[jax.ShapeDtypeStruct((t, d), F32), jax.ShapeDtypeStruct((t, d), MXU_DTYPE),
                   jax.ShapeDtypeStruct((t, 3 * d), F32), jax.ShapeDtypeStruct((t, d), MXU_DTYPE)]
                  + list(rider.out_shape),
        scratch_shapes=[pltpu.VMEM((nh * SUBLANES, d), F32)] + _time_scratch(tm, d) + list(rider.scratch_shapes),
        compiler_params=_params("arbitrary"),
    )(x, meta, g, w, conv_w, *rider.inputs)


def _sc_out_bwd_fused(dh, wot, p, conv_w, tm, name):
    t, d = dh.shape
    nt = t // tm
    cw = min(STRIP, d)
    kw = conv_w.shape[0]
    nh = kw - 1
    rev = lambda i: nt - 1 - i

    def body(dh_ref, wot_ref, p_ref, *rest):
        halo_refs = rest[:nh]
        cw_ref, dp_ref, dhb_ref, dw_ref, carry = rest[nh:]
        i = pl.program_id(0)
        newest, oldest = i == 0, i == nt - 1

        @pl.when(newest)
        def _():
            carry[...] = jnp.zeros_like(carry)
            dw_ref[...] = jnp.zeros_like(dw_ref)

        dhb_ref[...] = dh_ref[...].astype(dhb_ref.dtype)
        for c in range(0, d, cw):
            sl, csl, vsl = slice(c, c + cw), slice(d + c, d + c + cw), slice(2 * d + c, 2 * d + c + cw)
            w = _rows(cw_ref, sl)
            dq = _dot(dhb_ref[...], wot_ref[:, sl], _NN)
            bg, cg, v = p_ref[:, sl], p_ref[:, csl], p_ref[:, vsl]
            before = [jnp.where(oldest, 0.0, r[:, csl] * r[:, vsl]) for r in halo_refs]
            win = _windows(cg * v, before, kw)
            du = dq * bg
            dcv = _conv_t(du, _blocks(carry, sl, nh, False), w)
            dp_ref[:, sl] = (dq * _weighted(w, win)).astype(dp_ref.dtype)
            dp_ref[:, csl] = (dcv * v).astype(dp_ref.dtype)
            dp_ref[:, vsl] = (dcv * cg).astype(dp_ref.dtype)
            for k in range(kw):
                dw_ref[k:k + 1, sl] += jnp.sum(du * win[k], axis=0, keepdims=True)
            carry[:, sl] = du[0:nh * SUBLANES]

    return pl.pallas_call(
        body, name=name, grid=(nt,),
        in_specs=[pl.BlockSpec((tm, d), lambda i: (rev(i), 0)), _resident((d, d)),
                  pl.BlockSpec((tm, 3 * d), lambda i: (rev(i), 0))] + _halo_specs(tm, 3 * d, nh, rev)
                 + [_full(conv_w.shape)],
        out_specs=[pl.BlockSpec((tm, 3 * d), lambda i: (rev(i), 0)), pl.BlockSpec((tm, d), lambda i: (rev(i), 0)),
                   _full(conv_w.shape)],
        out_shape=[jax.ShapeDtypeStruct((t, 3 * d), MXU_DTYPE), jax.ShapeDtypeStruct((t, d), MXU_DTYPE),
                   jax.ShapeDtypeStruct(conv_w.shape, F32)],
        scratch_shapes=[pltpu.VMEM((nh * SUBLANES, d), F32)],
        compiler_params=_params("arbitrary"),
    )(dh, wot, p, *([p] * nh), conv_w)


def _pair_gate_weights(wa, wx):
    nb, bd, _ = wa.shape
    zero = jnp.zeros((nb // 2, bd, bd), wa.dtype)

    def pair(w):
        w = w.reshape(nb // 2, 2, bd, bd)
        top = jnp.concatenate([w[:, 0], zero], axis=2)
        bottom = jnp.concatenate([zero, w[:, 1]], axis=2)
        return jnp.concatenate([top, bottom], axis=1)

    both = jnp.concatenate([pair(wa), pair(wx)], axis=2)
    return both, jnp.swapaxes(both, 1, 2)


def _unpair_gate_grads(dw, bd):
    def blocks(cols):
        d0 = dw[:, :bd, cols:cols + bd]
        d1 = dw[:, bd:, cols + bd:cols + 2 * bd]
        return jnp.stack([d0, d1], axis=1).reshape(-1, bd, bd)
    return blocks(0), blocks(2 * bd)


def _rg_gates(u, k, wg_ref, ba, bx, lam):
    ub = u.astype(MXU_DTYPE)
    z = _dot(ub, wg_ref[k], _NN)
    half = z.shape[1] // 2
    r = _sigmoid_tanh(z[:, :half] + ba)
    ig = _sigmoid_tanh(z[:, half:] + bx)
    sp = _softplus(-lam)
    la = -RG_C * r * sp
    a = jnp.exp(la)
    th = jnp.tanh(la)
    mult = jnp.sqrt(-2.0 * th / (1.0 - th))
    return r, ig, a, mult


def _rg_in_fused(h, g, w, conv_w, conv_b, wg, ba, bx, lam, tm, name, rider=_NO_RIDER):
    t, d = h.shape
    r2 = w.shape[1]
    rr = r2 // 2
    nb, bd = wg.shape[0], wg.shape[1]
    kw = conv_w.shape[0]
    nh = kw - 1
    groups = tm // SUBLANES
    nt = t // tm

    def body(h_ref, g_ref, w_ref, cw_ref, cb_ref, wg_ref, ba_ref, bx_ref, lam_ref, *rest):
        r_in, (n_ref, p_ref, y_ref, hs_ref), r_out, (a_scr, b_scr, carry_rb, carry_h), r_scratch = _split_refs(
            rider, 4, 4, rest)
        i = pl.program_id(0)
        rider(i == 0, i == nt // 2, None, r_in, r_out, r_scratch)

        @pl.when(i == 0)
        def _():
            carry_rb[...] = jnp.zeros_like(carry_rb)
            carry_h[...] = jnp.zeros_like(carry_h)

        _rmsnorm_to(h_ref, g_ref, n_ref)
        for k in range(nb):
            sl = slice(k * bd, (k + 1) * bd)
            rsl = slice(rr + k * bd, rr + (k + 1) * bd)
            p_ref[:, sl] = _dot(n_ref[...], w_ref[:, sl], _NN)
            rb = _dot(n_ref[...], w_ref[:, rsl], _NN)
            p_ref[:, rsl] = rb
            u = _weighted(_rows(cw_ref, sl), _windows(rb, _blocks(carry_rb, sl, nh, True), kw)) + cb_ref[:, sl]
            carry_rb[:, sl] = rb[tm - nh * SUBLANES:tm]
            _, ig, a, mult = _rg_gates(u, k, wg_ref, ba_ref[:, sl], bx_ref[:, sl], lam_ref[:, sl])
            a_scr[:, sl] = a
            b_scr[:, sl] = mult * (ig * u)

        carry_h[...] = _scan_tile(a_scr, b_scr, hs_ref, carry_h[...], groups, reverse=False)

        for k in range(nb):
            sl = slice(k * bd, (k + 1) * bd)
            y_ref[:, sl] = (hs_ref[:, sl] * _gelu(p_ref[:, sl])).astype(y_ref.dtype)
        rider(None, None, i == nt - 1, r_in, r_out, r_scratch)

    vm = lambda rows: pltpu.VMEM((rows, rr), F32)
    return pl.pallas_call(
        body, name=name, grid=(nt,),
        in_specs=[pl.BlockSpec((tm, d), lambda i: (i, 0)), pl.BlockSpec((1, d), lambda i: (0, 0)),
                  _resident((d, r2)), _full(conv_w.shape), _full(conv_b.shape), _full(wg.shape), _full(ba.shape),
                  _full(bx.shape), _full(lam.shape)] + list(rider.in_specs),
        out_specs=[pl.BlockSpec((tm, d), lambda i: (i, 0)), pl.BlockSpec((tm, r2), lambda i: (i, 0)),
                   pl.BlockSpec((tm, rr), lambda i: (i, 0)), pl.BlockSpec((tm, rr), lambda i: (i, 0))]
                  + list(rider.out_specs),
        out_shape=[jax.ShapeDtypeStruct((t, d), MXU_DTYPE), jax.ShapeDtypeStruct((t, r2), F32),
                   jax.ShapeDtypeStruct((t, rr), MXU_DTYPE), jax.ShapeDtypeStruct((t, rr), F32)]
                  + list(rider.out_shape),
        scratch_shapes=[vm(tm), vm(tm), vm(nh * SUBLANES), vm(SUBLANES)] + list(rider.scratch_shapes),
        compiler_params=_params("arbitrary"),
    )(h, g, w, conv_w, conv_b, wg, ba, bx, lam, *rider.inputs)


def _rg_out_bwd_fused(dh, wot, p, hs, conv_w, conv_b, wg, wgt, ba, bx, lam, tm, name):
    t, d = dh.shape
    r2 = p.shape[1]
    rr = r2 // 2
    nb, bd = wg.shape[0], wg.shape[1]
    nt = t // tm
    kw = conv_w.shape[0]
    nh = kw - 1
    groups = tm // SUBLANES
    rev = lambda i: nt - 1 - i

    def body(dh_ref, wot_ref, p_ref, hs_ref, hsh_ref, *rest):
        halo_refs = rest[:nh]
        (cw_ref, cb_ref, wg_ref, wgt_ref, ba_ref, bx_ref, lam_ref,
         dp_ref, dhb_ref, dcw_ref, dcb_ref, dwg_ref, dba_ref, dbx_ref, dlam_ref,
         a_scr, as_scr, u_scr, r_scr, ig_scr, mult_scr, g_scr, carry_g, carry_du) = rest[nh:]
        i = pl.program_id(0)
        newest, oldest = i == 0, i == nt - 1

        @pl.when(newest)
        def _():
            carry_g[...] = jnp.zeros_like(carry_g)
            carry_du[...] = jnp.zeros_like(carry_du)
            for ref in (dcw_ref, dcb_ref, dwg_ref, dba_ref, dbx_ref, dlam_ref):
                ref[...] = jnp.zeros_like(ref)

        dhb_ref[...] = dh_ref[...].astype(dhb_ref.dtype)
        ones = jnp.ones((SUBLANES, bd), F32)

        def conv_windows(sl, rsl):
            before = [jnp.where(oldest, 0.0, r[:, rsl]) for r in halo_refs]
            return _windows(p_ref[:, rsl], before, kw)

        for k in range(nb):
            sl = slice(k * bd, (k + 1) * bd)
            rsl = slice(rr + k * bd, rr + (k + 1) * bd)
            dy = _dot(dhb_ref[...], wot_ref[:, sl], _NN)
            u = _weighted(_rows(cw_ref, sl), conv_windows(sl, rsl)) + cb_ref[:, sl]
            r, ig, a, mult = _rg_gates(u, k, wg_ref, ba_ref[:, sl], bx_ref[:, sl], lam_ref[:, sl])
            gate, dgate = _gelu_and_grad(p_ref[:, sl])
            g_scr[:, sl] = dy * gate
            dp_ref[:, sl] = (dy * hs_ref[:, sl] * dgate).astype(dp_ref.dtype)
            a_scr[:, sl] = a
            as_scr[:, sl] = _shift_up(a, [ones], 1)
            u_scr[:, sl] = u
            r_scr[:, sl] = r
            ig_scr[:, sl] = ig
            mult_scr[:, sl] = mult

        g_first = _scan_tile(as_scr, g_scr, g_scr, carry_g[...], groups, reverse=True)
        carry_g[...] = jnp.broadcast_to(a_scr[0:1, :], (SUBLANES, rr)) * g_first

        for k in range(nb):
            sl = slice(k * bd, (k + 1) * bd)
            rsl = slice(rr + k * bd, rr + (k + 1) * bd)
            cw = _rows(cw_ref, sl)
            lam_k = lam_ref[:, sl]
            sp = _softplus(-lam_k)
            g = g_scr[:, sl]
            a, u, r, ig, mult = a_scr[:, sl], u_scr[:, sl], r_scr[:, sl], ig_scr[:, sl], mult_scr[:, sl]
            da = g * _shift_down(hs_ref[:, sl], [jnp.where(oldest, 0.0, hsh_ref[:, sl])], 1)
            dmult = g * (ig * u)
            d_iu = g * mult
            dla = da * a - dmult * (a * a) / mult
            dr = dla * (-RG_C * sp)
            dsp = jnp.sum(dla * (-RG_C * r), axis=0, keepdims=True)
            dlam_ref[:, sl] += dsp * (-_sigmoid(-lam_k))
            dza = dr * r * (1.0 - r)
            dzx = (d_iu * u) * ig * (1.0 - ig)
            dba_ref[:, sl] += jnp.sum(dza, axis=0, keepdims=True)
            dbx_ref[:, sl] += jnp.sum(dzx, axis=0, keepdims=True)
            ub = u.astype(MXU_DTYPE)
            dz = jnp.concatenate([dza, dzx], axis=1).astype(MXU_DTYPE)
            dwg_ref[k] += _dot(ub, dz, _TN)
            du = d_iu * ig + _dot(dz, wgt_ref[k], _NN)
            dcb_ref[:, sl] += jnp.sum(du, axis=0, keepdims=True)
            win = conv_windows(sl, rsl)
            for kk in range(kw):
                dcw_ref[kk:kk + 1, sl] += jnp.sum(du * win[kk], axis=0, keepdims=True)
            dp_ref[:, rsl] = _conv_t(du, _blocks(carry_du, sl, nh, False), cw).astype(dp_ref.dtype)
            carry_du[:, sl] = du[0:nh * SUBLANES]

    tile = lambda cols: pl.BlockSpec((tm, cols), lambda i: (rev(i), 0))
    vm = lambda rows: pltpu.VMEM((rows, rr), F32)
    grads = [conv_w.shape, conv_b.shape, wg.shape, ba.shape, bx.shape, lam.shape]
    return pl.pallas_call(
        body, name=name, grid=(nt,),
        in_specs=[tile(d), _resident((d, rr)), tile(r2), tile(rr)] + _halo_specs(tm, rr, 1, rev)
                 + _halo_specs(tm, r2, nh, rev)
                 + [_full(conv_w.shape), _full(conv_b.shape), _full(wg.shape), _full(wgt.shape), _full(ba.shape),
                    _full(bx.shape), _full(lam.shape)],
        out_specs=[tile(r2), tile(d)] + [_full(s) for s in grads],
        out_shape=[jax.ShapeDtypeStruct((t, r2), MXU_DTYPE), jax.ShapeDtypeStruct((t, d), MXU_DTYPE)]
                  + [jax.ShapeDtypeStruct(s, F32) for s in grads],
        scratch_shapes=[vm(tm)] * 7 + [vm(SUBLANES), vm(nh * SUBLANES)],
        compiler_params=_params("arbitrary"),
    )(dh, wot, p, hs, hs, *([p] * nh), conv_w, conv_b, wg, wgt, ba, bx, lam)


def _loss_head(h, g, target, n_meta, tm, name):
    t, d = h.shape
    groups = tm // SUBLANES
    nt = t // tm

    def body(h_ref, g_ref, t_ref, dh_ref, loss_ref, dg_ref, tgt, tbuf, slabs, sems):
        i = pl.program_id(0)
        _fetch_time_tile(i, nt, t_ref, jnp.zeros((n_meta, d), F32), tbuf, sems, tm)
        _time_to_tile_order(tbuf.at[i % 2], slabs, tgt)
        x = h_ref[...]
        ms = jnp.mean(x * x, axis=-1, keepdims=True)
        r = lax.rsqrt(ms + RMS_EPS)
        xhat = x * r
        gg = g_ref[...]
        row = lax.broadcasted_iota(jnp.int32, (tm, 1), 0)
        time = i * tm + jnp.right_shift(row, 3) + jnp.bitwise_and(row, SUBLANES - 1) * groups
        err = jnp.where(time >= n_meta, xhat * gg - tgt[...], 0.0)
        dout = err * (1.0 / d)
        dng = dout * gg
        c = jnp.mean(dng * xhat, axis=-1, keepdims=True)
        dh_ref[...] = r * (dng - xhat * c)
        part_dg = jnp.sum(dout * xhat, axis=0, keepdims=True)
        part_loss = jnp.broadcast_to(0.5 * jnp.sum(err * dout, keepdims=True), loss_ref.shape)

        @pl.when(i == 0)
        def _():
            dg_ref[...] = part_dg
            loss_ref[...] = part_loss

        @pl.when(i > 0)
        def _():
            dg_ref[...] += part_dg
            loss_ref[...] += part_loss

    return pl.pallas_call(
        body, name=name, grid=(nt,),
        in_specs=[pl.BlockSpec((tm, d), lambda i: (i, 0)),
                  pl.BlockSpec((1, d), lambda i: (0, 0)),
                  _ANY],
        out_specs=[pl.BlockSpec((tm, d), lambda i: (i, 0)),
                   pl.BlockSpec((SUBLANES, LANES), lambda i: (0, 0)),
                   pl.BlockSpec((1, d), lambda i: (0, 0))],
        out_shape=[jax.ShapeDtypeStruct((t, d), F32), jax.ShapeDtypeStruct((SUBLANES, LANES), F32),
                   jax.ShapeDtypeStruct((1, d), F32)],
        scratch_shapes=[pltpu.VMEM((tm, d), F32)] + _time_scratch(tm, d),
        compiler_params=_params("arbitrary"),
    )(h, g, target)


def _adamw(w, g, m, v, name):
    rows, cols = w.shape
    tr = rows
    if rows > 512:
        for cand in range(8, 513, 8):
            if rows % cand == 0:
                tr = cand

    def body(w_ref, g_ref, m_ref, v_ref, d_ref, nm_ref, nv_ref):
        g_ = g_ref[...]
        m_ = ADAM_B1 * m_ref[...] + (1.0 - ADAM_B1) * g_
        v_ = ADAM_B2 * v_ref[...] + (1.0 - ADAM_B2) * (g_ * g_)
        m_hat = m_ / (1.0 - ADAM_B1 ** ADAM_STEP)
        v_hat = v_ / (1.0 - ADAM_B2 ** ADAM_STEP)
        d_ref[...] = -ADAM_LR * (m_hat / (jnp.sqrt(v_hat) + ADAM_EPS) + ADAM_WD * w_ref[...])
        nm_ref[...] = m_
        nv_ref[...] = v_

    spec = pl.BlockSpec((tr, cols), lambda i: (i, 0))
    shape = jax.ShapeDtypeStruct((rows, cols), F32)
    return pl.pallas_call(
        body, name=name, grid=(rows // tr,),
        in_specs=[spec] * 4, out_specs=[spec] * 3, out_shape=[shape] * 3,
        compiler_params=_params("parallel"),
    )(w, g, m, v)


def _adamw_nd(w, g, m, v, name):
    shape = w.shape
    two_d = (-1, shape[-1]) if w.ndim > 1 else (1, -1)
    outs = _adamw(w.reshape(two_d), g.reshape(two_d), m.reshape(two_d), v.reshape(two_d), name)
    return tuple(o.reshape(shape) for o in outs)


def _all_gather(blk, name):
    rows, cols = blk.shape

    def body(x_ref, out_ref, send_sems, recv_sems, local_sem):
        x, y, c = lax.axis_index("x"), lax.axis_index("y"), lax.axis_index("c")
        me, sibling = (x, y, c), (x, y, 1 - c)
        chips = [(1 - x, y), (x, 1 - y), (1 - x, 1 - y)]

        def slot(px, py, pc):
            return out_ref.at[4 * px + 2 * py + pc]

        def copy(k, block, to, src=None):
            return pltpu.make_async_remote_copy(
                src_ref=slot(*block) if src is None else src, dst_ref=slot(*block),
                send_sem=send_sems.at[k], recv_sem=recv_sems.at[k], device_id=to, device_id_type=_MESH)

        mine = pltpu.make_async_copy(x_ref, slot(*me), local_sem)
        mine.start()
        first = [copy(0, me, sibling, src=x_ref)]
        first += [copy(1 + j, me, (*chip, c), src=x_ref) for j, chip in enumerate(chips)]
        for cp in first:
            cp.start()
        passed = [copy(4 + j, (*chip, c), sibling) for j, chip in enumerate(chips)]
        for j, chip in enumerate(chips):
            copy(1 + j, (*chip, c), me).wait_recv()
            passed[j].start()
        copy(0, sibling, me).wait_recv()
        for j, chip in enumerate(chips):
            copy(4 + j, (*chip, 1 - c), me).wait_recv()
        for cp in first + passed:
            cp.wait_send()
        mine.wait()

    return pl.pallas_call(
        body, name=name,
        in_specs=[_ANY], out_specs=_ANY,
        out_shape=jax.ShapeDtypeStruct((N_DEV, rows, cols), blk.dtype),
        scratch_shapes=[pltpu.SemaphoreType.DMA((7,)), pltpu.SemaphoreType.DMA((7,)), pltpu.SemaphoreType.DMA(())],
    )(blk)


def _rs_sibling_exchange(buf, name):
    _, rows, cols = buf.shape

    def body(buf_ref, out_ref, send_sems, recv_sems):
        x, y, c = lax.axis_index("x"), lax.axis_index("y"), lax.axis_index("c")
        copies = [pltpu.make_async_remote_copy(
            src_ref=buf_ref.at[2 * k + (1 - c)], dst_ref=out_ref.at[k],
            send_sem=send_sems.at[k], recv_sem=recv_sems.at[k],
            device_id=(x, y, 1 - c), device_id_type=_MESH) for k in range(4)]
        for cp in copies:
            cp.start()
        for cp in copies:
            cp.wait()

    return pl.pallas_call(
        body, name=name,
        in_specs=[_ANY], out_specs=_ANY,
        out_shape=jax.ShapeDtypeStruct((4, rows, cols), buf.dtype),
        scratch_shapes=[pltpu.SemaphoreType.DMA((4,)), pltpu.SemaphoreType.DMA((4,))],
    )(buf)


def _rs_chip_sum(buf, recv, core, name):
    _, rows, cols = buf.shape
    tr = _row_tile(rows, 512) if rows % 16 == 0 else rows

    def body(c_ref, a_ref, b_ref, o_ref):
        o_ref[...] = a_ref[...] + b_ref[...]

    grid_spec = pltpu.PrefetchScalarGridSpec(
        num_scalar_prefetch=1, grid=(4, rows // tr),
        in_specs=[pl.BlockSpec((None, tr, cols), lambda k, j, c: (2 * k + c[0], j, 0)),
                  pl.BlockSpec((None, tr, cols), lambda k, j, c: (k, j, 0))],
        out_specs=pl.BlockSpec((None, tr, cols), lambda k, j, c: (k, j, 0)))
    return pl.pallas_call(
        body, name=name, grid_spec=grid_spec,
        out_shape=jax.ShapeDtypeStruct((4, rows, cols), buf.dtype),
        compiler_params=_params("parallel", "parallel"),
    )(core, buf, recv)


def _rs_chip_exchange(chip_sum, name):
    _, rows, cols = chip_sum.shape

    def body(cs_ref, out_ref, send_sems, recv_sems, local_sem):
        x, y, c = lax.axis_index("x"), lax.axis_index("y"), lax.axis_index("c")
        my_chip = 2 * x + y
        mine = pltpu.make_async_copy(cs_ref.at[my_chip], out_ref.at[my_chip], local_sem)
        mine.start()
        chips = [(1 - x, y), (x, 1 - y), (1 - x, 1 - y)]
        copies = [pltpu.make_async_remote_copy(
            src_ref=cs_ref.at[2 * px + py], dst_ref=out_ref.at[my_chip],
            send_sem=send_sems.at[j], recv_sem=recv_sems.at[j],
            device_id=(px, py, c), device_id_type=_MESH) for j, (px, py) in enumerate(chips)]
        for cp in copies:
            cp.start()
        for cp in copies:
            cp.wait()
        mine.wait()

    return pl.pallas_call(
        body, name=name,
        in_specs=[_ANY], out_specs=_ANY,
        out_shape=jax.ShapeDtypeStruct((4, rows, cols), chip_sum.dtype),
        scratch_shapes=[pltpu.SemaphoreType.DMA((3,)), pltpu.SemaphoreType.DMA((3,)), pltpu.SemaphoreType.DMA(())],
    )(chip_sum)


def _sum_slots(parts, name):
    slots, rows, cols = parts.shape
    tr = _row_tile(rows, 256) if rows % 16 == 0 else rows

    def body(p_ref, o_ref):
        acc = p_ref[0]
        for s in range(1, slots):
            acc = acc + p_ref[s]
        o_ref[...] = acc

    return pl.pallas_call(
        body, name=name, grid=(rows // tr,),
        in_specs=[pl.BlockSpec((slots, tr, cols), lambda i: (0, i, 0))],
        out_specs=pl.BlockSpec((tr, cols), lambda i: (i, 0)),
        out_shape=jax.ShapeDtypeStruct((rows, cols), parts.dtype),
        compiler_params=_params("parallel"),
    )(parts)


def _reduce_scatter(buf, core):
    recv = _rs_sibling_exchange(buf, "rs_sibling_exchange")
    chip_sum = _rs_chip_sum(buf, recv, core, "rs_chip_sum")
    parts = _rs_chip_exchange(chip_sum, "rs_chip_exchange")
    return _sum_slots(parts, "rs_sum_slots")


def _pad_rows(flat, cols, multiple):
    n = flat.shape[0]
    rows = -(-n // cols)
    rows = -(-rows // multiple) * multiple
    return jnp.pad(flat, (0, rows * cols - n)).reshape(rows, cols)


def _cols_to_chunks(full):
    lead = full.shape[:-1]
    c = full.shape[-1] // N_DEV
    x = full.reshape(-1, N_DEV, c)
    return jnp.transpose(x, (1, 0, 2)).reshape(N_DEV, -1)


def _chunks_to_cols(chunks, lead):
    n = 1
    for s in lead:
        n *= s
    c = chunks.shape[1] // n
    x = chunks.reshape(N_DEV, n, c)
    return jnp.transpose(x, (1, 0, 2)).reshape(tuple(lead) + (N_DEV * c,))


def kernel(x, meta_tokens, norm_mix_g, norm_ffn_g, final_norm_g, sc_w_in, sc_conv_w, sc_w_out, rg_w_in, rg_conv_w, rg_conv_b, rg_w_gate_a, rg_b_gate_a, rg_w_gate_x, rg_b_gate_x, rg_lambda, rg_w_out, ffn_w_up, ffn_conv_w, ffn_w_down, loss_target, m_meta_tokens, m_norm_mix_g, m_norm_ffn_g, m_final_norm_g, m_sc_w_in, m_sc_conv_w, m_sc_w_out, m_rg_w_in, m_rg_conv_w, m_rg_conv_b, m_rg_w_gate_a, m_rg_b_gate_a, m_rg_w_gate_x, m_rg_b_gate_x, m_rg_lambda, m_rg_w_out, m_ffn_w_up, m_ffn_conv_w, m_ffn_w_down, v_meta_tokens, v_norm_mix_g, v_norm_ffn_g, v_final_norm_g, v_sc_w_in, v_sc_conv_w, v_sc_w_out, v_rg_w_in, v_rg_conv_w, v_rg_conv_b, v_rg_w_gate_a, v_rg_b_gate_a, v_rg_w_gate_x, v_rg_b_gate_x, v_rg_lambda, v_rg_w_out, v_ffn_w_up, v_ffn_conv_w, v_ffn_w_down):
    weights = dict(meta_tokens=meta_tokens, norm_mix_g=norm_mix_g, norm_ffn_g=norm_ffn_g, final_norm_g=final_norm_g,
                   sc_w_in=sc_w_in, sc_conv_w=sc_conv_w, sc_w_out=sc_w_out, rg_w_in=rg_w_in, rg_conv_w=rg_conv_w,
                   rg_conv_b=rg_conv_b, rg_w_gate_a=rg_w_gate_a, rg_b_gate_a=rg_b_gate_a, rg_w_gate_x=rg_w_gate_x,
                   rg_b_gate_x=rg_b_gate_x, rg_lambda=rg_lambda, rg_w_out=rg_w_out, ffn_w_up=ffn_w_up,
                   ffn_conv_w=ffn_conv_w, ffn_w_down=ffn_w_down)
    m_in = dict(meta_tokens=m_meta_tokens, norm_mix_g=m_norm_mix_g, norm_ffn_g=m_norm_ffn_g, final_norm_g=m_final_norm_g,
                sc_w_in=m_sc_w_in, sc_conv_w=m_sc_conv_w, sc_w_out=m_sc_w_out, rg_w_in=m_rg_w_in, rg_conv_w=m_rg_conv_w,
                rg_conv_b=m_rg_conv_b, rg_w_gate_a=m_rg_w_gate_a, rg_b_gate_a=m_rg_b_gate_a, rg_w_gate_x=m_rg_w_gate_x,
                rg_b_gate_x=m_rg_b_gate_x, rg_lambda=m_rg_lambda, rg_w_out=m_rg_w_out, ffn_w_up=m_ffn_w_up,
                ffn_conv_w=m_ffn_conv_w, ffn_w_down=m_ffn_w_down)
    v_in = dict(meta_tokens=v_meta_tokens, norm_mix_g=v_norm_mix_g, norm_ffn_g=v_norm_ffn_g, final_norm_g=v_final_norm_g,
                sc_w_in=v_sc_w_in, sc_conv_w=v_sc_conv_w, sc_w_out=v_sc_w_out, rg_w_in=v_rg_w_in, rg_conv_w=v_rg_conv_w,
                rg_conv_b=v_rg_conv_b, rg_w_gate_a=v_rg_w_gate_a, rg_b_gate_a=v_rg_b_gate_a, rg_w_gate_x=v_rg_w_gate_x,
                rg_b_gate_x=v_rg_b_gate_x, rg_lambda=v_rg_lambda, rg_w_out=v_rg_w_out, ffn_w_up=v_ffn_w_up,
                ffn_conv_w=v_ffn_conv_w, ffn_w_down=v_ffn_w_down)
    names = list(weights)

    seq, d = x.shape[1], x.shape[2]
    n_meta = meta_tokens.shape[0]
    n_ffn = ffn_w_up.shape[0]
    rr = rg_w_gate_a.shape[1] * rg_w_gate_a.shape[2]
    core = lax.axis_index("c").astype(jnp.int32).reshape(1)

    assert n_ffn == 2
    def shard_rows(w_in, w_out):
        return jnp.concatenate([w_in.T, w_out], axis=0).astype(MXU_DTYPE)

    def both_orientations(gathered, w_in):
        r_in = w_in.shape[1]
        w_in_t = gathered[:, :r_in, :].reshape(N_DEV * r_in, d)
        w_out = gathered[:, r_in:, :].reshape(-1, d)
        return w_in_t.T, w_in_t, w_out, w_out.T

    shards = {"ffn0": shard_rows(ffn_w_up[0], ffn_w_down[0]), "rg": shard_rows(rg_w_in[0], rg_w_out[0]),
              "ffn1": shard_rows(ffn_w_up[1], ffn_w_down[1])}
    sc_w_in_f, sc_w_in_t, sc_w_out_f, sc_w_out_t = both_orientations(
        _all_gather(shard_rows(sc_w_in[0], sc_w_out[0]), "gather_sc"), sc_w_in[0])
    ffn_w_up_f, ffn_w_up_t, ffn_w_down_f, ffn_w_down_t = [None] * 2, [None] * 2, [None] * 2, [None] * 2

    small_names = ["meta_tokens", "sc_conv_w", "rg_conv_w", "rg_conv_b", "rg_b_gate_a", "rg_b_gate_x", "rg_lambda",
                   "ffn_conv_w"]
    small_lead = {n: weights[n].shape[:-1] for n in small_names}
    small_sizes = [weights[n].size for n in small_names]
    small_flat = jnp.concatenate([weights[n].reshape(-1) for n in small_names])
    small_rows = _pad_rows(small_flat, d, SUBLANES)
    small_g = _all_gather(small_rows, "gather_small").reshape(N_DEV, -1)
    small_full = {}
    o = 0
    for n, sz in zip(small_names, small_sizes):
        small_full[n] = _chunks_to_cols(small_g[:, o:o + sz], small_lead[n])
        o += sz

    wg, wgt = _pair_gate_weights(rg_w_gate_a[0].astype(MXU_DTYPE), rg_w_gate_x[0].astype(MXU_DTYPE))
    rg_cw, rg_cb = small_full["rg_conv_w"][0], small_full["rg_conv_b"]
    rg_ba, rg_bx, rg_lam = small_full["rg_b_gate_a"], small_full["rg_b_gate_x"], small_full["rg_lambda"]
    sc_cw = small_full["sc_conv_w"][0]
    ffn_cw = small_full["ffn_conv_w"]

    tp = _row_tile(n_meta + seq, ROW_TILE_PERM)

    def ffn_fwd(h, l, rider):
        n, a, z, *gathered = _ffn_up_fused(h, norm_ffn_g[l:l + 1], ffn_w_up_f[l], ffn_cw[l], tp, f"ffn{l}_up", rider)
        return _matmul_residual(z, ffn_w_down_f[l], h, f"ffn{l}_down"), (n, a, z), gathered

    h0, n0, p0, q0, ffn0_w = _sc_in_fused(x[0], small_full["meta_tokens"], norm_mix_g[0:1], sc_w_in_f, sc_cw, tp, "sc_in",
                                          _GatherRider(shards["ffn0"]))
    ffn_w_up_f[0], ffn_w_up_t[0], ffn_w_down_f[0], ffn_w_down_t[0] = both_orientations(ffn0_w, ffn_w_up[0])
    h1 = _matmul_residual(q0, sc_w_out_f, h0, "sc_out")
    h2, ffn0_saved, (rg_w,) = ffn_fwd(h1, 0, _GatherRider(shards["rg"]))
    rg_w_in_f, rg_w_in_t, rg_w_out_f, rg_w_out_t = both_orientations(rg_w, rg_w_in[0])
    n2, p2, y2, hs2, ffn1_w = _rg_in_fused(h2, norm_mix_g[1:2], rg_w_in_f, rg_cw, rg_cb, wg, rg_ba, rg_bx, rg_lam,
                                           tp, "rg_in", _GatherRider(shards["ffn1"]))
    ffn_w_up_f[1], ffn_w_up_t[1], ffn_w_down_f[1], ffn_w_down_t[1] = both_orientations(ffn1_w, ffn_w_up[1])
    h3 = _matmul_residual(y2, rg_w_out_f, h2, "rg_out")
    h4, ffn1_saved, _ = ffn_fwd(h3, 1, _NO_RIDER)

    dh4, loss_tile, d_final_g = _loss_head(h4, final_norm_g.reshape(1, d), loss_target[0], n_meta, tp, "loss_head")
    loss = lax.psum(loss_tile[0, 0], AXES)

    arrived = {}

    def ffn_bwd(dh_out, h_in, saved, l):
        n, a, z = saved
        da, dhb, d_cw = _ffn_down_bwd_fused(dh_out, ffn_w_down_t[l], a, ffn_cw[l], tp, f"ffn{l}_down_bwd")
        d_w_down = _wgrad(z, dhb, f"ffn{l}_down_wgrad")
        d_w_up_t, arrived[f"ffn_w_down{l}"] = _wgrad(da, n, f"ffn{l}_up_wgrad", _ScatterRider(d_w_down))
        dh_in, d_g, arrived[f"ffn_w_up{l}"] = _dgrad_in_norm(
            da, ffn_w_up_t[l], h_in, norm_ffn_g[l:l + 1], dh_out, f"ffn{l}_up_dgrad", _ScatterRider(d_w_up_t))
        return dh_in, d_cw, d_g

    dh3, d_fcw1, d_fg1 = ffn_bwd(dh4, h3, ffn1_saved, 1)

    dp2, dhb3, d_rg_cw, d_rg_cb, d_wg, d_rg_ba, d_rg_bx, d_rg_lam = _rg_out_bwd_fused(
        dh3, rg_w_out_t, p2, hs2, rg_cw, rg_cb, wg, wgt, rg_ba, rg_bx, rg_lam, tp, "rg_out_bwd")
    d_wa, d_wx = _unpair_gate_grads(d_wg, rg_w_gate_a.shape[2])
    d_rg_w_out = _wgrad(y2, dhb3, "rg_out_wgrad")
    d_rg_w_in_t, arrived["rg_w_out"] = _wgrad(dp2, n2, "rg_in_wgrad", _ScatterRider(d_rg_w_out))
    dh2, d_mg1, arrived["rg_w_in"] = _dgrad_in_norm(
        dp2, rg_w_in_t, h2, norm_mix_g[1:2], dh3, "rg_in_dgrad", _ScatterRider(d_rg_w_in_t))

    dh1, d_fcw0, d_fg0 = ffn_bwd(dh2, h1, ffn0_saved, 0)

    dp0, dhb1, d_sc_cw = _sc_out_bwd_fused(dh1, sc_w_out_t, p0, sc_cw, tp, "sc_out_bwd")
    d_sc_w_out = _wgrad(q0, dhb1, "sc_out_wgrad")
    d_sc_w_in_t, arrived["sc_w_out"] = _wgrad(dp0, n0, "sc_in_wgrad", _ScatterRider(d_sc_w_out))
    d_mg0, d_meta, d_x, arrived["sc_w_in"] = _dgrad_in_to_input(
        dp0, sc_w_in_t, h0, norm_mix_g[0:1], dh1, n_meta, tp, "sc_in_dgrad", _ScatterRider(d_sc_w_in_t))
    grad_x = d_x[None]

    grads = {}
    for n in ("sc_w_in", "rg_w_in"):
        grads[n] = _sum_slots(arrived[n], f"sum_{n}").T[None]
    for n in ("sc_w_out", "rg_w_out"):
        grads[n] = _sum_slots(arrived[n], f"sum_{n}")[None]
    grads["ffn_w_up"] = jnp.stack([_sum_slots(arrived[f"ffn_w_up{l}"], f"sum_ffn_w_up{l}").T for l in range(n_ffn)])
    grads["ffn_w_down"] = jnp.stack([_sum_slots(arrived[f"ffn_w_down{l}"], f"sum_ffn_w_down{l}") for l in range(n_ffn)])

    small_grads = {"meta_tokens": d_meta,"sc_conv_w": d_sc_cw[None], "rg_conv_w": d_rg_cw[None],
                   "rg_conv_b": d_rg_cb, "rg_b_gate_a": d_rg_ba, "rg_b_gate_x": d_rg_bx, "rg_lambda": d_rg_lam,
                   "ffn_conv_w": jnp.stack([d_fcw0, d_fcw1])}
    small_chunks = jnp.concatenate([_cols_to_chunks(small_grads[n]) for n in small_names], axis=1)
    pad = small_rows.shape[0] * d - small_chunks.shape[1]
    small_chunks = jnp.pad(small_chunks, ((0, 0), (0, pad))).reshape(N_DEV, small_rows.shape[0], d)
    rep_names = ["norm_mix_g", "norm_ffn_g", "final_norm_g", "rg_w_gate_a", "rg_w_gate_x"]
    rep_grads = {"norm_mix_g": jnp.concatenate([d_mg0, d_mg1], axis=0),
                 "norm_ffn_g": jnp.concatenate([d_fg0, d_fg1], axis=0),
                 "final_norm_g": d_final_g.reshape(-1), "rg_w_gate_a": d_wa[None], "rg_w_gate_x": d_wx[None]}
    rep_flat = jnp.concatenate([rep_grads[n].reshape(-1) for n in rep_names])
    rep_chunk_rows = -(-rep_flat.shape[0] // (N_DEV * d))
    rep_chunk_rows += -(small_rows.shape[0] + rep_chunk_rows) % 16
    rep_chunks = jnp.pad(rep_flat, (0, N_DEV * rep_chunk_rows * d - rep_flat.shape[0])).reshape(N_DEV, rep_chunk_rows, d)

    reduced = _reduce_scatter(jnp.concatenate([small_chunks, rep_chunks], axis=1), core)

    small_red = reduced[0:small_rows.shape[0]].reshape(-1)
    o = small_rows.shape[0]
    so = 0
    for n, sz in zip(small_names, small_sizes):
        grads[n] = small_red[so:so + sz].reshape(weights[n].shape)
        so += sz
    rep_red = _all_gather(reduced[o:o + rep_chunks.shape[1]], "gather_replicated_grads").reshape(-1)
    ro = 0
    for n in rep_names:
        sz = weights[n].size
        grads[n] = rep_red[ro:ro + sz].reshape(weights[n].shape)
        ro += sz

    delta, new_m, new_v = {}, {}, {}
    for n in names:
        delta[n], new_m[n], new_v[n] = _adamw_nd(weights[n], grads[n], m_in[n], v_in[n], f"adamw_{n}")

    return (loss, grad_x, *[grads[n] for n in names], *[delta[n] for n in names],
            *[new_m[n] for n in names], *[new_v[n] for n in names])
```

```python
import jax
import jax.numpy as jnp
from jax import lax
from jax.experimental import pallas as pl
from jax.experimental.pallas import tpu as pltpu

F32 = jnp.float32
MXU_DTYPE = jnp.bfloat16
RMS_EPS = 1e-6
RG_C = 8.0
ADAM_LR = 0.001
ADAM_B1 = 0.9
ADAM_B2 = 0.999
ADAM_EPS = 1e-08
ADAM_WD = 0.01
ADAM_STEP = 10

N_DEV = 8
AXES = ("x", "y", "c")
SUBLANES = 8
LANES = 128
VMEM_LIMIT_BYTES = 48 * 1024 * 1024
ROW_TILE_MATMUL = 700
ROW_TILE_WGRAD = 3300
ROW_TILE_PERM = 400
STRIP = 256

_TN = (((0,), (0,)), ((), ()))
_NN = (((1,), (0,)), ((), ()))


def _row_tile(t, target):
    best = None
    for tm in range(16, t + 1, 16):
        if t % tm == 0 and tm <= target:
            best = tm
    return best if best is not None else t


def _col_tile(n, target):
    best = None
    for tn in range(LANES, n + 1, LANES):
        if n % tn == 0 and tn <= target:
            best = tn
    return best if best is not None else n


def _params(*sem):
    return pltpu.CompilerParams(dimension_semantics=sem, vmem_limit_bytes=VMEM_LIMIT_BYTES)


def _dot(a, b, dims):
    return lax.dot_general(a, b, dims, preferred_element_type=F32)


def _sigmoid(x):
    return 1.0 / (1.0 + jnp.exp(-x))


def _sigmoid_tanh(x):
    return 0.5 * jnp.tanh(0.5 * x) + 0.5


def _gelu(x):
    c = 0.7978845608028654
    t = jnp.tanh(c * (x + 0.044715 * (x * x * x)))
    return 0.5 * x * (1.0 + t)


def _gelu_and_grad(x):
    c = 0.7978845608028654
    x2 = x * x
    t = jnp.tanh(c * (x + 0.044715 * (x2 * x)))
    half = 0.5 * (1.0 + t)
    return x * half, half + 0.5 * x * (1.0 - t * t) * c * (1.0 + 3.0 * 0.044715 * x2)


def _softplus(x):
    return jnp.maximum(x, 0.0) + jnp.log1p(jnp.exp(-jnp.abs(x)))


def _time_scratch(tm, d):
    return [pltpu.VMEM((2, tm, d), F32), pltpu.VMEM((d // LANES, tm, LANES), F32), pltpu.SemaphoreType.DMA((2,))]


def _fetch_time_tile(i, nt, src_ref, head, tbuf, sems, tm):
    n_head = head.shape[0]

    def tile(j, slot):
        start = pl.multiple_of(j * tm - n_head, SUBLANES)
        return pltpu.make_async_copy(src_ref.at[pl.ds(start, tm)], tbuf.at[slot], sems.at[slot])

    first = pltpu.make_async_copy(src_ref.at[pl.ds(0, tm - n_head)], tbuf.at[0, pl.ds(n_head, tm - n_head)], sems.at[0])

    @pl.when(i == 0)
    def _():
        tbuf[0, 0:n_head, :] = head
        first.start()

    @pl.when(i + 1 < nt)
    def _():
        tile(i + 1, (i + 1) % 2).start()

    @pl.when(i == 0)
    def _():
        first.wait()

    @pl.when(i > 0)
    def _():
        tile(i, i % 2).wait()


def _time_to_tile_order(t_ref, slabs, out_ref):
    tm, d = out_ref.shape
    groups = tm // SUBLANES
    for k in range(d // LANES):
        slabs[k] = t_ref[:, k * LANES:(k + 1) * LANES]
    for k in range(d // LANES):
        for g in range(groups):
            out_ref[g * SUBLANES:(g + 1) * SUBLANES, k * LANES:(k + 1) * LANES] = slabs[k, pl.ds(g, SUBLANES, stride=groups), :]


def _tile_to_time_order(p_ref, slabs, t_ref):
    tm, d = p_ref.shape
    groups = tm // SUBLANES
    for k in range(d // LANES):
        for g in range(groups):
            slabs[k, pl.ds(g, SUBLANES, stride=groups), :] = p_ref[g * SUBLANES:(g + 1) * SUBLANES, k * LANES:(k + 1) * LANES]
    for k in range(d // LANES):
        t_ref[:, k * LANES:(k + 1) * LANES] = slabs[k]


def _rows(ref, sl):
    return [ref[k:k + 1, sl] for k in range(ref.shape[0])]


def _shift_down(x, before, s):
    if s == 0:
        return x
    n = x.shape[0]
    row = lax.broadcasted_iota(jnp.int32, (SUBLANES, x.shape[1]), 0)
    heads = []
    for g in range(s):
        v = x[n - (s - g) * SUBLANES:n - (s - g - 1) * SUBLANES]
        heads.append(pltpu.roll(jnp.where(row == SUBLANES - 1, before[s - g - 1], v), 1, axis=0))
    return jnp.concatenate(heads + [x[0:n - s * SUBLANES]], axis=0)


def _shift_up(x, after, s):
    if s == 0:
        return x
    row = lax.broadcasted_iota(jnp.int32, (SUBLANES, x.shape[1]), 0)
    tails = []
    for m in range(s):
        v = x[m * SUBLANES:(m + 1) * SUBLANES]
        tails.append(pltpu.roll(jnp.where(row == 0, after[m], v), SUBLANES - 1, axis=0))
    return jnp.concatenate([x[s * SUBLANES:]] + tails, axis=0)


def _weighted(w, windows):
    y = w[0] * windows[0]
    for k in range(1, len(w)):
        y = y + w[k] * windows[k]
    return y


def _windows(x, before, k_width):
    return [_shift_down(x, before, k_width - 1 - k) for k in range(k_width)]


def _conv_t(dy, after, w):
    k_width = len(w)
    return _weighted(w, [_shift_up(dy, after, k_width - 1 - k) for k in range(k_width)])


def _blocks(ref, sl, count, newest_first):
    n = ref.shape[0] // SUBLANES
    order = range(n - 1, n - 1 - count, -1) if newest_first else range(count)
    return [ref[b * SUBLANES:(b + 1) * SUBLANES, sl] for b in order]


def _halo_specs(tm, cols, count, tile_of):
    def spec(k):
        return pl.BlockSpec((SUBLANES, cols), lambda i: (jnp.maximum(tile_of(i) * (tm // SUBLANES) - k, 0), 0))
    return [spec(k) for k in range(1, count + 1)]


def _scan_tile(coef, val, out, carry, groups, reverse):
    cols = coef.shape[1]
    row = lax.broadcasted_iota(jnp.int32, (SUBLANES, cols), 0)

    def blk(i):
        g = groups - 1 - i if reverse else i
        return pl.ds(pl.multiple_of(g * SUBLANES, SUBLANES), SUBLANES)

    def local(i, pl_):
        p_prev, l_prev = pl_
        a = coef[blk(i), :]
        p = a * p_prev
        l = a * l_prev + val[blk(i), :]
        coef[blk(i), :] = p
        val[blk(i), :] = l
        return p, l

    pf, lf = lax.fori_loop(0, groups, local, (jnp.ones((SUBLANES, cols), F32), jnp.zeros((SUBLANES, cols), F32)))
    for s in (1, 2, 4):
        keep, sh = (row < SUBLANES - s, SUBLANES - s) if reverse else (row >= s, s)
        p_s = jnp.where(keep, pltpu.roll(pf, sh, axis=0), 1.0)
        l_s = jnp.where(keep, pltpu.roll(lf, sh, axis=0), 0.0)
        lf = pf * l_s + lf
        pf = pf * p_s
    end = lf + pf * carry
    if reverse:
        init = jnp.where(row == SUBLANES - 1, carry, pltpu.roll(end, SUBLANES - 1, axis=0))
        leaving = jnp.broadcast_to(end[0:1, :], (SUBLANES, cols))
    else:
        init = jnp.where(row == 0, carry, pltpu.roll(end, 1, axis=0))
        leaving = jnp.broadcast_to(end[SUBLANES - 1:SUBLANES, :], (SUBLANES, cols))

    def fix(i, _):
        out[blk(i), :] = val[blk(i), :] + coef[blk(i), :] * init
        return 0

    lax.fori_loop(0, groups, fix, 0)
    return leaving


def _resident(shape):
    return pl.BlockSpec(shape, lambda *_: (0,) * len(shape), pipeline_mode=pl.Buffered(1))


def _full(shape):
    return pl.BlockSpec(shape, lambda *_: (0,) * len(shape))


def _rmsnorm_to(h_ref, g_ref, n_ref):
    x = h_ref[...]
    ms = jnp.mean(x * x, axis=-1, keepdims=True)
    n_ref[...] = (x * lax.rsqrt(ms + RMS_EPS) * g_ref[...]).astype(n_ref.dtype)


_ANY = pl.BlockSpec(memory_space=pl.ANY)
_MESH = pl.DeviceIdType.MESH


def _dma_sems(n):
    return pltpu.SemaphoreType.DMA((n,))


def _when_each(*phases):
    for cond, fn in phases:
        if cond is not None:
            pl.when(cond)(fn)


class _NoRider:
    inputs = in_specs = out_shape = out_specs = scratch_shapes = ()

    def __call__(self, first, middle, last, ins, outs, scratch):
        pass


_NO_RIDER = _NoRider()


def _split_refs(rider, n_out, n_scratch, rest):
    a = len(rider.inputs)
    b = a + n_out
    c = b + len(rider.out_shape)
    e = c + n_scratch
    return rest[:a], rest[a:b], rest[b:c], rest[c:e], rest[e:]


class _GatherRider:
    def __init__(self, blk):
        self.inputs = (blk,)
        self.in_specs = (_ANY,)
        self.out_shape = (jax.ShapeDtypeStruct((N_DEV,) + blk.shape, blk.dtype),)
        self.out_specs = (_ANY,)
        self.scratch_shapes = (_dma_sems(7), _dma_sems(7), pltpu.SemaphoreType.DMA(()))

    def __call__(self, first, middle, last, ins, outs, scratch):
        (x_ref,), (out_ref,) = ins, outs
        send_sems, recv_sems, local_sem = scratch
        x, y, c = lax.axis_index("x"), lax.axis_index("y"), lax.axis_index("c")
        me, sibling = (x, y, c), (x, y, 1 - c)
        chips = [(1 - x, y), (x, 1 - y), (1 - x, 1 - y)]

        def slot(px, py, pc):
            return out_ref.at[4 * px + 2 * py + pc]

        def copy(k, block, to, src=None):
            return pltpu.make_async_remote_copy(
                src_ref=slot(*block) if src is None else src, dst_ref=slot(*block),
                send_sem=send_sems.at[k], recv_sem=recv_sems.at[k], device_id=to, device_id_type=_MESH)

        mine = pltpu.make_async_copy(x_ref, slot(*me), local_sem)
        own = [copy(0, me, sibling, src=x_ref)]
        own += [copy(1 + j, me, (*chip, c), src=x_ref) for j, chip in enumerate(chips)]
        passed = [copy(4 + j, (*chip, c), sibling) for j, chip in enumerate(chips)]

        def at_first():
            mine.start()
            for cp in own:
                cp.start()

        def at_middle():
            for j, chip in enumerate(chips):
                copy(1 + j, (*chip, c), me).wait_recv()
                passed[j].start()

        def at_last():
            copy(0, sibling, me).wait_recv()
            for j, chip in enumerate(chips):
                copy(4 + j, (*chip, 1 - c), me).wait_recv()
            for cp in own + passed:
                cp.wait_send()
            mine.wait()

        _when_each((first, at_first), (middle, at_middle), (last, at_last))


class _ScatterRider:
    def __init__(self, grad, rows=None):
        r = grad.shape[0] // N_DEV
        chunks = grad.reshape(N_DEV, r, grad.shape[1])
        self.rows = rows if rows is not None else (0, r)
        self.inputs = (chunks,)
        self.in_specs = (_ANY,)
        self.out_shape = (jax.ShapeDtypeStruct((N_DEV, self.rows[1], grad.shape[1]), chunks.dtype),)
        self.out_specs = (_ANY,)
        self.scratch_shapes = (_dma_sems(N_DEV - 1), _dma_sems(N_DEV - 1), pltpu.SemaphoreType.DMA(()))

    def __call__(self, first, middle, last, ins, outs, scratch):
        (g_ref,), (r_ref,) = ins, outs
        send_sems, recv_sems, local_sem = scratch
        x, y, c = lax.axis_index("x"), lax.axis_index("y"), lax.axis_index("c")
        me = 4 * x + 2 * y + c
        part = pl.ds(*self.rows)
        mine = pltpu.make_async_copy(g_ref.at[me, part], r_ref.at[me], local_sem)
        copies = []
        for k in range(1, N_DEV):
            px, py, pc = (1 - x if k & 4 else x), (1 - y if k & 2 else y), (1 - c if k & 1 else c)
            copies.append(pltpu.make_async_remote_copy(
                src_ref=g_ref.at[4 * px + 2 * py + pc, part], dst_ref=r_ref.at[me],
                send_sem=send_sems.at[k - 1], recv_sem=recv_sems.at[k - 1],
                device_id=(px, py, pc), device_id_type=_MESH))

        def at_first():
            mine.start()
            for cp in copies:
                cp.start()

        def at_last():
            for cp in copies:
                cp.wait()
            mine.wait()

        _when_each((first, at_first), (last, at_last))


def _matmul_residual(q, w, h, name):
    t, k = q.shape
    d = w.shape[1]
    tm = _row_tile(t, ROW_TILE_MATMUL)

    def body(q_ref, w_ref, h_ref, o_ref):
        o_ref[...] = h_ref[...] + _dot(q_ref[...], w_ref[...], _NN)

    return pl.pallas_call(
        body, name=name, grid=(t // tm,),
        in_specs=[pl.BlockSpec((tm, k), lambda i: (i, 0)),
                  _resident((k, d)),
                  pl.BlockSpec((tm, d), lambda i: (i, 0))],
        out_specs=pl.BlockSpec((tm, d), lambda i: (i, 0)),
        out_shape=jax.ShapeDtypeStruct((t, d), F32),
        compiler_params=_params("parallel"),
    )(q, w, h)


def _dgrad_in_norm(dp, wt, h, g, dh_next, name, rider=_NO_RIDER):
    t, n = dp.shape
    d = wt.shape[1]
    tm = _row_tile(t, ROW_TILE_MATMUL)
    nt = t // tm

    def body(dp_ref, wt_ref, h_ref, g_ref, dhn_ref, *rest):
        r_in, (dh_ref, dg_ref), r_out, _, r_scratch = _split_refs(rider, 2, 0, rest)
        i = pl.program_id(0)
        rider(i == 0, i == nt // 2, None, r_in, r_out, r_scratch)
        dn = _dot(dp_ref[...], wt_ref[...], _NN)
        x = h_ref[...]
        ms = jnp.mean(x * x, axis=-1, keepdims=True)
        r = lax.rsqrt(ms + RMS_EPS)
        xhat = x * r
        dng = dn * g_ref[...]
        c = jnp.mean(dng * xhat, axis=-1, keepdims=True)
        dh_ref[...] = dhn_ref[...] + r * (dng - xhat * c)
        part = jnp.sum(dn * xhat, axis=0, keepdims=True)

        @pl.when(i == 0)
        def _():
            dg_ref[...] = part

        @pl.when(i > 0)
        def _():
            dg_ref[...] += part

        rider(None, None, i == nt - 1, r_in, r_out, r_scratch)

    return pl.pallas_call(
        body, name=name, grid=(nt,),
        in_specs=[pl.BlockSpec((tm, n), lambda i: (i, 0)),
                  _resident((n, d)),
                  pl.BlockSpec((tm, d), lambda i: (i, 0)),
                  pl.BlockSpec((1, d), lambda i: (0, 0)),
                  pl.BlockSpec((tm, d), lambda i: (i, 0))] + list(rider.in_specs),
        out_specs=[pl.BlockSpec((tm, d), lambda i: (i, 0)),
                   pl.BlockSpec((1, d), lambda i: (0, 0))] + list(rider.out_specs),
        out_shape=[jax.ShapeDtypeStruct((t, d), F32), jax.ShapeDtypeStruct((1, d), F32)] + list(rider.out_shape),
        scratch_shapes=list(rider.scratch_shapes),
        compiler_params=_params("arbitrary"),
    )(dp, wt, h, g, dh_next, *rider.inputs)


def _dgrad_in_to_input(dp, wt, h, g, dh_next, n_head, tm, name, rider=_NO_RIDER):
    t, n = dp.shape
    d = wt.shape[1]
    nt = t // tm

    def body(dp_ref, wt_ref, h_ref, g_ref, dhn_ref, *rest):
        r_in, (dg_ref, head_ref, rest_ref), r_out, (pbuf, tout, slabs, sems), r_scratch = _split_refs(rider, 3, 4, rest)
        i = pl.program_id(0)
        rider(i == 0, i == nt // 2, None, r_in, r_out, r_scratch)
        dn = _dot(dp_ref[...], wt_ref[...], _NN)
        x = h_ref[...]
        ms = jnp.mean(x * x, axis=-1, keepdims=True)
        r = lax.rsqrt(ms + RMS_EPS)
        xhat = x * r
        dng = dn * g_ref[...]
        c = jnp.mean(dng * xhat, axis=-1, keepdims=True)
        pbuf[...] = dhn_ref[...] + r * (dng - xhat * c)
        part = jnp.sum(dn * xhat, axis=0, keepdims=True)

        @pl.when(i == 0)
        def _():
            dg_ref[...] = part

        @pl.when(i > 0)
        def _():
            dg_ref[...] += part

        def store(j):
            if isinstance(j, int) and j == 0:
                return pltpu.make_async_copy(tout.at[0, pl.ds(n_head, tm - n_head)],
                                             rest_ref.at[pl.ds(0, tm - n_head)], sems.at[0])
            start = j * tm - n_head
            start = start if isinstance(start, int) else pl.multiple_of(start, SUBLANES)
            return pltpu.make_async_copy(tout.at[j % 2], rest_ref.at[pl.ds(start, tm)], sems.at[j % 2])

        @pl.when(i == 2)
        def _():
            store(0).wait()

        @pl.when(i > 2)
        def _():
            store(i - 2).wait()

        _tile_to_time_order(pbuf, slabs, tout.at[i % 2])

        @pl.when(i == 0)
        def _():
            head_ref[...] = tout[0, 0:n_head, :]
            store(0).start()

        @pl.when(i > 0)
        def _():
            store(i).start()

        @pl.when(i == nt - 1)
        def _():
            for j in (nt - 2, nt - 1):
                if j >= 0:
                    store(j).wait()

        rider(None, None, i == nt - 1, r_in, r_out, r_scratch)

    return pl.pallas_call(
        body, name=name, grid=(nt,),
        in_specs=[pl.BlockSpec((tm, n), lambda i: (i, 0)),
                  _resident((n, d)),
                  pl.BlockSpec((tm, d), lambda i: (i, 0)),
                  pl.BlockSpec((1, d), lambda i: (0, 0)),
                  pl.BlockSpec((tm, d), lambda i: (i, 0))] + list(rider.in_specs),
        out_specs=[pl.BlockSpec((1, d), lambda i: (0, 0)), _full((n_head, d)), _ANY] + list(rider.out_specs),
        out_shape=[jax.ShapeDtypeStruct((1, d), F32), jax.ShapeDtypeStruct((n_head, d), F32),
                   jax.ShapeDtypeStruct((t - n_head, d), F32)] + list(rider.out_shape),
        scratch_shapes=[pltpu.VMEM((tm, d), F32)] + _time_scratch(tm, d) + list(rider.scratch_shapes),
        compiler_params=_params("arbitrary"),
    )(dp, wt, h, g, dh_next, *rider.inputs)


def _wgrad(a, b, name, rider=_NO_RIDER):
    t, m = a.shape
    d = b.shape[1]
    tmm, tk = _col_tile(m, 1024), _row_tile(t, ROW_TILE_WGRAD)
    if tmm < 512:
        tmm, tk = _col_tile(m, 1536), _row_tile(t, ROW_TILE_MATMUL)
    nm, nk = m // tmm, t // tk

    def body(a_ref, b_ref, *rest):
        r_in, (o_ref,), r_out, _, r_scratch = _split_refs(rider, 1, 0, rest)
        i, k = pl.program_id(0), pl.program_id(1)
        rider(jnp.logical_and(i == 0, k == 0), None, None, r_in, r_out, r_scratch)

        @pl.when(k == 0)
        def _():
            o_ref[...] = jnp.zeros_like(o_ref)

        o_ref[...] += _dot(a_ref[...], b_ref[...], _TN)
        rider(None, None, jnp.logical_and(i == nm - 1, k == nk - 1), r_in, r_out, r_scratch)

    outs = pl.pallas_call(
        body, name=name, grid=(nm, nk),
        in_specs=[pl.BlockSpec((tk, tmm), lambda i, k: (k, i)),
                  pl.BlockSpec((tk, d), lambda i, k: (k, 0))] + list(rider.in_specs),
        out_specs=[pl.BlockSpec((tmm, d), lambda i, k: (i, 0))] + list(rider.out_specs),
        out_shape=[jax.ShapeDtypeStruct((m, d), F32)] + list(rider.out_shape),
        scratch_shapes=list(rider.scratch_shapes),
        compiler_params=_params("arbitrary", "arbitrary"),
    )(a, b, *rider.inputs)
    return outs if rider.out_shape else outs[0]


def _ffn_up_fused(h, g, w, conv_w, tm, name, rider=_NO_RIDER):
    t, d = h.shape
    f2 = w.shape[1]
    f = f2 // 2
    cw = min(STRIP, f)
    kw = conv_w.shape[0]
    nh = kw - 1
    nt = t // tm

    def body(h_ref, g_ref, w_ref, cw_ref, *rest):
        r_in, (n_ref, a_ref, z_ref), r_out, (carry,), r_scratch = _split_refs(rider, 3, 1, rest)
        i = pl.program_id(0)
        rider(i == 0, i == nt // 2, None, r_in, r_out, r_scratch)

        @pl.when(i == 0)
        def _():
            carry[...] = jnp.zeros_like(carry)

        _rmsnorm_to(h_ref, g_ref, n_ref)
        for c in range(0, f, cw):
            conv = []
            for sl in (slice(c, c + cw), slice(f + c, f + c + cw)):
                a = _dot(n_ref[...], w_ref[:, sl], _NN)
                a_ref[:, sl] = a
                conv.append(_weighted(_rows(cw_ref, sl), _windows(a, _blocks(carry, sl, nh, True), kw)))
                carry[:, sl] = a[tm - nh * SUBLANES:tm]
            gg, vv = conv
            z_ref[:, c:c + cw] = (gg * _sigmoid(gg) * vv).astype(z_ref.dtype)
        rider(None, None, i == nt - 1, r_in, r_out, r_scratch)

    return pl.pallas_call(
        body, name=name, grid=(nt,),
        in_specs=[pl.BlockSpec((tm, d), lambda i: (i, 0)), pl.BlockSpec((1, d), lambda i: (0, 0)),
                  _resident((d, f2)), _full(conv_w.shape)] + list(rider.in_specs),
        out_specs=[pl.BlockSpec((tm, d), lambda i: (i, 0)), pl.BlockSpec((tm, f2), lambda i: (i, 0)),
                   pl.BlockSpec((tm, f), lambda i: (i, 0))] + list(rider.out_specs),
        out_shape=[jax.ShapeDtypeStruct((t, d), MXU_DTYPE), jax.ShapeDtypeStruct((t, f2), F32),
                   jax.ShapeDtypeStruct((t, f), MXU_DTYPE)] + list(rider.out_shape),
        scratch_shapes=[pltpu.VMEM((nh * SUBLANES, f2), F32)] + list(rider.scratch_shapes),
        compiler_params=_params("arbitrary"),
    )(h, g, w, conv_w, *rider.inputs)


def _ffn_down_bwd_fused(dh, wdt, a, conv_w, tm, name):
    t, d = dh.shape
    f = wdt.shape[1]
    f2 = 2 * f
    nt = t // tm
    cw = min(STRIP, f)
    kw = conv_w.shape[0]
    nh = kw - 1
    rev = lambda i: nt - 1 - i

    def body(dh_ref, wdt_ref, a_ref, *rest):
        halo_refs = rest[:nh]
        cw_ref, da_ref, dhb_ref, dw_ref, carry = rest[nh:]
        i = pl.program_id(0)
        newest, oldest = i == 0, i == nt - 1

        @pl.when(newest)
        def _():
            carry[...] = jnp.zeros_like(carry)
            dw_ref[...] = jnp.zeros_like(dw_ref)

        dhb_ref[...] = dh_ref[...].astype(dhb_ref.dtype)
        for c in range(0, f, cw):
            gsl, vsl = slice(c, c + cw), slice(f + c, f + c + cw)
            dz = _dot(dhb_ref[...], wdt_ref[:, gsl], _NN)
            win, conv = {}, {}
            for sl in (gsl, vsl):
                before = [jnp.where(oldest, 0.0, r[:, sl]) for r in halo_refs]
                win[sl.start] = _windows(a_ref[:, sl], before, kw)
                conv[sl.start] = _weighted(_rows(cw_ref, sl), win[sl.start])
            gg, vv = conv[gsl.start], conv[vsl.start]
            s = _sigmoid(gg)
            grads = {gsl.start: dz * vv * s * (1.0 + gg * (1.0 - s)), vsl.start: dz * gg * s}
            for sl in (gsl, vsl):
                dconv = grads[sl.start]
                da_ref[:, sl] = _conv_t(dconv, _blocks(carry, sl, nh, False), _rows(cw_ref, sl)).astype(da_ref.dtype)
                for k in range(kw):
                    dw_ref[k:k + 1, sl] += jnp.sum(dconv * win[sl.start][k], axis=0, keepdims=True)
                carry[:, sl] = dconv[0:nh * SUBLANES]

    return pl.pallas_call(
        body, name=name, grid=(nt,),
        in_specs=[pl.BlockSpec((tm, d), lambda i: (rev(i), 0)), _resident((d, f)),
                  pl.BlockSpec((tm, f2), lambda i: (rev(i), 0))] + _halo_specs(tm, f2, nh, rev) + [_full(conv_w.shape)],
        out_specs=[pl.BlockSpec((tm, f2), lambda i: (rev(i), 0)), pl.BlockSpec((tm, d), lambda i: (rev(i), 0)),
                   _full(conv_w.shape)],
        out_shape=[jax.ShapeDtypeStruct((t, f2), MXU_DTYPE), jax.ShapeDtypeStruct((t, d), MXU_DTYPE),
                   jax.ShapeDtypeStruct(conv_w.shape, F32)],
        scratch_shapes=[pltpu.VMEM((nh * SUBLANES, f2), F32)],
        compiler_params=_params("arbitrary"),
    )(dh, wdt, a, *([a] * nh), conv_w)


def _sc_in_fused(x, meta, g, w, conv_w, tm, name, rider=_NO_RIDER):
    d = x.shape[1]
    t = x.shape[0] + meta.shape[0]
    cw = min(STRIP, d)
    kw = conv_w.shape[0]
    nh = kw - 1
    nt = t // tm

    def body(x_ref, meta_ref, g_ref, w_ref, cw_ref, *rest):
        r_in, (h_ref, n_ref, p_ref, q_ref), r_out, (carry, tbuf, slabs, sems), r_scratch = _split_refs(rider, 4, 4, rest)
        i = pl.program_id(0)
        rider(i == 0, i == nt // 2, None, r_in, r_out, r_scratch)

        @pl.when(i == 0)
        def _():
            carry[...] = jnp.zeros_like(carry)

        _fetch_time_tile(i, nt, x_ref, meta_ref[...], tbuf, sems, tm)
        _time_to_tile_order(tbuf.at[i % 2], slabs, h_ref)
        _rmsnorm_to(h_ref, g_ref, n_ref)
        for c in range(0, d, cw):
            sl = slice(c, c + cw)
            parts = []
            for base in (0, d, 2 * d):
                psl = slice(base + c, base + c + cw)
                parts.append(_dot(n_ref[...], w_ref[:, psl], _NN))
                p_ref[:, psl] = parts[-1]
            bg, cg, v = parts
            cv = cg * v
            u = _weighted(_rows(cw_ref, sl), _windows(cv, _blocks(carry, sl, nh, True), kw))
            carry[:, sl] = cv[tm - nh * SUBLANES:tm]
            q_ref[:, sl] = (bg * u).astype(q_ref.dtype)
        rider(None, None, i == nt - 1, r_in, r_out, r_scratch)

    row_tile = lambda cols: pl.BlockSpec((tm, cols), lambda i: (i, 0))
    return pl.pallas_call(
        body, name=name, grid=(nt,),
        in_specs=[_ANY, _full(meta.shape), pl.BlockSpec((1, d), lambda i: (0, 0)),
                  _resident((d, 3 * d)), _full(conv_w.shape)] + list(rider.in_specs),
        out_specs=[row_tile(d), row_tile(d), row_tile(3 * d), row_tile(d)] + list(rider.out_specs),
        out_shape=[jax.ShapeDtypeStruct((t, d), F32), jax.ShapeDtypeStruct((t, d), MXU_DTYPE),
                   jax.ShapeDtypeStruct((t, 3 * d), F32), jax.ShapeDtypeStruct((t, d), MXU_DTYPE)]
                  + list(rider.out_shape),
        scratch_shapes=[pltpu.VMEM((nh * SUBLANES, d), F32)] + _time_scratch(tm, d) + list(rider.scratch_shapes),
        compiler_params=_params("arbitrary"),
    )(x, meta, g, w, conv_w, *rider.inputs)


def _sc_out_bwd_fused(dh, wot, p, conv_w, tm, name, rider=_NO_RIDER):
    t, d = dh.shape
    nt = t // tm
    cw = min(STRIP, d)
    kw = conv_w.shape[0]
    nh = kw - 1
    rev = lambda i: nt - 1 - i

    def body(dh_ref, wot_ref, p_ref, *rest):
        halo_refs, cw_ref = rest[:nh], rest[nh]
        r_in, (dp_ref, dhb_ref, dw_ref), r_out, (carry,), r_scratch = _split_refs(rider, 3, 1, rest[nh + 1:])
        i = pl.program_id(0)
        newest, oldest = i == 0, i == nt - 1
        rider(newest, i == nt // 2, None, r_in, r_out, r_scratch)

        @pl.when(newest)
        def _():
            carry[...] = jnp.zeros_like(carry)
            dw_ref[...] = jnp.zeros_like(dw_ref)

        dhb_ref[...] = dh_ref[...].astype(dhb_ref.dtype)
        for c in range(0, d, cw):
            sl, csl, vsl = slice(c, c + cw), slice(d + c, d + c + cw), slice(2 * d + c, 2 * d + c + cw)
            w = _rows(cw_ref, sl)
            dq = _dot(dhb_ref[...], wot_ref[:, sl], _NN)
            bg, cg, v = p_ref[:, sl], p_ref[:, csl], p_ref[:, vsl]
            before = [jnp.where(oldest, 0.0, r[:, csl] * r[:, vsl]) for r in halo_refs]
            win = _windows(cg * v, before, kw)
            du = dq * bg
            dcv = _conv_t(du, _blocks(carry, sl, nh, False), w)
            dp_ref[:, sl] = (dq * _weighted(w, win)).astype(dp_ref.dtype)
            dp_ref[:, csl] = (dcv * v).astype(dp_ref.dtype)
            dp_ref[:, vsl] = (dcv * cg).astype(dp_ref.dtype)
            for k in range(kw):
                dw_ref[k:k + 1, sl] += jnp.sum(du * win[k], axis=0, keepdims=True)
            carry[:, sl] = du[0:nh * SUBLANES]
        rider(None, None, oldest, r_in, r_out, r_scratch)

    tile = lambda cols: pl.BlockSpec((tm, cols), lambda i: (rev(i), 0))
    return pl.pallas_call(
        body, name=name, grid=(nt,),
        in_specs=[tile(d), _resident((d, d)), tile(3 * d)] + _halo_specs(tm, 3 * d, nh, rev) + [_full(conv_w.shape)]
                 + list(rider.in_specs),
        out_specs=[tile(3 * d), tile(d), _full(conv_w.shape)] + list(rider.out_specs),
        out_shape=[jax.ShapeDtypeStruct((t, 3 * d), MXU_DTYPE), jax.ShapeDtypeStruct((t, d), MXU_DTYPE),
                   jax.ShapeDtypeStruct(conv_w.shape, F32)] + list(rider.out_shape),
        scratch_shapes=[pltpu.VMEM((nh * SUBLANES, d), F32)] + list(rider.scratch_shapes),
        compiler_params=_params("arbitrary"),
    )(dh, wot, p, *([p] * nh), conv_w, *rider.inputs)


def _pair_gate_weights(wa, wx):
    nb, bd, _ = wa.shape
    zero = jnp.zeros((nb // 2, bd, bd), wa.dtype)

    def pair(w):
        w = w.reshape(nb // 2, 2, bd, bd)
        top = jnp.concatenate([w[:, 0], zero], axis=2)
        bottom = jnp.concatenate([zero, w[:, 1]], axis=2)
        return jnp.concatenate([top, bottom], axis=1)

    both = jnp.concatenate([pair(wa), pair(wx)], axis=2)
    return both, jnp.swapaxes(both, 1, 2)


def _unpair_gate_grads(dw, bd):
    def blocks(cols):
        d0 = dw[:, :bd, cols:cols + bd]
        d1 = dw[:, bd:, cols + bd:cols + 2 * bd]
        return jnp.stack([d0, d1], axis=1).reshape(-1, bd, bd)
    return blocks(0), blocks(2 * bd)


def _rg_gates(u, k, wg_ref, ba, bx, lam):
    ub = u.astype(MXU_DTYPE)
    z = _dot(ub, wg_ref[k], _NN)
    half = z.shape[1] // 2
    r = _sigmoid_tanh(z[:, :half] + ba)
    ig = _sigmoid_tanh(z[:, half:] + bx)
    sp = _softplus(-lam)
    la = -RG_C * r * sp
    a = jnp.exp(la)
    th = jnp.tanh(la)
    mult = jnp.sqrt(-2.0 * th / (1.0 - th))
    return r, ig, a, mult


def _rg_in_fused(h, g, w, conv_w, conv_b, wg, ba, bx, lam, tm, name, rider=_NO_RIDER):
    t, d = h.shape
    r2 = w.shape[1]
    rr = r2 // 2
    nb, bd = wg.shape[0], wg.shape[1]
    kw = conv_w.shape[0]
    nh = kw - 1
    groups = tm // SUBLANES
    nt = t // tm

    def body(h_ref, g_ref, w_ref, cw_ref, cb_ref, wg_ref, ba_ref, bx_ref, lam_ref, *rest):
        r_in, (n_ref, p_ref, y_ref, hs_ref), r_out, (a_scr, b_scr, carry_rb, carry_h), r_scratch = _split_refs(
            rider, 4, 4, rest)
        i = pl.program_id(0)
        rider(i == 0, i == nt // 2, None, r_in, r_out, r_scratch)

        @pl.when(i == 0)
        def _():
            carry_rb[...] = jnp.zeros_like(carry_rb)
            carry_h[...] = jnp.zeros_like(carry_h)

        _rmsnorm_to(h_ref, g_ref, n_ref)
        for k in range(nb):
            sl = slice(k * bd, (k + 1) * bd)
            rsl = slice(rr + k * bd, rr + (k + 1) * bd)
            p_ref[:, sl] = _dot(n_ref[...], w_ref[:, sl], _NN)
            rb = _dot(n_ref[...], w_ref[:, rsl], _NN)
            p_ref[:, rsl] = rb
            u = _weighted(_rows(cw_ref, sl), _windows(rb, _blocks(carry_rb, sl, nh, True), kw)) + cb_ref[:, sl]
            carry_rb[:, sl] = rb[tm - nh * SUBLANES:tm]
            _, ig, a, mult = _rg_gates(u, k, wg_ref, ba_ref[:, sl], bx_ref[:, sl], lam_ref[:, sl])
            a_scr[:, sl] = a
            b_scr[:, sl] = mult * (ig * u)

        carry_h[...] = _scan_tile(a_scr, b_scr, hs_ref, carry_h[...], groups, reverse=False)

        for k in range(nb):
            sl = slice(k * bd, (k + 1) * bd)
            y_ref[:, sl] = (hs_ref[:, sl] * _gelu(p_ref[:, sl])).astype(y_ref.dtype)
        rider(None, None, i == nt - 1, r_in, r_out, r_scratch)

    vm = lambda rows: pltpu.VMEM((rows, rr), F32)
    return pl.pallas_call(
        body, name=name, grid=(nt,),
        in_specs=[pl.BlockSpec((tm, d), lambda i: (i, 0)), pl.BlockSpec((1, d), lambda i: (0, 0)),
                  _resident((d, r2)), _full(conv_w.shape), _full(conv_b.shape), _full(wg.shape), _full(ba.shape),
                  _full(bx.shape), _full(lam.shape)] + list(rider.in_specs),
        out_specs=[pl.BlockSpec((tm, d), lambda i: (i, 0)), pl.BlockSpec((tm, r2), lambda i: (i, 0)),
                   pl.BlockSpec((tm, rr), lambda i: (i, 0)), pl.BlockSpec((tm, rr), lambda i: (i, 0))]
                  + list(rider.out_specs),
        out_shape=[jax.ShapeDtypeStruct((t, d), MXU_DTYPE), jax.ShapeDtypeStruct((t, r2), F32),
                   jax.ShapeDtypeStruct((t, rr), MXU_DTYPE), jax.ShapeDtypeStruct((t, rr), F32)]
                  + list(rider.out_shape),
        scratch_shapes=[vm(tm), vm(tm), vm(nh * SUBLANES), vm(SUBLANES)] + list(rider.scratch_shapes),
        compiler_params=_params("arbitrary"),
    )(h, g, w, conv_w, conv_b, wg, ba, bx, lam, *rider.inputs)


def _rg_out_bwd_fused(dh, wot, p, hs, conv_w, conv_b, wg, wgt, ba, bx, lam, tm, name, rider=_NO_RIDER):
    t, d = dh.shape
    r2 = p.shape[1]
    rr = r2 // 2
    nb, bd = wg.shape[0], wg.shape[1]
    nt = t // tm
    kw = conv_w.shape[0]
    nh = kw - 1
    groups = tm // SUBLANES
    rev = lambda i: nt - 1 - i

    def body(dh_ref, wot_ref, p_ref, hs_ref, hsh_ref, *rest):
        halo_refs = rest[:nh]
        cw_ref, cb_ref, wg_ref, wgt_ref, ba_ref, bx_ref, lam_ref = rest[nh:nh + 7]
        (r_in, (dp_ref, dhb_ref, dcw_ref, dcb_ref, dwg_ref, dba_ref, dbx_ref, dlam_ref), r_out,
         (a_scr, as_scr, u_scr, r_scr, ig_scr, mult_scr, g_scr, carry_g, carry_du), r_scratch) = _split_refs(
            rider, 8, 9, rest[nh + 7:])
        i = pl.program_id(0)
        newest, oldest = i == 0, i == nt - 1
        rider(newest, i == nt // 2, None, r_in, r_out, r_scratch)

        @pl.when(newest)
        def _():
            carry_g[...] = jnp.zeros_like(carry_g)
            carry_du[...] = jnp.zeros_like(carry_du)
            for ref in (dcw_ref, dcb_ref, dwg_ref, dba_ref, dbx_ref, dlam_ref):
                ref[...] = jnp.zeros_like(ref)

        dhb_ref[...] = dh_ref[...].astype(dhb_ref.dtype)
        ones = jnp.ones((SUBLANES, bd), F32)

        def conv_windows(sl, rsl):
            before = [jnp.where(oldest, 0.0, r[:, rsl]) for r in halo_refs]
            return _windows(p_ref[:, rsl], before, kw)

        for k in range(nb):
            sl = slice(k * bd, (k + 1) * bd)
            rsl = slice(rr + k * bd, rr + (k + 1) * bd)
            dy = _dot(dhb_ref[...], wot_ref[:, sl], _NN)
            u = _weighted(_rows(cw_ref, sl), conv_windows(sl, rsl)) + cb_ref[:, sl]
            r, ig, a, mult = _rg_gates(u, k, wg_ref, ba_ref[:, sl], bx_ref[:, sl], lam_ref[:, sl])
            gate, dgate = _gelu_and_grad(p_ref[:, sl])
            g_scr[:, sl] = dy * gate
            dp_ref[:, sl] = (dy * hs_ref[:, sl] * dgate).astype(dp_ref.dtype)
            a_scr[:, sl] = a
            as_scr[:, sl] = _shift_up(a, [ones], 1)
            u_scr[:, sl] = u
            r_scr[:, sl] = r
            ig_scr[:, sl] = ig
            mult_scr[:, sl] = mult

        g_first = _scan_tile(as_scr, g_scr, g_scr, carry_g[...], groups, reverse=True)
        carry_g[...] = jnp.broadcast_to(a_scr[0:1, :], (SUBLANES, rr)) * g_first

        for k in range(nb):
            sl = slice(k * bd, (k + 1) * bd)
            rsl = slice(rr + k * bd, rr + (k + 1) * bd)
            cw = _rows(cw_ref, sl)
            lam_k = lam_ref[:, sl]
            sp = _softplus(-lam_k)
            g = g_scr[:, sl]
            a, u, r, ig, mult = a_scr[:, sl], u_scr[:, sl], r_scr[:, sl], ig_scr[:, sl], mult_scr[:, sl]
            da = g * _shift_down(hs_ref[:, sl], [jnp.where(oldest, 0.0, hsh_ref[:, sl])], 1)
            dmult = g * (ig * u)
            d_iu = g * mult
            dla = da * a - dmult * (a * a) / mult
            dr = dla * (-RG_C * sp)
            dsp = jnp.sum(dla * (-RG_C * r), axis=0, keepdims=True)
            dlam_ref[:, sl] += dsp * (-_sigmoid(-lam_k))
            dza = dr * r * (1.0 - r)
            dzx = (d_iu * u) * ig * (1.0 - ig)
            dba_ref[:, sl] += jnp.sum(dza, axis=0, keepdims=True)
            dbx_ref[:, sl] += jnp.sum(dzx, axis=0, keepdims=True)
            ub = u.astype(MXU_DTYPE)
            dz = jnp.concatenate([dza, dzx], axis=1).astype(MXU_DTYPE)
            dwg_ref[k] += _dot(ub, dz, _TN)
            du = d_iu * ig + _dot(dz, wgt_ref[k], _NN)
            dcb_ref[:, sl] += jnp.sum(du, axis=0, keepdims=True)
            win = conv_windows(sl, rsl)
            for kk in range(kw):
                dcw_ref[kk:kk + 1, sl] += jnp.sum(du * win[kk], axis=0, keepdims=True)
            dp_ref[:, rsl] = _conv_t(du, _blocks(carry_du, sl, nh, False), cw).astype(dp_ref.dtype)
            carry_du[:, sl] = du[0:nh * SUBLANES]
        rider(None, None, oldest, r_in, r_out, r_scratch)

    tile = lambda cols: pl.BlockSpec((tm, cols), lambda i: (rev(i), 0))
    vm = lambda rows: pltpu.VMEM((rows, rr), F32)
    grads = [conv_w.shape, conv_b.shape, wg.shape, ba.shape, bx.shape, lam.shape]
    return pl.pallas_call(
        body, name=name, grid=(nt,),
        in_specs=[tile(d), _resident((d, rr)), tile(r2), tile(rr)] + _halo_specs(tm, rr, 1, rev)
                 + _halo_specs(tm, r2, nh, rev)
                 + [_full(conv_w.shape), _full(conv_b.shape), _full(wg.shape), _full(wgt.shape), _full(ba.shape),
                    _full(bx.shape), _full(lam.shape)] + list(rider.in_specs),
        out_specs=[tile(r2), tile(d)] + [_full(s) for s in grads] + list(rider.out_specs),
        out_shape=[jax.ShapeDtypeStruct((t, r2), MXU_DTYPE), jax.ShapeDtypeStruct((t, d), MXU_DTYPE)]
                  + [jax.ShapeDtypeStruct(s, F32) for s in grads] + list(rider.out_shape),
        scratch_shapes=[vm(tm)] * 7 + [vm(SUBLANES), vm(nh * SUBLANES)] + list(rider.scratch_shapes),
        compiler_params=_params("arbitrary"),
    )(dh, wot, p, hs, hs, *([p] * nh), conv_w, conv_b, wg, wgt, ba, bx, lam, *rider.inputs)


def _down_loss_head(z, w, h, g, target, n_meta, tm, name):
    t, d = h.shape
    k = z.shape[1]
    groups = tm // SUBLANES
    nt = t // tm

    def body(z_ref, w_ref, h_ref, g_ref, t_ref, dh_ref, loss_ref, dg_ref, tgt, tbuf, slabs, sems):
        i = pl.program_id(0)
        _fetch_time_tile(i, nt, t_ref, jnp.zeros((n_meta, d), F32), tbuf, sems, tm)
        _time_to_tile_order(tbuf.at[i % 2], slabs, tgt)
        x = h_ref[...] + _dot(z_ref[...], w_ref[...], _NN)
        ms = jnp.mean(x * x, axis=-1, keepdims=True)
        r = lax.rsqrt(ms + RMS_EPS)
        xhat = x * r
        gg = g_ref[...]
        row = lax.broadcasted_iota(jnp.int32, (tm, 1), 0)
        time = i * tm + jnp.right_shift(row, 3) + jnp.bitwise_and(row, SUBLANES - 1) * groups
        err = jnp.where(time >= n_meta, xhat * gg - tgt[...], 0.0)
        dout = err * (1.0 / d)
        dng = dout * gg
        c = jnp.mean(dng * xhat, axis=-1, keepdims=True)
        dh_ref[...] = r * (dng - xhat * c)
        part_dg = jnp.sum(dout * xhat, axis=0, keepdims=True)
        part_loss = jnp.broadcast_to(0.5 * jnp.sum(err * dout, keepdims=True), loss_ref.shape)

        @pl.when(i == 0)
        def _():
            dg_ref[...] = part_dg
            loss_ref[...] = part_loss

        @pl.when(i > 0)
        def _():
            dg_ref[...] += part_dg
            loss_ref[...] += part_loss

    return pl.pallas_call(
        body, name=name, grid=(nt,),
        in_specs=[pl.BlockSpec((tm, k), lambda i: (i, 0)),
                  _resident((k, d)),
                  pl.BlockSpec((tm, d), lambda i: (i, 0)),
                  pl.BlockSpec((1, d), lambda i: (0, 0)),
                  _ANY],
        out_specs=[pl.BlockSpec((tm, d), lambda i: (i, 0)),
                   pl.BlockSpec((SUBLANES, LANES), lambda i: (0, 0)),
                   pl.BlockSpec((1, d), lambda i: (0, 0))],
        out_shape=[jax.ShapeDtypeStruct((t, d), F32), jax.ShapeDtypeStruct((SUBLANES, LANES), F32),
                   jax.ShapeDtypeStruct((1, d), F32)],
        scratch_shapes=[pltpu.VMEM((tm, d), F32)] + _time_scratch(tm, d),
        compiler_params=_params("arbitrary"),
    )(z, w, h, g, target)


def _adamw(w, g, m, v, name):
    rows, cols = w.shape
    tr = rows
    if rows > 512:
        for cand in range(8, 513, 8):
            if rows % cand == 0:
                tr = cand

    def body(w_ref, g_ref, m_ref, v_ref, d_ref, nm_ref, nv_ref):
        g_ = g_ref[...]
        m_ = ADAM_B1 * m_ref[...] + (1.0 - ADAM_B1) * g_
        v_ = ADAM_B2 * v_ref[...] + (1.0 - ADAM_B2) * (g_ * g_)
        m_hat = m_ / (1.0 - ADAM_B1 ** ADAM_STEP)
        v_hat = v_ / (1.0 - ADAM_B2 ** ADAM_STEP)
        d_ref[...] = -ADAM_LR * (m_hat / (jnp.sqrt(v_hat) + ADAM_EPS) + ADAM_WD * w_ref[...])
        nm_ref[...] = m_
        nv_ref[...] = v_

    spec = pl.BlockSpec((tr, cols), lambda i: (i, 0))
    shape = jax.ShapeDtypeStruct((rows, cols), F32)
    return pl.pallas_call(
        body, name=name, grid=(rows // tr,),
        in_specs=[spec] * 4, out_specs=[spec] * 3, out_shape=[shape] * 3,
        compiler_params=_params("parallel"),
    )(w, g, m, v)


def _adamw_nd(w, g, m, v, name):
    shape = w.shape
    two_d = (-1, shape[-1]) if w.ndim > 1 else (1, -1)
    outs = _adamw(w.reshape(two_d), g.reshape(two_d), m.reshape(two_d), v.reshape(two_d), name)
    return tuple(o.reshape(shape) for o in outs)


def _all_gather(blk, name):
    rows, cols = blk.shape

    def body(x_ref, out_ref, send_sems, recv_sems, local_sem):
        x, y, c = lax.axis_index("x"), lax.axis_index("y"), lax.axis_index("c")
        me, sibling = (x, y, c), (x, y, 1 - c)
        chips = [(1 - x, y), (x, 1 - y), (1 - x, 1 - y)]

        def slot(px, py, pc):
            return out_ref.at[4 * px + 2 * py + pc]

        def copy(k, block, to, src=None):
            return pltpu.make_async_remote_copy(
                src_ref=slot(*block) if src is None else src, dst_ref=slot(*block),
                send_sem=send_sems.at[k], recv_sem=recv_sems.at[k], device_id=to, device_id_type=_MESH)

        mine = pltpu.make_async_copy(x_ref, slot(*me), local_sem)
        mine.start()
        first = [copy(0, me, sibling, src=x_ref)]
        first += [copy(1 + j, me, (*chip, c), src=x_ref) for j, chip in enumerate(chips)]
        for cp in first:
            cp.start()
        passed = [copy(4 + j, (*chip, c), sibling) for j, chip in enumerate(chips)]
        for j, chip in enumerate(chips):
            copy(1 + j, (*chip, c), me).wait_recv()
            passed[j].start()
        copy(0, sibling, me).wait_recv()
        for j, chip in enumerate(chips):
            copy(4 + j, (*chip, 1 - c), me).wait_recv()
        for cp in first + passed:
            cp.wait_send()
        mine.wait()

    return pl.pallas_call(
        body, name=name,
        in_specs=[_ANY], out_specs=_ANY,
        out_shape=jax.ShapeDtypeStruct((N_DEV, rows, cols), blk.dtype),
        scratch_shapes=[pltpu.SemaphoreType.DMA((7,)), pltpu.SemaphoreType.DMA((7,)), pltpu.SemaphoreType.DMA(())],
    )(blk)


def _rs_sibling_exchange(buf, name):
    _, rows, cols = buf.shape

    def body(buf_ref, out_ref, send_sems, recv_sems):
        x, y, c = lax.axis_index("x"), lax.axis_index("y"), lax.axis_index("c")
        copies = [pltpu.make_async_remote_copy(
            src_ref=buf_ref.at[2 * k + (1 - c)], dst_ref=out_ref.at[k],
            send_sem=send_sems.at[k], recv_sem=recv_sems.at[k],
            device_id=(x, y, 1 - c), device_id_type=_MESH) for k in range(4)]
        for cp in copies:
            cp.start()
        for cp in copies:
            cp.wait()

    return pl.pallas_call(
        body, name=name,
        in_specs=[_ANY], out_specs=_ANY,
        out_shape=jax.ShapeDtypeStruct((4, rows, cols), buf.dtype),
        scratch_shapes=[pltpu.SemaphoreType.DMA((4,)), pltpu.SemaphoreType.DMA((4,))],
    )(buf)


def _rs_chip_sum(buf, recv, core, name):
    _, rows, cols = buf.shape
    tr = _row_tile(rows, 512) if rows % 16 == 0 else rows

    def body(c_ref, a_ref, b_ref, o_ref):
        o_ref[...] = a_ref[...] + b_ref[...]

    grid_spec = pltpu.PrefetchScalarGridSpec(
        num_scalar_prefetch=1, grid=(4, rows // tr),
        in_specs=[pl.BlockSpec((None, tr, cols), lambda k, j, c: (2 * k + c[0], j, 0)),
                  pl.BlockSpec((None, tr, cols), lambda k, j, c: (k, j, 0))],
        out_specs=pl.BlockSpec((None, tr, cols), lambda k, j, c: (k, j, 0)))
    return pl.pallas_call(
        body, name=name, grid_spec=grid_spec,
        out_shape=jax.ShapeDtypeStruct((4, rows, cols), buf.dtype),
        compiler_params=_params("parallel", "parallel"),
    )(core, buf, recv)


def _rs_chip_exchange(chip_sum, name):
    _, rows, cols = chip_sum.shape

    def body(cs_ref, out_ref, send_sems, recv_sems, local_sem):
        x, y, c = lax.axis_index("x"), lax.axis_index("y"), lax.axis_index("c")
        my_chip = 2 * x + y
        mine = pltpu.make_async_copy(cs_ref.at[my_chip], out_ref.at[my_chip], local_sem)
        mine.start()
        chips = [(1 - x, y), (x, 1 - y), (1 - x, 1 - y)]
        copies = [pltpu.make_async_remote_copy(
            src_ref=cs_ref.at[2 * px + py], dst_ref=out_ref.at[my_chip],
            send_sem=send_sems.at[j], recv_sem=recv_sems.at[j],
            device_id=(px, py, c), device_id_type=_MESH) for j, (px, py) in enumerate(chips)]
        for cp in copies:
            cp.start()
        for cp in copies:
            cp.wait()
        mine.wait()

    return pl.pallas_call(
        body, name=name,
        in_specs=[_ANY], out_specs=_ANY,
        out_shape=jax.ShapeDtypeStruct((4, rows, cols), chip_sum.dtype),
        scratch_shapes=[pltpu.SemaphoreType.DMA((3,)), pltpu.SemaphoreType.DMA((3,)), pltpu.SemaphoreType.DMA(())],
    )(chip_sum)


def _sum_slots(parts, name):
    slots, rows, cols = parts.shape
    tr = _row_tile(rows, 256) if rows % 16 == 0 else rows

    def body(p_ref, o_ref):
        acc = p_ref[0]
        for s in range(1, slots):
            acc = acc + p_ref[s]
        o_ref[...] = acc

    return pl.pallas_call(
        body, name=name, grid=(rows // tr,),
        in_specs=[pl.BlockSpec((slots, tr, cols), lambda i: (0, i, 0))],
        out_specs=pl.BlockSpec((tr, cols), lambda i: (i, 0)),
        out_shape=jax.ShapeDtypeStruct((rows, cols), parts.dtype),
        compiler_params=_params("parallel"),
    )(parts)


def _reduce_scatter(buf, core):
    recv = _rs_sibling_exchange(buf, "rs_sibling_exchange")
    chip_sum = _rs_chip_sum(buf, recv, core, "rs_chip_sum")
    parts = _rs_chip_exchange(chip_sum, "rs_chip_exchange")
    return _sum_slots(parts, "rs_sum_slots")


def _pad_rows(flat, cols, multiple):
    n = flat.shape[0]
    rows = -(-n // cols)
    rows = -(-rows // multiple) * multiple
    return jnp.pad(flat, (0, rows * cols - n)).reshape(rows, cols)


def _cols_to_chunks(full):
    lead = full.shape[:-1]
    c = full.shape[-1] // N_DEV
    x = full.reshape(-1, N_DEV, c)
    return jnp.transpose(x, (1, 0, 2)).reshape(N_DEV, -1)


def _chunks_to_cols(chunks, lead):
    n = 1
    for s in lead:
        n *= s
    c = chunks.shape[1] // n
    x = chunks.reshape(N_DEV, n, c)
    return jnp.transpose(x, (1, 0, 2)).reshape(tuple(lead) + (N_DEV * c,))


def kernel(x, meta_tokens, norm_mix_g, norm_ffn_g, final_norm_g, sc_w_in, sc_conv_w, sc_w_out, rg_w_in, rg_conv_w, rg_conv_b, rg_w_gate_a, rg_b_gate_a, rg_w_gate_x, rg_b_gate_x, rg_lambda, rg_w_out, ffn_w_up, ffn_conv_w, ffn_w_down, loss_target, m_meta_tokens, m_norm_mix_g, m_norm_ffn_g, m_final_norm_g, m_sc_w_in, m_sc_conv_w, m_sc_w_out, m_rg_w_in, m_rg_conv_w, m_rg_conv_b, m_rg_w_gate_a, m_rg_b_gate_a, m_rg_w_gate_x, m_rg_b_gate_x, m_rg_lambda, m_rg_w_out, m_ffn_w_up, m_ffn_conv_w, m_ffn_w_down, v_meta_tokens, v_norm_mix_g, v_norm_ffn_g, v_final_norm_g, v_sc_w_in, v_sc_conv_w, v_sc_w_out, v_rg_w_in, v_rg_conv_w, v_rg_conv_b, v_rg_w_gate_a, v_rg_b_gate_a, v_rg_w_gate_x, v_rg_b_gate_x, v_rg_lambda, v_rg_w_out, v_ffn_w_up, v_ffn_conv_w, v_ffn_w_down):
    weights = dict(meta_tokens=meta_tokens, norm_mix_g=norm_mix_g, norm_ffn_g=norm_ffn_g, final_norm_g=final_norm_g,
                   sc_w_in=sc_w_in, sc_conv_w=sc_conv_w, sc_w_out=sc_w_out, rg_w_in=rg_w_in, rg_conv_w=rg_conv_w,
                   rg_conv_b=rg_conv_b, rg_w_gate_a=rg_w_gate_a, rg_b_gate_a=rg_b_gate_a, rg_w_gate_x=rg_w_gate_x,
                   rg_b_gate_x=rg_b_gate_x, rg_lambda=rg_lambda, rg_w_out=rg_w_out, ffn_w_up=ffn_w_up,
                   ffn_conv_w=ffn_conv_w, ffn_w_down=ffn_w_down)
    m_in = dict(meta_tokens=m_meta_tokens, norm_mix_g=m_norm_mix_g, norm_ffn_g=m_norm_ffn_g, final_norm_g=m_final_norm_g,
                sc_w_in=m_sc_w_in, sc_conv_w=m_sc_conv_w, sc_w_out=m_sc_w_out, rg_w_in=m_rg_w_in, rg_conv_w=m_rg_conv_w,
                rg_conv_b=m_rg_conv_b, rg_w_gate_a=m_rg_w_gate_a, rg_b_gate_a=m_rg_b_gate_a, rg_w_gate_x=m_rg_w_gate_x,
                rg_b_gate_x=m_rg_b_gate_x, rg_lambda=m_rg_lambda, rg_w_out=m_rg_w_out, ffn_w_up=m_ffn_w_up,
                ffn_conv_w=m_ffn_conv_w, ffn_w_down=m_ffn_w_down)
    v_in = dict(meta_tokens=v_meta_tokens, norm_mix_g=v_norm_mix_g, norm_ffn_g=v_norm_ffn_g, final_norm_g=v_final_norm_g,
                sc_w_in=v_sc_w_in, sc_conv_w=v_sc_conv_w, sc_w_out=v_sc_w_out, rg_w_in=v_rg_w_in, rg_conv_w=v_rg_conv_w,
                rg_conv_b=v_rg_conv_b, rg_w_gate_a=v_rg_w_gate_a, rg_b_gate_a=v_rg_b_gate_a, rg_w_gate_x=v_rg_w_gate_x,
                rg_b_gate_x=v_rg_b_gate_x, rg_lambda=v_rg_lambda, rg_w_out=v_rg_w_out, ffn_w_up=v_ffn_w_up,
                ffn_conv_w=v_ffn_conv_w, ffn_w_down=v_ffn_w_down)
    names = list(weights)

    seq, d = x.shape[1], x.shape[2]
    n_meta = meta_tokens.shape[0]
    n_ffn = ffn_w_up.shape[0]
    rr = rg_w_gate_a.shape[1] * rg_w_gate_a.shape[2]
    core = lax.axis_index("c").astype(jnp.int32).reshape(1)

    assert n_ffn == 2
    def shard_rows(w_in, w_out):
        return jnp.concatenate([w_in.T, w_out], axis=0).astype(MXU_DTYPE)

    def both_orientations(gathered, w_in):
        r_in = w_in.shape[1]
        w_in_t = gathered[:, :r_in, :].reshape(N_DEV * r_in, d)
        w_out = gathered[:, r_in:, :].reshape(-1, d)
        return w_in_t.T, w_in_t, w_out, w_out.T

    shards = {"ffn0": shard_rows(ffn_w_up[0], ffn_w_down[0]), "rg": shard_rows(rg_w_in[0], rg_w_out[0]),
              "ffn1": shard_rows(ffn_w_up[1], ffn_w_down[1])}
    sc_w_in_f, sc_w_in_t, sc_w_out_f, sc_w_out_t = both_orientations(
        _all_gather(shard_rows(sc_w_in[0], sc_w_out[0]), "gather_sc"), sc_w_in[0])
    ffn_w_up_f, ffn_w_up_t, ffn_w_down_f, ffn_w_down_t = [None] * 2, [None] * 2, [None] * 2, [None] * 2

    small_names = ["meta_tokens", "sc_conv_w", "rg_conv_w", "rg_conv_b", "rg_b_gate_a", "rg_b_gate_x", "rg_lambda",
                   "ffn_conv_w"]
    small_lead = {n: weights[n].shape[:-1] for n in small_names}
    small_sizes = [weights[n].size for n in small_names]
    small_flat = jnp.concatenate([weights[n].reshape(-1) for n in small_names])
    small_rows = _pad_rows(small_flat, d, SUBLANES)
    small_g = _all_gather(small_rows, "gather_small").reshape(N_DEV, -1)
    small_full = {}
    o = 0
    for n, sz in zip(small_names, small_sizes):
        small_full[n] = _chunks_to_cols(small_g[:, o:o + sz], small_lead[n])
        o += sz

    wg, wgt = _pair_gate_weights(rg_w_gate_a[0].astype(MXU_DTYPE), rg_w_gate_x[0].astype(MXU_DTYPE))
    rg_cw, rg_cb = small_full["rg_conv_w"][0], small_full["rg_conv_b"]
    rg_ba, rg_bx, rg_lam = small_full["rg_b_gate_a"], small_full["rg_b_gate_x"], small_full["rg_lambda"]
    sc_cw = small_full["sc_conv_w"][0]
    ffn_cw = small_full["ffn_conv_w"]

    tp = _row_tile(n_meta + seq, ROW_TILE_PERM)

    def ffn_fwd(h, l, rider):
        n, a, z, *gathered = _ffn_up_fused(h, norm_ffn_g[l:l + 1], ffn_w_up_f[l], ffn_cw[l], tp, f"ffn{l}_up", rider)
        return _matmul_residual(z, ffn_w_down_f[l], h, f"ffn{l}_down"), (n, a, z), gathered

    h0, n0, p0, q0, ffn0_w = _sc_in_fused(x[0], small_full["meta_tokens"], norm_mix_g[0:1], sc_w_in_f, sc_cw, tp, "sc_in",
                                          _GatherRider(shards["ffn0"]))
    ffn_w_up_f[0], ffn_w_up_t[0], ffn_w_down_f[0], ffn_w_down_t[0] = both_orientations(ffn0_w, ffn_w_up[0])
    h1 = _matmul_residual(q0, sc_w_out_f, h0, "sc_out")
    h2, ffn0_saved, (rg_w,) = ffn_fwd(h1, 0, _GatherRider(shards["rg"]))
    rg_w_in_f, rg_w_in_t, rg_w_out_f, rg_w_out_t = both_orientations(rg_w, rg_w_in[0])
    n2, p2, y2, hs2, ffn1_w = _rg_in_fused(h2, norm_mix_g[1:2], rg_w_in_f, rg_cw, rg_cb, wg, rg_ba, rg_bx, rg_lam,
                                           tp, "rg_in", _GatherRider(shards["ffn1"]))
    ffn_w_up_f[1], ffn_w_up_t[1], ffn_w_down_f[1], ffn_w_down_t[1] = both_orientations(ffn1_w, ffn_w_up[1])
    h3 = _matmul_residual(y2, rg_w_out_f, h2, "rg_out")
    ffn1_saved = tuple(_ffn_up_fused(h3, norm_ffn_g[1:2], ffn_w_up_f[1], ffn_cw[1], tp, "ffn1_up"))
    dh4, loss_tile, d_final_g = _down_loss_head(ffn1_saved[2], ffn_w_down_f[1], h3, final_norm_g.reshape(1, d),
                                                loss_target[0], n_meta, tp, "ffn1_down_loss_head")
    loss = lax.psum(loss_tile[0, 0], AXES)

    arrived = {}

    def ffn_bwd(dh_out, h_in, saved, l):
        n, a, z = saved
        da, dhb, d_cw = _ffn_down_bwd_fused(dh_out, ffn_w_down_t[l], a, ffn_cw[l], tp, f"ffn{l}_down_bwd")
        d_w_down = _wgrad(z, dhb, f"ffn{l}_down_wgrad")
        d_w_up_t, arrived[f"ffn_w_down{l}"] = _wgrad(da, n, f"ffn{l}_up_wgrad", _ScatterRider(d_w_down))
        rows = d_w_up_t.shape[0] // N_DEV
        first_rows = (rows // 2) // (2 * SUBLANES) * (2 * SUBLANES)
        dh_in, d_g, arrived[f"ffn_w_up{l}a"] = _dgrad_in_norm(
            da, ffn_w_up_t[l], h_in, norm_ffn_g[l:l + 1], dh_out, f"ffn{l}_up_dgrad",
            _ScatterRider(d_w_up_t, (0, first_rows)))
        return dh_in, d_cw, d_g, _ScatterRider(d_w_up_t, (first_rows, rows - first_rows))

    dh3, d_fcw1, d_fg1, up1_rest = ffn_bwd(dh4, h3, ffn1_saved, 1)

    dp2, dhb3, d_rg_cw, d_rg_cb, d_wg, d_rg_ba, d_rg_bx, d_rg_lam, arrived["ffn_w_up1b"] = _rg_out_bwd_fused(
        dh3, rg_w_out_t, p2, hs2, rg_cw, rg_cb, wg, wgt, rg_ba, rg_bx, rg_lam, tp, "rg_out_bwd", up1_rest)
    d_wa, d_wx = _unpair_gate_grads(d_wg, rg_w_gate_a.shape[2])
    d_rg_w_out = _wgrad(y2, dhb3, "rg_out_wgrad")
    d_rg_w_in_t, arrived["rg_w_out"] = _wgrad(dp2, n2, "rg_in_wgrad", _ScatterRider(d_rg_w_out))
    dh2, d_mg1, arrived["rg_w_in"] = _dgrad_in_norm(
        dp2, rg_w_in_t, h2, norm_mix_g[1:2], dh3, "rg_in_dgrad", _ScatterRider(d_rg_w_in_t))

    dh1, d_fcw0, d_fg0, up0_rest = ffn_bwd(dh2, h1, ffn0_saved, 0)

    dp0, dhb1, d_sc_cw, arrived["ffn_w_up0b"] = _sc_out_bwd_fused(dh1, sc_w_out_t, p0, sc_cw, tp, "sc_out_bwd", up0_rest)
    d_sc_w_out = _wgrad(q0, dhb1, "sc_out_wgrad")
    d_sc_w_in_t, arrived["sc_w_out"] = _wgrad(dp0, n0, "sc_in_wgrad", _ScatterRider(d_sc_w_out))
    d_mg0, d_meta, d_x, arrived["sc_w_in"] = _dgrad_in_to_input(
        dp0, sc_w_in_t, h0, norm_mix_g[0:1], dh1, n_meta, tp, "sc_in_dgrad", _ScatterRider(d_sc_w_in_t))
    grad_x = d_x[None]

    grads = {}
    for n in ("sc_w_in", "rg_w_in"):
        grads[n] = _sum_slots(arrived[n], f"sum_{n}").T[None]
    for n in ("sc_w_out", "rg_w_out"):
        grads[n] = _sum_slots(arrived[n], f"sum_{n}")[None]
    grads["ffn_w_up"] = jnp.stack([jnp.concatenate(
        [_sum_slots(arrived[f"ffn_w_up{l}{part}"], f"sum_ffn_w_up{l}{part}") for part in "ab"], axis=0).T
        for l in range(n_ffn)])
    grads["ffn_w_down"] = jnp.stack([_sum_slots(arrived[f"ffn_w_down{l}"], f"sum_ffn_w_down{l}") for l in range(n_ffn)])

    small_grads = {"meta_tokens": d_meta,"sc_conv_w": d_sc_cw[None], "rg_conv_w": d_rg_cw[None],
                   "rg_conv_b": d_rg_cb, "rg_b_gate_a": d_rg_ba, "rg_b_gate_x": d_rg_bx, "rg_lambda": d_rg_lam,
                   "ffn_conv_w": jnp.stack([d_fcw0, d_fcw1])}
    small_chunks = jnp.concatenate([_cols_to_chunks(small_grads[n]) for n in small_names], axis=1)
    pad = small_rows.shape[0] * d - small_chunks.shape[1]
    small_chunks = jnp.pad(small_chunks, ((0, 0), (0, pad))).reshape(N_DEV, small_rows.shape[0], d)
    rep_names = ["norm_mix_g", "norm_ffn_g", "final_norm_g", "rg_w_gate_a", "rg_w_gate_x"]
    rep_grads = {"norm_mix_g": jnp.concatenate([d_mg0, d_mg1], axis=0),
                 "norm_ffn_g": jnp.concatenate([d_fg0, d_fg1], axis=0),
                 "final_norm_g": d_final_g.reshape(-1), "rg_w_gate_a": d_wa[None], "rg_w_gate_x": d_wx[None]}
    rep_flat = jnp.concatenate([rep_grads[n].reshape(-1) for n in rep_names])
    rep_chunk_rows = -(-rep_flat.shape[0] // (N_DEV * d))
    rep_chunk_rows += -(small_rows.shape[0] + rep_chunk_rows) % 16
    rep_chunks = jnp.pad(rep_flat, (0, N_DEV * rep_chunk_rows * d - rep_flat.shape[0])).reshape(N_DEV, rep_chunk_rows, d)

    reduced = _reduce_scatter(jnp.concatenate([small_chunks, rep_chunks], axis=1), core)

    small_red = reduced[0:small_rows.shape[0]].reshape(-1)
    o = small_rows.shape[0]
    so = 0
    for n, sz in zip(small_names, small_sizes):
        grads[n] = small_red[so:so + sz].reshape(weights[n].shape)
        so += sz
    rep_red = _all_gather(reduced[o:o + rep_chunks.shape[1]], "gather_replicated_grads").reshape(-1)
    ro = 0
    for n in rep_names:
        sz = weights[n].size
        grads[n] = rep_red[ro:ro + sz].reshape(weights[n].shape)
        ro += sz

    delta, new_m, new_v = {}, {}, {}
    for n in names:
        delta[n], new_m[n], new_v[n] = _adamw_nd(weights[n], grads[n], m_in[n], v_in[n], f"adamw_{n}")

    return (loss, grad_x, *[grads[n] for n in names], *[delta[n] for n in names],
            *[new_m[n] for n in names], *[new_v[n] for n in names])
```

```python
import jax
import jax.numpy as jnp
from jax import lax
from jax.experimental import pallas as pl
from jax.experimental.pallas import tpu as pltpu

F32 = jnp.float32
MXU_DTYPE = jnp.bfloat16
RMS_EPS = 1e-6
RG_C = 8.0
ADAM_LR = 0.001
ADAM_B1 = 0.9
ADAM_B2 = 0.999
ADAM_EPS = 1e-08
ADAM_WD = 0.01
ADAM_STEP = 10

N_DEV = 8
AXES = ("x", "y", "c")
SUBLANES = 8
LANES = 128
VMEM_LIMIT_BYTES = 48 * 1024 * 1024
ROW_TILE_MATMUL = 700
ROW_TILE_WGRAD = 3300
ROW_TILE_PERM = 400
STRIP = 256

_TN = (((0,), (0,)), ((), ()))
_NN = (((1,), (0,)), ((), ()))


def _row_tile(t, target):
    best = None
    for tm in range(16, t + 1, 16):
        if t % tm == 0 and tm <= target:
            best = tm
    return best if best is not None else t


def _col_tile(n, target):
    best = None
    for tn in range(LANES, n + 1, LANES):
        if n % tn == 0 and tn <= target:
            best = tn
    return best if best is not None else n


def _params(*sem):
    return pltpu.CompilerParams(dimension_semantics=sem, vmem_limit_bytes=VMEM_LIMIT_BYTES)


def _dot(a, b, dims):
    return lax.dot_general(a, b, dims, preferred_element_type=F32)


def _sigmoid(x):
    return 1.0 / (1.0 + jnp.exp(-x))


def _sigmoid_tanh(x):
    return 0.5 * jnp.tanh(0.5 * x) + 0.5


def _gelu(x):
    c = 0.7978845608028654
    t = jnp.tanh(c * (x + 0.044715 * (x * x * x)))
    return 0.5 * x * (1.0 + t)


def _gelu_and_grad(x):
    c = 0.7978845608028654
    x2 = x * x
    t = jnp.tanh(c * (x + 0.044715 * (x2 * x)))
    half = 0.5 * (1.0 + t)
    return x * half, half + 0.5 * x * (1.0 - t * t) * c * (1.0 + 3.0 * 0.044715 * x2)


def _softplus(x):
    return jnp.maximum(x, 0.0) + jnp.log1p(jnp.exp(-jnp.abs(x)))


def _time_scratch(tm, d):
    return [pltpu.VMEM((2, tm, d), F32), pltpu.VMEM((d // LANES, tm, LANES), F32), pltpu.SemaphoreType.DMA((2,))]


def _fetch_time_tile(i, nt, src_ref, head, tbuf, sems, tm):
    n_head = head.shape[0]

    def tile(j, slot):
        start = pl.multiple_of(j * tm - n_head, SUBLANES)
        return pltpu.make_async_copy(src_ref.at[pl.ds(start, tm)], tbuf.at[slot], sems.at[slot])

    first = pltpu.make_async_copy(src_ref.at[pl.ds(0, tm - n_head)], tbuf.at[0, pl.ds(n_head, tm - n_head)], sems.at[0])

    @pl.when(i == 0)
    def _():
        tbuf[0, 0:n_head, :] = head
        first.start()

    @pl.when(i + 1 < nt)
    def _():
        tile(i + 1, (i + 1) % 2).start()

    @pl.when(i == 0)
    def _():
        first.wait()

    @pl.when(i > 0)
    def _():
        tile(i, i % 2).wait()


def _time_to_tile_order(t_ref, slabs, out_ref):
    tm, d = out_ref.shape
    groups = tm // SUBLANES
    for k in range(d // LANES):
        slabs[k] = t_ref[:, k * LANES:(k + 1) * LANES]
    for k in range(d // LANES):
        for g in range(groups):
            out_ref[g * SUBLANES:(g + 1) * SUBLANES, k * LANES:(k + 1) * LANES] = slabs[k, pl.ds(g, SUBLANES, stride=groups), :]


def _tile_to_time_order(p_ref, slabs, t_ref):
    tm, d = p_ref.shape
    groups = tm // SUBLANES
    for k in range(d // LANES):
        for g in range(groups):
            slabs[k, pl.ds(g, SUBLANES, stride=groups), :] = p_ref[g * SUBLANES:(g + 1) * SUBLANES, k * LANES:(k + 1) * LANES]
    for k in range(d // LANES):
        t_ref[:, k * LANES:(k + 1) * LANES] = slabs[k]


def _rows(ref, sl):
    return [ref[k:k + 1, sl] for k in range(ref.shape[0])]


def _shift_down(x, before, s):
    if s == 0:
        return x
    n = x.shape[0]
    row = lax.broadcasted_iota(jnp.int32, (SUBLANES, x.shape[1]), 0)
    heads = []
    for g in range(s):
        v = x[n - (s - g) * SUBLANES:n - (s - g - 1) * SUBLANES]
        heads.append(pltpu.roll(jnp.where(row == SUBLANES - 1, before[s - g - 1], v), 1, axis=0))
    return jnp.concatenate(heads + [x[0:n - s * SUBLANES]], axis=0)


def _shift_up(x, after, s):
    if s == 0:
        return x
    row = lax.broadcasted_iota(jnp.int32, (SUBLANES, x.shape[1]), 0)
    tails = []
    for m in range(s):
        v = x[m * SUBLANES:(m + 1) * SUBLANES]
        tails.append(pltpu.roll(jnp.where(row == 0, after[m], v), SUBLANES - 1, axis=0))
    return jnp.concatenate([x[s * SUBLANES:]] + tails, axis=0)


def _weighted(w, windows):
    y = w[0] * windows[0]
    for k in range(1, len(w)):
        y = y + w[k] * windows[k]
    return y


def _windows(x, before, k_width):
    return [_shift_down(x, before, k_width - 1 - k) for k in range(k_width)]


def _conv_t(dy, after, w):
    k_width = len(w)
    return _weighted(w, [_shift_up(dy, after, k_width - 1 - k) for k in range(k_width)])


def _blocks(ref, sl, count, newest_first):
    n = ref.shape[0] // SUBLANES
    order = range(n - 1, n - 1 - count, -1) if newest_first else range(count)
    return [ref[b * SUBLANES:(b + 1) * SUBLANES, sl] for b in order]


def _halo_specs(tm, cols, count, tile_of):
    def spec(k):
        return pl.BlockSpec((SUBLANES, cols), lambda i: (jnp.maximum(tile_of(i) * (tm // SUBLANES) - k, 0), 0))
    return [spec(k) for k in range(1, count + 1)]


def _scan_tile(coef, val, out, carry, groups, reverse):
    cols = coef.shape[1]
    row = lax.broadcasted_iota(jnp.int32, (SUBLANES, cols), 0)

    def blk(i):
        g = groups - 1 - i if reverse else i
        return pl.ds(pl.multiple_of(g * SUBLANES, SUBLANES), SUBLANES)

    def local(i, pl_):
        p_prev, l_prev = pl_
        a = coef[blk(i), :]
        p = a * p_prev
        l = a * l_prev + val[blk(i), :]
        coef[blk(i), :] = p
        val[blk(i), :] = l
        return p, l

    pf, lf = lax.fori_loop(0, groups, local, (jnp.ones((SUBLANES, cols), F32), jnp.zeros((SUBLANES, cols), F32)))
    for s in (1, 2, 4):
        keep, sh = (row < SUBLANES - s, SUBLANES - s) if reverse else (row >= s, s)
        p_s = jnp.where(keep, pltpu.roll(pf, sh, axis=0), 1.0)
        l_s = jnp.where(keep, pltpu.roll(lf, sh, axis=0), 0.0)
        lf = pf * l_s + lf
        pf = pf * p_s
    end = lf + pf * carry
    if reverse:
        init = jnp.where(row == SUBLANES - 1, carry, pltpu.roll(end, SUBLANES - 1, axis=0))
        leaving = jnp.broadcast_to(end[0:1, :], (SUBLANES, cols))
    else:
        init = jnp.where(row == 0, carry, pltpu.roll(end, 1, axis=0))
        leaving = jnp.broadcast_to(end[SUBLANES - 1:SUBLANES, :], (SUBLANES, cols))

    def fix(i, _):
        out[blk(i), :] = val[blk(i), :] + coef[blk(i), :] * init
        return 0

    lax.fori_loop(0, groups, fix, 0)
    return leaving


def _resident(shape):
    return pl.BlockSpec(shape, lambda *_: (0,) * len(shape), pipeline_mode=pl.Buffered(1))


def _full(shape):
    return pl.BlockSpec(shape, lambda *_: (0,) * len(shape))


def _rmsnorm_to(h_ref, g_ref, n_ref):
    x = h_ref[...]
    ms = jnp.mean(x * x, axis=-1, keepdims=True)
    n_ref[...] = (x * lax.rsqrt(ms + RMS_EPS) * g_ref[...]).astype(n_ref.dtype)


_ANY = pl.BlockSpec(memory_space=pl.ANY)
_MESH = pl.DeviceIdType.MESH


def _dma_sems(n):
    return pltpu.SemaphoreType.DMA((n,))


def _when_each(*phases):
    for cond, fn in phases:
        if cond is not None:
            pl.when(cond)(fn)


class _NoRider:
    inputs = in_specs = out_shape = out_specs = scratch_shapes = ()

    def __call__(self, first, middle, last, ins, outs, scratch):
        pass


_NO_RIDER = _NoRider()


def _hand_on_step(nt):
    return (3 * nt) // 4


def _split_refs(rider, n_out, n_scratch, rest):
    a = len(rider.inputs)
    b = a + n_out
    c = b + len(rider.out_shape)
    e = c + n_scratch
    return rest[:a], rest[a:b], rest[b:c], rest[c:e], rest[e:]


class _GatherRider:
    def __init__(self, blk):
        self.inputs = (blk,)
        self.in_specs = (_ANY,)
        self.out_shape = (jax.ShapeDtypeStruct((N_DEV,) + blk.shape, blk.dtype),)
        self.out_specs = (_ANY,)
        self.scratch_shapes = (_dma_sems(7), _dma_sems(7), pltpu.SemaphoreType.DMA(()))

    def __call__(self, first, middle, last, ins, outs, scratch):
        (x_ref,), (out_ref,) = ins, outs
        send_sems, recv_sems, local_sem = scratch
        x, y, c = lax.axis_index("x"), lax.axis_index("y"), lax.axis_index("c")
        me, sibling = (x, y, c), (x, y, 1 - c)
        chips = [(1 - x, y), (x, 1 - y), (1 - x, 1 - y)]

        def slot(px, py, pc):
            return out_ref.at[4 * px + 2 * py + pc]

        def copy(k, block, to, src=None):
            return pltpu.make_async_remote_copy(
                src_ref=slot(*block) if src is None else src, dst_ref=slot(*block),
                send_sem=send_sems.at[k], recv_sem=recv_sems.at[k], device_id=to, device_id_type=_MESH)

        mine = pltpu.make_async_copy(x_ref, slot(*me), local_sem)
        own = [copy(0, me, sibling, src=x_ref)]
        own += [copy(1 + j, me, (*chip, c), src=x_ref) for j, chip in enumerate(chips)]
        passed = [copy(4 + j, (*chip, c), sibling) for j, chip in enumerate(chips)]

        def at_first():
            mine.start()
            for cp in own:
                cp.start()

        def at_middle():
            for j, chip in enumerate(chips):
                copy(1 + j, (*chip, c), me).wait_recv()
                passed[j].start()

        def at_last():
            copy(0, sibling, me).wait_recv()
            for j, chip in enumerate(chips):
                copy(4 + j, (*chip, 1 - c), me).wait_recv()
            for cp in own + passed:
                cp.wait_send()
            mine.wait()

        _when_each((first, at_first), (middle, at_middle), (last, at_last))


class _ScatterRider:
    def __init__(self, grad, rows=None):
        r = grad.shape[0] // N_DEV
        chunks = grad.reshape(N_DEV, r, grad.shape[1])
        self.rows = rows if rows is not None else (0, r)
        self.inputs = (chunks,)
        self.in_specs = (_ANY,)
        self.out_shape = (jax.ShapeDtypeStruct((N_DEV, self.rows[1], grad.shape[1]), chunks.dtype),)
        self.out_specs = (_ANY,)
        self.scratch_shapes = (_dma_sems(N_DEV - 1), _dma_sems(N_DEV - 1), pltpu.SemaphoreType.DMA(()))

    def __call__(self, first, middle, last, ins, outs, scratch):
        (g_ref,), (r_ref,) = ins, outs
        send_sems, recv_sems, local_sem = scratch
        x, y, c = lax.axis_index("x"), lax.axis_index("y"), lax.axis_index("c")
        me = 4 * x + 2 * y + c
        part = pl.ds(*self.rows)
        mine = pltpu.make_async_copy(g_ref.at[me, part], r_ref.at[me], local_sem)
        copies = []
        for k in range(1, N_DEV):
            px, py, pc = (1 - x if k & 4 else x), (1 - y if k & 2 else y), (1 - c if k & 1 else c)
            copies.append(pltpu.make_async_remote_copy(
                src_ref=g_ref.at[4 * px + 2 * py + pc, part], dst_ref=r_ref.at[me],
                send_sem=send_sems.at[k - 1], recv_sem=recv_sems.at[k - 1],
                device_id=(px, py, pc), device_id_type=_MESH))

        def at_first():
            mine.start()
            for cp in copies:
                cp.start()

        def at_last():
            for cp in copies:
                cp.wait()
            mine.wait()

        _when_each((first, at_first), (last, at_last))


def _matmul_residual(q, w, h, name):
    t, k = q.shape
    d = w.shape[1]
    tm = _row_tile(t, ROW_TILE_MATMUL)

    def body(q_ref, w_ref, h_ref, o_ref):
        o_ref[...] = h_ref[...] + _dot(q_ref[...], w_ref[...], _NN)

    return pl.pallas_call(
        body, name=name, grid=(t // tm,),
        in_specs=[pl.BlockSpec((tm, k), lambda i: (i, 0)),
                  _resident((k, d)),
                  pl.BlockSpec((tm, d), lambda i: (i, 0))],
        out_specs=pl.BlockSpec((tm, d), lambda i: (i, 0)),
        out_shape=jax.ShapeDtypeStruct((t, d), F32),
        compiler_params=_params("parallel"),
    )(q, w, h)


def _dgrad_in_norm(dp, wt, h, g, dh_next, name, rider=_NO_RIDER):
    t, n = dp.shape
    d = wt.shape[1]
    tm = _row_tile(t, ROW_TILE_MATMUL)
    nt = t // tm

    def body(dp_ref, wt_ref, h_ref, g_ref, dhn_ref, *rest):
        r_in, (dh_ref, dg_ref), r_out, _, r_scratch = _split_refs(rider, 2, 0, rest)
        i = pl.program_id(0)
        rider(i == 0, i == nt // 2, None, r_in, r_out, r_scratch)
        dn = _dot(dp_ref[...], wt_ref[...], _NN)
        x = h_ref[...]
        ms = jnp.mean(x * x, axis=-1, keepdims=True)
        r = lax.rsqrt(ms + RMS_EPS)
        xhat = x * r
        dng = dn * g_ref[...]
        c = jnp.mean(dng * xhat, axis=-1, keepdims=True)
        dh_ref[...] = dhn_ref[...] + r * (dng - xhat * c)
        part = jnp.sum(dn * xhat, axis=0, keepdims=True)

        @pl.when(i == 0)
        def _():
            dg_ref[...] = part

        @pl.when(i > 0)
        def _():
            dg_ref[...] += part

        rider(None, None, i == nt - 1, r_in, r_out, r_scratch)

    return pl.pallas_call(
        body, name=name, grid=(nt,),
        in_specs=[pl.BlockSpec((tm, n), lambda i: (i, 0)),
                  _resident((n, d)),
                  pl.BlockSpec((tm, d), lambda i: (i, 0)),
                  pl.BlockSpec((1, d), lambda i: (0, 0)),
                  pl.BlockSpec((tm, d), lambda i: (i, 0))] + list(rider.in_specs),
        out_specs=[pl.BlockSpec((tm, d), lambda i: (i, 0)),
                   pl.BlockSpec((1, d), lambda i: (0, 0))] + list(rider.out_specs),
        out_shape=[jax.ShapeDtypeStruct((t, d), F32), jax.ShapeDtypeStruct((1, d), F32)] + list(rider.out_shape),
        scratch_shapes=list(rider.scratch_shapes),
        compiler_params=_params("arbitrary"),
    )(dp, wt, h, g, dh_next, *rider.inputs)


def _dgrad_in_to_input(dp, wt, h, g, dh_next, n_head, tm, name, rider=_NO_RIDER):
    t, n = dp.shape
    d = wt.shape[1]
    nt = t // tm

    def body(dp_ref, wt_ref, h_ref, g_ref, dhn_ref, *rest):
        r_in, (dg_ref, head_ref, rest_ref), r_out, (pbuf, tout, slabs, sems), r_scratch = _split_refs(rider, 3, 4, rest)
        i = pl.program_id(0)
        rider(i == 0, i == nt // 2, None, r_in, r_out, r_scratch)
        dn = _dot(dp_ref[...], wt_ref[...], _NN)
        x = h_ref[...]
        ms = jnp.mean(x * x, axis=-1, keepdims=True)
        r = lax.rsqrt(ms + RMS_EPS)
        xhat = x * r
        dng = dn * g_ref[...]
        c = jnp.mean(dng * xhat, axis=-1, keepdims=True)
        pbuf[...] = dhn_ref[...] + r * (dng - xhat * c)
        part = jnp.sum(dn * xhat, axis=0, keepdims=True)

        @pl.when(i == 0)
        def _():
            dg_ref[...] = part

        @pl.when(i > 0)
        def _():
            dg_ref[...] += part

        def store(j):
            if isinstance(j, int) and j == 0:
                return pltpu.make_async_copy(tout.at[0, pl.ds(n_head, tm - n_head)],
                                             rest_ref.at[pl.ds(0, tm - n_head)], sems.at[0])
            start = j * tm - n_head
            start = start if isinstance(start, int) else pl.multiple_of(start, SUBLANES)
            return pltpu.make_async_copy(tout.at[j % 2], rest_ref.at[pl.ds(start, tm)], sems.at[j % 2])

        @pl.when(i == 2)
        def _():
            store(0).wait()

        @pl.when(i > 2)
        def _():
            store(i - 2).wait()

        _tile_to_time_order(pbuf, slabs, tout.at[i % 2])

        @pl.when(i == 0)
        def _():
            head_ref[...] = tout[0, 0:n_head, :]
            store(0).start()

        @pl.when(i > 0)
        def _():
            store(i).start()

        @pl.when(i == nt - 1)
        def _():
            for j in (nt - 2, nt - 1):
                if j >= 0:
                    store(j).wait()

        rider(None, None, i == nt - 1, r_in, r_out, r_scratch)

    return pl.pallas_call(
        body, name=name, grid=(nt,),
        in_specs=[pl.BlockSpec((tm, n), lambda i: (i, 0)),
                  _resident((n, d)),
                  pl.BlockSpec((tm, d), lambda i: (i, 0)),
                  pl.BlockSpec((1, d), lambda i: (0, 0)),
                  pl.BlockSpec((tm, d), lambda i: (i, 0))] + list(rider.in_specs),
        out_specs=[pl.BlockSpec((1, d), lambda i: (0, 0)), _full((n_head, d)), _ANY] + list(rider.out_specs),
        out_shape=[jax.ShapeDtypeStruct((1, d), F32), jax.ShapeDtypeStruct((n_head, d), F32),
                   jax.ShapeDtypeStruct((t - n_head, d), F32)] + list(rider.out_shape),
        scratch_shapes=[pltpu.VMEM((tm, d), F32)] + _time_scratch(tm, d) + list(rider.scratch_shapes),
        compiler_params=_params("arbitrary"),
    )(dp, wt, h, g, dh_next, *rider.inputs)


def _wgrad(a, b, name, rider=_NO_RIDER):
    t, m = a.shape
    d = b.shape[1]
    tmm, tk = _col_tile(m, 1024), _row_tile(t, ROW_TILE_WGRAD)
    if tmm < 512:
        tmm, tk = _col_tile(m, 1536), _row_tile(t, ROW_TILE_MATMUL)
    nm, nk = m // tmm, t // tk

    def body(a_ref, b_ref, *rest):
        r_in, (o_ref,), r_out, _, r_scratch = _split_refs(rider, 1, 0, rest)
        i, k = pl.program_id(0), pl.program_id(1)
        rider(jnp.logical_and(i == 0, k == 0), None, None, r_in, r_out, r_scratch)

        @pl.when(k == 0)
        def _():
            o_ref[...] = jnp.zeros_like(o_ref)

        o_ref[...] += _dot(a_ref[...], b_ref[...], _TN)
        rider(None, None, jnp.logical_and(i == nm - 1, k == nk - 1), r_in, r_out, r_scratch)

    outs = pl.pallas_call(
        body, name=name, grid=(nm, nk),
        in_specs=[pl.BlockSpec((tk, tmm), lambda i, k: (k, i)),
                  pl.BlockSpec((tk, d), lambda i, k: (k, 0))] + list(rider.in_specs),
        out_specs=[pl.BlockSpec((tmm, d), lambda i, k: (i, 0))] + list(rider.out_specs),
        out_shape=[jax.ShapeDtypeStruct((m, d), F32)] + list(rider.out_shape),
        scratch_shapes=list(rider.scratch_shapes),
        compiler_params=_params("arbitrary", "arbitrary"),
    )(a, b, *rider.inputs)
    return outs if rider.out_shape else outs[0]


def _ffn_up_fused(h, g, w, conv_w, tm, name, rider=_NO_RIDER):
    t, d = h.shape
    f2 = w.shape[1]
    f = f2 // 2
    cw = min(STRIP, f)
    kw = conv_w.shape[0]
    nh = kw - 1
    nt = t // tm

    def body(h_ref, g_ref, w_ref, cw_ref, *rest):
        r_in, (n_ref, a_ref, z_ref), r_out, (carry,), r_scratch = _split_refs(rider, 3, 1, rest)
        i = pl.program_id(0)
        rider(i == 0, i == _hand_on_step(nt), None, r_in, r_out, r_scratch)

        @pl.when(i == 0)
        def _():
            carry[...] = jnp.zeros_like(carry)

        _rmsnorm_to(h_ref, g_ref, n_ref)
        for c in range(0, f, cw):
            conv = []
            for sl in (slice(c, c + cw), slice(f + c, f + c + cw)):
                a = _dot(n_ref[...], w_ref[:, sl], _NN)
                a_ref[:, sl] = a
                conv.append(_weighted(_rows(cw_ref, sl), _windows(a, _blocks(carry, sl, nh, True), kw)))
                carry[:, sl] = a[tm - nh * SUBLANES:tm]
            gg, vv = conv
            z_ref[:, c:c + cw] = (gg * _sigmoid(gg) * vv).astype(z_ref.dtype)
        rider(None, None, i == nt - 1, r_in, r_out, r_scratch)

    return pl.pallas_call(
        body, name=name, grid=(nt,),
        in_specs=[pl.BlockSpec((tm, d), lambda i: (i, 0)), pl.BlockSpec((1, d), lambda i: (0, 0)),
                  _resident((d, f2)), _full(conv_w.shape)] + list(rider.in_specs),
        out_specs=[pl.BlockSpec((tm, d), lambda i: (i, 0)), pl.BlockSpec((tm, f2), lambda i: (i, 0)),
                   pl.BlockSpec((tm, f), lambda i: (i, 0))] + list(rider.out_specs),
        out_shape=[jax.ShapeDtypeStruct((t, d), MXU_DTYPE), jax.ShapeDtypeStruct((t, f2), F32),
                   jax.ShapeDtypeStruct((t, f), MXU_DTYPE)] + list(rider.out_shape),
        scratch_shapes=[pltpu.VMEM((nh * SUBLANES, f2), F32)] + list(rider.scratch_shapes),
        compiler_params=_params("arbitrary"),
    )(h, g, w, conv_w, *rider.inputs)


def _ffn_down_bwd_fused(dh, wdt, a, conv_w, tm, name):
    t, d = dh.shape
    f = wdt.shape[1]
    f2 = 2 * f
    nt = t // tm
    cw = min(STRIP, f)
    kw = conv_w.shape[0]
    nh = kw - 1
    rev = lambda i: nt - 1 - i

    def body(dh_ref, wdt_ref, a_ref, *rest):
        halo_refs = rest[:nh]
        cw_ref, da_ref, dhb_ref, dw_ref, carry = rest[nh:]
        i = pl.program_id(0)
        newest, oldest = i == 0, i == nt - 1

        @pl.when(newest)
        def _():
            carry[...] = jnp.zeros_like(carry)
            dw_ref[...] = jnp.zeros_like(dw_ref)

        dhb_ref[...] = dh_ref[...].astype(dhb_ref.dtype)
        for c in range(0, f, cw):
            gsl, vsl = slice(c, c + cw), slice(f + c, f + c + cw)
            dz = _dot(dhb_ref[...], wdt_ref[:, gsl], _NN)
            win, conv = {}, {}
            for sl in (gsl, vsl):
                before = [jnp.where(oldest, 0.0, r[:, sl]) for r in halo_refs]
                win[sl.start] = _windows(a_ref[:, sl], before, kw)
                conv[sl.start] = _weighted(_rows(cw_ref, sl), win[sl.start])
            gg, vv = conv[gsl.start], conv[vsl.start]
            s = _sigmoid(gg)
            grads = {gsl.start: dz * vv * s * (1.0 + gg * (1.0 - s)), vsl.start: dz * gg * s}
            for sl in (gsl, vsl):
                dconv = grads[sl.start]
                da_ref[:, sl] = _conv_t(dconv, _blocks(carry, sl, nh, False), _rows(cw_ref, sl)).astype(da_ref.dtype)
                for k in range(kw):
                    dw_ref[k:k + 1, sl] += jnp.sum(dconv * win[sl.start][k], axis=0, keepdims=True)
                carry[:, sl] = dconv[0:nh * SUBLANES]

    return pl.pallas_call(
        body, name=name, grid=(nt,),
        in_specs=[pl.BlockSpec((tm, d), lambda i: (rev(i), 0)), _resident((d, f)),
                  pl.BlockSpec((tm, f2), lambda i: (rev(i), 0))] + _halo_specs(tm, f2, nh, rev) + [_full(conv_w.shape)],
        out_specs=[pl.BlockSpec((tm, f2), lambda i: (rev(i), 0)), pl.BlockSpec((tm, d), lambda i: (rev(i), 0)),
                   _full(conv_w.shape)],
        out_shape=[jax.ShapeDtypeStruct((t, f2), MXU_DTYPE), jax.ShapeDtypeStruct((t, d), MXU_DTYPE),
                   jax.ShapeDtypeStruct(conv_w.shape, F32)],
        scratch_shapes=[pltpu.VMEM((nh * SUBLANES, f2), F32)],
        compiler_params=_params("arbitrary"),
    )(dh, wdt, a, *([a] * nh), conv_w)


def _sc_in_fused(x, meta, g, w, conv_w, tm, name, rider=_NO_RIDER):
    d = x.shape[1]
    t = x.shape[0] + meta.shape[0]
    cw = min(STRIP, d)
    kw = conv_w.shape[0]
    nh = kw - 1
    nt = t // tm

    def body(x_ref, meta_ref, g_ref, w_ref, cw_ref, *rest):
        r_in, (h_ref, n_ref, p_ref, q_ref), r_out, (carry, tbuf, slabs, sems), r_scratch = _split_refs(rider, 4, 4, rest)
        i = pl.program_id(0)
        rider(i == 0, i == _hand_on_step(nt), None, r_in, r_out, r_scratch)

        @pl.when(i == 0)
        def _():
            carry[...] = jnp.zeros_like(carry)

        _fetch_time_tile(i, nt, x_ref, meta_ref[...], tbuf, sems, tm)
        _time_to_tile_order(tbuf.at[i % 2], slabs, h_ref)
        _rmsnorm_to(h_ref, g_ref, n_ref)
        for c in range(0, d, cw):
            sl = slice(c, c + cw)
            parts = []
            for base in (0, d, 2 * d):
                psl = slice(base + c, base + c + cw)
                parts.append(_dot(n_ref[...], w_ref[:, psl], _NN))
                p_ref[:, psl] = parts[-1]
            bg, cg, v = parts
            cv = cg * v
            u = _weighted(_rows(cw_ref, sl), _windows(cv, _blocks(carry, sl, nh, True), kw))
            carry[:, sl] = cv[tm - nh * SUBLANES:tm]
            q_ref[:, sl] = (bg * u).astype(q_ref.dtype)
        rider(None, None, i == nt - 1, r_in, r_out, r_scratch)

    row_tile = lambda cols: pl.BlockSpec((tm, cols), lambda i: (i, 0))
    return pl.pallas_call(
        body, name=name, grid=(nt,),
        in_specs=[_ANY, _full(meta.shape), pl.BlockSpec((1, d), lambda i: (0, 0)),
                  _resident((d, 3 * d)), _full(conv_w.shape)] + list(rider.in_specs),
        out_specs=[row_tile(d), row_tile(d), row_tile(3 * d), row_tile(d)] + list(rider.out_specs),
        out_shape=[jax.ShapeDtypeStruct((t, d), F32), jax.ShapeDtypeStruct((t, d), MXU_DTYPE),
                   jax.ShapeDtypeStruct((t, 3 * d), F32), jax.ShapeDtypeStruct((t, d), MXU_DTYPE)]
                  + list(rider.out_shape),
        scratch_shapes=[pltpu.VMEM((nh * SUBLANES, d), F32)] + _time_scratch(tm, d) + list(rider.scratch_shapes),
        compiler_params=_params("arbitrary"),
    )(x, meta, g, w, conv_w, *rider.inputs)


def _sc_out_bwd_fused(dh, wot, p, conv_w, tm, name, rider=_NO_RIDER):
    t, d = dh.shape
    nt = t // tm
    cw = min(STRIP, d)
    kw = conv_w.shape[0]
    nh = kw - 1
    rev = lambda i: nt - 1 - i

    def body(dh_ref, wot_ref, p_ref, *rest):
        halo_refs, cw_ref = rest[:nh], rest[nh]
        r_in, (dp_ref, dhb_ref, dw_ref), r_out, (carry,), r_scratch = _split_refs(rider, 3, 1, rest[nh + 1:])
        i = pl.program_id(0)
        newest, oldest = i == 0, i == nt - 1
        rider(newest, i == nt // 2, None, r_in, r_out, r_scratch)

        @pl.when(newest)
        def _():
            carry[...] = jnp.zeros_like(carry)
            dw_ref[...] = jnp.zeros_like(dw_ref)

        dhb_ref[...] = dh_ref[...].astype(dhb_ref.dtype)
        for c in range(0, d, cw):
            sl, csl, vsl = slice(c, c + cw), slice(d + c, d + c + cw), slice(2 * d + c, 2 * d + c + cw)
            w = _rows(cw_ref, sl)
            dq = _dot(dhb_ref[...], wot_ref[:, sl], _NN)
            bg, cg, v = p_ref[:, sl], p_ref[:, csl], p_ref[:, vsl]
            before = [jnp.where(oldest, 0.0, r[:, csl] * r[:, vsl]) for r in halo_refs]
            win = _windows(cg * v, before, kw)
            du = dq * bg
            dcv = _conv_t(du, _blocks(carry, sl, nh, False), w)
            dp_ref[:, sl] = (dq * _weighted(w, win)).astype(dp_ref.dtype)
            dp_ref[:, csl] = (dcv * v).astype(dp_ref.dtype)
            dp_ref[:, vsl] = (dcv * cg).astype(dp_ref.dtype)
            for k in range(kw):
                dw_ref[k:k + 1, sl] += jnp.sum(du * win[k], axis=0, keepdims=True)
            carry[:, sl] = du[0:nh * SUBLANES]
        rider(None, None, oldest, r_in, r_out, r_scratch)

    tile = lambda cols: pl.BlockSpec((tm, cols), lambda i: (rev(i), 0))
    return pl.pallas_call(
        body, name=name, grid=(nt,),
        in_specs=[tile(d), _resident((d, d)), tile(3 * d)] + _halo_specs(tm, 3 * d, nh, rev) + [_full(conv_w.shape)]
                 + list(rider.in_specs),
        out_specs=[tile(3 * d), tile(d), _full(conv_w.shape)] + list(rider.out_specs),
        out_shape=[jax.ShapeDtypeStruct((t, 3 * d), MXU_DTYPE), jax.ShapeDtypeStruct((t, d), MXU_DTYPE),
                   jax.ShapeDtypeStruct(conv_w.shape, F32)] + list(rider.out_shape),
        scratch_shapes=[pltpu.VMEM((nh * SUBLANES, d), F32)] + list(rider.scratch_shapes),
        compiler_params=_params("arbitrary"),
    )(dh, wot, p, *([p] * nh), conv_w, *rider.inputs)


def _pair_gate_weights(wa, wx):
    nb, bd, _ = wa.shape
    zero = jnp.zeros((nb // 2, bd, bd), wa.dtype)

    def pair(w):
        w = w.reshape(nb // 2, 2, bd, bd)
        top = jnp.concatenate([w[:, 0], zero], axis=2)
        bottom = jnp.concatenate([zero, w[:, 1]], axis=2)
        return jnp.concatenate([top, bottom], axis=1)

    both = jnp.concatenate([pair(wa), pair(wx)], axis=2)
    return both, jnp.swapaxes(both, 1, 2)


def _unpair_gate_grads(dw, bd):
    def blocks(cols):
        d0 = dw[:, :bd, cols:cols + bd]
        d1 = dw[:, bd:, cols + bd:cols + 2 * bd]
        return jnp.stack([d0, d1], axis=1).reshape(-1, bd, bd)
    return blocks(0), blocks(2 * bd)


def _rg_gates(u, k, wg_ref, ba, bx, lam):
    ub = u.astype(MXU_DTYPE)
    z = _dot(ub, wg_ref[k], _NN)
    half = z.shape[1] // 2
    r = _sigmoid_tanh(z[:, :half] + ba)
    ig = _sigmoid_tanh(z[:, half:] + bx)
    sp = _softplus(-lam)
    la = -RG_C * r * sp
    a = jnp.exp(la)
    th = jnp.tanh(la)
    mult = jnp.sqrt(-2.0 * th / (1.0 - th))
    return r, ig, a, mult


def _rg_in_fused(h, g, w, conv_w, conv_b, wg, ba, bx, lam, tm, name, rider=_NO_RIDER):
    t, d = h.shape
    r2 = w.shape[1]
    rr = r2 // 2
    nb, bd = wg.shape[0], wg.shape[1]
    kw = conv_w.shape[0]
    nh = kw - 1
    groups = tm // SUBLANES
    nt = t // tm

    def body(h_ref, g_ref, w_ref, cw_ref, cb_ref, wg_ref, ba_ref, bx_ref, lam_ref, *rest):
        r_in, (n_ref, p_ref, y_ref, hs_ref), r_out, (a_scr, b_scr, carry_rb, carry_h), r_scratch = _split_refs(
            rider, 4, 4, rest)
        i = pl.program_id(0)
        rider(i == 0, i == _hand_on_step(nt), None, r_in, r_out, r_scratch)

        @pl.when(i == 0)
        def _():
            carry_rb[...] = jnp.zeros_like(carry_rb)
            carry_h[...] = jnp.zeros_like(carry_h)

        _rmsnorm_to(h_ref, g_ref, n_ref)
        for k in range(nb):
            sl = slice(k * bd, (k + 1) * bd)
            rsl = slice(rr + k * bd, rr + (k + 1) * bd)
            p_ref[:, sl] = _dot(n_ref[...], w_ref[:, sl], _NN)
            rb = _dot(n_ref[...], w_ref[:, rsl], _NN)
            p_ref[:, rsl] = rb
            u = _weighted(_rows(cw_ref, sl), _windows(rb, _blocks(carry_rb, sl, nh, True), kw)) + cb_ref[:, sl]
            carry_rb[:, sl] = rb[tm - nh * SUBLANES:tm]
            _, ig, a, mult = _rg_gates(u, k, wg_ref, ba_ref[:, sl], bx_ref[:, sl], lam_ref[:, sl])
            a_scr[:, sl] = a
            b_scr[:, sl] = mult * (ig * u)

        carry_h[...] = _scan_tile(a_scr, b_scr, hs_ref, carry_h[...], groups, reverse=False)

        for k in range(nb):
            sl = slice(k * bd, (k + 1) * bd)
            y_ref[:, sl] = (hs_ref[:, sl] * _gelu(p_ref[:, sl])).astype(y_ref.dtype)
        rider(None, None, i == nt - 1, r_in, r_out, r_scratch)

    vm = lambda rows: pltpu.VMEM((rows, rr), F32)
    return pl.pallas_call(
        body, name=name, grid=(nt,),
        in_specs=[pl.BlockSpec((tm, d), lambda i: (i, 0)), pl.BlockSpec((1, d), lambda i: (0, 0)),
                  _resident((d, r2)), _full(conv_w.shape), _full(conv_b.shape), _full(wg.shape), _full(ba.shape),
                  _full(bx.shape), _full(lam.shape)] + list(rider.in_specs),
        out_specs=[pl.BlockSpec((tm, d), lambda i: (i, 0)), pl.BlockSpec((tm, r2), lambda i: (i, 0)),
                   pl.BlockSpec((tm, rr), lambda i: (i, 0)), pl.BlockSpec((tm, rr), lambda i: (i, 0))]
                  + list(rider.out_specs),
        out_shape=[jax.ShapeDtypeStruct((t, d), MXU_DTYPE), jax.ShapeDtypeStruct((t, r2), F32),
                   jax.ShapeDtypeStruct((t, rr), MXU_DTYPE), jax.ShapeDtypeStruct((t, rr), F32)]
                  + list(rider.out_shape),
        scratch_shapes=[vm(tm), vm(tm), vm(nh * SUBLANES), vm(SUBLANES)] + list(rider.scratch_shapes),
        compiler_params=_params("arbitrary"),
    )(h, g, w, conv_w, conv_b, wg, ba, bx, lam, *rider.inputs)


def _rg_out_bwd_fused(dh, wot, p, hs, conv_w, conv_b, wg, wgt, ba, bx, lam, tm, name, rider=_NO_RIDER):
    t, d = dh.shape
    r2 = p.shape[1]
    rr = r2 // 2
    nb, bd = wg.shape[0], wg.shape[1]
    nt = t // tm
    kw = conv_w.shape[0]
    nh = kw - 1
    groups = tm // SUBLANES
    rev = lambda i: nt - 1 - i

    def body(dh_ref, wot_ref, p_ref, hs_ref, hsh_ref, *rest):
        halo_refs = rest[:nh]
        cw_ref, cb_ref, wg_ref, wgt_ref, ba_ref, bx_ref, lam_ref = rest[nh:nh + 7]
        (r_in, (dp_ref, dhb_ref, dcw_ref, dcb_ref, dwg_ref, dba_ref, dbx_ref, dlam_ref), r_out,
         (a_scr, as_scr, u_scr, r_scr, ig_scr, mult_scr, g_scr, carry_g, carry_du), r_scratch) = _split_refs(
            rider, 8, 9, rest[nh + 7:])
        i = pl.program_id(0)
        newest, oldest = i == 0, i == nt - 1
        rider(newest, i == nt // 2, None, r_in, r_out, r_scratch)

        @pl.when(newest)
        def _():
            carry_g[...] = jnp.zeros_like(carry_g)
            carry_du[...] = jnp.zeros_like(carry_du)
            for ref in (dcw_ref, dcb_ref, dwg_ref, dba_ref, dbx_ref, dlam_ref):
                ref[...] = jnp.zeros_like(ref)

        dhb_ref[...] = dh_ref[...].astype(dhb_ref.dtype)
        ones = jnp.ones((SUBLANES, bd), F32)

        def conv_windows(sl, rsl):
            before = [jnp.where(oldest, 0.0, r[:, rsl]) for r in halo_refs]
            return _windows(p_ref[:, rsl], before, kw)

        for k in range(nb):
            sl = slice(k * bd, (k + 1) * bd)
            rsl = slice(rr + k * bd, rr + (k + 1) * bd)
            dy = _dot(dhb_ref[...], wot_ref[:, sl], _NN)
            u = _weighted(_rows(cw_ref, sl), conv_windows(sl, rsl)) + cb_ref[:, sl]
            r, ig, a, mult = _rg_gates(u, k, wg_ref, ba_ref[:, sl], bx_ref[:, sl], lam_ref[:, sl])
            gate, dgate = _gelu_and_grad(p_ref[:, sl])
            g_scr[:, sl] = dy * gate
            dp_ref[:, sl] = (dy * hs_ref[:, sl] * dgate).astype(dp_ref.dtype)
            a_scr[:, sl] = a
            as_scr[:, sl] = _shift_up(a, [ones], 1)
            u_scr[:, sl] = u
            r_scr[:, sl] = r
            ig_scr[:, sl] = ig
            mult_scr[:, sl] = mult

        g_first = _scan_tile(as_scr, g_scr, g_scr, carry_g[...], groups, reverse=True)
        carry_g[...] = jnp.broadcast_to(a_scr[0:1, :], (SUBLANES, rr)) * g_first

        for k in range(nb):
            sl = slice(k * bd, (k + 1) * bd)
            rsl = slice(rr + k * bd, rr + (k + 1) * bd)
            cw = _rows(cw_ref, sl)
            lam_k = lam_ref[:, sl]
            sp = _softplus(-lam_k)
            g = g_scr[:, sl]
            a, u, r, ig, mult = a_scr[:, sl], u_scr[:, sl], r_scr[:, sl], ig_scr[:, sl], mult_scr[:, sl]
            da = g * _shift_down(hs_ref[:, sl], [jnp.where(oldest, 0.0, hsh_ref[:, sl])], 1)
            dmult = g * (ig * u)
            d_iu = g * mult
            dla = da * a - dmult * (a * a) / mult
            dr = dla * (-RG_C * sp)
            dsp = jnp.sum(dla * (-RG_C * r), axis=0, keepdims=True)
            dlam_ref[:, sl] += dsp * (-_sigmoid(-lam_k))
            dza = dr * r * (1.0 - r)
            dzx = (d_iu * u) * ig * (1.0 - ig)
            dba_ref[:, sl] += jnp.sum(dza, axis=0, keepdims=True)
            dbx_ref[:, sl] += jnp.sum(dzx, axis=0, keepdims=True)
            ub = u.astype(MXU_DTYPE)
            dz = jnp.concatenate([dza, dzx], axis=1).astype(MXU_DTYPE)
            dwg_ref[k] += _dot(ub, dz, _TN)
            du = d_iu * ig + _dot(dz, wgt_ref[k], _NN)
            dcb_ref[:, sl] += jnp.sum(du, axis=0, keepdims=True)
            win = conv_windows(sl, rsl)
            for kk in range(kw):
                dcw_ref[kk:kk + 1, sl] += jnp.sum(du * win[kk], axis=0, keepdims=True)
            dp_ref[:, rsl] = _conv_t(du, _blocks(carry_du, sl, nh, False), cw).astype(dp_ref.dtype)
            carry_du[:, sl] = du[0:nh * SUBLANES]
        rider(None, None, oldest, r_in, r_out, r_scratch)

    tile = lambda cols: pl.BlockSpec((tm, cols), lambda i: (rev(i), 0))
    vm = lambda rows: pltpu.VMEM((rows, rr), F32)
    grads = [conv_w.shape, conv_b.shape, wg.shape, ba.shape, bx.shape, lam.shape]
    return pl.pallas_call(
        body, name=name, grid=(nt,),
        in_specs=[tile(d), _resident((d, rr)), tile(r2), tile(rr)] + _halo_specs(tm, rr, 1, rev)
                 + _halo_specs(tm, r2, nh, rev)
                 + [_full(conv_w.shape), _full(conv_b.shape), _full(wg.shape), _full(wgt.shape), _full(ba.shape),
                    _full(bx.shape), _full(lam.shape)] + list(rider.in_specs),
        out_specs=[tile(r2), tile(d)] + [_full(s) for s in grads] + list(rider.out_specs),
        out_shape=[jax.ShapeDtypeStruct((t, r2), MXU_DTYPE), jax.ShapeDtypeStruct((t, d), MXU_DTYPE)]
                  + [jax.ShapeDtypeStruct(s, F32) for s in grads] + list(rider.out_shape),
        scratch_shapes=[vm(tm)] * 7 + [vm(SUBLANES), vm(nh * SUBLANES)] + list(rider.scratch_shapes),
        compiler_params=_params("arbitrary"),
    )(dh, wot, p, hs, hs, *([p] * nh), conv_w, conv_b, wg, wgt, ba, bx, lam, *rider.inputs)


def _down_loss_head(z, w, h, g, target, n_meta, tm, name):
    t, d = h.shape
    k = z.shape[1]
    groups = tm // SUBLANES
    nt = t // tm

    def body(z_ref, w_ref, h_ref, g_ref, t_ref, dh_ref, loss_ref, dg_ref, tgt, tbuf, slabs, sems):
        i = pl.program_id(0)
        _fetch_time_tile(i, nt, t_ref, jnp.zeros((n_meta, d), F32), tbuf, sems, tm)
        _time_to_tile_order(tbuf.at[i % 2], slabs, tgt)
        x = h_ref[...] + _dot(z_ref[...], w_ref[...], _NN)
        ms = jnp.mean(x * x, axis=-1, keepdims=True)
        r = lax.rsqrt(ms + RMS_EPS)
        xhat = x * r
        gg = g_ref[...]
        row = lax.broadcasted_iota(jnp.int32, (tm, 1), 0)
        time = i * tm + jnp.right_shift(row, 3) + jnp.bitwise_and(row, SUBLANES - 1) * groups
        err = jnp.where(time >= n_meta, xhat * gg - tgt[...], 0.0)
        dout = err * (1.0 / d)
        dng = dout * gg
        c = jnp.mean(dng * xhat, axis=-1, keepdims=True)
        dh_ref[...] = r * (dng - xhat * c)
        part_dg = jnp.sum(dout * xhat, axis=0, keepdims=True)
        part_loss = jnp.broadcast_to(0.5 * jnp.sum(err * dout, keepdims=True), loss_ref.shape)

        @pl.when(i == 0)
        def _():
            dg_ref[...] = part_dg
            loss_ref[...] = part_loss

        @pl.when(i > 0)
        def _():
            dg_ref[...] += part_dg
            loss_ref[...] += part_loss

    return pl.pallas_call(
        body, name=name, grid=(nt,),
        in_specs=[pl.BlockSpec((tm, k), lambda i: (i, 0)),
                  _resident((k, d)),
                  pl.BlockSpec((tm, d), lambda i: (i, 0)),
                  pl.BlockSpec((1, d), lambda i: (0, 0)),
                  _ANY],
        out_specs=[pl.BlockSpec((tm, d), lambda i: (i, 0)),
                   pl.BlockSpec((SUBLANES, LANES), lambda i: (0, 0)),
                   pl.BlockSpec((1, d), lambda i: (0, 0))],
        out_shape=[jax.ShapeDtypeStruct((t, d), F32), jax.ShapeDtypeStruct((SUBLANES, LANES), F32),
                   jax.ShapeDtypeStruct((1, d), F32)],
        scratch_shapes=[pltpu.VMEM((tm, d), F32)] + _time_scratch(tm, d),
        compiler_params=_params("arbitrary"),
    )(z, w, h, g, target)


def _adamw(w, g, m, v, name):
    rows, cols = w.shape
    tr = rows
    if rows > 512:
        for cand in range(8, 513, 8):
            if rows % cand == 0:
                tr = cand

    def body(w_ref, g_ref, m_ref, v_ref, d_ref, nm_ref, nv_ref):
        g_ = g_ref[...]
        m_ = ADAM_B1 * m_ref[...] + (1.0 - ADAM_B1) * g_
        v_ = ADAM_B2 * v_ref[...] + (1.0 - ADAM_B2) * (g_ * g_)
        m_hat = m_ / (1.0 - ADAM_B1 ** ADAM_STEP)
        v_hat = v_ / (1.0 - ADAM_B2 ** ADAM_STEP)
        d_ref[...] = -ADAM_LR * (m_hat / (jnp.sqrt(v_hat) + ADAM_EPS) + ADAM_WD * w_ref[...])
        nm_ref[...] = m_
        nv_ref[...] = v_

    spec = pl.BlockSpec((tr, cols), lambda i: (i, 0))
    shape = jax.ShapeDtypeStruct((rows, cols), F32)
    return pl.pallas_call(
        body, name=name, grid=(rows // tr,),
        in_specs=[spec] * 4, out_specs=[spec] * 3, out_shape=[shape] * 3,
        compiler_params=_params("parallel"),
    )(w, g, m, v)


def _adamw_nd(w, g, m, v, name):
    shape = w.shape
    two_d = (-1, shape[-1]) if w.ndim > 1 else (1, -1)
    outs = _adamw(w.reshape(two_d), g.reshape(two_d), m.reshape(two_d), v.reshape(two_d), name)
    return tuple(o.reshape(shape) for o in outs)


def _all_gather(blk, name):
    rows, cols = blk.shape

    def body(x_ref, out_ref, send_sems, recv_sems, local_sem):
        x, y, c = lax.axis_index("x"), lax.axis_index("y"), lax.axis_index("c")
        me, sibling = (x, y, c), (x, y, 1 - c)
        chips = [(1 - x, y), (x, 1 - y), (1 - x, 1 - y)]

        def slot(px, py, pc):
            return out_ref.at[4 * px + 2 * py + pc]

        def copy(k, block, to, src=None):
            return pltpu.make_async_remote_copy(
                src_ref=slot(*block) if src is None else src, dst_ref=slot(*block),
                send_sem=send_sems.at[k], recv_sem=recv_sems.at[k], device_id=to, device_id_type=_MESH)

        mine = pltpu.make_async_copy(x_ref, slot(*me), local_sem)
        mine.start()
        first = [copy(0, me, sibling, src=x_ref)]
        first += [copy(1 + j, me, (*chip, c), src=x_ref) for j, chip in enumerate(chips)]
        for cp in first:
            cp.start()
        passed = [copy(4 + j, (*chip, c), sibling) for j, chip in enumerate(chips)]
        for j, chip in enumerate(chips):
            copy(1 + j, (*chip, c), me).wait_recv()
            passed[j].start()
        copy(0, sibling, me).wait_recv()
        for j, chip in enumerate(chips):
            copy(4 + j, (*chip, 1 - c), me).wait_recv()
        for cp in first + passed:
            cp.wait_send()
        mine.wait()

    return pl.pallas_call(
        body, name=name,
        in_specs=[_ANY], out_specs=_ANY,
        out_shape=jax.ShapeDtypeStruct((N_DEV, rows, cols), blk.dtype),
        scratch_shapes=[pltpu.SemaphoreType.DMA((7,)), pltpu.SemaphoreType.DMA((7,)), pltpu.SemaphoreType.DMA(())],
    )(blk)


def _rs_sibling_exchange(buf, name):
    _, rows, cols = buf.shape

    def body(buf_ref, out_ref, send_sems, recv_sems):
        x, y, c = lax.axis_index("x"), lax.axis_index("y"), lax.axis_index("c")
        copies = [pltpu.make_async_remote_copy(
            src_ref=buf_ref.at[2 * k + (1 - c)], dst_ref=out_ref.at[k],
            send_sem=send_sems.at[k], recv_sem=recv_sems.at[k],
            device_id=(x, y, 1 - c), device_id_type=_MESH) for k in range(4)]
        for cp in copies:
            cp.start()
        for cp in copies:
            cp.wait()

    return pl.pallas_call(
        body, name=name,
        in_specs=[_ANY], out_specs=_ANY,
        out_shape=jax.ShapeDtypeStruct((4, rows, cols), buf.dtype),
        scratch_shapes=[pltpu.SemaphoreType.DMA((4,)), pltpu.SemaphoreType.DMA((4,))],
    )(buf)


def _rs_chip_sum(buf, recv, core, name):
    _, rows, cols = buf.shape
    tr = _row_tile(rows, 512) if rows % 16 == 0 else rows

    def body(c_ref, a_ref, b_ref, o_ref):
        o_ref[...] = a_ref[...] + b_ref[...]

    grid_spec = pltpu.PrefetchScalarGridSpec(
        num_scalar_prefetch=1, grid=(4, rows // tr),
        in_specs=[pl.BlockSpec((None, tr, cols), lambda k, j, c: (2 * k + c[0], j, 0)),
                  pl.BlockSpec((None, tr, cols), lambda k, j, c: (k, j, 0))],
        out_specs=pl.BlockSpec((None, tr, cols), lambda k, j, c: (k, j, 0)))
    return pl.pallas_call(
        body, name=name, grid_spec=grid_spec,
        out_shape=jax.ShapeDtypeStruct((4, rows, cols), buf.dtype),
        compiler_params=_params("parallel", "parallel"),
    )(core, buf, recv)


def _rs_chip_exchange(chip_sum, name):
    _, rows, cols = chip_sum.shape

    def body(cs_ref, out_ref, send_sems, recv_sems, local_sem):
        x, y, c = lax.axis_index("x"), lax.axis_index("y"), lax.axis_index("c")
        my_chip = 2 * x + y
        mine = pltpu.make_async_copy(cs_ref.at[my_chip], out_ref.at[my_chip], local_sem)
        mine.start()
        chips = [(1 - x, y), (x, 1 - y), (1 - x, 1 - y)]
        copies = [pltpu.make_async_remote_copy(
            src_ref=cs_ref.at[2 * px + py], dst_ref=out_ref.at[my_chip],
            send_sem=send_sems.at[j], recv_sem=recv_sems.at[j],
            device_id=(px, py, c), device_id_type=_MESH) for j, (px, py) in enumerate(chips)]
        for cp in copies:
            cp.start()
        for cp in copies:
            cp.wait()
        mine.wait()

    return pl.pallas_call(
        body, name=name,
        in_specs=[_ANY], out_specs=_ANY,
        out_shape=jax.ShapeDtypeStruct((4, rows, cols), chip_sum.dtype),
        scratch_shapes=[pltpu.SemaphoreType.DMA((3,)), pltpu.SemaphoreType.DMA((3,)), pltpu.SemaphoreType.DMA(())],
    )(chip_sum)


def _sum_slots(parts, name):
    slots, rows, cols = parts.shape
    tr = _row_tile(rows, 256) if rows % 16 == 0 else rows

    def body(p_ref, o_ref):
        acc = p_ref[0]
        for s in range(1, slots):
            acc = acc + p_ref[s]
        o_ref[...] = acc

    return pl.pallas_call(
        body, name=name, grid=(rows // tr,),
        in_specs=[pl.BlockSpec((slots, tr, cols), lambda i: (0, i, 0))],
        out_specs=pl.BlockSpec((tr, cols), lambda i: (i, 0)),
        out_shape=jax.ShapeDtypeStruct((rows, cols), parts.dtype),
        compiler_params=_params("parallel"),
    )(parts)


def _reduce_scatter(buf, core):
    recv = _rs_sibling_exchange(buf, "rs_sibling_exchange")
    chip_sum = _rs_chip_sum(buf, recv, core, "rs_chip_sum")
    parts = _rs_chip_exchange(chip_sum, "rs_chip_exchange")
    return _sum_slots(parts, "rs_sum_slots")


def _pad_rows(flat, cols, multiple):
    n = flat.shape[0]
    rows = -(-n // cols)
    rows = -(-rows // multiple) * multiple
    return jnp.pad(flat, (0, rows * cols - n)).reshape(rows, cols)


def _cols_to_chunks(full):
    lead = full.shape[:-1]
    c = full.shape[-1] // N_DEV
    x = full.reshape(-1, N_DEV, c)
    return jnp.transpose(x, (1, 0, 2)).reshape(N_DEV, -1)


def _chunks_to_cols(chunks, lead):
    n = 1
    for s in lead:
        n *= s
    c = chunks.shape[1] // n
    x = chunks.reshape(N_DEV, n, c)
    return jnp.transpose(x, (1, 0, 2)).reshape(tuple(lead) + (N_DEV * c,))


def kernel(x, meta_tokens, norm_mix_g, norm_ffn_g, final_norm_g, sc_w_in, sc_conv_w, sc_w_out, rg_w_in, rg_conv_w, rg_conv_b, rg_w_gate_a, rg_b_gate_a, rg_w_gate_x, rg_b_gate_x, rg_lambda, rg_w_out, ffn_w_up, ffn_conv_w, ffn_w_down, loss_target, m_meta_tokens, m_norm_mix_g, m_norm_ffn_g, m_final_norm_g, m_sc_w_in, m_sc_conv_w, m_sc_w_out, m_rg_w_in, m_rg_conv_w, m_rg_conv_b, m_rg_w_gate_a, m_rg_b_gate_a, m_rg_w_gate_x, m_rg_b_gate_x, m_rg_lambda, m_rg_w_out, m_ffn_w_up, m_ffn_conv_w, m_ffn_w_down, v_meta_tokens, v_norm_mix_g, v_norm_ffn_g, v_final_norm_g, v_sc_w_in, v_sc_conv_w, v_sc_w_out, v_rg_w_in, v_rg_conv_w, v_rg_conv_b, v_rg_w_gate_a, v_rg_b_gate_a, v_rg_w_gate_x, v_rg_b_gate_x, v_rg_lambda, v_rg_w_out, v_ffn_w_up, v_ffn_conv_w, v_ffn_w_down):
    weights = dict(meta_tokens=meta_tokens, norm_mix_g=norm_mix_g, norm_ffn_g=norm_ffn_g, final_norm_g=final_norm_g,
                   sc_w_in=sc_w_in, sc_conv_w=sc_conv_w, sc_w_out=sc_w_out, rg_w_in=rg_w_in, rg_conv_w=rg_conv_w,
                   rg_conv_b=rg_conv_b, rg_w_gate_a=rg_w_gate_a, rg_b_gate_a=rg_b_gate_a, rg_w_gate_x=rg_w_gate_x,
                   rg_b_gate_x=rg_b_gate_x, rg_lambda=rg_lambda, rg_w_out=rg_w_out, ffn_w_up=ffn_w_up,
                   ffn_conv_w=ffn_conv_w, ffn_w_down=ffn_w_down)
    m_in = dict(meta_tokens=m_meta_tokens, norm_mix_g=m_norm_mix_g, norm_ffn_g=m_norm_ffn_g, final_norm_g=m_final_norm_g,
                sc_w_in=m_sc_w_in, sc_conv_w=m_sc_conv_w, sc_w_out=m_sc_w_out, rg_w_in=m_rg_w_in, rg_conv_w=m_rg_conv_w,
                rg_conv_b=m_rg_conv_b, rg_w_gate_a=m_rg_w_gate_a, rg_b_gate_a=m_rg_b_gate_a, rg_w_gate_x=m_rg_w_gate_x,
                rg_b_gate_x=m_rg_b_gate_x, rg_lambda=m_rg_lambda, rg_w_out=m_rg_w_out, ffn_w_up=m_ffn_w_up,
                ffn_conv_w=m_ffn_conv_w, ffn_w_down=m_ffn_w_down)
    v_in = dict(meta_tokens=v_meta_tokens, norm_mix_g=v_norm_mix_g, norm_ffn_g=v_norm_ffn_g, final_norm_g=v_final_norm_g,
                sc_w_in=v_sc_w_in, sc_conv_w=v_sc_conv_w, sc_w_out=v_sc_w_out, rg_w_in=v_rg_w_in, rg_conv_w=v_rg_conv_w,
                rg_conv_b=v_rg_conv_b, rg_w_gate_a=v_rg_w_gate_a, rg_b_gate_a=v_rg_b_gate_a, rg_w_gate_x=v_rg_w_gate_x,
                rg_b_gate_x=v_rg_b_gate_x, rg_lambda=v_rg_lambda, rg_w_out=v_rg_w_out, ffn_w_up=v_ffn_w_up,
                ffn_conv_w=v_ffn_conv_w, ffn_w_down=v_ffn_w_down)
    names = list(weights)

    seq, d = x.shape[1], x.shape[2]
    n_meta = meta_tokens.shape[0]
    n_ffn = ffn_w_up.shape[0]
    rr = rg_w_gate_a.shape[1] * rg_w_gate_a.shape[2]
    core = lax.axis_index("c").astype(jnp.int32).reshape(1)

    assert n_ffn == 2
    def shard_rows(w_in, w_out):
        return jnp.concatenate([w_in.T, w_out], axis=0).astype(MXU_DTYPE)

    def both_orientations(gathered, w_in):
        r_in = w_in.shape[1]
        w_in_t = gathered[:, :r_in, :].reshape(N_DEV * r_in, d)
        w_out = gathered[:, r_in:, :].reshape(-1, d)
        return w_in_t.T, w_in_t, w_out, w_out.T

    shards = {"ffn0": shard_rows(ffn_w_up[0], ffn_w_down[0]), "rg": shard_rows(rg_w_in[0], rg_w_out[0]),
              "ffn1": shard_rows(ffn_w_up[1], ffn_w_down[1])}
    sc_w_in_f, sc_w_in_t, sc_w_out_f, sc_w_out_t = both_orientations(
        _all_gather(shard_rows(sc_w_in[0], sc_w_out[0]), "gather_sc"), sc_w_in[0])
    ffn_w_up_f, ffn_w_up_t, ffn_w_down_f, ffn_w_down_t = [None] * 2, [None] * 2, [None] * 2, [None] * 2

    small_names = ["meta_tokens", "sc_conv_w", "rg_conv_w", "rg_conv_b", "rg_b_gate_a", "rg_b_gate_x", "rg_lambda",
                   "ffn_conv_w"]
    small_lead = {n: weights[n].shape[:-1] for n in small_names}
    small_sizes = [weights[n].size for n in small_names]
    small_flat = jnp.concatenate([weights[n].reshape(-1) for n in small_names])
    small_rows = _pad_rows(small_flat, d, SUBLANES)
    small_g = _all_gather(small_rows, "gather_small").reshape(N_DEV, -1)
    small_full = {}
    o = 0
    for n, sz in zip(small_names, small_sizes):
        small_full[n] = _chunks_to_cols(small_g[:, o:o + sz], small_lead[n])
        o += sz

    wg, wgt = _pair_gate_weights(rg_w_gate_a[0].astype(MXU_DTYPE), rg_w_gate_x[0].astype(MXU_DTYPE))
    rg_cw, rg_cb = small_full["rg_conv_w"][0], small_full["rg_conv_b"]
    rg_ba, rg_bx, rg_lam = small_full["rg_b_gate_a"], small_full["rg_b_gate_x"], small_full["rg_lambda"]
    sc_cw = small_full["sc_conv_w"][0]
    ffn_cw = small_full["ffn_conv_w"]

    tp = _row_tile(n_meta + seq, ROW_TILE_PERM)

    def ffn_fwd(h, l, rider):
        n, a, z, *gathered = _ffn_up_fused(h, norm_ffn_g[l:l + 1], ffn_w_up_f[l], ffn_cw[l], tp, f"ffn{l}_up", rider)
        return _matmul_residual(z, ffn_w_down_f[l], h, f"ffn{l}_down"), (n, a, z), gathered

    h0, n0, p0, q0, ffn0_w = _sc_in_fused(x[0], small_full["meta_tokens"], norm_mix_g[0:1], sc_w_in_f, sc_cw, tp, "sc_in",
                                          _GatherRider(shards["ffn0"]))
    ffn_w_up_f[0], ffn_w_up_t[0], ffn_w_down_f[0], ffn_w_down_t[0] = both_orientations(ffn0_w, ffn_w_up[0])
    h1 = _matmul_residual(q0, sc_w_out_f, h0, "sc_out")
    h2, ffn0_saved, (rg_w,) = ffn_fwd(h1, 0, _GatherRider(shards["rg"]))
    rg_w_in_f, rg_w_in_t, rg_w_out_f, rg_w_out_t = both_orientations(rg_w, rg_w_in[0])
    n2, p2, y2, hs2, ffn1_w = _rg_in_fused(h2, norm_mix_g[1:2], rg_w_in_f, rg_cw, rg_cb, wg, rg_ba, rg_bx, rg_lam,
                                           tp, "rg_in", _GatherRider(shards["ffn1"]))
    ffn_w_up_f[1], ffn_w_up_t[1], ffn_w_down_f[1], ffn_w_down_t[1] = both_orientations(ffn1_w, ffn_w_up[1])
    h3 = _matmul_residual(y2, rg_w_out_f, h2, "rg_out")
    ffn1_saved = tuple(_ffn_up_fused(h3, norm_ffn_g[1:2], ffn_w_up_f[1], ffn_cw[1], tp, "ffn1_up"))
    dh4, loss_tile, d_final_g = _down_loss_head(ffn1_saved[2], ffn_w_down_f[1], h3, final_norm_g.reshape(1, d),
                                                loss_target[0], n_meta, tp, "ffn1_down_loss_head")
    loss = lax.psum(loss_tile[0, 0], AXES)

    arrived = {}

    def ffn_bwd(dh_out, h_in, saved, l):
        n, a, z = saved
        da, dhb, d_cw = _ffn_down_bwd_fused(dh_out, ffn_w_down_t[l], a, ffn_cw[l], tp, f"ffn{l}_down_bwd")
        d_w_down = _wgrad(z, dhb, f"ffn{l}_down_wgrad")
        d_w_up_t, arrived[f"ffn_w_down{l}"] = _wgrad(da, n, f"ffn{l}_up_wgrad", _ScatterRider(d_w_down))
        rows = d_w_up_t.shape[0] // N_DEV
        first_rows = (rows // 2) // (2 * SUBLANES) * (2 * SUBLANES)
        dh_in, d_g, arrived[f"ffn_w_up{l}a"] = _dgrad_in_norm(
            da, ffn_w_up_t[l], h_in, norm_ffn_g[l:l + 1], dh_out, f"ffn{l}_up_dgrad",
            _ScatterRider(d_w_up_t, (0, first_rows)))
        return dh_in, d_cw, d_g, _ScatterRider(d_w_up_t, (first_rows, rows - first_rows))

    dh3, d_fcw1, d_fg1, up1_rest = ffn_bwd(dh4, h3, ffn1_saved, 1)

    dp2, dhb3, d_rg_cw, d_rg_cb, d_wg, d_rg_ba, d_rg_bx, d_rg_lam, arrived["ffn_w_up1b"] = _rg_out_bwd_fused(
        dh3, rg_w_out_t, p2, hs2, rg_cw, rg_cb, wg, wgt, rg_ba, rg_bx, rg_lam, tp, "rg_out_bwd", up1_rest)
    d_wa, d_wx = _unpair_gate_grads(d_wg, rg_w_gate_a.shape[2])
    d_rg_w_out = _wgrad(y2, dhb3, "rg_out_wgrad")
    d_rg_w_in_t, arrived["rg_w_out"] = _wgrad(dp2, n2, "rg_in_wgrad", _ScatterRider(d_rg_w_out))
    dh2, d_mg1, arrived["rg_w_in"] = _dgrad_in_norm(
        dp2, rg_w_in_t, h2, norm_mix_g[1:2], dh3, "rg_in_dgrad", _ScatterRider(d_rg_w_in_t))

    dh1, d_fcw0, d_fg0, up0_rest = ffn_bwd(dh2, h1, ffn0_saved, 0)

    dp0, dhb1, d_sc_cw, arrived["ffn_w_up0b"] = _sc_out_bwd_fused(dh1, sc_w_out_t, p0, sc_cw, tp, "sc_out_bwd", up0_rest)
    d_sc_w_out = _wgrad(q0, dhb1, "sc_out_wgrad")
    d_sc_w_in_t, arrived["sc_w_out"] = _wgrad(dp0, n0, "sc_in_wgrad", _ScatterRider(d_sc_w_out))
    d_mg0, d_meta, d_x, arrived["sc_w_in"] = _dgrad_in_to_input(
        dp0, sc_w_in_t, h0, norm_mix_g[0:1], dh1, n_meta, tp, "sc_in_dgrad", _ScatterRider(d_sc_w_in_t))
    grad_x = d_x[None]

    grads = {}
    for n in ("sc_w_in", "rg_w_in"):
        grads[n] = _sum_slots(arrived[n], f"sum_{n}").T[None]
    for n in ("sc_w_out", "rg_w_out"):
        grads[n] = _sum_slots(arrived[n], f"sum_{n}")[None]
    grads["ffn_w_up"] = jnp.stack([jnp.concatenate(
        [_sum_slots(arrived[f"ffn_w_up{l}{part}"], f"sum_ffn_w_up{l}{part}") for part in "ab"], axis=0).T
        for l in range(n_ffn)])
    grads["ffn_w_down"] = jnp.stack([_sum_slots(arrived[f"ffn_w_down{l}"], f"sum_ffn_w_down{l}") for l in range(n_ffn)])

    small_grads = {"meta_tokens": d_meta,"sc_conv_w": d_sc_cw[None], "rg_conv_w": d_rg_cw[None],
                   "rg_conv_b": d_rg_cb, "rg_b_gate_a": d_rg_ba, "rg_b_gate_x": d_rg_bx, "rg_lambda": d_rg_lam,
                   "ffn_conv_w": jnp.stack([d_fcw0, d_fcw1])}
    small_chunks = jnp.concatenate([_cols_to_chunks(small_grads[n]) for n in small_names], axis=1)
    pad = small_rows.shape[0] * d - small_chunks.shape[1]
    small_chunks = jnp.pad(small_chunks, ((0, 0), (0, pad))).reshape(N_DEV, small_rows.shape[0], d)
    rep_names = ["norm_mix_g", "norm_ffn_g", "final_norm_g", "rg_w_gate_a", "rg_w_gate_x"]
    rep_grads = {"norm_mix_g": jnp.concatenate([d_mg0, d_mg1], axis=0),
                 "norm_ffn_g": jnp.concatenate([d_fg0, d_fg1], axis=0),
                 "final_norm_g": d_final_g.reshape(-1), "rg_w_gate_a": d_wa[None], "rg_w_gate_x": d_wx[None]}
    rep_flat = jnp.concatenate([rep_grads[n].reshape(-1) for n in rep_names])
    rep_chunk_rows = -(-rep_flat.shape[0] // (N_DEV * d))
    rep_chunk_rows += -(small_rows.shape[0] + rep_chunk_rows) % 16
    rep_chunks = jnp.pad(rep_flat, (0, N_DEV * rep_chunk_rows * d - rep_flat.shape[0])).reshape(N_DEV, rep_chunk_rows, d)

    reduced = _reduce_scatter(jnp.concatenate([small_chunks, rep_chunks], axis=1), core)

    small_red = reduced[0:small_rows.shape[0]].reshape(-1)
    o = small_rows.shape[0]
    so = 0
    for n, sz in zip(small_names, small_sizes):
        grads[n] = small_red[so:so + sz].reshape(weights[n].shape)
        so += sz
    rep_red = _all_gather(reduced[o:o + rep_chunks.shape[1]], "gather_replicated_grads").reshape(-1)
    ro = 0
    for n in rep_names:
        sz = weights[n].size
        grads[n] = rep_red[ro:ro + sz].reshape(weights[n].shape)
        ro += sz

    delta, new_m, new_v = {}, {}, {}
    for n in names:
        delta[n], new_m[n], new_v[n] = _adamw_nd(weights[n], grads[n], m_in[n], v_in[n], f"adamw_{n}")

    return (loss, grad_x, *[grads[n] for n in names], *[delta[n] for n in names],
            *[new_m[n] for n in names], *[new_v[n] for n in names])
```

```python
import jax
import jax.numpy as jnp
from jax import lax
from jax.experimental import pallas as pl
from jax.experimental.pallas import tpu as pltpu

F32 = jnp.float32
MXU_DTYPE = jnp.bfloat16
RMS_EPS = 1e-6
RG_C = 8.0
ADAM_LR = 0.001
ADAM_B1 = 0.9
ADAM_B2 = 0.999
ADAM_EPS = 1e-08
ADAM_WD = 0.01
ADAM_STEP = 10

N_DEV = 8
AXES = ("x", "y", "c")
SUBLANES = 8
LANES = 128
VMEM_LIMIT_BYTES = 48 * 1024 * 1024
ROW_TILE_MATMUL = 700
ROW_TILE_WGRAD = 3300
ROW_TILE_PERM = 400
STRIP = 256

_TN = (((0,), (0,)), ((), ()))
_NN = (((1,), (0,)), ((), ()))


def _row_tile(t, target):
    best = None
    for tm in range(16, t + 1, 16):
        if t % tm == 0 and tm <= target:
            best = tm
    return best if best is not None else t


def _col_tile(n, target):
    best = None
    for tn in range(LANES, n + 1, LANES):
        if n % tn == 0 and tn <= target:
            best = tn
    return best if best is not None else n


def _params(*sem):
    return pltpu.CompilerParams(dimension_semantics=sem, vmem_limit_bytes=VMEM_LIMIT_BYTES)


def _dot(a, b, dims):
    return lax.dot_general(a, b, dims, preferred_element_type=F32)


def _sigmoid(x):
    return 1.0 / (1.0 + jnp.exp(-x))


def _sigmoid_tanh(x):
    return 0.5 * jnp.tanh(0.5 * x) + 0.5


def _gelu(x):
    c = 0.7978845608028654
    t = jnp.tanh(c * (x + 0.044715 * (x * x * x)))
    return 0.5 * x * (1.0 + t)


def _gelu_and_grad(x):
    c = 0.7978845608028654
    x2 = x * x
    t = jnp.tanh(c * (x + 0.044715 * (x2 * x)))
    half = 0.5 * (1.0 + t)
    return x * half, half + 0.5 * x * (1.0 - t * t) * c * (1.0 + 3.0 * 0.044715 * x2)


def _softplus(x):
    return jnp.maximum(x, 0.0) + jnp.log1p(jnp.exp(-jnp.abs(x)))


def _time_scratch(tm, d):
    return [pltpu.VMEM((2, tm, d), F32), pltpu.VMEM((d // LANES, tm, LANES), F32), pltpu.SemaphoreType.DMA((2,))]


def _fetch_time_tile(i, nt, src_ref, head, tbuf, sems, tm):
    n_head = head.shape[0]

    def tile(j, slot):
        start = pl.multiple_of(j * tm - n_head, SUBLANES)
        return pltpu.make_async_copy(src_ref.at[pl.ds(start, tm)], tbuf.at[slot], sems.at[slot])

    first = pltpu.make_async_copy(src_ref.at[pl.ds(0, tm - n_head)], tbuf.at[0, pl.ds(n_head, tm - n_head)], sems.at[0])

    @pl.when(i == 0)
    def _():
        tbuf[0, 0:n_head, :] = head
        first.start()

    @pl.when(i + 1 < nt)
    def _():
        tile(i + 1, (i + 1) % 2).start()

    @pl.when(i == 0)
    def _():
        first.wait()

    @pl.when(i > 0)
    def _():
        tile(i, i % 2).wait()


def _time_to_tile_order(t_ref, slabs, out_ref):
    tm, d = out_ref.shape
    groups = tm // SUBLANES
    for k in range(d // LANES):
        slabs[k] = t_ref[:, k * LANES:(k + 1) * LANES]
    for k in range(d // LANES):
        for g in range(groups):
            out_ref[g * SUBLANES:(g + 1) * SUBLANES, k * LANES:(k + 1) * LANES] = slabs[k, pl.ds(g, SUBLANES, stride=groups), :]


def _tile_to_time_order(p_ref, slabs, t_ref):
    tm, d = p_ref.shape
    groups = tm // SUBLANES
    for k in range(d // LANES):
        for g in range(groups):
            slabs[k, pl.ds(g, SUBLANES, stride=groups), :] = p_ref[g * SUBLANES:(g + 1) * SUBLANES, k * LANES:(k + 1) * LANES]
    for k in range(d // LANES):
        t_ref[:, k * LANES:(k + 1) * LANES] = slabs[k]


def _rows(ref, sl):
    return [ref[k:k + 1, sl] for k in range(ref.shape[0])]


def _shift_down(x, before, s):
    if s == 0:
        return x
    n = x.shape[0]
    row = lax.broadcasted_iota(jnp.int32, (SUBLANES, x.shape[1]), 0)
    heads = []
    for g in range(s):
        v = x[n - (s - g) * SUBLANES:n - (s - g - 1) * SUBLANES]
        heads.append(pltpu.roll(jnp.where(row == SUBLANES - 1, before[s - g - 1], v), 1, axis=0))
    return jnp.concatenate(heads + [x[0:n - s * SUBLANES]], axis=0)


def _shift_up(x, after, s):
    if s == 0:
        return x
    row = lax.broadcasted_iota(jnp.int32, (SUBLANES, x.shape[1]), 0)
    tails = []
    for m in range(s):
        v = x[m * SUBLANES:(m + 1) * SUBLANES]
        tails.append(pltpu.roll(jnp.where(row == 0, after[m], v), SUBLANES - 1, axis=0))
    return jnp.concatenate([x[s * SUBLANES:]] + tails, axis=0)


def _weighted(w, windows):
    y = w[0] * windows[0]
    for k in range(1, len(w)):
        y = y + w[k] * windows[k]
    return y


def _windows(x, before, k_width):
    return [_shift_down(x, before, k_width - 1 - k) for k in range(k_width)]


def _conv_t(dy, after, w):
    k_width = len(w)
    return _weighted(w, [_shift_up(dy, after, k_width - 1 - k) for k in range(k_width)])


def _blocks(ref, sl, count, newest_first):
    n = ref.shape[0] // SUBLANES
    order = range(n - 1, n - 1 - count, -1) if newest_first else range(count)
    return [ref[b * SUBLANES:(b + 1) * SUBLANES, sl] for b in order]


def _halo_specs(tm, cols, count, tile_of):
    def spec(k):
        return pl.BlockSpec((SUBLANES, cols), lambda i: (jnp.maximum(tile_of(i) * (tm // SUBLANES) - k, 0), 0))
    return [spec(k) for k in range(1, count + 1)]


def _scan_tile(coef, val, out, carry, groups, reverse):
    cols = coef.shape[1]
    row = lax.broadcasted_iota(jnp.int32, (SUBLANES, cols), 0)

    def blk(i):
        g = groups - 1 - i if reverse else i
        return pl.ds(pl.multiple_of(g * SUBLANES, SUBLANES), SUBLANES)

    def local(i, pl_):
        p_prev, l_prev = pl_
        a = coef[blk(i), :]
        p = a * p_prev
        l = a * l_prev + val[blk(i), :]
        coef[blk(i), :] = p
        val[blk(i), :] = l
        return p, l

    pf, lf = lax.fori_loop(0, groups, local, (jnp.ones((SUBLANES, cols), F32), jnp.zeros((SUBLANES, cols), F32)))
    for s in (1, 2, 4):
        keep, sh = (row < SUBLANES - s, SUBLANES - s) if reverse else (row >= s, s)
        p_s = jnp.where(keep, pltpu.roll(pf, sh, axis=0), 1.0)
        l_s = jnp.where(keep, pltpu.roll(lf, sh, axis=0), 0.0)
        lf = pf * l_s + lf
        pf = pf * p_s
    end = lf + pf * carry
    if reverse:
        init = jnp.where(row == SUBLANES - 1, carry, pltpu.roll(end, SUBLANES - 1, axis=0))
        leaving = jnp.broadcast_to(end[0:1, :], (SUBLANES, cols))
    else:
        init = jnp.where(row == 0, carry, pltpu.roll(end, 1, axis=0))
        leaving = jnp.broadcast_to(end[SUBLANES - 1:SUBLANES, :], (SUBLANES, cols))

    def fix(i, _):
        out[blk(i), :] = val[blk(i), :] + coef[blk(i), :] * init
        return 0

    lax.fori_loop(0, groups, fix, 0)
    return leaving


def _resident(shape):
    return pl.BlockSpec(shape, lambda *_: (0,) * len(shape), pipeline_mode=pl.Buffered(1))


def _full(shape):
    return pl.BlockSpec(shape, lambda *_: (0,) * len(shape))


def _rmsnorm_to(h_ref, g_ref, n_ref):
    x = h_ref[...]
    ms = jnp.mean(x * x, axis=-1, keepdims=True)
    n_ref[...] = (x * lax.rsqrt(ms + RMS_EPS) * g_ref[...]).astype(n_ref.dtype)


_ANY = pl.BlockSpec(memory_space=pl.ANY)
_MESH = pl.DeviceIdType.MESH


def _dma_sems(n):
    return pltpu.SemaphoreType.DMA((n,))


def _when_each(*phases):
    for cond, fn in phases:
        if cond is not None:
            pl.when(cond)(fn)


class _NoRider:
    inputs = in_specs = out_shape = out_specs = scratch_shapes = ()

    def __call__(self, first, middle, last, ins, outs, scratch):
        pass


_NO_RIDER = _NoRider()


def _hand_on_step(nt):
    return (3 * nt) // 4


def _split_refs(rider, n_out, n_scratch, rest):
    a = len(rider.inputs)
    b = a + n_out
    c = b + len(rider.out_shape)
    e = c + n_scratch
    return rest[:a], rest[a:b], rest[b:c], rest[c:e], rest[e:]


class _GatherRider:
    def __init__(self, blocks):
        n = len(blocks)
        self.inputs = tuple(blocks)
        self.in_specs = (_ANY,) * n
        self.out_shape = tuple(jax.ShapeDtypeStruct((N_DEV,) + b.shape, b.dtype) for b in blocks)
        self.out_specs = (_ANY,) * n
        self.scratch_shapes = (_dma_sems(7 * n), _dma_sems(7 * n), _dma_sems(n))

    def __call__(self, first, middle, last, ins, outs, scratch):
        n = len(ins)
        send_sems, recv_sems, local_sems = scratch
        x, y, c = lax.axis_index("x"), lax.axis_index("y"), lax.axis_index("c")
        me, sibling = (x, y, c), (x, y, 1 - c)
        chips = [(1 - x, y), (x, 1 - y), (1 - x, 1 - y)]

        def slot(b, px, py, pc):
            return outs[b].at[4 * px + 2 * py + pc]

        def copy(k, b, block, to, src=None):
            return pltpu.make_async_remote_copy(
                src_ref=slot(b, *block) if src is None else src, dst_ref=slot(b, *block),
                send_sem=send_sems.at[k * n + b], recv_sem=recv_sems.at[k * n + b], device_id=to, device_id_type=_MESH)

        mine = [pltpu.make_async_copy(ins[b], slot(b, *me), local_sems.at[b]) for b in range(n)]
        own = [copy(0, b, me, sibling, src=ins[b]) for b in range(n)]
        own += [copy(1 + j, b, me, (*chip, c), src=ins[b]) for j, chip in enumerate(chips) for b in range(n)]
        passed = [[copy(4 + j, b, (*chip, c), sibling) for b in range(n)] for j, chip in enumerate(chips)]

        def at_first():
            for cp in mine + own:
                cp.start()

        def at_middle():
            for j, chip in enumerate(chips):
                for b in range(n):
                    copy(1 + j, b, (*chip, c), me).wait_recv()
                    passed[j][b].start()

        def at_last():
            for b in range(n):
                copy(0, b, sibling, me).wait_recv()
                for j, chip in enumerate(chips):
                    copy(4 + j, b, (*chip, 1 - c), me).wait_recv()
            for cp in own + [cp for group in passed for cp in group]:
                cp.wait_send()
            for cp in mine:
                cp.wait()

        _when_each((first, at_first), (middle, at_middle), (last, at_last))


class _ScatterRider:
    def __init__(self, grad, rows=None):
        r = grad.shape[0] // N_DEV
        chunks = grad.reshape(N_DEV, r, grad.shape[1])
        self.rows = rows if rows is not None else (0, r)
        self.inputs = (chunks,)
        self.in_specs = (_ANY,)
        self.out_shape = (jax.ShapeDtypeStruct((N_DEV, self.rows[1], grad.shape[1]), chunks.dtype),)
        self.out_specs = (_ANY,)
        self.scratch_shapes = (_dma_sems(N_DEV - 1), _dma_sems(N_DEV - 1), pltpu.SemaphoreType.DMA(()))

    def __call__(self, first, middle, last, ins, outs, scratch):
        (g_ref,), (r_ref,) = ins, outs
        send_sems, recv_sems, local_sem = scratch
        x, y, c = lax.axis_index("x"), lax.axis_index("y"), lax.axis_index("c")
        me = 4 * x + 2 * y + c
        part = pl.ds(*self.rows)
        mine = pltpu.make_async_copy(g_ref.at[me, part], r_ref.at[me], local_sem)
        copies = []
        for k in range(1, N_DEV):
            px, py, pc = (1 - x if k & 4 else x), (1 - y if k & 2 else y), (1 - c if k & 1 else c)
            copies.append(pltpu.make_async_remote_copy(
                src_ref=g_ref.at[4 * px + 2 * py + pc, part], dst_ref=r_ref.at[me],
                send_sem=send_sems.at[k - 1], recv_sem=recv_sems.at[k - 1],
                device_id=(px, py, pc), device_id_type=_MESH))

        def at_first():
            mine.start()
            for cp in copies:
                cp.start()

        def at_last():
            for cp in copies:
                cp.wait()
            mine.wait()

        _when_each((first, at_first), (last, at_last))


def _matmul_residual(q, w, h, name):
    t, k = q.shape
    d = w.shape[1]
    tm = _row_tile(t, ROW_TILE_MATMUL)

    def body(q_ref, w_ref, h_ref, o_ref):
        o_ref[...] = h_ref[...] + _dot(q_ref[...], w_ref[...], _NN)

    return pl.pallas_call(
        body, name=name, grid=(t // tm,),
        in_specs=[pl.BlockSpec((tm, k), lambda i: (i, 0)),
                  _resident((k, d)),
                  pl.BlockSpec((tm, d), lambda i: (i, 0))],
        out_specs=pl.BlockSpec((tm, d), lambda i: (i, 0)),
        out_shape=jax.ShapeDtypeStruct((t, d), F32),
        compiler_params=_params("parallel"),
    )(q, w, h)


def _dgrad_in_norm(dp, wt, h, g, dh_next, name, rider=_NO_RIDER):
    t, n = dp.shape
    d = wt.shape[1]
    tm = _row_tile(t, ROW_TILE_MATMUL)
    nt = t // tm

    def body(dp_ref, wt_ref, h_ref, g_ref, dhn_ref, *rest):
        r_in, (dh_ref, dg_ref), r_out, _, r_scratch = _split_refs(rider, 2, 0, rest)
        i = pl.program_id(0)
        rider(i == 0, i == nt // 2, None, r_in, r_out, r_scratch)
        dn = _dot(dp_ref[...], wt_ref[...], _NN)
        x = h_ref[...]
        ms = jnp.mean(x * x, axis=-1, keepdims=True)
        r = lax.rsqrt(ms + RMS_EPS)
        xhat = x * r
        dng = dn * g_ref[...]
        c = jnp.mean(dng * xhat, axis=-1, keepdims=True)
        dh_ref[...] = dhn_ref[...] + r * (dng - xhat * c)
        part = jnp.sum(dn * xhat, axis=0, keepdims=True)

        @pl.when(i == 0)
        def _():
            dg_ref[...] = part

        @pl.when(i > 0)
        def _():
            dg_ref[...] += part

        rider(None, None, i == nt - 1, r_in, r_out, r_scratch)

    return pl.pallas_call(
        body, name=name, grid=(nt,),
        in_specs=[pl.BlockSpec((tm, n), lambda i: (i, 0)),
                  _resident((n, d)),
                  pl.BlockSpec((tm, d), lambda i: (i, 0)),
                  pl.BlockSpec((1, d), lambda i: (0, 0)),
                  pl.BlockSpec((tm, d), lambda i: (i, 0))] + list(rider.in_specs),
        out_specs=[pl.BlockSpec((tm, d), lambda i: (i, 0)),
                   pl.BlockSpec((1, d), lambda i: (0, 0))] + list(rider.out_specs),
        out_shape=[jax.ShapeDtypeStruct((t, d), F32), jax.ShapeDtypeStruct((1, d), F32)] + list(rider.out_shape),
        scratch_shapes=list(rider.scratch_shapes),
        compiler_params=_params("arbitrary"),
    )(dp, wt, h, g, dh_next, *rider.inputs)


def _dgrad_in_to_input(dp, wt, h, g, dh_next, n_head, tm, name, rider=_NO_RIDER):
    t, n = dp.shape
    d = wt.shape[1]
    nt = t // tm

    def body(dp_ref, wt_ref, h_ref, g_ref, dhn_ref, *rest):
        r_in, (dg_ref, head_ref, rest_ref), r_out, (pbuf, tout, slabs, sems), r_scratch = _split_refs(rider, 3, 4, rest)
        i = pl.program_id(0)
        rider(i == 0, i == nt // 2, None, r_in, r_out, r_scratch)
        dn = _dot(dp_ref[...], wt_ref[...], _NN)
        x = h_ref[...]
        ms = jnp.mean(x * x, axis=-1, keepdims=True)
        r = lax.rsqrt(ms + RMS_EPS)
        xhat = x * r
        dng = dn * g_ref[...]
        c = jnp.mean(dng * xhat, axis=-1, keepdims=True)
        pbuf[...] = dhn_ref[...] + r * (dng - xhat * c)
        part = jnp.sum(dn * xhat, axis=0, keepdims=True)

        @pl.when(i == 0)
        def _():
            dg_ref[...] = part

        @pl.when(i > 0)
        def _():
            dg_ref[...] += part

        def store(j):
            if isinstance(j, int) and j == 0:
                return pltpu.make_async_copy(tout.at[0, pl.ds(n_head, tm - n_head)],
                                             rest_ref.at[pl.ds(0, tm - n_head)], sems.at[0])
            start = j * tm - n_head
            start = start if isinstance(start, int) else pl.multiple_of(start, SUBLANES)
            return pltpu.make_async_copy(tout.at[j % 2], rest_ref.at[pl.ds(start, tm)], sems.at[j % 2])

        @pl.when(i == 2)
        def _():
            store(0).wait()

        @pl.when(i > 2)
        def _():
            store(i - 2).wait()

        _tile_to_time_order(pbuf, slabs, tout.at[i % 2])

        @pl.when(i == 0)
        def _():
            head_ref[...] = tout[0, 0:n_head, :]
            store(0).start()

        @pl.when(i > 0)
        def _():
            store(i).start()

        @pl.when(i == nt - 1)
        def _():
            for j in (nt - 2, nt - 1):
                if j >= 0:
                    store(j).wait()

        rider(None, None, i == nt - 1, r_in, r_out, r_scratch)

    return pl.pallas_call(
        body, name=name, grid=(nt,),
        in_specs=[pl.BlockSpec((tm, n), lambda i: (i, 0)),
                  _resident((n, d)),
                  pl.BlockSpec((tm, d), lambda i: (i, 0)),
                  pl.BlockSpec((1, d), lambda i: (0, 0)),
                  pl.BlockSpec((tm, d), lambda i: (i, 0))] + list(rider.in_specs),
        out_specs=[pl.BlockSpec((1, d), lambda i: (0, 0)), _full((n_head, d)), _ANY] + list(rider.out_specs),
        out_shape=[jax.ShapeDtypeStruct((1, d), F32), jax.ShapeDtypeStruct((n_head, d), F32),
                   jax.ShapeDtypeStruct((t - n_head, d), F32)] + list(rider.out_shape),
        scratch_shapes=[pltpu.VMEM((tm, d), F32)] + _time_scratch(tm, d) + list(rider.scratch_shapes),
        compiler_params=_params("arbitrary"),
    )(dp, wt, h, g, dh_next, *rider.inputs)


def _wgrad(a, b, name, rider=_NO_RIDER):
    t, m = a.shape
    d = b.shape[1]
    tmm, tk = _col_tile(m, 1024), _row_tile(t, ROW_TILE_WGRAD)
    if tmm < 512:
        tmm, tk = _col_tile(m, 1536), _row_tile(t, ROW_TILE_MATMUL)
    nm, nk = m // tmm, t // tk

    def body(a_ref, b_ref, *rest):
        r_in, (o_ref,), r_out, _, r_scratch = _split_refs(rider, 1, 0, rest)
        i, k = pl.program_id(0), pl.program_id(1)
        rider(jnp.logical_and(i == 0, k == 0), None, None, r_in, r_out, r_scratch)

        @pl.when(k == 0)
        def _():
            o_ref[...] = jnp.zeros_like(o_ref)

        o_ref[...] += _dot(a_ref[...], b_ref[...], _TN)
        rider(None, None, jnp.logical_and(i == nm - 1, k == nk - 1), r_in, r_out, r_scratch)

    outs = pl.pallas_call(
        body, name=name, grid=(nm, nk),
        in_specs=[pl.BlockSpec((tk, tmm), lambda i, k: (k, i)),
                  pl.BlockSpec((tk, d), lambda i, k: (k, 0))] + list(rider.in_specs),
        out_specs=[pl.BlockSpec((tmm, d), lambda i, k: (i, 0))] + list(rider.out_specs),
        out_shape=[jax.ShapeDtypeStruct((m, d), F32)] + list(rider.out_shape),
        scratch_shapes=list(rider.scratch_shapes),
        compiler_params=_params("arbitrary", "arbitrary"),
    )(a, b, *rider.inputs)
    return outs if rider.out_shape else outs[0]


def _ffn_up_fused(h, g, w, conv_w, tm, name, rider=_NO_RIDER):
    t, d = h.shape
    f2 = w.shape[1]
    f = f2 // 2
    cw = min(STRIP, f)
    kw = conv_w.shape[0]
    nh = kw - 1
    nt = t // tm

    def body(h_ref, g_ref, w_ref, cw_ref, *rest):
        r_in, (n_ref, a_ref, z_ref), r_out, (carry,), r_scratch = _split_refs(rider, 3, 1, rest)
        i = pl.program_id(0)
        rider(i == 0, i == _hand_on_step(nt), None, r_in, r_out, r_scratch)

        @pl.when(i == 0)
        def _():
            carry[...] = jnp.zeros_like(carry)

        _rmsnorm_to(h_ref, g_ref, n_ref)
        for c in range(0, f, cw):
            conv = []
            for sl in (slice(c, c + cw), slice(f + c, f + c + cw)):
                a = _dot(n_ref[...], w_ref[:, sl], _NN)
                a_ref[:, sl] = a
                conv.append(_weighted(_rows(cw_ref, sl), _windows(a, _blocks(carry, sl, nh, True), kw)))
                carry[:, sl] = a[tm - nh * SUBLANES:tm]
            gg, vv = conv
            z_ref[:, c:c + cw] = (gg * _sigmoid(gg) * vv).astype(z_ref.dtype)
        rider(None, None, i == nt - 1, r_in, r_out, r_scratch)

    return pl.pallas_call(
        body, name=name, grid=(nt,),
        in_specs=[pl.BlockSpec((tm, d), lambda i: (i, 0)), pl.BlockSpec((1, d), lambda i: (0, 0)),
                  _resident((d, f2)), _full(conv_w.shape)] + list(rider.in_specs),
        out_specs=[pl.BlockSpec((tm, d), lambda i: (i, 0)), pl.BlockSpec((tm, f2), lambda i: (i, 0)),
                   pl.BlockSpec((tm, f), lambda i: (i, 0))] + list(rider.out_specs),
        out_shape=[jax.ShapeDtypeStruct((t, d), MXU_DTYPE), jax.ShapeDtypeStruct((t, f2), F32),
                   jax.ShapeDtypeStruct((t, f), MXU_DTYPE)] + list(rider.out_shape),
        scratch_shapes=[pltpu.VMEM((nh * SUBLANES, f2), F32)] + list(rider.scratch_shapes),
        compiler_params=_params("arbitrary"),
    )(h, g, w, conv_w, *rider.inputs)


def _ffn_down_bwd_fused(dh, wdt, a, conv_w, tm, name):
    t, d = dh.shape
    f = wdt.shape[1]
    f2 = 2 * f
    nt = t // tm
    cw = min(STRIP, f)
    kw = conv_w.shape[0]
    nh = kw - 1
    rev = lambda i: nt - 1 - i

    def body(dh_ref, wdt_ref, a_ref, *rest):
        halo_refs = rest[:nh]
        cw_ref, da_ref, dhb_ref, dw_ref, carry = rest[nh:]
        i = pl.program_id(0)
        newest, oldest = i == 0, i == nt - 1

        @pl.when(newest)
        def _():
            carry[...] = jnp.zeros_like(carry)
            dw_ref[...] = jnp.zeros_like(dw_ref)

        dhb_ref[...] = dh_ref[...].astype(dhb_ref.dtype)
        for c in range(0, f, cw):
            gsl, vsl = slice(c, c + cw), slice(f + c, f + c + cw)
            dz = _dot(dhb_ref[...], wdt_ref[:, gsl], _NN)
            win, conv = {}, {}
            for sl in (gsl, vsl):
                before = [jnp.where(oldest, 0.0, r[:, sl]) for r in halo_refs]
                win[sl.start] = _windows(a_ref[:, sl], before, kw)
                conv[sl.start] = _weighted(_rows(cw_ref, sl), win[sl.start])
            gg, vv = conv[gsl.start], conv[vsl.start]
            s = _sigmoid(gg)
            grads = {gsl.start: dz * vv * s * (1.0 + gg * (1.0 - s)), vsl.start: dz * gg * s}
            for sl in (gsl, vsl):
                dconv = grads[sl.start]
                da_ref[:, sl] = _conv_t(dconv, _blocks(carry, sl, nh, False), _rows(cw_ref, sl)).astype(da_ref.dtype)
                for k in range(kw):
                    dw_ref[k:k + 1, sl] += jnp.sum(dconv * win[sl.start][k], axis=0, keepdims=True)
                carry[:, sl] = dconv[0:nh * SUBLANES]

    return pl.pallas_call(
        body, name=name, grid=(nt,),
        in_specs=[pl.BlockSpec((tm, d), lambda i: (rev(i), 0)), _resident((d, f)),
                  pl.BlockSpec((tm, f2), lambda i: (rev(i), 0))] + _halo_specs(tm, f2, nh, rev) + [_full(conv_w.shape)],
        out_specs=[pl.BlockSpec((tm, f2), lambda i: (rev(i), 0)), pl.BlockSpec((tm, d), lambda i: (rev(i), 0)),
                   _full(conv_w.shape)],
        out_shape=[jax.ShapeDtypeStruct((t, f2), MXU_DTYPE), jax.ShapeDtypeStruct((t, d), MXU_DTYPE),
                   jax.ShapeDtypeStruct(conv_w.shape, F32)],
        scratch_shapes=[pltpu.VMEM((nh * SUBLANES, f2), F32)],
        compiler_params=_params("arbitrary"),
    )(dh, wdt, a, *([a] * nh), conv_w)


def _sc_in_fused(x, meta, g, w, conv_w, tm, name, rider=_NO_RIDER):
    d = x.shape[1]
    t = x.shape[0] + meta.shape[0]
    cw = min(STRIP, d)
    kw = conv_w.shape[0]
    nh = kw - 1
    nt = t // tm

    def body(x_ref, meta_ref, g_ref, w_ref, cw_ref, *rest):
        r_in, (h_ref, n_ref, p_ref, q_ref), r_out, (carry, tbuf, slabs, sems), r_scratch = _split_refs(rider, 4, 4, rest)
        i = pl.program_id(0)
        rider(i == 0, i == _hand_on_step(nt), None, r_in, r_out, r_scratch)

        @pl.when(i == 0)
        def _():
            carry[...] = jnp.zeros_like(carry)

        _fetch_time_tile(i, nt, x_ref, meta_ref[...], tbuf, sems, tm)
        _time_to_tile_order(tbuf.at[i % 2], slabs, h_ref)
        _rmsnorm_to(h_ref, g_ref, n_ref)
        for c in range(0, d, cw):
            sl = slice(c, c + cw)
            parts = []
            for base in (0, d, 2 * d):
                psl = slice(base + c, base + c + cw)
                parts.append(_dot(n_ref[...], w_ref[:, psl], _NN))
                p_ref[:, psl] = parts[-1]
            bg, cg, v = parts
            cv = cg * v
            u = _weighted(_rows(cw_ref, sl), _windows(cv, _blocks(carry, sl, nh, True), kw))
            carry[:, sl] = cv[tm - nh * SUBLANES:tm]
            q_ref[:, sl] = (bg * u).astype(q_ref.dtype)
        rider(None, None, i == nt - 1, r_in, r_out, r_scratch)

    row_tile = lambda cols: pl.BlockSpec((tm, cols), lambda i: (i, 0))
    return pl.pallas_call(
        body, name=name, grid=(nt,),
        in_specs=[_ANY, _full(meta.shape), pl.BlockSpec((1, d), lambda i: (0, 0)),
                  _resident((d, 3 * d)), _full(conv_w.shape)] + list(rider.in_specs),
        out_specs=[row_tile(d), row_tile(d), row_tile(3 * d), row_tile(d)] + list(rider.out_specs),
        out_shape=[jax.ShapeDtypeStruct((t, d), F32), jax.ShapeDtypeStruct((t, d), MXU_DTYPE),
                   jax.ShapeDtypeStruct((t, 3 * d), F32), jax.ShapeDtypeStruct((t, d), MXU_DTYPE)]
                  + list(rider.out_shape),
        scratch_shapes=[pltpu.VMEM((nh * SUBLANES, d), F32)] + _time_scratch(tm, d) + list(rider.scratch_shapes),
        compiler_params=_params("arbitrary"),
    )(x, meta, g, w, conv_w, *rider.inputs)


def _sc_out_bwd_fused(dh, wot, p, conv_w, tm, name, rider=_NO_RIDER):
    t, d = dh.shape
    nt = t // tm
    cw = min(STRIP, d)
    kw = conv_w.shape[0]
    nh = kw - 1
    rev = lambda i: nt - 1 - i

    def body(dh_ref, wot_ref, p_ref, *rest):
        halo_refs, cw_ref = rest[:nh], rest[nh]
        r_in, (dp_ref, dhb_ref, dw_ref), r_out, (carry,), r_scratch = _split_refs(rider, 3, 1, rest[nh + 1:])
        i = pl.program_id(0)
        newest, oldest = i == 0, i == nt - 1
        rider(newest, i == nt // 2, None, r_in, r_out, r_scratch)

        @pl.when(newest)
        def _():
            carry[...] = jnp.zeros_like(carry)
            dw_ref[...] = jnp.zeros_like(dw_ref)

        dhb_ref[...] = dh_ref[...].astype(dhb_ref.dtype)
        for c in range(0, d, cw):
            sl, csl, vsl = slice(c, c + cw), slice(d + c, d + c + cw), slice(2 * d + c, 2 * d + c + cw)
            w = _rows(cw_ref, sl)
            dq = _dot(dhb_ref[...], wot_ref[:, sl], _NN)
            bg, cg, v = p_ref[:, sl], p_ref[:, csl], p_ref[:, vsl]
            before = [jnp.where(oldest, 0.0, r[:, csl] * r[:, vsl]) for r in halo_refs]
            win = _windows(cg * v, before, kw)
            du = dq * bg
            dcv = _conv_t(du, _blocks(carry, sl, nh, False), w)
            dp_ref[:, sl] = (dq * _weighted(w, win)).astype(dp_ref.dtype)
            dp_ref[:, csl] = (dcv * v).astype(dp_ref.dtype)
            dp_ref[:, vsl] = (dcv * cg).astype(dp_ref.dtype)
            for k in range(kw):
                dw_ref[k:k + 1, sl] += jnp.sum(du * win[k], axis=0, keepdims=True)
            carry[:, sl] = du[0:nh * SUBLANES]
        rider(None, None, oldest, r_in, r_out, r_scratch)

    tile = lambda cols: pl.BlockSpec((tm, cols), lambda i: (rev(i), 0))
    return pl.pallas_call(
        body, name=name, grid=(nt,),
        in_specs=[tile(d), _resident((d, d)), tile(3 * d)] + _halo_specs(tm, 3 * d, nh, rev) + [_full(conv_w.shape)]
                 + list(rider.in_specs),
        out_specs=[tile(3 * d), tile(d), _full(conv_w.shape)] + list(rider.out_specs),
        out_shape=[jax.ShapeDtypeStruct((t, 3 * d), MXU_DTYPE), jax.ShapeDtypeStruct((t, d), MXU_DTYPE),
                   jax.ShapeDtypeStruct(conv_w.shape, F32)] + list(rider.out_shape),
        scratch_shapes=[pltpu.VMEM((nh * SUBLANES, d), F32)] + list(rider.scratch_shapes),
        compiler_params=_params("arbitrary"),
    )(dh, wot, p, *([p] * nh), conv_w, *rider.inputs)


def _pair_gate_weights(wa, wx):
    nb, bd, _ = wa.shape
    zero = jnp.zeros((nb // 2, bd, bd), wa.dtype)

    def pair(w):
        w = w.reshape(nb // 2, 2, bd, bd)
        top = jnp.concatenate([w[:, 0], zero], axis=2)
        bottom = jnp.concatenate([zero, w[:, 1]], axis=2)
        return jnp.concatenate([top, bottom], axis=1)

    both = jnp.concatenate([pair(wa), pair(wx)], axis=2)
    return both, jnp.swapaxes(both, 1, 2)


def _unpair_gate_grads(dw, bd):
    def blocks(cols):
        d0 = dw[:, :bd, cols:cols + bd]
        d1 = dw[:, bd:, cols + bd:cols + 2 * bd]
        return jnp.stack([d0, d1], axis=1).reshape(-1, bd, bd)
    return blocks(0), blocks(2 * bd)


def _rg_gates(u, k, wg_ref, ba, bx, lam):
    ub = u.astype(MXU_DTYPE)
    z = _dot(ub, wg_ref[k], _NN)
    half = z.shape[1] // 2
    r = _sigmoid_tanh(z[:, :half] + ba)
    ig = _sigmoid_tanh(z[:, half:] + bx)
    sp = _softplus(-lam)
    la = -RG_C * r * sp
    a = jnp.exp(la)
    th = jnp.tanh(la)
    mult = jnp.sqrt(-2.0 * th / (1.0 - th))
    return r, ig, a, mult


def _rg_in_fused(h, g, w, conv_w, conv_b, wg, ba, bx, lam, tm, name, rider=_NO_RIDER):
    t, d = h.shape
    r2 = w.shape[1]
    rr = r2 // 2
    nb, bd = wg.shape[0], wg.shape[1]
    kw = conv_w.shape[0]
    nh = kw - 1
    groups = tm // SUBLANES
    nt = t // tm

    def body(h_ref, g_ref, w_ref, cw_ref, cb_ref, wg_ref, ba_ref, bx_ref, lam_ref, *rest):
        r_in, (n_ref, p_ref, y_ref, hs_ref), r_out, (a_scr, b_scr, carry_rb, carry_h), r_scratch = _split_refs(
            rider, 4, 4, rest)
        i = pl.program_id(0)
        rider(i == 0, i == _hand_on_step(nt), None, r_in, r_out, r_scratch)

        @pl.when(i == 0)
        def _():
            carry_rb[...] = jnp.zeros_like(carry_rb)
            carry_h[...] = jnp.zeros_like(carry_h)

        _rmsnorm_to(h_ref, g_ref, n_ref)
        for k in range(nb):
            sl = slice(k * bd, (k + 1) * bd)
            rsl = slice(rr + k * bd, rr + (k + 1) * bd)
            p_ref[:, sl] = _dot(n_ref[...], w_ref[:, sl], _NN)
            rb = _dot(n_ref[...], w_ref[:, rsl], _NN)
            p_ref[:, rsl] = rb
            u = _weighted(_rows(cw_ref, sl), _windows(rb, _blocks(carry_rb, sl, nh, True), kw)) + cb_ref[:, sl]
            carry_rb[:, sl] = rb[tm - nh * SUBLANES:tm]
            _, ig, a, mult = _rg_gates(u, k, wg_ref, ba_ref[:, sl], bx_ref[:, sl], lam_ref[:, sl])
            a_scr[:, sl] = a
            b_scr[:, sl] = mult * (ig * u)

        carry_h[...] = _scan_tile(a_scr, b_scr, hs_ref, carry_h[...], groups, reverse=False)

        for k in range(nb):
            sl = slice(k * bd, (k + 1) * bd)
            y_ref[:, sl] = (hs_ref[:, sl] * _gelu(p_ref[:, sl])).astype(y_ref.dtype)
        rider(None, None, i == nt - 1, r_in, r_out, r_scratch)

    vm = lambda rows: pltpu.VMEM((rows, rr), F32)
    return pl.pallas_call(
        body, name=name, grid=(nt,),
        in_specs=[pl.BlockSpec((tm, d), lambda i: (i, 0)), pl.BlockSpec((1, d), lambda i: (0, 0)),
                  _resident((d, r2)), _full(conv_w.shape), _full(conv_b.shape), _full(wg.shape), _full(ba.shape),
                  _full(bx.shape), _full(lam.shape)] + list(rider.in_specs),
        out_specs=[pl.BlockSpec((tm, d), lambda i: (i, 0)), pl.BlockSpec((tm, r2), lambda i: (i, 0)),
                   pl.BlockSpec((tm, rr), lambda i: (i, 0)), pl.BlockSpec((tm, rr), lambda i: (i, 0))]
                  + list(rider.out_specs),
        out_shape=[jax.ShapeDtypeStruct((t, d), MXU_DTYPE), jax.ShapeDtypeStruct((t, r2), F32),
                   jax.ShapeDtypeStruct((t, rr), MXU_DTYPE), jax.ShapeDtypeStruct((t, rr), F32)]
                  + list(rider.out_shape),
        scratch_shapes=[vm(tm), vm(tm), vm(nh * SUBLANES), vm(SUBLANES)] + list(rider.scratch_shapes),
        compiler_params=_params("arbitrary"),
    )(h, g, w, conv_w, conv_b, wg, ba, bx, lam, *rider.inputs)


def _rg_out_bwd_fused(dh, wot, p, hs, conv_w, conv_b, wg, wgt, ba, bx, lam, tm, name, rider=_NO_RIDER):
    t, d = dh.shape
    r2 = p.shape[1]
    rr = r2 // 2
    nb, bd = wg.shape[0], wg.shape[1]
    nt = t // tm
    kw = conv_w.shape[0]
    nh = kw - 1
    groups = tm // SUBLANES
    rev = lambda i: nt - 1 - i

    def body(dh_ref, wot_ref, p_ref, hs_ref, hsh_ref, *rest):
        halo_refs = rest[:nh]
        cw_ref, cb_ref, wg_ref, wgt_ref, ba_ref, bx_ref, lam_ref = rest[nh:nh + 7]
        (r_in, (dp_ref, dhb_ref, dcw_ref, dcb_ref, dwg_ref, dba_ref, dbx_ref, dlam_ref), r_out,
         (a_scr, as_scr, u_scr, r_scr, ig_scr, mult_scr, g_scr, carry_g, carry_du), r_scratch) = _split_refs(
            rider, 8, 9, rest[nh + 7:])
        i = pl.program_id(0)
        newest, oldest = i == 0, i == nt - 1
        rider(newest, i == nt // 2, None, r_in, r_out, r_scratch)

        @pl.when(newest)
        def _():
            carry_g[...] = jnp.zeros_like(carry_g)
            carry_du[...] = jnp.zeros_like(carry_du)
            for ref in (dcw_ref, dcb_ref, dwg_ref, dba_ref, dbx_ref, dlam_ref):
                ref[...] = jnp.zeros_like(ref)

        dhb_ref[...] = dh_ref[...].astype(dhb_ref.dtype)
        ones = jnp.ones((SUBLANES, bd), F32)

        def conv_windows(sl, rsl):
            before = [jnp.where(oldest, 0.0, r[:, rsl]) for r in halo_refs]
            return _windows(p_ref[:, rsl], before, kw)

        for k in range(nb):
            sl = slice(k * bd, (k + 1) * bd)
            rsl = slice(rr + k * bd, rr + (k + 1) * bd)
            dy = _dot(dhb_ref[...], wot_ref[:, sl], _NN)
            u = _weighted(_rows(cw_ref, sl), conv_windows(sl, rsl)) + cb_ref[:, sl]
            r, ig, a, mult = _rg_gates(u, k, wg_ref, ba_ref[:, sl], bx_ref[:, sl], lam_ref[:, sl])
            gate, dgate = _gelu_and_grad(p_ref[:, sl])
            g_scr[:, sl] = dy * gate
            dp_ref[:, sl] = (dy * hs_ref[:, sl] * dgate).astype(dp_ref.dtype)
            a_scr[:, sl] = a
            as_scr[:, sl] = _shift_up(a, [ones], 1)
            u_scr[:, sl] = u
            r_scr[:, sl] = r
            ig_scr[:, sl] = ig
            mult_scr[:, sl] = mult

        g_first = _scan_tile(as_scr, g_scr, g_scr, carry_g[...], groups, reverse=True)
        carry_g[...] = jnp.broadcast_to(a_scr[0:1, :], (SUBLANES, rr)) * g_first

        for k in range(nb):
            sl = slice(k * bd, (k + 1) * bd)
            rsl = slice(rr + k * bd, rr + (k + 1) * bd)
            cw = _rows(cw_ref, sl)
            lam_k = lam_ref[:, sl]
            sp = _softplus(-lam_k)
            g = g_scr[:, sl]
            a, u, r, ig, mult = a_scr[:, sl], u_scr[:, sl], r_scr[:, sl], ig_scr[:, sl], mult_scr[:, sl]
            da = g * _shift_down(hs_ref[:, sl], [jnp.where(oldest, 0.0, hsh_ref[:, sl])], 1)
            dmult = g * (ig * u)
            d_iu = g * mult
            dla = da * a - dmult * (a * a) / mult
            dr = dla * (-RG_C * sp)
            dsp = jnp.sum(dla * (-RG_C * r), axis=0, keepdims=True)
            dlam_ref[:, sl] += dsp * (-_sigmoid(-lam_k))
            dza = dr * r * (1.0 - r)
            dzx = (d_iu * u) * ig * (1.0 - ig)
            dba_ref[:, sl] += jnp.sum(dza, axis=0, keepdims=True)
            dbx_ref[:, sl] += jnp.sum(dzx, axis=0, keepdims=True)
            ub = u.astype(MXU_DTYPE)
            dz = jnp.concatenate([dza, dzx], axis=1).astype(MXU_DTYPE)
            dwg_ref[k] += _dot(ub, dz, _TN)
            du = d_iu * ig + _dot(dz, wgt_ref[k], _NN)
            dcb_ref[:, sl] += jnp.sum(du, axis=0, keepdims=True)
            win = conv_windows(sl, rsl)
            for kk in range(kw):
                dcw_ref[kk:kk + 1, sl] += jnp.sum(du * win[kk], axis=0, keepdims=True)
            dp_ref[:, rsl] = _conv_t(du, _blocks(carry_du, sl, nh, False), cw).astype(dp_ref.dtype)
            carry_du[:, sl] = du[0:nh * SUBLANES]
        rider(None, None, oldest, r_in, r_out, r_scratch)

    tile = lambda cols: pl.BlockSpec((tm, cols), lambda i: (rev(i), 0))
    vm = lambda rows: pltpu.VMEM((rows, rr), F32)
    grads = [conv_w.shape, conv_b.shape, wg.shape, ba.shape, bx.shape, lam.shape]
    return pl.pallas_call(
        body, name=name, grid=(nt,),
        in_specs=[tile(d), _resident((d, rr)), tile(r2), tile(rr)] + _halo_specs(tm, rr, 1, rev)
                 + _halo_specs(tm, r2, nh, rev)
                 + [_full(conv_w.shape), _full(conv_b.shape), _full(wg.shape), _full(wgt.shape), _full(ba.shape),
                    _full(bx.shape), _full(lam.shape)] + list(rider.in_specs),
        out_specs=[tile(r2), tile(d)] + [_full(s) for s in grads] + list(rider.out_specs),
        out_shape=[jax.ShapeDtypeStruct((t, r2), MXU_DTYPE), jax.ShapeDtypeStruct((t, d), MXU_DTYPE)]
                  + [jax.ShapeDtypeStruct(s, F32) for s in grads] + list(rider.out_shape),
        scratch_shapes=[vm(tm)] * 7 + [vm(SUBLANES), vm(nh * SUBLANES)] + list(rider.scratch_shapes),
        compiler_params=_params("arbitrary"),
    )(dh, wot, p, hs, hs, *([p] * nh), conv_w, conv_b, wg, wgt, ba, bx, lam, *rider.inputs)


def _down_loss_head(z, w, h, g, target, n_meta, tm, name):
    t, d = h.shape
    k = z.shape[1]
    groups = tm // SUBLANES
    nt = t // tm

    def body(z_ref, w_ref, h_ref, g_ref, t_ref, dh_ref, loss_ref, dg_ref, tgt, tbuf, slabs, sems):
        i = pl.program_id(0)
        _fetch_time_tile(i, nt, t_ref, jnp.zeros((n_meta, d), F32), tbuf, sems, tm)
        _time_to_tile_order(tbuf.at[i % 2], slabs, tgt)
        x = h_ref[...] + _dot(z_ref[...], w_ref[...], _NN)
        ms = jnp.mean(x * x, axis=-1, keepdims=True)
        r = lax.rsqrt(ms + RMS_EPS)
        xhat = x * r
        gg = g_ref[...]
        row = lax.broadcasted_iota(jnp.int32, (tm, 1), 0)
        time = i * tm + jnp.right_shift(row, 3) + jnp.bitwise_and(row, SUBLANES - 1) * groups
        err = jnp.where(time >= n_meta, xhat * gg - tgt[...], 0.0)
        dout = err * (1.0 / d)
        dng = dout * gg
        c = jnp.mean(dng * xhat, axis=-1, keepdims=True)
        dh_ref[...] = r * (dng - xhat * c)
        part_dg = jnp.sum(dout * xhat, axis=0, keepdims=True)
        part_loss = jnp.broadcast_to(0.5 * jnp.sum(err * dout, keepdims=True), loss_ref.shape)

        @pl.when(i == 0)
        def _():
            dg_ref[...] = part_dg
            loss_ref[...] = part_loss

        @pl.when(i > 0)
        def _():
            dg_ref[...] += part_dg
            loss_ref[...] += part_loss

    return pl.pallas_call(
        body, name=name, grid=(nt,),
        in_specs=[pl.BlockSpec((tm, k), lambda i: (i, 0)),
                  _resident((k, d)),
                  pl.BlockSpec((tm, d), lambda i: (i, 0)),
                  pl.BlockSpec((1, d), lambda i: (0, 0)),
                  _ANY],
        out_specs=[pl.BlockSpec((tm, d), lambda i: (i, 0)),
                   pl.BlockSpec((SUBLANES, LANES), lambda i: (0, 0)),
                   pl.BlockSpec((1, d), lambda i: (0, 0))],
        out_shape=[jax.ShapeDtypeStruct((t, d), F32), jax.ShapeDtypeStruct((SUBLANES, LANES), F32),
                   jax.ShapeDtypeStruct((1, d), F32)],
        scratch_shapes=[pltpu.VMEM((tm, d), F32)] + _time_scratch(tm, d),
        compiler_params=_params("arbitrary"),
    )(z, w, h, g, target)


def _adamw(w, g, m, v, name):
    rows, cols = w.shape
    tr = rows
    if rows > 512:
        for cand in range(8, 513, 8):
            if rows % cand == 0:
                tr = cand

    def body(w_ref, g_ref, m_ref, v_ref, d_ref, nm_ref, nv_ref):
        g_ = g_ref[...]
        m_ = ADAM_B1 * m_ref[...] + (1.0 - ADAM_B1) * g_
        v_ = ADAM_B2 * v_ref[...] + (1.0 - ADAM_B2) * (g_ * g_)
        m_hat = m_ / (1.0 - ADAM_B1 ** ADAM_STEP)
        v_hat = v_ / (1.0 - ADAM_B2 ** ADAM_STEP)
        d_ref[...] = -ADAM_LR * (m_hat / (jnp.sqrt(v_hat) + ADAM_EPS) + ADAM_WD * w_ref[...])
        nm_ref[...] = m_
        nv_ref[...] = v_

    spec = pl.BlockSpec((tr, cols), lambda i: (i, 0))
    shape = jax.ShapeDtypeStruct((rows, cols), F32)
    return pl.pallas_call(
        body, name=name, grid=(rows // tr,),
        in_specs=[spec] * 4, out_specs=[spec] * 3, out_shape=[shape] * 3,
        compiler_params=_params("parallel"),
    )(w, g, m, v)


def _adamw_nd(w, g, m, v, name):
    shape = w.shape
    two_d = (-1, shape[-1]) if w.ndim > 1 else (1, -1)
    outs = _adamw(w.reshape(two_d), g.reshape(two_d), m.reshape(two_d), v.reshape(two_d), name)
    return tuple(o.reshape(shape) for o in outs)


def _all_gather(blk, name):
    rows, cols = blk.shape

    def body(x_ref, out_ref, send_sems, recv_sems, local_sem):
        x, y, c = lax.axis_index("x"), lax.axis_index("y"), lax.axis_index("c")
        me, sibling = (x, y, c), (x, y, 1 - c)
        chips = [(1 - x, y), (x, 1 - y), (1 - x, 1 - y)]

        def slot(px, py, pc):
            return out_ref.at[4 * px + 2 * py + pc]

        def copy(k, block, to, src=None):
            return pltpu.make_async_remote_copy(
                src_ref=slot(*block) if src is None else src, dst_ref=slot(*block),
                send_sem=send_sems.at[k], recv_sem=recv_sems.at[k], device_id=to, device_id_type=_MESH)

        mine = pltpu.make_async_copy(x_ref, slot(*me), local_sem)
        mine.start()
        first = [copy(0, me, sibling, src=x_ref)]
        first += [copy(1 + j, me, (*chip, c), src=x_ref) for j, chip in enumerate(chips)]
        for cp in first:
            cp.start()
        passed = [copy(4 + j, (*chip, c), sibling) for j, chip in enumerate(chips)]
        for j, chip in enumerate(chips):
            copy(1 + j, (*chip, c), me).wait_recv()
            passed[j].start()
        copy(0, sibling, me).wait_recv()
        for j, chip in enumerate(chips):
            copy(4 + j, (*chip, 1 - c), me).wait_recv()
        for cp in first + passed:
            cp.wait_send()
        mine.wait()

    return pl.pallas_call(
        body, name=name,
        in_specs=[_ANY], out_specs=_ANY,
        out_shape=jax.ShapeDtypeStruct((N_DEV, rows, cols), blk.dtype),
        scratch_shapes=[pltpu.SemaphoreType.DMA((7,)), pltpu.SemaphoreType.DMA((7,)), pltpu.SemaphoreType.DMA(())],
    )(blk)


def _rs_sibling_exchange(buf, name):
    _, rows, cols = buf.shape

    def body(buf_ref, out_ref, send_sems, recv_sems):
        x, y, c = lax.axis_index("x"), lax.axis_index("y"), lax.axis_index("c")
        copies = [pltpu.make_async_remote_copy(
            src_ref=buf_ref.at[2 * k + (1 - c)], dst_ref=out_ref.at[k],
            send_sem=send_sems.at[k], recv_sem=recv_sems.at[k],
            device_id=(x, y, 1 - c), device_id_type=_MESH) for k in range(4)]
        for cp in copies:
            cp.start()
        for cp in copies:
            cp.wait()

    return pl.pallas_call(
        body, name=name,
        in_specs=[_ANY], out_specs=_ANY,
        out_shape=jax.ShapeDtypeStruct((4, rows, cols), buf.dtype),
        scratch_shapes=[pltpu.SemaphoreType.DMA((4,)), pltpu.SemaphoreType.DMA((4,))],
    )(buf)


def _rs_chip_sum(buf, recv, core, name):
    _, rows, cols = buf.shape
    tr = _row_tile(rows, 512) if rows % 16 == 0 else rows

    def body(c_ref, a_ref, b_ref, o_ref):
        o_ref[...] = a_ref[...] + b_ref[...]

    grid_spec = pltpu.PrefetchScalarGridSpec(
        num_scalar_prefetch=1, grid=(4, rows // tr),
        in_specs=[pl.BlockSpec((None, tr, cols), lambda k, j, c: (2 * k + c[0], j, 0)),
                  pl.BlockSpec((None, tr, cols), lambda k, j, c: (k, j, 0))],
        out_specs=pl.BlockSpec((None, tr, cols), lambda k, j, c: (k, j, 0)))
    return pl.pallas_call(
        body, name=name, grid_spec=grid_spec,
        out_shape=jax.ShapeDtypeStruct((4, rows, cols), buf.dtype),
        compiler_params=_params("parallel", "parallel"),
    )(core, buf, recv)


def _rs_chip_exchange(chip_sum, name):
    _, rows, cols = chip_sum.shape

    def body(cs_ref, out_ref, send_sems, recv_sems, local_sem):
        x, y, c = lax.axis_index("x"), lax.axis_index("y"), lax.axis_index("c")
        my_chip = 2 * x + y
        mine = pltpu.make_async_copy(cs_ref.at[my_chip], out_ref.at[my_chip], local_sem)
        mine.start()
        chips = [(1 - x, y), (x, 1 - y), (1 - x, 1 - y)]
        copies = [pltpu.make_async_remote_copy(
            src_ref=cs_ref.at[2 * px + py], dst_ref=out_ref.at[my_chip],
            send_sem=send_sems.at[j], recv_sem=recv_sems.at[j],
            device_id=(px, py, c), device_id_type=_MESH) for j, (px, py) in enumerate(chips)]
        for cp in copies:
            cp.start()
        for cp in copies:
            cp.wait()
        mine.wait()

    return pl.pallas_call(
        body, name=name,
        in_specs=[_ANY], out_specs=_ANY,
        out_shape=jax.ShapeDtypeStruct((4, rows, cols), chip_sum.dtype),
        scratch_shapes=[pltpu.SemaphoreType.DMA((3,)), pltpu.SemaphoreType.DMA((3,)), pltpu.SemaphoreType.DMA(())],
    )(chip_sum)


def _sum_slots(parts, name):
    slots, rows, cols = parts.shape
    tr = _row_tile(rows, 256) if rows % 16 == 0 else rows

    def body(p_ref, o_ref):
        acc = p_ref[0]
        for s in range(1, slots):
            acc = acc + p_ref[s]
        o_ref[...] = acc

    return pl.pallas_call(
        body, name=name, grid=(rows // tr,),
        in_specs=[pl.BlockSpec((slots, tr, cols), lambda i: (0, i, 0))],
        out_specs=pl.BlockSpec((tr, cols), lambda i: (i, 0)),
        out_shape=jax.ShapeDtypeStruct((rows, cols), parts.dtype),
        compiler_params=_params("parallel"),
    )(parts)


def _reduce_scatter(buf, core):
    recv = _rs_sibling_exchange(buf, "rs_sibling_exchange")
    chip_sum = _rs_chip_sum(buf, recv, core, "rs_chip_sum")
    parts = _rs_chip_exchange(chip_sum, "rs_chip_exchange")
    return _sum_slots(parts, "rs_sum_slots")


def _pad_rows(flat, cols, multiple):
    n = flat.shape[0]
    rows = -(-n // cols)
    rows = -(-rows // multiple) * multiple
    return jnp.pad(flat, (0, rows * cols - n)).reshape(rows, cols)


def _cols_to_chunks(full):
    lead = full.shape[:-1]
    c = full.shape[-1] // N_DEV
    x = full.reshape(-1, N_DEV, c)
    return jnp.transpose(x, (1, 0, 2)).reshape(N_DEV, -1)


def _chunks_to_cols(chunks, lead):
    n = 1
    for s in lead:
        n *= s
    c = chunks.shape[1] // n
    x = chunks.reshape(N_DEV, n, c)
    return jnp.transpose(x, (1, 0, 2)).reshape(tuple(lead) + (N_DEV * c,))


def kernel(x, meta_tokens, norm_mix_g, norm_ffn_g, final_norm_g, sc_w_in, sc_conv_w, sc_w_out, rg_w_in, rg_conv_w, rg_conv_b, rg_w_gate_a, rg_b_gate_a, rg_w_gate_x, rg_b_gate_x, rg_lambda, rg_w_out, ffn_w_up, ffn_conv_w, ffn_w_down, loss_target, m_meta_tokens, m_norm_mix_g, m_norm_ffn_g, m_final_norm_g, m_sc_w_in, m_sc_conv_w, m_sc_w_out, m_rg_w_in, m_rg_conv_w, m_rg_conv_b, m_rg_w_gate_a, m_rg_b_gate_a, m_rg_w_gate_x, m_rg_b_gate_x, m_rg_lambda, m_rg_w_out, m_ffn_w_up, m_ffn_conv_w, m_ffn_w_down, v_meta_tokens, v_norm_mix_g, v_norm_ffn_g, v_final_norm_g, v_sc_w_in, v_sc_conv_w, v_sc_w_out, v_rg_w_in, v_rg_conv_w, v_rg_conv_b, v_rg_w_gate_a, v_rg_b_gate_a, v_rg_w_gate_x, v_rg_b_gate_x, v_rg_lambda, v_rg_w_out, v_ffn_w_up, v_ffn_conv_w, v_ffn_w_down):
    weights = dict(meta_tokens=meta_tokens, norm_mix_g=norm_mix_g, norm_ffn_g=norm_ffn_g, final_norm_g=final_norm_g,
                   sc_w_in=sc_w_in, sc_conv_w=sc_conv_w, sc_w_out=sc_w_out, rg_w_in=rg_w_in, rg_conv_w=rg_conv_w,
                   rg_conv_b=rg_conv_b, rg_w_gate_a=rg_w_gate_a, rg_b_gate_a=rg_b_gate_a, rg_w_gate_x=rg_w_gate_x,
                   rg_b_gate_x=rg_b_gate_x, rg_lambda=rg_lambda, rg_w_out=rg_w_out, ffn_w_up=ffn_w_up,
                   ffn_conv_w=ffn_conv_w, ffn_w_down=ffn_w_down)
    m_in = dict(meta_tokens=m_meta_tokens, norm_mix_g=m_norm_mix_g, norm_ffn_g=m_norm_ffn_g, final_norm_g=m_final_norm_g,
                sc_w_in=m_sc_w_in, sc_conv_w=m_sc_conv_w, sc_w_out=m_sc_w_out, rg_w_in=m_rg_w_in, rg_conv_w=m_rg_conv_w,
                rg_conv_b=m_rg_conv_b, rg_w_gate_a=m_rg_w_gate_a, rg_b_gate_a=m_rg_b_gate_a, rg_w_gate_x=m_rg_w_gate_x,
                rg_b_gate_x=m_rg_b_gate_x, rg_lambda=m_rg_lambda, rg_w_out=m_rg_w_out, ffn_w_up=m_ffn_w_up,
                ffn_conv_w=m_ffn_conv_w, ffn_w_down=m_ffn_w_down)
    v_in = dict(meta_tokens=v_meta_tokens, norm_mix_g=v_norm_mix_g, norm_ffn_g=v_norm_ffn_g, final_norm_g=v_final_norm_g,
                sc_w_in=v_sc_w_in, sc_conv_w=v_sc_conv_w, sc_w_out=v_sc_w_out, rg_w_in=v_rg_w_in, rg_conv_w=v_rg_conv_w,
                rg_conv_b=v_rg_conv_b, rg_w_gate_a=v_rg_w_gate_a, rg_b_gate_a=v_rg_b_gate_a, rg_w_gate_x=v_rg_w_gate_x,
                rg_b_gate_x=v_rg_b_gate_x, rg_lambda=v_rg_lambda, rg_w_out=v_rg_w_out, ffn_w_up=v_ffn_w_up,
                ffn_conv_w=v_ffn_conv_w, ffn_w_down=v_ffn_w_down)
    names = list(weights)

    seq, d = x.shape[1], x.shape[2]
    n_meta = meta_tokens.shape[0]
    n_ffn = ffn_w_up.shape[0]
    rr = rg_w_gate_a.shape[1] * rg_w_gate_a.shape[2]
    core = lax.axis_index("c").astype(jnp.int32).reshape(1)

    assert n_ffn == 2
    def shard_rows(w_in, w_out):
        return [w_in.T.astype(MXU_DTYPE), w_out.astype(MXU_DTYPE)]

    def both_orientations(in_t_gathered, out_gathered):
        w_in_t = in_t_gathered.reshape(-1, d)
        w_out = out_gathered.reshape(-1, d)
        return w_in_t.T, w_in_t, w_out, w_out.T

    shards = {"ffn0": shard_rows(ffn_w_up[0], ffn_w_down[0]), "rg": shard_rows(rg_w_in[0], rg_w_out[0]),
              "ffn1": shard_rows(ffn_w_up[1], ffn_w_down[1])}
    sc_rows = sc_w_in.shape[2]
    sc_gathered = _all_gather(jnp.concatenate(shard_rows(sc_w_in[0], sc_w_out[0]), axis=0), "gather_sc")
    sc_w_in_f, sc_w_in_t, sc_w_out_f, sc_w_out_t = both_orientations(sc_gathered[:, :sc_rows], sc_gathered[:, sc_rows:])
    ffn_w_up_f, ffn_w_up_t, ffn_w_down_f, ffn_w_down_t = [None] * 2, [None] * 2, [None] * 2, [None] * 2

    small_names = ["meta_tokens", "sc_conv_w", "rg_conv_w", "rg_conv_b", "rg_b_gate_a", "rg_b_gate_x", "rg_lambda",
                   "ffn_conv_w"]
    small_lead = {n: weights[n].shape[:-1] for n in small_names}
    small_sizes = [weights[n].size for n in small_names]
    small_flat = jnp.concatenate([weights[n].reshape(-1) for n in small_names])
    small_rows = _pad_rows(small_flat, d, SUBLANES)
    small_g = _all_gather(small_rows, "gather_small").reshape(N_DEV, -1)
    small_full = {}
    o = 0
    for n, sz in zip(small_names, small_sizes):
        small_full[n] = _chunks_to_cols(small_g[:, o:o + sz], small_lead[n])
        o += sz

    wg, wgt = _pair_gate_weights(rg_w_gate_a[0].astype(MXU_DTYPE), rg_w_gate_x[0].astype(MXU_DTYPE))
    rg_cw, rg_cb = small_full["rg_conv_w"][0], small_full["rg_conv_b"]
    rg_ba, rg_bx, rg_lam = small_full["rg_b_gate_a"], small_full["rg_b_gate_x"], small_full["rg_lambda"]
    sc_cw = small_full["sc_conv_w"][0]
    ffn_cw = small_full["ffn_conv_w"]

    tp = _row_tile(n_meta + seq, ROW_TILE_PERM)

    def ffn_fwd(h, l, rider):
        n, a, z, *gathered = _ffn_up_fused(h, norm_ffn_g[l:l + 1], ffn_w_up_f[l], ffn_cw[l], tp, f"ffn{l}_up", rider)
        return _matmul_residual(z, ffn_w_down_f[l], h, f"ffn{l}_down"), (n, a, z), gathered

    h0, n0, p0, q0, *ffn0_w = _sc_in_fused(x[0], small_full["meta_tokens"], norm_mix_g[0:1], sc_w_in_f, sc_cw, tp, "sc_in",
                                           _GatherRider(shards["ffn0"]))
    ffn_w_up_f[0], ffn_w_up_t[0], ffn_w_down_f[0], ffn_w_down_t[0] = both_orientations(*ffn0_w)
    h1 = _matmul_residual(q0, sc_w_out_f, h0, "sc_out")
    h2, ffn0_saved, rg_w = ffn_fwd(h1, 0, _GatherRider(shards["rg"]))
    rg_w_in_f, rg_w_in_t, rg_w_out_f, rg_w_out_t = both_orientations(*rg_w)
    n2, p2, y2, hs2, *ffn1_w = _rg_in_fused(h2, norm_mix_g[1:2], rg_w_in_f, rg_cw, rg_cb, wg, rg_ba, rg_bx, rg_lam,
                                            tp, "rg_in", _GatherRider(shards["ffn1"]))
    ffn_w_up_f[1], ffn_w_up_t[1], ffn_w_down_f[1], ffn_w_down_t[1] = both_orientations(*ffn1_w)
    h3 = _matmul_residual(y2, rg_w_out_f, h2, "rg_out")
    ffn1_saved = tuple(_ffn_up_fused(h3, norm_ffn_g[1:2], ffn_w_up_f[1], ffn_cw[1], tp, "ffn1_up"))
    dh4, loss_tile, d_final_g = _down_loss_head(ffn1_saved[2], ffn_w_down_f[1], h3, final_norm_g.reshape(1, d),
                                                loss_target[0], n_meta, tp, "ffn1_down_loss_head")
    loss = lax.psum(loss_tile[0, 0], AXES)

    arrived = {}

    def ffn_bwd(dh_out, h_in, saved, l):
        n, a, z = saved
        da, dhb, d_cw = _ffn_down_bwd_fused(dh_out, ffn_w_down_t[l], a, ffn_cw[l], tp, f"ffn{l}_down_bwd")
        d_w_down = _wgrad(z, dhb, f"ffn{l}_down_wgrad")
        d_w_up_t, arrived[f"ffn_w_down{l}"] = _wgrad(da, n, f"ffn{l}_up_wgrad", _ScatterRider(d_w_down))
        rows = d_w_up_t.shape[0] // N_DEV
        first_rows = (rows // 2) // (2 * SUBLANES) * (2 * SUBLANES)
        dh_in, d_g, arrived[f"ffn_w_up{l}a"] = _dgrad_in_norm(
            da, ffn_w_up_t[l], h_in, norm_ffn_g[l:l + 1], dh_out, f"ffn{l}_up_dgrad",
            _ScatterRider(d_w_up_t, (0, first_rows)))
        return dh_in, d_cw, d_g, _ScatterRider(d_w_up_t, (first_rows, rows - first_rows))

    dh3, d_fcw1, d_fg1, up1_rest = ffn_bwd(dh4, h3, ffn1_saved, 1)

    dp2, dhb3, d_rg_cw, d_rg_cb, d_wg, d_rg_ba, d_rg_bx, d_rg_lam, arrived["ffn_w_up1b"] = _rg_out_bwd_fused(
        dh3, rg_w_out_t, p2, hs2, rg_cw, rg_cb, wg, wgt, rg_ba, rg_bx, rg_lam, tp, "rg_out_bwd", up1_rest)
    d_wa, d_wx = _unpair_gate_grads(d_wg, rg_w_gate_a.shape[2])
    d_rg_w_out = _wgrad(y2, dhb3, "rg_out_wgrad")
    d_rg_w_in_t, arrived["rg_w_out"] = _wgrad(dp2, n2, "rg_in_wgrad", _ScatterRider(d_rg_w_out))
    dh2, d_mg1, arrived["rg_w_in"] = _dgrad_in_norm(
        dp2, rg_w_in_t, h2, norm_mix_g[1:2], dh3, "rg_in_dgrad", _ScatterRider(d_rg_w_in_t))

    dh1, d_fcw0, d_fg0, up0_rest = ffn_bwd(dh2, h1, ffn0_saved, 0)

    dp0, dhb1, d_sc_cw, arrived["ffn_w_up0b"] = _sc_out_bwd_fused(dh1, sc_w_out_t, p0, sc_cw, tp, "sc_out_bwd", up0_rest)
    d_sc_w_out = _wgrad(q0, dhb1, "sc_out_wgrad")
    d_sc_w_in_t, arrived["sc_w_out"] = _wgrad(dp0, n0, "sc_in_wgrad", _ScatterRider(d_sc_w_out))
    d_mg0, d_meta, d_x, arrived["sc_w_in"] = _dgrad_in_to_input(
        dp0, sc_w_in_t, h0, norm_mix_g[0:1], dh1, n_meta, tp, "sc_in_dgrad", _ScatterRider(d_sc_w_in_t))
    grad_x = d_x[None]

    grads = {}
    for n in ("sc_w_in", "rg_w_in"):
        grads[n] = _sum_slots(arrived[n], f"sum_{n}").T[None]
    for n in ("sc_w_out", "rg_w_out"):
        grads[n] = _sum_slots(arrived[n], f"sum_{n}")[None]
    grads["ffn_w_up"] = jnp.stack([jnp.concatenate(
        [_sum_slots(arrived[f"ffn_w_up{l}{part}"], f"sum_ffn_w_up{l}{part}") for part in "ab"], axis=0).T
        for l in range(n_ffn)])
    grads["ffn_w_down"] = jnp.stack([_sum_slots(arrived[f"ffn_w_down{l}"], f"sum_ffn_w_down{l}") for l in range(n_ffn)])

    small_grads = {"meta_tokens": d_meta,"sc_conv_w": d_sc_cw[None], "rg_conv_w": d_rg_cw[None],
                   "rg_conv_b": d_rg_cb, "rg_b_gate_a": d_rg_ba, "rg_b_gate_x": d_rg_bx, "rg_lambda": d_rg_lam,
                   "ffn_conv_w": jnp.stack([d_fcw0, d_fcw1])}
    small_chunks = jnp.concatenate([_cols_to_chunks(small_grads[n]) for n in small_names], axis=1)
    pad = small_rows.shape[0] * d - small_chunks.shape[1]
    small_chunks = jnp.pad(small_chunks, ((0, 0), (0, pad))).reshape(N_DEV, small_rows.shape[0], d)
    rep_names = ["norm_mix_g", "norm_ffn_g", "final_norm_g", "rg_w_gate_a", "rg_w_gate_x"]
    rep_grads = {"norm_mix_g": jnp.concatenate([d_mg0, d_mg1], axis=0),
                 "norm_ffn_g": jnp.concatenate([d_fg0, d_fg1], axis=0),
                 "final_norm_g": d_final_g.reshape(-1), "rg_w_gate_a": d_wa[None], "rg_w_gate_x": d_wx[None]}
    rep_flat = jnp.concatenate([rep_grads[n].reshape(-1) for n in rep_names])
    rep_chunk_rows = -(-rep_flat.shape[0] // (N_DEV * d))
    rep_chunk_rows += -(small_rows.shape[0] + rep_chunk_rows) % 16
    rep_chunks = jnp.pad(rep_flat, (0, N_DEV * rep_chunk_rows * d - rep_flat.shape[0])).reshape(N_DEV, rep_chunk_rows, d)

    reduced = _reduce_scatter(jnp.concatenate([small_chunks, rep_chunks], axis=1), core)

    small_red = reduced[0:small_rows.shape[0]].reshape(-1)
    o = small_rows.shape[0]
    so = 0
    for n, sz in zip(small_names, small_sizes):
        grads[n] = small_red[so:so + sz].reshape(weights[n].shape)
        so += sz
    rep_red = _all_gather(reduced[o:o + rep_chunks.shape[1]], "gather_replicated_grads").reshape(-1)
    ro = 0
    for n in rep_names:
        sz = weights[n].size
        grads[n] = rep_red[ro:ro + sz].reshape(weights[n].shape)
        ro += sz

    delta, new_m, new_v = {}, {}, {}
    for n in names:
        delta[n], new_m[n], new_v[n] = _adamw_nd(weights[n], grads[n], m_in[n], v_in[n], f"adamw_{n}")

    return (loss, grad_x, *[grads[n] for n in names], *[delta[n] for n in names],
            *[new_m[n] for n in names], *[new_v[n] for n in names])
```

```python
import jax
import jax.numpy as jnp
from jax import lax
from jax.experimental import pallas as pl
from jax.experimental.pallas import tpu as pltpu

F32 = jnp.float32
MXU_DTYPE = jnp.bfloat16
RMS_EPS = 1e-6
RG_C = 8.0
ADAM_LR = 0.001
ADAM_B1 = 0.9
ADAM_B2 = 0.999
ADAM_EPS = 1e-08
ADAM_WD = 0.01
ADAM_STEP = 10

N_DEV = 8
AXES = ("x", "y", "c")
SUBLANES = 8
LANES = 128
VMEM_LIMIT_BYTES = 48 * 1024 * 1024
ROW_TILE_MATMUL = 700
ROW_TILE_WGRAD = 3300
ROW_TILE_PERM = 400
STRIP = 256

_TN = (((0,), (0,)), ((), ()))
_NN = (((1,), (0,)), ((), ()))


def _row_tile(t, target):
    best = None
    for tm in range(16, t + 1, 16):
        if t % tm == 0 and tm <= target:
            best = tm
    return best if best is not None else t


def _col_tile(n, target):
    best = None
    for tn in range(LANES, n + 1, LANES):
        if n % tn == 0 and tn <= target:
            best = tn
    return best if best is not None else n


def _params(*sem):
    return pltpu.CompilerParams(dimension_semantics=sem, vmem_limit_bytes=VMEM_LIMIT_BYTES)


def _dot(a, b, dims):
    return lax.dot_general(a, b, dims, preferred_element_type=F32)


def _sigmoid(x):
    return 1.0 / (1.0 + jnp.exp(-x))


def _sigmoid_tanh(x):
    return 0.5 * jnp.tanh(0.5 * x) + 0.5


def _gelu(x):
    c = 0.7978845608028654
    t = jnp.tanh(c * (x + 0.044715 * (x * x * x)))
    return 0.5 * x * (1.0 + t)


def _gelu_and_grad(x):
    c = 0.7978845608028654
    x2 = x * x
    t = jnp.tanh(c * (x + 0.044715 * (x2 * x)))
    half = 0.5 * (1.0 + t)
    return x * half, half + 0.5 * x * (1.0 - t * t) * c * (1.0 + 3.0 * 0.044715 * x2)


def _softplus(x):
    return jnp.maximum(x, 0.0) + jnp.log1p(jnp.exp(-jnp.abs(x)))


def _time_scratch(tm, d):
    return [pltpu.VMEM((2, tm, d), F32), pltpu.VMEM((d // LANES, tm, LANES), F32), pltpu.SemaphoreType.DMA((2,))]


def _fetch_time_tile(i, nt, src_ref, head, tbuf, sems, tm):
    n_head = head.shape[0]

    def tile(j, slot):
        start = pl.multiple_of(j * tm - n_head, SUBLANES)
        return pltpu.make_async_copy(src_ref.at[pl.ds(start, tm)], tbuf.at[slot], sems.at[slot])

    first = pltpu.make_async_copy(src_ref.at[pl.ds(0, tm - n_head)], tbuf.at[0, pl.ds(n_head, tm - n_head)], sems.at[0])

    @pl.when(i == 0)
    def _():
        tbuf[0, 0:n_head, :] = head
        first.start()

    @pl.when(i + 1 < nt)
    def _():
        tile(i + 1, (i + 1) % 2).start()

    @pl.when(i == 0)
    def _():
        first.wait()

    @pl.when(i > 0)
    def _():
        tile(i, i % 2).wait()


def _time_to_tile_order(t_ref, slabs, out_ref):
    tm, d = out_ref.shape
    groups = tm // SUBLANES
    for k in range(d // LANES):
        slabs[k] = t_ref[:, k * LANES:(k + 1) * LANES]
    for k in range(d // LANES):
        for g in range(groups):
            out_ref[g * SUBLANES:(g + 1) * SUBLANES, k * LANES:(k + 1) * LANES] = slabs[k, pl.ds(g, SUBLANES, stride=groups), :]


def _tile_to_time_order(p_ref, slabs, t_ref):
    tm, d = p_ref.shape
    groups = tm // SUBLANES
    for k in range(d // LANES):
        for g in range(groups):
            slabs[k, pl.ds(g, SUBLANES, stride=groups), :] = p_ref[g * SUBLANES:(g + 1) * SUBLANES, k * LANES:(k + 1) * LANES]
    for k in range(d // LANES):
        t_ref[:, k * LANES:(k + 1) * LANES] = slabs[k]


def _rows(ref, sl):
    return [ref[k:k + 1, sl] for k in range(ref.shape[0])]


def _shift_down(x, before, s):
    if s == 0:
        return x
    n = x.shape[0]
    row = lax.broadcasted_iota(jnp.int32, (SUBLANES, x.shape[1]), 0)
    heads = []
    for g in range(s):
        v = x[n - (s - g) * SUBLANES:n - (s - g - 1) * SUBLANES]
        heads.append(pltpu.roll(jnp.where(row == SUBLANES - 1, before[s - g - 1], v), 1, axis=0))
    return jnp.concatenate(heads + [x[0:n - s * SUBLANES]], axis=0)


def _shift_up(x, after, s):
    if s == 0:
        return x
    row = lax.broadcasted_iota(jnp.int32, (SUBLANES, x.shape[1]), 0)
    tails = []
    for m in range(s):
        v = x[m * SUBLANES:(m + 1) * SUBLANES]
        tails.append(pltpu.roll(jnp.where(row == 0, after[m], v), SUBLANES - 1, axis=0))
    return jnp.concatenate([x[s * SUBLANES:]] + tails, axis=0)


def _weighted(w, windows):
    y = w[0] * windows[0]
    for k in range(1, len(w)):
        y = y + w[k] * windows[k]
    return y


def _windows(x, before, k_width):
    return [_shift_down(x, before, k_width - 1 - k) for k in range(k_width)]


def _conv_t(dy, after, w):
    k_width = len(w)
    return _weighted(w, [_shift_up(dy, after, k_width - 1 - k) for k in range(k_width)])


def _blocks(ref, sl, count, newest_first):
    n = ref.shape[0] // SUBLANES
    order = range(n - 1, n - 1 - count, -1) if newest_first else range(count)
    return [ref[b * SUBLANES:(b + 1) * SUBLANES, sl] for b in order]


def _halo_specs(tm, cols, count, tile_of):
    def spec(k):
        return pl.BlockSpec((SUBLANES, cols), lambda i: (jnp.maximum(tile_of(i) * (tm // SUBLANES) - k, 0), 0))
    return [spec(k) for k in range(1, count + 1)]


def _scan_tile(coef, val, out, carry, groups, reverse):
    cols = coef.shape[1]
    row = lax.broadcasted_iota(jnp.int32, (SUBLANES, cols), 0)

    def blk(i):
        g = groups - 1 - i if reverse else i
        return pl.ds(pl.multiple_of(g * SUBLANES, SUBLANES), SUBLANES)

    def local(i, pl_):
        p_prev, l_prev = pl_
        a = coef[blk(i), :]
        p = a * p_prev
        l = a * l_prev + val[blk(i), :]
        coef[blk(i), :] = p
        val[blk(i), :] = l
        return p, l

    pf, lf = lax.fori_loop(0, groups, local, (jnp.ones((SUBLANES, cols), F32), jnp.zeros((SUBLANES, cols), F32)))
    for s in (1, 2, 4):
        keep, sh = (row < SUBLANES - s, SUBLANES - s) if reverse else (row >= s, s)
        p_s = jnp.where(keep, pltpu.roll(pf, sh, axis=0), 1.0)
        l_s = jnp.where(keep, pltpu.roll(lf, sh, axis=0), 0.0)
        lf = pf * l_s + lf
        pf = pf * p_s
    end = lf + pf * carry
    if reverse:
        init = jnp.where(row == SUBLANES - 1, carry, pltpu.roll(end, SUBLANES - 1, axis=0))
        leaving = jnp.broadcast_to(end[0:1, :], (SUBLANES, cols))
    else:
        init = jnp.where(row == 0, carry, pltpu.roll(end, 1, axis=0))
        leaving = jnp.broadcast_to(end[SUBLANES - 1:SUBLANES, :], (SUBLANES, cols))

    def fix(i, _):
        out[blk(i), :] = val[blk(i), :] + coef[blk(i), :] * init
        return 0

    lax.fori_loop(0, groups, fix, 0)
    return leaving


def _resident(shape):
    return pl.BlockSpec(shape, lambda *_: (0,) * len(shape), pipeline_mode=pl.Buffered(1))


def _full(shape):
    return pl.BlockSpec(shape, lambda *_: (0,) * len(shape))


def _rmsnorm_to(h_ref, g_ref, n_ref):
    x = h_ref[...]
    ms = jnp.mean(x * x, axis=-1, keepdims=True)
    n_ref[...] = (x * lax.rsqrt(ms + RMS_EPS) * g_ref[...]).astype(n_ref.dtype)


_ANY = pl.BlockSpec(memory_space=pl.ANY)
_MESH = pl.DeviceIdType.MESH


def _dma_sems(n):
    return pltpu.SemaphoreType.DMA((n,))


def _when_each(*phases):
    for cond, fn in phases:
        if cond is not None:
            pl.when(cond)(fn)


class _NoRider:
    inputs = in_specs = out_shape = out_specs = scratch_shapes = ()

    def __call__(self, first, middle, last, ins, outs, scratch):
        pass


_NO_RIDER = _NoRider()


def _hand_on_step(nt):
    return (3 * nt) // 4


def _split_refs(rider, n_out, n_scratch, rest):
    a = len(rider.inputs)
    b = a + n_out
    c = b + len(rider.out_shape)
    e = c + n_scratch
    return rest[:a], rest[a:b], rest[b:c], rest[c:e], rest[e:]


class _GatherRider:
    def __init__(self, blocks):
        n = len(blocks)
        self.inputs = tuple(blocks)
        self.in_specs = (_ANY,) * n
        self.out_shape = tuple(jax.ShapeDtypeStruct((N_DEV,) + b.shape, b.dtype) for b in blocks)
        self.out_specs = (_ANY,) * n
        self.scratch_shapes = (_dma_sems(7 * n), _dma_sems(7 * n), _dma_sems(n))

    def __call__(self, first, middle, last, ins, outs, scratch):
        n = len(ins)
        send_sems, recv_sems, local_sems = scratch
        x, y, c = lax.axis_index("x"), lax.axis_index("y"), lax.axis_index("c")
        me, sibling = (x, y, c), (x, y, 1 - c)
        chips = [(1 - x, y), (x, 1 - y), (1 - x, 1 - y)]

        def slot(b, px, py, pc):
            return outs[b].at[4 * px + 2 * py + pc]

        def copy(k, b, block, to, src=None):
            return pltpu.make_async_remote_copy(
                src_ref=slot(b, *block) if src is None else src, dst_ref=slot(b, *block),
                send_sem=send_sems.at[k * n + b], recv_sem=recv_sems.at[k * n + b], device_id=to, device_id_type=_MESH)

        mine = [pltpu.make_async_copy(ins[b], slot(b, *me), local_sems.at[b]) for b in range(n)]
        own = [copy(0, b, me, sibling, src=ins[b]) for b in range(n)]
        own += [copy(1 + j, b, me, (*chip, c), src=ins[b]) for j, chip in enumerate(chips) for b in range(n)]
        passed = [[copy(4 + j, b, (*chip, c), sibling) for b in range(n)] for j, chip in enumerate(chips)]

        def at_first():
            for cp in mine + own:
                cp.start()

        def at_middle():
            for j, chip in enumerate(chips):
                for b in range(n):
                    copy(1 + j, b, (*chip, c), me).wait_recv()
                    passed[j][b].start()

        def at_last():
            for b in range(n):
                copy(0, b, sibling, me).wait_recv()
                for j, chip in enumerate(chips):
                    copy(4 + j, b, (*chip, 1 - c), me).wait_recv()
            for cp in own + [cp for group in passed for cp in group]:
                cp.wait_send()
            for cp in mine:
                cp.wait()

        _when_each((first, at_first), (middle, at_middle), (last, at_last))


class _ScatterRider:
    def __init__(self, grad, rows=None):
        r = grad.shape[0] // N_DEV
        chunks = grad.reshape(N_DEV, r, grad.shape[1])
        self.rows = rows if rows is not None else (0, r)
        self.inputs = (chunks,)
        self.in_specs = (_ANY,)
        self.out_shape = (jax.ShapeDtypeStruct((N_DEV, self.rows[1], grad.shape[1]), chunks.dtype),)
        self.out_specs = (_ANY,)
        self.scratch_shapes = (_dma_sems(N_DEV - 1), _dma_sems(N_DEV - 1), pltpu.SemaphoreType.DMA(()))

    def __call__(self, first, middle, last, ins, outs, scratch):
        (g_ref,), (r_ref,) = ins, outs
        send_sems, recv_sems, local_sem = scratch
        x, y, c = lax.axis_index("x"), lax.axis_index("y"), lax.axis_index("c")
        me = 4 * x + 2 * y + c
        part = pl.ds(*self.rows)
        mine = pltpu.make_async_copy(g_ref.at[me, part], r_ref.at[me], local_sem)
        copies = []
        for k in range(1, N_DEV):
            px, py, pc = (1 - x if k & 4 else x), (1 - y if k & 2 else y), (1 - c if k & 1 else c)
            copies.append(pltpu.make_async_remote_copy(
                src_ref=g_ref.at[4 * px + 2 * py + pc, part], dst_ref=r_ref.at[me],
                send_sem=send_sems.at[k - 1], recv_sem=recv_sems.at[k - 1],
                device_id=(px, py, pc), device_id_type=_MESH))

        def at_first():
            mine.start()
            for cp in copies:
                cp.start()

        def at_last():
            for cp in copies:
                cp.wait()
            mine.wait()

        _when_each((first, at_first), (last, at_last))


def _matmul_residual(q, w, h, name):
    t, k = q.shape
    d = w.shape[1]
    tm = _row_tile(t, ROW_TILE_MATMUL)

    def body(q_ref, w_ref, h_ref, o_ref):
        o_ref[...] = h_ref[...] + _dot(q_ref[...], w_ref[...], _NN)

    return pl.pallas_call(
        body, name=name, grid=(t // tm,),
        in_specs=[pl.BlockSpec((tm, k), lambda i: (i, 0)),
                  _resident((k, d)),
                  pl.BlockSpec((tm, d), lambda i: (i, 0))],
        out_specs=pl.BlockSpec((tm, d), lambda i: (i, 0)),
        out_shape=jax.ShapeDtypeStruct((t, d), F32),
        compiler_params=_params("parallel"),
    )(q, w, h)


def _dgrad_in_norm(dp, wt, h, g, dh_next, name, rider=_NO_RIDER):
    t, n = dp.shape
    d = wt.shape[1]
    tm = _row_tile(t, ROW_TILE_MATMUL)
    nt = t // tm

    def body(dp_ref, wt_ref, h_ref, g_ref, dhn_ref, *rest):
        r_in, (dh_ref, dg_ref), r_out, _, r_scratch = _split_refs(rider, 2, 0, rest)
        i = pl.program_id(0)
        rider(i == 0, i == nt // 2, None, r_in, r_out, r_scratch)
        dn = _dot(dp_ref[...], wt_ref[...], _NN)
        x = h_ref[...]
        ms = jnp.mean(x * x, axis=-1, keepdims=True)
        r = lax.rsqrt(ms + RMS_EPS)
        xhat = x * r
        dng = dn * g_ref[...]
        c = jnp.mean(dng * xhat, axis=-1, keepdims=True)
        dh_ref[...] = dhn_ref[...] + r * (dng - xhat * c)
        part = jnp.sum(dn * xhat, axis=0, keepdims=True)

        @pl.when(i == 0)
        def _():
            dg_ref[...] = part

        @pl.when(i > 0)
        def _():
            dg_ref[...] += part

        rider(None, None, i == nt - 1, r_in, r_out, r_scratch)

    return pl.pallas_call(
        body, name=name, grid=(nt,),
        in_specs=[pl.BlockSpec((tm, n), lambda i: (i, 0)),
                  _resident((n, d)),
                  pl.BlockSpec((tm, d), lambda i: (i, 0)),
                  pl.BlockSpec((1, d), lambda i: (0, 0)),
                  pl.BlockSpec((tm, d), lambda i: (i, 0))] + list(rider.in_specs),
        out_specs=[pl.BlockSpec((tm, d), lambda i: (i, 0)),
                   pl.BlockSpec((1, d), lambda i: (0, 0))] + list(rider.out_specs),
        out_shape=[jax.ShapeDtypeStruct((t, d), F32), jax.ShapeDtypeStruct((1, d), F32)] + list(rider.out_shape),
        scratch_shapes=list(rider.scratch_shapes),
        compiler_params=_params("arbitrary"),
    )(dp, wt, h, g, dh_next, *rider.inputs)


def _dgrad_in_to_input(dp, wt, h, g, dh_next, n_head, tm, name, rider=_NO_RIDER):
    t, n = dp.shape
    d = wt.shape[1]
    nt = t // tm

    def body(dp_ref, wt_ref, h_ref, g_ref, dhn_ref, *rest):
        r_in, (dg_ref, head_ref, rest_ref), r_out, (pbuf, tout, slabs, sems), r_scratch = _split_refs(rider, 3, 4, rest)
        i = pl.program_id(0)
        rider(i == 0, i == nt // 2, None, r_in, r_out, r_scratch)
        dn = _dot(dp_ref[...], wt_ref[...], _NN)
        x = h_ref[...]
        ms = jnp.mean(x * x, axis=-1, keepdims=True)
        r = lax.rsqrt(ms + RMS_EPS)
        xhat = x * r
        dng = dn * g_ref[...]
        c = jnp.mean(dng * xhat, axis=-1, keepdims=True)
        pbuf[...] = dhn_ref[...] + r * (dng - xhat * c)
        part = jnp.sum(dn * xhat, axis=0, keepdims=True)

        @pl.when(i == 0)
        def _():
            dg_ref[...] = part

        @pl.when(i > 0)
        def _():
            dg_ref[...] += part

        def store(j):
            if isinstance(j, int) and j == 0:
                return pltpu.make_async_copy(tout.at[0, pl.ds(n_head, tm - n_head)],
                                             rest_ref.at[pl.ds(0, tm - n_head)], sems.at[0])
            start = j * tm - n_head
            start = start if isinstance(start, int) else pl.multiple_of(start, SUBLANES)
            return pltpu.make_async_copy(tout.at[j % 2], rest_ref.at[pl.ds(start, tm)], sems.at[j % 2])

        @pl.when(i == 2)
        def _():
            store(0).wait()

        @pl.when(i > 2)
        def _():
            store(i - 2).wait()

        _tile_to_time_order(pbuf, slabs, tout.at[i % 2])

        @pl.when(i == 0)
        def _():
            head_ref[...] = tout[0, 0:n_head, :]
            store(0).start()

        @pl.when(i > 0)
        def _():
            store(i).start()

        @pl.when(i == nt - 1)
        def _():
            for j in (nt - 2, nt - 1):
                if j >= 0:
                    store(j).wait()

        rider(None, None, i == nt - 1, r_in, r_out, r_scratch)

    return pl.pallas_call(
        body, name=name, grid=(nt,),
        in_specs=[pl.BlockSpec((tm, n), lambda i: (i, 0)),
                  _resident((n, d)),
                  pl.BlockSpec((tm, d), lambda i: (i, 0)),
                  pl.BlockSpec((1, d), lambda i: (0, 0)),
                  pl.BlockSpec((tm, d), lambda i: (i, 0))] + list(rider.in_specs),
        out_specs=[pl.BlockSpec((1, d), lambda i: (0, 0)), _full((n_head, d)), _ANY] + list(rider.out_specs),
        out_shape=[jax.ShapeDtypeStruct((1, d), F32), jax.ShapeDtypeStruct((n_head, d), F32),
                   jax.ShapeDtypeStruct((t - n_head, d), F32)] + list(rider.out_shape),
        scratch_shapes=[pltpu.VMEM((tm, d), F32)] + _time_scratch(tm, d) + list(rider.scratch_shapes),
        compiler_params=_params("arbitrary"),
    )(dp, wt, h, g, dh_next, *rider.inputs)


def _wgrad(a, b, name, rider=_NO_RIDER):
    t, m = a.shape
    d = b.shape[1]
    tmm, tk = _col_tile(m, 1024), _row_tile(t, ROW_TILE_WGRAD)
    if tmm < 512:
        tmm, tk = _col_tile(m, 1536), _row_tile(t, ROW_TILE_MATMUL)
    nm, nk = m // tmm, t // tk

    def body(a_ref, b_ref, *rest):
        r_in, (o_ref,), r_out, _, r_scratch = _split_refs(rider, 1, 0, rest)
        i, k = pl.program_id(0), pl.program_id(1)
        rider(jnp.logical_and(i == 0, k == 0), None, None, r_in, r_out, r_scratch)

        @pl.when(k == 0)
        def _():
            o_ref[...] = jnp.zeros_like(o_ref)

        o_ref[...] += _dot(a_ref[...], b_ref[...], _TN)
        rider(None, None, jnp.logical_and(i == nm - 1, k == nk - 1), r_in, r_out, r_scratch)

    outs = pl.pallas_call(
        body, name=name, grid=(nm, nk),
        in_specs=[pl.BlockSpec((tk, tmm), lambda i, k: (k, i)),
                  pl.BlockSpec((tk, d), lambda i, k: (k, 0))] + list(rider.in_specs),
        out_specs=[pl.BlockSpec((tmm, d), lambda i, k: (i, 0))] + list(rider.out_specs),
        out_shape=[jax.ShapeDtypeStruct((m, d), F32)] + list(rider.out_shape),
        scratch_shapes=list(rider.scratch_shapes),
        compiler_params=_params("arbitrary", "arbitrary"),
    )(a, b, *rider.inputs)
    return outs if rider.out_shape else outs[0]


def _ffn_up_fused(h, g, w, conv_w, tm, name, rider=_NO_RIDER):
    t, d = h.shape
    f2 = w.shape[1]
    f = f2 // 2
    cw = min(STRIP, f)
    kw = conv_w.shape[0]
    nh = kw - 1
    nt = t // tm

    def body(h_ref, g_ref, w_ref, cw_ref, *rest):
        r_in, (n_ref, a_ref, z_ref), r_out, (carry,), r_scratch = _split_refs(rider, 3, 1, rest)
        i = pl.program_id(0)
        rider(i == 0, i == _hand_on_step(nt), None, r_in, r_out, r_scratch)

        @pl.when(i == 0)
        def _():
            carry[...] = jnp.zeros_like(carry)

        _rmsnorm_to(h_ref, g_ref, n_ref)
        for c in range(0, f, cw):
            conv = []
            for sl in (slice(c, c + cw), slice(f + c, f + c + cw)):
                a = _dot(n_ref[...], w_ref[:, sl], _NN)
                a_ref[:, sl] = a
                conv.append(_weighted(_rows(cw_ref, sl), _windows(a, _blocks(carry, sl, nh, True), kw)))
                carry[:, sl] = a[tm - nh * SUBLANES:tm]
            gg, vv = conv
            z_ref[:, c:c + cw] = (gg * _sigmoid(gg) * vv).astype(z_ref.dtype)
        rider(None, None, i == nt - 1, r_in, r_out, r_scratch)

    return pl.pallas_call(
        body, name=name, grid=(nt,),
        in_specs=[pl.BlockSpec((tm, d), lambda i: (i, 0)), pl.BlockSpec((1, d), lambda i: (0, 0)),
                  _resident((d, f2)), _full(conv_w.shape)] + list(rider.in_specs),
        out_specs=[pl.BlockSpec((tm, d), lambda i: (i, 0)), pl.BlockSpec((tm, f2), lambda i: (i, 0)),
                   pl.BlockSpec((tm, f), lambda i: (i, 0))] + list(rider.out_specs),
        out_shape=[jax.ShapeDtypeStruct((t, d), MXU_DTYPE), jax.ShapeDtypeStruct((t, f2), F32),
                   jax.ShapeDtypeStruct((t, f), MXU_DTYPE)] + list(rider.out_shape),
        scratch_shapes=[pltpu.VMEM((nh * SUBLANES, f2), F32)] + list(rider.scratch_shapes),
        compiler_params=_params("arbitrary"),
    )(h, g, w, conv_w, *rider.inputs)


def _ffn_down_bwd_fused(dh, wdt, a, conv_w, tm, name):
    t, d = dh.shape
    f = wdt.shape[1]
    f2 = 2 * f
    nt = t // tm
    cw = min(STRIP, f)
    kw = conv_w.shape[0]
    nh = kw - 1
    rev = lambda i: nt - 1 - i

    def body(dh_ref, wdt_ref, a_ref, *rest):
        halo_refs = rest[:nh]
        cw_ref, da_ref, dhb_ref, dw_ref, carry = rest[nh:]
        i = pl.program_id(0)
        newest, oldest = i == 0, i == nt - 1

        @pl.when(newest)
        def _():
            carry[...] = jnp.zeros_like(carry)
            dw_ref[...] = jnp.zeros_like(dw_ref)

        dhb_ref[...] = dh_ref[...].astype(dhb_ref.dtype)
        for c in range(0, f, cw):
            gsl, vsl = slice(c, c + cw), slice(f + c, f + c + cw)
            dz = _dot(dhb_ref[...], wdt_ref[:, gsl], _NN)
            win, conv = {}, {}
            for sl in (gsl, vsl):
                before = [jnp.where(oldest, 0.0, r[:, sl]) for r in halo_refs]
                win[sl.start] = _windows(a_ref[:, sl], before, kw)
                conv[sl.start] = _weighted(_rows(cw_ref, sl), win[sl.start])
            gg, vv = conv[gsl.start], conv[vsl.start]
            s = _sigmoid(gg)
            grads = {gsl.start: dz * vv * s * (1.0 + gg * (1.0 - s)), vsl.start: dz * gg * s}
            for sl in (gsl, vsl):
                dconv = grads[sl.start]
                da_ref[:, sl] = _conv_t(dconv, _blocks(carry, sl, nh, False), _rows(cw_ref, sl)).astype(da_ref.dtype)
                for k in range(kw):
                    dw_ref[k:k + 1, sl] += jnp.sum(dconv * win[sl.start][k], axis=0, keepdims=True)
                carry[:, sl] = dconv[0:nh * SUBLANES]

    return pl.pallas_call(
        body, name=name, grid=(nt,),
        in_specs=[pl.BlockSpec((tm, d), lambda i: (rev(i), 0)), _resident((d, f)),
                  pl.BlockSpec((tm, f2), lambda i: (rev(i), 0))] + _halo_specs(tm, f2, nh, rev) + [_full(conv_w.shape)],
        out_specs=[pl.BlockSpec((tm, f2), lambda i: (rev(i), 0)), pl.BlockSpec((tm, d), lambda i: (rev(i), 0)),
                   _full(conv_w.shape)],
        out_shape=[jax.ShapeDtypeStruct((t, f2), MXU_DTYPE), jax.ShapeDtypeStruct((t, d), MXU_DTYPE),
                   jax.ShapeDtypeStruct(conv_w.shape, F32)],
        scratch_shapes=[pltpu.VMEM((nh * SUBLANES, f2), F32)],
        compiler_params=_params("arbitrary"),
    )(dh, wdt, a, *([a] * nh), conv_w)


def _sc_in_fused(x, meta, g, w, conv_w, tm, name, rider=_NO_RIDER):
    d = x.shape[1]
    t = x.shape[0] + meta.shape[0]
    cw = min(STRIP, d)
    kw = conv_w.shape[0]
    nh = kw - 1
    nt = t // tm

    def body(x_ref, meta_ref, g_ref, w_ref, cw_ref, *rest):
        r_in, (h_ref, n_ref, p_ref, q_ref), r_out, (carry, tbuf, slabs, sems), r_scratch = _split_refs(rider, 4, 4, rest)
        i = pl.program_id(0)
        rider(i == 0, i == _hand_on_step(nt), None, r_in, r_out, r_scratch)

        @pl.when(i == 0)
        def _():
            carry[...] = jnp.zeros_like(carry)

        _fetch_time_tile(i, nt, x_ref, meta_ref[...], tbuf, sems, tm)
        _time_to_tile_order(tbuf.at[i % 2], slabs, h_ref)
        _rmsnorm_to(h_ref, g_ref, n_ref)
        for c in range(0, d, cw):
            sl = slice(c, c + cw)
            parts = []
            for base in (0, d, 2 * d):
                psl = slice(base + c, base + c + cw)
                parts.append(_dot(n_ref[...], w_ref[:, psl], _NN))
                p_ref[:, psl] = parts[-1]
            bg, cg, v = parts
            cv = cg * v
            u = _weighted(_rows(cw_ref, sl), _windows(cv, _blocks(carry, sl, nh, True), kw))
            carry[:, sl] = cv[tm - nh * SUBLANES:tm]
            q_ref[:, sl] = (bg * u).astype(q_ref.dtype)
        rider(None, None, i == nt - 1, r_in, r_out, r_scratch)

    row_tile = lambda cols: pl.BlockSpec((tm, cols), lambda i: (i, 0))
    return pl.pallas_call(
        body, name=name, grid=(nt,),
        in_specs=[_ANY, _full(meta.shape), pl.BlockSpec((1, d), lambda i: (0, 0)),
                  _resident((d, 3 * d)), _full(conv_w.shape)] + list(rider.in_specs),
        out_specs=[row_tile(d), row_tile(d), row_tile(3 * d), row_tile(d)] + list(rider.out_specs),
        out_shape=[jax.ShapeDtypeStruct((t, d), F32), jax.ShapeDtypeStruct((t, d), MXU_DTYPE),
                   jax.ShapeDtypeStruct((t, 3 * d), F32), jax.ShapeDtypeStruct((t, d), MXU_DTYPE)]
                  + list(rider.out_shape),
        scratch_shapes=[pltpu.VMEM((nh * SUBLANES, d), F32)] + _time_scratch(tm, d) + list(rider.scratch_shapes),
        compiler_params=_params("arbitrary"),
    )(x, meta, g, w, conv_w, *rider.inputs)


def _sc_out_bwd_fused(dh, wot, p, conv_w, tm, name, rider=_NO_RIDER):
    t, d = dh.shape
    nt = t // tm
    cw = min(STRIP, d)
    kw = conv_w.shape[0]
    nh = kw - 1
    rev = lambda i: nt - 1 - i

    def body(dh_ref, wot_ref, p_ref, *rest):
        halo_refs, cw_ref = rest[:nh], rest[nh]
        r_in, (dp_ref, dhb_ref, dw_ref), r_out, (carry,), r_scratch = _split_refs(rider, 3, 1, rest[nh + 1:])
        i = pl.program_id(0)
        newest, oldest = i == 0, i == nt - 1
        rider(newest, i == nt // 2, None, r_in, r_out, r_scratch)

        @pl.when(newest)
        def _():
            carry[...] = jnp.zeros_like(carry)
            dw_ref[...] = jnp.zeros_like(dw_ref)

        dhb_ref[...] = dh_ref[...].astype(dhb_ref.dtype)
        for c in range(0, d, cw):
            sl, csl, vsl = slice(c, c + cw), slice(d + c, d + c + cw), slice(2 * d + c, 2 * d + c + cw)
            w = _rows(cw_ref, sl)
            dq = _dot(dhb_ref[...], wot_ref[:, sl], _NN)
            bg, cg, v = p_ref[:, sl], p_ref[:, csl], p_ref[:, vsl]
            before = [jnp.where(oldest, 0.0, r[:, csl] * r[:, vsl]) for r in halo_refs]
            win = _windows(cg * v, before, kw)
            du = dq * bg
            dcv = _conv_t(du, _blocks(carry, sl, nh, False), w)
            dp_ref[:, sl] = (dq * _weighted(w, win)).astype(dp_ref.dtype)
            dp_ref[:, csl] = (dcv * v).astype(dp_ref.dtype)
            dp_ref[:, vsl] = (dcv * cg).astype(dp_ref.dtype)
            for k in range(kw):
                dw_ref[k:k + 1, sl] += jnp.sum(du * win[k], axis=0, keepdims=True)
            carry[:, sl] = du[0:nh * SUBLANES]
        rider(None, None, oldest, r_in, r_out, r_scratch)

    tile = lambda cols: pl.BlockSpec((tm, cols), lambda i: (rev(i), 0))
    return pl.pallas_call(
        body, name=name, grid=(nt,),
        in_specs=[tile(d), _resident((d, d)), tile(3 * d)] + _halo_specs(tm, 3 * d, nh, rev) + [_full(conv_w.shape)]
                 + list(rider.in_specs),
        out_specs=[tile(3 * d), tile(d), _full(conv_w.shape)] + list(rider.out_specs),
        out_shape=[jax.ShapeDtypeStruct((t, 3 * d), MXU_DTYPE), jax.ShapeDtypeStruct((t, d), MXU_DTYPE),
                   jax.ShapeDtypeStruct(conv_w.shape, F32)] + list(rider.out_shape),
        scratch_shapes=[pltpu.VMEM((nh * SUBLANES, d), F32)] + list(rider.scratch_shapes),
        compiler_params=_params("arbitrary"),
    )(dh, wot, p, *([p] * nh), conv_w, *rider.inputs)


def _pair_gate_weights(wa, wx):
    nb, bd, _ = wa.shape
    zero = jnp.zeros((nb // 2, bd, bd), wa.dtype)

    def pair(w):
        w = w.reshape(nb // 2, 2, bd, bd)
        top = jnp.concatenate([w[:, 0], zero], axis=2)
        bottom = jnp.concatenate([zero, w[:, 1]], axis=2)
        return jnp.concatenate([top, bottom], axis=1)

    both = jnp.concatenate([pair(wa), pair(wx)], axis=2)
    return both, jnp.swapaxes(both, 1, 2)


def _unpair_gate_grads(dw, bd):
    def blocks(cols):
        d0 = dw[:, :bd, cols:cols + bd]
        d1 = dw[:, bd:, cols + bd:cols + 2 * bd]
        return jnp.stack([d0, d1], axis=1).reshape(-1, bd, bd)
    return blocks(0), blocks(2 * bd)


def _rg_gates(u, k, wg_ref, ba, bx, lam):
    ub = u.astype(MXU_DTYPE)
    z = _dot(ub, wg_ref[k], _NN)
    half = z.shape[1] // 2
    r = _sigmoid_tanh(z[:, :half] + ba)
    ig = _sigmoid_tanh(z[:, half:] + bx)
    sp = _softplus(-lam)
    la = -RG_C * r * sp
    a = jnp.exp(la)
    th = jnp.tanh(la)
    mult = jnp.sqrt(-2.0 * th / (1.0 - th))
    return r, ig, a, mult


def _rg_in_fused(h, g, w, conv_w, conv_b, wg, ba, bx, lam, tm, name, rider=_NO_RIDER):
    t, d = h.shape
    r2 = w.shape[1]
    rr = r2 // 2
    nb, bd = wg.shape[0], wg.shape[1]
    kw = conv_w.shape[0]
    nh = kw - 1
    groups = tm // SUBLANES
    nt = t // tm

    def body(h_ref, g_ref, w_ref, cw_ref, cb_ref, wg_ref, ba_ref, bx_ref, lam_ref, *rest):
        r_in, (n_ref, p_ref, y_ref, hs_ref), r_out, (a_scr, b_scr, carry_rb, carry_h), r_scratch = _split_refs(
            rider, 4, 4, rest)
        i = pl.program_id(0)
        rider(i == 0, i == _hand_on_step(nt), None, r_in, r_out, r_scratch)

        @pl.when(i == 0)
        def _():
            carry_rb[...] = jnp.zeros_like(carry_rb)
            carry_h[...] = jnp.zeros_like(carry_h)

        _rmsnorm_to(h_ref, g_ref, n_ref)
        for k in range(nb):
            sl = slice(k * bd, (k + 1) * bd)
            rsl = slice(rr + k * bd, rr + (k + 1) * bd)
            p_ref[:, sl] = _dot(n_ref[...], w_ref[:, sl], _NN)
            rb = _dot(n_ref[...], w_ref[:, rsl], _NN)
            p_ref[:, rsl] = rb
            u = _weighted(_rows(cw_ref, sl), _windows(rb, _blocks(carry_rb, sl, nh, True), kw)) + cb_ref[:, sl]
            carry_rb[:, sl] = rb[tm - nh * SUBLANES:tm]
            _, ig, a, mult = _rg_gates(u, k, wg_ref, ba_ref[:, sl], bx_ref[:, sl], lam_ref[:, sl])
            a_scr[:, sl] = a
            b_scr[:, sl] = mult * (ig * u)

        carry_h[...] = _scan_tile(a_scr, b_scr, hs_ref, carry_h[...], groups, reverse=False)

        for k in range(nb):
            sl = slice(k * bd, (k + 1) * bd)
            y_ref[:, sl] = (hs_ref[:, sl] * _gelu(p_ref[:, sl])).astype(y_ref.dtype)
        rider(None, None, i == nt - 1, r_in, r_out, r_scratch)

    vm = lambda rows: pltpu.VMEM((rows, rr), F32)
    return pl.pallas_call(
        body, name=name, grid=(nt,),
        in_specs=[pl.BlockSpec((tm, d), lambda i: (i, 0)), pl.BlockSpec((1, d), lambda i: (0, 0)),
                  _resident((d, r2)), _full(conv_w.shape), _full(conv_b.shape), _full(wg.shape), _full(ba.shape),
                  _full(bx.shape), _full(lam.shape)] + list(rider.in_specs),
        out_specs=[pl.BlockSpec((tm, d), lambda i: (i, 0)), pl.BlockSpec((tm, r2), lambda i: (i, 0)),
                   pl.BlockSpec((tm, rr), lambda i: (i, 0)), pl.BlockSpec((tm, rr), lambda i: (i, 0))]
                  + list(rider.out_specs),
        out_shape=[jax.ShapeDtypeStruct((t, d), MXU_DTYPE), jax.ShapeDtypeStruct((t, r2), F32),
                   jax.ShapeDtypeStruct((t, rr), MXU_DTYPE), jax.ShapeDtypeStruct((t, rr), F32)]
                  + list(rider.out_shape),
        scratch_shapes=[vm(tm), vm(tm), vm(nh * SUBLANES), vm(SUBLANES)] + list(rider.scratch_shapes),
        compiler_params=_params("arbitrary"),
    )(h, g, w, conv_w, conv_b, wg, ba, bx, lam, *rider.inputs)


def _rg_out_bwd_fused(dh, wot, p, hs, conv_w, conv_b, wg, wgt, ba, bx, lam, tm, name, rider=_NO_RIDER):
    t, d = dh.shape
    r2 = p.shape[1]
    rr = r2 // 2
    nb, bd = wg.shape[0], wg.shape[1]
    nt = t // tm
    kw = conv_w.shape[0]
    nh = kw - 1
    groups = tm // SUBLANES
    rev = lambda i: nt - 1 - i

    def body(dh_ref, wot_ref, p_ref, hs_ref, hsh_ref, *rest):
        halo_refs = rest[:nh]
        cw_ref, cb_ref, wg_ref, wgt_ref, ba_ref, bx_ref, lam_ref = rest[nh:nh + 7]
        (r_in, (dp_ref, dhb_ref, dcw_ref, dcb_ref, dwg_ref, dba_ref, dbx_ref, dlam_ref), r_out,
         (a_scr, as_scr, u_scr, r_scr, ig_scr, mult_scr, g_scr, carry_g, carry_du), r_scratch) = _split_refs(
            rider, 8, 9, rest[nh + 7:])
        i = pl.program_id(0)
        newest, oldest = i == 0, i == nt - 1
        rider(newest, i == nt // 2, None, r_in, r_out, r_scratch)

        @pl.when(newest)
        def _():
            carry_g[...] = jnp.zeros_like(carry_g)
            carry_du[...] = jnp.zeros_like(carry_du)
            for ref in (dcw_ref, dcb_ref, dwg_ref, dba_ref, dbx_ref, dlam_ref):
                ref[...] = jnp.zeros_like(ref)

        dhb_ref[...] = dh_ref[...].astype(dhb_ref.dtype)
        ones = jnp.ones((SUBLANES, bd), F32)

        def conv_windows(sl, rsl):
            before = [jnp.where(oldest, 0.0, r[:, rsl]) for r in halo_refs]
            return _windows(p_ref[:, rsl], before, kw)

        for k in range(nb):
            sl = slice(k * bd, (k + 1) * bd)
            rsl = slice(rr + k * bd, rr + (k + 1) * bd)
            dy = _dot(dhb_ref[...], wot_ref[:, sl], _NN)
            u = _weighted(_rows(cw_ref, sl), conv_windows(sl, rsl)) + cb_ref[:, sl]
            r, ig, a, mult = _rg_gates(u, k, wg_ref, ba_ref[:, sl], bx_ref[:, sl], lam_ref[:, sl])
            gate, dgate = _gelu_and_grad(p_ref[:, sl])
            g_scr[:, sl] = dy * gate
            dp_ref[:, sl] = (dy * hs_ref[:, sl] * dgate).astype(dp_ref.dtype)
            a_scr[:, sl] = a
            as_scr[:, sl] = _shift_up(a, [ones], 1)
            u_scr[:, sl] = u
            r_scr[:, sl] = r
            ig_scr[:, sl] = ig
            mult_scr[:, sl] = mult

        g_first = _scan_tile(as_scr, g_scr, g_scr, carry_g[...], groups, reverse=True)
        carry_g[...] = jnp.broadcast_to(a_scr[0:1, :], (SUBLANES, rr)) * g_first

        for k in range(nb):
            sl = slice(k * bd, (k + 1) * bd)
            rsl = slice(rr + k * bd, rr + (k + 1) * bd)
            cw = _rows(cw_ref, sl)
            lam_k = lam_ref[:, sl]
            sp = _softplus(-lam_k)
            g = g_scr[:, sl]
            a, u, r, ig, mult = a_scr[:, sl], u_scr[:, sl], r_scr[:, sl], ig_scr[:, sl], mult_scr[:, sl]
            da = g * _shift_down(hs_ref[:, sl], [jnp.where(oldest, 0.0, hsh_ref[:, sl])], 1)
            dmult = g * (ig * u)
            d_iu = g * mult
            dla = da * a - dmult * (a * a) / mult
            dr = dla * (-RG_C * sp)
            dsp = jnp.sum(dla * (-RG_C * r), axis=0, keepdims=True)
            dlam_ref[:, sl] += dsp * (-_sigmoid(-lam_k))
            dza = dr * r * (1.0 - r)
            dzx = (d_iu * u) * ig * (1.0 - ig)
            dba_ref[:, sl] += jnp.sum(dza, axis=0, keepdims=True)
            dbx_ref[:, sl] += jnp.sum(dzx, axis=0, keepdims=True)
            ub = u.astype(MXU_DTYPE)
            dz = jnp.concatenate([dza, dzx], axis=1).astype(MXU_DTYPE)
            dwg_ref[k] += _dot(ub, dz, _TN)
            du = d_iu * ig + _dot(dz, wgt_ref[k], _NN)
            dcb_ref[:, sl] += jnp.sum(du, axis=0, keepdims=True)
            win = conv_windows(sl, rsl)
            for kk in range(kw):
                dcw_ref[kk:kk + 1, sl] += jnp.sum(du * win[kk], axis=0, keepdims=True)
            dp_ref[:, rsl] = _conv_t(du, _blocks(carry_du, sl, nh, False), cw).astype(dp_ref.dtype)
            carry_du[:, sl] = du[0:nh * SUBLANES]
        rider(None, None, oldest, r_in, r_out, r_scratch)

    tile = lambda cols: pl.BlockSpec((tm, cols), lambda i: (rev(i), 0))
    vm = lambda rows: pltpu.VMEM((rows, rr), F32)
    grads = [conv_w.shape, conv_b.shape, wg.shape, ba.shape, bx.shape, lam.shape]
    return pl.pallas_call(
        body, name=name, grid=(nt,),
        in_specs=[tile(d), _resident((d, rr)), tile(r2), tile(rr)] + _halo_specs(tm, rr, 1, rev)
                 + _halo_specs(tm, r2, nh, rev)
                 + [_full(conv_w.shape), _full(conv_b.shape), _full(wg.shape), _full(wgt.shape), _full(ba.shape),
                    _full(bx.shape), _full(lam.shape)] + list(rider.in_specs),
        out_specs=[tile(r2), tile(d)] + [_full(s) for s in grads] + list(rider.out_specs),
        out_shape=[jax.ShapeDtypeStruct((t, r2), MXU_DTYPE), jax.ShapeDtypeStruct((t, d), MXU_DTYPE)]
                  + [jax.ShapeDtypeStruct(s, F32) for s in grads] + list(rider.out_shape),
        scratch_shapes=[vm(tm)] * 7 + [vm(SUBLANES), vm(nh * SUBLANES)] + list(rider.scratch_shapes),
        compiler_params=_params("arbitrary"),
    )(dh, wot, p, hs, hs, *([p] * nh), conv_w, conv_b, wg, wgt, ba, bx, lam, *rider.inputs)


def _down_loss_head(z, w, h, g, target, n_meta, tm, name):
    t, d = h.shape
    k = z.shape[1]
    groups = tm // SUBLANES
    nt = t // tm

    def body(z_ref, w_ref, h_ref, g_ref, t_ref, dh_ref, loss_ref, dg_ref, tgt, tbuf, slabs, sems):
        i = pl.program_id(0)
        _fetch_time_tile(i, nt, t_ref, jnp.zeros((n_meta, d), F32), tbuf, sems, tm)
        _time_to_tile_order(tbuf.at[i % 2], slabs, tgt)
        x = h_ref[...] + _dot(z_ref[...], w_ref[...], _NN)
        ms = jnp.mean(x * x, axis=-1, keepdims=True)
        r = lax.rsqrt(ms + RMS_EPS)
        xhat = x * r
        gg = g_ref[...]
        row = lax.broadcasted_iota(jnp.int32, (tm, 1), 0)
        time = i * tm + jnp.right_shift(row, 3) + jnp.bitwise_and(row, SUBLANES - 1) * groups
        err = jnp.where(time >= n_meta, xhat * gg - tgt[...], 0.0)
        dout = err * (1.0 / d)
        dng = dout * gg
        c = jnp.mean(dng * xhat, axis=-1, keepdims=True)
        dh_ref[...] = r * (dng - xhat * c)
        part_dg = jnp.sum(dout * xhat, axis=0, keepdims=True)
        part_loss = jnp.broadcast_to(0.5 * jnp.sum(err * dout, keepdims=True), loss_ref.shape)

        @pl.when(i == 0)
        def _():
            dg_ref[...] = part_dg
            loss_ref[...] = part_loss

        @pl.when(i > 0)
        def _():
            dg_ref[...] += part_dg
            loss_ref[...] += part_loss

    return pl.pallas_call(
        body, name=name, grid=(nt,),
        in_specs=[pl.BlockSpec((tm, k), lambda i: (i, 0)),
                  _resident((k, d)),
                  pl.BlockSpec((tm, d), lambda i: (i, 0)),
                  pl.BlockSpec((1, d), lambda i: (0, 0)),
                  _ANY],
        out_specs=[pl.BlockSpec((tm, d), lambda i: (i, 0)),
                   pl.BlockSpec((SUBLANES, LANES), lambda i: (0, 0)),
                   pl.BlockSpec((1, d), lambda i: (0, 0))],
        out_shape=[jax.ShapeDtypeStruct((t, d), F32), jax.ShapeDtypeStruct((SUBLANES, LANES), F32),
                   jax.ShapeDtypeStruct((1, d), F32)],
        scratch_shapes=[pltpu.VMEM((tm, d), F32)] + _time_scratch(tm, d),
        compiler_params=_params("arbitrary"),
    )(z, w, h, g, target)


def _adamw(w, g, m, v, name):
    rows, cols = w.shape
    tr = rows
    if rows > 512:
        for cand in range(8, 513, 8):
            if rows % cand == 0:
                tr = cand

    def body(w_ref, g_ref, m_ref, v_ref, d_ref, nm_ref, nv_ref):
        g_ = g_ref[...]
        m_ = ADAM_B1 * m_ref[...] + (1.0 - ADAM_B1) * g_
        v_ = ADAM_B2 * v_ref[...] + (1.0 - ADAM_B2) * (g_ * g_)
        m_hat = m_ / (1.0 - ADAM_B1 ** ADAM_STEP)
        v_hat = v_ / (1.0 - ADAM_B2 ** ADAM_STEP)
        d_ref[...] = -ADAM_LR * (m_hat / (jnp.sqrt(v_hat) + ADAM_EPS) + ADAM_WD * w_ref[...])
        nm_ref[...] = m_
        nv_ref[...] = v_

    spec = pl.BlockSpec((tr, cols), lambda i: (i, 0))
    shape = jax.ShapeDtypeStruct((rows, cols), F32)
    return pl.pallas_call(
        body, name=name, grid=(rows // tr,),
        in_specs=[spec] * 4, out_specs=[spec] * 3, out_shape=[shape] * 3,
        compiler_params=_params("parallel"),
    )(w, g, m, v)


def _adamw_nd(w, g, m, v, name):
    shape = w.shape
    two_d = (-1, shape[-1]) if w.ndim > 1 else (1, -1)
    outs = _adamw(w.reshape(two_d), g.reshape(two_d), m.reshape(two_d), v.reshape(two_d), name)
    return tuple(o.reshape(shape) for o in outs)


def _ride_alone(rider, name):
    def body(*refs):
        r_in, _, r_out, _, r_scratch = _split_refs(rider, 0, 0, refs)
        now = pl.program_id(0) == 0
        rider(now, None, None, r_in, r_out, r_scratch)
        rider(None, now, None, r_in, r_out, r_scratch)
        rider(None, None, now, r_in, r_out, r_scratch)

    return pl.pallas_call(
        body, name=name, grid=(1,),
        in_specs=list(rider.in_specs), out_specs=list(rider.out_specs), out_shape=list(rider.out_shape),
        scratch_shapes=list(rider.scratch_shapes),
        compiler_params=_params("arbitrary"),
    )(*rider.inputs)


def _sum_slots(parts, name):
    slots, rows, cols = parts.shape
    tr = _row_tile(rows, 256) if rows % 16 == 0 else rows

    def body(p_ref, o_ref):
        acc = p_ref[0]
        for s in range(1, slots):
            acc = acc + p_ref[s]
        o_ref[...] = acc

    return pl.pallas_call(
        body, name=name, grid=(rows // tr,),
        in_specs=[pl.BlockSpec((slots, tr, cols), lambda i: (0, i, 0))],
        out_specs=pl.BlockSpec((tr, cols), lambda i: (i, 0)),
        out_shape=jax.ShapeDtypeStruct((rows, cols), parts.dtype),
        compiler_params=_params("parallel"),
    )(parts)


def _pad_rows(flat, cols, multiple):
    n = flat.shape[0]
    rows = -(-n // cols)
    rows = -(-rows // multiple) * multiple
    return jnp.pad(flat, (0, rows * cols - n)).reshape(rows, cols)


def _cols_to_chunks(full):
    lead = full.shape[:-1]
    c = full.shape[-1] // N_DEV
    x = full.reshape(-1, N_DEV, c)
    return jnp.transpose(x, (1, 0, 2)).reshape(N_DEV, -1)


def _chunks_to_cols(chunks, lead):
    n = 1
    for s in lead:
        n *= s
    c = chunks.shape[1] // n
    x = chunks.reshape(N_DEV, n, c)
    return jnp.transpose(x, (1, 0, 2)).reshape(tuple(lead) + (N_DEV * c,))


def kernel(x, meta_tokens, norm_mix_g, norm_ffn_g, final_norm_g, sc_w_in, sc_conv_w, sc_w_out, rg_w_in, rg_conv_w, rg_conv_b, rg_w_gate_a, rg_b_gate_a, rg_w_gate_x, rg_b_gate_x, rg_lambda, rg_w_out, ffn_w_up, ffn_conv_w, ffn_w_down, loss_target, m_meta_tokens, m_norm_mix_g, m_norm_ffn_g, m_final_norm_g, m_sc_w_in, m_sc_conv_w, m_sc_w_out, m_rg_w_in, m_rg_conv_w, m_rg_conv_b, m_rg_w_gate_a, m_rg_b_gate_a, m_rg_w_gate_x, m_rg_b_gate_x, m_rg_lambda, m_rg_w_out, m_ffn_w_up, m_ffn_conv_w, m_ffn_w_down, v_meta_tokens, v_norm_mix_g, v_norm_ffn_g, v_final_norm_g, v_sc_w_in, v_sc_conv_w, v_sc_w_out, v_rg_w_in, v_rg_conv_w, v_rg_conv_b, v_rg_w_gate_a, v_rg_b_gate_a, v_rg_w_gate_x, v_rg_b_gate_x, v_rg_lambda, v_rg_w_out, v_ffn_w_up, v_ffn_conv_w, v_ffn_w_down):
    weights = dict(meta_tokens=meta_tokens, norm_mix_g=norm_mix_g, norm_ffn_g=norm_ffn_g, final_norm_g=final_norm_g,
                   sc_w_in=sc_w_in, sc_conv_w=sc_conv_w, sc_w_out=sc_w_out, rg_w_in=rg_w_in, rg_conv_w=rg_conv_w,
                   rg_conv_b=rg_conv_b, rg_w_gate_a=rg_w_gate_a, rg_b_gate_a=rg_b_gate_a, rg_w_gate_x=rg_w_gate_x,
                   rg_b_gate_x=rg_b_gate_x, rg_lambda=rg_lambda, rg_w_out=rg_w_out, ffn_w_up=ffn_w_up,
                   ffn_conv_w=ffn_conv_w, ffn_w_down=ffn_w_down)
    m_in = dict(meta_tokens=m_meta_tokens, norm_mix_g=m_norm_mix_g, norm_ffn_g=m_norm_ffn_g, final_norm_g=m_final_norm_g,
                sc_w_in=m_sc_w_in, sc_conv_w=m_sc_conv_w, sc_w_out=m_sc_w_out, rg_w_in=m_rg_w_in, rg_conv_w=m_rg_conv_w,
                rg_conv_b=m_rg_conv_b, rg_w_gate_a=m_rg_w_gate_a, rg_b_gate_a=m_rg_b_gate_a, rg_w_gate_x=m_rg_w_gate_x,
                rg_b_gate_x=m_rg_b_gate_x, rg_lambda=m_rg_lambda, rg_w_out=m_rg_w_out, ffn_w_up=m_ffn_w_up,
                ffn_conv_w=m_ffn_conv_w, ffn_w_down=m_ffn_w_down)
    v_in = dict(meta_tokens=v_meta_tokens, norm_mix_g=v_norm_mix_g, norm_ffn_g=v_norm_ffn_g, final_norm_g=v_final_norm_g,
                sc_w_in=v_sc_w_in, sc_conv_w=v_sc_conv_w, sc_w_out=v_sc_w_out, rg_w_in=v_rg_w_in, rg_conv_w=v_rg_conv_w,
                rg_conv_b=v_rg_conv_b, rg_w_gate_a=v_rg_w_gate_a, rg_b_gate_a=v_rg_b_gate_a, rg_w_gate_x=v_rg_w_gate_x,
                rg_b_gate_x=v_rg_b_gate_x, rg_lambda=v_rg_lambda, rg_w_out=v_rg_w_out, ffn_w_up=v_ffn_w_up,
                ffn_conv_w=v_ffn_conv_w, ffn_w_down=v_ffn_w_down)
    names = list(weights)

    seq, d = x.shape[1], x.shape[2]
    n_meta = meta_tokens.shape[0]
    n_ffn = ffn_w_up.shape[0]

    assert n_ffn == 2
    def shard_rows(w_in, w_out):
        return [w_in.T.astype(MXU_DTYPE), w_out.astype(MXU_DTYPE)]

    def both_orientations(in_t_gathered, out_gathered):
        w_in_t = in_t_gathered.reshape(-1, d)
        w_out = out_gathered.reshape(-1, d)
        return w_in_t.T, w_in_t, w_out, w_out.T

    shards = {"ffn0": shard_rows(ffn_w_up[0], ffn_w_down[0]), "rg": shard_rows(rg_w_in[0], rg_w_out[0]),
              "ffn1": shard_rows(ffn_w_up[1], ffn_w_down[1])}
    ffn_w_up_f, ffn_w_up_t, ffn_w_down_f, ffn_w_down_t = [None] * 2, [None] * 2, [None] * 2, [None] * 2

    small_names = ["meta_tokens", "sc_conv_w", "rg_conv_w", "rg_conv_b", "rg_b_gate_a", "rg_b_gate_x", "rg_lambda",
                   "ffn_conv_w"]
    small_lead = {n: weights[n].shape[:-1] for n in small_names}
    small_sizes = [weights[n].size for n in small_names]
    small_flat = jnp.concatenate([weights[n].reshape(-1) for n in small_names])
    small_rows = _pad_rows(small_flat, d, SUBLANES)
    sc_in_g, sc_out_g, small_g = _ride_alone(
        _GatherRider(shard_rows(sc_w_in[0], sc_w_out[0]) + [small_rows]), "gather_first")
    sc_w_in_f, sc_w_in_t, sc_w_out_f, sc_w_out_t = both_orientations(sc_in_g, sc_out_g)
    small_g = small_g.reshape(N_DEV, -1)
    small_full = {}
    o = 0
    for n, sz in zip(small_names, small_sizes):
        small_full[n] = _chunks_to_cols(small_g[:, o:o + sz], small_lead[n])
        o += sz

    wg, wgt = _pair_gate_weights(rg_w_gate_a[0].astype(MXU_DTYPE), rg_w_gate_x[0].astype(MXU_DTYPE))
    rg_cw, rg_cb = small_full["rg_conv_w"][0], small_full["rg_conv_b"]
    rg_ba, rg_bx, rg_lam = small_full["rg_b_gate_a"], small_full["rg_b_gate_x"], small_full["rg_lambda"]
    sc_cw = small_full["sc_conv_w"][0]
    ffn_cw = small_full["ffn_conv_w"]

    tp = _row_tile(n_meta + seq, ROW_TILE_PERM)

    def ffn_fwd(h, l, rider):
        n, a, z, *gathered = _ffn_up_fused(h, norm_ffn_g[l:l + 1], ffn_w_up_f[l], ffn_cw[l], tp, f"ffn{l}_up", rider)
        return _matmul_residual(z, ffn_w_down_f[l], h, f"ffn{l}_down"), (n, a, z), gathered

    h0, n0, p0, q0, *ffn0_w = _sc_in_fused(x[0], small_full["meta_tokens"], norm_mix_g[0:1], sc_w_in_f, sc_cw, tp, "sc_in",
                                           _GatherRider(shards["ffn0"]))
    ffn_w_up_f[0], ffn_w_up_t[0], ffn_w_down_f[0], ffn_w_down_t[0] = both_orientations(*ffn0_w)
    h1 = _matmul_residual(q0, sc_w_out_f, h0, "sc_out")
    h2, ffn0_saved, rg_w = ffn_fwd(h1, 0, _GatherRider(shards["rg"]))
    rg_w_in_f, rg_w_in_t, rg_w_out_f, rg_w_out_t = both_orientations(*rg_w)
    n2, p2, y2, hs2, *ffn1_w = _rg_in_fused(h2, norm_mix_g[1:2], rg_w_in_f, rg_cw, rg_cb, wg, rg_ba, rg_bx, rg_lam,
                                            tp, "rg_in", _GatherRider(shards["ffn1"]))
    ffn_w_up_f[1], ffn_w_up_t[1], ffn_w_down_f[1], ffn_w_down_t[1] = both_orientations(*ffn1_w)
    h3 = _matmul_residual(y2, rg_w_out_f, h2, "rg_out")
    ffn1_saved = tuple(_ffn_up_fused(h3, norm_ffn_g[1:2], ffn_w_up_f[1], ffn_cw[1], tp, "ffn1_up"))
    dh4, loss_tile, d_final_g = _down_loss_head(ffn1_saved[2], ffn_w_down_f[1], h3, final_norm_g.reshape(1, d),
                                                loss_target[0], n_meta, tp, "ffn1_down_loss_head")
    loss = lax.psum(loss_tile[0, 0], AXES)

    arrived = {}

    def ffn_bwd(dh_out, h_in, saved, l):
        n, a, z = saved
        da, dhb, d_cw = _ffn_down_bwd_fused(dh_out, ffn_w_down_t[l], a, ffn_cw[l], tp, f"ffn{l}_down_bwd")
        d_w_down = _wgrad(z, dhb, f"ffn{l}_down_wgrad")
        d_w_up_t, arrived[f"ffn_w_down{l}"] = _wgrad(da, n, f"ffn{l}_up_wgrad", _ScatterRider(d_w_down))
        rows = d_w_up_t.shape[0] // N_DEV
        first_rows = (rows // 2) // (2 * SUBLANES) * (2 * SUBLANES)
        dh_in, d_g, arrived[f"ffn_w_up{l}a"] = _dgrad_in_norm(
            da, ffn_w_up_t[l], h_in, norm_ffn_g[l:l + 1], dh_out, f"ffn{l}_up_dgrad",
            _ScatterRider(d_w_up_t, (0, first_rows)))
        return dh_in, d_cw, d_g, _ScatterRider(d_w_up_t, (first_rows, rows - first_rows))

    dh3, d_fcw1, d_fg1, up1_rest = ffn_bwd(dh4, h3, ffn1_saved, 1)

    dp2, dhb3, d_rg_cw, d_rg_cb, d_wg, d_rg_ba, d_rg_bx, d_rg_lam, arrived["ffn_w_up1b"] = _rg_out_bwd_fused(
        dh3, rg_w_out_t, p2, hs2, rg_cw, rg_cb, wg, wgt, rg_ba, rg_bx, rg_lam, tp, "rg_out_bwd", up1_rest)
    d_wa, d_wx = _unpair_gate_grads(d_wg, rg_w_gate_a.shape[2])
    d_rg_w_out = _wgrad(y2, dhb3, "rg_out_wgrad")
    d_rg_w_in_t, arrived["rg_w_out"] = _wgrad(dp2, n2, "rg_in_wgrad", _ScatterRider(d_rg_w_out))
    dh2, d_mg1, arrived["rg_w_in"] = _dgrad_in_norm(
        dp2, rg_w_in_t, h2, norm_mix_g[1:2], dh3, "rg_in_dgrad", _ScatterRider(d_rg_w_in_t))

    dh1, d_fcw0, d_fg0, up0_rest = ffn_bwd(dh2, h1, ffn0_saved, 0)

    dp0, dhb1, d_sc_cw, arrived["ffn_w_up0b"] = _sc_out_bwd_fused(dh1, sc_w_out_t, p0, sc_cw, tp, "sc_out_bwd", up0_rest)
    d_sc_w_out = _wgrad(q0, dhb1, "sc_out_wgrad")
    d_sc_w_in_t, arrived["sc_w_out"] = _wgrad(dp0, n0, "sc_in_wgrad", _ScatterRider(d_sc_w_out))
    d_mg0, d_meta, d_x, arrived["sc_w_in"] = _dgrad_in_to_input(
        dp0, sc_w_in_t, h0, norm_mix_g[0:1], dh1, n_meta, tp, "sc_in_dgrad", _ScatterRider(d_sc_w_in_t))
    grad_x = d_x[None]

    grads = {}
    for n in ("sc_w_in", "rg_w_in"):
        grads[n] = _sum_slots(arrived[n], f"sum_{n}").T[None]
    for n in ("sc_w_out", "rg_w_out"):
        grads[n] = _sum_slots(arrived[n], f"sum_{n}")[None]
    grads["ffn_w_up"] = jnp.stack([jnp.concatenate(
        [_sum_slots(arrived[f"ffn_w_up{l}{part}"], f"sum_ffn_w_up{l}{part}") for part in "ab"], axis=0).T
        for l in range(n_ffn)])
    grads["ffn_w_down"] = jnp.stack([_sum_slots(arrived[f"ffn_w_down{l}"], f"sum_ffn_w_down{l}") for l in range(n_ffn)])

    small_grads = {"meta_tokens": d_meta,"sc_conv_w": d_sc_cw[None], "rg_conv_w": d_rg_cw[None],
                   "rg_conv_b": d_rg_cb, "rg_b_gate_a": d_rg_ba, "rg_b_gate_x": d_rg_bx, "rg_lambda": d_rg_lam,
                   "ffn_conv_w": jnp.stack([d_fcw0, d_fcw1])}
    small_chunks = jnp.concatenate([_cols_to_chunks(small_grads[n]) for n in small_names], axis=1)
    pad = small_rows.shape[0] * d - small_chunks.shape[1]
    small_chunks = jnp.pad(small_chunks, ((0, 0), (0, pad))).reshape(N_DEV, small_rows.shape[0], d)
    rep_names = ["norm_mix_g", "norm_ffn_g", "final_norm_g", "rg_w_gate_a", "rg_w_gate_x"]
    rep_grads = {"norm_mix_g": jnp.concatenate([d_mg0, d_mg1], axis=0),
                 "norm_ffn_g": jnp.concatenate([d_fg0, d_fg1], axis=0),
                 "final_norm_g": d_final_g.reshape(-1), "rg_w_gate_a": d_wa[None], "rg_w_gate_x": d_wx[None]}
    rep_flat = jnp.concatenate([rep_grads[n].reshape(-1) for n in rep_names])
    rep_chunk_rows = -(-rep_flat.shape[0] // (N_DEV * d))
    rep_chunk_rows += -(small_rows.shape[0] + rep_chunk_rows) % 16
    rep_chunks = jnp.pad(rep_flat, (0, N_DEV * rep_chunk_rows * d - rep_flat.shape[0])).reshape(N_DEV, rep_chunk_rows, d)

    last_chunks = jnp.concatenate([small_chunks, rep_chunks], axis=1)
    reduced = _sum_slots(_ride_alone(_ScatterRider(last_chunks.reshape(-1, d)), "scatter_last")[0], "sum_last")

    small_red = reduced[0:small_rows.shape[0]].reshape(-1)
    o = small_rows.shape[0]
    so = 0
    for n, sz in zip(small_names, small_sizes):
        grads[n] = small_red[so:so + sz].reshape(weights[n].shape)
        so += sz
    rep_red = _ride_alone(_GatherRider([reduced[o:o + rep_chunks.shape[1]]]), "gather_replicated_grads")[0].reshape(-1)
    ro = 0
    for n in rep_names:
        sz = weights[n].size
        grads[n] = rep_red[ro:ro + sz].reshape(weights[n].shape)
        ro += sz

    delta, new_m, new_v = {}, {}, {}
    for n in names:
        delta[n], new_m[n], new_v[n] = _adamw_nd(weights[n], grads[n], m_in[n], v_in[n], f"adamw_{n}")

    return (loss, grad_x, *[grads[n] for n in names], *[delta[n] for n in names],
            *[new_m[n] for n in names], *[new_v[n] for n in names])
```

```python
import jax
import jax.numpy as jnp
from jax import lax
from jax.experimental import pallas as pl
from jax.experimental.pallas import tpu as pltpu

F32 = jnp.float32
MXU_DTYPE = jnp.bfloat16
RMS_EPS = 1e-6
RG_C = 8.0
ADAM_LR = 0.001
ADAM_B1 = 0.9
ADAM_B2 = 0.999
ADAM_EPS = 1e-08
ADAM_WD = 0.01
ADAM_STEP = 10

N_DEV = 8
AXES = ("x", "y", "c")
SUBLANES = 8
LANES = 128
VMEM_LIMIT_BYTES = 48 * 1024 * 1024
ROW_TILE_MATMUL = 700
ROW_TILE_WGRAD = 3300
ROW_TILE_PERM = 400
STRIP = 256

_TN = (((0,), (0,)), ((), ()))
_NN = (((1,), (0,)), ((), ()))


def _row_tile(t, target):
    best = None
    for tm in range(16, t + 1, 16):
        if t % tm == 0 and tm <= target:
            best = tm
    return best if best is not None else t


def _col_tile(n, target):
    best = None
    for tn in range(LANES, n + 1, LANES):
        if n % tn == 0 and tn <= target:
            best = tn
    return best if best is not None else n


def _params(*sem):
    return pltpu.CompilerParams(dimension_semantics=sem, vmem_limit_bytes=VMEM_LIMIT_BYTES)


def _dot(a, b, dims):
    return lax.dot_general(a, b, dims, preferred_element_type=F32)


def _sigmoid(x):
    return 1.0 / (1.0 + jnp.exp(-x))


def _sigmoid_tanh(x):
    return 0.5 * jnp.tanh(0.5 * x) + 0.5


def _gelu(x):
    c = 0.7978845608028654
    t = jnp.tanh(c * (x + 0.044715 * (x * x * x)))
    return 0.5 * x * (1.0 + t)


def _gelu_and_grad(x):
    c = 0.7978845608028654
    x2 = x * x
    t = jnp.tanh(c * (x + 0.044715 * (x2 * x)))
    half = 0.5 * (1.0 + t)
    return x * half, half + 0.5 * x * (1.0 - t * t) * c * (1.0 + 3.0 * 0.044715 * x2)


def _softplus(x):
    return jnp.maximum(x, 0.0) + jnp.log1p(jnp.exp(-jnp.abs(x)))


def _time_scratch(tm, d):
    return [pltpu.VMEM((2, tm, d), F32), pltpu.VMEM((d // LANES, tm, LANES), F32), pltpu.SemaphoreType.DMA((2,))]


def _fetch_time_tile(i, nt, src_ref, head, tbuf, sems, tm):
    n_head = head.shape[0]

    def tile(j, slot):
        start = pl.multiple_of(j * tm - n_head, SUBLANES)
        return pltpu.make_async_copy(src_ref.at[pl.ds(start, tm)], tbuf.at[slot], sems.at[slot])

    first = pltpu.make_async_copy(src_ref.at[pl.ds(0, tm - n_head)], tbuf.at[0, pl.ds(n_head, tm - n_head)], sems.at[0])

    @pl.when(i == 0)
    def _():
        tbuf[0, 0:n_head, :] = head
        first.start()

    @pl.when(i + 1 < nt)
    def _():
        tile(i + 1, (i + 1) % 2).start()

    @pl.when(i == 0)
    def _():
        first.wait()

    @pl.when(i > 0)
    def _():
        tile(i, i % 2).wait()


def _time_to_tile_order(t_ref, slabs, out_ref):
    tm, d = out_ref.shape
    groups = tm // SUBLANES
    for k in range(d // LANES):
        slabs[k] = t_ref[:, k * LANES:(k + 1) * LANES]
    for k in range(d // LANES):
        for g in range(groups):
            out_ref[g * SUBLANES:(g + 1) * SUBLANES, k * LANES:(k + 1) * LANES] = slabs[k, pl.ds(g, SUBLANES, stride=groups), :]


def _tile_to_time_order(p_ref, slabs, t_ref):
    tm, d = p_ref.shape
    groups = tm // SUBLANES
    for k in range(d // LANES):
        for g in range(groups):
            slabs[k, pl.ds(g, SUBLANES, stride=groups), :] = p_ref[g * SUBLANES:(g + 1) * SUBLANES, k * LANES:(k + 1) * LANES]
    for k in range(d // LANES):
        t_ref[:, k * LANES:(k + 1) * LANES] = slabs[k]


def _rows(ref, sl):
    return [ref[k:k + 1, sl] for k in range(ref.shape[0])]


def _shift_down(x, before, s):
    if s == 0:
        return x
    n = x.shape[0]
    row = lax.broadcasted_iota(jnp.int32, (SUBLANES, x.shape[1]), 0)
    heads = []
    for g in range(s):
        v = x[n - (s - g) * SUBLANES:n - (s - g - 1) * SUBLANES]
        heads.append(pltpu.roll(jnp.where(row == SUBLANES - 1, before[s - g - 1], v), 1, axis=0))
    return jnp.concatenate(heads + [x[0:n - s * SUBLANES]], axis=0)


def _shift_up(x, after, s):
    if s == 0:
        return x
    row = lax.broadcasted_iota(jnp.int32, (SUBLANES, x.shape[1]), 0)
    tails = []
    for m in range(s):
        v = x[m * SUBLANES:(m + 1) * SUBLANES]
        tails.append(pltpu.roll(jnp.where(row == 0, after[m], v), SUBLANES - 1, axis=0))
    return jnp.concatenate([x[s * SUBLANES:]] + tails, axis=0)


def _weighted(w, windows):
    y = w[0] * windows[0]
    for k in range(1, len(w)):
        y = y + w[k] * windows[k]
    return y


def _windows(x, before, k_width):
    return [_shift_down(x, before, k_width - 1 - k) for k in range(k_width)]


def _conv_t(dy, after, w):
    k_width = len(w)
    return _weighted(w, [_shift_up(dy, after, k_width - 1 - k) for k in range(k_width)])


def _blocks(ref, sl, count, newest_first):
    n = ref.shape[0] // SUBLANES
    order = range(n - 1, n - 1 - count, -1) if newest_first else range(count)
    return [ref[b * SUBLANES:(b + 1) * SUBLANES, sl] for b in order]


def _halo_specs(tm, cols, count, tile_of):
    def spec(k):
        return pl.BlockSpec((SUBLANES, cols), lambda i: (jnp.maximum(tile_of(i) * (tm // SUBLANES) - k, 0), 0))
    return [spec(k) for k in range(1, count + 1)]


def _scan_tile(coef, val, out, carry, groups, reverse):
    cols = coef.shape[1]
    row = lax.broadcasted_iota(jnp.int32, (SUBLANES, cols), 0)

    def blk(i):
        g = groups - 1 - i if reverse else i
        return pl.ds(pl.multiple_of(g * SUBLANES, SUBLANES), SUBLANES)

    def local(i, pl_):
        p_prev, l_prev = pl_
        a = coef[blk(i), :]
        p = a * p_prev
        l = a * l_prev + val[blk(i), :]
        coef[blk(i), :] = p
        val[blk(i), :] = l
        return p, l

    pf, lf = lax.fori_loop(0, groups, local, (jnp.ones((SUBLANES, cols), F32), jnp.zeros((SUBLANES, cols), F32)))
    for s in (1, 2, 4):
        keep, sh = (row < SUBLANES - s, SUBLANES - s) if reverse else (row >= s, s)
        p_s = jnp.where(keep, pltpu.roll(pf, sh, axis=0), 1.0)
        l_s = jnp.where(keep, pltpu.roll(lf, sh, axis=0), 0.0)
        lf = pf * l_s + lf
        pf = pf * p_s
    end = lf + pf * carry
    if reverse:
        init = jnp.where(row == SUBLANES - 1, carry, pltpu.roll(end, SUBLANES - 1, axis=0))
        leaving = jnp.broadcast_to(end[0:1, :], (SUBLANES, cols))
    else:
        init = jnp.where(row == 0, carry, pltpu.roll(end, 1, axis=0))
        leaving = jnp.broadcast_to(end[SUBLANES - 1:SUBLANES, :], (SUBLANES, cols))

    def fix(i, _):
        out[blk(i), :] = val[blk(i), :] + coef[blk(i), :] * init
        return 0

    lax.fori_loop(0, groups, fix, 0)
    return leaving


def _resident(shape):
    return pl.BlockSpec(shape, lambda *_: (0,) * len(shape), pipeline_mode=pl.Buffered(1))


def _full(shape):
    return pl.BlockSpec(shape, lambda *_: (0,) * len(shape))


def _rmsnorm_to(h_ref, g_ref, n_ref):
    x = h_ref[...]
    ms = jnp.mean(x * x, axis=-1, keepdims=True)
    n_ref[...] = (x * lax.rsqrt(ms + RMS_EPS) * g_ref[...]).astype(n_ref.dtype)


_ANY = pl.BlockSpec(memory_space=pl.ANY)
_MESH = pl.DeviceIdType.MESH


def _dma_sems(n):
    return pltpu.SemaphoreType.DMA((n,))


def _when_each(*phases):
    for cond, fn in phases:
        if cond is not None:
            pl.when(cond)(fn)


class _NoRider:
    inputs = in_specs = out_shape = out_specs = scratch_shapes = ()

    def __call__(self, first, middle, last, ins, outs, scratch):
        pass


_NO_RIDER = _NoRider()


def _hand_on_step(nt):
    return (3 * nt) // 4


def _split_refs(rider, n_out, n_scratch, rest):
    a = len(rider.inputs)
    b = a + n_out
    c = b + len(rider.out_shape)
    e = c + n_scratch
    return rest[:a], rest[a:b], rest[b:c], rest[c:e], rest[e:]


class _GatherRider:
    def __init__(self, blocks):
        n = len(blocks)
        self.inputs = tuple(blocks)
        self.in_specs = (_ANY,) * n
        self.out_shape = tuple(jax.ShapeDtypeStruct((N_DEV,) + b.shape, b.dtype) for b in blocks)
        self.out_specs = (_ANY,) * n
        self.scratch_shapes = (_dma_sems(7 * n), _dma_sems(7 * n), _dma_sems(n))

    def __call__(self, first, middle, last, ins, outs, scratch):
        n = len(ins)
        send_sems, recv_sems, local_sems = scratch
        x, y, c = lax.axis_index("x"), lax.axis_index("y"), lax.axis_index("c")
        me, sibling = (x, y, c), (x, y, 1 - c)
        chips = [(1 - x, y), (x, 1 - y), (1 - x, 1 - y)]

        def slot(b, px, py, pc):
            return outs[b].at[4 * px + 2 * py + pc]

        def copy(k, b, block, to, src=None):
            return pltpu.make_async_remote_copy(
                src_ref=slot(b, *block) if src is None else src, dst_ref=slot(b, *block),
                send_sem=send_sems.at[k * n + b], recv_sem=recv_sems.at[k * n + b], device_id=to, device_id_type=_MESH)

        mine = [pltpu.make_async_copy(ins[b], slot(b, *me), local_sems.at[b]) for b in range(n)]
        own = [copy(0, b, me, sibling, src=ins[b]) for b in range(n)]
        own += [copy(1 + j, b, me, (*chip, c), src=ins[b]) for j, chip in enumerate(chips) for b in range(n)]
        passed = [[copy(4 + j, b, (*chip, c), sibling) for b in range(n)] for j, chip in enumerate(chips)]

        def at_first():
            for cp in mine + own:
                cp.start()

        def at_middle():
            for j, chip in enumerate(chips):
                for b in range(n):
                    copy(1 + j, b, (*chip, c), me).wait_recv()
                    passed[j][b].start()

        def at_last():
            for b in range(n):
                copy(0, b, sibling, me).wait_recv()
                for j, chip in enumerate(chips):
                    copy(4 + j, b, (*chip, 1 - c), me).wait_recv()
            for cp in own + [cp for group in passed for cp in group]:
                cp.wait_send()
            for cp in mine:
                cp.wait()

        _when_each((first, at_first), (middle, at_middle), (last, at_last))


class _ScatterRider:
    def __init__(self, grad, rows=None):
        r = grad.shape[0] // N_DEV
        chunks = grad.reshape(N_DEV, r, grad.shape[1])
        self.rows = rows if rows is not None else (0, r)
        self.inputs = (chunks,)
        self.in_specs = (_ANY,)
        self.out_shape = (jax.ShapeDtypeStruct((N_DEV, self.rows[1], grad.shape[1]), chunks.dtype),)
        self.out_specs = (_ANY,)
        self.scratch_shapes = (_dma_sems(N_DEV - 1), _dma_sems(N_DEV - 1), pltpu.SemaphoreType.DMA(()))

    def __call__(self, first, middle, last, ins, outs, scratch):
        (g_ref,), (r_ref,) = ins, outs
        send_sems, recv_sems, local_sem = scratch
        x, y, c = lax.axis_index("x"), lax.axis_index("y"), lax.axis_index("c")
        me = 4 * x + 2 * y + c
        part = pl.ds(*self.rows)
        mine = pltpu.make_async_copy(g_ref.at[me, part], r_ref.at[me], local_sem)
        copies = []
        for k in range(1, N_DEV):
            px, py, pc = (1 - x if k & 4 else x), (1 - y if k & 2 else y), (1 - c if k & 1 else c)
            copies.append(pltpu.make_async_remote_copy(
                src_ref=g_ref.at[4 * px + 2 * py + pc, part], dst_ref=r_ref.at[me],
                send_sem=send_sems.at[k - 1], recv_sem=recv_sems.at[k - 1],
                device_id=(px, py, pc), device_id_type=_MESH))

        def at_first():
            mine.start()
            for cp in copies:
                cp.start()

        def at_last():
            for cp in copies:
                cp.wait()
            mine.wait()

        _when_each((first, at_first), (last, at_last))


def _matmul_residual(q, w, h, name):
    t, k = q.shape
    d = w.shape[1]
    tm = _row_tile(t, ROW_TILE_MATMUL)

    def body(q_ref, w_ref, h_ref, o_ref):
        o_ref[...] = h_ref[...] + _dot(q_ref[...], w_ref[...], _NN)

    return pl.pallas_call(
        body, name=name, grid=(t // tm,),
        in_specs=[pl.BlockSpec((tm, k), lambda i: (i, 0)),
                  _resident((k, d)),
                  pl.BlockSpec((tm, d), lambda i: (i, 0))],
        out_specs=pl.BlockSpec((tm, d), lambda i: (i, 0)),
        out_shape=jax.ShapeDtypeStruct((t, d), F32),
        compiler_params=_params("parallel"),
    )(q, w, h)


def _dgrad_in_norm(dp, wt, h, g, dh_next, name, rider=_NO_RIDER):
    t, n = dp.shape
    d = wt.shape[1]
    tm = _row_tile(t, ROW_TILE_MATMUL)
    nt = t // tm

    def body(dp_ref, wt_ref, h_ref, g_ref, dhn_ref, *rest):
        r_in, (dh_ref, dg_ref), r_out, _, r_scratch = _split_refs(rider, 2, 0, rest)
        i = pl.program_id(0)
        rider(i == 0, i == nt // 2, None, r_in, r_out, r_scratch)
        dn = _dot(dp_ref[...], wt_ref[...], _NN)
        x = h_ref[...]
        ms = jnp.mean(x * x, axis=-1, keepdims=True)
        r = lax.rsqrt(ms + RMS_EPS)
        xhat = x * r
        dng = dn * g_ref[...]
        c = jnp.mean(dng * xhat, axis=-1, keepdims=True)
        dh_ref[...] = dhn_ref[...] + r * (dng - xhat * c)
        part = jnp.sum(dn * xhat, axis=0, keepdims=True)

        @pl.when(i == 0)
        def _():
            dg_ref[...] = part

        @pl.when(i > 0)
        def _():
            dg_ref[...] += part

        rider(None, None, i == nt - 1, r_in, r_out, r_scratch)

    return pl.pallas_call(
        body, name=name, grid=(nt,),
        in_specs=[pl.BlockSpec((tm, n), lambda i: (i, 0)),
                  _resident((n, d)),
                  pl.BlockSpec((tm, d), lambda i: (i, 0)),
                  pl.BlockSpec((1, d), lambda i: (0, 0)),
                  pl.BlockSpec((tm, d), lambda i: (i, 0))] + list(rider.in_specs),
        out_specs=[pl.BlockSpec((tm, d), lambda i: (i, 0)),
                   pl.BlockSpec((1, d), lambda i: (0, 0))] + list(rider.out_specs),
        out_shape=[jax.ShapeDtypeStruct((t, d), F32), jax.ShapeDtypeStruct((1, d), F32)] + list(rider.out_shape),
        scratch_shapes=list(rider.scratch_shapes),
        compiler_params=_params("arbitrary"),
    )(dp, wt, h, g, dh_next, *rider.inputs)


def _dgrad_in_to_input(dp, wt, h, g, dh_next, n_head, tm, name, rider=_NO_RIDER):
    t, n = dp.shape
    d = wt.shape[1]
    nt = t // tm

    def body(dp_ref, wt_ref, h_ref, g_ref, dhn_ref, *rest):
        r_in, (dg_ref, head_ref, rest_ref), r_out, (pbuf, tout, slabs, sems), r_scratch = _split_refs(rider, 3, 4, rest)
        i = pl.program_id(0)
        rider(i == 0, i == nt // 2, None, r_in, r_out, r_scratch)
        dn = _dot(dp_ref[...], wt_ref[...], _NN)
        x = h_ref[...]
        ms = jnp.mean(x * x, axis=-1, keepdims=True)
        r = lax.rsqrt(ms + RMS_EPS)
        xhat = x * r
        dng = dn * g_ref[...]
        c = jnp.mean(dng * xhat, axis=-1, keepdims=True)
        pbuf[...] = dhn_ref[...] + r * (dng - xhat * c)
        part = jnp.sum(dn * xhat, axis=0, keepdims=True)

        @pl.when(i == 0)
        def _():
            dg_ref[...] = part

        @pl.when(i > 0)
        def _():
            dg_ref[...] += part

        def store(j):
            if isinstance(j, int) and j == 0:
                return pltpu.make_async_copy(tout.at[0, pl.ds(n_head, tm - n_head)],
                                             rest_ref.at[pl.ds(0, tm - n_head)], sems.at[0])
            start = j * tm - n_head
            start = start if isinstance(start, int) else pl.multiple_of(start, SUBLANES)
            return pltpu.make_async_copy(tout.at[j % 2], rest_ref.at[pl.ds(start, tm)], sems.at[j % 2])

        @pl.when(i == 2)
        def _():
            store(0).wait()

        @pl.when(i > 2)
        def _():
            store(i - 2).wait()

        _tile_to_time_order(pbuf, slabs, tout.at[i % 2])

        @pl.when(i == 0)
        def _():
            head_ref[...] = tout[0, 0:n_head, :]
            store(0).start()

        @pl.when(i > 0)
        def _():
            store(i).start()

        @pl.when(i == nt - 1)
        def _():
            for j in (nt - 2, nt - 1):
                if j >= 0:
                    store(j).wait()

        rider(None, None, i == nt - 1, r_in, r_out, r_scratch)

    return pl.pallas_call(
        body, name=name, grid=(nt,),
        in_specs=[pl.BlockSpec((tm, n), lambda i: (i, 0)),
                  _resident((n, d)),
                  pl.BlockSpec((tm, d), lambda i: (i, 0)),
                  pl.BlockSpec((1, d), lambda i: (0, 0)),
                  pl.BlockSpec((tm, d), lambda i: (i, 0))] + list(rider.in_specs),
        out_specs=[pl.BlockSpec((1, d), lambda i: (0, 0)), _full((n_head, d)), _ANY] + list(rider.out_specs),
        out_shape=[jax.ShapeDtypeStruct((1, d), F32), jax.ShapeDtypeStruct((n_head, d), F32),
                   jax.ShapeDtypeStruct((t - n_head, d), F32)] + list(rider.out_shape),
        scratch_shapes=[pltpu.VMEM((tm, d), F32)] + _time_scratch(tm, d) + list(rider.scratch_shapes),
        compiler_params=_params("arbitrary"),
    )(dp, wt, h, g, dh_next, *rider.inputs)


def _wgrad(a, b, name, rider=_NO_RIDER):
    t, m = a.shape
    d = b.shape[1]
    tmm, tk = _col_tile(m, 1024), _row_tile(t, ROW_TILE_WGRAD)
    if tmm < 512:
        tmm, tk = _col_tile(m, 1536), _row_tile(t, ROW_TILE_MATMUL)
    nm, nk = m // tmm, t // tk

    def body(a_ref, b_ref, *rest):
        r_in, (o_ref,), r_out, _, r_scratch = _split_refs(rider, 1, 0, rest)
        i, k = pl.program_id(0), pl.program_id(1)
        rider(jnp.logical_and(i == 0, k == 0), None, None, r_in, r_out, r_scratch)

        @pl.when(k == 0)
        def _():
            o_ref[...] = jnp.zeros_like(o_ref)

        o_ref[...] += _dot(a_ref[...], b_ref[...], _TN)
        rider(None, None, jnp.logical_and(i == nm - 1, k == nk - 1), r_in, r_out, r_scratch)

    outs = pl.pallas_call(
        body, name=name, grid=(nm, nk),
        in_specs=[pl.BlockSpec((tk, tmm), lambda i, k: (k, i)),
                  pl.BlockSpec((tk, d), lambda i, k: (k, 0))] + list(rider.in_specs),
        out_specs=[pl.BlockSpec((tmm, d), lambda i, k: (i, 0))] + list(rider.out_specs),
        out_shape=[jax.ShapeDtypeStruct((m, d), F32)] + list(rider.out_shape),
        scratch_shapes=list(rider.scratch_shapes),
        compiler_params=_params("arbitrary", "arbitrary"),
    )(a, b, *rider.inputs)
    return outs if rider.out_shape else outs[0]


def _ffn_up_fused(h, g, w, conv_w, tm, name, rider=_NO_RIDER):
    t, d = h.shape
    f2 = w.shape[1]
    f = f2 // 2
    cw = min(STRIP, f)
    kw = conv_w.shape[0]
    nh = kw - 1
    nt = t // tm

    def body(h_ref, g_ref, w_ref, cw_ref, *rest):
        r_in, (n_ref, a_ref, z_ref), r_out, (carry,), r_scratch = _split_refs(rider, 3, 1, rest)
        i = pl.program_id(0)
        rider(i == 0, i == _hand_on_step(nt), None, r_in, r_out, r_scratch)

        @pl.when(i == 0)
        def _():
            carry[...] = jnp.zeros_like(carry)

        _rmsnorm_to(h_ref, g_ref, n_ref)
        for c in range(0, f, cw):
            conv = []
            for sl in (slice(c, c + cw), slice(f + c, f + c + cw)):
                a = _dot(n_ref[...], w_ref[:, sl], _NN)
                a_ref[:, sl] = a
                conv.append(_weighted(_rows(cw_ref, sl), _windows(a, _blocks(carry, sl, nh, True), kw)))
                carry[:, sl] = a[tm - nh * SUBLANES:tm]
            gg, vv = conv
            z_ref[:, c:c + cw] = (gg * _sigmoid(gg) * vv).astype(z_ref.dtype)
        rider(None, None, i == nt - 1, r_in, r_out, r_scratch)

    return pl.pallas_call(
        body, name=name, grid=(nt,),
        in_specs=[pl.BlockSpec((tm, d), lambda i: (i, 0)), pl.BlockSpec((1, d), lambda i: (0, 0)),
                  _resident((d, f2)), _full(conv_w.shape)] + list(rider.in_specs),
        out_specs=[pl.BlockSpec((tm, d), lambda i: (i, 0)), pl.BlockSpec((tm, f2), lambda i: (i, 0)),
                   pl.BlockSpec((tm, f), lambda i: (i, 0))] + list(rider.out_specs),
        out_shape=[jax.ShapeDtypeStruct((t, d), MXU_DTYPE), jax.ShapeDtypeStruct((t, f2), F32),
                   jax.ShapeDtypeStruct((t, f), MXU_DTYPE)] + list(rider.out_shape),
        scratch_shapes=[pltpu.VMEM((nh * SUBLANES, f2), F32)] + list(rider.scratch_shapes),
        compiler_params=_params("arbitrary"),
    )(h, g, w, conv_w, *rider.inputs)


def _ffn_down_bwd_fused(dh, wdt, a, conv_w, tm, name):
    t, d = dh.shape
    f = wdt.shape[1]
    f2 = 2 * f
    nt = t // tm
    cw = min(STRIP, f)
    kw = conv_w.shape[0]
    nh = kw - 1
    rev = lambda i: nt - 1 - i

    def body(dh_ref, wdt_ref, a_ref, *rest):
        halo_refs = rest[:nh]
        cw_ref, da_ref, dhb_ref, dw_ref, carry = rest[nh:]
        i = pl.program_id(0)
        newest, oldest = i == 0, i == nt - 1

        @pl.when(newest)
        def _():
            carry[...] = jnp.zeros_like(carry)
            dw_ref[...] = jnp.zeros_like(dw_ref)

        dhb_ref[...] = dh_ref[...].astype(dhb_ref.dtype)
        for c in range(0, f, cw):
            gsl, vsl = slice(c, c + cw), slice(f + c, f + c + cw)
            dz = _dot(dhb_ref[...], wdt_ref[:, gsl], _NN)
            win, conv = {}, {}
            for sl in (gsl, vsl):
                before = [jnp.where(oldest, 0.0, r[:, sl]) for r in halo_refs]
                win[sl.start] = _windows(a_ref[:, sl], before, kw)
                conv[sl.start] = _weighted(_rows(cw_ref, sl), win[sl.start])
            gg, vv = conv[gsl.start], conv[vsl.start]
            s = _sigmoid(gg)
            grads = {gsl.start: dz * vv * s * (1.0 + gg * (1.0 - s)), vsl.start: dz * gg * s}
            for sl in (gsl, vsl):
                dconv = grads[sl.start]
                da_ref[:, sl] = _conv_t(dconv, _blocks(carry, sl, nh, False), _rows(cw_ref, sl)).astype(da_ref.dtype)
                for k in range(kw):
                    dw_ref[k:k + 1, sl] += jnp.sum(dconv * win[sl.start][k], axis=0, keepdims=True)
                carry[:, sl] = dconv[0:nh * SUBLANES]

    return pl.pallas_call(
        body, name=name, grid=(nt,),
        in_specs=[pl.BlockSpec((tm, d), lambda i: (rev(i), 0)), _resident((d, f)),
                  pl.BlockSpec((tm, f2), lambda i: (rev(i), 0))] + _halo_specs(tm, f2, nh, rev) + [_full(conv_w.shape)],
        out_specs=[pl.BlockSpec((tm, f2), lambda i: (rev(i), 0)), pl.BlockSpec((tm, d), lambda i: (rev(i), 0)),
                   _full(conv_w.shape)],
        out_shape=[jax.ShapeDtypeStruct((t, f2), MXU_DTYPE), jax.ShapeDtypeStruct((t, d), MXU_DTYPE),
                   jax.ShapeDtypeStruct(conv_w.shape, F32)],
        scratch_shapes=[pltpu.VMEM((nh * SUBLANES, f2), F32)],
        compiler_params=_params("arbitrary"),
    )(dh, wdt, a, *([a] * nh), conv_w)


def _sc_in_fused(x, meta, g, w, conv_w, tm, name, rider=_NO_RIDER):
    d = x.shape[1]
    t = x.shape[0] + meta.shape[0]
    cw = min(STRIP, d)
    kw = conv_w.shape[0]
    nh = kw - 1
    nt = t // tm

    def body(x_ref, meta_ref, g_ref, w_ref, cw_ref, *rest):
        r_in, (h_ref, n_ref, p_ref, q_ref), r_out, (carry, tbuf, slabs, sems), r_scratch = _split_refs(rider, 4, 4, rest)
        i = pl.program_id(0)
        rider(i == 0, i == _hand_on_step(nt), None, r_in, r_out, r_scratch)

        @pl.when(i == 0)
        def _():
            carry[...] = jnp.zeros_like(carry)

        _fetch_time_tile(i, nt, x_ref, meta_ref[...], tbuf, sems, tm)
        _time_to_tile_order(tbuf.at[i % 2], slabs, h_ref)
        _rmsnorm_to(h_ref, g_ref, n_ref)
        for c in range(0, d, cw):
            sl = slice(c, c + cw)
            parts = []
            for base in (0, d, 2 * d):
                psl = slice(base + c, base + c + cw)
                parts.append(_dot(n_ref[...], w_ref[:, psl], _NN))
                p_ref[:, psl] = parts[-1]
            bg, cg, v = parts
            cv = cg * v
            u = _weighted(_rows(cw_ref, sl), _windows(cv, _blocks(carry, sl, nh, True), kw))
            carry[:, sl] = cv[tm - nh * SUBLANES:tm]
            q_ref[:, sl] = (bg * u).astype(q_ref.dtype)
        rider(None, None, i == nt - 1, r_in, r_out, r_scratch)

    row_tile = lambda cols: pl.BlockSpec((tm, cols), lambda i: (i, 0))
    return pl.pallas_call(
        body, name=name, grid=(nt,),
        in_specs=[_ANY, _full(meta.shape), pl.BlockSpec((1, d), lambda i: (0, 0)),
                  _resident((d, 3 * d)), _full(conv_w.shape)] + list(rider.in_specs),
        out_specs=[row_tile(d), row_tile(d), row_tile(3 * d), row_tile(d)] + list(rider.out_specs),
        out_shape=[jax.ShapeDtypeStruct((t, d), F32), jax.ShapeDtypeStruct((t, d), MXU_DTYPE),
                   jax.ShapeDtypeStruct((t, 3 * d), F32), jax.ShapeDtypeStruct((t, d), MXU_DTYPE)]
                  + list(rider.out_shape),
        scratch_shapes=[pltpu.VMEM((nh * SUBLANES, d), F32)] + _time_scratch(tm, d) + list(rider.scratch_shapes),
        compiler_params=_params("arbitrary"),
    )(x, meta, g, w, conv_w, *rider.inputs)


def _sc_out_bwd_fused(dh, wot, p, conv_w, tm, name, rider=_NO_RIDER):
    t, d = dh.shape
    nt = t // tm
    cw = min(STRIP, d)
    kw = conv_w.shape[0]
    nh = kw - 1
    rev = lambda i: nt - 1 - i

    def body(dh_ref, wot_ref, p_ref, *rest):
        halo_refs, cw_ref = rest[:nh], rest[nh]
        r_in, (dp_ref, dhb_ref, dw_ref), r_out, (carry,), r_scratch = _split_refs(rider, 3, 1, rest[nh + 1:])
        i = pl.program_id(0)
        newest, oldest = i == 0, i == nt - 1
        rider(newest, i == nt // 2, None, r_in, r_out, r_scratch)

        @pl.when(newest)
        def _():
            carry[...] = jnp.zeros_like(carry)
            dw_ref[...] = jnp.zeros_like(dw_ref)

        dhb_ref[...] = dh_ref[...].astype(dhb_ref.dtype)
        for c in range(0, d, cw):
            sl, csl, vsl = slice(c, c + cw), slice(d + c, d + c + cw), slice(2 * d + c, 2 * d + c + cw)
            w = _rows(cw_ref, sl)
            dq = _dot(dhb_ref[...], wot_ref[:, sl], _NN)
            bg, cg, v = p_ref[:, sl], p_ref[:, csl], p_ref[:, vsl]
            before = [jnp.where(oldest, 0.0, r[:, csl] * r[:, vsl]) for r in halo_refs]
            win = _windows(cg * v, before, kw)
            du = dq * bg
            dcv = _conv_t(du, _blocks(carry, sl, nh, False), w)
            dp_ref[:, sl] = (dq * _weighted(w, win)).astype(dp_ref.dtype)
            dp_ref[:, csl] = (dcv * v).astype(dp_ref.dtype)
            dp_ref[:, vsl] = (dcv * cg).astype(dp_ref.dtype)
            for k in range(kw):
                dw_ref[k:k + 1, sl] += jnp.sum(du * win[k], axis=0, keepdims=True)
            carry[:, sl] = du[0:nh * SUBLANES]
        rider(None, None, oldest, r_in, r_out, r_scratch)

    tile = lambda cols: pl.BlockSpec((tm, cols), lambda i: (rev(i), 0))
    return pl.pallas_call(
        body, name=name, grid=(nt,),
        in_specs=[tile(d), _resident((d, d)), tile(3 * d)] + _halo_specs(tm, 3 * d, nh, rev) + [_full(conv_w.shape)]
                 + list(rider.in_specs),
        out_specs=[tile(3 * d), tile(d), _full(conv_w.shape)] + list(rider.out_specs),
        out_shape=[jax.ShapeDtypeStruct((t, 3 * d), MXU_DTYPE), jax.ShapeDtypeStruct((t, d), MXU_DTYPE),
                   jax.ShapeDtypeStruct(conv_w.shape, F32)] + list(rider.out_shape),
        scratch_shapes=[pltpu.VMEM((nh * SUBLANES, d), F32)] + list(rider.scratch_shapes),
        compiler_params=_params("arbitrary"),
    )(dh, wot, p, *([p] * nh), conv_w, *rider.inputs)


def _pair_gate_weights(wa, wx):
    nb, bd, _ = wa.shape
    zero = jnp.zeros((nb // 2, bd, bd), wa.dtype)

    def pair(w):
        w = w.reshape(nb // 2, 2, bd, bd)
        top = jnp.concatenate([w[:, 0], zero], axis=2)
        bottom = jnp.concatenate([zero, w[:, 1]], axis=2)
        return jnp.concatenate([top, bottom], axis=1)

    both = jnp.concatenate([pair(wa), pair(wx)], axis=2)
    return both, jnp.swapaxes(both, 1, 2)


def _unpair_gate_grads(dw, bd):
    def blocks(cols):
        d0 = dw[:, :bd, cols:cols + bd]
        d1 = dw[:, bd:, cols + bd:cols + 2 * bd]
        return jnp.stack([d0, d1], axis=1).reshape(-1, bd, bd)
    return blocks(0), blocks(2 * bd)


def _rg_gates(u, k, wg_ref, ba, bx, lam):
    ub = u.astype(MXU_DTYPE)
    z = _dot(ub, wg_ref[k], _NN)
    half = z.shape[1] // 2
    r = _sigmoid_tanh(z[:, :half] + ba)
    ig = _sigmoid_tanh(z[:, half:] + bx)
    sp = _softplus(-lam)
    la = -RG_C * r * sp
    a = jnp.exp(la)
    th = jnp.tanh(la)
    mult = jnp.sqrt(-2.0 * th / (1.0 - th))
    return r, ig, a, mult


def _rg_in_fused(h, g, w, conv_w, conv_b, wg, ba, bx, lam, tm, name, rider=_NO_RIDER):
    t, d = h.shape
    r2 = w.shape[1]
    rr = r2 // 2
    nb, bd = wg.shape[0], wg.shape[1]
    kw = conv_w.shape[0]
    nh = kw - 1
    groups = tm // SUBLANES
    nt = t // tm

    def body(h_ref, g_ref, w_ref, cw_ref, cb_ref, wg_ref, ba_ref, bx_ref, lam_ref, *rest):
        r_in, (n_ref, p_ref, y_ref, hs_ref), r_out, (a_scr, b_scr, carry_rb, carry_h), r_scratch = _split_refs(
            rider, 4, 4, rest)
        i = pl.program_id(0)
        rider(i == 0, i == _hand_on_step(nt), None, r_in, r_out, r_scratch)

        @pl.when(i == 0)
        def _():
            carry_rb[...] = jnp.zeros_like(carry_rb)
            carry_h[...] = jnp.zeros_like(carry_h)

        _rmsnorm_to(h_ref, g_ref, n_ref)
        for k in range(nb):
            sl = slice(k * bd, (k + 1) * bd)
            rsl = slice(rr + k * bd, rr + (k + 1) * bd)
            p_ref[:, sl] = _dot(n_ref[...], w_ref[:, sl], _NN)
            rb = _dot(n_ref[...], w_ref[:, rsl], _NN)
            p_ref[:, rsl] = rb
            u = _weighted(_rows(cw_ref, sl), _windows(rb, _blocks(carry_rb, sl, nh, True), kw)) + cb_ref[:, sl]
            carry_rb[:, sl] = rb[tm - nh * SUBLANES:tm]
            _, ig, a, mult = _rg_gates(u, k, wg_ref, ba_ref[:, sl], bx_ref[:, sl], lam_ref[:, sl])
            a_scr[:, sl] = a
            b_scr[:, sl] = mult * (ig * u)

        carry_h[...] = _scan_tile(a_scr, b_scr, hs_ref, carry_h[...], groups, reverse=False)

        for k in range(nb):
            sl = slice(k * bd, (k + 1) * bd)
            y_ref[:, sl] = (hs_ref[:, sl] * _gelu(p_ref[:, sl])).astype(y_ref.dtype)
        rider(None, None, i == nt - 1, r_in, r_out, r_scratch)

    vm = lambda rows: pltpu.VMEM((rows, rr), F32)
    return pl.pallas_call(
        body, name=name, grid=(nt,),
        in_specs=[pl.BlockSpec((tm, d), lambda i: (i, 0)), pl.BlockSpec((1, d), lambda i: (0, 0)),
                  _resident((d, r2)), _full(conv_w.shape), _full(conv_b.shape), _full(wg.shape), _full(ba.shape),
                  _full(bx.shape), _full(lam.shape)] + list(rider.in_specs),
        out_specs=[pl.BlockSpec((tm, d), lambda i: (i, 0)), pl.BlockSpec((tm, r2), lambda i: (i, 0)),
                   pl.BlockSpec((tm, rr), lambda i: (i, 0)), pl.BlockSpec((tm, rr), lambda i: (i, 0))]
                  + list(rider.out_specs),
        out_shape=[jax.ShapeDtypeStruct((t, d), MXU_DTYPE), jax.ShapeDtypeStruct((t, r2), F32),
                   jax.ShapeDtypeStruct((t, rr), MXU_DTYPE), jax.ShapeDtypeStruct((t, rr), F32)]
                  + list(rider.out_shape),
        scratch_shapes=[vm(tm), vm(tm), vm(nh * SUBLANES), vm(SUBLANES)] + list(rider.scratch_shapes),
        compiler_params=_params("arbitrary"),
    )(h, g, w, conv_w, conv_b, wg, ba, bx, lam, *rider.inputs)


def _rg_out_bwd_fused(dh, wot, p, hs, conv_w, conv_b, wg, wgt, ba, bx, lam, tm, name, rider=_NO_RIDER):
    t, d = dh.shape
    r2 = p.shape[1]
    rr = r2 // 2
    nb, bd = wg.shape[0], wg.shape[1]
    nt = t // tm
    kw = conv_w.shape[0]
    nh = kw - 1
    groups = tm // SUBLANES
    rev = lambda i: nt - 1 - i

    def body(dh_ref, wot_ref, p_ref, hs_ref, hsh_ref, *rest):
        halo_refs = rest[:nh]
        cw_ref, cb_ref, wg_ref, wgt_ref, ba_ref, bx_ref, lam_ref = rest[nh:nh + 7]
        (r_in, (dp_ref, dhb_ref, dcw_ref, dcb_ref, dwg_ref, dba_ref, dbx_ref, dlam_ref), r_out,
         (a_scr, as_scr, u_scr, r_scr, ig_scr, mult_scr, g_scr, carry_g, carry_du), r_scratch) = _split_refs(
            rider, 8, 9, rest[nh + 7:])
        i = pl.program_id(0)
        newest, oldest = i == 0, i == nt - 1
        rider(newest, i == nt // 2, None, r_in, r_out, r_scratch)

        @pl.when(newest)
        def _():
            carry_g[...] = jnp.zeros_like(carry_g)
            carry_du[...] = jnp.zeros_like(carry_du)
            for ref in (dcw_ref, dcb_ref, dwg_ref, dba_ref, dbx_ref, dlam_ref):
                ref[...] = jnp.zeros_like(ref)

        dhb_ref[...] = dh_ref[...].astype(dhb_ref.dtype)
        ones = jnp.ones((SUBLANES, bd), F32)

        def conv_windows(sl, rsl):
            before = [jnp.where(oldest, 0.0, r[:, rsl]) for r in halo_refs]
            return _windows(p_ref[:, rsl], before, kw)

        for k in range(nb):
            sl = slice(k * bd, (k + 1) * bd)
            rsl = slice(rr + k * bd, rr + (k + 1) * bd)
            dy = _dot(dhb_ref[...], wot_ref[:, sl], _NN)
            u = _weighted(_rows(cw_ref, sl), conv_windows(sl, rsl)) + cb_ref[:, sl]
            r, ig, a, mult = _rg_gates(u, k, wg_ref, ba_ref[:, sl], bx_ref[:, sl], lam_ref[:, sl])
            gate, dgate = _gelu_and_grad(p_ref[:, sl])
            g_scr[:, sl] = dy * gate
            dp_ref[:, sl] = (dy * hs_ref[:, sl] * dgate).astype(dp_ref.dtype)
            a_scr[:, sl] = a
            as_scr[:, sl] = _shift_up(a, [ones], 1)
            u_scr[:, sl] = u
            r_scr[:, sl] = r
            ig_scr[:, sl] = ig
            mult_scr[:, sl] = mult

        g_first = _scan_tile(as_scr, g_scr, g_scr, carry_g[...], groups, reverse=True)
        carry_g[...] = jnp.broadcast_to(a_scr[0:1, :], (SUBLANES, rr)) * g_first

        for k in range(nb):
            sl = slice(k * bd, (k + 1) * bd)
            rsl = slice(rr + k * bd, rr + (k + 1) * bd)
            cw = _rows(cw_ref, sl)
            lam_k = lam_ref[:, sl]
            sp = _softplus(-lam_k)
            g = g_scr[:, sl]
            a, u, r, ig, mult = a_scr[:, sl], u_scr[:, sl], r_scr[:, sl], ig_scr[:, sl], mult_scr[:, sl]
            da = g * _shift_down(hs_ref[:, sl], [jnp.where(oldest, 0.0, hsh_ref[:, sl])], 1)
            dmult = g * (ig * u)
            d_iu = g * mult
            dla = da * a - dmult * (a * a) / mult
            dr = dla * (-RG_C * sp)
            dsp = jnp.sum(dla * (-RG_C * r), axis=0, keepdims=True)
            dlam_ref[:, sl] += dsp * (-_sigmoid(-lam_k))
            dza = dr * r * (1.0 - r)
            dzx = (d_iu * u) * ig * (1.0 - ig)
            dba_ref[:, sl] += jnp.sum(dza, axis=0, keepdims=True)
            dbx_ref[:, sl] += jnp.sum(dzx, axis=0, keepdims=True)
            ub = u.astype(MXU_DTYPE)
            dz = jnp.concatenate([dza, dzx], axis=1).astype(MXU_DTYPE)
            dwg_ref[k] += _dot(ub, dz, _TN)
            du = d_iu * ig + _dot(dz, wgt_ref[k], _NN)
            dcb_ref[:, sl] += jnp.sum(du, axis=0, keepdims=True)
            win = conv_windows(sl, rsl)
            for kk in range(kw):
                dcw_ref[kk:kk + 1, sl] += jnp.sum(du * win[kk], axis=0, keepdims=True)
            dp_ref[:, rsl] = _conv_t(du, _blocks(carry_du, sl, nh, False), cw).astype(dp_ref.dtype)
            carry_du[:, sl] = du[0:nh * SUBLANES]
        rider(None, None, oldest, r_in, r_out, r_scratch)

    tile = lambda cols: pl.BlockSpec((tm, cols), lambda i: (rev(i), 0))
    vm = lambda rows: pltpu.VMEM((rows, rr), F32)
    grads = [conv_w.shape, conv_b.shape, wg.shape, ba.shape, bx.shape, lam.shape]
    return pl.pallas_call(
        body, name=name, grid=(nt,),
        in_specs=[tile(d), _resident((d, rr)), tile(r2), tile(rr)] + _halo_specs(tm, rr, 1, rev)
                 + _halo_specs(tm, r2, nh, rev)
                 + [_full(conv_w.shape), _full(conv_b.shape), _full(wg.shape), _full(wgt.shape), _full(ba.shape),
                    _full(bx.shape), _full(lam.shape)] + list(rider.in_specs),
        out_specs=[tile(r2), tile(d)] + [_full(s) for s in grads] + list(rider.out_specs),
        out_shape=[jax.ShapeDtypeStruct((t, r2), MXU_DTYPE), jax.ShapeDtypeStruct((t, d), MXU_DTYPE)]
                  + [jax.ShapeDtypeStruct(s, F32) for s in grads] + list(rider.out_shape),
        scratch_shapes=[vm(tm)] * 7 + [vm(SUBLANES), vm(nh * SUBLANES)] + list(rider.scratch_shapes),
        compiler_params=_params("arbitrary"),
    )(dh, wot, p, hs, hs, *([p] * nh), conv_w, conv_b, wg, wgt, ba, bx, lam, *rider.inputs)


def _down_loss_head(z, w, h, g, target, n_meta, tm, name):
    t, d = h.shape
    k = z.shape[1]
    groups = tm // SUBLANES
    nt = t // tm

    def body(z_ref, w_ref, h_ref, g_ref, t_ref, dh_ref, loss_ref, dg_ref, tgt, tbuf, slabs, sems):
        i = pl.program_id(0)
        _fetch_time_tile(i, nt, t_ref, jnp.zeros((n_meta, d), F32), tbuf, sems, tm)
        _time_to_tile_order(tbuf.at[i % 2], slabs, tgt)
        x = h_ref[...] + _dot(z_ref[...], w_ref[...], _NN)
        ms = jnp.mean(x * x, axis=-1, keepdims=True)
        r = lax.rsqrt(ms + RMS_EPS)
        xhat = x * r
        gg = g_ref[...]
        row = lax.broadcasted_iota(jnp.int32, (tm, 1), 0)
        time = i * tm + jnp.right_shift(row, 3) + jnp.bitwise_and(row, SUBLANES - 1) * groups
        err = jnp.where(time >= n_meta, xhat * gg - tgt[...], 0.0)
        dout = err * (1.0 / d)
        dng = dout * gg
        c = jnp.mean(dng * xhat, axis=-1, keepdims=True)
        dh_ref[...] = r * (dng - xhat * c)
        part_dg = jnp.sum(dout * xhat, axis=0, keepdims=True)
        part_loss = jnp.broadcast_to(0.5 * jnp.sum(err * dout, keepdims=True), loss_ref.shape)

        @pl.when(i == 0)
        def _():
            dg_ref[...] = part_dg
            loss_ref[...] = part_loss

        @pl.when(i > 0)
        def _():
            dg_ref[...] += part_dg
            loss_ref[...] += part_loss

    return pl.pallas_call(
        body, name=name, grid=(nt,),
        in_specs=[pl.BlockSpec((tm, k), lambda i: (i, 0)),
                  _resident((k, d)),
                  pl.BlockSpec((tm, d), lambda i: (i, 0)),
                  pl.BlockSpec((1, d), lambda i: (0, 0)),
                  _ANY],
        out_specs=[pl.BlockSpec((tm, d), lambda i: (i, 0)),
                   pl.BlockSpec((SUBLANES, LANES), lambda i: (0, 0)),
                   pl.BlockSpec((1, d), lambda i: (0, 0))],
        out_shape=[jax.ShapeDtypeStruct((t, d), F32), jax.ShapeDtypeStruct((SUBLANES, LANES), F32),
                   jax.ShapeDtypeStruct((1, d), F32)],
        scratch_shapes=[pltpu.VMEM((tm, d), F32)] + _time_scratch(tm, d),
        compiler_params=_params("arbitrary"),
    )(z, w, h, g, target)


def _adamw(w, g, m, v, name):
    lead, rows, cols = w.shape
    tr = rows
    if rows > 512:
        for cand in range(8, 513, 8):
            if rows % cand == 0:
                tr = cand

    def body(w_ref, g_ref, m_ref, v_ref, d_ref, nm_ref, nv_ref):
        g_ = g_ref[...]
        m_ = ADAM_B1 * m_ref[...] + (1.0 - ADAM_B1) * g_
        v_ = ADAM_B2 * v_ref[...] + (1.0 - ADAM_B2) * (g_ * g_)
        m_hat = m_ / (1.0 - ADAM_B1 ** ADAM_STEP)
        v_hat = v_ / (1.0 - ADAM_B2 ** ADAM_STEP)
        d_ref[...] = -ADAM_LR * (m_hat / (jnp.sqrt(v_hat) + ADAM_EPS) + ADAM_WD * w_ref[...])
        nm_ref[...] = m_
        nv_ref[...] = v_

    spec = pl.BlockSpec((None, tr, cols), lambda l, i: (l, i, 0))
    shape = jax.ShapeDtypeStruct((lead, rows, cols), F32)
    return pl.pallas_call(
        body, name=name, grid=(lead, rows // tr),
        in_specs=[spec] * 4, out_specs=[spec] * 3, out_shape=[shape] * 3,
        compiler_params=_params("parallel", "parallel"),
    )(w, g, m, v)


def _adamw_nd(w, g, m, v, name):
    shape = w.shape
    three_d = (-1,) + shape[-2:] if w.ndim > 1 else (1, 1, -1)
    outs = _adamw(w.reshape(three_d), g.reshape(three_d), m.reshape(three_d), v.reshape(three_d), name)
    return tuple(o.reshape(shape) for o in outs)


def _ride_alone(rider, name):
    def body(*refs):
        r_in, _, r_out, _, r_scratch = _split_refs(rider, 0, 0, refs)
        now = pl.program_id(0) == 0
        rider(now, None, None, r_in, r_out, r_scratch)
        rider(None, now, None, r_in, r_out, r_scratch)
        rider(None, None, now, r_in, r_out, r_scratch)

    return pl.pallas_call(
        body, name=name, grid=(1,),
        in_specs=list(rider.in_specs), out_specs=list(rider.out_specs), out_shape=list(rider.out_shape),
        scratch_shapes=list(rider.scratch_shapes),
        compiler_params=_params("arbitrary"),
    )(*rider.inputs)


def _sum_slots(parts, name):
    slots, rows, cols = parts.shape
    tr = _row_tile(rows, 256) if rows % 16 == 0 else rows

    def body(p_ref, o_ref):
        acc = p_ref[0]
        for s in range(1, slots):
            acc = acc + p_ref[s]
        o_ref[...] = acc

    return pl.pallas_call(
        body, name=name, grid=(rows // tr,),
        in_specs=[pl.BlockSpec((slots, tr, cols), lambda i: (0, i, 0))],
        out_specs=pl.BlockSpec((tr, cols), lambda i: (i, 0)),
        out_shape=jax.ShapeDtypeStruct((rows, cols), parts.dtype),
        compiler_params=_params("parallel"),
    )(parts)


def _pad_rows(flat, cols, multiple):
    n = flat.shape[0]
    rows = -(-n // cols)
    rows = -(-rows // multiple) * multiple
    return jnp.pad(flat, (0, rows * cols - n)).reshape(rows, cols)


def _cols_to_chunks(full):
    lead = full.shape[:-1]
    c = full.shape[-1] // N_DEV
    x = full.reshape(-1, N_DEV, c)
    return jnp.transpose(x, (1, 0, 2)).reshape(N_DEV, -1)


def _chunks_to_cols(chunks, lead):
    n = 1
    for s in lead:
        n *= s
    c = chunks.shape[1] // n
    x = chunks.reshape(N_DEV, n, c)
    return jnp.transpose(x, (1, 0, 2)).reshape(tuple(lead) + (N_DEV * c,))


def kernel(x, meta_tokens, norm_mix_g, norm_ffn_g, final_norm_g, sc_w_in, sc_conv_w, sc_w_out, rg_w_in, rg_conv_w, rg_conv_b, rg_w_gate_a, rg_b_gate_a, rg_w_gate_x, rg_b_gate_x, rg_lambda, rg_w_out, ffn_w_up, ffn_conv_w, ffn_w_down, loss_target, m_meta_tokens, m_norm_mix_g, m_norm_ffn_g, m_final_norm_g, m_sc_w_in, m_sc_conv_w, m_sc_w_out, m_rg_w_in, m_rg_conv_w, m_rg_conv_b, m_rg_w_gate_a, m_rg_b_gate_a, m_rg_w_gate_x, m_rg_b_gate_x, m_rg_lambda, m_rg_w_out, m_ffn_w_up, m_ffn_conv_w, m_ffn_w_down, v_meta_tokens, v_norm_mix_g, v_norm_ffn_g, v_final_norm_g, v_sc_w_in, v_sc_conv_w, v_sc_w_out, v_rg_w_in, v_rg_conv_w, v_rg_conv_b, v_rg_w_gate_a, v_rg_b_gate_a, v_rg_w_gate_x, v_rg_b_gate_x, v_rg_lambda, v_rg_w_out, v_ffn_w_up, v_ffn_conv_w, v_ffn_w_down):
    weights = dict(meta_tokens=meta_tokens, norm_mix_g=norm_mix_g, norm_ffn_g=norm_ffn_g, final_norm_g=final_norm_g,
                   sc_w_in=sc_w_in, sc_conv_w=sc_conv_w, sc_w_out=sc_w_out, rg_w_in=rg_w_in, rg_conv_w=rg_conv_w,
                   rg_conv_b=rg_conv_b, rg_w_gate_a=rg_w_gate_a, rg_b_gate_a=rg_b_gate_a, rg_w_gate_x=rg_w_gate_x,
                   rg_b_gate_x=rg_b_gate_x, rg_lambda=rg_lambda, rg_w_out=rg_w_out, ffn_w_up=ffn_w_up,
                   ffn_conv_w=ffn_conv_w, ffn_w_down=ffn_w_down)
    m_in = dict(meta_tokens=m_meta_tokens, norm_mix_g=m_norm_mix_g, norm_ffn_g=m_norm_ffn_g, final_norm_g=m_final_norm_g,
                sc_w_in=m_sc_w_in, sc_conv_w=m_sc_conv_w, sc_w_out=m_sc_w_out, rg_w_in=m_rg_w_in, rg_conv_w=m_rg_conv_w,
                rg_conv_b=m_rg_conv_b, rg_w_gate_a=m_rg_w_gate_a, rg_b_gate_a=m_rg_b_gate_a, rg_w_gate_x=m_rg_w_gate_x,
                rg_b_gate_x=m_rg_b_gate_x, rg_lambda=m_rg_lambda, rg_w_out=m_rg_w_out, ffn_w_up=m_ffn_w_up,
                ffn_conv_w=m_ffn_conv_w, ffn_w_down=m_ffn_w_down)
    v_in = dict(meta_tokens=v_meta_tokens, norm_mix_g=v_norm_mix_g, norm_ffn_g=v_norm_ffn_g, final_norm_g=v_final_norm_g,
                sc_w_in=v_sc_w_in, sc_conv_w=v_sc_conv_w, sc_w_out=v_sc_w_out, rg_w_in=v_rg_w_in, rg_conv_w=v_rg_conv_w,
                rg_conv_b=v_rg_conv_b, rg_w_gate_a=v_rg_w_gate_a, rg_b_gate_a=v_rg_b_gate_a, rg_w_gate_x=v_rg_w_gate_x,
                rg_b_gate_x=v_rg_b_gate_x, rg_lambda=v_rg_lambda, rg_w_out=v_rg_w_out, ffn_w_up=v_ffn_w_up,
                ffn_conv_w=v_ffn_conv_w, ffn_w_down=v_ffn_w_down)
    names = list(weights)

    seq, d = x.shape[1], x.shape[2]
    n_meta = meta_tokens.shape[0]
    n_ffn = ffn_w_up.shape[0]

    assert n_ffn == 2
    def shard_rows(w_in, w_out):
        return [w_in.T.astype(MXU_DTYPE), w_out.astype(MXU_DTYPE)]

    def both_orientations(in_t_gathered, out_gathered):
        w_in_t = in_t_gathered.reshape(-1, d)
        w_out = out_gathered.reshape(-1, d)
        return w_in_t.T, w_in_t, w_out, w_out.T

    shards = {"ffn0": shard_rows(ffn_w_up[0], ffn_w_down[0]), "rg": shard_rows(rg_w_in[0], rg_w_out[0]),
              "ffn1": shard_rows(ffn_w_up[1], ffn_w_down[1])}
    ffn_w_up_f, ffn_w_up_t, ffn_w_down_f, ffn_w_down_t = [None] * 2, [None] * 2, [None] * 2, [None] * 2

    small_names = ["meta_tokens", "sc_conv_w", "rg_conv_w", "rg_conv_b", "rg_b_gate_a", "rg_b_gate_x", "rg_lambda",
                   "ffn_conv_w"]
    small_lead = {n: weights[n].shape[:-1] for n in small_names}
    small_sizes = [weights[n].size for n in small_names]
    small_flat = jnp.concatenate([weights[n].reshape(-1) for n in small_names])
    small_rows = _pad_rows(small_flat, d, SUBLANES)
    sc_in_g, sc_out_g, small_g = _ride_alone(
        _GatherRider(shard_rows(sc_w_in[0], sc_w_out[0]) + [small_rows]), "gather_first")
    sc_w_in_f, sc_w_in_t, sc_w_out_f, sc_w_out_t = both_orientations(sc_in_g, sc_out_g)
    small_g = small_g.reshape(N_DEV, -1)
    small_full = {}
    o = 0
    for n, sz in zip(small_names, small_sizes):
        small_full[n] = _chunks_to_cols(small_g[:, o:o + sz], small_lead[n])
        o += sz

    wg, wgt = _pair_gate_weights(rg_w_gate_a[0].astype(MXU_DTYPE), rg_w_gate_x[0].astype(MXU_DTYPE))
    rg_cw, rg_cb = small_full["rg_conv_w"][0], small_full["rg_conv_b"]
    rg_ba, rg_bx, rg_lam = small_full["rg_b_gate_a"], small_full["rg_b_gate_x"], small_full["rg_lambda"]
    sc_cw = small_full["sc_conv_w"][0]
    ffn_cw = small_full["ffn_conv_w"]

    tp = _row_tile(n_meta + seq, ROW_TILE_PERM)

    def ffn_fwd(h, l, rider):
        n, a, z, *gathered = _ffn_up_fused(h, norm_ffn_g[l:l + 1], ffn_w_up_f[l], ffn_cw[l], tp, f"ffn{l}_up", rider)
        return _matmul_residual(z, ffn_w_down_f[l], h, f"ffn{l}_down"), (n, a, z), gathered

    h0, n0, p0, q0, *ffn0_w = _sc_in_fused(x[0], small_full["meta_tokens"], norm_mix_g[0:1], sc_w_in_f, sc_cw, tp, "sc_in",
                                           _GatherRider(shards["ffn0"]))
    ffn_w_up_f[0], ffn_w_up_t[0], ffn_w_down_f[0], ffn_w_down_t[0] = both_orientations(*ffn0_w)
    h1 = _matmul_residual(q0, sc_w_out_f, h0, "sc_out")
    h2, ffn0_saved, rg_w = ffn_fwd(h1, 0, _GatherRider(shards["rg"]))
    rg_w_in_f, rg_w_in_t, rg_w_out_f, rg_w_out_t = both_orientations(*rg_w)
    n2, p2, y2, hs2, *ffn1_w = _rg_in_fused(h2, norm_mix_g[1:2], rg_w_in_f, rg_cw, rg_cb, wg, rg_ba, rg_bx, rg_lam,
                                            tp, "rg_in", _GatherRider(shards["ffn1"]))
    ffn_w_up_f[1], ffn_w_up_t[1], ffn_w_down_f[1], ffn_w_down_t[1] = both_orientations(*ffn1_w)
    h3 = _matmul_residual(y2, rg_w_out_f, h2, "rg_out")
    ffn1_saved = tuple(_ffn_up_fused(h3, norm_ffn_g[1:2], ffn_w_up_f[1], ffn_cw[1], tp, "ffn1_up"))
    dh4, loss_tile, d_final_g = _down_loss_head(ffn1_saved[2], ffn_w_down_f[1], h3, final_norm_g.reshape(1, d),
                                                loss_target[0], n_meta, tp, "ffn1_down_loss_head")
    loss = lax.psum(loss_tile[0, 0], AXES)

    arrived = {}

    def ffn_bwd(dh_out, h_in, saved, l):
        n, a, z = saved
        da, dhb, d_cw = _ffn_down_bwd_fused(dh_out, ffn_w_down_t[l], a, ffn_cw[l], tp, f"ffn{l}_down_bwd")
        d_w_down = _wgrad(z, dhb, f"ffn{l}_down_wgrad")
        d_w_up_t, arrived[f"ffn_w_down{l}"] = _wgrad(da, n, f"ffn{l}_up_wgrad", _ScatterRider(d_w_down))
        rows = d_w_up_t.shape[0] // N_DEV
        first_rows = (rows // 2) // (2 * SUBLANES) * (2 * SUBLANES)
        dh_in, d_g, arrived[f"ffn_w_up{l}a"] = _dgrad_in_norm(
            da, ffn_w_up_t[l], h_in, norm_ffn_g[l:l + 1], dh_out, f"ffn{l}_up_dgrad",
            _ScatterRider(d_w_up_t, (0, first_rows)))
        return dh_in, d_cw, d_g, _ScatterRider(d_w_up_t, (first_rows, rows - first_rows))

    dh3, d_fcw1, d_fg1, up1_rest = ffn_bwd(dh4, h3, ffn1_saved, 1)

    dp2, dhb3, d_rg_cw, d_rg_cb, d_wg, d_rg_ba, d_rg_bx, d_rg_lam, arrived["ffn_w_up1b"] = _rg_out_bwd_fused(
        dh3, rg_w_out_t, p2, hs2, rg_cw, rg_cb, wg, wgt, rg_ba, rg_bx, rg_lam, tp, "rg_out_bwd", up1_rest)
    d_wa, d_wx = _unpair_gate_grads(d_wg, rg_w_gate_a.shape[2])
    d_rg_w_out = _wgrad(y2, dhb3, "rg_out_wgrad")
    d_rg_w_in_t, arrived["rg_w_out"] = _wgrad(dp2, n2, "rg_in_wgrad", _ScatterRider(d_rg_w_out))
    dh2, d_mg1, arrived["rg_w_in"] = _dgrad_in_norm(
        dp2, rg_w_in_t, h2, norm_mix_g[1:2], dh3, "rg_in_dgrad", _ScatterRider(d_rg_w_in_t))

    dh1, d_fcw0, d_fg0, up0_rest = ffn_bwd(dh2, h1, ffn0_saved, 0)

    dp0, dhb1, d_sc_cw, arrived["ffn_w_up0b"] = _sc_out_bwd_fused(dh1, sc_w_out_t, p0, sc_cw, tp, "sc_out_bwd", up0_rest)
    d_sc_w_out = _wgrad(q0, dhb1, "sc_out_wgrad")
    d_sc_w_in_t, arrived["sc_w_out"] = _wgrad(dp0, n0, "sc_in_wgrad", _ScatterRider(d_sc_w_out))
    d_mg0, d_meta, d_x, arrived["sc_w_in"] = _dgrad_in_to_input(
        dp0, sc_w_in_t, h0, norm_mix_g[0:1], dh1, n_meta, tp, "sc_in_dgrad", _ScatterRider(d_sc_w_in_t))
    grad_x = d_x[None]

    grads = {}
    for n in ("sc_w_in", "rg_w_in"):
        grads[n] = _sum_slots(arrived[n], f"sum_{n}").T[None]
    for n in ("sc_w_out", "rg_w_out"):
        grads[n] = _sum_slots(arrived[n], f"sum_{n}")[None]
    grads["ffn_w_up"] = jnp.stack([jnp.concatenate(
        [_sum_slots(arrived[f"ffn_w_up{l}{part}"], f"sum_ffn_w_up{l}{part}") for part in "ab"], axis=0).T
        for l in range(n_ffn)])
    grads["ffn_w_down"] = jnp.stack([_sum_slots(arrived[f"ffn_w_down{l}"], f"sum_ffn_w_down{l}") for l in range(n_ffn)])

    small_grads = {"meta_tokens": d_meta,"sc_conv_w": d_sc_cw[None], "rg_conv_w": d_rg_cw[None],
                   "rg_conv_b": d_rg_cb, "rg_b_gate_a": d_rg_ba, "rg_b_gate_x": d_rg_bx, "rg_lambda": d_rg_lam,
                   "ffn_conv_w": jnp.stack([d_fcw0, d_fcw1])}
    small_chunks = jnp.concatenate([_cols_to_chunks(small_grads[n]) for n in small_names], axis=1)
    pad = small_rows.shape[0] * d - small_chunks.shape[1]
    small_chunks = jnp.pad(small_chunks, ((0, 0), (0, pad))).reshape(N_DEV, small_rows.shape[0], d)
    rep_names = ["norm_mix_g", "norm_ffn_g", "final_norm_g", "rg_w_gate_a", "rg_w_gate_x"]
    rep_grads = {"norm_mix_g": jnp.concatenate([d_mg0, d_mg1], axis=0),
                 "norm_ffn_g": jnp.concatenate([d_fg0, d_fg1], axis=0),
                 "final_norm_g": d_final_g.reshape(-1), "rg_w_gate_a": d_wa[None], "rg_w_gate_x": d_wx[None]}
    rep_flat = jnp.concatenate([rep_grads[n].reshape(-1) for n in rep_names])
    rep_chunk_rows = -(-rep_flat.shape[0] // (N_DEV * d))
    rep_chunk_rows += -(small_rows.shape[0] + rep_chunk_rows) % 16
    rep_chunks = jnp.pad(rep_flat, (0, N_DEV * rep_chunk_rows * d - rep_flat.shape[0])).reshape(N_DEV, rep_chunk_rows, d)

    last_chunks = jnp.concatenate([small_chunks, rep_chunks], axis=1)
    reduced = _sum_slots(_ride_alone(_ScatterRider(last_chunks.reshape(-1, d)), "scatter_last")[0], "sum_last")

    small_red = reduced[0:small_rows.shape[0]].reshape(-1)
    o = small_rows.shape[0]
    so = 0
    for n, sz in zip(small_names, small_sizes):
        grads[n] = small_red[so:so + sz].reshape(weights[n].shape)
        so += sz
    rep_red = _ride_alone(_GatherRider([reduced[o:o + rep_chunks.shape[1]]]), "gather_replicated_grads")[0].reshape(-1)
    ro = 0
    for n in rep_names:
        sz = weights[n].size
        grads[n] = rep_red[ro:ro + sz].reshape(weights[n].shape)
        ro += sz

    delta, new_m, new_v = {}, {}, {}
    for n in names:
        delta[n], new_m[n], new_v[n] = _adamw_nd(weights[n], grads[n], m_in[n], v_in[n], f"adamw_{n}")

    return (loss, grad_x, *[grads[n] for n in names], *[delta[n] for n in names],
            *[new_m[n] for n in names], *[new_v[n] for n in names])
```

```python
import jax
import jax.numpy as jnp
from jax import lax
from jax.experimental import pallas as pl
from jax.experimental.pallas import tpu as pltpu

F32 = jnp.float32
MXU_DTYPE = jnp.bfloat16
RMS_EPS = 1e-6
RG_C = 8.0
ADAM_LR = 0.001
ADAM_B1 = 0.9
ADAM_B2 = 0.999
ADAM_EPS = 1e-08
ADAM_WD = 0.01
ADAM_STEP = 10

N_DEV = 8
AXES = ("x", "y", "c")
SUBLANES = 8
LANES = 128
VMEM_LIMIT_BYTES = 48 * 1024 * 1024
ROW_TILE_MATMUL = 700
ROW_TILE_WGRAD = 3300
ROW_TILE_PERM = 400
STRIP = 256
STRIP_FFN = 4096

_TN = (((0,), (0,)), ((), ()))
_NN = (((1,), (0,)), ((), ()))


def _row_tile(t, target):
    best = None
    for tm in range(16, t + 1, 16):
        if t % tm == 0 and tm <= target:
            best = tm
    return best if best is not None else t


def _col_tile(n, target):
    best = None
    for tn in range(LANES, n + 1, LANES):
        if n % tn == 0 and tn <= target:
            best = tn
    return best if best is not None else n


def _params(*sem):
    return pltpu.CompilerParams(dimension_semantics=sem, vmem_limit_bytes=VMEM_LIMIT_BYTES)


def _dot(a, b, dims):
    return lax.dot_general(a, b, dims, preferred_element_type=F32)


def _sigmoid(x):
    return 1.0 / (1.0 + jnp.exp(-x))


def _sigmoid_tanh(x):
    return 0.5 * jnp.tanh(0.5 * x) + 0.5


def _gelu(x):
    c = 0.7978845608028654
    t = jnp.tanh(c * (x + 0.044715 * (x * x * x)))
    return 0.5 * x * (1.0 + t)


def _gelu_and_grad(x):
    c = 0.7978845608028654
    x2 = x * x
    t = jnp.tanh(c * (x + 0.044715 * (x2 * x)))
    half = 0.5 * (1.0 + t)
    return x * half, half + 0.5 * x * (1.0 - t * t) * c * (1.0 + 3.0 * 0.044715 * x2)


def _softplus(x):
    return jnp.maximum(x, 0.0) + jnp.log1p(jnp.exp(-jnp.abs(x)))


def _time_scratch(tm, d):
    return [pltpu.VMEM((2, tm, d), F32), pltpu.VMEM((d // LANES, tm, LANES), F32), pltpu.SemaphoreType.DMA((2,))]


def _fetch_time_tile(i, nt, src_ref, head, tbuf, sems, tm):
    n_head = head.shape[0]

    def tile(j, slot):
        start = pl.multiple_of(j * tm - n_head, SUBLANES)
        return pltpu.make_async_copy(src_ref.at[pl.ds(start, tm)], tbuf.at[slot], sems.at[slot])

    first = pltpu.make_async_copy(src_ref.at[pl.ds(0, tm - n_head)], tbuf.at[0, pl.ds(n_head, tm - n_head)], sems.at[0])

    @pl.when(i == 0)
    def _():
        tbuf[0, 0:n_head, :] = head
        first.start()

    @pl.when(i + 1 < nt)
    def _():
        tile(i + 1, (i + 1) % 2).start()

    @pl.when(i == 0)
    def _():
        first.wait()

    @pl.when(i > 0)
    def _():
        tile(i, i % 2).wait()


def _time_to_tile_order(t_ref, slabs, out_ref):
    tm, d = out_ref.shape
    groups = tm // SUBLANES
    for k in range(d // LANES):
        slabs[k] = t_ref[:, k * LANES:(k + 1) * LANES]
    for k in range(d // LANES):
        for g in range(groups):
            out_ref[g * SUBLANES:(g + 1) * SUBLANES, k * LANES:(k + 1) * LANES] = slabs[k, pl.ds(g, SUBLANES, stride=groups), :]


def _tile_to_time_order(p_ref, slabs, t_ref):
    tm, d = p_ref.shape
    groups = tm // SUBLANES
    for k in range(d // LANES):
        for g in range(groups):
            slabs[k, pl.ds(g, SUBLANES, stride=groups), :] = p_ref[g * SUBLANES:(g + 1) * SUBLANES, k * LANES:(k + 1) * LANES]
    for k in range(d // LANES):
        t_ref[:, k * LANES:(k + 1) * LANES] = slabs[k]


def _rows(ref, sl):
    return [ref[k:k + 1, sl] for k in range(ref.shape[0])]


def _shift_down(x, before, s):
    if s == 0:
        return x
    n = x.shape[0]
    row = lax.broadcasted_iota(jnp.int32, (SUBLANES, x.shape[1]), 0)
    heads = []
    for g in range(s):
        v = x[n - (s - g) * SUBLANES:n - (s - g - 1) * SUBLANES]
        heads.append(pltpu.roll(jnp.where(row == SUBLANES - 1, before[s - g - 1], v), 1, axis=0))
    return jnp.concatenate(heads + [x[0:n - s * SUBLANES]], axis=0)


def _shift_up(x, after, s):
    if s == 0:
        return x
    row = lax.broadcasted_iota(jnp.int32, (SUBLANES, x.shape[1]), 0)
    tails = []
    for m in range(s):
        v = x[m * SUBLANES:(m + 1) * SUBLANES]
        tails.append(pltpu.roll(jnp.where(row == 0, after[m], v), SUBLANES - 1, axis=0))
    return jnp.concatenate([x[s * SUBLANES:]] + tails, axis=0)


def _weighted(w, windows):
    y = w[0] * windows[0]
    for k in range(1, len(w)):
        y = y + w[k] * windows[k]
    return y


def _windows(x, before, k_width):
    return [_shift_down(x, before, k_width - 1 - k) for k in range(k_width)]


def _conv_t(dy, after, w):
    k_width = len(w)
    return _weighted(w, [_shift_up(dy, after, k_width - 1 - k) for k in range(k_width)])


def _blocks(ref, sl, count, newest_first):
    n = ref.shape[0] // SUBLANES
    order = range(n - 1, n - 1 - count, -1) if newest_first else range(count)
    return [ref[b * SUBLANES:(b + 1) * SUBLANES, sl] for b in order]


def _halo_specs(tm, cols, count, tile_of):
    def spec(k):
        return pl.BlockSpec((SUBLANES, cols), lambda i: (jnp.maximum(tile_of(i) * (tm // SUBLANES) - k, 0), 0))
    return [spec(k) for k in range(1, count + 1)]


def _scan_tile(coef, val, out, carry, groups, reverse):
    cols = coef.shape[1]
    row = lax.broadcasted_iota(jnp.int32, (SUBLANES, cols), 0)

    def blk(i):
        g = groups - 1 - i if reverse else i
        return pl.ds(pl.multiple_of(g * SUBLANES, SUBLANES), SUBLANES)

    def local(i, pl_):
        p_prev, l_prev = pl_
        a = coef[blk(i), :]
        p = a * p_prev
        l = a * l_prev + val[blk(i), :]
        coef[blk(i), :] = p
        val[blk(i), :] = l
        return p, l

    pf, lf = lax.fori_loop(0, groups, local, (jnp.ones((SUBLANES, cols), F32), jnp.zeros((SUBLANES, cols), F32)))
    for s in (1, 2, 4):
        keep, sh = (row < SUBLANES - s, SUBLANES - s) if reverse else (row >= s, s)
        p_s = jnp.where(keep, pltpu.roll(pf, sh, axis=0), 1.0)
        l_s = jnp.where(keep, pltpu.roll(lf, sh, axis=0), 0.0)
        lf = pf * l_s + lf
        pf = pf * p_s
    end = lf + pf * carry
    if reverse:
        init = jnp.where(row == SUBLANES - 1, carry, pltpu.roll(end, SUBLANES - 1, axis=0))
        leaving = jnp.broadcast_to(end[0:1, :], (SUBLANES, cols))
    else:
        init = jnp.where(row == 0, carry, pltpu.roll(end, 1, axis=0))
        leaving = jnp.broadcast_to(end[SUBLANES - 1:SUBLANES, :], (SUBLANES, cols))

    def fix(i, _):
        out[blk(i), :] = val[blk(i), :] + coef[blk(i), :] * init
        return 0

    lax.fori_loop(0, groups, fix, 0)
    return leaving


def _resident(shape):
    return pl.BlockSpec(shape, lambda *_: (0,) * len(shape), pipeline_mode=pl.Buffered(1))


def _full(shape):
    return pl.BlockSpec(shape, lambda *_: (0,) * len(shape))


def _rmsnorm_to(h_ref, g_ref, n_ref):
    x = h_ref[...]
    ms = jnp.mean(x * x, axis=-1, keepdims=True)
    n_ref[...] = (x * lax.rsqrt(ms + RMS_EPS) * g_ref[...]).astype(n_ref.dtype)


_ANY = pl.BlockSpec(memory_space=pl.ANY)
_MESH = pl.DeviceIdType.MESH


def _dma_sems(n):
    return pltpu.SemaphoreType.DMA((n,))


def _when_each(*phases):
    for cond, fn in phases:
        if cond is not None:
            pl.when(cond)(fn)


class _NoRider:
    inputs = in_specs = out_shape = out_specs = scratch_shapes = ()

    def __call__(self, first, middle, last, ins, outs, scratch):
        pass


_NO_RIDER = _NoRider()


def _hand_on_step(nt):
    return (3 * nt) // 4


def _split_refs(rider, n_out, n_scratch, rest):
    a = len(rider.inputs)
    b = a + n_out
    c = b + len(rider.out_shape)
    e = c + n_scratch
    return rest[:a], rest[a:b], rest[b:c], rest[c:e], rest[e:]


class _GatherRider:
    def __init__(self, blocks):
        n = len(blocks)
        self.inputs = tuple(blocks)
        self.in_specs = (_ANY,) * n
        self.out_shape = tuple(jax.ShapeDtypeStruct((N_DEV,) + b.shape, b.dtype) for b in blocks)
        self.out_specs = (_ANY,) * n
        self.scratch_shapes = (_dma_sems(7 * n), _dma_sems(7 * n), _dma_sems(n))

    def __call__(self, first, middle, last, ins, outs, scratch):
        n = len(ins)
        send_sems, recv_sems, local_sems = scratch
        x, y, c = lax.axis_index("x"), lax.axis_index("y"), lax.axis_index("c")
        me, sibling = (x, y, c), (x, y, 1 - c)
        chips = [(1 - x, y), (x, 1 - y), (1 - x, 1 - y)]

        def slot(b, px, py, pc):
            return outs[b].at[4 * px + 2 * py + pc]

        def copy(k, b, block, to, src=None):
            return pltpu.make_async_remote_copy(
                src_ref=slot(b, *block) if src is None else src, dst_ref=slot(b, *block),
                send_sem=send_sems.at[k * n + b], recv_sem=recv_sems.at[k * n + b], device_id=to, device_id_type=_MESH)

        mine = [pltpu.make_async_copy(ins[b], slot(b, *me), local_sems.at[b]) for b in range(n)]
        own = [copy(0, b, me, sibling, src=ins[b]) for b in range(n)]
        own += [copy(1 + j, b, me, (*chip, c), src=ins[b]) for j, chip in enumerate(chips) for b in range(n)]
        passed = [[copy(4 + j, b, (*chip, c), sibling) for b in range(n)] for j, chip in enumerate(chips)]

        def at_first():
            for cp in mine + own:
                cp.start()

        def at_middle():
            for j, chip in enumerate(chips):
                for b in range(n):
                    copy(1 + j, b, (*chip, c), me).wait_recv()
                    passed[j][b].start()

        def at_last():
            for b in range(n):
                copy(0, b, sibling, me).wait_recv()
                for j, chip in enumerate(chips):
                    copy(4 + j, b, (*chip, 1 - c), me).wait_recv()
            for cp in own + [cp for group in passed for cp in group]:
                cp.wait_send()
            for cp in mine:
                cp.wait()

        _when_each((first, at_first), (middle, at_middle), (last, at_last))


class _ScatterRider:
    def __init__(self, grad, rows=None):
        r = grad.shape[0] // N_DEV
        chunks = grad.reshape(N_DEV, r, grad.shape[1])
        self.rows = rows if rows is not None else (0, r)
        self.inputs = (chunks,)
        self.in_specs = (_ANY,)
        self.out_shape = (jax.ShapeDtypeStruct((N_DEV, self.rows[1], grad.shape[1]), chunks.dtype),)
        self.out_specs = (_ANY,)
        self.scratch_shapes = (_dma_sems(N_DEV - 1), _dma_sems(N_DEV - 1), pltpu.SemaphoreType.DMA(()))

    def __call__(self, first, middle, last, ins, outs, scratch):
        (g_ref,), (r_ref,) = ins, outs
        send_sems, recv_sems, local_sem = scratch
        x, y, c = lax.axis_index("x"), lax.axis_index("y"), lax.axis_index("c")
        me = 4 * x + 2 * y + c
        part = pl.ds(*self.rows)
        mine = pltpu.make_async_copy(g_ref.at[me, part], r_ref.at[me], local_sem)
        copies = []
        for k in range(1, N_DEV):
            px, py, pc = (1 - x if k & 4 else x), (1 - y if k & 2 else y), (1 - c if k & 1 else c)
            copies.append(pltpu.make_async_remote_copy(
                src_ref=g_ref.at[4 * px + 2 * py + pc, part], dst_ref=r_ref.at[me],
                send_sem=send_sems.at[k - 1], recv_sem=recv_sems.at[k - 1],
                device_id=(px, py, pc), device_id_type=_MESH))

        def at_first():
            mine.start()
            for cp in copies:
                cp.start()

        def at_last():
            for cp in copies:
                cp.wait()
            mine.wait()

        _when_each((first, at_first), (last, at_last))


def _matmul_residual(q, w, h, name):
    t, k = q.shape
    d = w.shape[1]
    tm = _row_tile(t, ROW_TILE_MATMUL)

    def body(q_ref, w_ref, h_ref, o_ref):
        o_ref[...] = h_ref[...] + _dot(q_ref[...], w_ref[...], _NN)

    return pl.pallas_call(
        body, name=name, grid=(t // tm,),
        in_specs=[pl.BlockSpec((tm, k), lambda i: (i, 0)),
                  _resident((k, d)),
                  pl.BlockSpec((tm, d), lambda i: (i, 0))],
        out_specs=pl.BlockSpec((tm, d), lambda i: (i, 0)),
        out_shape=jax.ShapeDtypeStruct((t, d), F32),
        compiler_params=_params("parallel"),
    )(q, w, h)


def _dgrad_in_norm(dp, wt, h, g, dh_next, name, rider=_NO_RIDER):
    t, n = dp.shape
    d = wt.shape[1]
    tm = _row_tile(t, ROW_TILE_MATMUL)
    nt = t // tm

    def body(dp_ref, wt_ref, h_ref, g_ref, dhn_ref, *rest):
        r_in, (dh_ref, dg_ref), r_out, _, r_scratch = _split_refs(rider, 2, 0, rest)
        i = pl.program_id(0)
        rider(i == 0, i == nt // 2, None, r_in, r_out, r_scratch)
        dn = _dot(dp_ref[...], wt_ref[...], _NN)
        x = h_ref[...]
        ms = jnp.mean(x * x, axis=-1, keepdims=True)
        r = lax.rsqrt(ms + RMS_EPS)
        xhat = x * r
        dng = dn * g_ref[...]
        c = jnp.mean(dng * xhat, axis=-1, keepdims=True)
        dh_ref[...] = dhn_ref[...] + r * (dng - xhat * c)
        part = jnp.sum(dn * xhat, axis=0, keepdims=True)

        @pl.when(i == 0)
        def _():
            dg_ref[...] = part

        @pl.when(i > 0)
        def _():
            dg_ref[...] += part

        rider(None, None, i == nt - 1, r_in, r_out, r_scratch)

    return pl.pallas_call(
        body, name=name, grid=(nt,),
        in_specs=[pl.BlockSpec((tm, n), lambda i: (i, 0)),
                  _resident((n, d)),
                  pl.BlockSpec((tm, d), lambda i: (i, 0)),
                  pl.BlockSpec((1, d), lambda i: (0, 0)),
                  pl.BlockSpec((tm, d), lambda i: (i, 0))] + list(rider.in_specs),
        out_specs=[pl.BlockSpec((tm, d), lambda i: (i, 0)),
                   pl.BlockSpec((1, d), lambda i: (0, 0))] + list(rider.out_specs),
        out_shape=[jax.ShapeDtypeStruct((t, d), F32), jax.ShapeDtypeStruct((1, d), F32)] + list(rider.out_shape),
        scratch_shapes=list(rider.scratch_shapes),
        compiler_params=_params("arbitrary"),
    )(dp, wt, h, g, dh_next, *rider.inputs)


def _dgrad_in_to_input(dp, wt, h, g, dh_next, n_head, tm, name, rider=_NO_RIDER):
    t, n = dp.shape
    d = wt.shape[1]
    nt = t // tm

    def body(dp_ref, wt_ref, h_ref, g_ref, dhn_ref, *rest):
        r_in, (dg_ref, head_ref, rest_ref), r_out, (pbuf, tout, slabs, sems), r_scratch = _split_refs(rider, 3, 4, rest)
        i = pl.program_id(0)
        rider(i == 0, i == nt // 2, None, r_in, r_out, r_scratch)
        dn = _dot(dp_ref[...], wt_ref[...], _NN)
        x = h_ref[...]
        ms = jnp.mean(x * x, axis=-1, keepdims=True)
        r = lax.rsqrt(ms + RMS_EPS)
        xhat = x * r
        dng = dn * g_ref[...]
        c = jnp.mean(dng * xhat, axis=-1, keepdims=True)
        pbuf[...] = dhn_ref[...] + r * (dng - xhat * c)
        part = jnp.sum(dn * xhat, axis=0, keepdims=True)

        @pl.when(i == 0)
        def _():
            dg_ref[...] = part

        @pl.when(i > 0)
        def _():
            dg_ref[...] += part

        def store(j):
            if isinstance(j, int) and j == 0:
                return pltpu.make_async_copy(tout.at[0, pl.ds(n_head, tm - n_head)],
                                             rest_ref.at[pl.ds(0, tm - n_head)], sems.at[0])
            start = j * tm - n_head
            start = start if isinstance(start, int) else pl.multiple_of(start, SUBLANES)
            return pltpu.make_async_copy(tout.at[j % 2], rest_ref.at[pl.ds(start, tm)], sems.at[j % 2])

        @pl.when(i == 2)
        def _():
            store(0).wait()

        @pl.when(i > 2)
        def _():
            store(i - 2).wait()

        _tile_to_time_order(pbuf, slabs, tout.at[i % 2])

        @pl.when(i == 0)
        def _():
            head_ref[...] = tout[0, 0:n_head, :]
            store(0).start()

        @pl.when(i > 0)
        def _():
            store(i).start()

        @pl.when(i == nt - 1)
        def _():
            for j in (nt - 2, nt - 1):
                if j >= 0:
                    store(j).wait()

        rider(None, None, i == nt - 1, r_in, r_out, r_scratch)

    return pl.pallas_call(
        body, name=name, grid=(nt,),
        in_specs=[pl.BlockSpec((tm, n), lambda i: (i, 0)),
                  _resident((n, d)),
                  pl.BlockSpec((tm, d), lambda i: (i, 0)),
                  pl.BlockSpec((1, d), lambda i: (0, 0)),
                  pl.BlockSpec((tm, d), lambda i: (i, 0))] + list(rider.in_specs),
        out_specs=[pl.BlockSpec((1, d), lambda i: (0, 0)), _full((n_head, d)), _ANY] + list(rider.out_specs),
        out_shape=[jax.ShapeDtypeStruct((1, d), F32), jax.ShapeDtypeStruct((n_head, d), F32),
                   jax.ShapeDtypeStruct((t - n_head, d), F32)] + list(rider.out_shape),
        scratch_shapes=[pltpu.VMEM((tm, d), F32)] + _time_scratch(tm, d) + list(rider.scratch_shapes),
        compiler_params=_params("arbitrary"),
    )(dp, wt, h, g, dh_next, *rider.inputs)


def _wgrad(a, b, name, rider=_NO_RIDER):
    t, m = a.shape
    d = b.shape[1]
    tmm, tk = _col_tile(m, 1024), _row_tile(t, ROW_TILE_WGRAD)
    if tmm < 512:
        tmm, tk = _col_tile(m, 1536), _row_tile(t, ROW_TILE_MATMUL)
    nm, nk = m // tmm, t // tk

    def body(a_ref, b_ref, *rest):
        r_in, (o_ref,), r_out, _, r_scratch = _split_refs(rider, 1, 0, rest)
        i, k = pl.program_id(0), pl.program_id(1)
        rider(jnp.logical_and(i == 0, k == 0), None, None, r_in, r_out, r_scratch)

        @pl.when(k == 0)
        def _():
            o_ref[...] = jnp.zeros_like(o_ref)

        o_ref[...] += _dot(a_ref[...], b_ref[...], _TN)
        rider(None, None, jnp.logical_and(i == nm - 1, k == nk - 1), r_in, r_out, r_scratch)

    outs = pl.pallas_call(
        body, name=name, grid=(nm, nk),
        in_specs=[pl.BlockSpec((tk, tmm), lambda i, k: (k, i)),
                  pl.BlockSpec((tk, d), lambda i, k: (k, 0))] + list(rider.in_specs),
        out_specs=[pl.BlockSpec((tmm, d), lambda i, k: (i, 0))] + list(rider.out_specs),
        out_shape=[jax.ShapeDtypeStruct((m, d), F32)] + list(rider.out_shape),
        scratch_shapes=list(rider.scratch_shapes),
        compiler_params=_params("arbitrary", "arbitrary"),
    )(a, b, *rider.inputs)
    return outs if rider.out_shape else outs[0]


def _ffn_up_fused(h, g, w, conv_w, tm, name, rider=_NO_RIDER):
    t, d = h.shape
    f2 = w.shape[1]
    f = f2 // 2
    cw = min(STRIP_FFN, f)
    kw = conv_w.shape[0]
    nh = kw - 1
    nt = t // tm

    def body(h_ref, g_ref, w_ref, cw_ref, *rest):
        r_in, (n_ref, a_ref, z_ref), r_out, (carry,), r_scratch = _split_refs(rider, 3, 1, rest)
        i = pl.program_id(0)
        rider(i == 0, i == _hand_on_step(nt), None, r_in, r_out, r_scratch)

        @pl.when(i == 0)
        def _():
            carry[...] = jnp.zeros_like(carry)

        _rmsnorm_to(h_ref, g_ref, n_ref)
        for c in range(0, f, cw):
            conv = []
            for sl in (slice(c, c + cw), slice(f + c, f + c + cw)):
                a = _dot(n_ref[...], w_ref[:, sl], _NN)
                a_ref[:, sl] = a
                conv.append(_weighted(_rows(cw_ref, sl), _windows(a, _blocks(carry, sl, nh, True), kw)))
                carry[:, sl] = a[tm - nh * SUBLANES:tm]
            gg, vv = conv
            z_ref[:, c:c + cw] = (gg * _sigmoid(gg) * vv).astype(z_ref.dtype)
        rider(None, None, i == nt - 1, r_in, r_out, r_scratch)

    return pl.pallas_call(
        body, name=name, grid=(nt,),
        in_specs=[pl.BlockSpec((tm, d), lambda i: (i, 0)), pl.BlockSpec((1, d), lambda i: (0, 0)),
                  _resident((d, f2)), _full(conv_w.shape)] + list(rider.in_specs),
        out_specs=[pl.BlockSpec((tm, d), lambda i: (i, 0)), pl.BlockSpec((tm, f2), lambda i: (i, 0)),
                   pl.BlockSpec((tm, f), lambda i: (i, 0))] + list(rider.out_specs),
        out_shape=[jax.ShapeDtypeStruct((t, d), MXU_DTYPE), jax.ShapeDtypeStruct((t, f2), F32),
                   jax.ShapeDtypeStruct((t, f), MXU_DTYPE)] + list(rider.out_shape),
        scratch_shapes=[pltpu.VMEM((nh * SUBLANES, f2), F32)] + list(rider.scratch_shapes),
        compiler_params=_params("arbitrary"),
    )(h, g, w, conv_w, *rider.inputs)


def _ffn_down_bwd_fused(dh, wdt, a, conv_w, tm, name):
    t, d = dh.shape
    f = wdt.shape[1]
    f2 = 2 * f
    nt = t // tm
    cw = min(STRIP_FFN, f)
    kw = conv_w.shape[0]
    nh = kw - 1
    rev = lambda i: nt - 1 - i

    def body(dh_ref, wdt_ref, a_ref, *rest):
        halo_refs = rest[:nh]
        cw_ref, da_ref, dhb_ref, dw_ref, carry = rest[nh:]
        i = pl.program_id(0)
        newest, oldest = i == 0, i == nt - 1

        @pl.when(newest)
        def _():
            carry[...] = jnp.zeros_like(carry)
            dw_ref[...] = jnp.zeros_like(dw_ref)

        dhb_ref[...] = dh_ref[...].astype(dhb_ref.dtype)
        for c in range(0, f, cw):
            gsl, vsl = slice(c, c + cw), slice(f + c, f + c + cw)
            dz = _dot(dhb_ref[...], wdt_ref[:, gsl], _NN)
            win, conv = {}, {}
            for sl in (gsl, vsl):
                before = [jnp.where(oldest, 0.0, r[:, sl]) for r in halo_refs]
                win[sl.start] = _windows(a_ref[:, sl], before, kw)
                conv[sl.start] = _weighted(_rows(cw_ref, sl), win[sl.start])
            gg, vv = conv[gsl.start], conv[vsl.start]
            s = _sigmoid(gg)
            grads = {gsl.start: dz * vv * s * (1.0 + gg * (1.0 - s)), vsl.start: dz * gg * s}
            for sl in (gsl, vsl):
                dconv = grads[sl.start]
                da_ref[:, sl] = _conv_t(dconv, _blocks(carry, sl, nh, False), _rows(cw_ref, sl)).astype(da_ref.dtype)
                for k in range(kw):
                    dw_ref[k:k + 1, sl] += jnp.sum(dconv * win[sl.start][k], axis=0, keepdims=True)
                carry[:, sl] = dconv[0:nh * SUBLANES]

    return pl.pallas_call(
        body, name=name, grid=(nt,),
        in_specs=[pl.BlockSpec((tm, d), lambda i: (rev(i), 0)), _resident((d, f)),
                  pl.BlockSpec((tm, f2), lambda i: (rev(i), 0))] + _halo_specs(tm, f2, nh, rev) + [_full(conv_w.shape)],
        out_specs=[pl.BlockSpec((tm, f2), lambda i: (rev(i), 0)), pl.BlockSpec((tm, d), lambda i: (rev(i), 0)),
                   _full(conv_w.shape)],
        out_shape=[jax.ShapeDtypeStruct((t, f2), MXU_DTYPE), jax.ShapeDtypeStruct((t, d), MXU_DTYPE),
                   jax.ShapeDtypeStruct(conv_w.shape, F32)],
        scratch_shapes=[pltpu.VMEM((nh * SUBLANES, f2), F32)],
        compiler_params=_params("arbitrary"),
    )(dh, wdt, a, *([a] * nh), conv_w)


def _sc_in_fused(x, meta, g, w, conv_w, tm, name, rider=_NO_RIDER):
    d = x.shape[1]
    t = x.shape[0] + meta.shape[0]
    cw = min(STRIP, d)
    kw = conv_w.shape[0]
    nh = kw - 1
    nt = t // tm

    def body(x_ref, meta_ref, g_ref, w_ref, cw_ref, *rest):
        r_in, (h_ref, n_ref, p_ref, q_ref), r_out, (carry, tbuf, slabs, sems), r_scratch = _split_refs(rider, 4, 4, rest)
        i = pl.program_id(0)
        rider(i == 0, i == _hand_on_step(nt), None, r_in, r_out, r_scratch)

        @pl.when(i == 0)
        def _():
            carry[...] = jnp.zeros_like(carry)

        _fetch_time_tile(i, nt, x_ref, meta_ref[...], tbuf, sems, tm)
        _time_to_tile_order(tbuf.at[i % 2], slabs, h_ref)
        _rmsnorm_to(h_ref, g_ref, n_ref)
        for c in range(0, d, cw):
            sl = slice(c, c + cw)
            parts = []
            for base in (0, d, 2 * d):
                psl = slice(base + c, base + c + cw)
                parts.append(_dot(n_ref[...], w_ref[:, psl], _NN))
                p_ref[:, psl] = parts[-1]
            bg, cg, v = parts
            cv = cg * v
            u = _weighted(_rows(cw_ref, sl), _windows(cv, _blocks(carry, sl, nh, True), kw))
            carry[:, sl] = cv[tm - nh * SUBLANES:tm]
            q_ref[:, sl] = (bg * u).astype(q_ref.dtype)
        rider(None, None, i == nt - 1, r_in, r_out, r_scratch)

    row_tile = lambda cols: pl.BlockSpec((tm, cols), lambda i: (i, 0))
    return pl.pallas_call(
        body, name=name, grid=(nt,),
        in_specs=[_ANY, _full(meta.shape), pl.BlockSpec((1, d), lambda i: (0, 0)),
                  _resident((d, 3 * d)), _full(conv_w.shape)] + list(rider.in_specs),
        out_specs=[row_tile(d), row_tile(d), row_tile(3 * d), row_tile(d)] + list(rider.out_specs),
        out_shape=[jax.ShapeDtypeStruct((t, d), F32), jax.ShapeDtypeStruct((t, d), MXU_DTYPE),
                   jax.ShapeDtypeStruct((t, 3 * d), F32), jax.ShapeDtypeStruct((t, d), MXU_DTYPE)]
                  + list(rider.out_shape),
        scratch_shapes=[pltpu.VMEM((nh * SUBLANES, d), F32)] + _time_scratch(tm, d) + list(rider.scratch_shapes),
        compiler_params=_params("arbitrary"),
    )(x, meta, g, w, conv_w, *rider.inputs)


def _sc_out_bwd_fused(dh, wot, p, conv_w, tm, name, rider=_NO_RIDER):
    t, d = dh.shape
    nt = t // tm
    cw = min(STRIP, d)
    kw = conv_w.shape[0]
    nh = kw - 1
    rev = lambda i: nt - 1 - i

    def body(dh_ref, wot_ref, p_ref, *rest):
        halo_refs, cw_ref = rest[:nh], rest[nh]
        r_in, (dp_ref, dhb_ref, dw_ref), r_out, (carry,), r_scratch = _split_refs(rider, 3, 1, rest[nh + 1:])
        i = pl.program_id(0)
        newest, oldest = i == 0, i == nt - 1
        rider(newest, i == nt // 2, None, r_in, r_out, r_scratch)

        @pl.when(newest)
        def _():
            carry[...] = jnp.zeros_like(carry)
            dw_ref[...] = jnp.zeros_like(dw_ref)

        dhb_ref[...] = dh_ref[...].astype(dhb_ref.dtype)
        for c in range(0, d, cw):
            sl, csl, vsl = slice(c, c + cw), slice(d + c, d + c + cw), slice(2 * d + c, 2 * d + c + cw)
            w = _rows(cw_ref, sl)
            dq = _dot(dhb_ref[...], wot_ref[:, sl], _NN)
            bg, cg, v = p_ref[:, sl], p_ref[:, csl], p_ref[:, vsl]
            before = [jnp.where(oldest, 0.0, r[:, csl] * r[:, vsl]) for r in halo_refs]
            win = _windows(cg * v, before, kw)
            du = dq * bg
            dcv = _conv_t(du, _blocks(carry, sl, nh, False), w)
            dp_ref[:, sl] = (dq * _weighted(w, win)).astype(dp_ref.dtype)
            dp_ref[:, csl] = (dcv * v).astype(dp_ref.dtype)
            dp_ref[:, vsl] = (dcv * cg).astype(dp_ref.dtype)
            for k in range(kw):
                dw_ref[k:k + 1, sl] += jnp.sum(du * win[k], axis=0, keepdims=True)
            carry[:, sl] = du[0:nh * SUBLANES]
        rider(None, None, oldest, r_in, r_out, r_scratch)

    tile = lambda cols: pl.BlockSpec((tm, cols), lambda i: (rev(i), 0))
    return pl.pallas_call(
        body, name=name, grid=(nt,),
        in_specs=[tile(d), _resident((d, d)), tile(3 * d)] + _halo_specs(tm, 3 * d, nh, rev) + [_full(conv_w.shape)]
                 + list(rider.in_specs),
        out_specs=[tile(3 * d), tile(d), _full(conv_w.shape)] + list(rider.out_specs),
        out_shape=[jax.ShapeDtypeStruct((t, 3 * d), MXU_DTYPE), jax.ShapeDtypeStruct((t, d), MXU_DTYPE),
                   jax.ShapeDtypeStruct(conv_w.shape, F32)] + list(rider.out_shape),
        scratch_shapes=[pltpu.VMEM((nh * SUBLANES, d), F32)] + list(rider.scratch_shapes),
        compiler_params=_params("arbitrary"),
    )(dh, wot, p, *([p] * nh), conv_w, *rider.inputs)


def _pair_gate_weights(wa, wx):
    nb, bd, _ = wa.shape
    zero = jnp.zeros((nb // 2, bd, bd), wa.dtype)

    def pair(w):
        w = w.reshape(nb // 2, 2, bd, bd)
        top = jnp.concatenate([w[:, 0], zero], axis=2)
        bottom = jnp.concatenate([zero, w[:, 1]], axis=2)
        return jnp.concatenate([top, bottom], axis=1)

    both = jnp.concatenate([pair(wa), pair(wx)], axis=2)
    return both, jnp.swapaxes(both, 1, 2)


def _unpair_gate_grads(dw, bd):
    def blocks(cols):
        d0 = dw[:, :bd, cols:cols + bd]
        d1 = dw[:, bd:, cols + bd:cols + 2 * bd]
        return jnp.stack([d0, d1], axis=1).reshape(-1, bd, bd)
    return blocks(0), blocks(2 * bd)


def _rg_gates(u, k, wg_ref, ba, bx, lam):
    ub = u.astype(MXU_DTYPE)
    z = _dot(ub, wg_ref[k], _NN)
    half = z.shape[1] // 2
    r = _sigmoid_tanh(z[:, :half] + ba)
    ig = _sigmoid_tanh(z[:, half:] + bx)
    sp = _softplus(-lam)
    la = -RG_C * r * sp
    a = jnp.exp(la)
    th = jnp.tanh(la)
    mult = jnp.sqrt(-2.0 * th / (1.0 - th))
    return r, ig, a, mult


def _rg_in_fused(h, g, w, conv_w, conv_b, wg, ba, bx, lam, tm, name, rider=_NO_RIDER):
    t, d = h.shape
    r2 = w.shape[1]
    rr = r2 // 2
    nb, bd = wg.shape[0], wg.shape[1]
    kw = conv_w.shape[0]
    nh = kw - 1
    groups = tm // SUBLANES
    nt = t // tm

    def body(h_ref, g_ref, w_ref, cw_ref, cb_ref, wg_ref, ba_ref, bx_ref, lam_ref, *rest):
        r_in, (n_ref, p_ref, y_ref, hs_ref), r_out, (a_scr, b_scr, carry_rb, carry_h), r_scratch = _split_refs(
            rider, 4, 4, rest)
        i = pl.program_id(0)
        rider(i == 0, i == _hand_on_step(nt), None, r_in, r_out, r_scratch)

        @pl.when(i == 0)
        def _():
            carry_rb[...] = jnp.zeros_like(carry_rb)
            carry_h[...] = jnp.zeros_like(carry_h)

        _rmsnorm_to(h_ref, g_ref, n_ref)
        for k in range(nb):
            sl = slice(k * bd, (k + 1) * bd)
            rsl = slice(rr + k * bd, rr + (k + 1) * bd)
            p_ref[:, sl] = _dot(n_ref[...], w_ref[:, sl], _NN)
            rb = _dot(n_ref[...], w_ref[:, rsl], _NN)
            p_ref[:, rsl] = rb
            u = _weighted(_rows(cw_ref, sl), _windows(rb, _blocks(carry_rb, sl, nh, True), kw)) + cb_ref[:, sl]
            carry_rb[:, sl] = rb[tm - nh * SUBLANES:tm]
            _, ig, a, mult = _rg_gates(u, k, wg_ref, ba_ref[:, sl], bx_ref[:, sl], lam_ref[:, sl])
            a_scr[:, sl] = a
            b_scr[:, sl] = mult * (ig * u)

        carry_h[...] = _scan_tile(a_scr, b_scr, hs_ref, carry_h[...], groups, reverse=False)

        for k in range(nb):
            sl = slice(k * bd, (k + 1) * bd)
            y_ref[:, sl] = (hs_ref[:, sl] * _gelu(p_ref[:, sl])).astype(y_ref.dtype)
        rider(None, None, i == nt - 1, r_in, r_out, r_scratch)

    vm = lambda rows: pltpu.VMEM((rows, rr), F32)
    return pl.pallas_call(
        body, name=name, grid=(nt,),
        in_specs=[pl.BlockSpec((tm, d), lambda i: (i, 0)), pl.BlockSpec((1, d), lambda i: (0, 0)),
                  _resident((d, r2)), _full(conv_w.shape), _full(conv_b.shape), _full(wg.shape), _full(ba.shape),
                  _full(bx.shape), _full(lam.shape)] + list(rider.in_specs),
        out_specs=[pl.BlockSpec((tm, d), lambda i: (i, 0)), pl.BlockSpec((tm, r2), lambda i: (i, 0)),
                   pl.BlockSpec((tm, rr), lambda i: (i, 0)), pl.BlockSpec((tm, rr), lambda i: (i, 0))]
                  + list(rider.out_specs),
        out_shape=[jax.ShapeDtypeStruct((t, d), MXU_DTYPE), jax.ShapeDtypeStruct((t, r2), F32),
                   jax.ShapeDtypeStruct((t, rr), MXU_DTYPE), jax.ShapeDtypeStruct((t, rr), F32)]
                  + list(rider.out_shape),
        scratch_shapes=[vm(tm), vm(tm), vm(nh * SUBLANES), vm(SUBLANES)] + list(rider.scratch_shapes),
        compiler_params=_params("arbitrary"),
    )(h, g, w, conv_w, conv_b, wg, ba, bx, lam, *rider.inputs)


def _rg_out_bwd_fused(dh, wot, p, hs, conv_w, conv_b, wg, wgt, ba, bx, lam, tm, name, rider=_NO_RIDER):
    t, d = dh.shape
    r2 = p.shape[1]
    rr = r2 // 2
    nb, bd = wg.shape[0], wg.shape[1]
    nt = t // tm
    kw = conv_w.shape[0]
    nh = kw - 1
    groups = tm // SUBLANES
    rev = lambda i: nt - 1 - i

    def body(dh_ref, wot_ref, p_ref, hs_ref, hsh_ref, *rest):
        halo_refs = rest[:nh]
        cw_ref, cb_ref, wg_ref, wgt_ref, ba_ref, bx_ref, lam_ref = rest[nh:nh + 7]
        (r_in, (dp_ref, dhb_ref, dcw_ref, dcb_ref, dwg_ref, dba_ref, dbx_ref, dlam_ref), r_out,
         (a_scr, as_scr, u_scr, r_scr, ig_scr, mult_scr, g_scr, carry_g, carry_du), r_scratch) = _split_refs(
            rider, 8, 9, rest[nh + 7:])
        i = pl.program_id(0)
        newest, oldest = i == 0, i == nt - 1
        rider(newest, i == nt // 2, None, r_in, r_out, r_scratch)

        @pl.when(newest)
        def _():
            carry_g[...] = jnp.zeros_like(carry_g)
            carry_du[...] = jnp.zeros_like(carry_du)
            for ref in (dcw_ref, dcb_ref, dwg_ref, dba_ref, dbx_ref, dlam_ref):
                ref[...] = jnp.zeros_like(ref)

        dhb_ref[...] = dh_ref[...].astype(dhb_ref.dtype)
        ones = jnp.ones((SUBLANES, bd), F32)

        def conv_windows(sl, rsl):
            before = [jnp.where(oldest, 0.0, r[:, rsl]) for r in halo_refs]
            return _windows(p_ref[:, rsl], before, kw)

        for k in range(nb):
            sl = slice(k * bd, (k + 1) * bd)
            rsl = slice(rr + k * bd, rr + (k + 1) * bd)
            dy = _dot(dhb_ref[...], wot_ref[:, sl], _NN)
            u = _weighted(_rows(cw_ref, sl), conv_windows(sl, rsl)) + cb_ref[:, sl]
            r, ig, a, mult = _rg_gates(u, k, wg_ref, ba_ref[:, sl], bx_ref[:, sl], lam_ref[:, sl])
            gate, dgate = _gelu_and_grad(p_ref[:, sl])
            g_scr[:, sl] = dy * gate
            dp_ref[:, sl] = (dy * hs_ref[:, sl] * dgate).astype(dp_ref.dtype)
            a_scr[:, sl] = a
            as_scr[:, sl] = _shift_up(a, [ones], 1)
            u_scr[:, sl] = u
            r_scr[:, sl] = r
            ig_scr[:, sl] = ig
            mult_scr[:, sl] = mult

        g_first = _scan_tile(as_scr, g_scr, g_scr, carry_g[...], groups, reverse=True)
        carry_g[...] = jnp.broadcast_to(a_scr[0:1, :], (SUBLANES, rr)) * g_first

        for k in range(nb):
            sl = slice(k * bd, (k + 1) * bd)
            rsl = slice(rr + k * bd, rr + (k + 1) * bd)
            cw = _rows(cw_ref, sl)
            lam_k = lam_ref[:, sl]
            sp = _softplus(-lam_k)
            g = g_scr[:, sl]
            a, u, r, ig, mult = a_scr[:, sl], u_scr[:, sl], r_scr[:, sl], ig_scr[:, sl], mult_scr[:, sl]
            da = g * _shift_down(hs_ref[:, sl], [jnp.where(oldest, 0.0, hsh_ref[:, sl])], 1)
            dmult = g * (ig * u)
            d_iu = g * mult
            dla = da * a - dmult * (a * a) / mult
            dr = dla * (-RG_C * sp)
            dsp = jnp.sum(dla * (-RG_C * r), axis=0, keepdims=True)
            dlam_ref[:, sl] += dsp * (-_sigmoid(-lam_k))
            dza = dr * r * (1.0 - r)
            dzx = (d_iu * u) * ig * (1.0 - ig)
            dba_ref[:, sl] += jnp.sum(dza, axis=0, keepdims=True)
            dbx_ref[:, sl] += jnp.sum(dzx, axis=0, keepdims=True)
            ub = u.astype(MXU_DTYPE)
            dz = jnp.concatenate([dza, dzx], axis=1).astype(MXU_DTYPE)
            dwg_ref[k] += _dot(ub, dz, _TN)
            du = d_iu * ig + _dot(dz, wgt_ref[k], _NN)
            dcb_ref[:, sl] += jnp.sum(du, axis=0, keepdims=True)
            win = conv_windows(sl, rsl)
            for kk in range(kw):
                dcw_ref[kk:kk + 1, sl] += jnp.sum(du * win[kk], axis=0, keepdims=True)
            dp_ref[:, rsl] = _conv_t(du, _blocks(carry_du, sl, nh, False), cw).astype(dp_ref.dtype)
            carry_du[:, sl] = du[0:nh * SUBLANES]
        rider(None, None, oldest, r_in, r_out, r_scratch)

    tile = lambda cols: pl.BlockSpec((tm, cols), lambda i: (rev(i), 0))
    vm = lambda rows: pltpu.VMEM((rows, rr), F32)
    grads = [conv_w.shape, conv_b.shape, wg.shape, ba.shape, bx.shape, lam.shape]
    return pl.pallas_call(
        body, name=name, grid=(nt,),
        in_specs=[tile(d), _resident((d, rr)), tile(r2), tile(rr)] + _halo_specs(tm, rr, 1, rev)
                 + _halo_specs(tm, r2, nh, rev)
                 + [_full(conv_w.shape), _full(conv_b.shape), _full(wg.shape), _full(wgt.shape), _full(ba.shape),
                    _full(bx.shape), _full(lam.shape)] + list(rider.in_specs),
        out_specs=[tile(r2), tile(d)] + [_full(s) for s in grads] + list(rider.out_specs),
        out_shape=[jax.ShapeDtypeStruct((t, r2), MXU_DTYPE), jax.ShapeDtypeStruct((t, d), MXU_DTYPE)]
                  + [jax.ShapeDtypeStruct(s, F32) for s in grads] + list(rider.out_shape),
        scratch_shapes=[vm(tm)] * 7 + [vm(SUBLANES), vm(nh * SUBLANES)] + list(rider.scratch_shapes),
        compiler_params=_params("arbitrary"),
    )(dh, wot, p, hs, hs, *([p] * nh), conv_w, conv_b, wg, wgt, ba, bx, lam, *rider.inputs)


def _down_loss_head(z, w, h, g, target, n_meta, tm, name):
    t, d = h.shape
    k = z.shape[1]
    groups = tm // SUBLANES
    nt = t // tm

    def body(z_ref, w_ref, h_ref, g_ref, t_ref, dh_ref, loss_ref, dg_ref, tgt, tbuf, slabs, sems):
        i = pl.program_id(0)
        _fetch_time_tile(i, nt, t_ref, jnp.zeros((n_meta, d), F32), tbuf, sems, tm)
        _time_to_tile_order(tbuf.at[i % 2], slabs, tgt)
        x = h_ref[...] + _dot(z_ref[...], w_ref[...], _NN)
        ms = jnp.mean(x * x, axis=-1, keepdims=True)
        r = lax.rsqrt(ms + RMS_EPS)
        xhat = x * r
        gg = g_ref[...]
        row = lax.broadcasted_iota(jnp.int32, (tm, 1), 0)
        time = i * tm + jnp.right_shift(row, 3) + jnp.bitwise_and(row, SUBLANES - 1) * groups
        err = jnp.where(time >= n_meta, xhat * gg - tgt[...], 0.0)
        dout = err * (1.0 / d)
        dng = dout * gg
        c = jnp.mean(dng * xhat, axis=-1, keepdims=True)
        dh_ref[...] = r * (dng - xhat * c)
        part_dg = jnp.sum(dout * xhat, axis=0, keepdims=True)
        part_loss = jnp.broadcast_to(0.5 * jnp.sum(err * dout, keepdims=True), loss_ref.shape)

        @pl.when(i == 0)
        def _():
            dg_ref[...] = part_dg
            loss_ref[...] = part_loss

        @pl.when(i > 0)
        def _():
            dg_ref[...] += part_dg
            loss_ref[...] += part_loss

    return pl.pallas_call(
        body, name=name, grid=(nt,),
        in_specs=[pl.BlockSpec((tm, k), lambda i: (i, 0)),
                  _resident((k, d)),
                  pl.BlockSpec((tm, d), lambda i: (i, 0)),
                  pl.BlockSpec((1, d), lambda i: (0, 0)),
                  _ANY],
        out_specs=[pl.BlockSpec((tm, d), lambda i: (i, 0)),
                   pl.BlockSpec((SUBLANES, LANES), lambda i: (0, 0)),
                   pl.BlockSpec((1, d), lambda i: (0, 0))],
        out_shape=[jax.ShapeDtypeStruct((t, d), F32), jax.ShapeDtypeStruct((SUBLANES, LANES), F32),
                   jax.ShapeDtypeStruct((1, d), F32)],
        scratch_shapes=[pltpu.VMEM((tm, d), F32)] + _time_scratch(tm, d),
        compiler_params=_params("arbitrary"),
    )(z, w, h, g, target)


def _adamw(w, g, m, v, name):
    rows, cols = w.shape
    tr = rows
    if rows > 512:
        for cand in range(8, 513, 8):
            if rows % cand == 0:
                tr = cand

    def body(w_ref, g_ref, m_ref, v_ref, d_ref, nm_ref, nv_ref):
        g_ = g_ref[...]
        m_ = ADAM_B1 * m_ref[...] + (1.0 - ADAM_B1) * g_
        v_ = ADAM_B2 * v_ref[...] + (1.0 - ADAM_B2) * (g_ * g_)
        m_hat = m_ / (1.0 - ADAM_B1 ** ADAM_STEP)
        v_hat = v_ / (1.0 - ADAM_B2 ** ADAM_STEP)
        d_ref[...] = -ADAM_LR * (m_hat / (jnp.sqrt(v_hat) + ADAM_EPS) + ADAM_WD * w_ref[...])
        nm_ref[...] = m_
        nv_ref[...] = v_

    spec = pl.BlockSpec((tr, cols), lambda i: (i, 0))
    shape = jax.ShapeDtypeStruct((rows, cols), F32)
    return pl.pallas_call(
        body, name=name, grid=(rows // tr,),
        in_specs=[spec] * 4, out_specs=[spec] * 3, out_shape=[shape] * 3,
        compiler_params=_params("parallel"),
    )(w, g, m, v)


def _adamw_nd(w, g, m, v, name):
    shape = w.shape
    two_d = (-1, shape[-1]) if w.ndim > 1 else (1, -1)
    outs = _adamw(w.reshape(two_d), g.reshape(two_d), m.reshape(two_d), v.reshape(two_d), name)
    return tuple(o.reshape(shape) for o in outs)


def _ride_alone(rider, name):
    def body(*refs):
        r_in, _, r_out, _, r_scratch = _split_refs(rider, 0, 0, refs)
        now = pl.program_id(0) == 0
        rider(now, None, None, r_in, r_out, r_scratch)
        rider(None, now, None, r_in, r_out, r_scratch)
        rider(None, None, now, r_in, r_out, r_scratch)

    return pl.pallas_call(
        body, name=name, grid=(1,),
        in_specs=list(rider.in_specs), out_specs=list(rider.out_specs), out_shape=list(rider.out_shape),
        scratch_shapes=list(rider.scratch_shapes),
        compiler_params=_params("arbitrary"),
    )(*rider.inputs)


def _sum_slots(parts, name):
    slots, rows, cols = parts.shape
    tr = _row_tile(rows, 256) if rows % 16 == 0 else rows

    def body(p_ref, o_ref):
        acc = p_ref[0]
        for s in range(1, slots):
            acc = acc + p_ref[s]
        o_ref[...] = acc

    return pl.pallas_call(
        body, name=name, grid=(rows // tr,),
        in_specs=[pl.BlockSpec((slots, tr, cols), lambda i: (0, i, 0))],
        out_specs=pl.BlockSpec((tr, cols), lambda i: (i, 0)),
        out_shape=jax.ShapeDtypeStruct((rows, cols), parts.dtype),
        compiler_params=_params("parallel"),
    )(parts)


def _pad_rows(flat, cols, multiple):
    n = flat.shape[0]
    rows = -(-n // cols)
    rows = -(-rows // multiple) * multiple
    return jnp.pad(flat, (0, rows * cols - n)).reshape(rows, cols)


def _cols_to_chunks(full):
    lead = full.shape[:-1]
    c = full.shape[-1] // N_DEV
    x = full.reshape(-1, N_DEV, c)
    return jnp.transpose(x, (1, 0, 2)).reshape(N_DEV, -1)


def _chunks_to_cols(chunks, lead):
    n = 1
    for s in lead:
        n *= s
    c = chunks.shape[1] // n
    x = chunks.reshape(N_DEV, n, c)
    return jnp.transpose(x, (1, 0, 2)).reshape(tuple(lead) + (N_DEV * c,))


def kernel(x, meta_tokens, norm_mix_g, norm_ffn_g, final_norm_g, sc_w_in, sc_conv_w, sc_w_out, rg_w_in, rg_conv_w, rg_conv_b, rg_w_gate_a, rg_b_gate_a, rg_w_gate_x, rg_b_gate_x, rg_lambda, rg_w_out, ffn_w_up, ffn_conv_w, ffn_w_down, loss_target, m_meta_tokens, m_norm_mix_g, m_norm_ffn_g, m_final_norm_g, m_sc_w_in, m_sc_conv_w, m_sc_w_out, m_rg_w_in, m_rg_conv_w, m_rg_conv_b, m_rg_w_gate_a, m_rg_b_gate_a, m_rg_w_gate_x, m_rg_b_gate_x, m_rg_lambda, m_rg_w_out, m_ffn_w_up, m_ffn_conv_w, m_ffn_w_down, v_meta_tokens, v_norm_mix_g, v_norm_ffn_g, v_final_norm_g, v_sc_w_in, v_sc_conv_w, v_sc_w_out, v_rg_w_in, v_rg_conv_w, v_rg_conv_b, v_rg_w_gate_a, v_rg_b_gate_a, v_rg_w_gate_x, v_rg_b_gate_x, v_rg_lambda, v_rg_w_out, v_ffn_w_up, v_ffn_conv_w, v_ffn_w_down):
    weights = dict(meta_tokens=meta_tokens, norm_mix_g=norm_mix_g, norm_ffn_g=norm_ffn_g, final_norm_g=final_norm_g,
                   sc_w_in=sc_w_in, sc_conv_w=sc_conv_w, sc_w_out=sc_w_out, rg_w_in=rg_w_in, rg_conv_w=rg_conv_w,
                   rg_conv_b=rg_conv_b, rg_w_gate_a=rg_w_gate_a, rg_b_gate_a=rg_b_gate_a, rg_w_gate_x=rg_w_gate_x,
                   rg_b_gate_x=rg_b_gate_x, rg_lambda=rg_lambda, rg_w_out=rg_w_out, ffn_w_up=ffn_w_up,
                   ffn_conv_w=ffn_conv_w, ffn_w_down=ffn_w_down)
    m_in = dict(meta_tokens=m_meta_tokens, norm_mix_g=m_norm_mix_g, norm_ffn_g=m_norm_ffn_g, final_norm_g=m_final_norm_g,
                sc_w_in=m_sc_w_in, sc_conv_w=m_sc_conv_w, sc_w_out=m_sc_w_out, rg_w_in=m_rg_w_in, rg_conv_w=m_rg_conv_w,
                rg_conv_b=m_rg_conv_b, rg_w_gate_a=m_rg_w_gate_a, rg_b_gate_a=m_rg_b_gate_a, rg_w_gate_x=m_rg_w_gate_x,
                rg_b_gate_x=m_rg_b_gate_x, rg_lambda=m_rg_lambda, rg_w_out=m_rg_w_out, ffn_w_up=m_ffn_w_up,
                ffn_conv_w=m_ffn_conv_w, ffn_w_down=m_ffn_w_down)
    v_in = dict(meta_tokens=v_meta_tokens, norm_mix_g=v_norm_mix_g, norm_ffn_g=v_norm_ffn_g, final_norm_g=v_final_norm_g,
                sc_w_in=v_sc_w_in, sc_conv_w=v_sc_conv_w, sc_w_out=v_sc_w_out, rg_w_in=v_rg_w_in, rg_conv_w=v_rg_conv_w,
                rg_conv_b=v_rg_conv_b, rg_w_gate_a=v_rg_w_gate_a, rg_b_gate_a=v_rg_b_gate_a, rg_w_gate_x=v_rg_w_gate_x,
                rg_b_gate_x=v_rg_b_gate_x, rg_lambda=v_rg_lambda, rg_w_out=v_rg_w_out, ffn_w_up=v_ffn_w_up,
                ffn_conv_w=v_ffn_conv_w, ffn_w_down=v_ffn_w_down)
    names = list(weights)

    seq, d = x.shape[1], x.shape[2]
    n_meta = meta_tokens.shape[0]
    n_ffn = ffn_w_up.shape[0]

    assert n_ffn == 2
    def shard_rows(w_in, w_out):
        return [w_in.T.astype(MXU_DTYPE), w_out.astype(MXU_DTYPE)]

    def both_orientations(in_t_gathered, out_gathered):
        w_in_t = in_t_gathered.reshape(-1, d)
        w_out = out_gathered.reshape(-1, d)
        return w_in_t.T, w_in_t, w_out, w_out.T

    shards = {"ffn0": shard_rows(ffn_w_up[0], ffn_w_down[0]), "rg": shard_rows(rg_w_in[0], rg_w_out[0]),
              "ffn1": shard_rows(ffn_w_up[1], ffn_w_down[1])}
    ffn_w_up_f, ffn_w_up_t, ffn_w_down_f, ffn_w_down_t = [None] * 2, [None] * 2, [None] * 2, [None] * 2

    small_names = ["meta_tokens", "sc_conv_w", "rg_conv_w", "rg_conv_b", "rg_b_gate_a", "rg_b_gate_x", "rg_lambda",
                   "ffn_conv_w"]
    small_lead = {n: weights[n].shape[:-1] for n in small_names}
    small_sizes = [weights[n].size for n in small_names]
    small_flat = jnp.concatenate([weights[n].reshape(-1) for n in small_names])
    small_rows = _pad_rows(small_flat, d, SUBLANES)
    sc_in_g, sc_out_g, small_g = _ride_alone(
        _GatherRider(shard_rows(sc_w_in[0], sc_w_out[0]) + [small_rows]), "gather_first")
    sc_w_in_f, sc_w_in_t, sc_w_out_f, sc_w_out_t = both_orientations(sc_in_g, sc_out_g)
    small_g = small_g.reshape(N_DEV, -1)
    small_full = {}
    o = 0
    for n, sz in zip(small_names, small_sizes):
        small_full[n] = _chunks_to_cols(small_g[:, o:o + sz], small_lead[n])
        o += sz

    wg, wgt = _pair_gate_weights(rg_w_gate_a[0].astype(MXU_DTYPE), rg_w_gate_x[0].astype(MXU_DTYPE))
    rg_cw, rg_cb = small_full["rg_conv_w"][0], small_full["rg_conv_b"]
    rg_ba, rg_bx, rg_lam = small_full["rg_b_gate_a"], small_full["rg_b_gate_x"], small_full["rg_lambda"]
    sc_cw = small_full["sc_conv_w"][0]
    ffn_cw = small_full["ffn_conv_w"]

    tp = _row_tile(n_meta + seq, ROW_TILE_PERM)

    def ffn_fwd(h, l, rider):
        n, a, z, *gathered = _ffn_up_fused(h, norm_ffn_g[l:l + 1], ffn_w_up_f[l], ffn_cw[l], tp, f"ffn{l}_up", rider)
        return _matmul_residual(z, ffn_w_down_f[l], h, f"ffn{l}_down"), (n, a, z), gathered

    h0, n0, p0, q0, *ffn0_w = _sc_in_fused(x[0], small_full["meta_tokens"], norm_mix_g[0:1], sc_w_in_f, sc_cw, tp, "sc_in",
                                           _GatherRider(shards["ffn0"]))
    ffn_w_up_f[0], ffn_w_up_t[0], ffn_w_down_f[0], ffn_w_down_t[0] = both_orientations(*ffn0_w)
    h1 = _matmul_residual(q0, sc_w_out_f, h0, "sc_out")
    h2, ffn0_saved, rg_w = ffn_fwd(h1, 0, _GatherRider(shards["rg"]))
    rg_w_in_f, rg_w_in_t, rg_w_out_f, rg_w_out_t = both_orientations(*rg_w)
    n2, p2, y2, hs2, *ffn1_w = _rg_in_fused(h2, norm_mix_g[1:2], rg_w_in_f, rg_cw, rg_cb, wg, rg_ba, rg_bx, rg_lam,
                                            tp, "rg_in", _GatherRider(shards["ffn1"]))
    ffn_w_up_f[1], ffn_w_up_t[1], ffn_w_down_f[1], ffn_w_down_t[1] = both_orientations(*ffn1_w)
    h3 = _matmul_residual(y2, rg_w_out_f, h2, "rg_out")
    ffn1_saved = tuple(_ffn_up_fused(h3, norm_ffn_g[1:2], ffn_w_up_f[1], ffn_cw[1], tp, "ffn1_up"))
    dh4, loss_tile, d_final_g = _down_loss_head(ffn1_saved[2], ffn_w_down_f[1], h3, final_norm_g.reshape(1, d),
                                                loss_target[0], n_meta, tp, "ffn1_down_loss_head")
    loss = lax.psum(loss_tile[0, 0], AXES)

    arrived = {}

    def ffn_bwd(dh_out, h_in, saved, l):
        n, a, z = saved
        da, dhb, d_cw = _ffn_down_bwd_fused(dh_out, ffn_w_down_t[l], a, ffn_cw[l], tp, f"ffn{l}_down_bwd")
        d_w_down = _wgrad(z, dhb, f"ffn{l}_down_wgrad")
        d_w_up_t, arrived[f"ffn_w_down{l}"] = _wgrad(da, n, f"ffn{l}_up_wgrad", _ScatterRider(d_w_down))
        rows = d_w_up_t.shape[0] // N_DEV
        first_rows = (rows // 2) // (2 * SUBLANES) * (2 * SUBLANES)
        dh_in, d_g, arrived[f"ffn_w_up{l}a"] = _dgrad_in_norm(
            da, ffn_w_up_t[l], h_in, norm_ffn_g[l:l + 1], dh_out, f"ffn{l}_up_dgrad",
            _ScatterRider(d_w_up_t, (0, first_rows)))
        return dh_in, d_cw, d_g, _ScatterRider(d_w_up_t, (first_rows, rows - first_rows))

    dh3, d_fcw1, d_fg1, up1_rest = ffn_bwd(dh4, h3, ffn1_saved, 1)

    dp2, dhb3, d_rg_cw, d_rg_cb, d_wg, d_rg_ba, d_rg_bx, d_rg_lam, arrived["ffn_w_up1b"] = _rg_out_bwd_fused(
        dh3, rg_w_out_t, p2, hs2, rg_cw, rg_cb, wg, wgt, rg_ba, rg_bx, rg_lam, tp, "rg_out_bwd", up1_rest)
    d_wa, d_wx = _unpair_gate_grads(d_wg, rg_w_gate_a.shape[2])
    d_rg_w_out = _wgrad(y2, dhb3, "rg_out_wgrad")
    d_rg_w_in_t, arrived["rg_w_out"] = _wgrad(dp2, n2, "rg_in_wgrad", _ScatterRider(d_rg_w_out))
    dh2, d_mg1, arrived["rg_w_in"] = _dgrad_in_norm(
        dp2, rg_w_in_t, h2, norm_mix_g[1:2], dh3, "rg_in_dgrad", _ScatterRider(d_rg_w_in_t))

    dh1, d_fcw0, d_fg0, up0_rest = ffn_bwd(dh2, h1, ffn0_saved, 0)

    dp0, dhb1, d_sc_cw, arrived["ffn_w_up0b"] = _sc_out_bwd_fused(dh1, sc_w_out_t, p0, sc_cw, tp, "sc_out_bwd", up0_rest)
    d_sc_w_out = _wgrad(q0, dhb1, "sc_out_wgrad")
    d_sc_w_in_t, arrived["sc_w_out"] = _wgrad(dp0, n0, "sc_in_wgrad", _ScatterRider(d_sc_w_out))
    d_mg0, d_meta, d_x, arrived["sc_w_in"] = _dgrad_in_to_input(
        dp0, sc_w_in_t, h0, norm_mix_g[0:1], dh1, n_meta, tp, "sc_in_dgrad", _ScatterRider(d_sc_w_in_t))
    grad_x = d_x[None]

    grads = {}
    for n in ("sc_w_in", "rg_w_in"):
        grads[n] = _sum_slots(arrived[n], f"sum_{n}").T[None]
    for n in ("sc_w_out", "rg_w_out"):
        grads[n] = _sum_slots(arrived[n], f"sum_{n}")[None]
    grads["ffn_w_up"] = jnp.stack([jnp.concatenate(
        [_sum_slots(arrived[f"ffn_w_up{l}{part}"], f"sum_ffn_w_up{l}{part}") for part in "ab"], axis=0).T
        for l in range(n_ffn)])
    grads["ffn_w_down"] = jnp.stack([_sum_slots(arrived[f"ffn_w_down{l}"], f"sum_ffn_w_down{l}") for l in range(n_ffn)])

    small_grads = {"meta_tokens": d_meta,"sc_conv_w": d_sc_cw[None], "rg_conv_w": d_rg_cw[None],
                   "rg_conv_b": d_rg_cb, "rg_b_gate_a": d_rg_ba, "rg_b_gate_x": d_rg_bx, "rg_lambda": d_rg_lam,
                   "ffn_conv_w": jnp.stack([d_fcw0, d_fcw1])}
    small_chunks = jnp.concatenate([_cols_to_chunks(small_grads[n]) for n in small_names], axis=1)
    pad = small_rows.shape[0] * d - small_chunks.shape[1]
    small_chunks = jnp.pad(small_chunks, ((0, 0), (0, pad))).reshape(N_DEV, small_rows.shape[0], d)
    rep_names = ["norm_mix_g", "norm_ffn_g", "final_norm_g", "rg_w_gate_a", "rg_w_gate_x"]
    rep_grads = {"norm_mix_g": jnp.concatenate([d_mg0, d_mg1], axis=0),
                 "norm_ffn_g": jnp.concatenate([d_fg0, d_fg1], axis=0),
                 "final_norm_g": d_final_g.reshape(-1), "rg_w_gate_a": d_wa[None], "rg_w_gate_x": d_wx[None]}
    rep_flat = jnp.concatenate([rep_grads[n].reshape(-1) for n in rep_names])
    rep_chunk_rows = -(-rep_flat.shape[0] // (N_DEV * d))
    rep_chunk_rows += -(small_rows.shape[0] + rep_chunk_rows) % 16
    rep_chunks = jnp.pad(rep_flat, (0, N_DEV * rep_chunk_rows * d - rep_flat.shape[0])).reshape(N_DEV, rep_chunk_rows, d)

    last_chunks = jnp.concatenate([small_chunks, rep_chunks], axis=1)
    reduced = _sum_slots(_ride_alone(_ScatterRider(last_chunks.reshape(-1, d)), "scatter_last")[0], "sum_last")

    small_red = reduced[0:small_rows.shape[0]].reshape(-1)
    o = small_rows.shape[0]
    so = 0
    for n, sz in zip(small_names, small_sizes):
        grads[n] = small_red[so:so + sz].reshape(weights[n].shape)
        so += sz
    rep_red = _ride_alone(_GatherRider([reduced[o:o + rep_chunks.shape[1]]]), "gather_replicated_grads")[0].reshape(-1)
    ro = 0
    for n in rep_names:
        sz = weights[n].size
        grads[n] = rep_red[ro:ro + sz].reshape(weights[n].shape)
        ro += sz

    delta, new_m, new_v = {}, {}, {}
    for n in names:
        delta[n], new_m[n], new_v[n] = _adamw_nd(weights[n], grads[n], m_in[n], v_in[n], f"adamw_{n}")

    return (loss, grad_x, *[grads[n] for n in names], *[delta[n] for n in names],
            *[new_m[n] for n in names], *[new_v[n] for n in names])
```

```python
import jax
import jax.numpy as jnp
from jax import lax
from jax.experimental import pallas as pl
from jax.experimental.pallas import tpu as pltpu

F32 = jnp.float32
MXU_DTYPE = jnp.bfloat16
RMS_EPS = 1e-6
RG_C = 8.0
ADAM_LR = 0.001
ADAM_B1 = 0.9
ADAM_B2 = 0.999
ADAM_EPS = 1e-08
ADAM_WD = 0.01
ADAM_STEP = 10

N_DEV = 8
AXES = ("x", "y", "c")
SUBLANES = 8
LANES = 128
VMEM_LIMIT_BYTES = 48 * 1024 * 1024
ROW_TILE_MATMUL = 700
ROW_TILE_WGRAD = 3300
ROW_TILE_PERM = 400
STRIP = 256
STRIP_FFN = 4096

_TN = (((0,), (0,)), ((), ()))
_NN = (((1,), (0,)), ((), ()))


def _row_tile(t, target):
    best = None
    for tm in range(16, t + 1, 16):
        if t % tm == 0 and tm <= target:
            best = tm
    return best if best is not None else t


def _col_tile(n, target):
    best = None
    for tn in range(LANES, n + 1, LANES):
        if n % tn == 0 and tn <= target:
            best = tn
    return best if best is not None else n


def _params(*sem):
    return pltpu.CompilerParams(dimension_semantics=sem, vmem_limit_bytes=VMEM_LIMIT_BYTES)


def _dot(a, b, dims):
    return lax.dot_general(a, b, dims, preferred_element_type=F32)


def _sigmoid(x):
    return 1.0 / (1.0 + jnp.exp(-x))


def _sigmoid_tanh(x):
    return 0.5 * jnp.tanh(0.5 * x) + 0.5


def _gelu(x):
    c = 0.7978845608028654
    t = jnp.tanh(c * (x + 0.044715 * (x * x * x)))
    return 0.5 * x * (1.0 + t)


def _gelu_and_grad(x):
    c = 0.7978845608028654
    x2 = x * x
    t = jnp.tanh(c * (x + 0.044715 * (x2 * x)))
    half = 0.5 * (1.0 + t)
    return x * half, half + 0.5 * x * (1.0 - t * t) * c * (1.0 + 3.0 * 0.044715 * x2)


def _softplus(x):
    return jnp.maximum(x, 0.0) + jnp.log1p(jnp.exp(-jnp.abs(x)))


def _time_scratch(tm, d):
    return [pltpu.VMEM((2, tm, d), F32), pltpu.VMEM((d // LANES, tm, LANES), F32), pltpu.SemaphoreType.DMA((2,))]


def _fetch_time_tile(i, nt, src_ref, head, tbuf, sems, tm):
    n_head = head.shape[0]

    def tile(j, slot):
        start = pl.multiple_of(j * tm - n_head, SUBLANES)
        return pltpu.make_async_copy(src_ref.at[pl.ds(start, tm)], tbuf.at[slot], sems.at[slot])

    first = pltpu.make_async_copy(src_ref.at[pl.ds(0, tm - n_head)], tbuf.at[0, pl.ds(n_head, tm - n_head)], sems.at[0])

    @pl.when(i == 0)
    def _():
        tbuf[0, 0:n_head, :] = head
        first.start()

    @pl.when(i + 1 < nt)
    def _():
        tile(i + 1, (i + 1) % 2).start()

    @pl.when(i == 0)
    def _():
        first.wait()

    @pl.when(i > 0)
    def _():
        tile(i, i % 2).wait()


def _time_to_tile_order(t_ref, slabs, out_ref):
    tm, d = out_ref.shape
    groups = tm // SUBLANES
    for k in range(d // LANES):
        slabs[k] = t_ref[:, k * LANES:(k + 1) * LANES]
    for k in range(d // LANES):
        for g in range(groups):
            out_ref[g * SUBLANES:(g + 1) * SUBLANES, k * LANES:(k + 1) * LANES] = slabs[k, pl.ds(g, SUBLANES, stride=groups), :]


def _tile_to_time_order(p_ref, slabs, t_ref):
    tm, d = p_ref.shape
    groups = tm // SUBLANES
    for k in range(d // LANES):
        for g in range(groups):
            slabs[k, pl.ds(g, SUBLANES, stride=groups), :] = p_ref[g * SUBLANES:(g + 1) * SUBLANES, k * LANES:(k + 1) * LANES]
    for k in range(d // LANES):
        t_ref[:, k * LANES:(k + 1) * LANES] = slabs[k]


def _rows(ref, sl):
    return [ref[k:k + 1, sl] for k in range(ref.shape[0])]


def _shift_down(x, before, s):
    if s == 0:
        return x
    n = x.shape[0]
    row = lax.broadcasted_iota(jnp.int32, (SUBLANES, x.shape[1]), 0)
    heads = []
    for g in range(s):
        v = x[n - (s - g) * SUBLANES:n - (s - g - 1) * SUBLANES]
        heads.append(pltpu.roll(jnp.where(row == SUBLANES - 1, before[s - g - 1], v), 1, axis=0))
    return jnp.concatenate(heads + [x[0:n - s * SUBLANES]], axis=0)


def _shift_up(x, after, s):
    if s == 0:
        return x
    row = lax.broadcasted_iota(jnp.int32, (SUBLANES, x.shape[1]), 0)
    tails = []
    for m in range(s):
        v = x[m * SUBLANES:(m + 1) * SUBLANES]
        tails.append(pltpu.roll(jnp.where(row == 0, after[m], v), SUBLANES - 1, axis=0))
    return jnp.concatenate([x[s * SUBLANES:]] + tails, axis=0)


def _weighted(w, windows):
    y = w[0] * windows[0]
    for k in range(1, len(w)):
        y = y + w[k] * windows[k]
    return y


def _windows(x, before, k_width):
    return [_shift_down(x, before, k_width - 1 - k) for k in range(k_width)]


def _conv_t(dy, after, w):
    k_width = len(w)
    return _weighted(w, [_shift_up(dy, after, k_width - 1 - k) for k in range(k_width)])


def _blocks(ref, sl, count, newest_first):
    n = ref.shape[0] // SUBLANES
    order = range(n - 1, n - 1 - count, -1) if newest_first else range(count)
    return [ref[b * SUBLANES:(b + 1) * SUBLANES, sl] for b in order]


def _halo_specs(tm, cols, count, tile_of):
    def spec(k):
        return pl.BlockSpec((SUBLANES, cols), lambda i: (jnp.maximum(tile_of(i) * (tm // SUBLANES) - k, 0), 0))
    return [spec(k) for k in range(1, count + 1)]


def _scan_tile(coef, val, out, carry, groups, reverse):
    cols = coef.shape[1]
    row = lax.broadcasted_iota(jnp.int32, (SUBLANES, cols), 0)

    def blk(i):
        g = groups - 1 - i if reverse else i
        return pl.ds(pl.multiple_of(g * SUBLANES, SUBLANES), SUBLANES)

    def local(i, pl_):
        p_prev, l_prev = pl_
        a = coef[blk(i), :]
        p = a * p_prev
        l = a * l_prev + val[blk(i), :]
        coef[blk(i), :] = p
        val[blk(i), :] = l
        return p, l

    pf, lf = lax.fori_loop(0, groups, local, (jnp.ones((SUBLANES, cols), F32), jnp.zeros((SUBLANES, cols), F32)))
    for s in (1, 2, 4):
        keep, sh = (row < SUBLANES - s, SUBLANES - s) if reverse else (row >= s, s)
        p_s = jnp.where(keep, pltpu.roll(pf, sh, axis=0), 1.0)
        l_s = jnp.where(keep, pltpu.roll(lf, sh, axis=0), 0.0)
        lf = pf * l_s + lf
        pf = pf * p_s
    end = lf + pf * carry
    if reverse:
        init = jnp.where(row == SUBLANES - 1, carry, pltpu.roll(end, SUBLANES - 1, axis=0))
        leaving = jnp.broadcast_to(end[0:1, :], (SUBLANES, cols))
    else:
        init = jnp.where(row == 0, carry, pltpu.roll(end, 1, axis=0))
        leaving = jnp.broadcast_to(end[SUBLANES - 1:SUBLANES, :], (SUBLANES, cols))

    def fix(i, _):
        out[blk(i), :] = val[blk(i), :] + coef[blk(i), :] * init
        return 0

    lax.fori_loop(0, groups, fix, 0)
    return leaving


def _resident(shape):
    return pl.BlockSpec(shape, lambda *_: (0,) * len(shape), pipeline_mode=pl.Buffered(1))


def _full(shape):
    return pl.BlockSpec(shape, lambda *_: (0,) * len(shape))


def _rmsnorm_to(h_ref, g_ref, n_ref):
    x = h_ref[...]
    ms = jnp.mean(x * x, axis=-1, keepdims=True)
    n_ref[...] = (x * lax.rsqrt(ms + RMS_EPS) * g_ref[...]).astype(n_ref.dtype)


_ANY = pl.BlockSpec(memory_space=pl.ANY)
_MESH = pl.DeviceIdType.MESH


def _dma_sems(n):
    return pltpu.SemaphoreType.DMA((n,))


def _when_each(*phases):
    for cond, fn in phases:
        if cond is not None:
            pl.when(cond)(fn)


class _NoRider:
    inputs = in_specs = out_shape = out_specs = scratch_shapes = ()

    def __call__(self, first, middle, last, ins, outs, scratch):
        pass


_NO_RIDER = _NoRider()


def _hand_on_step(nt):
    return (3 * nt) // 4


def _split_refs(rider, n_out, n_scratch, rest):
    a = len(rider.inputs)
    b = a + n_out
    c = b + len(rider.out_shape)
    e = c + n_scratch
    return rest[:a], rest[a:b], rest[b:c], rest[c:e], rest[e:]


class _GatherRider:
    def __init__(self, blocks):
        n = len(blocks)
        self.inputs = tuple(blocks)
        self.in_specs = (_ANY,) * n
        self.out_shape = tuple(jax.ShapeDtypeStruct((N_DEV,) + b.shape, b.dtype) for b in blocks)
        self.out_specs = (_ANY,) * n
        self.scratch_shapes = (_dma_sems(7 * n), _dma_sems(7 * n), _dma_sems(n))

    def __call__(self, first, middle, last, ins, outs, scratch):
        n = len(ins)
        send_sems, recv_sems, local_sems = scratch
        x, y, c = lax.axis_index("x"), lax.axis_index("y"), lax.axis_index("c")
        me, sibling = (x, y, c), (x, y, 1 - c)
        chips = [(1 - x, y), (x, 1 - y), (1 - x, 1 - y)]

        def slot(b, px, py, pc):
            return outs[b].at[4 * px + 2 * py + pc]

        def copy(k, b, block, to, src=None):
            return pltpu.make_async_remote_copy(
                src_ref=slot(b, *block) if src is None else src, dst_ref=slot(b, *block),
                send_sem=send_sems.at[k * n + b], recv_sem=recv_sems.at[k * n + b], device_id=to, device_id_type=_MESH)

        mine = [pltpu.make_async_copy(ins[b], slot(b, *me), local_sems.at[b]) for b in range(n)]
        own = [copy(0, b, me, sibling, src=ins[b]) for b in range(n)]
        own += [copy(1 + j, b, me, (*chip, c), src=ins[b]) for j, chip in enumerate(chips) for b in range(n)]
        passed = [[copy(4 + j, b, (*chip, c), sibling) for b in range(n)] for j, chip in enumerate(chips)]

        def at_first():
            for cp in mine + own:
                cp.start()

        def at_middle():
            for j, chip in enumerate(chips):
                for b in range(n):
                    copy(1 + j, b, (*chip, c), me).wait_recv()
                    passed[j][b].start()

        def at_last():
            for b in range(n):
                copy(0, b, sibling, me).wait_recv()
                for j, chip in enumerate(chips):
                    copy(4 + j, b, (*chip, 1 - c), me).wait_recv()
            for cp in own + [cp for group in passed for cp in group]:
                cp.wait_send()
            for cp in mine:
                cp.wait()

        _when_each((first, at_first), (middle, at_middle), (last, at_last))


class _ScatterRider:
    def __init__(self, grad, rows=None):
        r = grad.shape[0] // N_DEV
        chunks = grad.reshape(N_DEV, r, grad.shape[1])
        self.rows = rows if rows is not None else (0, r)
        self.inputs = (chunks,)
        self.in_specs = (_ANY,)
        self.out_shape = (jax.ShapeDtypeStruct((N_DEV, self.rows[1], grad.shape[1]), chunks.dtype),)
        self.out_specs = (_ANY,)
        self.scratch_shapes = (_dma_sems(N_DEV - 1), _dma_sems(N_DEV - 1), pltpu.SemaphoreType.DMA(()))

    def __call__(self, first, middle, last, ins, outs, scratch):
        (g_ref,), (r_ref,) = ins, outs
        send_sems, recv_sems, local_sem = scratch
        x, y, c = lax.axis_index("x"), lax.axis_index("y"), lax.axis_index("c")
        me = 4 * x + 2 * y + c
        part = pl.ds(*self.rows)
        mine = pltpu.make_async_copy(g_ref.at[me, part], r_ref.at[me], local_sem)
        copies = []
        for k in range(1, N_DEV):
            px, py, pc = (1 - x if k & 4 else x), (1 - y if k & 2 else y), (1 - c if k & 1 else c)
            copies.append(pltpu.make_async_remote_copy(
                src_ref=g_ref.at[4 * px + 2 * py + pc, part], dst_ref=r_ref.at[me],
                send_sem=send_sems.at[k - 1], recv_sem=recv_sems.at[k - 1],
                device_id=(px, py, pc), device_id_type=_MESH))

        def at_first():
            mine.start()
            for cp in copies:
                cp.start()

        def at_last():
            for cp in copies:
                cp.wait()
            mine.wait()

        _when_each((first, at_first), (last, at_last))


def _matmul_residual(q, w, h, name):
    t, k = q.shape
    d = w.shape[1]
    tm = _row_tile(t, ROW_TILE_MATMUL)

    def body(q_ref, w_ref, h_ref, o_ref):
        o_ref[...] = h_ref[...] + _dot(q_ref[...], w_ref[...], _NN)

    return pl.pallas_call(
        body, name=name, grid=(t // tm,),
        in_specs=[pl.BlockSpec((tm, k), lambda i: (i, 0)),
                  _resident((k, d)),
                  pl.BlockSpec((tm, d), lambda i: (i, 0))],
        out_specs=pl.BlockSpec((tm, d), lambda i: (i, 0)),
        out_shape=jax.ShapeDtypeStruct((t, d), F32),
        compiler_params=_params("parallel"),
    )(q, w, h)


def _dgrad_in_norm(dp, wt, h, g, dh_next, name, rider=_NO_RIDER):
    t, n = dp.shape
    d = wt.shape[1]
    tm = _row_tile(t, ROW_TILE_MATMUL)
    nt = t // tm

    def body(dp_ref, wt_ref, h_ref, g_ref, dhn_ref, *rest):
        r_in, (dh_ref, dg_ref), r_out, _, r_scratch = _split_refs(rider, 2, 0, rest)
        i = pl.program_id(0)
        rider(i == 0, i == nt // 2, None, r_in, r_out, r_scratch)
        dn = _dot(dp_ref[...], wt_ref[...], _NN)
        x = h_ref[...]
        ms = jnp.mean(x * x, axis=-1, keepdims=True)
        r = lax.rsqrt(ms + RMS_EPS)
        xhat = x * r
        dng = dn * g_ref[...]
        c = jnp.mean(dng * xhat, axis=-1, keepdims=True)
        dh_ref[...] = dhn_ref[...] + r * (dng - xhat * c)
        part = jnp.sum(dn * xhat, axis=0, keepdims=True)

        @pl.when(i == 0)
        def _():
            dg_ref[...] = part

        @pl.when(i > 0)
        def _():
            dg_ref[...] += part

        rider(None, None, i == nt - 1, r_in, r_out, r_scratch)

    return pl.pallas_call(
        body, name=name, grid=(nt,),
        in_specs=[pl.BlockSpec((tm, n), lambda i: (i, 0)),
                  _resident((n, d)),
                  pl.BlockSpec((tm, d), lambda i: (i, 0)),
                  pl.BlockSpec((1, d), lambda i: (0, 0)),
                  pl.BlockSpec((tm, d), lambda i: (i, 0))] + list(rider.in_specs),
        out_specs=[pl.BlockSpec((tm, d), lambda i: (i, 0)),
                   pl.BlockSpec((1, d), lambda i: (0, 0))] + list(rider.out_specs),
        out_shape=[jax.ShapeDtypeStruct((t, d), F32), jax.ShapeDtypeStruct((1, d), F32)] + list(rider.out_shape),
        scratch_shapes=list(rider.scratch_shapes),
        compiler_params=_params("arbitrary"),
    )(dp, wt, h, g, dh_next, *rider.inputs)


def _dgrad_in_to_input(dp, wt, h, g, dh_next, n_head, tm, name, rider=_NO_RIDER):
    t, n = dp.shape
    d = wt.shape[1]
    nt = t // tm

    def body(dp_ref, wt_ref, h_ref, g_ref, dhn_ref, *rest):
        r_in, (dg_ref, head_ref, rest_ref), r_out, (pbuf, tout, slabs, sems), r_scratch = _split_refs(rider, 3, 4, rest)
        i = pl.program_id(0)
        rider(i == 0, i == nt // 2, None, r_in, r_out, r_scratch)
        dn = _dot(dp_ref[...], wt_ref[...], _NN)
        x = h_ref[...]
        ms = jnp.mean(x * x, axis=-1, keepdims=True)
        r = lax.rsqrt(ms + RMS_EPS)
        xhat = x * r
        dng = dn * g_ref[...]
        c = jnp.mean(dng * xhat, axis=-1, keepdims=True)
        pbuf[...] = dhn_ref[...] + r * (dng - xhat * c)
        part = jnp.sum(dn * xhat, axis=0, keepdims=True)

        @pl.when(i == 0)
        def _():
            dg_ref[...] = part

        @pl.when(i > 0)
        def _():
            dg_ref[...] += part

        def store(j):
            if isinstance(j, int) and j == 0:
                return pltpu.make_async_copy(tout.at[0, pl.ds(n_head, tm - n_head)],
                                             rest_ref.at[pl.ds(0, tm - n_head)], sems.at[0])
            start = j * tm - n_head
            start = start if isinstance(start, int) else pl.multiple_of(start, SUBLANES)
            return pltpu.make_async_copy(tout.at[j % 2], rest_ref.at[pl.ds(start, tm)], sems.at[j % 2])

        @pl.when(i == 2)
        def _():
            store(0).wait()

        @pl.when(i > 2)
        def _():
            store(i - 2).wait()

        _tile_to_time_order(pbuf, slabs, tout.at[i % 2])

        @pl.when(i == 0)
        def _():
            head_ref[...] = tout[0, 0:n_head, :]
            store(0).start()

        @pl.when(i > 0)
        def _():
            store(i).start()

        @pl.when(i == nt - 1)
        def _():
            for j in (nt - 2, nt - 1):
                if j >= 0:
                    store(j).wait()

        rider(None, None, i == nt - 1, r_in, r_out, r_scratch)

    return pl.pallas_call(
        body, name=name, grid=(nt,),
        in_specs=[pl.BlockSpec((tm, n), lambda i: (i, 0)),
                  _resident((n, d)),
                  pl.BlockSpec((tm, d), lambda i: (i, 0)),
                  pl.BlockSpec((1, d), lambda i: (0, 0)),
                  pl.BlockSpec((tm, d), lambda i: (i, 0))] + list(rider.in_specs),
        out_specs=[pl.BlockSpec((1, d), lambda i: (0, 0)), _full((n_head, d)), _ANY] + list(rider.out_specs),
        out_shape=[jax.ShapeDtypeStruct((1, d), F32), jax.ShapeDtypeStruct((n_head, d), F32),
                   jax.ShapeDtypeStruct((t - n_head, d), F32)] + list(rider.out_shape),
        scratch_shapes=[pltpu.VMEM((tm, d), F32)] + _time_scratch(tm, d) + list(rider.scratch_shapes),
        compiler_params=_params("arbitrary"),
    )(dp, wt, h, g, dh_next, *rider.inputs)


def _wgrad(a, b, name, rider=_NO_RIDER):
    t, m = a.shape
    d = b.shape[1]
    tmm, tk = _col_tile(m, 1024), _row_tile(t, ROW_TILE_WGRAD)
    if tmm < 512:
        tmm, tk = _col_tile(m, 1536), _row_tile(t, ROW_TILE_MATMUL)
    nm, nk = m // tmm, t // tk

    def body(a_ref, b_ref, *rest):
        r_in, (o_ref,), r_out, _, r_scratch = _split_refs(rider, 1, 0, rest)
        i, k = pl.program_id(0), pl.program_id(1)
        rider(jnp.logical_and(i == 0, k == 0), None, None, r_in, r_out, r_scratch)

        @pl.when(k == 0)
        def _():
            o_ref[...] = jnp.zeros_like(o_ref)

        o_ref[...] += _dot(a_ref[...], b_ref[...], _TN)
        rider(None, None, jnp.logical_and(i == nm - 1, k == nk - 1), r_in, r_out, r_scratch)

    outs = pl.pallas_call(
        body, name=name, grid=(nm, nk),
        in_specs=[pl.BlockSpec((tk, tmm), lambda i, k: (k, i)),
                  pl.BlockSpec((tk, d), lambda i, k: (k, 0))] + list(rider.in_specs),
        out_specs=[pl.BlockSpec((tmm, d), lambda i, k: (i, 0))] + list(rider.out_specs),
        out_shape=[jax.ShapeDtypeStruct((m, d), F32)] + list(rider.out_shape),
        scratch_shapes=list(rider.scratch_shapes),
        compiler_params=_params("arbitrary", "arbitrary"),
    )(a, b, *rider.inputs)
    return outs if rider.out_shape else outs[0]


def _ffn_up_fused(h, g, w, conv_w, tm, name, rider=_NO_RIDER):
    t, d = h.shape
    f2 = w.shape[1]
    f = f2 // 2
    cw = min(STRIP_FFN, f)
    kw = conv_w.shape[0]
    nh = kw - 1
    nt = t // tm

    def body(h_ref, g_ref, w_ref, cw_ref, *rest):
        r_in, (n_ref, a_ref, z_ref), r_out, (carry,), r_scratch = _split_refs(rider, 3, 1, rest)
        i = pl.program_id(0)
        rider(i == 0, i == _hand_on_step(nt), None, r_in, r_out, r_scratch)

        @pl.when(i == 0)
        def _():
            carry[...] = jnp.zeros_like(carry)

        _rmsnorm_to(h_ref, g_ref, n_ref)
        for c in range(0, f, cw):
            conv = []
            for sl in (slice(c, c + cw), slice(f + c, f + c + cw)):
                a = _dot(n_ref[...], w_ref[:, sl], _NN)
                a_ref[:, sl] = a
                conv.append(_weighted(_rows(cw_ref, sl), _windows(a, _blocks(carry, sl, nh, True), kw)))
                carry[:, sl] = a[tm - nh * SUBLANES:tm]
            gg, vv = conv
            z_ref[:, c:c + cw] = (gg * _sigmoid(gg) * vv).astype(z_ref.dtype)
        rider(None, None, i == nt - 1, r_in, r_out, r_scratch)

    return pl.pallas_call(
        body, name=name, grid=(nt,),
        in_specs=[pl.BlockSpec((tm, d), lambda i: (i, 0)), pl.BlockSpec((1, d), lambda i: (0, 0)),
                  _resident((d, f2)), _full(conv_w.shape)] + list(rider.in_specs),
        out_specs=[pl.BlockSpec((tm, d), lambda i: (i, 0)), pl.BlockSpec((tm, f2), lambda i: (i, 0)),
                   pl.BlockSpec((tm, f), lambda i: (i, 0))] + list(rider.out_specs),
        out_shape=[jax.ShapeDtypeStruct((t, d), MXU_DTYPE), jax.ShapeDtypeStruct((t, f2), F32),
                   jax.ShapeDtypeStruct((t, f), MXU_DTYPE)] + list(rider.out_shape),
        scratch_shapes=[pltpu.VMEM((nh * SUBLANES, f2), F32)] + list(rider.scratch_shapes),
        compiler_params=_params("arbitrary"),
    )(h, g, w, conv_w, *rider.inputs)


def _ffn_down_bwd_fused(dh, wdt, a, conv_w, tm, name):
    t, d = dh.shape
    f = wdt.shape[1]
    f2 = 2 * f
    nt = t // tm
    cw = min(STRIP_FFN, f)
    kw = conv_w.shape[0]
    nh = kw - 1
    rev = lambda i: nt - 1 - i

    def body(dh_ref, wdt_ref, a_ref, *rest):
        halo_refs = rest[:nh]
        cw_ref, da_ref, dhb_ref, dw_ref, carry = rest[nh:]
        i = pl.program_id(0)
        newest, oldest = i == 0, i == nt - 1

        @pl.when(newest)
        def _():
            carry[...] = jnp.zeros_like(carry)
            dw_ref[...] = jnp.zeros_like(dw_ref)

        dhb_ref[...] = dh_ref[...].astype(dhb_ref.dtype)
        for c in range(0, f, cw):
            gsl, vsl = slice(c, c + cw), slice(f + c, f + c + cw)
            dz = _dot(dhb_ref[...], wdt_ref[:, gsl], _NN)
            win, conv = {}, {}
            for sl in (gsl, vsl):
                before = [jnp.where(oldest, 0.0, r[:, sl]) for r in halo_refs]
                win[sl.start] = _windows(a_ref[:, sl], before, kw)
                conv[sl.start] = _weighted(_rows(cw_ref, sl), win[sl.start])
            gg, vv = conv[gsl.start], conv[vsl.start]
            s = _sigmoid(gg)
            grads = {gsl.start: dz * vv * s * (1.0 + gg * (1.0 - s)), vsl.start: dz * gg * s}
            for sl in (gsl, vsl):
                dconv = grads[sl.start]
                da_ref[:, sl] = _conv_t(dconv, _blocks(carry, sl, nh, False), _rows(cw_ref, sl)).astype(da_ref.dtype)
                for k in range(kw):
                    dw_ref[k:k + 1, sl] += jnp.sum(dconv * win[sl.start][k], axis=0, keepdims=True)
                carry[:, sl] = dconv[0:nh * SUBLANES]

    return pl.pallas_call(
        body, name=name, grid=(nt,),
        in_specs=[pl.BlockSpec((tm, d), lambda i: (rev(i), 0)), _resident((d, f)),
                  pl.BlockSpec((tm, f2), lambda i: (rev(i), 0))] + _halo_specs(tm, f2, nh, rev) + [_full(conv_w.shape)],
        out_specs=[pl.BlockSpec((tm, f2), lambda i: (rev(i), 0)), pl.BlockSpec((tm, d), lambda i: (rev(i), 0)),
                   _full(conv_w.shape)],
        out_shape=[jax.ShapeDtypeStruct((t, f2), MXU_DTYPE), jax.ShapeDtypeStruct((t, d), MXU_DTYPE),
                   jax.ShapeDtypeStruct(conv_w.shape, F32)],
        scratch_shapes=[pltpu.VMEM((nh * SUBLANES, f2), F32)],
        compiler_params=_params("arbitrary"),
    )(dh, wdt, a, *([a] * nh), conv_w)


def _sc_in_fused(x, meta, g, w, conv_w, tm, name, rider=_NO_RIDER):
    d = x.shape[1]
    t = x.shape[0] + meta.shape[0]
    cw = min(STRIP, d)
    kw = conv_w.shape[0]
    nh = kw - 1
    nt = t // tm

    def body(x_ref, meta_ref, g_ref, w_ref, cw_ref, *rest):
        r_in, (h_ref, n_ref, p_ref, q_ref), r_out, (carry, tbuf, slabs, sems), r_scratch = _split_refs(rider, 4, 4, rest)
        i = pl.program_id(0)
        rider(i == 0, i == _hand_on_step(nt), None, r_in, r_out, r_scratch)

        @pl.when(i == 0)
        def _():
            carry[...] = jnp.zeros_like(carry)

        _fetch_time_tile(i, nt, x_ref, meta_ref[...], tbuf, sems, tm)
        _time_to_tile_order(tbuf.at[i % 2], slabs, h_ref)
        _rmsnorm_to(h_ref, g_ref, n_ref)
        for c in range(0, d, cw):
            sl = slice(c, c + cw)
            parts = []
            for base in (0, d, 2 * d):
                psl = slice(base + c, base + c + cw)
                parts.append(_dot(n_ref[...], w_ref[:, psl], _NN))
                p_ref[:, psl] = parts[-1]
            bg, cg, v = parts
            cv = cg * v
            u = _weighted(_rows(cw_ref, sl), _windows(cv, _blocks(carry, sl, nh, True), kw))
            carry[:, sl] = cv[tm - nh * SUBLANES:tm]
            q_ref[:, sl] = (bg * u).astype(q_ref.dtype)
        rider(None, None, i == nt - 1, r_in, r_out, r_scratch)

    row_tile = lambda cols: pl.BlockSpec((tm, cols), lambda i: (i, 0))
    return pl.pallas_call(
        body, name=name, grid=(nt,),
        in_specs=[_ANY, _full(meta.shape), pl.BlockSpec((1, d), lambda i: (0, 0)),
                  _resident((d, 3 * d)), _full(conv_w.shape)] + list(rider.in_specs),
        out_specs=[row_tile(d), row_tile(d), row_tile(3 * d), row_tile(d)] + list(rider.out_specs),
        out_shape=[jax.ShapeDtypeStruct((t, d), F32), jax.ShapeDtypeStruct((t, d), MXU_DTYPE),
                   jax.ShapeDtypeStruct((t, 3 * d), F32), jax.ShapeDtypeStruct((t, d), MXU_DTYPE)]
                  + list(rider.out_shape),
        scratch_shapes=[pltpu.VMEM((nh * SUBLANES, d), F32)] + _time_scratch(tm, d) + list(rider.scratch_shapes),
        compiler_params=_params("arbitrary"),
    )(x, meta, g, w, conv_w, *rider.inputs)


def _sc_out_bwd_fused(dh, wot, p, conv_w, tm, name, rider=_NO_RIDER):
    t, d = dh.shape
    nt = t // tm
    cw = min(STRIP, d)
    kw = conv_w.shape[0]
    nh = kw - 1
    rev = lambda i: nt - 1 - i

    def body(dh_ref, wot_ref, p_ref, *rest):
        halo_refs, cw_ref = rest[:nh], rest[nh]
        r_in, (dp_ref, dhb_ref, dw_ref), r_out, (carry,), r_scratch = _split_refs(rider, 3, 1, rest[nh + 1:])
        i = pl.program_id(0)
        newest, oldest = i == 0, i == nt - 1
        rider(newest, i == nt // 2, None, r_in, r_out, r_scratch)

        @pl.when(newest)
        def _():
            carry[...] = jnp.zeros_like(carry)
            dw_ref[...] = jnp.zeros_like(dw_ref)

        dhb_ref[...] = dh_ref[...].astype(dhb_ref.dtype)
        for c in range(0, d, cw):
            sl, csl, vsl = slice(c, c + cw), slice(d + c, d + c + cw), slice(2 * d + c, 2 * d + c + cw)
            w = _rows(cw_ref, sl)
            dq = _dot(dhb_ref[...], wot_ref[:, sl], _NN)
            bg, cg, v = p_ref[:, sl], p_ref[:, csl], p_ref[:, vsl]
            before = [jnp.where(oldest, 0.0, r[:, csl] * r[:, vsl]) for r in halo_refs]
            win = _windows(cg * v, before, kw)
            du = dq * bg
            dcv = _conv_t(du, _blocks(carry, sl, nh, False), w)
            dp_ref[:, sl] = (dq * _weighted(w, win)).astype(dp_ref.dtype)
            dp_ref[:, csl] = (dcv * v).astype(dp_ref.dtype)
            dp_ref[:, vsl] = (dcv * cg).astype(dp_ref.dtype)
            for k in range(kw):
                dw_ref[k:k + 1, sl] += jnp.sum(du * win[k], axis=0, keepdims=True)
            carry[:, sl] = du[0:nh * SUBLANES]
        rider(None, None, oldest, r_in, r_out, r_scratch)

    tile = lambda cols: pl.BlockSpec((tm, cols), lambda i: (rev(i), 0))
    return pl.pallas_call(
        body, name=name, grid=(nt,),
        in_specs=[tile(d), _resident((d, d)), tile(3 * d)] + _halo_specs(tm, 3 * d, nh, rev) + [_full(conv_w.shape)]
                 + list(rider.in_specs),
        out_specs=[tile(3 * d), tile(d), _full(conv_w.shape)] + list(rider.out_specs),
        out_shape=[jax.ShapeDtypeStruct((t, 3 * d), MXU_DTYPE), jax.ShapeDtypeStruct((t, d), MXU_DTYPE),
                   jax.ShapeDtypeStruct(conv_w.shape, F32)] + list(rider.out_shape),
        scratch_shapes=[pltpu.VMEM((nh * SUBLANES, d), F32)] + list(rider.scratch_shapes),
        compiler_params=_params("arbitrary"),
    )(dh, wot, p, *([p] * nh), conv_w, *rider.inputs)


def _pair_gate_weights(wa, wx):
    nb, bd, _ = wa.shape
    zero = jnp.zeros((nb // 2, bd, bd), wa.dtype)

    def pair(w):
        w = w.reshape(nb // 2, 2, bd, bd)
        top = jnp.concatenate([w[:, 0], zero], axis=2)
        bottom = jnp.concatenate([zero, w[:, 1]], axis=2)
        return jnp.concatenate([top, bottom], axis=1)

    both = jnp.concatenate([pair(wa), pair(wx)], axis=2)
    return both, jnp.swapaxes(both, 1, 2)


def _unpair_gate_grads(dw, bd):
    def blocks(cols):
        d0 = dw[:, :bd, cols:cols + bd]
        d1 = dw[:, bd:, cols + bd:cols + 2 * bd]
        return jnp.stack([d0, d1], axis=1).reshape(-1, bd, bd)
    return blocks(0), blocks(2 * bd)


def _rg_gates(u, k, wg_ref, ba, bx, lam):
    ub = u.astype(MXU_DTYPE)
    z = _dot(ub, wg_ref[k], _NN)
    half = z.shape[1] // 2
    r = _sigmoid_tanh(z[:, :half] + ba)
    ig = _sigmoid_tanh(z[:, half:] + bx)
    sp = _softplus(-lam)
    la = -RG_C * r * sp
    a = jnp.exp(la)
    th = jnp.tanh(la)
    mult = jnp.sqrt(-2.0 * th / (1.0 - th))
    return r, ig, a, mult


def _rg_in_fused(h, g, w, conv_w, conv_b, wg, ba, bx, lam, tm, name, rider=_NO_RIDER):
    t, d = h.shape
    r2 = w.shape[1]
    rr = r2 // 2
    nb, bd = wg.shape[0], wg.shape[1]
    kw = conv_w.shape[0]
    nh = kw - 1
    groups = tm // SUBLANES
    nt = t // tm

    def body(h_ref, g_ref, w_ref, cw_ref, cb_ref, wg_ref, ba_ref, bx_ref, lam_ref, *rest):
        r_in, (n_ref, p_ref, y_ref, hs_ref), r_out, (a_scr, b_scr, carry_rb, carry_h), r_scratch = _split_refs(
            rider, 4, 4, rest)
        i = pl.program_id(0)
        rider(i == 0, i == _hand_on_step(nt), None, r_in, r_out, r_scratch)

        @pl.when(i == 0)
        def _():
            carry_rb[...] = jnp.zeros_like(carry_rb)
            carry_h[...] = jnp.zeros_like(carry_h)

        _rmsnorm_to(h_ref, g_ref, n_ref)
        for k in range(nb):
            sl = slice(k * bd, (k + 1) * bd)
            rsl = slice(rr + k * bd, rr + (k + 1) * bd)
            p_ref[:, sl] = _dot(n_ref[...], w_ref[:, sl], _NN)
            rb = _dot(n_ref[...], w_ref[:, rsl], _NN)
            p_ref[:, rsl] = rb
            u = _weighted(_rows(cw_ref, sl), _windows(rb, _blocks(carry_rb, sl, nh, True), kw)) + cb_ref[:, sl]
            carry_rb[:, sl] = rb[tm - nh * SUBLANES:tm]
            _, ig, a, mult = _rg_gates(u, k, wg_ref, ba_ref[:, sl], bx_ref[:, sl], lam_ref[:, sl])
            a_scr[:, sl] = a
            b_scr[:, sl] = mult * (ig * u)

        carry_h[...] = _scan_tile(a_scr, b_scr, hs_ref, carry_h[...], groups, reverse=False)

        for k in range(nb):
            sl = slice(k * bd, (k + 1) * bd)
            y_ref[:, sl] = (hs_ref[:, sl] * _gelu(p_ref[:, sl])).astype(y_ref.dtype)
        rider(None, None, i == nt - 1, r_in, r_out, r_scratch)

    vm = lambda rows: pltpu.VMEM((rows, rr), F32)
    return pl.pallas_call(
        body, name=name, grid=(nt,),
        in_specs=[pl.BlockSpec((tm, d), lambda i: (i, 0)), pl.BlockSpec((1, d), lambda i: (0, 0)),
                  _resident((d, r2)), _full(conv_w.shape), _full(conv_b.shape), _full(wg.shape), _full(ba.shape),
                  _full(bx.shape), _full(lam.shape)] + list(rider.in_specs),
        out_specs=[pl.BlockSpec((tm, d), lambda i: (i, 0)), pl.BlockSpec((tm, r2), lambda i: (i, 0)),
                   pl.BlockSpec((tm, rr), lambda i: (i, 0)), pl.BlockSpec((tm, rr), lambda i: (i, 0))]
                  + list(rider.out_specs),
        out_shape=[jax.ShapeDtypeStruct((t, d), MXU_DTYPE), jax.ShapeDtypeStruct((t, r2), F32),
                   jax.ShapeDtypeStruct((t, rr), MXU_DTYPE), jax.ShapeDtypeStruct((t, rr), F32)]
                  + list(rider.out_shape),
        scratch_shapes=[vm(tm), vm(tm), vm(nh * SUBLANES), vm(SUBLANES)] + list(rider.scratch_shapes),
        compiler_params=_params("arbitrary"),
    )(h, g, w, conv_w, conv_b, wg, ba, bx, lam, *rider.inputs)


def _rg_out_bwd_fused(dh, wot, p, hs, conv_w, conv_b, wg, wgt, ba, bx, lam, tm, name, rider=_NO_RIDER):
    t, d = dh.shape
    r2 = p.shape[1]
    rr = r2 // 2
    nb, bd = wg.shape[0], wg.shape[1]
    nt = t // tm
    kw = conv_w.shape[0]
    nh = kw - 1
    groups = tm // SUBLANES
    rev = lambda i: nt - 1 - i

    def body(dh_ref, wot_ref, p_ref, hs_ref, hsh_ref, *rest):
        halo_refs = rest[:nh]
        cw_ref, cb_ref, wg_ref, wgt_ref, ba_ref, bx_ref, lam_ref = rest[nh:nh + 7]
        (r_in, (dp_ref, dhb_ref, dcw_ref, dcb_ref, dwg_ref, dba_ref, dbx_ref, dlam_ref), r_out,
         (a_scr, as_scr, u_scr, r_scr, ig_scr, mult_scr, g_scr, carry_g, carry_du), r_scratch) = _split_refs(
            rider, 8, 9, rest[nh + 7:])
        i = pl.program_id(0)
        newest, oldest = i == 0, i == nt - 1
        rider(newest, i == nt // 2, None, r_in, r_out, r_scratch)

        @pl.when(newest)
        def _():
            carry_g[...] = jnp.zeros_like(carry_g)
            carry_du[...] = jnp.zeros_like(carry_du)
            for ref in (dcw_ref, dcb_ref, dwg_ref, dba_ref, dbx_ref, dlam_ref):
                ref[...] = jnp.zeros_like(ref)

        dhb_ref[...] = dh_ref[...].astype(dhb_ref.dtype)
        ones = jnp.ones((SUBLANES, bd), F32)

        def conv_windows(sl, rsl):
            before = [jnp.where(oldest, 0.0, r[:, rsl]) for r in halo_refs]
            return _windows(p_ref[:, rsl], before, kw)

        for k in range(nb):
            sl = slice(k * bd, (k + 1) * bd)
            rsl = slice(rr + k * bd, rr + (k + 1) * bd)
            dy = _dot(dhb_ref[...], wot_ref[:, sl], _NN)
            u = _weighted(_rows(cw_ref, sl), conv_windows(sl, rsl)) + cb_ref[:, sl]
            r, ig, a, mult = _rg_gates(u, k, wg_ref, ba_ref[:, sl], bx_ref[:, sl], lam_ref[:, sl])
            gate, dgate = _gelu_and_grad(p_ref[:, sl])
            g_scr[:, sl] = dy * gate
            dp_ref[:, sl] = (dy * hs_ref[:, sl] * dgate).astype(dp_ref.dtype)
            a_scr[:, sl] = a
            as_scr[:, sl] = _shift_up(a, [ones], 1)
            u_scr[:, sl] = u
            r_scr[:, sl] = r
            ig_scr[:, sl] = ig
            mult_scr[:, sl] = mult

        g_first = _scan_tile(as_scr, g_scr, g_scr, carry_g[...], groups, reverse=True)
        carry_g[...] = jnp.broadcast_to(a_scr[0:1, :], (SUBLANES, rr)) * g_first

        for k in range(nb):
            sl = slice(k * bd, (k + 1) * bd)
            rsl = slice(rr + k * bd, rr + (k + 1) * bd)
            cw = _rows(cw_ref, sl)
            lam_k = lam_ref[:, sl]
            sp = _softplus(-lam_k)
            g = g_scr[:, sl]
            a, u, r, ig, mult = a_scr[:, sl], u_scr[:, sl], r_scr[:, sl], ig_scr[:, sl], mult_scr[:, sl]
            da = g * _shift_down(hs_ref[:, sl], [jnp.where(oldest, 0.0, hsh_ref[:, sl])], 1)
            dmult = g * (ig * u)
            d_iu = g * mult
            dla = da * a - dmult * (a * a) / mult
            dr = dla * (-RG_C * sp)
            dsp = jnp.sum(dla * (-RG_C * r), axis=0, keepdims=True)
            dlam_ref[:, sl] += dsp * (-_sigmoid(-lam_k))
            dza = dr * r * (1.0 - r)
            dzx = (d_iu * u) * ig * (1.0 - ig)
            dba_ref[:, sl] += jnp.sum(dza, axis=0, keepdims=True)
            dbx_ref[:, sl] += jnp.sum(dzx, axis=0, keepdims=True)
            ub = u.astype(MXU_DTYPE)
            dz = jnp.concatenate([dza, dzx], axis=1).astype(MXU_DTYPE)
            dwg_ref[k] += _dot(ub, dz, _TN)
            du = d_iu * ig + _dot(dz, wgt_ref[k], _NN)
            dcb_ref[:, sl] += jnp.sum(du, axis=0, keepdims=True)
            win = conv_windows(sl, rsl)
            for kk in range(kw):
                dcw_ref[kk:kk + 1, sl] += jnp.sum(du * win[kk], axis=0, keepdims=True)
            dp_ref[:, rsl] = _conv_t(du, _blocks(carry_du, sl, nh, False), cw).astype(dp_ref.dtype)
            carry_du[:, sl] = du[0:nh * SUBLANES]
        rider(None, None, oldest, r_in, r_out, r_scratch)

    tile = lambda cols: pl.BlockSpec((tm, cols), lambda i: (rev(i), 0))
    vm = lambda rows: pltpu.VMEM((rows, rr), F32)
    grads = [conv_w.shape, conv_b.shape, wg.shape, ba.shape, bx.shape, lam.shape]
    return pl.pallas_call(
        body, name=name, grid=(nt,),
        in_specs=[tile(d), _resident((d, rr)), tile(r2), tile(rr)] + _halo_specs(tm, rr, 1, rev)
                 + _halo_specs(tm, r2, nh, rev)
                 + [_full(conv_w.shape), _full(conv_b.shape), _full(wg.shape), _full(wgt.shape), _full(ba.shape),
                    _full(bx.shape), _full(lam.shape)] + list(rider.in_specs),
        out_specs=[tile(r2), tile(d)] + [_full(s) for s in grads] + list(rider.out_specs),
        out_shape=[jax.ShapeDtypeStruct((t, r2), MXU_DTYPE), jax.ShapeDtypeStruct((t, d), MXU_DTYPE)]
                  + [jax.ShapeDtypeStruct(s, F32) for s in grads] + list(rider.out_shape),
        scratch_shapes=[vm(tm)] * 7 + [vm(SUBLANES), vm(nh * SUBLANES)] + list(rider.scratch_shapes),
        compiler_params=_params("arbitrary"),
    )(dh, wot, p, hs, hs, *([p] * nh), conv_w, conv_b, wg, wgt, ba, bx, lam, *rider.inputs)


def _down_loss_head(z, w, h, g, target, n_meta, tm, name):
    t, d = h.shape
    k = z.shape[1]
    groups = tm // SUBLANES
    nt = t // tm

    def body(z_ref, w_ref, h_ref, g_ref, t_ref, dh_ref, loss_ref, dg_ref, tgt, tbuf, slabs, sems):
        i = pl.program_id(0)
        _fetch_time_tile(i, nt, t_ref, jnp.zeros((n_meta, d), F32), tbuf, sems, tm)
        _time_to_tile_order(tbuf.at[i % 2], slabs, tgt)
        x = h_ref[...] + _dot(z_ref[...], w_ref[...], _NN)
        ms = jnp.mean(x * x, axis=-1, keepdims=True)
        r = lax.rsqrt(ms + RMS_EPS)
        xhat = x * r
        gg = g_ref[...]
        row = lax.broadcasted_iota(jnp.int32, (tm, 1), 0)
        time = i * tm + jnp.right_shift(row, 3) + jnp.bitwise_and(row, SUBLANES - 1) * groups
        err = jnp.where(time >= n_meta, xhat * gg - tgt[...], 0.0)
        dout = err * (1.0 / d)
        dng = dout * gg
        c = jnp.mean(dng * xhat, axis=-1, keepdims=True)
        dh_ref[...] = r * (dng - xhat * c)
        part_dg = jnp.sum(dout * xhat, axis=0, keepdims=True)
        part_loss = jnp.broadcast_to(0.5 * jnp.sum(err * dout, keepdims=True), loss_ref.shape)

        @pl.when(i == 0)
        def _():
            dg_ref[...] = part_dg
            loss_ref[...] = part_loss

        @pl.when(i > 0)
        def _():
            dg_ref[...] += part_dg
            loss_ref[...] += part_loss

    return pl.pallas_call(
        body, name=name, grid=(nt,),
        in_specs=[pl.BlockSpec((tm, k), lambda i: (i, 0)),
                  _resident((k, d)),
                  pl.BlockSpec((tm, d), lambda i: (i, 0)),
                  pl.BlockSpec((1, d), lambda i: (0, 0)),
                  _ANY],
        out_specs=[pl.BlockSpec((tm, d), lambda i: (i, 0)),
                   pl.BlockSpec((SUBLANES, LANES), lambda i: (0, 0)),
                   pl.BlockSpec((1, d), lambda i: (0, 0))],
        out_shape=[jax.ShapeDtypeStruct((t, d), F32), jax.ShapeDtypeStruct((SUBLANES, LANES), F32),
                   jax.ShapeDtypeStruct((1, d), F32)],
        scratch_shapes=[pltpu.VMEM((tm, d), F32)] + _time_scratch(tm, d),
        compiler_params=_params("arbitrary"),
    )(z, w, h, g, target)


def _adamw(w, g, m, v, name):
    rows, cols = w.shape
    tr = rows
    if rows > 512:
        for cand in range(8, 513, 8):
            if rows % cand == 0:
                tr = cand

    def body(w_ref, g_ref, m_ref, v_ref, d_ref, nm_ref, nv_ref):
        g_ = g_ref[...]
        m_ = ADAM_B1 * m_ref[...] + (1.0 - ADAM_B1) * g_
        v_ = ADAM_B2 * v_ref[...] + (1.0 - ADAM_B2) * (g_ * g_)
        m_hat = m_ / (1.0 - ADAM_B1 ** ADAM_STEP)
        v_hat = v_ / (1.0 - ADAM_B2 ** ADAM_STEP)
        d_ref[...] = -ADAM_LR * (m_hat / (jnp.sqrt(v_hat) + ADAM_EPS) + ADAM_WD * w_ref[...])
        nm_ref[...] = m_
        nv_ref[...] = v_

    spec = pl.BlockSpec((tr, cols), lambda i: (i, 0))
    shape = jax.ShapeDtypeStruct((rows, cols), F32)
    return pl.pallas_call(
        body, name=name, grid=(rows // tr,),
        in_specs=[spec] * 4, out_specs=[spec] * 3, out_shape=[shape] * 3,
        compiler_params=_params("parallel"),
    )(w, g, m, v)


def _adamw_nd(w, g, m, v, name):
    shape = w.shape
    two_d = (-1, shape[-1]) if w.ndim > 1 else (1, -1)
    outs = _adamw(w.reshape(two_d), g.reshape(two_d), m.reshape(two_d), v.reshape(two_d), name)
    return tuple(o.reshape(shape) for o in outs)


def _ride_alone(rider, name):
    def body(*refs):
        r_in, _, r_out, _, r_scratch = _split_refs(rider, 0, 0, refs)
        now = pl.program_id(0) == 0
        rider(now, None, None, r_in, r_out, r_scratch)
        rider(None, now, None, r_in, r_out, r_scratch)
        rider(None, None, now, r_in, r_out, r_scratch)

    return pl.pallas_call(
        body, name=name, grid=(1,),
        in_specs=list(rider.in_specs), out_specs=list(rider.out_specs), out_shape=list(rider.out_shape),
        scratch_shapes=list(rider.scratch_shapes),
        compiler_params=_params("arbitrary"),
    )(*rider.inputs)


def _sum_slots(parts, name):
    slots, rows, cols = parts.shape
    tr = _row_tile(rows, 256) if rows % 16 == 0 else rows

    def body(p_ref, o_ref):
        acc = p_ref[0]
        for s in range(1, slots):
            acc = acc + p_ref[s]
        o_ref[...] = acc

    return pl.pallas_call(
        body, name=name, grid=(rows // tr,),
        in_specs=[pl.BlockSpec((slots, tr, cols), lambda i: (0, i, 0))],
        out_specs=pl.BlockSpec((tr, cols), lambda i: (i, 0)),
        out_shape=jax.ShapeDtypeStruct((rows, cols), parts.dtype),
        compiler_params=_params("parallel"),
    )(parts)


def _pad_rows(flat, cols, multiple):
    n = flat.shape[0]
    rows = -(-n // cols)
    rows = -(-rows // multiple) * multiple
    return jnp.pad(flat, (0, rows * cols - n)).reshape(rows, cols)


def _cols_to_chunks(full):
    lead = full.shape[:-1]
    c = full.shape[-1] // N_DEV
    x = full.reshape(-1, N_DEV, c)
    return jnp.transpose(x, (1, 0, 2)).reshape(N_DEV, -1)


def _chunks_to_cols(chunks, lead):
    n = 1
    for s in lead:
        n *= s
    c = chunks.shape[1] // n
    x = chunks.reshape(N_DEV, n, c)
    return jnp.transpose(x, (1, 0, 2)).reshape(tuple(lead) + (N_DEV * c,))


def kernel(x, meta_tokens, norm_mix_g, norm_ffn_g, final_norm_g, sc_w_in, sc_conv_w, sc_w_out, rg_w_in, rg_conv_w, rg_conv_b, rg_w_gate_a, rg_b_gate_a, rg_w_gate_x, rg_b_gate_x, rg_lambda, rg_w_out, ffn_w_up, ffn_conv_w, ffn_w_down, loss_target, m_meta_tokens, m_norm_mix_g, m_norm_ffn_g, m_final_norm_g, m_sc_w_in, m_sc_conv_w, m_sc_w_out, m_rg_w_in, m_rg_conv_w, m_rg_conv_b, m_rg_w_gate_a, m_rg_b_gate_a, m_rg_w_gate_x, m_rg_b_gate_x, m_rg_lambda, m_rg_w_out, m_ffn_w_up, m_ffn_conv_w, m_ffn_w_down, v_meta_tokens, v_norm_mix_g, v_norm_ffn_g, v_final_norm_g, v_sc_w_in, v_sc_conv_w, v_sc_w_out, v_rg_w_in, v_rg_conv_w, v_rg_conv_b, v_rg_w_gate_a, v_rg_b_gate_a, v_rg_w_gate_x, v_rg_b_gate_x, v_rg_lambda, v_rg_w_out, v_ffn_w_up, v_ffn_conv_w, v_ffn_w_down):
    weights = dict(meta_tokens=meta_tokens, norm_mix_g=norm_mix_g, norm_ffn_g=norm_ffn_g, final_norm_g=final_norm_g,
                   sc_w_in=sc_w_in, sc_conv_w=sc_conv_w, sc_w_out=sc_w_out, rg_w_in=rg_w_in, rg_conv_w=rg_conv_w,
                   rg_conv_b=rg_conv_b, rg_w_gate_a=rg_w_gate_a, rg_b_gate_a=rg_b_gate_a, rg_w_gate_x=rg_w_gate_x,
                   rg_b_gate_x=rg_b_gate_x, rg_lambda=rg_lambda, rg_w_out=rg_w_out, ffn_w_up=ffn_w_up,
                   ffn_conv_w=ffn_conv_w, ffn_w_down=ffn_w_down)
    m_in = dict(meta_tokens=m_meta_tokens, norm_mix_g=m_norm_mix_g, norm_ffn_g=m_norm_ffn_g, final_norm_g=m_final_norm_g,
                sc_w_in=m_sc_w_in, sc_conv_w=m_sc_conv_w, sc_w_out=m_sc_w_out, rg_w_in=m_rg_w_in, rg_conv_w=m_rg_conv_w,
                rg_conv_b=m_rg_conv_b, rg_w_gate_a=m_rg_w_gate_a, rg_b_gate_a=m_rg_b_gate_a, rg_w_gate_x=m_rg_w_gate_x,
                rg_b_gate_x=m_rg_b_gate_x, rg_lambda=m_rg_lambda, rg_w_out=m_rg_w_out, ffn_w_up=m_ffn_w_up,
                ffn_conv_w=m_ffn_conv_w, ffn_w_down=m_ffn_w_down)
    v_in = dict(meta_tokens=v_meta_tokens, norm_mix_g=v_norm_mix_g, norm_ffn_g=v_norm_ffn_g, final_norm_g=v_final_norm_g,
                sc_w_in=v_sc_w_in, sc_conv_w=v_sc_conv_w, sc_w_out=v_sc_w_out, rg_w_in=v_rg_w_in, rg_conv_w=v_rg_conv_w,
                rg_conv_b=v_rg_conv_b, rg_w_gate_a=v_rg_w_gate_a, rg_b_gate_a=v_rg_b_gate_a, rg_w_gate_x=v_rg_w_gate_x,
                rg_b_gate_x=v_rg_b_gate_x, rg_lambda=v_rg_lambda, rg_w_out=v_rg_w_out, ffn_w_up=v_ffn_w_up,
                ffn_conv_w=v_ffn_conv_w, ffn_w_down=v_ffn_w_down)
    names = list(weights)

    seq, d = x.shape[1], x.shape[2]
    n_meta = meta_tokens.shape[0]
    n_ffn = ffn_w_up.shape[0]

    assert n_ffn == 2
    def shard_rows(w_in, w_out):
        return [w_in.T.astype(MXU_DTYPE), w_out.astype(MXU_DTYPE)]

    def both_orientations(in_t_gathered, out_gathered):
        w_in_t = in_t_gathered.reshape(-1, d)
        w_out = out_gathered.reshape(-1, d)
        return w_in_t.T, w_in_t, w_out, w_out.T

    shards = {"ffn0": shard_rows(ffn_w_up[0], ffn_w_down[0]), "rg": shard_rows(rg_w_in[0], rg_w_out[0]),
              "ffn1": shard_rows(ffn_w_up[1], ffn_w_down[1])}
    ffn_w_up_f, ffn_w_up_t, ffn_w_down_f, ffn_w_down_t = [None] * 2, [None] * 2, [None] * 2, [None] * 2

    small_names = ["meta_tokens", "sc_conv_w", "rg_conv_w", "rg_conv_b", "rg_b_gate_a", "rg_b_gate_x", "rg_lambda",
                   "ffn_conv_w"]
    small_lead = {n: weights[n].shape[:-1] for n in small_names}
    small_sizes = [weights[n].size for n in small_names]
    small_flat = jnp.concatenate([weights[n].reshape(-1) for n in small_names])
    small_rows = _pad_rows(small_flat, d, SUBLANES)
    sc_in_g, sc_out_g, small_g = _ride_alone(
        _GatherRider(shard_rows(sc_w_in[0], sc_w_out[0]) + [small_rows]), "gather_first")
    sc_w_in_f, sc_w_in_t, sc_w_out_f, sc_w_out_t = both_orientations(sc_in_g, sc_out_g)
    small_g = small_g.reshape(N_DEV, -1)
    small_full = {}
    o = 0
    for n, sz in zip(small_names, small_sizes):
        small_full[n] = _chunks_to_cols(small_g[:, o:o + sz], small_lead[n])
        o += sz

    wg, wgt = _pair_gate_weights(rg_w_gate_a[0].astype(MXU_DTYPE), rg_w_gate_x[0].astype(MXU_DTYPE))
    rg_cw, rg_cb = small_full["rg_conv_w"][0], small_full["rg_conv_b"]
    rg_ba, rg_bx, rg_lam = small_full["rg_b_gate_a"], small_full["rg_b_gate_x"], small_full["rg_lambda"]
    sc_cw = small_full["sc_conv_w"][0]
    ffn_cw = small_full["ffn_conv_w"]

    tp = _row_tile(n_meta + seq, ROW_TILE_PERM)

    def ffn_fwd(h, l, rider):
        n, a, z, *gathered = _ffn_up_fused(h, norm_ffn_g[l:l + 1], ffn_w_up_f[l], ffn_cw[l], tp, f"ffn{l}_up", rider)
        return _matmul_residual(z, ffn_w_down_f[l], h, f"ffn{l}_down"), (n, a, z), gathered

    h0, n0, p0, q0, *ffn0_w = _sc_in_fused(x[0], small_full["meta_tokens"], norm_mix_g[0:1], sc_w_in_f, sc_cw, tp, "sc_in",
                                           _GatherRider(shards["ffn0"]))
    ffn_w_up_f[0], ffn_w_up_t[0], ffn_w_down_f[0], ffn_w_down_t[0] = both_orientations(*ffn0_w)
    h1 = _matmul_residual(q0, sc_w_out_f, h0, "sc_out")
    h2, ffn0_saved, rg_w = ffn_fwd(h1, 0, _GatherRider(shards["rg"]))
    rg_w_in_f, rg_w_in_t, rg_w_out_f, rg_w_out_t = both_orientations(*rg_w)
    n2, p2, y2, hs2, *ffn1_w = _rg_in_fused(h2, norm_mix_g[1:2], rg_w_in_f, rg_cw, rg_cb, wg, rg_ba, rg_bx, rg_lam,
                                            tp, "rg_in", _GatherRider(shards["ffn1"]))
    ffn_w_up_f[1], ffn_w_up_t[1], ffn_w_down_f[1], ffn_w_down_t[1] = both_orientations(*ffn1_w)
    h3 = _matmul_residual(y2, rg_w_out_f, h2, "rg_out")
    ffn1_saved = tuple(_ffn_up_fused(h3, norm_ffn_g[1:2], ffn_w_up_f[1], ffn_cw[1], tp, "ffn1_up"))
    dh4, loss_tile, d_final_g = _down_loss_head(ffn1_saved[2], ffn_w_down_f[1], h3, final_norm_g.reshape(1, d),
                                                loss_target[0], n_meta, tp, "ffn1_down_loss_head")
    loss = lax.psum(loss_tile[0, 0], AXES)

    arrived = {}

    def ffn_bwd(dh_out, h_in, saved, l):
        n, a, z = saved
        da, dhb, d_cw = _ffn_down_bwd_fused(dh_out, ffn_w_down_t[l], a, ffn_cw[l], tp, f"ffn{l}_down_bwd")
        d_w_down = _wgrad(z, dhb, f"ffn{l}_down_wgrad")
        d_w_up_t, arrived[f"ffn_w_down{l}"] = _wgrad(da, n, f"ffn{l}_up_wgrad", _ScatterRider(d_w_down))
        rows = d_w_up_t.shape[0] // N_DEV
        first_rows = (5 * rows // 8) // (2 * SUBLANES) * (2 * SUBLANES)
        dh_in, d_g, arrived[f"ffn_w_up{l}a"] = _dgrad_in_norm(
            da, ffn_w_up_t[l], h_in, norm_ffn_g[l:l + 1], dh_out, f"ffn{l}_up_dgrad",
            _ScatterRider(d_w_up_t, (0, first_rows)))
        return dh_in, d_cw, d_g, _ScatterRider(d_w_up_t, (first_rows, rows - first_rows))

    dh3, d_fcw1, d_fg1, up1_rest = ffn_bwd(dh4, h3, ffn1_saved, 1)

    dp2, dhb3, d_rg_cw, d_rg_cb, d_wg, d_rg_ba, d_rg_bx, d_rg_lam, arrived["ffn_w_up1b"] = _rg_out_bwd_fused(
        dh3, rg_w_out_t, p2, hs2, rg_cw, rg_cb, wg, wgt, rg_ba, rg_bx, rg_lam, tp, "rg_out_bwd", up1_rest)
    d_wa, d_wx = _unpair_gate_grads(d_wg, rg_w_gate_a.shape[2])
    d_rg_w_out = _wgrad(y2, dhb3, "rg_out_wgrad")
    d_rg_w_in_t, arrived["rg_w_out"] = _wgrad(dp2, n2, "rg_in_wgrad", _ScatterRider(d_rg_w_out))
    dh2, d_mg1, arrived["rg_w_in"] = _dgrad_in_norm(
        dp2, rg_w_in_t, h2, norm_mix_g[1:2], dh3, "rg_in_dgrad", _ScatterRider(d_rg_w_in_t))

    dh1, d_fcw0, d_fg0, up0_rest = ffn_bwd(dh2, h1, ffn0_saved, 0)

    dp0, dhb1, d_sc_cw, arrived["ffn_w_up0b"] = _sc_out_bwd_fused(dh1, sc_w_out_t, p0, sc_cw, tp, "sc_out_bwd", up0_rest)
    d_sc_w_out = _wgrad(q0, dhb1, "sc_out_wgrad")
    d_sc_w_in_t, arrived["sc_w_out"] = _wgrad(dp0, n0, "sc_in_wgrad", _ScatterRider(d_sc_w_out))
    d_mg0, d_meta, d_x, arrived["sc_w_in"] = _dgrad_in_to_input(
        dp0, sc_w_in_t, h0, norm_mix_g[0:1], dh1, n_meta, tp, "sc_in_dgrad", _ScatterRider(d_sc_w_in_t))
    grad_x = d_x[None]

    grads = {}
    for n in ("sc_w_in", "rg_w_in"):
        grads[n] = _sum_slots(arrived[n], f"sum_{n}").T[None]
    for n in ("sc_w_out", "rg_w_out"):
        grads[n] = _sum_slots(arrived[n], f"sum_{n}")[None]
    grads["ffn_w_up"] = jnp.stack([jnp.concatenate(
        [_sum_slots(arrived[f"ffn_w_up{l}{part}"], f"sum_ffn_w_up{l}{part}") for part in "ab"], axis=0).T
        for l in range(n_ffn)])
    grads["ffn_w_down"] = jnp.stack([_sum_slots(arrived[f"ffn_w_down{l}"], f"sum_ffn_w_down{l}") for l in range(n_ffn)])

    small_grads = {"meta_tokens": d_meta,"sc_conv_w": d_sc_cw[None], "rg_conv_w": d_rg_cw[None],
                   "rg_conv_b": d_rg_cb, "rg_b_gate_a": d_rg_ba, "rg_b_gate_x": d_rg_bx, "rg_lambda": d_rg_lam,
                   "ffn_conv_w": jnp.stack([d_fcw0, d_fcw1])}
    small_chunks = jnp.concatenate([_cols_to_chunks(small_grads[n]) for n in small_names], axis=1)
    pad = small_rows.shape[0] * d - small_chunks.shape[1]
    small_chunks = jnp.pad(small_chunks, ((0, 0), (0, pad))).reshape(N_DEV, small_rows.shape[0], d)
    rep_names = ["norm_mix_g", "norm_ffn_g", "final_norm_g", "rg_w_gate_a", "rg_w_gate_x"]
    rep_grads = {"norm_mix_g": jnp.concatenate([d_mg0, d_mg1], axis=0),
                 "norm_ffn_g": jnp.concatenate([d_fg0, d_fg1], axis=0),
                 "final_norm_g": d_final_g.reshape(-1), "rg_w_gate_a": d_wa[None], "rg_w_gate_x": d_wx[None]}
    rep_flat = jnp.concatenate([rep_grads[n].reshape(-1) for n in rep_names])
    rep_chunk_rows = -(-rep_flat.shape[0] // (N_DEV * d))
    rep_chunk_rows += -(small_rows.shape[0] + rep_chunk_rows) % 16
    rep_chunks = jnp.pad(rep_flat, (0, N_DEV * rep_chunk_rows * d - rep_flat.shape[0])).reshape(N_DEV, rep_chunk_rows, d)

    last_chunks = jnp.concatenate([small_chunks, rep_chunks], axis=1)
    reduced = _sum_slots(_ride_alone(_ScatterRider(last_chunks.reshape(-1, d)), "scatter_last")[0], "sum_last")

    small_red = reduced[0:small_rows.shape[0]].reshape(-1)
    o = small_rows.shape[0]
    so = 0
    for n, sz in zip(small_names, small_sizes):
        grads[n] = small_red[so:so + sz].reshape(weights[n].shape)
        so += sz
    rep_red = _ride_alone(_GatherRider([reduced[o:o + rep_chunks.shape[1]]]), "gather_replicated_grads")[0].reshape(-1)
    ro = 0
    for n in rep_names:
        sz = weights[n].size
        grads[n] = rep_red[ro:ro + sz].reshape(weights[n].shape)
        ro += sz

    delta, new_m, new_v = {}, {}, {}
    for n in names:
        delta[n], new_m[n], new_v[n] = _adamw_nd(weights[n], grads[n], m_in[n], v_in[n], f"adamw_{n}")

    return (loss, grad_x, *[grads[n] for n in names], *[delta[n] for n in names],
            *[new_m[n] for n in names], *[new_v[n] for n in names])
```

```python
import jax
import jax.numpy as jnp
from jax import lax
from jax.experimental import pallas as pl
from jax.experimental.pallas import tpu as pltpu

F32 = jnp.float32
MXU_DTYPE = jnp.bfloat16
RMS_EPS = 1e-6
RG_C = 8.0
ADAM_LR = 0.001
ADAM_B1 = 0.9
ADAM_B2 = 0.999
ADAM_EPS = 1e-08
ADAM_WD = 0.01
ADAM_STEP = 10

N_DEV = 8
AXES = ("x", "y", "c")
SUBLANES = 8
LANES = 128
VMEM_LIMIT_BYTES = 48 * 1024 * 1024
ROW_TILE_MATMUL = 700
VMEM_LIMIT_WGRAD_BYTES = 58 * 1024 * 1024
ROW_TILE_WGRAD = 3300
COL_TILE_WGRAD = 1408
ROW_TILE_PERM = 400
STRIP = 256
STRIP_FFN = 4096

_TN = (((0,), (0,)), ((), ()))
_NN = (((1,), (0,)), ((), ()))


def _row_tile(t, target):
    best = None
    for tm in range(16, t + 1, 16):
        if t % tm == 0 and tm <= target:
            best = tm
    return best if best is not None else t


def _col_tile(n, target):
    best = None
    for tn in range(LANES, n + 1, LANES):
        if n % tn == 0 and tn <= target:
            best = tn
    return best if best is not None else n


def _params(*sem):
    return pltpu.CompilerParams(dimension_semantics=sem, vmem_limit_bytes=VMEM_LIMIT_BYTES)


def _dot(a, b, dims):
    return lax.dot_general(a, b, dims, preferred_element_type=F32)


def _sigmoid(x):
    return 1.0 / (1.0 + jnp.exp(-x))


def _sigmoid_tanh(x):
    return 0.5 * jnp.tanh(0.5 * x) + 0.5


def _gelu(x):
    c = 0.7978845608028654
    t = jnp.tanh(c * (x + 0.044715 * (x * x * x)))
    return 0.5 * x * (1.0 + t)


def _gelu_and_grad(x):
    c = 0.7978845608028654
    x2 = x * x
    t = jnp.tanh(c * (x + 0.044715 * (x2 * x)))
    half = 0.5 * (1.0 + t)
    return x * half, half + 0.5 * x * (1.0 - t * t) * c * (1.0 + 3.0 * 0.044715 * x2)


def _softplus(x):
    return jnp.maximum(x, 0.0) + jnp.log1p(jnp.exp(-jnp.abs(x)))


def _time_scratch(tm, d):
    return [pltpu.VMEM((2, tm, d), F32), pltpu.VMEM((d // LANES, tm, LANES), F32), pltpu.SemaphoreType.DMA((2,))]


def _fetch_time_tile(i, nt, src_ref, head, tbuf, sems, tm):
    n_head = head.shape[0]

    def tile(j, slot):
        start = pl.multiple_of(j * tm - n_head, SUBLANES)
        return pltpu.make_async_copy(src_ref.at[pl.ds(start, tm)], tbuf.at[slot], sems.at[slot])

    first = pltpu.make_async_copy(src_ref.at[pl.ds(0, tm - n_head)], tbuf.at[0, pl.ds(n_head, tm - n_head)], sems.at[0])

    @pl.when(i == 0)
    def _():
        tbuf[0, 0:n_head, :] = head
        first.start()

    @pl.when(i + 1 < nt)
    def _():
        tile(i + 1, (i + 1) % 2).start()

    @pl.when(i == 0)
    def _():
        first.wait()

    @pl.when(i > 0)
    def _():
        tile(i, i % 2).wait()


def _time_to_tile_order(t_ref, slabs, out_ref):
    tm, d = out_ref.shape
    groups = tm // SUBLANES
    for k in range(d // LANES):
        slabs[k] = t_ref[:, k * LANES:(k + 1) * LANES]
    for k in range(d // LANES):
        for g in range(groups):
            out_ref[g * SUBLANES:(g + 1) * SUBLANES, k * LANES:(k + 1) * LANES] = slabs[k, pl.ds(g, SUBLANES, stride=groups), :]


def _tile_to_time_order(p_ref, slabs, t_ref):
    tm, d = p_ref.shape
    groups = tm // SUBLANES
    for k in range(d // LANES):
        for g in range(groups):
            slabs[k, pl.ds(g, SUBLANES, stride=groups), :] = p_ref[g * SUBLANES:(g + 1) * SUBLANES, k * LANES:(k + 1) * LANES]
    for k in range(d // LANES):
        t_ref[:, k * LANES:(k + 1) * LANES] = slabs[k]


def _rows(ref, sl):
    return [ref[k:k + 1, sl] for k in range(ref.shape[0])]


def _shift_down(x, before, s):
    if s == 0:
        return x
    n = x.shape[0]
    row = lax.broadcasted_iota(jnp.int32, (SUBLANES, x.shape[1]), 0)
    heads = []
    for g in range(s):
        v = x[n - (s - g) * SUBLANES:n - (s - g - 1) * SUBLANES]
        heads.append(pltpu.roll(jnp.where(row == SUBLANES - 1, before[s - g - 1], v), 1, axis=0))
    return jnp.concatenate(heads + [x[0:n - s * SUBLANES]], axis=0)


def _shift_up(x, after, s):
    if s == 0:
        return x
    row = lax.broadcasted_iota(jnp.int32, (SUBLANES, x.shape[1]), 0)
    tails = []
    for m in range(s):
        v = x[m * SUBLANES:(m + 1) * SUBLANES]
        tails.append(pltpu.roll(jnp.where(row == 0, after[m], v), SUBLANES - 1, axis=0))
    return jnp.concatenate([x[s * SUBLANES:]] + tails, axis=0)


def _weighted(w, windows):
    y = w[0] * windows[0]
    for k in range(1, len(w)):
        y = y + w[k] * windows[k]
    return y


def _windows(x, before, k_width):
    return [_shift_down(x, before, k_width - 1 - k) for k in range(k_width)]


def _conv_t(dy, after, w):
    k_width = len(w)
    return _weighted(w, [_shift_up(dy, after, k_width - 1 - k) for k in range(k_width)])


def _blocks(ref, sl, count, newest_first):
    n = ref.shape[0] // SUBLANES
    order = range(n - 1, n - 1 - count, -1) if newest_first else range(count)
    return [ref[b * SUBLANES:(b + 1) * SUBLANES, sl] for b in order]


def _halo_specs(tm, cols, count, tile_of):
    def spec(k):
        return pl.BlockSpec((SUBLANES, cols), lambda i: (jnp.maximum(tile_of(i) * (tm // SUBLANES) - k, 0), 0))
    return [spec(k) for k in range(1, count + 1)]


def _scan_tile(coef, val, out, carry, groups, reverse):
    cols = coef.shape[1]
    row = lax.broadcasted_iota(jnp.int32, (SUBLANES, cols), 0)

    def blk(i):
        g = groups - 1 - i if reverse else i
        return pl.ds(pl.multiple_of(g * SUBLANES, SUBLANES), SUBLANES)

    def local(i, pl_):
        p_prev, l_prev = pl_
        a = coef[blk(i), :]
        p = a * p_prev
        l = a * l_prev + val[blk(i), :]
        coef[blk(i), :] = p
        val[blk(i), :] = l
        return p, l

    pf, lf = lax.fori_loop(0, groups, local, (jnp.ones((SUBLANES, cols), F32), jnp.zeros((SUBLANES, cols), F32)))
    for s in (1, 2, 4):
        keep, sh = (row < SUBLANES - s, SUBLANES - s) if reverse else (row >= s, s)
        p_s = jnp.where(keep, pltpu.roll(pf, sh, axis=0), 1.0)
        l_s = jnp.where(keep, pltpu.roll(lf, sh, axis=0), 0.0)
        lf = pf * l_s + lf
        pf = pf * p_s
    end = lf + pf * carry
    if reverse:
        init = jnp.where(row == SUBLANES - 1, carry, pltpu.roll(end, SUBLANES - 1, axis=0))
        leaving = jnp.broadcast_to(end[0:1, :], (SUBLANES, cols))
    else:
        init = jnp.where(row == 0, carry, pltpu.roll(end, 1, axis=0))
        leaving = jnp.broadcast_to(end[SUBLANES - 1:SUBLANES, :], (SUBLANES, cols))

    def fix(i, _):
        out[blk(i), :] = val[blk(i), :] + coef[blk(i), :] * init
        return 0

    lax.fori_loop(0, groups, fix, 0)
    return leaving


def _resident(shape):
    return pl.BlockSpec(shape, lambda *_: (0,) * len(shape), pipeline_mode=pl.Buffered(1))


def _full(shape):
    return pl.BlockSpec(shape, lambda *_: (0,) * len(shape))


def _rmsnorm_to(h_ref, g_ref, n_ref):
    x = h_ref[...]
    ms = jnp.mean(x * x, axis=-1, keepdims=True)
    n_ref[...] = (x * lax.rsqrt(ms + RMS_EPS) * g_ref[...]).astype(n_ref.dtype)


_ANY = pl.BlockSpec(memory_space=pl.ANY)
_MESH = pl.DeviceIdType.MESH


def _dma_sems(n):
    return pltpu.SemaphoreType.DMA((n,))


def _when_each(*phases):
    for cond, fn in phases:
        if cond is not None:
            pl.when(cond)(fn)


class _NoRider:
    inputs = in_specs = out_shape = out_specs = scratch_shapes = ()

    def __call__(self, first, middle, last, ins, outs, scratch):
        pass


_NO_RIDER = _NoRider()


def _hand_on_step(nt):
    return (3 * nt) // 4


def _split_refs(rider, n_out, n_scratch, rest):
    a = len(rider.inputs)
    b = a + n_out
    c = b + len(rider.out_shape)
    e = c + n_scratch
    return rest[:a], rest[a:b], rest[b:c], rest[c:e], rest[e:]


class _GatherRider:
    def __init__(self, blocks):
        n = len(blocks)
        self.inputs = tuple(blocks)
        self.in_specs = (_ANY,) * n
        self.out_shape = tuple(jax.ShapeDtypeStruct((N_DEV,) + b.shape, b.dtype) for b in blocks)
        self.out_specs = (_ANY,) * n
        self.scratch_shapes = (_dma_sems(7 * n), _dma_sems(7 * n), _dma_sems(n))

    def __call__(self, first, middle, last, ins, outs, scratch):
        n = len(ins)
        send_sems, recv_sems, local_sems = scratch
        x, y, c = lax.axis_index("x"), lax.axis_index("y"), lax.axis_index("c")
        me, sibling = (x, y, c), (x, y, 1 - c)
        chips = [(1 - x, y), (x, 1 - y), (1 - x, 1 - y)]

        def slot(b, px, py, pc):
            return outs[b].at[4 * px + 2 * py + pc]

        def copy(k, b, block, to, src=None):
            return pltpu.make_async_remote_copy(
                src_ref=slot(b, *block) if src is None else src, dst_ref=slot(b, *block),
                send_sem=send_sems.at[k * n + b], recv_sem=recv_sems.at[k * n + b], device_id=to, device_id_type=_MESH)

        mine = [pltpu.make_async_copy(ins[b], slot(b, *me), local_sems.at[b]) for b in range(n)]
        own = [copy(0, b, me, sibling, src=ins[b]) for b in range(n)]
        own += [copy(1 + j, b, me, (*chip, c), src=ins[b]) for j, chip in enumerate(chips) for b in range(n)]
        passed = [[copy(4 + j, b, (*chip, c), sibling) for b in range(n)] for j, chip in enumerate(chips)]

        def at_first():
            for cp in mine + own:
                cp.start()

        def at_middle():
            for j, chip in enumerate(chips):
                for b in range(n):
                    copy(1 + j, b, (*chip, c), me).wait_recv()
                    passed[j][b].start()

        def at_last():
            for b in range(n):
                copy(0, b, sibling, me).wait_recv()
                for j, chip in enumerate(chips):
                    copy(4 + j, b, (*chip, 1 - c), me).wait_recv()
            for cp in own + [cp for group in passed for cp in group]:
                cp.wait_send()
            for cp in mine:
                cp.wait()

        _when_each((first, at_first), (middle, at_middle), (last, at_last))


class _ScatterRider:
    def __init__(self, grad, rows=None):
        r = grad.shape[0] // N_DEV
        chunks = grad.reshape(N_DEV, r, grad.shape[1])
        self.rows = rows if rows is not None else (0, r)
        self.inputs = (chunks,)
        self.in_specs = (_ANY,)
        self.out_shape = (jax.ShapeDtypeStruct((N_DEV, self.rows[1], grad.shape[1]), chunks.dtype),)
        self.out_specs = (_ANY,)
        self.scratch_shapes = (_dma_sems(N_DEV - 1), _dma_sems(N_DEV - 1), pltpu.SemaphoreType.DMA(()))

    def __call__(self, first, middle, last, ins, outs, scratch):
        (g_ref,), (r_ref,) = ins, outs
        send_sems, recv_sems, local_sem = scratch
        x, y, c = lax.axis_index("x"), lax.axis_index("y"), lax.axis_index("c")
        me = 4 * x + 2 * y + c
        part = pl.ds(*self.rows)
        mine = pltpu.make_async_copy(g_ref.at[me, part], r_ref.at[me], local_sem)
        copies = []
        for k in range(1, N_DEV):
            px, py, pc = (1 - x if k & 4 else x), (1 - y if k & 2 else y), (1 - c if k & 1 else c)
            copies.append(pltpu.make_async_remote_copy(
                src_ref=g_ref.at[4 * px + 2 * py + pc, part], dst_ref=r_ref.at[me],
                send_sem=send_sems.at[k - 1], recv_sem=recv_sems.at[k - 1],
                device_id=(px, py, pc), device_id_type=_MESH))

        def at_first():
            mine.start()
            for cp in copies:
                cp.start()

        def at_last():
            for cp in copies:
                cp.wait()
            mine.wait()

        _when_each((first, at_first), (last, at_last))


def _matmul_residual(q, w, h, name):
    t, k = q.shape
    d = w.shape[1]
    tm = _row_tile(t, ROW_TILE_MATMUL)

    def body(q_ref, w_ref, h_ref, o_ref):
        o_ref[...] = h_ref[...] + _dot(q_ref[...], w_ref[...], _NN)

    return pl.pallas_call(
        body, name=name, grid=(t // tm,),
        in_specs=[pl.BlockSpec((tm, k), lambda i: (i, 0)),
                  _resident((k, d)),
                  pl.BlockSpec((tm, d), lambda i: (i, 0))],
        out_specs=pl.BlockSpec((tm, d), lambda i: (i, 0)),
        out_shape=jax.ShapeDtypeStruct((t, d), F32),
        compiler_params=_params("parallel"),
    )(q, w, h)


def _dgrad_in_norm(dp, wt, h, g, dh_next, name, rider=_NO_RIDER):
    t, n = dp.shape
    d = wt.shape[1]
    tm = _row_tile(t, ROW_TILE_MATMUL)
    nt = t // tm

    def body(dp_ref, wt_ref, h_ref, g_ref, dhn_ref, *rest):
        r_in, (dh_ref, dg_ref), r_out, _, r_scratch = _split_refs(rider, 2, 0, rest)
        i = pl.program_id(0)
        rider(i == 0, i == nt // 2, None, r_in, r_out, r_scratch)
        dn = _dot(dp_ref[...], wt_ref[...], _NN)
        x = h_ref[...]
        ms = jnp.mean(x * x, axis=-1, keepdims=True)
        r = lax.rsqrt(ms + RMS_EPS)
        xhat = x * r
        dng = dn * g_ref[...]
        c = jnp.mean(dng * xhat, axis=-1, keepdims=True)
        dh_ref[...] = dhn_ref[...] + r * (dng - xhat * c)
        part = jnp.sum(dn * xhat, axis=0, keepdims=True)

        @pl.when(i == 0)
        def _():
            dg_ref[...] = part

        @pl.when(i > 0)
        def _():
            dg_ref[...] += part

        rider(None, None, i == nt - 1, r_in, r_out, r_scratch)

    return pl.pallas_call(
        body, name=name, grid=(nt,),
        in_specs=[pl.BlockSpec((tm, n), lambda i: (i, 0)),
                  _resident((n, d)),
                  pl.BlockSpec((tm, d), lambda i: (i, 0)),
                  pl.BlockSpec((1, d), lambda i: (0, 0)),
                  pl.BlockSpec((tm, d), lambda i: (i, 0))] + list(rider.in_specs),
        out_specs=[pl.BlockSpec((tm, d), lambda i: (i, 0)),
                   pl.BlockSpec((1, d), lambda i: (0, 0))] + list(rider.out_specs),
        out_shape=[jax.ShapeDtypeStruct((t, d), F32), jax.ShapeDtypeStruct((1, d), F32)] + list(rider.out_shape),
        scratch_shapes=list(rider.scratch_shapes),
        compiler_params=_params("arbitrary"),
    )(dp, wt, h, g, dh_next, *rider.inputs)


def _dgrad_in_to_input(dp, wt, h, g, dh_next, n_head, tm, name, rider=_NO_RIDER):
    t, n = dp.shape
    d = wt.shape[1]
    nt = t // tm

    def body(dp_ref, wt_ref, h_ref, g_ref, dhn_ref, *rest):
        r_in, (dg_ref, head_ref, rest_ref), r_out, (pbuf, tout, slabs, sems), r_scratch = _split_refs(rider, 3, 4, rest)
        i = pl.program_id(0)
        rider(i == 0, i == nt // 2, None, r_in, r_out, r_scratch)
        dn = _dot(dp_ref[...], wt_ref[...], _NN)
        x = h_ref[...]
        ms = jnp.mean(x * x, axis=-1, keepdims=True)
        r = lax.rsqrt(ms + RMS_EPS)
        xhat = x * r
        dng = dn * g_ref[...]
        c = jnp.mean(dng * xhat, axis=-1, keepdims=True)
        pbuf[...] = dhn_ref[...] + r * (dng - xhat * c)
        part = jnp.sum(dn * xhat, axis=0, keepdims=True)

        @pl.when(i == 0)
        def _():
            dg_ref[...] = part

        @pl.when(i > 0)
        def _():
            dg_ref[...] += part

        def store(j):
            if isinstance(j, int) and j == 0:
                return pltpu.make_async_copy(tout.at[0, pl.ds(n_head, tm - n_head)],
                                             rest_ref.at[pl.ds(0, tm - n_head)], sems.at[0])
            start = j * tm - n_head
            start = start if isinstance(start, int) else pl.multiple_of(start, SUBLANES)
            return pltpu.make_async_copy(tout.at[j % 2], rest_ref.at[pl.ds(start, tm)], sems.at[j % 2])

        @pl.when(i == 2)
        def _():
            store(0).wait()

        @pl.when(i > 2)
        def _():
            store(i - 2).wait()

        _tile_to_time_order(pbuf, slabs, tout.at[i % 2])

        @pl.when(i == 0)
        def _():
            head_ref[...] = tout[0, 0:n_head, :]
            store(0).start()

        @pl.when(i > 0)
        def _():
            store(i).start()

        @pl.when(i == nt - 1)
        def _():
            for j in (nt - 2, nt - 1):
                if j >= 0:
                    store(j).wait()

        rider(None, None, i == nt - 1, r_in, r_out, r_scratch)

    return pl.pallas_call(
        body, name=name, grid=(nt,),
        in_specs=[pl.BlockSpec((tm, n), lambda i: (i, 0)),
                  _resident((n, d)),
                  pl.BlockSpec((tm, d), lambda i: (i, 0)),
                  pl.BlockSpec((1, d), lambda i: (0, 0)),
                  pl.BlockSpec((tm, d), lambda i: (i, 0))] + list(rider.in_specs),
        out_specs=[pl.BlockSpec((1, d), lambda i: (0, 0)), _full((n_head, d)), _ANY] + list(rider.out_specs),
        out_shape=[jax.ShapeDtypeStruct((1, d), F32), jax.ShapeDtypeStruct((n_head, d), F32),
                   jax.ShapeDtypeStruct((t - n_head, d), F32)] + list(rider.out_shape),
        scratch_shapes=[pltpu.VMEM((tm, d), F32)] + _time_scratch(tm, d) + list(rider.scratch_shapes),
        compiler_params=_params("arbitrary"),
    )(dp, wt, h, g, dh_next, *rider.inputs)


def _wgrad(a, b, name, rider=_NO_RIDER):
    t, m = a.shape
    d = b.shape[1]
    tmm, tk = _col_tile(m, COL_TILE_WGRAD), _row_tile(t, ROW_TILE_WGRAD)
    nm, nk = m // tmm, t // tk

    def body(a_ref, b_ref, *rest):
        r_in, (o_ref,), r_out, _, r_scratch = _split_refs(rider, 1, 0, rest)
        i, k = pl.program_id(0), pl.program_id(1)
        rider(jnp.logical_and(i == 0, k == 0), None, None, r_in, r_out, r_scratch)

        @pl.when(k == 0)
        def _():
            o_ref[...] = jnp.zeros_like(o_ref)

        o_ref[...] += _dot(a_ref[...], b_ref[...], _TN)
        rider(None, None, jnp.logical_and(i == nm - 1, k == nk - 1), r_in, r_out, r_scratch)

    outs = pl.pallas_call(
        body, name=name, grid=(nm, nk),
        in_specs=[pl.BlockSpec((tk, tmm), lambda i, k: (k, i)),
                  pl.BlockSpec((tk, d), lambda i, k: (k, 0))] + list(rider.in_specs),
        out_specs=[pl.BlockSpec((tmm, d), lambda i, k: (i, 0))] + list(rider.out_specs),
        out_shape=[jax.ShapeDtypeStruct((m, d), F32)] + list(rider.out_shape),
        scratch_shapes=list(rider.scratch_shapes),
        compiler_params=pltpu.CompilerParams(dimension_semantics=("arbitrary", "arbitrary"),
                                             vmem_limit_bytes=VMEM_LIMIT_WGRAD_BYTES),
    )(a, b, *rider.inputs)
    return outs if rider.out_shape else outs[0]


def _ffn_up_fused(h, g, w, conv_w, tm, name, rider=_NO_RIDER):
    t, d = h.shape
    f2 = w.shape[1]
    f = f2 // 2
    cw = min(STRIP_FFN, f)
    kw = conv_w.shape[0]
    nh = kw - 1
    nt = t // tm

    def body(h_ref, g_ref, w_ref, cw_ref, *rest):
        r_in, (n_ref, a_ref, z_ref), r_out, (carry,), r_scratch = _split_refs(rider, 3, 1, rest)
        i = pl.program_id(0)
        rider(i == 0, i == _hand_on_step(nt), None, r_in, r_out, r_scratch)

        @pl.when(i == 0)
        def _():
            carry[...] = jnp.zeros_like(carry)

        _rmsnorm_to(h_ref, g_ref, n_ref)
        for c in range(0, f, cw):
            conv = []
            for sl in (slice(c, c + cw), slice(f + c, f + c + cw)):
                a = _dot(n_ref[...], w_ref[:, sl], _NN)
                a_ref[:, sl] = a
                conv.append(_weighted(_rows(cw_ref, sl), _windows(a, _blocks(carry, sl, nh, True), kw)))
                carry[:, sl] = a[tm - nh * SUBLANES:tm]
            gg, vv = conv
            z_ref[:, c:c + cw] = (gg * _sigmoid(gg) * vv).astype(z_ref.dtype)
        rider(None, None, i == nt - 1, r_in, r_out, r_scratch)

    return pl.pallas_call(
        body, name=name, grid=(nt,),
        in_specs=[pl.BlockSpec((tm, d), lambda i: (i, 0)), pl.BlockSpec((1, d), lambda i: (0, 0)),
                  _resident((d, f2)), _full(conv_w.shape)] + list(rider.in_specs),
        out_specs=[pl.BlockSpec((tm, d), lambda i: (i, 0)), pl.BlockSpec((tm, f2), lambda i: (i, 0)),
                   pl.BlockSpec((tm, f), lambda i: (i, 0))] + list(rider.out_specs),
        out_shape=[jax.ShapeDtypeStruct((t, d), MXU_DTYPE), jax.ShapeDtypeStruct((t, f2), F32),
                   jax.ShapeDtypeStruct((t, f), MXU_DTYPE)] + list(rider.out_shape),
        scratch_shapes=[pltpu.VMEM((nh * SUBLANES, f2), F32)] + list(rider.scratch_shapes),
        compiler_params=_params("arbitrary"),
    )(h, g, w, conv_w, *rider.inputs)


def _ffn_down_bwd_fused(dh, wdt, a, conv_w, tm, name):
    t, d = dh.shape
    f = wdt.shape[1]
    f2 = 2 * f
    nt = t // tm
    cw = min(STRIP_FFN, f)
    kw = conv_w.shape[0]
    nh = kw - 1
    rev = lambda i: nt - 1 - i

    def body(dh_ref, wdt_ref, a_ref, *rest):
        halo_refs = rest[:nh]
        cw_ref, da_ref, dhb_ref, dw_ref, carry = rest[nh:]
        i = pl.program_id(0)
        newest, oldest = i == 0, i == nt - 1

        @pl.when(newest)
        def _():
            carry[...] = jnp.zeros_like(carry)
            dw_ref[...] = jnp.zeros_like(dw_ref)

        dhb_ref[...] = dh_ref[...].astype(dhb_ref.dtype)
        for c in range(0, f, cw):
            gsl, vsl = slice(c, c + cw), slice(f + c, f + c + cw)
            dz = _dot(dhb_ref[...], wdt_ref[:, gsl], _NN)
            win, conv = {}, {}
            for sl in (gsl, vsl):
                before = [jnp.where(oldest, 0.0, r[:, sl]) for r in halo_refs]
                win[sl.start] = _windows(a_ref[:, sl], before, kw)
                conv[sl.start] = _weighted(_rows(cw_ref, sl), win[sl.start])
            gg, vv = conv[gsl.start], conv[vsl.start]
            s = _sigmoid(gg)
            grads = {gsl.start: dz * vv * s * (1.0 + gg * (1.0 - s)), vsl.start: dz * gg * s}
            for sl in (gsl, vsl):
                dconv = grads[sl.start]
                da_ref[:, sl] = _conv_t(dconv, _blocks(carry, sl, nh, False), _rows(cw_ref, sl)).astype(da_ref.dtype)
                for k in range(kw):
                    dw_ref[k:k + 1, sl] += jnp.sum(dconv * win[sl.start][k], axis=0, keepdims=True)
                carry[:, sl] = dconv[0:nh * SUBLANES]

    return pl.pallas_call(
        body, name=name, grid=(nt,),
        in_specs=[pl.BlockSpec((tm, d), lambda i: (rev(i), 0)), _resident((d, f)),
                  pl.BlockSpec((tm, f2), lambda i: (rev(i), 0))] + _halo_specs(tm, f2, nh, rev) + [_full(conv_w.shape)],
        out_specs=[pl.BlockSpec((tm, f2), lambda i: (rev(i), 0)), pl.BlockSpec((tm, d), lambda i: (rev(i), 0)),
                   _full(conv_w.shape)],
        out_shape=[jax.ShapeDtypeStruct((t, f2), MXU_DTYPE), jax.ShapeDtypeStruct((t, d), MXU_DTYPE),
                   jax.ShapeDtypeStruct(conv_w.shape, F32)],
        scratch_shapes=[pltpu.VMEM((nh * SUBLANES, f2), F32)],
        compiler_params=_params("arbitrary"),
    )(dh, wdt, a, *([a] * nh), conv_w)


def _sc_in_fused(x, meta, g, w, conv_w, tm, name, rider=_NO_RIDER):
    d = x.shape[1]
    t = x.shape[0] + meta.shape[0]
    cw = min(STRIP, d)
    kw = conv_w.shape[0]
    nh = kw - 1
    nt = t // tm

    def body(x_ref, meta_ref, g_ref, w_ref, cw_ref, *rest):
        r_in, (h_ref, n_ref, p_ref, q_ref), r_out, (carry, tbuf, slabs, sems), r_scratch = _split_refs(rider, 4, 4, rest)
        i = pl.program_id(0)
        rider(i == 0, i == _hand_on_step(nt), None, r_in, r_out, r_scratch)

        @pl.when(i == 0)
        def _():
            carry[...] = jnp.zeros_like(carry)

        _fetch_time_tile(i, nt, x_ref, meta_ref[...], tbuf, sems, tm)
        _time_to_tile_order(tbuf.at[i % 2], slabs, h_ref)
        _rmsnorm_to(h_ref, g_ref, n_ref)
        for c in range(0, d, cw):
            sl = slice(c, c + cw)
            parts = []
            for base in (0, d, 2 * d):
                psl = slice(base + c, base + c + cw)
                parts.append(_dot(n_ref[...], w_ref[:, psl], _NN))
                p_ref[:, psl] = parts[-1]
            bg, cg, v = parts
            cv = cg * v
            u = _weighted(_rows(cw_ref, sl), _windows(cv, _blocks(carry, sl, nh, True), kw))
            carry[:, sl] = cv[tm - nh * SUBLANES:tm]
            q_ref[:, sl] = (bg * u).astype(q_ref.dtype)
        rider(None, None, i == nt - 1, r_in, r_out, r_scratch)

    row_tile = lambda cols: pl.BlockSpec((tm, cols), lambda i: (i, 0))
    return pl.pallas_call(
        body, name=name, grid=(nt,),
        in_specs=[_ANY, _full(meta.shape), pl.BlockSpec((1, d), lambda i: (0, 0)),
                  _resident((d, 3 * d)), _full(conv_w.shape)] + list(rider.in_specs),
        out_specs=[row_tile(d), row_tile(d), row_tile(3 * d), row_tile(d)] + list(rider.out_specs),
        out_shape=[jax.ShapeDtypeStruct((t, d), F32), jax.ShapeDtypeStruct((t, d), MXU_DTYPE),
                   jax.ShapeDtypeStruct((t, 3 * d), F32), jax.ShapeDtypeStruct((t, d), MXU_DTYPE)]
                  + list(rider.out_shape),
        scratch_shapes=[pltpu.VMEM((nh * SUBLANES, d), F32)] + _time_scratch(tm, d) + list(rider.scratch_shapes),
        compiler_params=_params("arbitrary"),
    )(x, meta, g, w, conv_w, *rider.inputs)


def _sc_out_bwd_fused(dh, wot, p, conv_w, tm, name, rider=_NO_RIDER):
    t, d = dh.shape
    nt = t // tm
    cw = min(STRIP, d)
    kw = conv_w.shape[0]
    nh = kw - 1
    rev = lambda i: nt - 1 - i

    def body(dh_ref, wot_ref, p_ref, *rest):
        halo_refs, cw_ref = rest[:nh], rest[nh]
        r_in, (dp_ref, dhb_ref, dw_ref), r_out, (carry,), r_scratch = _split_refs(rider, 3, 1, rest[nh + 1:])
        i = pl.program_id(0)
        newest, oldest = i == 0, i == nt - 1
        rider(newest, i == nt // 2, None, r_in, r_out, r_scratch)

        @pl.when(newest)
        def _():
            carry[...] = jnp.zeros_like(carry)
            dw_ref[...] = jnp.zeros_like(dw_ref)

        dhb_ref[...] = dh_ref[...].astype(dhb_ref.dtype)
        for c in range(0, d, cw):
            sl, csl, vsl = slice(c, c + cw), slice(d + c, d + c + cw), slice(2 * d + c, 2 * d + c + cw)
            w = _rows(cw_ref, sl)
            dq = _dot(dhb_ref[...], wot_ref[:, sl], _NN)
            bg, cg, v = p_ref[:, sl], p_ref[:, csl], p_ref[:, vsl]
            before = [jnp.where(oldest, 0.0, r[:, csl] * r[:, vsl]) for r in halo_refs]
            win = _windows(cg * v, before, kw)
            du = dq * bg
            dcv = _conv_t(du, _blocks(carry, sl, nh, False), w)
            dp_ref[:, sl] = (dq * _weighted(w, win)).astype(dp_ref.dtype)
            dp_ref[:, csl] = (dcv * v).astype(dp_ref.dtype)
            dp_ref[:, vsl] = (dcv * cg).astype(dp_ref.dtype)
            for k in range(kw):
                dw_ref[k:k + 1, sl] += jnp.sum(du * win[k], axis=0, keepdims=True)
            carry[:, sl] = du[0:nh * SUBLANES]
        rider(None, None, oldest, r_in, r_out, r_scratch)

    tile = lambda cols: pl.BlockSpec((tm, cols), lambda i: (rev(i), 0))
    return pl.pallas_call(
        body, name=name, grid=(nt,),
        in_specs=[tile(d), _resident((d, d)), tile(3 * d)] + _halo_specs(tm, 3 * d, nh, rev) + [_full(conv_w.shape)]
                 + list(rider.in_specs),
        out_specs=[tile(3 * d), tile(d), _full(conv_w.shape)] + list(rider.out_specs),
        out_shape=[jax.ShapeDtypeStruct((t, 3 * d), MXU_DTYPE), jax.ShapeDtypeStruct((t, d), MXU_DTYPE),
                   jax.ShapeDtypeStruct(conv_w.shape, F32)] + list(rider.out_shape),
        scratch_shapes=[pltpu.VMEM((nh * SUBLANES, d), F32)] + list(rider.scratch_shapes),
        compiler_params=_params("arbitrary"),
    )(dh, wot, p, *([p] * nh), conv_w, *rider.inputs)


def _pair_gate_weights(wa, wx):
    nb, bd, _ = wa.shape
    zero = jnp.zeros((nb // 2, bd, bd), wa.dtype)

    def pair(w):
        w = w.reshape(nb // 2, 2, bd, bd)
        top = jnp.concatenate([w[:, 0], zero], axis=2)
        bottom = jnp.concatenate([zero, w[:, 1]], axis=2)
        return jnp.concatenate([top, bottom], axis=1)

    both = jnp.concatenate([pair(wa), pair(wx)], axis=2)
    return both, jnp.swapaxes(both, 1, 2)


def _unpair_gate_grads(dw, bd):
    def blocks(cols):
        d0 = dw[:, :bd, cols:cols + bd]
        d1 = dw[:, bd:, cols + bd:cols + 2 * bd]
        return jnp.stack([d0, d1], axis=1).reshape(-1, bd, bd)
    return blocks(0), blocks(2 * bd)


def _rg_gates(u, k, wg_ref, ba, bx, lam):
    ub = u.astype(MXU_DTYPE)
    z = _dot(ub, wg_ref[k], _NN)
    half = z.shape[1] // 2
    r = _sigmoid_tanh(z[:, :half] + ba)
    ig = _sigmoid_tanh(z[:, half:] + bx)
    sp = _softplus(-lam)
    la = -RG_C * r * sp
    a = jnp.exp(la)
    th = jnp.tanh(la)
    mult = jnp.sqrt(-2.0 * th / (1.0 - th))
    return r, ig, a, mult


def _rg_in_fused(h, g, w, conv_w, conv_b, wg, ba, bx, lam, tm, name, rider=_NO_RIDER):
    t, d = h.shape
    r2 = w.shape[1]
    rr = r2 // 2
    nb, bd = wg.shape[0], wg.shape[1]
    kw = conv_w.shape[0]
    nh = kw - 1
    groups = tm // SUBLANES
    nt = t // tm

    def body(h_ref, g_ref, w_ref, cw_ref, cb_ref, wg_ref, ba_ref, bx_ref, lam_ref, *rest):
        r_in, (n_ref, p_ref, y_ref, hs_ref), r_out, (a_scr, b_scr, carry_rb, carry_h), r_scratch = _split_refs(
            rider, 4, 4, rest)
        i = pl.program_id(0)
        rider(i == 0, i == _hand_on_step(nt), None, r_in, r_out, r_scratch)

        @pl.when(i == 0)
        def _():
            carry_rb[...] = jnp.zeros_like(carry_rb)
            carry_h[...] = jnp.zeros_like(carry_h)

        _rmsnorm_to(h_ref, g_ref, n_ref)
        for k in range(nb):
            sl = slice(k * bd, (k + 1) * bd)
            rsl = slice(rr + k * bd, rr + (k + 1) * bd)
            p_ref[:, sl] = _dot(n_ref[...], w_ref[:, sl], _NN)
            rb = _dot(n_ref[...], w_ref[:, rsl], _NN)
            p_ref[:, rsl] = rb
            u = _weighted(_rows(cw_ref, sl), _windows(rb, _blocks(carry_rb, sl, nh, True), kw)) + cb_ref[:, sl]
            carry_rb[:, sl] = rb[tm - nh * SUBLANES:tm]
            _, ig, a, mult = _rg_gates(u, k, wg_ref, ba_ref[:, sl], bx_ref[:, sl], lam_ref[:, sl])
            a_scr[:, sl] = a
            b_scr[:, sl] = mult * (ig * u)

        carry_h[...] = _scan_tile(a_scr, b_scr, hs_ref, carry_h[...], groups, reverse=False)

        for k in range(nb):
            sl = slice(k * bd, (k + 1) * bd)
            y_ref[:, sl] = (hs_ref[:, sl] * _gelu(p_ref[:, sl])).astype(y_ref.dtype)
        rider(None, None, i == nt - 1, r_in, r_out, r_scratch)

    vm = lambda rows: pltpu.VMEM((rows, rr), F32)
    return pl.pallas_call(
        body, name=name, grid=(nt,),
        in_specs=[pl.BlockSpec((tm, d), lambda i: (i, 0)), pl.BlockSpec((1, d), lambda i: (0, 0)),
                  _resident((d, r2)), _full(conv_w.shape), _full(conv_b.shape), _full(wg.shape), _full(ba.shape),
                  _full(bx.shape), _full(lam.shape)] + list(rider.in_specs),
        out_specs=[pl.BlockSpec((tm, d), lambda i: (i, 0)), pl.BlockSpec((tm, r2), lambda i: (i, 0)),
                   pl.BlockSpec((tm, rr), lambda i: (i, 0)), pl.BlockSpec((tm, rr), lambda i: (i, 0))]
                  + list(rider.out_specs),
        out_shape=[jax.ShapeDtypeStruct((t, d), MXU_DTYPE), jax.ShapeDtypeStruct((t, r2), F32),
                   jax.ShapeDtypeStruct((t, rr), MXU_DTYPE), jax.ShapeDtypeStruct((t, rr), F32)]
                  + list(rider.out_shape),
        scratch_shapes=[vm(tm), vm(tm), vm(nh * SUBLANES), vm(SUBLANES)] + list(rider.scratch_shapes),
        compiler_params=_params("arbitrary"),
    )(h, g, w, conv_w, conv_b, wg, ba, bx, lam, *rider.inputs)


def _rg_out_bwd_fused(dh, wot, p, hs, conv_w, conv_b, wg, wgt, ba, bx, lam, tm, name, rider=_NO_RIDER):
    t, d = dh.shape
    r2 = p.shape[1]
    rr = r2 // 2
    nb, bd = wg.shape[0], wg.shape[1]
    nt = t // tm
    kw = conv_w.shape[0]
    nh = kw - 1
    groups = tm // SUBLANES
    rev = lambda i: nt - 1 - i

    def body(dh_ref, wot_ref, p_ref, hs_ref, hsh_ref, *rest):
        halo_refs = rest[:nh]
        cw_ref, cb_ref, wg_ref, wgt_ref, ba_ref, bx_ref, lam_ref = rest[nh:nh + 7]
        (r_in, (dp_ref, dhb_ref, dcw_ref, dcb_ref, dwg_ref, dba_ref, dbx_ref, dlam_ref), r_out,
         (a_scr, as_scr, u_scr, r_scr, ig_scr, mult_scr, g_scr, carry_g, carry_du), r_scratch) = _split_refs(
            rider, 8, 9, rest[nh + 7:])
        i = pl.program_id(0)
        newest, oldest = i == 0, i == nt - 1
        rider(newest, i == nt // 2, None, r_in, r_out, r_scratch)

        @pl.when(newest)
        def _():
            carry_g[...] = jnp.zeros_like(carry_g)
            carry_du[...] = jnp.zeros_like(carry_du)
            for ref in (dcw_ref, dcb_ref, dwg_ref, dba_ref, dbx_ref, dlam_ref):
                ref[...] = jnp.zeros_like(ref)

        dhb_ref[...] = dh_ref[...].astype(dhb_ref.dtype)
        ones = jnp.ones((SUBLANES, bd), F32)

        def conv_windows(sl, rsl):
            before = [jnp.where(oldest, 0.0, r[:, rsl]) for r in halo_refs]
            return _windows(p_ref[:, rsl], before, kw)

        for k in range(nb):
            sl = slice(k * bd, (k + 1) * bd)
            rsl = slice(rr + k * bd, rr + (k + 1) * bd)
            dy = _dot(dhb_ref[...], wot_ref[:, sl], _NN)
            u = _weighted(_rows(cw_ref, sl), conv_windows(sl, rsl)) + cb_ref[:, sl]
            r, ig, a, mult = _rg_gates(u, k, wg_ref, ba_ref[:, sl], bx_ref[:, sl], lam_ref[:, sl])
            gate, dgate = _gelu_and_grad(p_ref[:, sl])
            g_scr[:, sl] = dy * gate
            dp_ref[:, sl] = (dy * hs_ref[:, sl] * dgate).astype(dp_ref.dtype)
            a_scr[:, sl] = a
            as_scr[:, sl] = _shift_up(a, [ones], 1)
            u_scr[:, sl] = u
            r_scr[:, sl] = r
            ig_scr[:, sl] = ig
            mult_scr[:, sl] = mult

        g_first = _scan_tile(as_scr, g_scr, g_scr, carry_g[...], groups, reverse=True)
        carry_g[...] = jnp.broadcast_to(a_scr[0:1, :], (SUBLANES, rr)) * g_first

        for k in range(nb):
            sl = slice(k * bd, (k + 1) * bd)
            rsl = slice(rr + k * bd, rr + (k + 1) * bd)
            cw = _rows(cw_ref, sl)
            lam_k = lam_ref[:, sl]
            sp = _softplus(-lam_k)
            g = g_scr[:, sl]
            a, u, r, ig, mult = a_scr[:, sl], u_scr[:, sl], r_scr[:, sl], ig_scr[:, sl], mult_scr[:, sl]
            da = g * _shift_down(hs_ref[:, sl], [jnp.where(oldest, 0.0, hsh_ref[:, sl])], 1)
            dmult = g * (ig * u)
            d_iu = g * mult
            dla = da * a - dmult * (a * a) / mult
            dr = dla * (-RG_C * sp)
            dsp = jnp.sum(dla * (-RG_C * r), axis=0, keepdims=True)
            dlam_ref[:, sl] += dsp * (-_sigmoid(-lam_k))
            dza = dr * r * (1.0 - r)
            dzx = (d_iu * u) * ig * (1.0 - ig)
            dba_ref[:, sl] += jnp.sum(dza, axis=0, keepdims=True)
            dbx_ref[:, sl] += jnp.sum(dzx, axis=0, keepdims=True)
            ub = u.astype(MXU_DTYPE)
            dz = jnp.concatenate([dza, dzx], axis=1).astype(MXU_DTYPE)
            dwg_ref[k] += _dot(ub, dz, _TN)
            du = d_iu * ig + _dot(dz, wgt_ref[k], _NN)
            dcb_ref[:, sl] += jnp.sum(du, axis=0, keepdims=True)
            win = conv_windows(sl, rsl)
            for kk in range(kw):
                dcw_ref[kk:kk + 1, sl] += jnp.sum(du * win[kk], axis=0, keepdims=True)
            dp_ref[:, rsl] = _conv_t(du, _blocks(carry_du, sl, nh, False), cw).astype(dp_ref.dtype)
            carry_du[:, sl] = du[0:nh * SUBLANES]
        rider(None, None, oldest, r_in, r_out, r_scratch)

    tile = lambda cols: pl.BlockSpec((tm, cols), lambda i: (rev(i), 0))
    vm = lambda rows: pltpu.VMEM((rows, rr), F32)
    grads = [conv_w.shape, conv_b.shape, wg.shape, ba.shape, bx.shape, lam.shape]
    return pl.pallas_call(
        body, name=name, grid=(nt,),
        in_specs=[tile(d), _resident((d, rr)), tile(r2), tile(rr)] + _halo_specs(tm, rr, 1, rev)
                 + _halo_specs(tm, r2, nh, rev)
                 + [_full(conv_w.shape), _full(conv_b.shape), _full(wg.shape), _full(wgt.shape), _full(ba.shape),
                    _full(bx.shape), _full(lam.shape)] + list(rider.in_specs),
        out_specs=[tile(r2), tile(d)] + [_full(s) for s in grads] + list(rider.out_specs),
        out_shape=[jax.ShapeDtypeStruct((t, r2), MXU_DTYPE), jax.ShapeDtypeStruct((t, d), MXU_DTYPE)]
                  + [jax.ShapeDtypeStruct(s, F32) for s in grads] + list(rider.out_shape),
        scratch_shapes=[vm(tm)] * 7 + [vm(SUBLANES), vm(nh * SUBLANES)] + list(rider.scratch_shapes),
        compiler_params=_params("arbitrary"),
    )(dh, wot, p, hs, hs, *([p] * nh), conv_w, conv_b, wg, wgt, ba, bx, lam, *rider.inputs)


def _down_loss_head(z, w, h, g, target, n_meta, tm, name):
    t, d = h.shape
    k = z.shape[1]
    groups = tm // SUBLANES
    nt = t // tm

    def body(z_ref, w_ref, h_ref, g_ref, t_ref, dh_ref, loss_ref, dg_ref, tgt, tbuf, slabs, sems):
        i = pl.program_id(0)
        _fetch_time_tile(i, nt, t_ref, jnp.zeros((n_meta, d), F32), tbuf, sems, tm)
        _time_to_tile_order(tbuf.at[i % 2], slabs, tgt)
        x = h_ref[...] + _dot(z_ref[...], w_ref[...], _NN)
        ms = jnp.mean(x * x, axis=-1, keepdims=True)
        r = lax.rsqrt(ms + RMS_EPS)
        xhat = x * r
        gg = g_ref[...]
        row = lax.broadcasted_iota(jnp.int32, (tm, 1), 0)
        time = i * tm + jnp.right_shift(row, 3) + jnp.bitwise_and(row, SUBLANES - 1) * groups
        err = jnp.where(time >= n_meta, xhat * gg - tgt[...], 0.0)
        dout = err * (1.0 / d)
        dng = dout * gg
        c = jnp.mean(dng * xhat, axis=-1, keepdims=True)
        dh_ref[...] = r * (dng - xhat * c)
        part_dg = jnp.sum(dout * xhat, axis=0, keepdims=True)
        part_loss = jnp.broadcast_to(0.5 * jnp.sum(err * dout, keepdims=True), loss_ref.shape)

        @pl.when(i == 0)
        def _():
            dg_ref[...] = part_dg
            loss_ref[...] = part_loss

        @pl.when(i > 0)
        def _():
            dg_ref[...] += part_dg
            loss_ref[...] += part_loss

    return pl.pallas_call(
        body, name=name, grid=(nt,),
        in_specs=[pl.BlockSpec((tm, k), lambda i: (i, 0)),
                  _resident((k, d)),
                  pl.BlockSpec((tm, d), lambda i: (i, 0)),
                  pl.BlockSpec((1, d), lambda i: (0, 0)),
                  _ANY],
        out_specs=[pl.BlockSpec((tm, d), lambda i: (i, 0)),
                   pl.BlockSpec((SUBLANES, LANES), lambda i: (0, 0)),
                   pl.BlockSpec((1, d), lambda i: (0, 0))],
        out_shape=[jax.ShapeDtypeStruct((t, d), F32), jax.ShapeDtypeStruct((SUBLANES, LANES), F32),
                   jax.ShapeDtypeStruct((1, d), F32)],
        scratch_shapes=[pltpu.VMEM((tm, d), F32)] + _time_scratch(tm, d),
        compiler_params=_params("arbitrary"),
    )(z, w, h, g, target)


def _adamw(w, g, m, v, name):
    rows, cols = w.shape
    tr = rows
    if rows > 512:
        for cand in range(8, 513, 8):
            if rows % cand == 0:
                tr = cand

    def body(w_ref, g_ref, m_ref, v_ref, d_ref, nm_ref, nv_ref):
        g_ = g_ref[...]
        m_ = ADAM_B1 * m_ref[...] + (1.0 - ADAM_B1) * g_
        v_ = ADAM_B2 * v_ref[...] + (1.0 - ADAM_B2) * (g_ * g_)
        m_hat = m_ / (1.0 - ADAM_B1 ** ADAM_STEP)
        v_hat = v_ / (1.0 - ADAM_B2 ** ADAM_STEP)
        d_ref[...] = -ADAM_LR * (m_hat / (jnp.sqrt(v_hat) + ADAM_EPS) + ADAM_WD * w_ref[...])
        nm_ref[...] = m_
        nv_ref[...] = v_

    spec = pl.BlockSpec((tr, cols), lambda i: (i, 0))
    shape = jax.ShapeDtypeStruct((rows, cols), F32)
    return pl.pallas_call(
        body, name=name, grid=(rows // tr,),
        in_specs=[spec] * 4, out_specs=[spec] * 3, out_shape=[shape] * 3,
        compiler_params=_params("parallel"),
    )(w, g, m, v)


def _adamw_nd(w, g, m, v, name):
    shape = w.shape
    two_d = (-1, shape[-1]) if w.ndim > 1 else (1, -1)
    outs = _adamw(w.reshape(two_d), g.reshape(two_d), m.reshape(two_d), v.reshape(two_d), name)
    return tuple(o.reshape(shape) for o in outs)


def _ride_alone(rider, name):
    def body(*refs):
        r_in, _, r_out, _, r_scratch = _split_refs(rider, 0, 0, refs)
        now = pl.program_id(0) == 0
        rider(now, None, None, r_in, r_out, r_scratch)
        rider(None, now, None, r_in, r_out, r_scratch)
        rider(None, None, now, r_in, r_out, r_scratch)

    return pl.pallas_call(
        body, name=name, grid=(1,),
        in_specs=list(rider.in_specs), out_specs=list(rider.out_specs), out_shape=list(rider.out_shape),
        scratch_shapes=list(rider.scratch_shapes),
        compiler_params=_params("arbitrary"),
    )(*rider.inputs)


def _sum_slots(parts, name):
    slots, rows, cols = parts.shape
    tr = _row_tile(rows, 256) if rows % 16 == 0 else rows

    def body(p_ref, o_ref):
        acc = p_ref[0]
        for s in range(1, slots):
            acc = acc + p_ref[s]
        o_ref[...] = acc

    return pl.pallas_call(
        body, name=name, grid=(rows // tr,),
        in_specs=[pl.BlockSpec((slots, tr, cols), lambda i: (0, i, 0))],
        out_specs=pl.BlockSpec((tr, cols), lambda i: (i, 0)),
        out_shape=jax.ShapeDtypeStruct((rows, cols), parts.dtype),
        compiler_params=_params("parallel"),
    )(parts)


def _pad_rows(flat, cols, multiple):
    n = flat.shape[0]
    rows = -(-n // cols)
    rows = -(-rows // multiple) * multiple
    return jnp.pad(flat, (0, rows * cols - n)).reshape(rows, cols)


def _cols_to_chunks(full):
    lead = full.shape[:-1]
    c = full.shape[-1] // N_DEV
    x = full.reshape(-1, N_DEV, c)
    return jnp.transpose(x, (1, 0, 2)).reshape(N_DEV, -1)


def _chunks_to_cols(chunks, lead):
    n = 1
    for s in lead:
        n *= s
    c = chunks.shape[1] // n
    x = chunks.reshape(N_DEV, n, c)
    return jnp.transpose(x, (1, 0, 2)).reshape(tuple(lead) + (N_DEV * c,))


def kernel(x, meta_tokens, norm_mix_g, norm_ffn_g, final_norm_g, sc_w_in, sc_conv_w, sc_w_out, rg_w_in, rg_conv_w, rg_conv_b, rg_w_gate_a, rg_b_gate_a, rg_w_gate_x, rg_b_gate_x, rg_lambda, rg_w_out, ffn_w_up, ffn_conv_w, ffn_w_down, loss_target, m_meta_tokens, m_norm_mix_g, m_norm_ffn_g, m_final_norm_g, m_sc_w_in, m_sc_conv_w, m_sc_w_out, m_rg_w_in, m_rg_conv_w, m_rg_conv_b, m_rg_w_gate_a, m_rg_b_gate_a, m_rg_w_gate_x, m_rg_b_gate_x, m_rg_lambda, m_rg_w_out, m_ffn_w_up, m_ffn_conv_w, m_ffn_w_down, v_meta_tokens, v_norm_mix_g, v_norm_ffn_g, v_final_norm_g, v_sc_w_in, v_sc_conv_w, v_sc_w_out, v_rg_w_in, v_rg_conv_w, v_rg_conv_b, v_rg_w_gate_a, v_rg_b_gate_a, v_rg_w_gate_x, v_rg_b_gate_x, v_rg_lambda, v_rg_w_out, v_ffn_w_up, v_ffn_conv_w, v_ffn_w_down):
    weights = dict(meta_tokens=meta_tokens, norm_mix_g=norm_mix_g, norm_ffn_g=norm_ffn_g, final_norm_g=final_norm_g,
                   sc_w_in=sc_w_in, sc_conv_w=sc_conv_w, sc_w_out=sc_w_out, rg_w_in=rg_w_in, rg_conv_w=rg_conv_w,
                   rg_conv_b=rg_conv_b, rg_w_gate_a=rg_w_gate_a, rg_b_gate_a=rg_b_gate_a, rg_w_gate_x=rg_w_gate_x,
                   rg_b_gate_x=rg_b_gate_x, rg_lambda=rg_lambda, rg_w_out=rg_w_out, ffn_w_up=ffn_w_up,
                   ffn_conv_w=ffn_conv_w, ffn_w_down=ffn_w_down)
    m_in = dict(meta_tokens=m_meta_tokens, norm_mix_g=m_norm_mix_g, norm_ffn_g=m_norm_ffn_g, final_norm_g=m_final_norm_g,
                sc_w_in=m_sc_w_in, sc_conv_w=m_sc_conv_w, sc_w_out=m_sc_w_out, rg_w_in=m_rg_w_in, rg_conv_w=m_rg_conv_w,
                rg_conv_b=m_rg_conv_b, rg_w_gate_a=m_rg_w_gate_a, rg_b_gate_a=m_rg_b_gate_a, rg_w_gate_x=m_rg_w_gate_x,
                rg_b_gate_x=m_rg_b_gate_x, rg_lambda=m_rg_lambda, rg_w_out=m_rg_w_out, ffn_w_up=m_ffn_w_up,
                ffn_conv_w=m_ffn_conv_w, ffn_w_down=m_ffn_w_down)
    v_in = dict(meta_tokens=v_meta_tokens, norm_mix_g=v_norm_mix_g, norm_ffn_g=v_norm_ffn_g, final_norm_g=v_final_norm_g,
                sc_w_in=v_sc_w_in, sc_conv_w=v_sc_conv_w, sc_w_out=v_sc_w_out, rg_w_in=v_rg_w_in, rg_conv_w=v_rg_conv_w,
                rg_conv_b=v_rg_conv_b, rg_w_gate_a=v_rg_w_gate_a, rg_b_gate_a=v_rg_b_gate_a, rg_w_gate_x=v_rg_w_gate_x,
                rg_b_gate_x=v_rg_b_gate_x, rg_lambda=v_rg_lambda, rg_w_out=v_rg_w_out, ffn_w_up=v_ffn_w_up,
                ffn_conv_w=v_ffn_conv_w, ffn_w_down=v_ffn_w_down)
    names = list(weights)

    seq, d = x.shape[1], x.shape[2]
    n_meta = meta_tokens.shape[0]
    n_ffn = ffn_w_up.shape[0]

    assert n_ffn == 2
    def shard_rows(w_in, w_out):
        return [w_in.T.astype(MXU_DTYPE), w_out.astype(MXU_DTYPE)]

    def both_orientations(in_t_gathered, out_gathered):
        w_in_t = in_t_gathered.reshape(-1, d)
        w_out = out_gathered.reshape(-1, d)
        return w_in_t.T, w_in_t, w_out, w_out.T

    shards = {"ffn0": shard_rows(ffn_w_up[0], ffn_w_down[0]), "rg": shard_rows(rg_w_in[0], rg_w_out[0]),
              "ffn1": shard_rows(ffn_w_up[1], ffn_w_down[1])}
    ffn_w_up_f, ffn_w_up_t, ffn_w_down_f, ffn_w_down_t = [None] * 2, [None] * 2, [None] * 2, [None] * 2

    small_names = ["meta_tokens", "sc_conv_w", "rg_conv_w", "rg_conv_b", "rg_b_gate_a", "rg_b_gate_x", "rg_lambda",
                   "ffn_conv_w"]
    small_lead = {n: weights[n].shape[:-1] for n in small_names}
    small_sizes = [weights[n].size for n in small_names]
    small_flat = jnp.concatenate([weights[n].reshape(-1) for n in small_names])
    small_rows = _pad_rows(small_flat, d, SUBLANES)
    sc_in_g, sc_out_g, small_g = _ride_alone(
        _GatherRider(shard_rows(sc_w_in[0], sc_w_out[0]) + [small_rows]), "gather_first")
    sc_w_in_f, sc_w_in_t, sc_w_out_f, sc_w_out_t = both_orientations(sc_in_g, sc_out_g)
    small_g = small_g.reshape(N_DEV, -1)
    small_full = {}
    o = 0
    for n, sz in zip(small_names, small_sizes):
        small_full[n] = _chunks_to_cols(small_g[:, o:o + sz], small_lead[n])
        o += sz

    wg, wgt = _pair_gate_weights(rg_w_gate_a[0].astype(MXU_DTYPE), rg_w_gate_x[0].astype(MXU_DTYPE))
    rg_cw, rg_cb = small_full["rg_conv_w"][0], small_full["rg_conv_b"]
    rg_ba, rg_bx, rg_lam = small_full["rg_b_gate_a"], small_full["rg_b_gate_x"], small_full["rg_lambda"]
    sc_cw = small_full["sc_conv_w"][0]
    ffn_cw = small_full["ffn_conv_w"]

    tp = _row_tile(n_meta + seq, ROW_TILE_PERM)

    def ffn_fwd(h, l, rider):
        n, a, z, *gathered = _ffn_up_fused(h, norm_ffn_g[l:l + 1], ffn_w_up_f[l], ffn_cw[l], tp, f"ffn{l}_up", rider)
        return _matmul_residual(z, ffn_w_down_f[l], h, f"ffn{l}_down"), (n, a, z), gathered

    h0, n0, p0, q0, *ffn0_w = _sc_in_fused(x[0], small_full["meta_tokens"], norm_mix_g[0:1], sc_w_in_f, sc_cw, tp, "sc_in",
                                           _GatherRider(shards["ffn0"]))
    ffn_w_up_f[0], ffn_w_up_t[0], ffn_w_down_f[0], ffn_w_down_t[0] = both_orientations(*ffn0_w)
    h1 = _matmul_residual(q0, sc_w_out_f, h0, "sc_out")
    h2, ffn0_saved, rg_w = ffn_fwd(h1, 0, _GatherRider(shards["rg"]))
    rg_w_in_f, rg_w_in_t, rg_w_out_f, rg_w_out_t = both_orientations(*rg_w)
    n2, p2, y2, hs2, *ffn1_w = _rg_in_fused(h2, norm_mix_g[1:2], rg_w_in_f, rg_cw, rg_cb, wg, rg_ba, rg_bx, rg_lam,
                                            tp, "rg_in", _GatherRider(shards["ffn1"]))
    ffn_w_up_f[1], ffn_w_up_t[1], ffn_w_down_f[1], ffn_w_down_t[1] = both_orientations(*ffn1_w)
    h3 = _matmul_residual(y2, rg_w_out_f, h2, "rg_out")
    ffn1_saved = tuple(_ffn_up_fused(h3, norm_ffn_g[1:2], ffn_w_up_f[1], ffn_cw[1], tp, "ffn1_up"))
    dh4, loss_tile, d_final_g = _down_loss_head(ffn1_saved[2], ffn_w_down_f[1], h3, final_norm_g.reshape(1, d),
                                                loss_target[0], n_meta, tp, "ffn1_down_loss_head")
    loss = lax.psum(loss_tile[0, 0], AXES)

    arrived = {}

    def ffn_bwd(dh_out, h_in, saved, l):
        n, a, z = saved
        da, dhb, d_cw = _ffn_down_bwd_fused(dh_out, ffn_w_down_t[l], a, ffn_cw[l], tp, f"ffn{l}_down_bwd")
        d_w_down = _wgrad(z, dhb, f"ffn{l}_down_wgrad")
        d_w_up_t, arrived[f"ffn_w_down{l}"] = _wgrad(da, n, f"ffn{l}_up_wgrad", _ScatterRider(d_w_down))
        rows = d_w_up_t.shape[0] // N_DEV
        first_rows = (rows // 2) // (2 * SUBLANES) * (2 * SUBLANES)
        dh_in, d_g, arrived[f"ffn_w_up{l}a"] = _dgrad_in_norm(
            da, ffn_w_up_t[l], h_in, norm_ffn_g[l:l + 1], dh_out, f"ffn{l}_up_dgrad",
            _ScatterRider(d_w_up_t, (0, first_rows)))
        return dh_in, d_cw, d_g, _ScatterRider(d_w_up_t, (first_rows, rows - first_rows))

    dh3, d_fcw1, d_fg1, up1_rest = ffn_bwd(dh4, h3, ffn1_saved, 1)

    dp2, dhb3, d_rg_cw, d_rg_cb, d_wg, d_rg_ba, d_rg_bx, d_rg_lam, arrived["ffn_w_up1b"] = _rg_out_bwd_fused(
        dh3, rg_w_out_t, p2, hs2, rg_cw, rg_cb, wg, wgt, rg_ba, rg_bx, rg_lam, tp, "rg_out_bwd", up1_rest)
    d_wa, d_wx = _unpair_gate_grads(d_wg, rg_w_gate_a.shape[2])
    d_rg_w_out = _wgrad(y2, dhb3, "rg_out_wgrad")
    d_rg_w_in_t, arrived["rg_w_out"] = _wgrad(dp2, n2, "rg_in_wgrad", _ScatterRider(d_rg_w_out))
    dh2, d_mg1, arrived["rg_w_in"] = _dgrad_in_norm(
        dp2, rg_w_in_t, h2, norm_mix_g[1:2], dh3, "rg_in_dgrad", _ScatterRider(d_rg_w_in_t))

    dh1, d_fcw0, d_fg0, up0_rest = ffn_bwd(dh2, h1, ffn0_saved, 0)

    dp0, dhb1, d_sc_cw, arrived["ffn_w_up0b"] = _sc_out_bwd_fused(dh1, sc_w_out_t, p0, sc_cw, tp, "sc_out_bwd", up0_rest)
    d_sc_w_out = _wgrad(q0, dhb1, "sc_out_wgrad")
    d_sc_w_in_t, arrived["sc_w_out"] = _wgrad(dp0, n0, "sc_in_wgrad", _ScatterRider(d_sc_w_out))
    d_mg0, d_meta, d_x, arrived["sc_w_in"] = _dgrad_in_to_input(
        dp0, sc_w_in_t, h0, norm_mix_g[0:1], dh1, n_meta, tp, "sc_in_dgrad", _ScatterRider(d_sc_w_in_t))
    grad_x = d_x[None]

    grads = {}
    for n in ("sc_w_in", "rg_w_in"):
        grads[n] = _sum_slots(arrived[n], f"sum_{n}").T[None]
    for n in ("sc_w_out", "rg_w_out"):
        grads[n] = _sum_slots(arrived[n], f"sum_{n}")[None]
    grads["ffn_w_up"] = jnp.stack([jnp.concatenate(
        [_sum_slots(arrived[f"ffn_w_up{l}{part}"], f"sum_ffn_w_up{l}{part}") for part in "ab"], axis=0).T
        for l in range(n_ffn)])
    grads["ffn_w_down"] = jnp.stack([_sum_slots(arrived[f"ffn_w_down{l}"], f"sum_ffn_w_down{l}") for l in range(n_ffn)])

    small_grads = {"meta_tokens": d_meta,"sc_conv_w": d_sc_cw[None], "rg_conv_w": d_rg_cw[None],
                   "rg_conv_b": d_rg_cb, "rg_b_gate_a": d_rg_ba, "rg_b_gate_x": d_rg_bx, "rg_lambda": d_rg_lam,
                   "ffn_conv_w": jnp.stack([d_fcw0, d_fcw1])}
    small_chunks = jnp.concatenate([_cols_to_chunks(small_grads[n]) for n in small_names], axis=1)
    pad = small_rows.shape[0] * d - small_chunks.shape[1]
    small_chunks = jnp.pad(small_chunks, ((0, 0), (0, pad))).reshape(N_DEV, small_rows.shape[0], d)
    rep_names = ["norm_mix_g", "norm_ffn_g", "final_norm_g", "rg_w_gate_a", "rg_w_gate_x"]
    rep_grads = {"norm_mix_g": jnp.concatenate([d_mg0, d_mg1], axis=0),
                 "norm_ffn_g": jnp.concatenate([d_fg0, d_fg1], axis=0),
                 "final_norm_g": d_final_g.reshape(-1), "rg_w_gate_a": d_wa[None], "rg_w_gate_x": d_wx[None]}
    rep_flat = jnp.concatenate([rep_grads[n].reshape(-1) for n in rep_names])
    rep_chunk_rows = -(-rep_flat.shape[0] // (N_DEV * d))
    rep_chunk_rows += -(small_rows.shape[0] + rep_chunk_rows) % 16
    rep_chunks = jnp.pad(rep_flat, (0, N_DEV * rep_chunk_rows * d - rep_flat.shape[0])).reshape(N_DEV, rep_chunk_rows, d)

    last_chunks = jnp.concatenate([small_chunks, rep_chunks], axis=1)
    reduced = _sum_slots(_ride_alone(_ScatterRider(last_chunks.reshape(-1, d)), "scatter_last")[0], "sum_last")

    small_red = reduced[0:small_rows.shape[0]].reshape(-1)
    o = small_rows.shape[0]
    so = 0
    for n, sz in zip(small_names, small_sizes):
        grads[n] = small_red[so:so + sz].reshape(weights[n].shape)
        so += sz
    rep_red = _ride_alone(_GatherRider([reduced[o:o + rep_chunks.shape[1]]]), "gather_replicated_grads")[0].reshape(-1)
    ro = 0
    for n in rep_names:
        sz = weights[n].size
        grads[n] = rep_red[ro:ro + sz].reshape(weights[n].shape)
        ro += sz

    delta, new_m, new_v = {}, {}, {}
    for n in names:
        delta[n], new_m[n], new_v[n] = _adamw_nd(weights[n], grads[n], m_in[n], v_in[n], f"adamw_{n}")

    return (loss, grad_x, *[grads[n] for n in names], *[delta[n] for n in names],
            *[new_m[n] for n in names], *[new_v[n] for n in names])
```

```python
import jax
import jax.numpy as jnp
from jax import lax
from jax.experimental import pallas as pl
from jax.experimental.pallas import tpu as pltpu

F32 = jnp.float32
MXU_DTYPE = jnp.bfloat16
RMS_EPS = 1e-6
RG_C = 8.0
ADAM_LR = 0.001
ADAM_B1 = 0.9
ADAM_B2 = 0.999
ADAM_EPS = 1e-08
ADAM_WD = 0.01
ADAM_STEP = 10

N_DEV = 8
AXES = ("x", "y", "c")
SUBLANES = 8
LANES = 128
VMEM_LIMIT_BYTES = 48 * 1024 * 1024
ROW_TILE_MATMUL = 700
VMEM_LIMIT_WIDE_BYTES = 58 * 1024 * 1024
ROW_TILE_WGRAD = 3300
COL_TILE_WGRAD = 1408
ROW_TILE_PERM = 400
STRIP = 256
STRIP_FFN = 4096

_TN = (((0,), (0,)), ((), ()))
_NN = (((1,), (0,)), ((), ()))


def _row_tile(t, target):
    best = None
    for tm in range(16, t + 1, 16):
        if t % tm == 0 and tm <= target:
            best = tm
    return best if best is not None else t


def _col_tile(n, target):
    best = None
    for tn in range(LANES, n + 1, LANES):
        if n % tn == 0 and tn <= target:
            best = tn
    return best if best is not None else n


def _params(*sem):
    return pltpu.CompilerParams(dimension_semantics=sem, vmem_limit_bytes=VMEM_LIMIT_BYTES)


def _dot(a, b, dims):
    return lax.dot_general(a, b, dims, preferred_element_type=F32)


def _sigmoid(x):
    return 1.0 / (1.0 + jnp.exp(-x))


def _sigmoid_tanh(x):
    return 0.5 * jnp.tanh(0.5 * x) + 0.5


def _gelu(x):
    c = 0.7978845608028654
    t = jnp.tanh(c * (x + 0.044715 * (x * x * x)))
    return 0.5 * x * (1.0 + t)


def _gelu_and_grad(x):
    c = 0.7978845608028654
    x2 = x * x
    t = jnp.tanh(c * (x + 0.044715 * (x2 * x)))
    half = 0.5 * (1.0 + t)
    return x * half, half + 0.5 * x * (1.0 - t * t) * c * (1.0 + 3.0 * 0.044715 * x2)


def _softplus(x):
    return jnp.maximum(x, 0.0) + jnp.log1p(jnp.exp(-jnp.abs(x)))


def _time_scratch(tm, d):
    return [pltpu.VMEM((2, tm, d), F32), pltpu.VMEM((d // LANES, tm, LANES), F32), pltpu.SemaphoreType.DMA((2,))]


def _fetch_time_tile(i, nt, src_ref, head, tbuf, sems, tm):
    n_head = head.shape[0]

    def tile(j, slot):
        start = pl.multiple_of(j * tm - n_head, SUBLANES)
        return pltpu.make_async_copy(src_ref.at[pl.ds(start, tm)], tbuf.at[slot], sems.at[slot])

    first = pltpu.make_async_copy(src_ref.at[pl.ds(0, tm - n_head)], tbuf.at[0, pl.ds(n_head, tm - n_head)], sems.at[0])

    @pl.when(i == 0)
    def _():
        tbuf[0, 0:n_head, :] = head
        first.start()

    @pl.when(i + 1 < nt)
    def _():
        tile(i + 1, (i + 1) % 2).start()

    @pl.when(i == 0)
    def _():
        first.wait()

    @pl.when(i > 0)
    def _():
        tile(i, i % 2).wait()


def _time_to_tile_order(t_ref, slabs, out_ref):
    tm, d = out_ref.shape
    groups = tm // SUBLANES
    for k in range(d // LANES):
        slabs[k] = t_ref[:, k * LANES:(k + 1) * LANES]
    for k in range(d // LANES):
        for g in range(groups):
            out_ref[g * SUBLANES:(g + 1) * SUBLANES, k * LANES:(k + 1) * LANES] = slabs[k, pl.ds(g, SUBLANES, stride=groups), :]


def _tile_to_time_order(p_ref, slabs, t_ref):
    tm, d = p_ref.shape
    groups = tm // SUBLANES
    for k in range(d // LANES):
        for g in range(groups):
            slabs[k, pl.ds(g, SUBLANES, stride=groups), :] = p_ref[g * SUBLANES:(g + 1) * SUBLANES, k * LANES:(k + 1) * LANES]
    for k in range(d // LANES):
        t_ref[:, k * LANES:(k + 1) * LANES] = slabs[k]


def _rows(ref, sl):
    return [ref[k:k + 1, sl] for k in range(ref.shape[0])]


def _shift_down(x, before, s):
    if s == 0:
        return x
    n = x.shape[0]
    row = lax.broadcasted_iota(jnp.int32, (SUBLANES, x.shape[1]), 0)
    heads = []
    for g in range(s):
        v = x[n - (s - g) * SUBLANES:n - (s - g - 1) * SUBLANES]
        heads.append(pltpu.roll(jnp.where(row == SUBLANES - 1, before[s - g - 1], v), 1, axis=0))
    return jnp.concatenate(heads + [x[0:n - s * SUBLANES]], axis=0)


def _shift_up(x, after, s):
    if s == 0:
        return x
    row = lax.broadcasted_iota(jnp.int32, (SUBLANES, x.shape[1]), 0)
    tails = []
    for m in range(s):
        v = x[m * SUBLANES:(m + 1) * SUBLANES]
        tails.append(pltpu.roll(jnp.where(row == 0, after[m], v), SUBLANES - 1, axis=0))
    return jnp.concatenate([x[s * SUBLANES:]] + tails, axis=0)


def _weighted(w, windows):
    y = w[0] * windows[0]
    for k in range(1, len(w)):
        y = y + w[k] * windows[k]
    return y


def _windows(x, before, k_width):
    return [_shift_down(x, before, k_width - 1 - k) for k in range(k_width)]


def _conv_t(dy, after, w):
    k_width = len(w)
    return _weighted(w, [_shift_up(dy, after, k_width - 1 - k) for k in range(k_width)])


def _blocks(ref, sl, count, newest_first):
    n = ref.shape[0] // SUBLANES
    order = range(n - 1, n - 1 - count, -1) if newest_first else range(count)
    return [ref[b * SUBLANES:(b + 1) * SUBLANES, sl] for b in order]


def _halo_specs(tm, cols, count, tile_of):
    def spec(k):
        return pl.BlockSpec((SUBLANES, cols), lambda i: (jnp.maximum(tile_of(i) * (tm // SUBLANES) - k, 0), 0))
    return [spec(k) for k in range(1, count + 1)]


def _scan_tile(coef, val, out, carry, groups, reverse):
    cols = coef.shape[1]
    row = lax.broadcasted_iota(jnp.int32, (SUBLANES, cols), 0)

    def blk(i):
        g = groups - 1 - i if reverse else i
        return pl.ds(pl.multiple_of(g * SUBLANES, SUBLANES), SUBLANES)

    def local(i, pl_):
        p_prev, l_prev = pl_
        a = coef[blk(i), :]
        p = a * p_prev
        l = a * l_prev + val[blk(i), :]
        coef[blk(i), :] = p
        val[blk(i), :] = l
        return p, l

    pf, lf = lax.fori_loop(0, groups, local, (jnp.ones((SUBLANES, cols), F32), jnp.zeros((SUBLANES, cols), F32)))
    for s in (1, 2, 4):
        keep, sh = (row < SUBLANES - s, SUBLANES - s) if reverse else (row >= s, s)
        p_s = jnp.where(keep, pltpu.roll(pf, sh, axis=0), 1.0)
        l_s = jnp.where(keep, pltpu.roll(lf, sh, axis=0), 0.0)
        lf = pf * l_s + lf
        pf = pf * p_s
    end = lf + pf * carry
    if reverse:
        init = jnp.where(row == SUBLANES - 1, carry, pltpu.roll(end, SUBLANES - 1, axis=0))
        leaving = jnp.broadcast_to(end[0:1, :], (SUBLANES, cols))
    else:
        init = jnp.where(row == 0, carry, pltpu.roll(end, 1, axis=0))
        leaving = jnp.broadcast_to(end[SUBLANES - 1:SUBLANES, :], (SUBLANES, cols))

    def fix(i, _):
        out[blk(i), :] = val[blk(i), :] + coef[blk(i), :] * init
        return 0

    lax.fori_loop(0, groups, fix, 0)
    return leaving


def _resident(shape):
    return pl.BlockSpec(shape, lambda *_: (0,) * len(shape), pipeline_mode=pl.Buffered(1))


def _full(shape):
    return pl.BlockSpec(shape, lambda *_: (0,) * len(shape))


def _rmsnorm_to(h_ref, g_ref, n_ref):
    x = h_ref[...]
    ms = jnp.mean(x * x, axis=-1, keepdims=True)
    n_ref[...] = (x * lax.rsqrt(ms + RMS_EPS) * g_ref[...]).astype(n_ref.dtype)


_ANY = pl.BlockSpec(memory_space=pl.ANY)
_MESH = pl.DeviceIdType.MESH


def _dma_sems(n):
    return pltpu.SemaphoreType.DMA((n,))


def _when_each(*phases):
    for cond, fn in phases:
        if cond is not None:
            pl.when(cond)(fn)


class _NoRider:
    inputs = in_specs = out_shape = out_specs = scratch_shapes = ()

    def __call__(self, first, middle, last, ins, outs, scratch):
        pass


_NO_RIDER = _NoRider()


def _hand_on_step(nt):
    return (3 * nt) // 4


def _split_refs(rider, n_out, n_scratch, rest):
    a = len(rider.inputs)
    b = a + n_out
    c = b + len(rider.out_shape)
    e = c + n_scratch
    return rest[:a], rest[a:b], rest[b:c], rest[c:e], rest[e:]


class _GatherRider:
    def __init__(self, blocks):
        n = len(blocks)
        self.inputs = tuple(blocks)
        self.in_specs = (_ANY,) * n
        self.out_shape = tuple(jax.ShapeDtypeStruct((N_DEV,) + b.shape, b.dtype) for b in blocks)
        self.out_specs = (_ANY,) * n
        self.scratch_shapes = (_dma_sems(7 * n), _dma_sems(7 * n), _dma_sems(n))

    def __call__(self, first, middle, last, ins, outs, scratch):
        n = len(ins)
        send_sems, recv_sems, local_sems = scratch
        x, y, c = lax.axis_index("x"), lax.axis_index("y"), lax.axis_index("c")
        me, sibling = (x, y, c), (x, y, 1 - c)
        chips = [(1 - x, y), (x, 1 - y), (1 - x, 1 - y)]

        def slot(b, px, py, pc):
            return outs[b].at[4 * px + 2 * py + pc]

        def copy(k, b, block, to, src=None):
            return pltpu.make_async_remote_copy(
                src_ref=slot(b, *block) if src is None else src, dst_ref=slot(b, *block),
                send_sem=send_sems.at[k * n + b], recv_sem=recv_sems.at[k * n + b], device_id=to, device_id_type=_MESH)

        mine = [pltpu.make_async_copy(ins[b], slot(b, *me), local_sems.at[b]) for b in range(n)]
        own = [copy(0, b, me, sibling, src=ins[b]) for b in range(n)]
        own += [copy(1 + j, b, me, (*chip, c), src=ins[b]) for j, chip in enumerate(chips) for b in range(n)]
        passed = [[copy(4 + j, b, (*chip, c), sibling) for b in range(n)] for j, chip in enumerate(chips)]

        def at_first():
            for cp in mine + own:
                cp.start()

        def at_middle():
            for j, chip in enumerate(chips):
                for b in range(n):
                    copy(1 + j, b, (*chip, c), me).wait_recv()
                    passed[j][b].start()

        def at_last():
            for b in range(n):
                copy(0, b, sibling, me).wait_recv()
                for j, chip in enumerate(chips):
                    copy(4 + j, b, (*chip, 1 - c), me).wait_recv()
            for cp in own + [cp for group in passed for cp in group]:
                cp.wait_send()
            for cp in mine:
                cp.wait()

        _when_each((first, at_first), (middle, at_middle), (last, at_last))


class _ScatterRider:
    def __init__(self, grad, rows=None):
        r = grad.shape[0] // N_DEV
        chunks = grad.reshape(N_DEV, r, grad.shape[1])
        self.rows = rows if rows is not None else (0, r)
        self.inputs = (chunks,)
        self.in_specs = (_ANY,)
        self.out_shape = (jax.ShapeDtypeStruct((N_DEV, self.rows[1], grad.shape[1]), chunks.dtype),)
        self.out_specs = (_ANY,)
        self.scratch_shapes = (_dma_sems(N_DEV - 1), _dma_sems(N_DEV - 1), pltpu.SemaphoreType.DMA(()))

    def __call__(self, first, middle, last, ins, outs, scratch):
        (g_ref,), (r_ref,) = ins, outs
        send_sems, recv_sems, local_sem = scratch
        x, y, c = lax.axis_index("x"), lax.axis_index("y"), lax.axis_index("c")
        me = 4 * x + 2 * y + c
        part = pl.ds(*self.rows)
        mine = pltpu.make_async_copy(g_ref.at[me, part], r_ref.at[me], local_sem)
        copies = []
        for k in range(1, N_DEV):
            px, py, pc = (1 - x if k & 4 else x), (1 - y if k & 2 else y), (1 - c if k & 1 else c)
            copies.append(pltpu.make_async_remote_copy(
                src_ref=g_ref.at[4 * px + 2 * py + pc, part], dst_ref=r_ref.at[me],
                send_sem=send_sems.at[k - 1], recv_sem=recv_sems.at[k - 1],
                device_id=(px, py, pc), device_id_type=_MESH))

        def at_first():
            mine.start()
            for cp in copies:
                cp.start()

        def at_last():
            for cp in copies:
                cp.wait()
            mine.wait()

        _when_each((first, at_first), (last, at_last))


def _matmul_residual(q, w, h, name):
    t, k = q.shape
    d = w.shape[1]
    tm = _row_tile(t, ROW_TILE_MATMUL)

    def body(q_ref, w_ref, h_ref, o_ref):
        o_ref[...] = h_ref[...] + _dot(q_ref[...], w_ref[...], _NN)

    return pl.pallas_call(
        body, name=name, grid=(t // tm,),
        in_specs=[pl.BlockSpec((tm, k), lambda i: (i, 0)),
                  _resident((k, d)),
                  pl.BlockSpec((tm, d), lambda i: (i, 0))],
        out_specs=pl.BlockSpec((tm, d), lambda i: (i, 0)),
        out_shape=jax.ShapeDtypeStruct((t, d), F32),
        compiler_params=_params("parallel"),
    )(q, w, h)


def _dgrad_in_norm(dp, wt, h, g, dh_next, name, rider=_NO_RIDER):
    t, n = dp.shape
    d = wt.shape[1]
    tm = _row_tile(t, ROW_TILE_MATMUL)
    nt = t // tm

    def body(dp_ref, wt_ref, h_ref, g_ref, dhn_ref, *rest):
        r_in, (dh_ref, dg_ref), r_out, _, r_scratch = _split_refs(rider, 2, 0, rest)
        i = pl.program_id(0)
        rider(i == 0, i == nt // 2, None, r_in, r_out, r_scratch)
        dn = _dot(dp_ref[...], wt_ref[...], _NN)
        x = h_ref[...]
        ms = jnp.mean(x * x, axis=-1, keepdims=True)
        r = lax.rsqrt(ms + RMS_EPS)
        xhat = x * r
        dng = dn * g_ref[...]
        c = jnp.mean(dng * xhat, axis=-1, keepdims=True)
        dh_ref[...] = dhn_ref[...] + r * (dng - xhat * c)
        part = jnp.sum(dn * xhat, axis=0, keepdims=True)

        @pl.when(i == 0)
        def _():
            dg_ref[...] = part

        @pl.when(i > 0)
        def _():
            dg_ref[...] += part

        rider(None, None, i == nt - 1, r_in, r_out, r_scratch)

    return pl.pallas_call(
        body, name=name, grid=(nt,),
        in_specs=[pl.BlockSpec((tm, n), lambda i: (i, 0)),
                  _resident((n, d)),
                  pl.BlockSpec((tm, d), lambda i: (i, 0)),
                  pl.BlockSpec((1, d), lambda i: (0, 0)),
                  pl.BlockSpec((tm, d), lambda i: (i, 0))] + list(rider.in_specs),
        out_specs=[pl.BlockSpec((tm, d), lambda i: (i, 0)),
                   pl.BlockSpec((1, d), lambda i: (0, 0))] + list(rider.out_specs),
        out_shape=[jax.ShapeDtypeStruct((t, d), F32), jax.ShapeDtypeStruct((1, d), F32)] + list(rider.out_shape),
        scratch_shapes=list(rider.scratch_shapes),
        compiler_params=_params("arbitrary"),
    )(dp, wt, h, g, dh_next, *rider.inputs)


def _dgrad_in_to_input(dp, wt, h, g, dh_next, n_head, tm, name, rider=_NO_RIDER):
    t, n = dp.shape
    d = wt.shape[1]
    nt = t // tm

    def body(dp_ref, wt_ref, h_ref, g_ref, dhn_ref, *rest):
        r_in, (dg_ref, head_ref, rest_ref), r_out, (pbuf, tout, slabs, sems), r_scratch = _split_refs(rider, 3, 4, rest)
        i = pl.program_id(0)
        rider(i == 0, i == nt // 2, None, r_in, r_out, r_scratch)
        dn = _dot(dp_ref[...], wt_ref[...], _NN)
        x = h_ref[...]
        ms = jnp.mean(x * x, axis=-1, keepdims=True)
        r = lax.rsqrt(ms + RMS_EPS)
        xhat = x * r
        dng = dn * g_ref[...]
        c = jnp.mean(dng * xhat, axis=-1, keepdims=True)
        pbuf[...] = dhn_ref[...] + r * (dng - xhat * c)
        part = jnp.sum(dn * xhat, axis=0, keepdims=True)

        @pl.when(i == 0)
        def _():
            dg_ref[...] = part

        @pl.when(i > 0)
        def _():
            dg_ref[...] += part

        def store(j):
            if isinstance(j, int) and j == 0:
                return pltpu.make_async_copy(tout.at[0, pl.ds(n_head, tm - n_head)],
                                             rest_ref.at[pl.ds(0, tm - n_head)], sems.at[0])
            start = j * tm - n_head
            start = start if isinstance(start, int) else pl.multiple_of(start, SUBLANES)
            return pltpu.make_async_copy(tout.at[j % 2], rest_ref.at[pl.ds(start, tm)], sems.at[j % 2])

        @pl.when(i == 2)
        def _():
            store(0).wait()

        @pl.when(i > 2)
        def _():
            store(i - 2).wait()

        _tile_to_time_order(pbuf, slabs, tout.at[i % 2])

        @pl.when(i == 0)
        def _():
            head_ref[...] = tout[0, 0:n_head, :]
            store(0).start()

        @pl.when(i > 0)
        def _():
            store(i).start()

        @pl.when(i == nt - 1)
        def _():
            for j in (nt - 2, nt - 1):
                if j >= 0:
                    store(j).wait()

        rider(None, None, i == nt - 1, r_in, r_out, r_scratch)

    return pl.pallas_call(
        body, name=name, grid=(nt,),
        in_specs=[pl.BlockSpec((tm, n), lambda i: (i, 0)),
                  _resident((n, d)),
                  pl.BlockSpec((tm, d), lambda i: (i, 0)),
                  pl.BlockSpec((1, d), lambda i: (0, 0)),
                  pl.BlockSpec((tm, d), lambda i: (i, 0))] + list(rider.in_specs),
        out_specs=[pl.BlockSpec((1, d), lambda i: (0, 0)), _full((n_head, d)), _ANY] + list(rider.out_specs),
        out_shape=[jax.ShapeDtypeStruct((1, d), F32), jax.ShapeDtypeStruct((n_head, d), F32),
                   jax.ShapeDtypeStruct((t - n_head, d), F32)] + list(rider.out_shape),
        scratch_shapes=[pltpu.VMEM((tm, d), F32)] + _time_scratch(tm, d) + list(rider.scratch_shapes),
        compiler_params=_params("arbitrary"),
    )(dp, wt, h, g, dh_next, *rider.inputs)


def _wgrad(a, b, name, rider=_NO_RIDER):
    t, m = a.shape
    d = b.shape[1]
    tmm, tk = _col_tile(m, COL_TILE_WGRAD), _row_tile(t, ROW_TILE_WGRAD)
    nm, nk = m // tmm, t // tk

    def body(a_ref, b_ref, *rest):
        r_in, (o_ref,), r_out, _, r_scratch = _split_refs(rider, 1, 0, rest)
        i, k = pl.program_id(0), pl.program_id(1)
        rider(jnp.logical_and(i == 0, k == 0), None, None, r_in, r_out, r_scratch)

        @pl.when(k == 0)
        def _():
            o_ref[...] = jnp.zeros_like(o_ref)

        o_ref[...] += _dot(a_ref[...], b_ref[...], _TN)
        rider(None, None, jnp.logical_and(i == nm - 1, k == nk - 1), r_in, r_out, r_scratch)

    outs = pl.pallas_call(
        body, name=name, grid=(nm, nk),
        in_specs=[pl.BlockSpec((tk, tmm), lambda i, k: (k, i)),
                  pl.BlockSpec((tk, d), lambda i, k: (k, 0))] + list(rider.in_specs),
        out_specs=[pl.BlockSpec((tmm, d), lambda i, k: (i, 0))] + list(rider.out_specs),
        out_shape=[jax.ShapeDtypeStruct((m, d), F32)] + list(rider.out_shape),
        scratch_shapes=list(rider.scratch_shapes),
        compiler_params=pltpu.CompilerParams(dimension_semantics=("arbitrary", "arbitrary"),
                                             vmem_limit_bytes=VMEM_LIMIT_WIDE_BYTES),
    )(a, b, *rider.inputs)
    return outs if rider.out_shape else outs[0]


def _ffn_up_fused(h, g, w, conv_w, tm, name, rider=_NO_RIDER):
    t, d = h.shape
    f2 = w.shape[1]
    f = f2 // 2
    cw = min(STRIP_FFN, f)
    kw = conv_w.shape[0]
    nh = kw - 1
    nt = t // tm

    def body(h_ref, g_ref, w_ref, cw_ref, *rest):
        r_in, (n_ref, a_ref, z_ref), r_out, (carry,), r_scratch = _split_refs(rider, 3, 1, rest)
        i = pl.program_id(0)
        rider(i == 0, i == _hand_on_step(nt), None, r_in, r_out, r_scratch)

        @pl.when(i == 0)
        def _():
            carry[...] = jnp.zeros_like(carry)

        _rmsnorm_to(h_ref, g_ref, n_ref)
        for c in range(0, f, cw):
            conv = []
            for sl in (slice(c, c + cw), slice(f + c, f + c + cw)):
                a = _dot(n_ref[...], w_ref[:, sl], _NN)
                a_ref[:, sl] = a
                conv.append(_weighted(_rows(cw_ref, sl), _windows(a, _blocks(carry, sl, nh, True), kw)))
                carry[:, sl] = a[tm - nh * SUBLANES:tm]
            gg, vv = conv
            z_ref[:, c:c + cw] = (gg * _sigmoid(gg) * vv).astype(z_ref.dtype)
        rider(None, None, i == nt - 1, r_in, r_out, r_scratch)

    return pl.pallas_call(
        body, name=name, grid=(nt,),
        in_specs=[pl.BlockSpec((tm, d), lambda i: (i, 0)), pl.BlockSpec((1, d), lambda i: (0, 0)),
                  _resident((d, f2)), _full(conv_w.shape)] + list(rider.in_specs),
        out_specs=[pl.BlockSpec((tm, d), lambda i: (i, 0)), pl.BlockSpec((tm, f2), lambda i: (i, 0)),
                   pl.BlockSpec((tm, f), lambda i: (i, 0))] + list(rider.out_specs),
        out_shape=[jax.ShapeDtypeStruct((t, d), MXU_DTYPE), jax.ShapeDtypeStruct((t, f2), F32),
                   jax.ShapeDtypeStruct((t, f), MXU_DTYPE)] + list(rider.out_shape),
        scratch_shapes=[pltpu.VMEM((nh * SUBLANES, f2), F32)] + list(rider.scratch_shapes),
        compiler_params=_params("arbitrary"),
    )(h, g, w, conv_w, *rider.inputs)


def _ffn_down_bwd_fused(dh, wdt, a, conv_w, tm, name):
    t, d = dh.shape
    f = wdt.shape[1]
    f2 = 2 * f
    nt = t // tm
    cw = min(STRIP_FFN, f)
    kw = conv_w.shape[0]
    nh = kw - 1
    rev = lambda i: nt - 1 - i

    def body(dh_ref, wdt_ref, a_ref, *rest):
        halo_refs = rest[:nh]
        cw_ref, da_ref, dhb_ref, dw_ref, carry = rest[nh:]
        i = pl.program_id(0)
        newest, oldest = i == 0, i == nt - 1

        @pl.when(newest)
        def _():
            carry[...] = jnp.zeros_like(carry)
            dw_ref[...] = jnp.zeros_like(dw_ref)

        dhb_ref[...] = dh_ref[...].astype(dhb_ref.dtype)
        for c in range(0, f, cw):
            gsl, vsl = slice(c, c + cw), slice(f + c, f + c + cw)
            dz = _dot(dhb_ref[...], wdt_ref[:, gsl], _NN)
            win, conv = {}, {}
            for sl in (gsl, vsl):
                before = [jnp.where(oldest, 0.0, r[:, sl]) for r in halo_refs]
                win[sl.start] = _windows(a_ref[:, sl], before, kw)
                conv[sl.start] = _weighted(_rows(cw_ref, sl), win[sl.start])
            gg, vv = conv[gsl.start], conv[vsl.start]
            s = _sigmoid(gg)
            grads = {gsl.start: dz * vv * s * (1.0 + gg * (1.0 - s)), vsl.start: dz * gg * s}
            for sl in (gsl, vsl):
                dconv = grads[sl.start]
                da_ref[:, sl] = _conv_t(dconv, _blocks(carry, sl, nh, False), _rows(cw_ref, sl)).astype(da_ref.dtype)
                for k in range(kw):
                    dw_ref[k:k + 1, sl] += jnp.sum(dconv * win[sl.start][k], axis=0, keepdims=True)
                carry[:, sl] = dconv[0:nh * SUBLANES]

    return pl.pallas_call(
        body, name=name, grid=(nt,),
        in_specs=[pl.BlockSpec((tm, d), lambda i: (rev(i), 0)), _resident((d, f)),
                  pl.BlockSpec((tm, f2), lambda i: (rev(i), 0))] + _halo_specs(tm, f2, nh, rev) + [_full(conv_w.shape)],
        out_specs=[pl.BlockSpec((tm, f2), lambda i: (rev(i), 0)), pl.BlockSpec((tm, d), lambda i: (rev(i), 0)),
                   _full(conv_w.shape)],
        out_shape=[jax.ShapeDtypeStruct((t, f2), MXU_DTYPE), jax.ShapeDtypeStruct((t, d), MXU_DTYPE),
                   jax.ShapeDtypeStruct(conv_w.shape, F32)],
        scratch_shapes=[pltpu.VMEM((nh * SUBLANES, f2), F32)],
        compiler_params=_params("arbitrary"),
    )(dh, wdt, a, *([a] * nh), conv_w)


def _sc_in_fused(x, meta, g, w, conv_w, tm, name, rider=_NO_RIDER):
    d = x.shape[1]
    t = x.shape[0] + meta.shape[0]
    cw = min(STRIP, d)
    kw = conv_w.shape[0]
    nh = kw - 1
    nt = t // tm

    def body(x_ref, meta_ref, g_ref, w_ref, cw_ref, *rest):
        r_in, (h_ref, n_ref, p_ref, q_ref), r_out, (carry, tbuf, slabs, sems), r_scratch = _split_refs(rider, 4, 4, rest)
        i = pl.program_id(0)
        rider(i == 0, i == _hand_on_step(nt), None, r_in, r_out, r_scratch)

        @pl.when(i == 0)
        def _():
            carry[...] = jnp.zeros_like(carry)

        _fetch_time_tile(i, nt, x_ref, meta_ref[...], tbuf, sems, tm)
        _time_to_tile_order(tbuf.at[i % 2], slabs, h_ref)
        _rmsnorm_to(h_ref, g_ref, n_ref)
        for c in range(0, d, cw):
            sl = slice(c, c + cw)
            parts = []
            for base in (0, d, 2 * d):
                psl = slice(base + c, base + c + cw)
                parts.append(_dot(n_ref[...], w_ref[:, psl], _NN))
                p_ref[:, psl] = parts[-1]
            bg, cg, v = parts
            cv = cg * v
            u = _weighted(_rows(cw_ref, sl), _windows(cv, _blocks(carry, sl, nh, True), kw))
            carry[:, sl] = cv[tm - nh * SUBLANES:tm]
            q_ref[:, sl] = (bg * u).astype(q_ref.dtype)
        rider(None, None, i == nt - 1, r_in, r_out, r_scratch)

    row_tile = lambda cols: pl.BlockSpec((tm, cols), lambda i: (i, 0))
    return pl.pallas_call(
        body, name=name, grid=(nt,),
        in_specs=[_ANY, _full(meta.shape), pl.BlockSpec((1, d), lambda i: (0, 0)),
                  _resident((d, 3 * d)), _full(conv_w.shape)] + list(rider.in_specs),
        out_specs=[row_tile(d), row_tile(d), row_tile(3 * d), row_tile(d)] + list(rider.out_specs),
        out_shape=[jax.ShapeDtypeStruct((t, d), F32), jax.ShapeDtypeStruct((t, d), MXU_DTYPE),
                   jax.ShapeDtypeStruct((t, 3 * d), F32), jax.ShapeDtypeStruct((t, d), MXU_DTYPE)]
                  + list(rider.out_shape),
        scratch_shapes=[pltpu.VMEM((nh * SUBLANES, d), F32)] + _time_scratch(tm, d) + list(rider.scratch_shapes),
        compiler_params=_params("arbitrary"),
    )(x, meta, g, w, conv_w, *rider.inputs)


def _sc_out_bwd_fused(dh, wot, p, conv_w, tm, name, rider=_NO_RIDER):
    t, d = dh.shape
    nt = t // tm
    cw = min(STRIP, d)
    kw = conv_w.shape[0]
    nh = kw - 1
    rev = lambda i: nt - 1 - i

    def body(dh_ref, wot_ref, p_ref, *rest):
        halo_refs, cw_ref = rest[:nh], rest[nh]
        r_in, (dp_ref, dhb_ref, dw_ref), r_out, (carry,), r_scratch = _split_refs(rider, 3, 1, rest[nh + 1:])
        i = pl.program_id(0)
        newest, oldest = i == 0, i == nt - 1
        rider(newest, i == nt // 2, None, r_in, r_out, r_scratch)

        @pl.when(newest)
        def _():
            carry[...] = jnp.zeros_like(carry)
            dw_ref[...] = jnp.zeros_like(dw_ref)

        dhb_ref[...] = dh_ref[...].astype(dhb_ref.dtype)
        for c in range(0, d, cw):
            sl, csl, vsl = slice(c, c + cw), slice(d + c, d + c + cw), slice(2 * d + c, 2 * d + c + cw)
            w = _rows(cw_ref, sl)
            dq = _dot(dhb_ref[...], wot_ref[:, sl], _NN)
            bg, cg, v = p_ref[:, sl], p_ref[:, csl], p_ref[:, vsl]
            before = [jnp.where(oldest, 0.0, r[:, csl] * r[:, vsl]) for r in halo_refs]
            win = _windows(cg * v, before, kw)
            du = dq * bg
            dcv = _conv_t(du, _blocks(carry, sl, nh, False), w)
            dp_ref[:, sl] = (dq * _weighted(w, win)).astype(dp_ref.dtype)
            dp_ref[:, csl] = (dcv * v).astype(dp_ref.dtype)
            dp_ref[:, vsl] = (dcv * cg).astype(dp_ref.dtype)
            for k in range(kw):
                dw_ref[k:k + 1, sl] += jnp.sum(du * win[k], axis=0, keepdims=True)
            carry[:, sl] = du[0:nh * SUBLANES]
        rider(None, None, oldest, r_in, r_out, r_scratch)

    tile = lambda cols: pl.BlockSpec((tm, cols), lambda i: (rev(i), 0))
    return pl.pallas_call(
        body, name=name, grid=(nt,),
        in_specs=[tile(d), _resident((d, d)), tile(3 * d)] + _halo_specs(tm, 3 * d, nh, rev) + [_full(conv_w.shape)]
                 + list(rider.in_specs),
        out_specs=[tile(3 * d), tile(d), _full(conv_w.shape)] + list(rider.out_specs),
        out_shape=[jax.ShapeDtypeStruct((t, 3 * d), MXU_DTYPE), jax.ShapeDtypeStruct((t, d), MXU_DTYPE),
                   jax.ShapeDtypeStruct(conv_w.shape, F32)] + list(rider.out_shape),
        scratch_shapes=[pltpu.VMEM((nh * SUBLANES, d), F32)] + list(rider.scratch_shapes),
        compiler_params=_params("arbitrary"),
    )(dh, wot, p, *([p] * nh), conv_w, *rider.inputs)


def _pair_gate_weights(wa, wx):
    nb, bd, _ = wa.shape
    zero = jnp.zeros((nb // 2, bd, bd), wa.dtype)

    def pair(w):
        w = w.reshape(nb // 2, 2, bd, bd)
        top = jnp.concatenate([w[:, 0], zero], axis=2)
        bottom = jnp.concatenate([zero, w[:, 1]], axis=2)
        return jnp.concatenate([top, bottom], axis=1)

    both = jnp.concatenate([pair(wa), pair(wx)], axis=2)
    return both, jnp.swapaxes(both, 1, 2)


def _unpair_gate_grads(dw, bd):
    def blocks(cols):
        d0 = dw[:, :bd, cols:cols + bd]
        d1 = dw[:, bd:, cols + bd:cols + 2 * bd]
        return jnp.stack([d0, d1], axis=1).reshape(-1, bd, bd)
    return blocks(0), blocks(2 * bd)


def _rg_gates(u, k, wg_ref, ba, bx, lam):
    ub = u.astype(MXU_DTYPE)
    z = _dot(ub, wg_ref[k], _NN)
    half = z.shape[1] // 2
    r = _sigmoid_tanh(z[:, :half] + ba)
    ig = _sigmoid_tanh(z[:, half:] + bx)
    sp = _softplus(-lam)
    la = -RG_C * r * sp
    a = jnp.exp(la)
    th = jnp.tanh(la)
    mult = jnp.sqrt(-2.0 * th / (1.0 - th))
    return r, ig, a, mult


def _rg_in_fused(h, g, w, conv_w, conv_b, wg, ba, bx, lam, tm, name, rider=_NO_RIDER):
    t, d = h.shape
    r2 = w.shape[1]
    rr = r2 // 2
    nb, bd = wg.shape[0], wg.shape[1]
    kw = conv_w.shape[0]
    nh = kw - 1
    groups = tm // SUBLANES
    nt = t // tm

    def body(h_ref, g_ref, w_ref, cw_ref, cb_ref, wg_ref, ba_ref, bx_ref, lam_ref, *rest):
        (r_in, (n_ref, p_ref, y_ref, hs_ref, u_ref, r_ref, ig_ref, a_ref, mult_ref), r_out,
         (a_scr, b_scr, carry_rb, carry_h), r_scratch) = _split_refs(rider, 9, 4, rest)
        i = pl.program_id(0)
        rider(i == 0, i == _hand_on_step(nt), None, r_in, r_out, r_scratch)

        @pl.when(i == 0)
        def _():
            carry_rb[...] = jnp.zeros_like(carry_rb)
            carry_h[...] = jnp.zeros_like(carry_h)

        _rmsnorm_to(h_ref, g_ref, n_ref)
        for k in range(nb):
            sl = slice(k * bd, (k + 1) * bd)
            rsl = slice(rr + k * bd, rr + (k + 1) * bd)
            p_ref[:, sl] = _dot(n_ref[...], w_ref[:, sl], _NN)
            rb = _dot(n_ref[...], w_ref[:, rsl], _NN)
            p_ref[:, rsl] = rb
            u = _weighted(_rows(cw_ref, sl), _windows(rb, _blocks(carry_rb, sl, nh, True), kw)) + cb_ref[:, sl]
            carry_rb[:, sl] = rb[tm - nh * SUBLANES:tm]
            r, ig, a, mult = _rg_gates(u, k, wg_ref, ba_ref[:, sl], bx_ref[:, sl], lam_ref[:, sl])
            for ref, val in ((u_ref, u), (r_ref, r), (ig_ref, ig), (a_ref, a), (mult_ref, mult)):
                ref[:, sl] = val
            a_scr[:, sl] = a
            b_scr[:, sl] = mult * (ig * u)

        carry_h[...] = _scan_tile(a_scr, b_scr, hs_ref, carry_h[...], groups, reverse=False)

        for k in range(nb):
            sl = slice(k * bd, (k + 1) * bd)
            y_ref[:, sl] = (hs_ref[:, sl] * _gelu(p_ref[:, sl])).astype(y_ref.dtype)
        rider(None, None, i == nt - 1, r_in, r_out, r_scratch)

    vm = lambda rows: pltpu.VMEM((rows, rr), F32)
    return pl.pallas_call(
        body, name=name, grid=(nt,),
        in_specs=[pl.BlockSpec((tm, d), lambda i: (i, 0)), pl.BlockSpec((1, d), lambda i: (0, 0)),
                  _resident((d, r2)), _full(conv_w.shape), _full(conv_b.shape), _full(wg.shape), _full(ba.shape),
                  _full(bx.shape), _full(lam.shape)] + list(rider.in_specs),
        out_specs=[pl.BlockSpec((tm, d), lambda i: (i, 0)), pl.BlockSpec((tm, r2), lambda i: (i, 0))]
                  + [pl.BlockSpec((tm, rr), lambda i: (i, 0))] * 7 + list(rider.out_specs),
        out_shape=[jax.ShapeDtypeStruct((t, d), MXU_DTYPE), jax.ShapeDtypeStruct((t, r2), F32),
                   jax.ShapeDtypeStruct((t, rr), MXU_DTYPE)] + [jax.ShapeDtypeStruct((t, rr), F32)] * 6
                  + list(rider.out_shape),
        scratch_shapes=[vm(tm), vm(tm), vm(nh * SUBLANES), vm(SUBLANES)] + list(rider.scratch_shapes),
        compiler_params=pltpu.CompilerParams(dimension_semantics=("arbitrary",), vmem_limit_bytes=VMEM_LIMIT_WIDE_BYTES),
    )(h, g, w, conv_w, conv_b, wg, ba, bx, lam, *rider.inputs)


def _rg_out_bwd_fused(dh, wot, p, hs, gates, conv_w, wg, wgt, lam, tm, name, rider=_NO_RIDER):
    t, d = dh.shape
    r2 = p.shape[1]
    rr = r2 // 2
    nb, bd = wg.shape[0], wg.shape[1]
    nt = t // tm
    kw = conv_w.shape[0]
    nh = kw - 1
    groups = tm // SUBLANES
    rev = lambda i: nt - 1 - i

    def body(dh_ref, wot_ref, p_ref, hs_ref, hsh_ref, u_ref, r_ref, ig_ref, a_ref, mult_ref, *rest):
        halo_refs = rest[:nh]
        cw_ref, wg_ref, wgt_ref, lam_ref = rest[nh:nh + 4]
        (r_in, (dp_ref, dhb_ref, dcw_ref, dcb_ref, dwg_ref, dba_ref, dbx_ref, dlam_ref), r_out,
         (as_scr, g_scr, carry_g, carry_du), r_scratch) = _split_refs(rider, 8, 4, rest[nh + 4:])
        i = pl.program_id(0)
        newest, oldest = i == 0, i == nt - 1
        rider(newest, i == nt // 2, None, r_in, r_out, r_scratch)

        @pl.when(newest)
        def _():
            carry_g[...] = jnp.zeros_like(carry_g)
            carry_du[...] = jnp.zeros_like(carry_du)
            for ref in (dcw_ref, dcb_ref, dwg_ref, dba_ref, dbx_ref, dlam_ref):
                ref[...] = jnp.zeros_like(ref)

        dhb_ref[...] = dh_ref[...].astype(dhb_ref.dtype)
        ones = jnp.ones((SUBLANES, bd), F32)

        def conv_windows(sl, rsl):
            before = [jnp.where(oldest, 0.0, r[:, rsl]) for r in halo_refs]
            return _windows(p_ref[:, rsl], before, kw)

        for k in range(nb):
            sl = slice(k * bd, (k + 1) * bd)
            rsl = slice(rr + k * bd, rr + (k + 1) * bd)
            dy = _dot(dhb_ref[...], wot_ref[:, sl], _NN)
            gate, dgate = _gelu_and_grad(p_ref[:, sl])
            g_scr[:, sl] = dy * gate
            dp_ref[:, sl] = (dy * hs_ref[:, sl] * dgate).astype(dp_ref.dtype)
            as_scr[:, sl] = _shift_up(a_ref[:, sl], [ones], 1)

        g_first = _scan_tile(as_scr, g_scr, g_scr, carry_g[...], groups, reverse=True)
        carry_g[...] = jnp.broadcast_to(a_ref[0:1, :], (SUBLANES, rr)) * g_first

        for k in range(nb):
            sl = slice(k * bd, (k + 1) * bd)
            rsl = slice(rr + k * bd, rr + (k + 1) * bd)
            cw = _rows(cw_ref, sl)
            lam_k = lam_ref[:, sl]
            sp = _softplus(-lam_k)
            g = g_scr[:, sl]
            a, u, r, ig, mult = a_ref[:, sl], u_ref[:, sl], r_ref[:, sl], ig_ref[:, sl], mult_ref[:, sl]
            da = g * _shift_down(hs_ref[:, sl], [jnp.where(oldest, 0.0, hsh_ref[:, sl])], 1)
            dmult = g * (ig * u)
            d_iu = g * mult
            dla = da * a - dmult * (a * a) / mult
            dr = dla * (-RG_C * sp)
            dsp = jnp.sum(dla * (-RG_C * r), axis=0, keepdims=True)
            dlam_ref[:, sl] += dsp * (-_sigmoid(-lam_k))
            dza = dr * r * (1.0 - r)
            dzx = (d_iu * u) * ig * (1.0 - ig)
            dba_ref[:, sl] += jnp.sum(dza, axis=0, keepdims=True)
            dbx_ref[:, sl] += jnp.sum(dzx, axis=0, keepdims=True)
            ub = u.astype(MXU_DTYPE)
            dz = jnp.concatenate([dza, dzx], axis=1).astype(MXU_DTYPE)
            dwg_ref[k] += _dot(ub, dz, _TN)
            du = d_iu * ig + _dot(dz, wgt_ref[k], _NN)
            dcb_ref[:, sl] += jnp.sum(du, axis=0, keepdims=True)
            win = conv_windows(sl, rsl)
            for kk in range(kw):
                dcw_ref[kk:kk + 1, sl] += jnp.sum(du * win[kk], axis=0, keepdims=True)
            dp_ref[:, rsl] = _conv_t(du, _blocks(carry_du, sl, nh, False), cw).astype(dp_ref.dtype)
            carry_du[:, sl] = du[0:nh * SUBLANES]
        rider(None, None, oldest, r_in, r_out, r_scratch)

    tile = lambda cols: pl.BlockSpec((tm, cols), lambda i: (rev(i), 0))
    vm = lambda rows: pltpu.VMEM((rows, rr), F32)
    grads = [conv_w.shape, lam.shape, wg.shape, lam.shape, lam.shape, lam.shape]
    return pl.pallas_call(
        body, name=name, grid=(nt,),
        in_specs=[tile(d), _resident((d, rr)), tile(r2), tile(rr)] + _halo_specs(tm, rr, 1, rev) + [tile(rr)] * 5
                 + _halo_specs(tm, r2, nh, rev)
                 + [_full(conv_w.shape), _full(wg.shape), _full(wgt.shape), _full(lam.shape)] + list(rider.in_specs),
        out_specs=[tile(r2), tile(d)] + [_full(s) for s in grads] + list(rider.out_specs),
        out_shape=[jax.ShapeDtypeStruct((t, r2), MXU_DTYPE), jax.ShapeDtypeStruct((t, d), MXU_DTYPE)]
                  + [jax.ShapeDtypeStruct(s, F32) for s in grads] + list(rider.out_shape),
        scratch_shapes=[vm(tm)] * 2 + [vm(SUBLANES), vm(nh * SUBLANES)] + list(rider.scratch_shapes),
        compiler_params=pltpu.CompilerParams(dimension_semantics=("arbitrary",), vmem_limit_bytes=VMEM_LIMIT_WIDE_BYTES),
    )(dh, wot, p, hs, hs, *gates, *([p] * nh), conv_w, wg, wgt, lam, *rider.inputs)


def _down_loss_head(z, w, h, g, target, n_meta, tm, name):
    t, d = h.shape
    k = z.shape[1]
    groups = tm // SUBLANES
    nt = t // tm

    def body(z_ref, w_ref, h_ref, g_ref, t_ref, dh_ref, loss_ref, dg_ref, tgt, tbuf, slabs, sems):
        i = pl.program_id(0)
        _fetch_time_tile(i, nt, t_ref, jnp.zeros((n_meta, d), F32), tbuf, sems, tm)
        _time_to_tile_order(tbuf.at[i % 2], slabs, tgt)
        x = h_ref[...] + _dot(z_ref[...], w_ref[...], _NN)
        ms = jnp.mean(x * x, axis=-1, keepdims=True)
        r = lax.rsqrt(ms + RMS_EPS)
        xhat = x * r
        gg = g_ref[...]
        row = lax.broadcasted_iota(jnp.int32, (tm, 1), 0)
        time = i * tm + jnp.right_shift(row, 3) + jnp.bitwise_and(row, SUBLANES - 1) * groups
        err = jnp.where(time >= n_meta, xhat * gg - tgt[...], 0.0)
        dout = err * (1.0 / d)
        dng = dout * gg
        c = jnp.mean(dng * xhat, axis=-1, keepdims=True)
        dh_ref[...] = r * (dng - xhat * c)
        part_dg = jnp.sum(dout * xhat, axis=0, keepdims=True)
        part_loss = jnp.broadcast_to(0.5 * jnp.sum(err * dout, keepdims=True), loss_ref.shape)

        @pl.when(i == 0)
        def _():
            dg_ref[...] = part_dg
            loss_ref[...] = part_loss

        @pl.when(i > 0)
        def _():
            dg_ref[...] += part_dg
            loss_ref[...] += part_loss

    return pl.pallas_call(
        body, name=name, grid=(nt,),
        in_specs=[pl.BlockSpec((tm, k), lambda i: (i, 0)),
                  _resident((k, d)),
                  pl.BlockSpec((tm, d), lambda i: (i, 0)),
                  pl.BlockSpec((1, d), lambda i: (0, 0)),
                  _ANY],
        out_specs=[pl.BlockSpec((tm, d), lambda i: (i, 0)),
                   pl.BlockSpec((SUBLANES, LANES), lambda i: (0, 0)),
                   pl.BlockSpec((1, d), lambda i: (0, 0))],
        out_shape=[jax.ShapeDtypeStruct((t, d), F32), jax.ShapeDtypeStruct((SUBLANES, LANES), F32),
                   jax.ShapeDtypeStruct((1, d), F32)],
        scratch_shapes=[pltpu.VMEM((tm, d), F32)] + _time_scratch(tm, d),
        compiler_params=_params("arbitrary"),
    )(z, w, h, g, target)


def _adamw(w, g, m, v, name):
    rows, cols = w.shape
    tr = rows
    if rows > 512:
        for cand in range(8, 513, 8):
            if rows % cand == 0:
                tr = cand

    def body(w_ref, g_ref, m_ref, v_ref, d_ref, nm_ref, nv_ref):
        g_ = g_ref[...]
        m_ = ADAM_B1 * m_ref[...] + (1.0 - ADAM_B1) * g_
        v_ = ADAM_B2 * v_ref[...] + (1.0 - ADAM_B2) * (g_ * g_)
        m_hat = m_ / (1.0 - ADAM_B1 ** ADAM_STEP)
        v_hat = v_ / (1.0 - ADAM_B2 ** ADAM_STEP)
        d_ref[...] = -ADAM_LR * (m_hat / (jnp.sqrt(v_hat) + ADAM_EPS) + ADAM_WD * w_ref[...])
        nm_ref[...] = m_
        nv_ref[...] = v_

    spec = pl.BlockSpec((tr, cols), lambda i: (i, 0))
    shape = jax.ShapeDtypeStruct((rows, cols), F32)
    return pl.pallas_call(
        body, name=name, grid=(rows // tr,),
        in_specs=[spec] * 4, out_specs=[spec] * 3, out_shape=[shape] * 3,
        compiler_params=_params("parallel"),
    )(w, g, m, v)


def _adamw_nd(w, g, m, v, name):
    shape = w.shape
    two_d = (-1, shape[-1]) if w.ndim > 1 else (1, -1)
    outs = _adamw(w.reshape(two_d), g.reshape(two_d), m.reshape(two_d), v.reshape(two_d), name)
    return tuple(o.reshape(shape) for o in outs)


def _ride_alone(rider, name):
    def body(*refs):
        r_in, _, r_out, _, r_scratch = _split_refs(rider, 0, 0, refs)
        now = pl.program_id(0) == 0
        rider(now, None, None, r_in, r_out, r_scratch)
        rider(None, now, None, r_in, r_out, r_scratch)
        rider(None, None, now, r_in, r_out, r_scratch)

    return pl.pallas_call(
        body, name=name, grid=(1,),
        in_specs=list(rider.in_specs), out_specs=list(rider.out_specs), out_shape=list(rider.out_shape),
        scratch_shapes=list(rider.scratch_shapes),
        compiler_params=_params("arbitrary"),
    )(*rider.inputs)


def _sum_slots(parts, name):
    slots, rows, cols = parts.shape
    tr = _row_tile(rows, 256) if rows % 16 == 0 else rows

    def body(p_ref, o_ref):
        acc = p_ref[0]
        for s in range(1, slots):
            acc = acc + p_ref[s]
        o_ref[...] = acc

    return pl.pallas_call(
        body, name=name, grid=(rows // tr,),
        in_specs=[pl.BlockSpec((slots, tr, cols), lambda i: (0, i, 0))],
        out_specs=pl.BlockSpec((tr, cols), lambda i: (i, 0)),
        out_shape=jax.ShapeDtypeStruct((rows, cols), parts.dtype),
        compiler_params=_params("parallel"),
    )(parts)


def _pad_rows(flat, cols, multiple):
    n = flat.shape[0]
    rows = -(-n // cols)
    rows = -(-rows // multiple) * multiple
    return jnp.pad(flat, (0, rows * cols - n)).reshape(rows, cols)


def _cols_to_chunks(full):
    lead = full.shape[:-1]
    c = full.shape[-1] // N_DEV
    x = full.reshape(-1, N_DEV, c)
    return jnp.transpose(x, (1, 0, 2)).reshape(N_DEV, -1)


def _chunks_to_cols(chunks, lead):
    n = 1
    for s in lead:
        n *= s
    c = chunks.shape[1] // n
    x = chunks.reshape(N_DEV, n, c)
    return jnp.transpose(x, (1, 0, 2)).reshape(tuple(lead) + (N_DEV * c,))


def kernel(x, meta_tokens, norm_mix_g, norm_ffn_g, final_norm_g, sc_w_in, sc_conv_w, sc_w_out, rg_w_in, rg_conv_w, rg_conv_b, rg_w_gate_a, rg_b_gate_a, rg_w_gate_x, rg_b_gate_x, rg_lambda, rg_w_out, ffn_w_up, ffn_conv_w, ffn_w_down, loss_target, m_meta_tokens, m_norm_mix_g, m_norm_ffn_g, m_final_norm_g, m_sc_w_in, m_sc_conv_w, m_sc_w_out, m_rg_w_in, m_rg_conv_w, m_rg_conv_b, m_rg_w_gate_a, m_rg_b_gate_a, m_rg_w_gate_x, m_rg_b_gate_x, m_rg_lambda, m_rg_w_out, m_ffn_w_up, m_ffn_conv_w, m_ffn_w_down, v_meta_tokens, v_norm_mix_g, v_norm_ffn_g, v_final_norm_g, v_sc_w_in, v_sc_conv_w, v_sc_w_out, v_rg_w_in, v_rg_conv_w, v_rg_conv_b, v_rg_w_gate_a, v_rg_b_gate_a, v_rg_w_gate_x, v_rg_b_gate_x, v_rg_lambda, v_rg_w_out, v_ffn_w_up, v_ffn_conv_w, v_ffn_w_down):
    weights = dict(meta_tokens=meta_tokens, norm_mix_g=norm_mix_g, norm_ffn_g=norm_ffn_g, final_norm_g=final_norm_g,
                   sc_w_in=sc_w_in, sc_conv_w=sc_conv_w, sc_w_out=sc_w_out, rg_w_in=rg_w_in, rg_conv_w=rg_conv_w,
                   rg_conv_b=rg_conv_b, rg_w_gate_a=rg_w_gate_a, rg_b_gate_a=rg_b_gate_a, rg_w_gate_x=rg_w_gate_x,
                   rg_b_gate_x=rg_b_gate_x, rg_lambda=rg_lambda, rg_w_out=rg_w_out, ffn_w_up=ffn_w_up,
                   ffn_conv_w=ffn_conv_w, ffn_w_down=ffn_w_down)
    m_in = dict(meta_tokens=m_meta_tokens, norm_mix_g=m_norm_mix_g, norm_ffn_g=m_norm_ffn_g, final_norm_g=m_final_norm_g,
                sc_w_in=m_sc_w_in, sc_conv_w=m_sc_conv_w, sc_w_out=m_sc_w_out, rg_w_in=m_rg_w_in, rg_conv_w=m_rg_conv_w,
                rg_conv_b=m_rg_conv_b, rg_w_gate_a=m_rg_w_gate_a, rg_b_gate_a=m_rg_b_gate_a, rg_w_gate_x=m_rg_w_gate_x,
                rg_b_gate_x=m_rg_b_gate_x, rg_lambda=m_rg_lambda, rg_w_out=m_rg_w_out, ffn_w_up=m_ffn_w_up,
                ffn_conv_w=m_ffn_conv_w, ffn_w_down=m_ffn_w_down)
    v_in = dict(meta_tokens=v_meta_tokens, norm_mix_g=v_norm_mix_g, norm_ffn_g=v_norm_ffn_g, final_norm_g=v_final_norm_g,
                sc_w_in=v_sc_w_in, sc_conv_w=v_sc_conv_w, sc_w_out=v_sc_w_out, rg_w_in=v_rg_w_in, rg_conv_w=v_rg_conv_w,
                rg_conv_b=v_rg_conv_b, rg_w_gate_a=v_rg_w_gate_a, rg_b_gate_a=v_rg_b_gate_a, rg_w_gate_x=v_rg_w_gate_x,
                rg_b_gate_x=v_rg_b_gate_x, rg_lambda=v_rg_lambda, rg_w_out=v_rg_w_out, ffn_w_up=v_ffn_w_up,
                ffn_conv_w=v_ffn_conv_w, ffn_w_down=v_ffn_w_down)
    names = list(weights)

    seq, d = x.shape[1], x.shape[2]
    n_meta = meta_tokens.shape[0]
    n_ffn = ffn_w_up.shape[0]

    assert n_ffn == 2
    def shard_rows(w_in, w_out):
        return [w_in.T.astype(MXU_DTYPE), w_out.astype(MXU_DTYPE)]

    def both_orientations(in_t_gathered, out_gathered):
        w_in_t = in_t_gathered.reshape(-1, d)
        w_out = out_gathered.reshape(-1, d)
        return w_in_t.T, w_in_t, w_out, w_out.T

    shards = {"ffn0": shard_rows(ffn_w_up[0], ffn_w_down[0]), "rg": shard_rows(rg_w_in[0], rg_w_out[0]),
              "ffn1": shard_rows(ffn_w_up[1], ffn_w_down[1])}
    ffn_w_up_f, ffn_w_up_t, ffn_w_down_f, ffn_w_down_t = [None] * 2, [None] * 2, [None] * 2, [None] * 2

    small_names = ["meta_tokens", "sc_conv_w", "rg_conv_w", "rg_conv_b", "rg_b_gate_a", "rg_b_gate_x", "rg_lambda",
                   "ffn_conv_w"]
    small_lead = {n: weights[n].shape[:-1] for n in small_names}
    small_sizes = [weights[n].size for n in small_names]
    small_flat = jnp.concatenate([weights[n].reshape(-1) for n in small_names])
    small_rows = _pad_rows(small_flat, d, SUBLANES)
    sc_in_g, sc_out_g, small_g = _ride_alone(
        _GatherRider(shard_rows(sc_w_in[0], sc_w_out[0]) + [small_rows]), "gather_first")
    sc_w_in_f, sc_w_in_t, sc_w_out_f, sc_w_out_t = both_orientations(sc_in_g, sc_out_g)
    small_g = small_g.reshape(N_DEV, -1)
    small_full = {}
    o = 0
    for n, sz in zip(small_names, small_sizes):
        small_full[n] = _chunks_to_cols(small_g[:, o:o + sz], small_lead[n])
        o += sz

    wg, wgt = _pair_gate_weights(rg_w_gate_a[0].astype(MXU_DTYPE), rg_w_gate_x[0].astype(MXU_DTYPE))
    rg_cw, rg_cb = small_full["rg_conv_w"][0], small_full["rg_conv_b"]
    rg_ba, rg_bx, rg_lam = small_full["rg_b_gate_a"], small_full["rg_b_gate_x"], small_full["rg_lambda"]
    sc_cw = small_full["sc_conv_w"][0]
    ffn_cw = small_full["ffn_conv_w"]

    tp = _row_tile(n_meta + seq, ROW_TILE_PERM)

    def ffn_fwd(h, l, rider):
        n, a, z, *gathered = _ffn_up_fused(h, norm_ffn_g[l:l + 1], ffn_w_up_f[l], ffn_cw[l], tp, f"ffn{l}_up", rider)
        return _matmul_residual(z, ffn_w_down_f[l], h, f"ffn{l}_down"), (n, a, z), gathered

    h0, n0, p0, q0, *ffn0_w = _sc_in_fused(x[0], small_full["meta_tokens"], norm_mix_g[0:1], sc_w_in_f, sc_cw, tp, "sc_in",
                                           _GatherRider(shards["ffn0"]))
    ffn_w_up_f[0], ffn_w_up_t[0], ffn_w_down_f[0], ffn_w_down_t[0] = both_orientations(*ffn0_w)
    h1 = _matmul_residual(q0, sc_w_out_f, h0, "sc_out")
    h2, ffn0_saved, rg_w = ffn_fwd(h1, 0, _GatherRider(shards["rg"]))
    rg_w_in_f, rg_w_in_t, rg_w_out_f, rg_w_out_t = both_orientations(*rg_w)
    n2, p2, y2, hs2, *rest2 = _rg_in_fused(h2, norm_mix_g[1:2], rg_w_in_f, rg_cw, rg_cb, wg, rg_ba, rg_bx, rg_lam,
                                           tp, "rg_in", _GatherRider(shards["ffn1"]))
    rg_gates, ffn1_w = rest2[:5], rest2[5:]
    ffn_w_up_f[1], ffn_w_up_t[1], ffn_w_down_f[1], ffn_w_down_t[1] = both_orientations(*ffn1_w)
    h3 = _matmul_residual(y2, rg_w_out_f, h2, "rg_out")
    ffn1_saved = tuple(_ffn_up_fused(h3, norm_ffn_g[1:2], ffn_w_up_f[1], ffn_cw[1], tp, "ffn1_up"))
    dh4, loss_tile, d_final_g = _down_loss_head(ffn1_saved[2], ffn_w_down_f[1], h3, final_norm_g.reshape(1, d),
                                                loss_target[0], n_meta, tp, "ffn1_down_loss_head")
    loss = lax.psum(loss_tile[0, 0], AXES)

    arrived = {}

    def ffn_bwd(dh_out, h_in, saved, l):
        n, a, z = saved
        da, dhb, d_cw = _ffn_down_bwd_fused(dh_out, ffn_w_down_t[l], a, ffn_cw[l], tp, f"ffn{l}_down_bwd")
        d_w_down = _wgrad(z, dhb, f"ffn{l}_down_wgrad")
        d_w_up_t, arrived[f"ffn_w_down{l}"] = _wgrad(da, n, f"ffn{l}_up_wgrad", _ScatterRider(d_w_down))
        rows = d_w_up_t.shape[0] // N_DEV
        first_rows = (rows // 2) // (2 * SUBLANES) * (2 * SUBLANES)
        dh_in, d_g, arrived[f"ffn_w_up{l}a"] = _dgrad_in_norm(
            da, ffn_w_up_t[l], h_in, norm_ffn_g[l:l + 1], dh_out, f"ffn{l}_up_dgrad",
            _ScatterRider(d_w_up_t, (0, first_rows)))
        return dh_in, d_cw, d_g, _ScatterRider(d_w_up_t, (first_rows, rows - first_rows))

    dh3, d_fcw1, d_fg1, up1_rest = ffn_bwd(dh4, h3, ffn1_saved, 1)

    dp2, dhb3, d_rg_cw, d_rg_cb, d_wg, d_rg_ba, d_rg_bx, d_rg_lam, arrived["ffn_w_up1b"] = _rg_out_bwd_fused(
        dh3, rg_w_out_t, p2, hs2, rg_gates, rg_cw, wg, wgt, rg_lam, tp, "rg_out_bwd", up1_rest)
    d_wa, d_wx = _unpair_gate_grads(d_wg, rg_w_gate_a.shape[2])
    d_rg_w_out = _wgrad(y2, dhb3, "rg_out_wgrad")
    d_rg_w_in_t, arrived["rg_w_out"] = _wgrad(dp2, n2, "rg_in_wgrad", _ScatterRider(d_rg_w_out))
    dh2, d_mg1, arrived["rg_w_in"] = _dgrad_in_norm(
        dp2, rg_w_in_t, h2, norm_mix_g[1:2], dh3, "rg_in_dgrad", _ScatterRider(d_rg_w_in_t))

    dh1, d_fcw0, d_fg0, up0_rest = ffn_bwd(dh2, h1, ffn0_saved, 0)

    dp0, dhb1, d_sc_cw, arrived["ffn_w_up0b"] = _sc_out_bwd_fused(dh1, sc_w_out_t, p0, sc_cw, tp, "sc_out_bwd", up0_rest)
    d_sc_w_out = _wgrad(q0, dhb1, "sc_out_wgrad")
    d_sc_w_in_t, arrived["sc_w_out"] = _wgrad(dp0, n0, "sc_in_wgrad", _ScatterRider(d_sc_w_out))
    d_mg0, d_meta, d_x, arrived["sc_w_in"] = _dgrad_in_to_input(
        dp0, sc_w_in_t, h0, norm_mix_g[0:1], dh1, n_meta, tp, "sc_in_dgrad", _ScatterRider(d_sc_w_in_t))
    grad_x = d_x[None]

    grads = {}
    for n in ("sc_w_in", "rg_w_in"):
        grads[n] = _sum_slots(arrived[n], f"sum_{n}").T[None]
    for n in ("sc_w_out", "rg_w_out"):
        grads[n] = _sum_slots(arrived[n], f"sum_{n}")[None]
    grads["ffn_w_up"] = jnp.stack([jnp.concatenate(
        [_sum_slots(arrived[f"ffn_w_up{l}{part}"], f"sum_ffn_w_up{l}{part}") for part in "ab"], axis=0).T
        for l in range(n_ffn)])
    grads["ffn_w_down"] = jnp.stack([_sum_slots(arrived[f"ffn_w_down{l}"], f"sum_ffn_w_down{l}") for l in range(n_ffn)])

    small_grads = {"meta_tokens": d_meta,"sc_conv_w": d_sc_cw[None], "rg_conv_w": d_rg_cw[None],
                   "rg_conv_b": d_rg_cb, "rg_b_gate_a": d_rg_ba, "rg_b_gate_x": d_rg_bx, "rg_lambda": d_rg_lam,
                   "ffn_conv_w": jnp.stack([d_fcw0, d_fcw1])}
    small_chunks = jnp.concatenate([_cols_to_chunks(small_grads[n]) for n in small_names], axis=1)
    pad = small_rows.shape[0] * d - small_chunks.shape[1]
    small_chunks = jnp.pad(small_chunks, ((0, 0), (0, pad))).reshape(N_DEV, small_rows.shape[0], d)
    rep_names = ["norm_mix_g", "norm_ffn_g", "final_norm_g", "rg_w_gate_a", "rg_w_gate_x"]
    rep_grads = {"norm_mix_g": jnp.concatenate([d_mg0, d_mg1], axis=0),
                 "norm_ffn_g": jnp.concatenate([d_fg0, d_fg1], axis=0),
                 "final_norm_g": d_final_g.reshape(-1), "rg_w_gate_a": d_wa[None], "rg_w_gate_x": d_wx[None]}
    rep_flat = jnp.concatenate([rep_grads[n].reshape(-1) for n in rep_names])
    rep_chunk_rows = -(-rep_flat.shape[0] // (N_DEV * d))
    rep_chunk_rows += -(small_rows.shape[0] + rep_chunk_rows) % 16
    rep_chunks = jnp.pad(rep_flat, (0, N_DEV * rep_chunk_rows * d - rep_flat.shape[0])).reshape(N_DEV, rep_chunk_rows, d)

    last_chunks = jnp.concatenate([small_chunks, rep_chunks], axis=1)
    reduced = _sum_slots(_ride_alone(_ScatterRider(last_chunks.reshape(-1, d)), "scatter_last")[0], "sum_last")

    small_red = reduced[0:small_rows.shape[0]].reshape(-1)
    o = small_rows.shape[0]
    so = 0
    for n, sz in zip(small_names, small_sizes):
        grads[n] = small_red[so:so + sz].reshape(weights[n].shape)
        so += sz
    rep_red = _ride_alone(_GatherRider([reduced[o:o + rep_chunks.shape[1]]]), "gather_replicated_grads")[0].reshape(-1)
    ro = 0
    for n in rep_names:
        sz = weights[n].size
        grads[n] = rep_red[ro:ro + sz].reshape(weights[n].shape)
        ro += sz

    delta, new_m, new_v = {}, {}, {}
    for n in names:
        delta[n], new_m[n], new_v[n] = _adamw_nd(weights[n], grads[n], m_in[n], v_in[n], f"adamw_{n}")

    return (loss, grad_x, *[grads[n] for n in names], *[delta[n] for n in names],
            *[new_m[n] for n in names], *[new_v[n] for n in names])
```

```python
import jax
import jax.numpy as jnp
from jax import lax
from jax.experimental import pallas as pl
from jax.experimental.pallas import tpu as pltpu

F32 = jnp.float32
MXU_DTYPE = jnp.bfloat16
RMS_EPS = 1e-6
RG_C = 8.0
ADAM_LR = 0.001
ADAM_B1 = 0.9
ADAM_B2 = 0.999
ADAM_EPS = 1e-08
ADAM_WD = 0.01
ADAM_STEP = 10

N_DEV = 8
AXES = ("x", "y", "c")
SUBLANES = 8
LANES = 128
VMEM_LIMIT_BYTES = 48 * 1024 * 1024
ROW_TILE_MATMUL = 700
VMEM_LIMIT_WIDE_BYTES = 58 * 1024 * 1024
ROW_TILE_WGRAD = 3300
COL_TILE_WGRAD = 1408
ROW_TILE_PERM = 400
STRIP = 256
STRIP_FFN = 4096

_TN = (((0,), (0,)), ((), ()))
_NN = (((1,), (0,)), ((), ()))


def _row_tile(t, target):
    best = None
    for tm in range(16, t + 1, 16):
        if t % tm == 0 and tm <= target:
            best = tm
    return best if best is not None else t


def _col_tile(n, target):
    best = None
    for tn in range(LANES, n + 1, LANES):
        if n % tn == 0 and tn <= target:
            best = tn
    return best if best is not None else n


def _params(*sem):
    return pltpu.CompilerParams(dimension_semantics=sem, vmem_limit_bytes=VMEM_LIMIT_BYTES)


def _dot(a, b, dims):
    return lax.dot_general(a, b, dims, preferred_element_type=F32)


def _sigmoid(x):
    return 1.0 / (1.0 + jnp.exp(-x))


def _sigmoid_tanh(x):
    return 0.5 * jnp.tanh(0.5 * x) + 0.5


def _gelu(x):
    c = 0.7978845608028654
    t = jnp.tanh(c * (x + 0.044715 * (x * x * x)))
    return 0.5 * x * (1.0 + t)


def _gelu_and_grad(x):
    c = 0.7978845608028654
    x2 = x * x
    t = jnp.tanh(c * (x + 0.044715 * (x2 * x)))
    half = 0.5 * (1.0 + t)
    return x * half, half + 0.5 * x * (1.0 - t * t) * c * (1.0 + 3.0 * 0.044715 * x2)


def _softplus(x):
    return jnp.maximum(x, 0.0) + jnp.log1p(jnp.exp(-jnp.abs(x)))


def _time_scratch(tm, d):
    return [pltpu.VMEM((2, tm, d), F32), pltpu.VMEM((d // LANES, tm, LANES), F32), pltpu.SemaphoreType.DMA((2,))]


def _fetch_time_tile(i, nt, src_ref, head, tbuf, sems, tm):
    n_head = head.shape[0]

    def tile(j, slot):
        start = pl.multiple_of(j * tm - n_head, SUBLANES)
        return pltpu.make_async_copy(src_ref.at[pl.ds(start, tm)], tbuf.at[slot], sems.at[slot])

    first = pltpu.make_async_copy(src_ref.at[pl.ds(0, tm - n_head)], tbuf.at[0, pl.ds(n_head, tm - n_head)], sems.at[0])

    @pl.when(i == 0)
    def _():
        tbuf[0, 0:n_head, :] = head
        first.start()

    @pl.when(i + 1 < nt)
    def _():
        tile(i + 1, (i + 1) % 2).start()

    @pl.when(i == 0)
    def _():
        first.wait()

    @pl.when(i > 0)
    def _():
        tile(i, i % 2).wait()


def _time_to_tile_order(t_ref, slabs, out_ref):
    tm, d = out_ref.shape
    groups = tm // SUBLANES
    for k in range(d // LANES):
        slabs[k] = t_ref[:, k * LANES:(k + 1) * LANES]
    for k in range(d // LANES):
        for g in range(groups):
            out_ref[g * SUBLANES:(g + 1) * SUBLANES, k * LANES:(k + 1) * LANES] = slabs[k, pl.ds(g, SUBLANES, stride=groups), :]


def _tile_to_time_order(p_ref, slabs, t_ref):
    tm, d = p_ref.shape
    groups = tm // SUBLANES
    for k in range(d // LANES):
        for g in range(groups):
            slabs[k, pl.ds(g, SUBLANES, stride=groups), :] = p_ref[g * SUBLANES:(g + 1) * SUBLANES, k * LANES:(k + 1) * LANES]
    for k in range(d // LANES):
        t_ref[:, k * LANES:(k + 1) * LANES] = slabs[k]


def _rows(ref, sl):
    return [ref[k:k + 1, sl] for k in range(ref.shape[0])]


def _shift_down(x, before, s):
    if s == 0:
        return x
    n = x.shape[0]
    row = lax.broadcasted_iota(jnp.int32, (SUBLANES, x.shape[1]), 0)
    heads = []
    for g in range(s):
        v = x[n - (s - g) * SUBLANES:n - (s - g - 1) * SUBLANES]
        heads.append(pltpu.roll(jnp.where(row == SUBLANES - 1, before[s - g - 1], v), 1, axis=0))
    return jnp.concatenate(heads + [x[0:n - s * SUBLANES]], axis=0)


def _shift_up(x, after, s):
    if s == 0:
        return x
    row = lax.broadcasted_iota(jnp.int32, (SUBLANES, x.shape[1]), 0)
    tails = []
    for m in range(s):
        v = x[m * SUBLANES:(m + 1) * SUBLANES]
        tails.append(pltpu.roll(jnp.where(row == 0, after[m], v), SUBLANES - 1, axis=0))
    return jnp.concatenate([x[s * SUBLANES:]] + tails, axis=0)


def _weighted(w, windows):
    y = w[0] * windows[0]
    for k in range(1, len(w)):
        y = y + w[k] * windows[k]
    return y


def _windows(x, before, k_width):
    return [_shift_down(x, before, k_width - 1 - k) for k in range(k_width)]


def _conv_t(dy, after, w):
    k_width = len(w)
    return _weighted(w, [_shift_up(dy, after, k_width - 1 - k) for k in range(k_width)])


def _blocks(ref, sl, count, newest_first):
    n = ref.shape[0] // SUBLANES
    order = range(n - 1, n - 1 - count, -1) if newest_first else range(count)
    return [ref[b * SUBLANES:(b + 1) * SUBLANES, sl] for b in order]


def _halo_specs(tm, cols, count, tile_of):
    def spec(k):
        return pl.BlockSpec((SUBLANES, cols), lambda i: (jnp.maximum(tile_of(i) * (tm // SUBLANES) - k, 0), 0))
    return [spec(k) for k in range(1, count + 1)]


def _scan_tile(coef, val, out, carry, groups, reverse):
    cols = coef.shape[1]
    row = lax.broadcasted_iota(jnp.int32, (SUBLANES, cols), 0)

    def blk(i):
        g = groups - 1 - i if reverse else i
        return pl.ds(pl.multiple_of(g * SUBLANES, SUBLANES), SUBLANES)

    def local(i, pl_):
        p_prev, l_prev = pl_
        a = coef[blk(i), :]
        p = a * p_prev
        l = a * l_prev + val[blk(i), :]
        coef[blk(i), :] = p
        val[blk(i), :] = l
        return p, l

    pf, lf = lax.fori_loop(0, groups, local, (jnp.ones((SUBLANES, cols), F32), jnp.zeros((SUBLANES, cols), F32)))
    for s in (1, 2, 4):
        keep, sh = (row < SUBLANES - s, SUBLANES - s) if reverse else (row >= s, s)
        p_s = jnp.where(keep, pltpu.roll(pf, sh, axis=0), 1.0)
        l_s = jnp.where(keep, pltpu.roll(lf, sh, axis=0), 0.0)
        lf = pf * l_s + lf
        pf = pf * p_s
    end = lf + pf * carry
    if reverse:
        init = jnp.where(row == SUBLANES - 1, carry, pltpu.roll(end, SUBLANES - 1, axis=0))
        leaving = jnp.broadcast_to(end[0:1, :], (SUBLANES, cols))
    else:
        init = jnp.where(row == 0, carry, pltpu.roll(end, 1, axis=0))
        leaving = jnp.broadcast_to(end[SUBLANES - 1:SUBLANES, :], (SUBLANES, cols))

    def fix(i, _):
        out[blk(i), :] = val[blk(i), :] + coef[blk(i), :] * init
        return 0

    lax.fori_loop(0, groups, fix, 0)
    return leaving


def _resident(shape):
    return pl.BlockSpec(shape, lambda *_: (0,) * len(shape), pipeline_mode=pl.Buffered(1))


def _full(shape):
    return pl.BlockSpec(shape, lambda *_: (0,) * len(shape))


def _rmsnorm_to(h_ref, g_ref, n_ref):
    x = h_ref[...]
    ms = jnp.mean(x * x, axis=-1, keepdims=True)
    n_ref[...] = (x * lax.rsqrt(ms + RMS_EPS) * g_ref[...]).astype(n_ref.dtype)


_ANY = pl.BlockSpec(memory_space=pl.ANY)
_MESH = pl.DeviceIdType.MESH


def _dma_sems(n):
    return pltpu.SemaphoreType.DMA((n,))


def _when_each(*phases):
    for cond, fn in phases:
        if cond is not None:
            pl.when(cond)(fn)


class _NoRider:
    inputs = in_specs = out_shape = out_specs = scratch_shapes = ()

    def __call__(self, first, middle, last, ins, outs, scratch):
        pass


_NO_RIDER = _NoRider()


def _hand_on_step(nt):
    return (3 * nt) // 4


def _split_refs(rider, n_out, n_scratch, rest):
    a = len(rider.inputs)
    b = a + n_out
    c = b + len(rider.out_shape)
    e = c + n_scratch
    return rest[:a], rest[a:b], rest[b:c], rest[c:e], rest[e:]


class _GatherRider:
    def __init__(self, blocks):
        n = len(blocks)
        self.inputs = tuple(blocks)
        self.in_specs = (_ANY,) * n
        self.out_shape = tuple(jax.ShapeDtypeStruct((N_DEV,) + b.shape, b.dtype) for b in blocks)
        self.out_specs = (_ANY,) * n
        self.scratch_shapes = (_dma_sems(7 * n), _dma_sems(7 * n), _dma_sems(n))

    def __call__(self, first, middle, last, ins, outs, scratch):
        n = len(ins)
        send_sems, recv_sems, local_sems = scratch
        x, y, c = lax.axis_index("x"), lax.axis_index("y"), lax.axis_index("c")
        me, sibling = (x, y, c), (x, y, 1 - c)
        chips = [(1 - x, y), (x, 1 - y), (1 - x, 1 - y)]

        def slot(b, px, py, pc):
            return outs[b].at[4 * px + 2 * py + pc]

        def copy(k, b, block, to, src=None):
            return pltpu.make_async_remote_copy(
                src_ref=slot(b, *block) if src is None else src, dst_ref=slot(b, *block),
                send_sem=send_sems.at[k * n + b], recv_sem=recv_sems.at[k * n + b], device_id=to, device_id_type=_MESH)

        mine = [pltpu.make_async_copy(ins[b], slot(b, *me), local_sems.at[b]) for b in range(n)]
        own = [copy(0, b, me, sibling, src=ins[b]) for b in range(n)]
        own += [copy(1 + j, b, me, (*chip, c), src=ins[b]) for j, chip in enumerate(chips) for b in range(n)]
        passed = [[copy(4 + j, b, (*chip, c), sibling) for b in range(n)] for j, chip in enumerate(chips)]

        def at_first():
            for cp in mine + own:
                cp.start()

        def at_middle():
            for j, chip in enumerate(chips):
                for b in range(n):
                    copy(1 + j, b, (*chip, c), me).wait_recv()
                    passed[j][b].start()

        def at_last():
            for b in range(n):
                copy(0, b, sibling, me).wait_recv()
                for j, chip in enumerate(chips):
                    copy(4 + j, b, (*chip, 1 - c), me).wait_recv()
            for cp in own + [cp for group in passed for cp in group]:
                cp.wait_send()
            for cp in mine:
                cp.wait()

        _when_each((first, at_first), (middle, at_middle), (last, at_last))


class _ScatterRider:
    def __init__(self, grad, rows=None):
        r = grad.shape[0] // N_DEV
        chunks = grad.reshape(N_DEV, r, grad.shape[1])
        self.rows = rows if rows is not None else (0, r)
        self.inputs = (chunks,)
        self.in_specs = (_ANY,)
        self.out_shape = (jax.ShapeDtypeStruct((N_DEV, self.rows[1], grad.shape[1]), chunks.dtype),)
        self.out_specs = (_ANY,)
        self.scratch_shapes = (_dma_sems(N_DEV - 1), _dma_sems(N_DEV - 1), pltpu.SemaphoreType.DMA(()))

    def __call__(self, first, middle, last, ins, outs, scratch):
        (g_ref,), (r_ref,) = ins, outs
        send_sems, recv_sems, local_sem = scratch
        x, y, c = lax.axis_index("x"), lax.axis_index("y"), lax.axis_index("c")
        me = 4 * x + 2 * y + c
        part = pl.ds(*self.rows)
        mine = pltpu.make_async_copy(g_ref.at[me, part], r_ref.at[me], local_sem)
        copies = []
        for k in range(1, N_DEV):
            px, py, pc = (1 - x if k & 4 else x), (1 - y if k & 2 else y), (1 - c if k & 1 else c)
            copies.append(pltpu.make_async_remote_copy(
                src_ref=g_ref.at[4 * px + 2 * py + pc, part], dst_ref=r_ref.at[me],
                send_sem=send_sems.at[k - 1], recv_sem=recv_sems.at[k - 1],
                device_id=(px, py, pc), device_id_type=_MESH))

        def at_first():
            mine.start()
            for cp in copies:
                cp.start()

        def at_last():
            for cp in copies:
                cp.wait()
            mine.wait()

        _when_each((first, at_first), (last, at_last))


def _matmul_residual(q, w, h, name):
    t, k = q.shape
    d = w.shape[1]
    tm = _row_tile(t, ROW_TILE_MATMUL)

    def body(q_ref, w_ref, h_ref, o_ref):
        o_ref[...] = h_ref[...] + _dot(q_ref[...], w_ref[...], _NN)

    return pl.pallas_call(
        body, name=name, grid=(t // tm,),
        in_specs=[pl.BlockSpec((tm, k), lambda i: (i, 0)),
                  _resident((k, d)),
                  pl.BlockSpec((tm, d), lambda i: (i, 0))],
        out_specs=pl.BlockSpec((tm, d), lambda i: (i, 0)),
        out_shape=jax.ShapeDtypeStruct((t, d), F32),
        compiler_params=_params("parallel"),
    )(q, w, h)


def _dgrad_in_norm(dp, wt, h, g, dh_next, name, rider=_NO_RIDER):
    t, n = dp.shape
    d = wt.shape[1]
    tm = _row_tile(t, ROW_TILE_MATMUL)
    nt = t // tm

    def body(dp_ref, wt_ref, h_ref, g_ref, dhn_ref, *rest):
        r_in, (dh_ref, dg_ref), r_out, _, r_scratch = _split_refs(rider, 2, 0, rest)
        i = pl.program_id(0)
        rider(i == 0, i == nt // 2, None, r_in, r_out, r_scratch)
        dn = _dot(dp_ref[...], wt_ref[...], _NN)
        x = h_ref[...]
        ms = jnp.mean(x * x, axis=-1, keepdims=True)
        r = lax.rsqrt(ms + RMS_EPS)
        xhat = x * r
        dng = dn * g_ref[...]
        c = jnp.mean(dng * xhat, axis=-1, keepdims=True)
        dh_ref[...] = dhn_ref[...] + r * (dng - xhat * c)
        part = jnp.sum(dn * xhat, axis=0, keepdims=True)

        @pl.when(i == 0)
        def _():
            dg_ref[...] = part

        @pl.when(i > 0)
        def _():
            dg_ref[...] += part

        rider(None, None, i == nt - 1, r_in, r_out, r_scratch)

    return pl.pallas_call(
        body, name=name, grid=(nt,),
        in_specs=[pl.BlockSpec((tm, n), lambda i: (i, 0)),
                  _resident((n, d)),
                  pl.BlockSpec((tm, d), lambda i: (i, 0)),
                  pl.BlockSpec((1, d), lambda i: (0, 0)),
                  pl.BlockSpec((tm, d), lambda i: (i, 0))] + list(rider.in_specs),
        out_specs=[pl.BlockSpec((tm, d), lambda i: (i, 0)),
                   pl.BlockSpec((1, d), lambda i: (0, 0))] + list(rider.out_specs),
        out_shape=[jax.ShapeDtypeStruct((t, d), F32), jax.ShapeDtypeStruct((1, d), F32)] + list(rider.out_shape),
        scratch_shapes=list(rider.scratch_shapes),
        compiler_params=_params("arbitrary"),
    )(dp, wt, h, g, dh_next, *rider.inputs)


def _dgrad_in_to_input(dp, wt, h, g, dh_next, n_head, tm, name, rider=_NO_RIDER):
    t, n = dp.shape
    d = wt.shape[1]
    nt = t // tm

    def body(dp_ref, wt_ref, h_ref, g_ref, dhn_ref, *rest):
        r_in, (dg_ref, head_ref, rest_ref), r_out, (pbuf, tout, slabs, sems), r_scratch = _split_refs(rider, 3, 4, rest)
        i = pl.program_id(0)
        rider(i == 0, i == nt // 2, None, r_in, r_out, r_scratch)
        dn = _dot(dp_ref[...], wt_ref[...], _NN)
        x = h_ref[...]
        ms = jnp.mean(x * x, axis=-1, keepdims=True)
        r = lax.rsqrt(ms + RMS_EPS)
        xhat = x * r
        dng = dn * g_ref[...]
        c = jnp.mean(dng * xhat, axis=-1, keepdims=True)
        pbuf[...] = dhn_ref[...] + r * (dng - xhat * c)
        part = jnp.sum(dn * xhat, axis=0, keepdims=True)

        @pl.when(i == 0)
        def _():
            dg_ref[...] = part

        @pl.when(i > 0)
        def _():
            dg_ref[...] += part

        def store(j):
            if isinstance(j, int) and j == 0:
                return pltpu.make_async_copy(tout.at[0, pl.ds(n_head, tm - n_head)],
                                             rest_ref.at[pl.ds(0, tm - n_head)], sems.at[0])
            start = j * tm - n_head
            start = start if isinstance(start, int) else pl.multiple_of(start, SUBLANES)
            return pltpu.make_async_copy(tout.at[j % 2], rest_ref.at[pl.ds(start, tm)], sems.at[j % 2])

        @pl.when(i == 2)
        def _():
            store(0).wait()

        @pl.when(i > 2)
        def _():
            store(i - 2).wait()

        _tile_to_time_order(pbuf, slabs, tout.at[i % 2])

        @pl.when(i == 0)
        def _():
            head_ref[...] = tout[0, 0:n_head, :]
            store(0).start()

        @pl.when(i > 0)
        def _():
            store(i).start()

        @pl.when(i == nt - 1)
        def _():
            for j in (nt - 2, nt - 1):
                if j >= 0:
                    store(j).wait()

        rider(None, None, i == nt - 1, r_in, r_out, r_scratch)

    return pl.pallas_call(
        body, name=name, grid=(nt,),
        in_specs=[pl.BlockSpec((tm, n), lambda i: (i, 0)),
                  _resident((n, d)),
                  pl.BlockSpec((tm, d), lambda i: (i, 0)),
                  pl.BlockSpec((1, d), lambda i: (0, 0)),
                  pl.BlockSpec((tm, d), lambda i: (i, 0))] + list(rider.in_specs),
        out_specs=[pl.BlockSpec((1, d), lambda i: (0, 0)), _full((n_head, d)), _ANY] + list(rider.out_specs),
        out_shape=[jax.ShapeDtypeStruct((1, d), F32), jax.ShapeDtypeStruct((n_head, d), F32),
                   jax.ShapeDtypeStruct((t - n_head, d), F32)] + list(rider.out_shape),
        scratch_shapes=[pltpu.VMEM((tm, d), F32)] + _time_scratch(tm, d) + list(rider.scratch_shapes),
        compiler_params=_params("arbitrary"),
    )(dp, wt, h, g, dh_next, *rider.inputs)


def _wgrad(a, b, name, rider=_NO_RIDER):
    t, m = a.shape
    d = b.shape[1]
    tmm, tk = _col_tile(m, COL_TILE_WGRAD), _row_tile(t, ROW_TILE_WGRAD)
    nm, nk = m // tmm, t // tk

    def body(a_ref, b_ref, *rest):
        r_in, (o_ref,), r_out, _, r_scratch = _split_refs(rider, 1, 0, rest)
        i, k = pl.program_id(0), pl.program_id(1)
        rider(jnp.logical_and(i == 0, k == 0), None, None, r_in, r_out, r_scratch)

        @pl.when(k == 0)
        def _():
            o_ref[...] = jnp.zeros_like(o_ref)

        o_ref[...] += _dot(a_ref[...], b_ref[...], _TN)
        rider(None, None, jnp.logical_and(i == nm - 1, k == nk - 1), r_in, r_out, r_scratch)

    outs = pl.pallas_call(
        body, name=name, grid=(nm, nk),
        in_specs=[pl.BlockSpec((tk, tmm), lambda i, k: (k, i)),
                  pl.BlockSpec((tk, d), lambda i, k: (k, 0))] + list(rider.in_specs),
        out_specs=[pl.BlockSpec((tmm, d), lambda i, k: (i, 0))] + list(rider.out_specs),
        out_shape=[jax.ShapeDtypeStruct((m, d), F32)] + list(rider.out_shape),
        scratch_shapes=list(rider.scratch_shapes),
        compiler_params=pltpu.CompilerParams(dimension_semantics=("arbitrary", "arbitrary"),
                                             vmem_limit_bytes=VMEM_LIMIT_WIDE_BYTES),
    )(a, b, *rider.inputs)
    return outs if rider.out_shape else outs[0]


def _ffn_up_fused(h, g, w, conv_w, tm, name, rider=_NO_RIDER):
    t, d = h.shape
    f2 = w.shape[1]
    f = f2 // 2
    cw = min(STRIP_FFN, f)
    kw = conv_w.shape[0]
    nh = kw - 1
    nt = t // tm

    def body(h_ref, g_ref, w_ref, cw_ref, *rest):
        r_in, (n_ref, a_ref, z_ref), r_out, (carry,), r_scratch = _split_refs(rider, 3, 1, rest)
        i = pl.program_id(0)
        rider(i == 0, i == _hand_on_step(nt), None, r_in, r_out, r_scratch)

        @pl.when(i == 0)
        def _():
            carry[...] = jnp.zeros_like(carry)

        _rmsnorm_to(h_ref, g_ref, n_ref)
        for c in range(0, f, cw):
            conv = []
            for sl in (slice(c, c + cw), slice(f + c, f + c + cw)):
                a = _dot(n_ref[...], w_ref[:, sl], _NN)
                a_ref[:, sl] = a
                conv.append(_weighted(_rows(cw_ref, sl), _windows(a, _blocks(carry, sl, nh, True), kw)))
                carry[:, sl] = a[tm - nh * SUBLANES:tm]
            gg, vv = conv
            z_ref[:, c:c + cw] = (gg * _sigmoid(gg) * vv).astype(z_ref.dtype)
        rider(None, None, i == nt - 1, r_in, r_out, r_scratch)

    return pl.pallas_call(
        body, name=name, grid=(nt,),
        in_specs=[pl.BlockSpec((tm, d), lambda i: (i, 0)), pl.BlockSpec((1, d), lambda i: (0, 0)),
                  _resident((d, f2)), _full(conv_w.shape)] + list(rider.in_specs),
        out_specs=[pl.BlockSpec((tm, d), lambda i: (i, 0)), pl.BlockSpec((tm, f2), lambda i: (i, 0)),
                   pl.BlockSpec((tm, f), lambda i: (i, 0))] + list(rider.out_specs),
        out_shape=[jax.ShapeDtypeStruct((t, d), MXU_DTYPE), jax.ShapeDtypeStruct((t, f2), F32),
                   jax.ShapeDtypeStruct((t, f), MXU_DTYPE)] + list(rider.out_shape),
        scratch_shapes=[pltpu.VMEM((nh * SUBLANES, f2), F32)] + list(rider.scratch_shapes),
        compiler_params=_params("arbitrary"),
    )(h, g, w, conv_w, *rider.inputs)


def _ffn_down_bwd_fused(dh, wdt, a, conv_w, tm, name):
    t, d = dh.shape
    f = wdt.shape[1]
    f2 = 2 * f
    nt = t // tm
    cw = min(STRIP_FFN, f)
    kw = conv_w.shape[0]
    nh = kw - 1
    rev = lambda i: nt - 1 - i

    def body(dh_ref, wdt_ref, a_ref, *rest):
        halo_refs = rest[:nh]
        cw_ref, da_ref, dhb_ref, dw_ref, carry = rest[nh:]
        i = pl.program_id(0)
        newest, oldest = i == 0, i == nt - 1

        @pl.when(newest)
        def _():
            carry[...] = jnp.zeros_like(carry)
            dw_ref[...] = jnp.zeros_like(dw_ref)

        dhb_ref[...] = dh_ref[...].astype(dhb_ref.dtype)
        for c in range(0, f, cw):
            gsl, vsl = slice(c, c + cw), slice(f + c, f + c + cw)
            dz = _dot(dhb_ref[...], wdt_ref[:, gsl], _NN)
            win, conv = {}, {}
            for sl in (gsl, vsl):
                before = [jnp.where(oldest, 0.0, r[:, sl]) for r in halo_refs]
                win[sl.start] = _windows(a_ref[:, sl], before, kw)
                conv[sl.start] = _weighted(_rows(cw_ref, sl), win[sl.start])
            gg, vv = conv[gsl.start], conv[vsl.start]
            s = _sigmoid(gg)
            grads = {gsl.start: dz * vv * s * (1.0 + gg * (1.0 - s)), vsl.start: dz * gg * s}
            for sl in (gsl, vsl):
                dconv = grads[sl.start]
                da_ref[:, sl] = _conv_t(dconv, _blocks(carry, sl, nh, False), _rows(cw_ref, sl)).astype(da_ref.dtype)
                for k in range(kw):
                    dw_ref[k:k + 1, sl] += jnp.sum(dconv * win[sl.start][k], axis=0, keepdims=True)
                carry[:, sl] = dconv[0:nh * SUBLANES]

    return pl.pallas_call(
        body, name=name, grid=(nt,),
        in_specs=[pl.BlockSpec((tm, d), lambda i: (rev(i), 0)), _resident((d, f)),
                  pl.BlockSpec((tm, f2), lambda i: (rev(i), 0))] + _halo_specs(tm, f2, nh, rev) + [_full(conv_w.shape)],
        out_specs=[pl.BlockSpec((tm, f2), lambda i: (rev(i), 0)), pl.BlockSpec((tm, d), lambda i: (rev(i), 0)),
                   _full(conv_w.shape)],
        out_shape=[jax.ShapeDtypeStruct((t, f2), MXU_DTYPE), jax.ShapeDtypeStruct((t, d), MXU_DTYPE),
                   jax.ShapeDtypeStruct(conv_w.shape, F32)],
        scratch_shapes=[pltpu.VMEM((nh * SUBLANES, f2), F32)],
        compiler_params=_params("arbitrary"),
    )(dh, wdt, a, *([a] * nh), conv_w)


def _sc_in_fused(x, meta, g, w, conv_w, w_out, tm, name, rider=_NO_RIDER):
    d = x.shape[1]
    t = x.shape[0] + meta.shape[0]
    cw = min(STRIP, d)
    kw = conv_w.shape[0]
    nh = kw - 1
    nt = t // tm

    def body(x_ref, meta_ref, g_ref, w_ref, cw_ref, wo_ref, *rest):
        r_in, (h_ref, n_ref, p_ref, q_ref, h1_ref), r_out, (carry, tbuf, slabs, sems), r_scratch = _split_refs(
            rider, 5, 4, rest)
        i = pl.program_id(0)
        rider(i == 0, i == _hand_on_step(nt), None, r_in, r_out, r_scratch)

        @pl.when(i == 0)
        def _():
            carry[...] = jnp.zeros_like(carry)

        _fetch_time_tile(i, nt, x_ref, meta_ref[...], tbuf, sems, tm)
        _time_to_tile_order(tbuf.at[i % 2], slabs, h_ref)
        _rmsnorm_to(h_ref, g_ref, n_ref)
        for c in range(0, d, cw):
            sl = slice(c, c + cw)
            parts = []
            for base in (0, d, 2 * d):
                psl = slice(base + c, base + c + cw)
                parts.append(_dot(n_ref[...], w_ref[:, psl], _NN))
                p_ref[:, psl] = parts[-1]
            bg, cg, v = parts
            cv = cg * v
            u = _weighted(_rows(cw_ref, sl), _windows(cv, _blocks(carry, sl, nh, True), kw))
            carry[:, sl] = cv[tm - nh * SUBLANES:tm]
            q_ref[:, sl] = (bg * u).astype(q_ref.dtype)
        h1_ref[...] = h_ref[...] + _dot(q_ref[...], wo_ref[...], _NN)
        rider(None, None, i == nt - 1, r_in, r_out, r_scratch)

    row_tile = lambda cols: pl.BlockSpec((tm, cols), lambda i: (i, 0))
    return pl.pallas_call(
        body, name=name, grid=(nt,),
        in_specs=[_ANY, _full(meta.shape), pl.BlockSpec((1, d), lambda i: (0, 0)),
                  _resident((d, 3 * d)), _full(conv_w.shape), _resident(w_out.shape)] + list(rider.in_specs),
        out_specs=[row_tile(d), row_tile(d), row_tile(3 * d), row_tile(d), row_tile(d)] + list(rider.out_specs),
        out_shape=[jax.ShapeDtypeStruct((t, d), F32), jax.ShapeDtypeStruct((t, d), MXU_DTYPE),
                   jax.ShapeDtypeStruct((t, 3 * d), F32), jax.ShapeDtypeStruct((t, d), MXU_DTYPE),
                   jax.ShapeDtypeStruct((t, d), F32)] + list(rider.out_shape),
        scratch_shapes=[pltpu.VMEM((nh * SUBLANES, d), F32)] + _time_scratch(tm, d) + list(rider.scratch_shapes),
        compiler_params=_params("arbitrary"),
    )(x, meta, g, w, conv_w, w_out, *rider.inputs)


def _sc_out_bwd_fused(dh, wot, p, conv_w, tm, name, rider=_NO_RIDER):
    t, d = dh.shape
    nt = t // tm
    cw = min(STRIP, d)
    kw = conv_w.shape[0]
    nh = kw - 1
    rev = lambda i: nt - 1 - i

    def body(dh_ref, wot_ref, p_ref, *rest):
        halo_refs, cw_ref = rest[:nh], rest[nh]
        r_in, (dp_ref, dhb_ref, dw_ref), r_out, (carry,), r_scratch = _split_refs(rider, 3, 1, rest[nh + 1:])
        i = pl.program_id(0)
        newest, oldest = i == 0, i == nt - 1
        rider(newest, i == nt // 2, None, r_in, r_out, r_scratch)

        @pl.when(newest)
        def _():
            carry[...] = jnp.zeros_like(carry)
            dw_ref[...] = jnp.zeros_like(dw_ref)

        dhb_ref[...] = dh_ref[...].astype(dhb_ref.dtype)
        for c in range(0, d, cw):
            sl, csl, vsl = slice(c, c + cw), slice(d + c, d + c + cw), slice(2 * d + c, 2 * d + c + cw)
            w = _rows(cw_ref, sl)
            dq = _dot(dhb_ref[...], wot_ref[:, sl], _NN)
            bg, cg, v = p_ref[:, sl], p_ref[:, csl], p_ref[:, vsl]
            before = [jnp.where(oldest, 0.0, r[:, csl] * r[:, vsl]) for r in halo_refs]
            win = _windows(cg * v, before, kw)
            du = dq * bg
            dcv = _conv_t(du, _blocks(carry, sl, nh, False), w)
            dp_ref[:, sl] = (dq * _weighted(w, win)).astype(dp_ref.dtype)
            dp_ref[:, csl] = (dcv * v).astype(dp_ref.dtype)
            dp_ref[:, vsl] = (dcv * cg).astype(dp_ref.dtype)
            for k in range(kw):
                dw_ref[k:k + 1, sl] += jnp.sum(du * win[k], axis=0, keepdims=True)
            carry[:, sl] = du[0:nh * SUBLANES]
        rider(None, None, oldest, r_in, r_out, r_scratch)

    tile = lambda cols: pl.BlockSpec((tm, cols), lambda i: (rev(i), 0))
    return pl.pallas_call(
        body, name=name, grid=(nt,),
        in_specs=[tile(d), _resident((d, d)), tile(3 * d)] + _halo_specs(tm, 3 * d, nh, rev) + [_full(conv_w.shape)]
                 + list(rider.in_specs),
        out_specs=[tile(3 * d), tile(d), _full(conv_w.shape)] + list(rider.out_specs),
        out_shape=[jax.ShapeDtypeStruct((t, 3 * d), MXU_DTYPE), jax.ShapeDtypeStruct((t, d), MXU_DTYPE),
                   jax.ShapeDtypeStruct(conv_w.shape, F32)] + list(rider.out_shape),
        scratch_shapes=[pltpu.VMEM((nh * SUBLANES, d), F32)] + list(rider.scratch_shapes),
        compiler_params=_params("arbitrary"),
    )(dh, wot, p, *([p] * nh), conv_w, *rider.inputs)


def _pair_gate_weights(wa, wx):
    nb, bd, _ = wa.shape
    zero = jnp.zeros((nb // 2, bd, bd), wa.dtype)

    def pair(w):
        w = w.reshape(nb // 2, 2, bd, bd)
        top = jnp.concatenate([w[:, 0], zero], axis=2)
        bottom = jnp.concatenate([zero, w[:, 1]], axis=2)
        return jnp.concatenate([top, bottom], axis=1)

    both = jnp.concatenate([pair(wa), pair(wx)], axis=2)
    return both, jnp.swapaxes(both, 1, 2)


def _unpair_gate_grads(dw, bd):
    def blocks(cols):
        d0 = dw[:, :bd, cols:cols + bd]
        d1 = dw[:, bd:, cols + bd:cols + 2 * bd]
        return jnp.stack([d0, d1], axis=1).reshape(-1, bd, bd)
    return blocks(0), blocks(2 * bd)


def _rg_gates(u, k, wg_ref, ba, bx, lam):
    ub = u.astype(MXU_DTYPE)
    z = _dot(ub, wg_ref[k], _NN)
    half = z.shape[1] // 2
    r = _sigmoid_tanh(z[:, :half] + ba)
    ig = _sigmoid_tanh(z[:, half:] + bx)
    sp = _softplus(-lam)
    la = -RG_C * r * sp
    a = jnp.exp(la)
    th = jnp.tanh(la)
    mult = jnp.sqrt(-2.0 * th / (1.0 - th))
    return r, ig, a, mult


def _rg_in_fused(h, g, w, conv_w, conv_b, wg, ba, bx, lam, w_out, tm, name, rider=_NO_RIDER):
    t, d = h.shape
    r2 = w.shape[1]
    rr = r2 // 2
    nb, bd = wg.shape[0], wg.shape[1]
    kw = conv_w.shape[0]
    nh = kw - 1
    groups = tm // SUBLANES
    nt = t // tm

    def body(h_ref, g_ref, w_ref, cw_ref, cb_ref, wg_ref, ba_ref, bx_ref, lam_ref, wo_ref, *rest):
        (r_in, (n_ref, p_ref, y_ref, hs_ref, u_ref, r_ref, ig_ref, a_ref, mult_ref, hn_ref), r_out,
         (a_scr, b_scr, carry_rb, carry_h), r_scratch) = _split_refs(rider, 10, 4, rest)
        i = pl.program_id(0)
        rider(i == 0, i == _hand_on_step(nt), None, r_in, r_out, r_scratch)

        @pl.when(i == 0)
        def _():
            carry_rb[...] = jnp.zeros_like(carry_rb)
            carry_h[...] = jnp.zeros_like(carry_h)

        _rmsnorm_to(h_ref, g_ref, n_ref)
        for k in range(nb):
            sl = slice(k * bd, (k + 1) * bd)
            rsl = slice(rr + k * bd, rr + (k + 1) * bd)
            p_ref[:, sl] = _dot(n_ref[...], w_ref[:, sl], _NN)
            rb = _dot(n_ref[...], w_ref[:, rsl], _NN)
            p_ref[:, rsl] = rb
            u = _weighted(_rows(cw_ref, sl), _windows(rb, _blocks(carry_rb, sl, nh, True), kw)) + cb_ref[:, sl]
            carry_rb[:, sl] = rb[tm - nh * SUBLANES:tm]
            r, ig, a, mult = _rg_gates(u, k, wg_ref, ba_ref[:, sl], bx_ref[:, sl], lam_ref[:, sl])
            for ref, val in ((u_ref, u), (r_ref, r), (ig_ref, ig), (a_ref, a), (mult_ref, mult)):
                ref[:, sl] = val
            a_scr[:, sl] = a
            b_scr[:, sl] = mult * (ig * u)

        carry_h[...] = _scan_tile(a_scr, b_scr, hs_ref, carry_h[...], groups, reverse=False)

        for k in range(nb):
            sl = slice(k * bd, (k + 1) * bd)
            y_ref[:, sl] = (hs_ref[:, sl] * _gelu(p_ref[:, sl])).astype(y_ref.dtype)
        hn_ref[...] = h_ref[...] + _dot(y_ref[...], wo_ref[...], _NN)
        rider(None, None, i == nt - 1, r_in, r_out, r_scratch)

    vm = lambda rows: pltpu.VMEM((rows, rr), F32)
    return pl.pallas_call(
        body, name=name, grid=(nt,),
        in_specs=[pl.BlockSpec((tm, d), lambda i: (i, 0)), pl.BlockSpec((1, d), lambda i: (0, 0)),
                  _resident((d, r2)), _full(conv_w.shape), _full(conv_b.shape), _full(wg.shape), _full(ba.shape),
                  _full(bx.shape), _full(lam.shape), _resident(w_out.shape)] + list(rider.in_specs),
        out_specs=[pl.BlockSpec((tm, d), lambda i: (i, 0)), pl.BlockSpec((tm, r2), lambda i: (i, 0))]
                  + [pl.BlockSpec((tm, rr), lambda i: (i, 0))] * 7 + [pl.BlockSpec((tm, d), lambda i: (i, 0))]
                  + list(rider.out_specs),
        out_shape=[jax.ShapeDtypeStruct((t, d), MXU_DTYPE), jax.ShapeDtypeStruct((t, r2), F32),
                   jax.ShapeDtypeStruct((t, rr), MXU_DTYPE)] + [jax.ShapeDtypeStruct((t, rr), F32)] * 6
                  + [jax.ShapeDtypeStruct((t, d), F32)] + list(rider.out_shape),
        scratch_shapes=[vm(tm), vm(tm), vm(nh * SUBLANES), vm(SUBLANES)] + list(rider.scratch_shapes),
        compiler_params=pltpu.CompilerParams(dimension_semantics=("arbitrary",), vmem_limit_bytes=VMEM_LIMIT_WIDE_BYTES),
    )(h, g, w, conv_w, conv_b, wg, ba, bx, lam, w_out, *rider.inputs)


def _rg_out_bwd_fused(dh, wot, p, hs, gates, conv_w, wg, wgt, lam, tm, name, rider=_NO_RIDER):
    t, d = dh.shape
    r2 = p.shape[1]
    rr = r2 // 2
    nb, bd = wg.shape[0], wg.shape[1]
    nt = t // tm
    kw = conv_w.shape[0]
    nh = kw - 1
    groups = tm // SUBLANES
    rev = lambda i: nt - 1 - i

    def body(dh_ref, wot_ref, p_ref, hs_ref, hsh_ref, u_ref, r_ref, ig_ref, a_ref, mult_ref, *rest):
        halo_refs = rest[:nh]
        cw_ref, wg_ref, wgt_ref, lam_ref = rest[nh:nh + 4]
        (r_in, (dp_ref, dhb_ref, dcw_ref, dcb_ref, dwg_ref, dba_ref, dbx_ref, dlam_ref), r_out,
         (as_scr, g_scr, carry_g, carry_du), r_scratch) = _split_refs(rider, 8, 4, rest[nh + 4:])
        i = pl.program_id(0)
        newest, oldest = i == 0, i == nt - 1
        rider(newest, i == nt // 2, None, r_in, r_out, r_scratch)

        @pl.when(newest)
        def _():
            carry_g[...] = jnp.zeros_like(carry_g)
            carry_du[...] = jnp.zeros_like(carry_du)
            for ref in (dcw_ref, dcb_ref, dwg_ref, dba_ref, dbx_ref, dlam_ref):
                ref[...] = jnp.zeros_like(ref)

        dhb_ref[...] = dh_ref[...].astype(dhb_ref.dtype)
        ones = jnp.ones((SUBLANES, bd), F32)

        def conv_windows(sl, rsl):
            before = [jnp.where(oldest, 0.0, r[:, rsl]) for r in halo_refs]
            return _windows(p_ref[:, rsl], before, kw)

        for k in range(nb):
            sl = slice(k * bd, (k + 1) * bd)
            rsl = slice(rr + k * bd, rr + (k + 1) * bd)
            dy = _dot(dhb_ref[...], wot_ref[:, sl], _NN)
            gate, dgate = _gelu_and_grad(p_ref[:, sl])
            g_scr[:, sl] = dy * gate
            dp_ref[:, sl] = (dy * hs_ref[:, sl] * dgate).astype(dp_ref.dtype)
            as_scr[:, sl] = _shift_up(a_ref[:, sl], [ones], 1)

        g_first = _scan_tile(as_scr, g_scr, g_scr, carry_g[...], groups, reverse=True)
        carry_g[...] = jnp.broadcast_to(a_ref[0:1, :], (SUBLANES, rr)) * g_first

        for k in range(nb):
            sl = slice(k * bd, (k + 1) * bd)
            rsl = slice(rr + k * bd, rr + (k + 1) * bd)
            cw = _rows(cw_ref, sl)
            lam_k = lam_ref[:, sl]
            sp = _softplus(-lam_k)
            g = g_scr[:, sl]
            a, u, r, ig, mult = a_ref[:, sl], u_ref[:, sl], r_ref[:, sl], ig_ref[:, sl], mult_ref[:, sl]
            da = g * _shift_down(hs_ref[:, sl], [jnp.where(oldest, 0.0, hsh_ref[:, sl])], 1)
            dmult = g * (ig * u)
            d_iu = g * mult
            dla = da * a - dmult * (a * a) / mult
            dr = dla * (-RG_C * sp)
            dsp = jnp.sum(dla * (-RG_C * r), axis=0, keepdims=True)
            dlam_ref[:, sl] += dsp * (-_sigmoid(-lam_k))
            dza = dr * r * (1.0 - r)
            dzx = (d_iu * u) * ig * (1.0 - ig)
            dba_ref[:, sl] += jnp.sum(dza, axis=0, keepdims=True)
            dbx_ref[:, sl] += jnp.sum(dzx, axis=0, keepdims=True)
            ub = u.astype(MXU_DTYPE)
            dz = jnp.concatenate([dza, dzx], axis=1).astype(MXU_DTYPE)
            dwg_ref[k] += _dot(ub, dz, _TN)
            du = d_iu * ig + _dot(dz, wgt_ref[k], _NN)
            dcb_ref[:, sl] += jnp.sum(du, axis=0, keepdims=True)
            win = conv_windows(sl, rsl)
            for kk in range(kw):
                dcw_ref[kk:kk + 1, sl] += jnp.sum(du * win[kk], axis=0, keepdims=True)
            dp_ref[:, rsl] = _conv_t(du, _blocks(carry_du, sl, nh, False), cw).astype(dp_ref.dtype)
            carry_du[:, sl] = du[0:nh * SUBLANES]
        rider(None, None, oldest, r_in, r_out, r_scratch)

    tile = lambda cols: pl.BlockSpec((tm, cols), lambda i: (rev(i), 0))
    vm = lambda rows: pltpu.VMEM((rows, rr), F32)
    grads = [conv_w.shape, lam.shape, wg.shape, lam.shape, lam.shape, lam.shape]
    return pl.pallas_call(
        body, name=name, grid=(nt,),
        in_specs=[tile(d), _resident((d, rr)), tile(r2), tile(rr)] + _halo_specs(tm, rr, 1, rev) + [tile(rr)] * 5
                 + _halo_specs(tm, r2, nh, rev)
                 + [_full(conv_w.shape), _full(wg.shape), _full(wgt.shape), _full(lam.shape)] + list(rider.in_specs),
        out_specs=[tile(r2), tile(d)] + [_full(s) for s in grads] + list(rider.out_specs),
        out_shape=[jax.ShapeDtypeStruct((t, r2), MXU_DTYPE), jax.ShapeDtypeStruct((t, d), MXU_DTYPE)]
                  + [jax.ShapeDtypeStruct(s, F32) for s in grads] + list(rider.out_shape),
        scratch_shapes=[vm(tm)] * 2 + [vm(SUBLANES), vm(nh * SUBLANES)] + list(rider.scratch_shapes),
        compiler_params=pltpu.CompilerParams(dimension_semantics=("arbitrary",), vmem_limit_bytes=VMEM_LIMIT_WIDE_BYTES),
    )(dh, wot, p, hs, hs, *gates, *([p] * nh), conv_w, wg, wgt, lam, *rider.inputs)


def _down_loss_head(z, w, h, g, target, n_meta, tm, name):
    t, d = h.shape
    k = z.shape[1]
    groups = tm // SUBLANES
    nt = t // tm

    def body(z_ref, w_ref, h_ref, g_ref, t_ref, dh_ref, loss_ref, dg_ref, tgt, tbuf, slabs, sems):
        i = pl.program_id(0)
        _fetch_time_tile(i, nt, t_ref, jnp.zeros((n_meta, d), F32), tbuf, sems, tm)
        _time_to_tile_order(tbuf.at[i % 2], slabs, tgt)
        x = h_ref[...] + _dot(z_ref[...], w_ref[...], _NN)
        ms = jnp.mean(x * x, axis=-1, keepdims=True)
        r = lax.rsqrt(ms + RMS_EPS)
        xhat = x * r
        gg = g_ref[...]
        row = lax.broadcasted_iota(jnp.int32, (tm, 1), 0)
        time = i * tm + jnp.right_shift(row, 3) + jnp.bitwise_and(row, SUBLANES - 1) * groups
        err = jnp.where(time >= n_meta, xhat * gg - tgt[...], 0.0)
        dout = err * (1.0 / d)
        dng = dout * gg
        c = jnp.mean(dng * xhat, axis=-1, keepdims=True)
        dh_ref[...] = r * (dng - xhat * c)
        part_dg = jnp.sum(dout * xhat, axis=0, keepdims=True)
        part_loss = jnp.broadcast_to(0.5 * jnp.sum(err * dout, keepdims=True), loss_ref.shape)

        @pl.when(i == 0)
        def _():
            dg_ref[...] = part_dg
            loss_ref[...] = part_loss

        @pl.when(i > 0)
        def _():
            dg_ref[...] += part_dg
            loss_ref[...] += part_loss

    return pl.pallas_call(
        body, name=name, grid=(nt,),
        in_specs=[pl.BlockSpec((tm, k), lambda i: (i, 0)),
                  _resident((k, d)),
                  pl.BlockSpec((tm, d), lambda i: (i, 0)),
                  pl.BlockSpec((1, d), lambda i: (0, 0)),
                  _ANY],
        out_specs=[pl.BlockSpec((tm, d), lambda i: (i, 0)),
                   pl.BlockSpec((SUBLANES, LANES), lambda i: (0, 0)),
                   pl.BlockSpec((1, d), lambda i: (0, 0))],
        out_shape=[jax.ShapeDtypeStruct((t, d), F32), jax.ShapeDtypeStruct((SUBLANES, LANES), F32),
                   jax.ShapeDtypeStruct((1, d), F32)],
        scratch_shapes=[pltpu.VMEM((tm, d), F32)] + _time_scratch(tm, d),
        compiler_params=_params("arbitrary"),
    )(z, w, h, g, target)


def _adamw(w, g, m, v, name):
    rows, cols = w.shape
    tr = rows
    if rows > 512:
        for cand in range(8, 513, 8):
            if rows % cand == 0:
                tr = cand

    def body(w_ref, g_ref, m_ref, v_ref, d_ref, nm_ref, nv_ref):
        g_ = g_ref[...]
        m_ = ADAM_B1 * m_ref[...] + (1.0 - ADAM_B1) * g_
        v_ = ADAM_B2 * v_ref[...] + (1.0 - ADAM_B2) * (g_ * g_)
        m_hat = m_ / (1.0 - ADAM_B1 ** ADAM_STEP)
        v_hat = v_ / (1.0 - ADAM_B2 ** ADAM_STEP)
        d_ref[...] = -ADAM_LR * (m_hat / (jnp.sqrt(v_hat) + ADAM_EPS) + ADAM_WD * w_ref[...])
        nm_ref[...] = m_
        nv_ref[...] = v_

    spec = pl.BlockSpec((tr, cols), lambda i: (i, 0))
    shape = jax.ShapeDtypeStruct((rows, cols), F32)
    return pl.pallas_call(
        body, name=name, grid=(rows // tr,),
        in_specs=[spec] * 4, out_specs=[spec] * 3, out_shape=[shape] * 3,
        compiler_params=_params("parallel"),
    )(w, g, m, v)


def _adamw_nd(w, g, m, v, name):
    shape = w.shape
    two_d = (-1, shape[-1]) if w.ndim > 1 else (1, -1)
    outs = _adamw(w.reshape(two_d), g.reshape(two_d), m.reshape(two_d), v.reshape(two_d), name)
    return tuple(o.reshape(shape) for o in outs)


def _ride_alone(rider, name):
    def body(*refs):
        r_in, _, r_out, _, r_scratch = _split_refs(rider, 0, 0, refs)
        now = pl.program_id(0) == 0
        rider(now, None, None, r_in, r_out, r_scratch)
        rider(None, now, None, r_in, r_out, r_scratch)
        rider(None, None, now, r_in, r_out, r_scratch)

    return pl.pallas_call(
        body, name=name, grid=(1,),
        in_specs=list(rider.in_specs), out_specs=list(rider.out_specs), out_shape=list(rider.out_shape),
        scratch_shapes=list(rider.scratch_shapes),
        compiler_params=_params("arbitrary"),
    )(*rider.inputs)


def _sum_slots(parts, name):
    slots, rows, cols = parts.shape
    tr = _row_tile(rows, 256) if rows % 16 == 0 else rows

    def body(p_ref, o_ref):
        acc = p_ref[0]
        for s in range(1, slots):
            acc = acc + p_ref[s]
        o_ref[...] = acc

    return pl.pallas_call(
        body, name=name, grid=(rows // tr,),
        in_specs=[pl.BlockSpec((slots, tr, cols), lambda i: (0, i, 0))],
        out_specs=pl.BlockSpec((tr, cols), lambda i: (i, 0)),
        out_shape=jax.ShapeDtypeStruct((rows, cols), parts.dtype),
        compiler_params=_params("parallel"),
    )(parts)


def _pad_rows(flat, cols, multiple):
    n = flat.shape[0]
    rows = -(-n // cols)
    rows = -(-rows // multiple) * multiple
    return jnp.pad(flat, (0, rows * cols - n)).reshape(rows, cols)


def _cols_to_chunks(full):
    lead = full.shape[:-1]
    c = full.shape[-1] // N_DEV
    x = full.reshape(-1, N_DEV, c)
    return jnp.transpose(x, (1, 0, 2)).reshape(N_DEV, -1)


def _chunks_to_cols(chunks, lead):
    n = 1
    for s in lead:
        n *= s
    c = chunks.shape[1] // n
    x = chunks.reshape(N_DEV, n, c)
    return jnp.transpose(x, (1, 0, 2)).reshape(tuple(lead) + (N_DEV * c,))


def kernel(x, meta_tokens, norm_mix_g, norm_ffn_g, final_norm_g, sc_w_in, sc_conv_w, sc_w_out, rg_w_in, rg_conv_w, rg_conv_b, rg_w_gate_a, rg_b_gate_a, rg_w_gate_x, rg_b_gate_x, rg_lambda, rg_w_out, ffn_w_up, ffn_conv_w, ffn_w_down, loss_target, m_meta_tokens, m_norm_mix_g, m_norm_ffn_g, m_final_norm_g, m_sc_w_in, m_sc_conv_w, m_sc_w_out, m_rg_w_in, m_rg_conv_w, m_rg_conv_b, m_rg_w_gate_a, m_rg_b_gate_a, m_rg_w_gate_x, m_rg_b_gate_x, m_rg_lambda, m_rg_w_out, m_ffn_w_up, m_ffn_conv_w, m_ffn_w_down, v_meta_tokens, v_norm_mix_g, v_norm_ffn_g, v_final_norm_g, v_sc_w_in, v_sc_conv_w, v_sc_w_out, v_rg_w_in, v_rg_conv_w, v_rg_conv_b, v_rg_w_gate_a, v_rg_b_gate_a, v_rg_w_gate_x, v_rg_b_gate_x, v_rg_lambda, v_rg_w_out, v_ffn_w_up, v_ffn_conv_w, v_ffn_w_down):
    weights = dict(meta_tokens=meta_tokens, norm_mix_g=norm_mix_g, norm_ffn_g=norm_ffn_g, final_norm_g=final_norm_g,
                   sc_w_in=sc_w_in, sc_conv_w=sc_conv_w, sc_w_out=sc_w_out, rg_w_in=rg_w_in, rg_conv_w=rg_conv_w,
                   rg_conv_b=rg_conv_b, rg_w_gate_a=rg_w_gate_a, rg_b_gate_a=rg_b_gate_a, rg_w_gate_x=rg_w_gate_x,
                   rg_b_gate_x=rg_b_gate_x, rg_lambda=rg_lambda, rg_w_out=rg_w_out, ffn_w_up=ffn_w_up,
                   ffn_conv_w=ffn_conv_w, ffn_w_down=ffn_w_down)
    m_in = dict(meta_tokens=m_meta_tokens, norm_mix_g=m_norm_mix_g, norm_ffn_g=m_norm_ffn_g, final_norm_g=m_final_norm_g,
                sc_w_in=m_sc_w_in, sc_conv_w=m_sc_conv_w, sc_w_out=m_sc_w_out, rg_w_in=m_rg_w_in, rg_conv_w=m_rg_conv_w,
                rg_conv_b=m_rg_conv_b, rg_w_gate_a=m_rg_w_gate_a, rg_b_gate_a=m_rg_b_gate_a, rg_w_gate_x=m_rg_w_gate_x,
                rg_b_gate_x=m_rg_b_gate_x, rg_lambda=m_rg_lambda, rg_w_out=m_rg_w_out, ffn_w_up=m_ffn_w_up,
                ffn_conv_w=m_ffn_conv_w, ffn_w_down=m_ffn_w_down)
    v_in = dict(meta_tokens=v_meta_tokens, norm_mix_g=v_norm_mix_g, norm_ffn_g=v_norm_ffn_g, final_norm_g=v_final_norm_g,
                sc_w_in=v_sc_w_in, sc_conv_w=v_sc_conv_w, sc_w_out=v_sc_w_out, rg_w_in=v_rg_w_in, rg_conv_w=v_rg_conv_w,
                rg_conv_b=v_rg_conv_b, rg_w_gate_a=v_rg_w_gate_a, rg_b_gate_a=v_rg_b_gate_a, rg_w_gate_x=v_rg_w_gate_x,
                rg_b_gate_x=v_rg_b_gate_x, rg_lambda=v_rg_lambda, rg_w_out=v_rg_w_out, ffn_w_up=v_ffn_w_up,
                ffn_conv_w=v_ffn_conv_w, ffn_w_down=v_ffn_w_down)
    names = list(weights)

    seq, d = x.shape[1], x.shape[2]
    n_meta = meta_tokens.shape[0]
    n_ffn = ffn_w_up.shape[0]

    assert n_ffn == 2
    def shard_rows(w_in, w_out):
        return [w_in.T.astype(MXU_DTYPE), w_out.astype(MXU_DTYPE)]

    def both_orientations(in_t_gathered, out_gathered):
        w_in_t = in_t_gathered.reshape(-1, d)
        w_out = out_gathered.reshape(-1, d)
        return w_in_t.T, w_in_t, w_out, w_out.T

    shards = {"ffn0": shard_rows(ffn_w_up[0], ffn_w_down[0]), "rg": shard_rows(rg_w_in[0], rg_w_out[0]),
              "ffn1": shard_rows(ffn_w_up[1], ffn_w_down[1])}
    ffn_w_up_f, ffn_w_up_t, ffn_w_down_f, ffn_w_down_t = [None] * 2, [None] * 2, [None] * 2, [None] * 2

    small_names = ["meta_tokens", "sc_conv_w", "rg_conv_w", "rg_conv_b", "rg_b_gate_a", "rg_b_gate_x", "rg_lambda",
                   "ffn_conv_w"]
    small_lead = {n: weights[n].shape[:-1] for n in small_names}
    small_sizes = [weights[n].size for n in small_names]
    small_flat = jnp.concatenate([weights[n].reshape(-1) for n in small_names])
    small_rows = _pad_rows(small_flat, d, SUBLANES)
    sc_in_g, sc_out_g, small_g = _ride_alone(
        _GatherRider(shard_rows(sc_w_in[0], sc_w_out[0]) + [small_rows]), "gather_first")
    sc_w_in_f, sc_w_in_t, sc_w_out_f, sc_w_out_t = both_orientations(sc_in_g, sc_out_g)
    small_g = small_g.reshape(N_DEV, -1)
    small_full = {}
    o = 0
    for n, sz in zip(small_names, small_sizes):
        small_full[n] = _chunks_to_cols(small_g[:, o:o + sz], small_lead[n])
        o += sz

    wg, wgt = _pair_gate_weights(rg_w_gate_a[0].astype(MXU_DTYPE), rg_w_gate_x[0].astype(MXU_DTYPE))
    rg_cw, rg_cb = small_full["rg_conv_w"][0], small_full["rg_conv_b"]
    rg_ba, rg_bx, rg_lam = small_full["rg_b_gate_a"], small_full["rg_b_gate_x"], small_full["rg_lambda"]
    sc_cw = small_full["sc_conv_w"][0]
    ffn_cw = small_full["ffn_conv_w"]

    tp = _row_tile(n_meta + seq, ROW_TILE_PERM)

    def ffn_fwd(h, l, rider):
        n, a, z, *gathered = _ffn_up_fused(h, norm_ffn_g[l:l + 1], ffn_w_up_f[l], ffn_cw[l], tp, f"ffn{l}_up", rider)
        return _matmul_residual(z, ffn_w_down_f[l], h, f"ffn{l}_down"), (n, a, z), gathered

    h0, n0, p0, q0, h1, *ffn0_w = _sc_in_fused(x[0], small_full["meta_tokens"], norm_mix_g[0:1], sc_w_in_f, sc_cw,
                                               sc_w_out_f, tp, "sc_mixer", _GatherRider(shards["ffn0"]))
    ffn_w_up_f[0], ffn_w_up_t[0], ffn_w_down_f[0], ffn_w_down_t[0] = both_orientations(*ffn0_w)
    h2, ffn0_saved, rg_w = ffn_fwd(h1, 0, _GatherRider(shards["rg"]))
    rg_w_in_f, rg_w_in_t, rg_w_out_f, rg_w_out_t = both_orientations(*rg_w)
    n2, p2, y2, hs2, *rest2 = _rg_in_fused(h2, norm_mix_g[1:2], rg_w_in_f, rg_cw, rg_cb, wg, rg_ba, rg_bx, rg_lam,
                                           rg_w_out_f, tp, "rg_mixer", _GatherRider(shards["ffn1"]))
    rg_gates, h3, ffn1_w = rest2[:5], rest2[5], rest2[6:]
    ffn_w_up_f[1], ffn_w_up_t[1], ffn_w_down_f[1], ffn_w_down_t[1] = both_orientations(*ffn1_w)
    ffn1_saved = tuple(_ffn_up_fused(h3, norm_ffn_g[1:2], ffn_w_up_f[1], ffn_cw[1], tp, "ffn1_up"))
    dh4, loss_tile, d_final_g = _down_loss_head(ffn1_saved[2], ffn_w_down_f[1], h3, final_norm_g.reshape(1, d),
                                                loss_target[0], n_meta, tp, "ffn1_down_loss_head")
    loss = lax.psum(loss_tile[0, 0], AXES)

    arrived = {}

    def ffn_bwd(dh_out, h_in, saved, l):
        n, a, z = saved
        da, dhb, d_cw = _ffn_down_bwd_fused(dh_out, ffn_w_down_t[l], a, ffn_cw[l], tp, f"ffn{l}_down_bwd")
        d_w_down = _wgrad(z, dhb, f"ffn{l}_down_wgrad")
        d_w_up_t, arrived[f"ffn_w_down{l}"] = _wgrad(da, n, f"ffn{l}_up_wgrad", _ScatterRider(d_w_down))
        rows = d_w_up_t.shape[0] // N_DEV
        first_rows = (rows // 2) // (2 * SUBLANES) * (2 * SUBLANES)
        dh_in, d_g, arrived[f"ffn_w_up{l}a"] = _dgrad_in_norm(
            da, ffn_w_up_t[l], h_in, norm_ffn_g[l:l + 1], dh_out, f"ffn{l}_up_dgrad",
            _ScatterRider(d_w_up_t, (0, first_rows)))
        return dh_in, d_cw, d_g, _ScatterRider(d_w_up_t, (first_rows, rows - first_rows))

    dh3, d_fcw1, d_fg1, up1_rest = ffn_bwd(dh4, h3, ffn1_saved, 1)

    dp2, dhb3, d_rg_cw, d_rg_cb, d_wg, d_rg_ba, d_rg_bx, d_rg_lam, arrived["ffn_w_up1b"] = _rg_out_bwd_fused(
        dh3, rg_w_out_t, p2, hs2, rg_gates, rg_cw, wg, wgt, rg_lam, tp, "rg_out_bwd", up1_rest)
    d_wa, d_wx = _unpair_gate_grads(d_wg, rg_w_gate_a.shape[2])
    d_rg_w_out = _wgrad(y2, dhb3, "rg_out_wgrad")
    d_rg_w_in_t, arrived["rg_w_out"] = _wgrad(dp2, n2, "rg_in_wgrad", _ScatterRider(d_rg_w_out))
    dh2, d_mg1, arrived["rg_w_in"] = _dgrad_in_norm(
        dp2, rg_w_in_t, h2, norm_mix_g[1:2], dh3, "rg_in_dgrad", _ScatterRider(d_rg_w_in_t))

    dh1, d_fcw0, d_fg0, up0_rest = ffn_bwd(dh2, h1, ffn0_saved, 0)

    dp0, dhb1, d_sc_cw, arrived["ffn_w_up0b"] = _sc_out_bwd_fused(dh1, sc_w_out_t, p0, sc_cw, tp, "sc_out_bwd", up0_rest)
    d_sc_w_out = _wgrad(q0, dhb1, "sc_out_wgrad")
    d_sc_w_in_t, arrived["sc_w_out"] = _wgrad(dp0, n0, "sc_in_wgrad", _ScatterRider(d_sc_w_out))
    d_mg0, d_meta, d_x, arrived["sc_w_in"] = _dgrad_in_to_input(
        dp0, sc_w_in_t, h0, norm_mix_g[0:1], dh1, n_meta, tp, "sc_in_dgrad", _ScatterRider(d_sc_w_in_t))
    grad_x = d_x[None]

    grads = {}
    for n in ("sc_w_in", "rg_w_in"):
        grads[n] = _sum_slots(arrived[n], f"sum_{n}").T[None]
    for n in ("sc_w_out", "rg_w_out"):
        grads[n] = _sum_slots(arrived[n], f"sum_{n}")[None]
    grads["ffn_w_up"] = jnp.stack([jnp.concatenate(
        [_sum_slots(arrived[f"ffn_w_up{l}{part}"], f"sum_ffn_w_up{l}{part}") for part in "ab"], axis=0).T
        for l in range(n_ffn)])
    grads["ffn_w_down"] = jnp.stack([_sum_slots(arrived[f"ffn_w_down{l}"], f"sum_ffn_w_down{l}") for l in range(n_ffn)])

    small_grads = {"meta_tokens": d_meta,"sc_conv_w": d_sc_cw[None], "rg_conv_w": d_rg_cw[None],
                   "rg_conv_b": d_rg_cb, "rg_b_gate_a": d_rg_ba, "rg_b_gate_x": d_rg_bx, "rg_lambda": d_rg_lam,
                   "ffn_conv_w": jnp.stack([d_fcw0, d_fcw1])}
    small_chunks = jnp.concatenate([_cols_to_chunks(small_grads[n]) for n in small_names], axis=1)
    pad = small_rows.shape[0] * d - small_chunks.shape[1]
    small_chunks = jnp.pad(small_chunks, ((0, 0), (0, pad))).reshape(N_DEV, small_rows.shape[0], d)
    rep_names = ["norm_mix_g", "norm_ffn_g", "final_norm_g", "rg_w_gate_a", "rg_w_gate_x"]
    rep_grads = {"norm_mix_g": jnp.concatenate([d_mg0, d_mg1], axis=0),
                 "norm_ffn_g": jnp.concatenate([d_fg0, d_fg1], axis=0),
                 "final_norm_g": d_final_g.reshape(-1), "rg_w_gate_a": d_wa[None], "rg_w_gate_x": d_wx[None]}
    rep_flat = jnp.concatenate([rep_grads[n].reshape(-1) for n in rep_names])
    rep_chunk_rows = -(-rep_flat.shape[0] // (N_DEV * d))
    rep_chunk_rows += -(small_rows.shape[0] + rep_chunk_rows) % 16
    rep_chunks = jnp.pad(rep_flat, (0, N_DEV * rep_chunk_rows * d - rep_flat.shape[0])).reshape(N_DEV, rep_chunk_rows, d)

    last_chunks = jnp.concatenate([small_chunks, rep_chunks], axis=1)
    reduced = _sum_slots(_ride_alone(_ScatterRider(last_chunks.reshape(-1, d)), "scatter_last")[0], "sum_last")

    small_red = reduced[0:small_rows.shape[0]].reshape(-1)
    o = small_rows.shape[0]
    so = 0
    for n, sz in zip(small_names, small_sizes):
        grads[n] = small_red[so:so + sz].reshape(weights[n].shape)
        so += sz
    rep_red = _ride_alone(_GatherRider([reduced[o:o + rep_chunks.shape[1]]]), "gather_replicated_grads")[0].reshape(-1)
    ro = 0
    for n in rep_names:
        sz = weights[n].size
        grads[n] = rep_red[ro:ro + sz].reshape(weights[n].shape)
        ro += sz

    delta, new_m, new_v = {}, {}, {}
    for n in names:
        delta[n], new_m[n], new_v[n] = _adamw_nd(weights[n], grads[n], m_in[n], v_in[n], f"adamw_{n}")

    return (loss, grad_x, *[grads[n] for n in names], *[delta[n] for n in names],
            *[new_m[n] for n in names], *[new_v[n] for n in names])
```

```python
import jax
import jax.numpy as jnp
from jax import lax
from jax.experimental import pallas as pl
from jax.experimental.pallas import tpu as pltpu

F32 = jnp.float32
MXU_DTYPE = jnp.bfloat16
RMS_EPS = 1e-6
RG_C = 8.0
ADAM_LR = 0.001
ADAM_B1 = 0.9
ADAM_B2 = 0.999
ADAM_EPS = 1e-08
ADAM_WD = 0.01
ADAM_STEP = 10

N_DEV = 8
AXES = ("x", "y", "c")
SUBLANES = 8
LANES = 128
VMEM_LIMIT_BYTES = 48 * 1024 * 1024
ROW_TILE_MATMUL = 700
VMEM_LIMIT_WIDE_BYTES = 58 * 1024 * 1024
ROW_TILE_WGRAD = 3300
COL_TILE_WGRAD = 1408
ROW_TILE_PERM = 400
STRIP = 256
STRIP_FFN = 4096

_TN = (((0,), (0,)), ((), ()))
_NN = (((1,), (0,)), ((), ()))


def _row_tile(t, target):
    best = None
    for tm in range(16, t + 1, 16):
        if t % tm == 0 and tm <= target:
            best = tm
    return best if best is not None else t


def _col_tile(n, target):
    best = None
    for tn in range(LANES, n + 1, LANES):
        if n % tn == 0 and tn <= target:
            best = tn
    return best if best is not None else n


def _params(*sem):
    return pltpu.CompilerParams(dimension_semantics=sem, vmem_limit_bytes=VMEM_LIMIT_BYTES)


def _dot(a, b, dims):
    return lax.dot_general(a, b, dims, preferred_element_type=F32)


def _sigmoid(x):
    return 1.0 / (1.0 + jnp.exp(-x))


def _sigmoid_tanh(x):
    return 0.5 * jnp.tanh(0.5 * x) + 0.5


def _gelu(x):
    c = 0.7978845608028654
    t = jnp.tanh(c * (x + 0.044715 * (x * x * x)))
    return 0.5 * x * (1.0 + t)


def _gelu_and_grad(x):
    c = 0.7978845608028654
    x2 = x * x
    t = jnp.tanh(c * (x + 0.044715 * (x2 * x)))
    half = 0.5 * (1.0 + t)
    return x * half, half + 0.5 * x * (1.0 - t * t) * c * (1.0 + 3.0 * 0.044715 * x2)


def _softplus(x):
    return jnp.maximum(x, 0.0) + jnp.log1p(jnp.exp(-jnp.abs(x)))


def _time_scratch(tm, d):
    return [pltpu.VMEM((2, tm, d), F32), pltpu.VMEM((d // LANES, tm, LANES), F32), pltpu.SemaphoreType.DMA((2,))]


def _fetch_time_tile(i, nt, src_ref, head, tbuf, sems, tm):
    n_head = head.shape[0]

    def tile(j, slot):
        start = pl.multiple_of(j * tm - n_head, SUBLANES)
        return pltpu.make_async_copy(src_ref.at[pl.ds(start, tm)], tbuf.at[slot], sems.at[slot])

    first = pltpu.make_async_copy(src_ref.at[pl.ds(0, tm - n_head)], tbuf.at[0, pl.ds(n_head, tm - n_head)], sems.at[0])

    @pl.when(i == 0)
    def _():
        tbuf[0, 0:n_head, :] = head
        first.start()

    @pl.when(i + 1 < nt)
    def _():
        tile(i + 1, (i + 1) % 2).start()

    @pl.when(i == 0)
    def _():
        first.wait()

    @pl.when(i > 0)
    def _():
        tile(i, i % 2).wait()


def _time_to_tile_order(t_ref, slabs, out_ref):
    tm, d = out_ref.shape
    groups = tm // SUBLANES
    for k in range(d // LANES):
        slabs[k] = t_ref[:, k * LANES:(k + 1) * LANES]
    for k in range(d // LANES):
        for g in range(groups):
            out_ref[g * SUBLANES:(g + 1) * SUBLANES, k * LANES:(k + 1) * LANES] = slabs[k, pl.ds(g, SUBLANES, stride=groups), :]


def _tile_to_time_order(p_ref, slabs, t_ref):
    tm, d = p_ref.shape
    groups = tm // SUBLANES
    for k in range(d // LANES):
        for g in range(groups):
            slabs[k, pl.ds(g, SUBLANES, stride=groups), :] = p_ref[g * SUBLANES:(g + 1) * SUBLANES, k * LANES:(k + 1) * LANES]
    for k in range(d // LANES):
        t_ref[:, k * LANES:(k + 1) * LANES] = slabs[k]


def _rows(ref, sl):
    return [ref[k:k + 1, sl] for k in range(ref.shape[0])]


def _shift_down(x, before, s):
    if s == 0:
        return x
    n = x.shape[0]
    row = lax.broadcasted_iota(jnp.int32, (SUBLANES, x.shape[1]), 0)
    heads = []
    for g in range(s):
        v = x[n - (s - g) * SUBLANES:n - (s - g - 1) * SUBLANES]
        heads.append(pltpu.roll(jnp.where(row == SUBLANES - 1, before[s - g - 1], v), 1, axis=0))
    return jnp.concatenate(heads + [x[0:n - s * SUBLANES]], axis=0)


def _shift_up(x, after, s):
    if s == 0:
        return x
    row = lax.broadcasted_iota(jnp.int32, (SUBLANES, x.shape[1]), 0)
    tails = []
    for m in range(s):
        v = x[m * SUBLANES:(m + 1) * SUBLANES]
        tails.append(pltpu.roll(jnp.where(row == 0, after[m], v), SUBLANES - 1, axis=0))
    return jnp.concatenate([x[s * SUBLANES:]] + tails, axis=0)


def _weighted(w, windows):
    y = w[0] * windows[0]
    for k in range(1, len(w)):
        y = y + w[k] * windows[k]
    return y


def _windows(x, before, k_width):
    return [_shift_down(x, before, k_width - 1 - k) for k in range(k_width)]


def _conv_t(dy, after, w):
    k_width = len(w)
    return _weighted(w, [_shift_up(dy, after, k_width - 1 - k) for k in range(k_width)])


def _blocks(ref, sl, count, newest_first):
    n = ref.shape[0] // SUBLANES
    order = range(n - 1, n - 1 - count, -1) if newest_first else range(count)
    return [ref[b * SUBLANES:(b + 1) * SUBLANES, sl] for b in order]


def _halo_specs(tm, cols, count, tile_of):
    def spec(k):
        return pl.BlockSpec((SUBLANES, cols), lambda i: (jnp.maximum(tile_of(i) * (tm // SUBLANES) - k, 0), 0))
    return [spec(k) for k in range(1, count + 1)]


def _scan_tile(coef, val, out, carry, groups, reverse):
    cols = coef.shape[1]
    row = lax.broadcasted_iota(jnp.int32, (SUBLANES, cols), 0)

    def blk(i):
        g = groups - 1 - i if reverse else i
        return pl.ds(pl.multiple_of(g * SUBLANES, SUBLANES), SUBLANES)

    def local(i, pl_):
        p_prev, l_prev = pl_
        a = coef[blk(i), :]
        p = a * p_prev
        l = a * l_prev + val[blk(i), :]
        coef[blk(i), :] = p
        val[blk(i), :] = l
        return p, l

    pf, lf = lax.fori_loop(0, groups, local, (jnp.ones((SUBLANES, cols), F32), jnp.zeros((SUBLANES, cols), F32)))
    for s in (1, 2, 4):
        keep, sh = (row < SUBLANES - s, SUBLANES - s) if reverse else (row >= s, s)
        p_s = jnp.where(keep, pltpu.roll(pf, sh, axis=0), 1.0)
        l_s = jnp.where(keep, pltpu.roll(lf, sh, axis=0), 0.0)
        lf = pf * l_s + lf
        pf = pf * p_s
    end = lf + pf * carry
    if reverse:
        init = jnp.where(row == SUBLANES - 1, carry, pltpu.roll(end, SUBLANES - 1, axis=0))
        leaving = jnp.broadcast_to(end[0:1, :], (SUBLANES, cols))
    else:
        init = jnp.where(row == 0, carry, pltpu.roll(end, 1, axis=0))
        leaving = jnp.broadcast_to(end[SUBLANES - 1:SUBLANES, :], (SUBLANES, cols))

    def fix(i, _):
        out[blk(i), :] = val[blk(i), :] + coef[blk(i), :] * init
        return 0

    lax.fori_loop(0, groups, fix, 0)
    return leaving


def _resident(shape):
    return pl.BlockSpec(shape, lambda *_: (0,) * len(shape), pipeline_mode=pl.Buffered(1))


def _full(shape):
    return pl.BlockSpec(shape, lambda *_: (0,) * len(shape))


def _rmsnorm_to(h_ref, g_ref, n_ref):
    x = h_ref[...]
    ms = jnp.mean(x * x, axis=-1, keepdims=True)
    n_ref[...] = (x * lax.rsqrt(ms + RMS_EPS) * g_ref[...]).astype(n_ref.dtype)


_ANY = pl.BlockSpec(memory_space=pl.ANY)
_MESH = pl.DeviceIdType.MESH


def _dma_sems(n):
    return pltpu.SemaphoreType.DMA((n,))


def _when_each(*phases):
    for cond, fn in phases:
        if cond is not None:
            pl.when(cond)(fn)


class _NoRider:
    inputs = in_specs = out_shape = out_specs = scratch_shapes = ()

    def __call__(self, first, middle, last, ins, outs, scratch):
        pass


_NO_RIDER = _NoRider()


def _hand_on_step(nt):
    return (3 * nt) // 4


def _split_refs(rider, n_out, n_scratch, rest):
    a = len(rider.inputs)
    b = a + n_out
    c = b + len(rider.out_shape)
    e = c + n_scratch
    return rest[:a], rest[a:b], rest[b:c], rest[c:e], rest[e:]


class _GatherRider:
    def __init__(self, blocks):
        n = len(blocks)
        self.inputs = tuple(blocks)
        self.in_specs = (_ANY,) * n
        self.out_shape = tuple(jax.ShapeDtypeStruct((N_DEV,) + b.shape, b.dtype) for b in blocks)
        self.out_specs = (_ANY,) * n
        self.scratch_shapes = (_dma_sems(7 * n), _dma_sems(7 * n), _dma_sems(n))

    def __call__(self, first, middle, last, ins, outs, scratch):
        n = len(ins)
        send_sems, recv_sems, local_sems = scratch
        x, y, c = lax.axis_index("x"), lax.axis_index("y"), lax.axis_index("c")
        me, sibling = (x, y, c), (x, y, 1 - c)
        chips = [(1 - x, y), (x, 1 - y), (1 - x, 1 - y)]

        def slot(b, px, py, pc):
            return outs[b].at[4 * px + 2 * py + pc]

        def copy(k, b, block, to, src=None):
            return pltpu.make_async_remote_copy(
                src_ref=slot(b, *block) if src is None else src, dst_ref=slot(b, *block),
                send_sem=send_sems.at[k * n + b], recv_sem=recv_sems.at[k * n + b], device_id=to, device_id_type=_MESH)

        mine = [pltpu.make_async_copy(ins[b], slot(b, *me), local_sems.at[b]) for b in range(n)]
        own = [copy(0, b, me, sibling, src=ins[b]) for b in range(n)]
        own += [copy(1 + j, b, me, (*chip, c), src=ins[b]) for j, chip in enumerate(chips) for b in range(n)]
        passed = [[copy(4 + j, b, (*chip, c), sibling) for b in range(n)] for j, chip in enumerate(chips)]

        def at_first():
            for cp in mine + own:
                cp.start()

        def at_middle():
            for j, chip in enumerate(chips):
                for b in range(n):
                    copy(1 + j, b, (*chip, c), me).wait_recv()
                    passed[j][b].start()

        def at_last():
            for b in range(n):
                copy(0, b, sibling, me).wait_recv()
                for j, chip in enumerate(chips):
                    copy(4 + j, b, (*chip, 1 - c), me).wait_recv()
            for cp in own + [cp for group in passed for cp in group]:
                cp.wait_send()
            for cp in mine:
                cp.wait()

        _when_each((first, at_first), (middle, at_middle), (last, at_last))


class _ScatterRider:
    def __init__(self, grad, rows=None):
        r = grad.shape[0] // N_DEV
        chunks = grad.reshape(N_DEV, r, grad.shape[1])
        self.rows = rows if rows is not None else (0, r)
        self.inputs = (chunks,)
        self.in_specs = (_ANY,)
        self.out_shape = (jax.ShapeDtypeStruct((N_DEV, self.rows[1], grad.shape[1]), chunks.dtype),)
        self.out_specs = (_ANY,)
        self.scratch_shapes = (_dma_sems(N_DEV - 1), _dma_sems(N_DEV - 1), pltpu.SemaphoreType.DMA(()))

    def __call__(self, first, middle, last, ins, outs, scratch):
        (g_ref,), (r_ref,) = ins, outs
        send_sems, recv_sems, local_sem = scratch
        x, y, c = lax.axis_index("x"), lax.axis_index("y"), lax.axis_index("c")
        me = 4 * x + 2 * y + c
        part = pl.ds(*self.rows)
        mine = pltpu.make_async_copy(g_ref.at[me, part], r_ref.at[me], local_sem)
        copies = []
        for k in range(1, N_DEV):
            px, py, pc = (1 - x if k & 4 else x), (1 - y if k & 2 else y), (1 - c if k & 1 else c)
            copies.append(pltpu.make_async_remote_copy(
                src_ref=g_ref.at[4 * px + 2 * py + pc, part], dst_ref=r_ref.at[me],
                send_sem=send_sems.at[k - 1], recv_sem=recv_sems.at[k - 1],
                device_id=(px, py, pc), device_id_type=_MESH))

        def at_first():
            mine.start()
            for cp in copies:
                cp.start()

        def at_last():
            for cp in copies:
                cp.wait()
            mine.wait()

        _when_each((first, at_first), (last, at_last))


def _matmul_residual(q, w, h, name):
    t, k = q.shape
    d = w.shape[1]
    tm = _row_tile(t, ROW_TILE_MATMUL)

    def body(q_ref, w_ref, h_ref, o_ref):
        o_ref[...] = h_ref[...] + _dot(q_ref[...], w_ref[...], _NN)

    return pl.pallas_call(
        body, name=name, grid=(t // tm,),
        in_specs=[pl.BlockSpec((tm, k), lambda i: (i, 0)),
                  _resident((k, d)),
                  pl.BlockSpec((tm, d), lambda i: (i, 0))],
        out_specs=pl.BlockSpec((tm, d), lambda i: (i, 0)),
        out_shape=jax.ShapeDtypeStruct((t, d), F32),
        compiler_params=_params("parallel"),
    )(q, w, h)


def _dgrad_in_norm(dp, wt, h, g, dh_next, name, rider=_NO_RIDER):
    t, n = dp.shape
    d = wt.shape[1]
    tm = _row_tile(t, ROW_TILE_MATMUL)
    nt = t // tm

    def body(dp_ref, wt_ref, h_ref, g_ref, dhn_ref, *rest):
        r_in, (dh_ref, dg_ref), r_out, _, r_scratch = _split_refs(rider, 2, 0, rest)
        i = pl.program_id(0)
        rider(i == 0, i == nt // 2, None, r_in, r_out, r_scratch)
        dn = _dot(dp_ref[...], wt_ref[...], _NN)
        x = h_ref[...]
        ms = jnp.mean(x * x, axis=-1, keepdims=True)
        r = lax.rsqrt(ms + RMS_EPS)
        xhat = x * r
        dng = dn * g_ref[...]
        c = jnp.mean(dng * xhat, axis=-1, keepdims=True)
        dh_ref[...] = dhn_ref[...] + r * (dng - xhat * c)
        part = jnp.sum(dn * xhat, axis=0, keepdims=True)

        @pl.when(i == 0)
        def _():
            dg_ref[...] = part

        @pl.when(i > 0)
        def _():
            dg_ref[...] += part

        rider(None, None, i == nt - 1, r_in, r_out, r_scratch)

    return pl.pallas_call(
        body, name=name, grid=(nt,),
        in_specs=[pl.BlockSpec((tm, n), lambda i: (i, 0)),
                  _resident((n, d)),
                  pl.BlockSpec((tm, d), lambda i: (i, 0)),
                  pl.BlockSpec((1, d), lambda i: (0, 0)),
                  pl.BlockSpec((tm, d), lambda i: (i, 0))] + list(rider.in_specs),
        out_specs=[pl.BlockSpec((tm, d), lambda i: (i, 0)),
                   pl.BlockSpec((1, d), lambda i: (0, 0))] + list(rider.out_specs),
        out_shape=[jax.ShapeDtypeStruct((t, d), F32), jax.ShapeDtypeStruct((1, d), F32)] + list(rider.out_shape),
        scratch_shapes=list(rider.scratch_shapes),
        compiler_params=_params("arbitrary"),
    )(dp, wt, h, g, dh_next, *rider.inputs)


def _dgrad_in_to_input(dp, wt, h, g, dh_next, n_head, tm, name, rider=_NO_RIDER):
    t, n = dp.shape
    d = wt.shape[1]
    nt = t // tm

    def body(dp_ref, wt_ref, h_ref, g_ref, dhn_ref, *rest):
        r_in, (dg_ref, head_ref, rest_ref), r_out, (pbuf, tout, slabs, sems), r_scratch = _split_refs(rider, 3, 4, rest)
        i = pl.program_id(0)
        rider(i == 0, i == nt // 2, None, r_in, r_out, r_scratch)
        dn = _dot(dp_ref[...], wt_ref[...], _NN)
        x = h_ref[...]
        ms = jnp.mean(x * x, axis=-1, keepdims=True)
        r = lax.rsqrt(ms + RMS_EPS)
        xhat = x * r
        dng = dn * g_ref[...]
        c = jnp.mean(dng * xhat, axis=-1, keepdims=True)
        pbuf[...] = dhn_ref[...] + r * (dng - xhat * c)
        part = jnp.sum(dn * xhat, axis=0, keepdims=True)

        @pl.when(i == 0)
        def _():
            dg_ref[...] = part

        @pl.when(i > 0)
        def _():
            dg_ref[...] += part

        def store(j):
            if isinstance(j, int) and j == 0:
                return pltpu.make_async_copy(tout.at[0, pl.ds(n_head, tm - n_head)],
                                             rest_ref.at[pl.ds(0, tm - n_head)], sems.at[0])
            start = j * tm - n_head
            start = start if isinstance(start, int) else pl.multiple_of(start, SUBLANES)
            return pltpu.make_async_copy(tout.at[j % 2], rest_ref.at[pl.ds(start, tm)], sems.at[j % 2])

        @pl.when(i == 2)
        def _():
            store(0).wait()

        @pl.when(i > 2)
        def _():
            store(i - 2).wait()

        _tile_to_time_order(pbuf, slabs, tout.at[i % 2])

        @pl.when(i == 0)
        def _():
            head_ref[...] = tout[0, 0:n_head, :]
            store(0).start()

        @pl.when(i > 0)
        def _():
            store(i).start()

        @pl.when(i == nt - 1)
        def _():
            for j in (nt - 2, nt - 1):
                if j >= 0:
                    store(j).wait()

        rider(None, None, i == nt - 1, r_in, r_out, r_scratch)

    return pl.pallas_call(
        body, name=name, grid=(nt,),
        in_specs=[pl.BlockSpec((tm, n), lambda i: (i, 0)),
                  _resident((n, d)),
                  pl.BlockSpec((tm, d), lambda i: (i, 0)),
                  pl.BlockSpec((1, d), lambda i: (0, 0)),
                  pl.BlockSpec((tm, d), lambda i: (i, 0))] + list(rider.in_specs),
        out_specs=[pl.BlockSpec((1, d), lambda i: (0, 0)), _full((n_head, d)), _ANY] + list(rider.out_specs),
        out_shape=[jax.ShapeDtypeStruct((1, d), F32), jax.ShapeDtypeStruct((n_head, d), F32),
                   jax.ShapeDtypeStruct((t - n_head, d), F32)] + list(rider.out_shape),
        scratch_shapes=[pltpu.VMEM((tm, d), F32)] + _time_scratch(tm, d) + list(rider.scratch_shapes),
        compiler_params=_params("arbitrary"),
    )(dp, wt, h, g, dh_next, *rider.inputs)


def _wgrad(a, b, name, rider=_NO_RIDER):
    t, m = a.shape
    d = b.shape[1]
    tmm, tk = _col_tile(m, COL_TILE_WGRAD), _row_tile(t, ROW_TILE_WGRAD)
    nm, nk = m // tmm, t // tk

    def body(a_ref, b_ref, *rest):
        r_in, (o_ref,), r_out, _, r_scratch = _split_refs(rider, 1, 0, rest)
        i, k = pl.program_id(0), pl.program_id(1)
        rider(jnp.logical_and(i == 0, k == 0), None, None, r_in, r_out, r_scratch)

        @pl.when(k == 0)
        def _():
            o_ref[...] = jnp.zeros_like(o_ref)

        o_ref[...] += _dot(a_ref[...], b_ref[...], _TN)
        rider(None, None, jnp.logical_and(i == nm - 1, k == nk - 1), r_in, r_out, r_scratch)

    outs = pl.pallas_call(
        body, name=name, grid=(nm, nk),
        in_specs=[pl.BlockSpec((tk, tmm), lambda i, k: (k, i)),
                  pl.BlockSpec((tk, d), lambda i, k: (k, 0))] + list(rider.in_specs),
        out_specs=[pl.BlockSpec((tmm, d), lambda i, k: (i, 0))] + list(rider.out_specs),
        out_shape=[jax.ShapeDtypeStruct((m, d), F32)] + list(rider.out_shape),
        scratch_shapes=list(rider.scratch_shapes),
        compiler_params=pltpu.CompilerParams(dimension_semantics=("arbitrary", "arbitrary"),
                                             vmem_limit_bytes=VMEM_LIMIT_WIDE_BYTES),
    )(a, b, *rider.inputs)
    return outs if rider.out_shape else outs[0]


def _ffn_up_fused(h, g, w, conv_w, tm, name, rider=_NO_RIDER):
    t, d = h.shape
    f2 = w.shape[1]
    f = f2 // 2
    cw = min(STRIP_FFN, f)
    kw = conv_w.shape[0]
    nh = kw - 1
    nt = t // tm

    def body(h_ref, g_ref, w_ref, cw_ref, *rest):
        r_in, (n_ref, a_ref, z_ref), r_out, (carry,), r_scratch = _split_refs(rider, 3, 1, rest)
        i = pl.program_id(0)
        rider(i == 0, i == _hand_on_step(nt), None, r_in, r_out, r_scratch)

        @pl.when(i == 0)
        def _():
            carry[...] = jnp.zeros_like(carry)

        _rmsnorm_to(h_ref, g_ref, n_ref)
        for c in range(0, f, cw):
            conv = []
            for sl in (slice(c, c + cw), slice(f + c, f + c + cw)):
                a = _dot(n_ref[...], w_ref[:, sl], _NN)
                a_ref[:, sl] = a
                conv.append(_weighted(_rows(cw_ref, sl), _windows(a, _blocks(carry, sl, nh, True), kw)))
                carry[:, sl] = a[tm - nh * SUBLANES:tm]
            gg, vv = conv
            z_ref[:, c:c + cw] = (gg * _sigmoid(gg) * vv).astype(z_ref.dtype)
        rider(None, None, i == nt - 1, r_in, r_out, r_scratch)

    return pl.pallas_call(
        body, name=name, grid=(nt,),
        in_specs=[pl.BlockSpec((tm, d), lambda i: (i, 0)), pl.BlockSpec((1, d), lambda i: (0, 0)),
                  _resident((d, f2)), _full(conv_w.shape)] + list(rider.in_specs),
        out_specs=[pl.BlockSpec((tm, d), lambda i: (i, 0)), pl.BlockSpec((tm, f2), lambda i: (i, 0)),
                   pl.BlockSpec((tm, f), lambda i: (i, 0))] + list(rider.out_specs),
        out_shape=[jax.ShapeDtypeStruct((t, d), MXU_DTYPE), jax.ShapeDtypeStruct((t, f2), F32),
                   jax.ShapeDtypeStruct((t, f), MXU_DTYPE)] + list(rider.out_shape),
        scratch_shapes=[pltpu.VMEM((nh * SUBLANES, f2), F32)] + list(rider.scratch_shapes),
        compiler_params=_params("arbitrary"),
    )(h, g, w, conv_w, *rider.inputs)


def _ffn_down_bwd_fused(dh, wdt, a, conv_w, tm, name):
    t, d = dh.shape
    f = wdt.shape[1]
    f2 = 2 * f
    nt = t // tm
    cw = min(STRIP_FFN, f)
    kw = conv_w.shape[0]
    nh = kw - 1
    rev = lambda i: nt - 1 - i

    def body(dh_ref, wdt_ref, a_ref, *rest):
        halo_refs = rest[:nh]
        cw_ref, da_ref, dhb_ref, dw_ref, carry = rest[nh:]
        i = pl.program_id(0)
        newest, oldest = i == 0, i == nt - 1

        @pl.when(newest)
        def _():
            carry[...] = jnp.zeros_like(carry)
            dw_ref[...] = jnp.zeros_like(dw_ref)

        dhb_ref[...] = dh_ref[...].astype(dhb_ref.dtype)
        for c in range(0, f, cw):
            gsl, vsl = slice(c, c + cw), slice(f + c, f + c + cw)
            dz = _dot(dhb_ref[...], wdt_ref[:, gsl], _NN)
            win, conv = {}, {}
            for sl in (gsl, vsl):
                before = [jnp.where(oldest, 0.0, r[:, sl]) for r in halo_refs]
                win[sl.start] = _windows(a_ref[:, sl], before, kw)
                conv[sl.start] = _weighted(_rows(cw_ref, sl), win[sl.start])
            gg, vv = conv[gsl.start], conv[vsl.start]
            s = _sigmoid(gg)
            grads = {gsl.start: dz * vv * s * (1.0 + gg * (1.0 - s)), vsl.start: dz * gg * s}
            for sl in (gsl, vsl):
                dconv = grads[sl.start]
                da_ref[:, sl] = _conv_t(dconv, _blocks(carry, sl, nh, False), _rows(cw_ref, sl)).astype(da_ref.dtype)
                for k in range(kw):
                    dw_ref[k:k + 1, sl] += jnp.sum(dconv * win[sl.start][k], axis=0, keepdims=True)
                carry[:, sl] = dconv[0:nh * SUBLANES]

    return pl.pallas_call(
        body, name=name, grid=(nt,),
        in_specs=[pl.BlockSpec((tm, d), lambda i: (rev(i), 0)), _resident((d, f)),
                  pl.BlockSpec((tm, f2), lambda i: (rev(i), 0))] + _halo_specs(tm, f2, nh, rev) + [_full(conv_w.shape)],
        out_specs=[pl.BlockSpec((tm, f2), lambda i: (rev(i), 0)), pl.BlockSpec((tm, d), lambda i: (rev(i), 0)),
                   _full(conv_w.shape)],
        out_shape=[jax.ShapeDtypeStruct((t, f2), MXU_DTYPE), jax.ShapeDtypeStruct((t, d), MXU_DTYPE),
                   jax.ShapeDtypeStruct(conv_w.shape, F32)],
        scratch_shapes=[pltpu.VMEM((nh * SUBLANES, f2), F32)],
        compiler_params=_params("arbitrary"),
    )(dh, wdt, a, *([a] * nh), conv_w)


def _sc_in_fused(x, meta, g, w, conv_w, w_out, tm, name, rider=_NO_RIDER):
    d = x.shape[1]
    t = x.shape[0] + meta.shape[0]
    cw = min(STRIP, d)
    kw = conv_w.shape[0]
    nh = kw - 1
    nt = t // tm

    def body(x_ref, meta_ref, g_ref, w_ref, cw_ref, wo_ref, *rest):
        r_in, (h_ref, n_ref, p_ref, q_ref, h1_ref), r_out, (carry, tbuf, slabs, sems), r_scratch = _split_refs(
            rider, 5, 4, rest)
        i = pl.program_id(0)
        rider(i == 0, i == _hand_on_step(nt), None, r_in, r_out, r_scratch)

        @pl.when(i == 0)
        def _():
            carry[...] = jnp.zeros_like(carry)

        _fetch_time_tile(i, nt, x_ref, meta_ref[...], tbuf, sems, tm)
        _time_to_tile_order(tbuf.at[i % 2], slabs, h_ref)
        _rmsnorm_to(h_ref, g_ref, n_ref)
        for c in range(0, d, cw):
            sl = slice(c, c + cw)
            parts = []
            for base in (0, d, 2 * d):
                psl = slice(base + c, base + c + cw)
                parts.append(_dot(n_ref[...], w_ref[:, psl], _NN))
                p_ref[:, psl] = parts[-1]
            bg, cg, v = parts
            cv = cg * v
            u = _weighted(_rows(cw_ref, sl), _windows(cv, _blocks(carry, sl, nh, True), kw))
            carry[:, sl] = cv[tm - nh * SUBLANES:tm]
            q_ref[:, sl] = (bg * u).astype(q_ref.dtype)
        h1_ref[...] = h_ref[...] + _dot(q_ref[...], wo_ref[...], _NN)
        rider(None, None, i == nt - 1, r_in, r_out, r_scratch)

    row_tile = lambda cols: pl.BlockSpec((tm, cols), lambda i: (i, 0))
    return pl.pallas_call(
        body, name=name, grid=(nt,),
        in_specs=[_ANY, _full(meta.shape), pl.BlockSpec((1, d), lambda i: (0, 0)),
                  _resident((d, 3 * d)), _full(conv_w.shape), _resident(w_out.shape)] + list(rider.in_specs),
        out_specs=[row_tile(d), row_tile(d), row_tile(3 * d), row_tile(d), row_tile(d)] + list(rider.out_specs),
        out_shape=[jax.ShapeDtypeStruct((t, d), F32), jax.ShapeDtypeStruct((t, d), MXU_DTYPE),
                   jax.ShapeDtypeStruct((t, 3 * d), F32), jax.ShapeDtypeStruct((t, d), MXU_DTYPE),
                   jax.ShapeDtypeStruct((t, d), F32)] + list(rider.out_shape),
        scratch_shapes=[pltpu.VMEM((nh * SUBLANES, d), F32)] + _time_scratch(tm, d) + list(rider.scratch_shapes),
        compiler_params=_params("arbitrary"),
    )(x, meta, g, w, conv_w, w_out, *rider.inputs)


def _sc_out_bwd_fused(dh, wot, p, conv_w, tm, name, rider=_NO_RIDER):
    t, d = dh.shape
    nt = t // tm
    cw = min(STRIP, d)
    kw = conv_w.shape[0]
    nh = kw - 1
    rev = lambda i: nt - 1 - i

    def body(dh_ref, wot_ref, p_ref, *rest):
        halo_refs, cw_ref = rest[:nh], rest[nh]
        r_in, (dp_ref, dhb_ref, dw_ref), r_out, (carry,), r_scratch = _split_refs(rider, 3, 1, rest[nh + 1:])
        i = pl.program_id(0)
        newest, oldest = i == 0, i == nt - 1
        rider(newest, i == nt // 2, None, r_in, r_out, r_scratch)

        @pl.when(newest)
        def _():
            carry[...] = jnp.zeros_like(carry)
            dw_ref[...] = jnp.zeros_like(dw_ref)

        dhb_ref[...] = dh_ref[...].astype(dhb_ref.dtype)
        for c in range(0, d, cw):
            sl, csl, vsl = slice(c, c + cw), slice(d + c, d + c + cw), slice(2 * d + c, 2 * d + c + cw)
            w = _rows(cw_ref, sl)
            dq = _dot(dhb_ref[...], wot_ref[:, sl], _NN)
            bg, cg, v = p_ref[:, sl], p_ref[:, csl], p_ref[:, vsl]
            before = [jnp.where(oldest, 0.0, r[:, csl] * r[:, vsl]) for r in halo_refs]
            win = _windows(cg * v, before, kw)
            du = dq * bg
            dcv = _conv_t(du, _blocks(carry, sl, nh, False), w)
            dp_ref[:, sl] = (dq * _weighted(w, win)).astype(dp_ref.dtype)
            dp_ref[:, csl] = (dcv * v).astype(dp_ref.dtype)
            dp_ref[:, vsl] = (dcv * cg).astype(dp_ref.dtype)
            for k in range(kw):
                dw_ref[k:k + 1, sl] += jnp.sum(du * win[k], axis=0, keepdims=True)
            carry[:, sl] = du[0:nh * SUBLANES]
        rider(None, None, oldest, r_in, r_out, r_scratch)

    tile = lambda cols: pl.BlockSpec((tm, cols), lambda i: (rev(i), 0))
    return pl.pallas_call(
        body, name=name, grid=(nt,),
        in_specs=[tile(d), _resident((d, d)), tile(3 * d)] + _halo_specs(tm, 3 * d, nh, rev) + [_full(conv_w.shape)]
                 + list(rider.in_specs),
        out_specs=[tile(3 * d), tile(d), _full(conv_w.shape)] + list(rider.out_specs),
        out_shape=[jax.ShapeDtypeStruct((t, 3 * d), MXU_DTYPE), jax.ShapeDtypeStruct((t, d), MXU_DTYPE),
                   jax.ShapeDtypeStruct(conv_w.shape, F32)] + list(rider.out_shape),
        scratch_shapes=[pltpu.VMEM((nh * SUBLANES, d), F32)] + list(rider.scratch_shapes),
        compiler_params=_params("arbitrary"),
    )(dh, wot, p, *([p] * nh), conv_w, *rider.inputs)


def _pair_gate_weights(wa, wx):
    nb, bd, _ = wa.shape
    zero = jnp.zeros((nb // 2, bd, bd), wa.dtype)

    def pair(w):
        w = w.reshape(nb // 2, 2, bd, bd)
        top = jnp.concatenate([w[:, 0], zero], axis=2)
        bottom = jnp.concatenate([zero, w[:, 1]], axis=2)
        return jnp.concatenate([top, bottom], axis=1)

    both = jnp.concatenate([pair(wa), pair(wx)], axis=2)
    return both, jnp.swapaxes(both, 1, 2)


def _unpair_gate_grads(dw, bd):
    def blocks(cols):
        d0 = dw[:, :bd, cols:cols + bd]
        d1 = dw[:, bd:, cols + bd:cols + 2 * bd]
        return jnp.stack([d0, d1], axis=1).reshape(-1, bd, bd)
    return blocks(0), blocks(2 * bd)


def _rg_gates(u, k, wg_ref, ba, bx, lam):
    ub = u.astype(MXU_DTYPE)
    z = _dot(ub, wg_ref[k], _NN)
    half = z.shape[1] // 2
    r = _sigmoid_tanh(z[:, :half] + ba)
    ig = _sigmoid_tanh(z[:, half:] + bx)
    sp = _softplus(-lam)
    la = -RG_C * r * sp
    a = jnp.exp(la)
    th = jnp.tanh(la)
    mult = jnp.sqrt(-2.0 * th / (1.0 - th))
    return r, ig, a, mult


def _rg_in_fused(h, g, w, conv_w, conv_b, wg, ba, bx, lam, w_out, tm, name, rider=_NO_RIDER):
    t, d = h.shape
    r2 = w.shape[1]
    rr = r2 // 2
    nb, bd = wg.shape[0], wg.shape[1]
    kw = conv_w.shape[0]
    nh = kw - 1
    groups = tm // SUBLANES
    nt = t // tm

    def body(h_ref, g_ref, w_ref, cw_ref, cb_ref, wg_ref, ba_ref, bx_ref, lam_ref, wo_ref, *rest):
        (r_in, (n_ref, p_ref, y_ref, hs_ref, r_ref, ig_ref, a_ref, mult_ref, hn_ref), r_out,
         (a_scr, b_scr, carry_rb, carry_h), r_scratch) = _split_refs(rider, 9, 4, rest)
        i = pl.program_id(0)
        rider(i == 0, i == _hand_on_step(nt), None, r_in, r_out, r_scratch)

        @pl.when(i == 0)
        def _():
            carry_rb[...] = jnp.zeros_like(carry_rb)
            carry_h[...] = jnp.zeros_like(carry_h)

        _rmsnorm_to(h_ref, g_ref, n_ref)
        for k in range(nb):
            sl = slice(k * bd, (k + 1) * bd)
            rsl = slice(rr + k * bd, rr + (k + 1) * bd)
            p_ref[:, sl] = _dot(n_ref[...], w_ref[:, sl], _NN)
            rb = _dot(n_ref[...], w_ref[:, rsl], _NN)
            p_ref[:, rsl] = rb
            u = _weighted(_rows(cw_ref, sl), _windows(rb, _blocks(carry_rb, sl, nh, True), kw)) + cb_ref[:, sl]
            carry_rb[:, sl] = rb[tm - nh * SUBLANES:tm]
            r, ig, a, mult = _rg_gates(u, k, wg_ref, ba_ref[:, sl], bx_ref[:, sl], lam_ref[:, sl])
            for ref, val in ((r_ref, r), (ig_ref, ig), (a_ref, a), (mult_ref, mult)):
                ref[:, sl] = val
            a_scr[:, sl] = a
            b_scr[:, sl] = mult * (ig * u)

        carry_h[...] = _scan_tile(a_scr, b_scr, hs_ref, carry_h[...], groups, reverse=False)

        for k in range(nb):
            sl = slice(k * bd, (k + 1) * bd)
            y_ref[:, sl] = (hs_ref[:, sl] * _gelu(p_ref[:, sl])).astype(y_ref.dtype)
        hn_ref[...] = h_ref[...] + _dot(y_ref[...], wo_ref[...], _NN)
        rider(None, None, i == nt - 1, r_in, r_out, r_scratch)

    vm = lambda rows: pltpu.VMEM((rows, rr), F32)
    return pl.pallas_call(
        body, name=name, grid=(nt,),
        in_specs=[pl.BlockSpec((tm, d), lambda i: (i, 0)), pl.BlockSpec((1, d), lambda i: (0, 0)),
                  _resident((d, r2)), _full(conv_w.shape), _full(conv_b.shape), _full(wg.shape), _full(ba.shape),
                  _full(bx.shape), _full(lam.shape), _resident(w_out.shape)] + list(rider.in_specs),
        out_specs=[pl.BlockSpec((tm, d), lambda i: (i, 0)), pl.BlockSpec((tm, r2), lambda i: (i, 0))]
                  + [pl.BlockSpec((tm, rr), lambda i: (i, 0))] * 6 + [pl.BlockSpec((tm, d), lambda i: (i, 0))]
                  + list(rider.out_specs),
        out_shape=[jax.ShapeDtypeStruct((t, d), MXU_DTYPE), jax.ShapeDtypeStruct((t, r2), F32),
                   jax.ShapeDtypeStruct((t, rr), MXU_DTYPE)] + [jax.ShapeDtypeStruct((t, rr), F32)] * 5
                  + [jax.ShapeDtypeStruct((t, d), F32)] + list(rider.out_shape),
        scratch_shapes=[vm(tm), vm(tm), vm(nh * SUBLANES), vm(SUBLANES)] + list(rider.scratch_shapes),
        compiler_params=pltpu.CompilerParams(dimension_semantics=("arbitrary",), vmem_limit_bytes=VMEM_LIMIT_WIDE_BYTES),
    )(h, g, w, conv_w, conv_b, wg, ba, bx, lam, w_out, *rider.inputs)


def _rg_out_bwd_fused(dh, wot, p, hs, gates, conv_w, conv_b, wg, wgt, lam, tm, name, rider=_NO_RIDER):
    t, d = dh.shape
    r2 = p.shape[1]
    rr = r2 // 2
    nb, bd = wg.shape[0], wg.shape[1]
    nt = t // tm
    kw = conv_w.shape[0]
    nh = kw - 1
    groups = tm // SUBLANES
    rev = lambda i: nt - 1 - i

    def body(dh_ref, wot_ref, p_ref, hs_ref, hsh_ref, r_ref, ig_ref, a_ref, mult_ref, *rest):
        halo_refs = rest[:nh]
        cw_ref, cb_ref, wg_ref, wgt_ref, lam_ref = rest[nh:nh + 5]
        (r_in, (dp_ref, dhb_ref, dcw_ref, dcb_ref, dwg_ref, dba_ref, dbx_ref, dlam_ref), r_out,
         (as_scr, g_scr, carry_g, carry_du), r_scratch) = _split_refs(rider, 8, 4, rest[nh + 5:])
        i = pl.program_id(0)
        newest, oldest = i == 0, i == nt - 1
        rider(newest, i == nt // 2, None, r_in, r_out, r_scratch)

        @pl.when(newest)
        def _():
            carry_g[...] = jnp.zeros_like(carry_g)
            carry_du[...] = jnp.zeros_like(carry_du)
            for ref in (dcw_ref, dcb_ref, dwg_ref, dba_ref, dbx_ref, dlam_ref):
                ref[...] = jnp.zeros_like(ref)

        dhb_ref[...] = dh_ref[...].astype(dhb_ref.dtype)
        ones = jnp.ones((SUBLANES, bd), F32)

        def conv_windows(sl, rsl):
            before = [jnp.where(oldest, 0.0, r[:, rsl]) for r in halo_refs]
            return _windows(p_ref[:, rsl], before, kw)

        for k in range(nb):
            sl = slice(k * bd, (k + 1) * bd)
            rsl = slice(rr + k * bd, rr + (k + 1) * bd)
            dy = _dot(dhb_ref[...], wot_ref[:, sl], _NN)
            gate, dgate = _gelu_and_grad(p_ref[:, sl])
            g_scr[:, sl] = dy * gate
            dp_ref[:, sl] = (dy * hs_ref[:, sl] * dgate).astype(dp_ref.dtype)
            as_scr[:, sl] = _shift_up(a_ref[:, sl], [ones], 1)

        g_first = _scan_tile(as_scr, g_scr, g_scr, carry_g[...], groups, reverse=True)
        carry_g[...] = jnp.broadcast_to(a_ref[0:1, :], (SUBLANES, rr)) * g_first

        for k in range(nb):
            sl = slice(k * bd, (k + 1) * bd)
            rsl = slice(rr + k * bd, rr + (k + 1) * bd)
            cw = _rows(cw_ref, sl)
            lam_k = lam_ref[:, sl]
            sp = _softplus(-lam_k)
            g = g_scr[:, sl]
            win = conv_windows(sl, rsl)
            u = _weighted(cw, win) + cb_ref[:, sl]
            a, r, ig, mult = a_ref[:, sl], r_ref[:, sl], ig_ref[:, sl], mult_ref[:, sl]
            da = g * _shift_down(hs_ref[:, sl], [jnp.where(oldest, 0.0, hsh_ref[:, sl])], 1)
            dmult = g * (ig * u)
            d_iu = g * mult
            dla = da * a - dmult * (a * a) / mult
            dr = dla * (-RG_C * sp)
            dsp = jnp.sum(dla * (-RG_C * r), axis=0, keepdims=True)
            dlam_ref[:, sl] += dsp * (-_sigmoid(-lam_k))
            dza = dr * r * (1.0 - r)
            dzx = (d_iu * u) * ig * (1.0 - ig)
            dba_ref[:, sl] += jnp.sum(dza, axis=0, keepdims=True)
            dbx_ref[:, sl] += jnp.sum(dzx, axis=0, keepdims=True)
            ub = u.astype(MXU_DTYPE)
            dz = jnp.concatenate([dza, dzx], axis=1).astype(MXU_DTYPE)
            dwg_ref[k] += _dot(ub, dz, _TN)
            du = d_iu * ig + _dot(dz, wgt_ref[k], _NN)
            dcb_ref[:, sl] += jnp.sum(du, axis=0, keepdims=True)
            for kk in range(kw):
                dcw_ref[kk:kk + 1, sl] += jnp.sum(du * win[kk], axis=0, keepdims=True)
            dp_ref[:, rsl] = _conv_t(du, _blocks(carry_du, sl, nh, False), cw).astype(dp_ref.dtype)
            carry_du[:, sl] = du[0:nh * SUBLANES]
        rider(None, None, oldest, r_in, r_out, r_scratch)

    tile = lambda cols: pl.BlockSpec((tm, cols), lambda i: (rev(i), 0))
    vm = lambda rows: pltpu.VMEM((rows, rr), F32)
    grads = [conv_w.shape, lam.shape, wg.shape, lam.shape, lam.shape, lam.shape]
    return pl.pallas_call(
        body, name=name, grid=(nt,),
        in_specs=[tile(d), _resident((d, rr)), tile(r2), tile(rr)] + _halo_specs(tm, rr, 1, rev) + [tile(rr)] * 4
                 + _halo_specs(tm, r2, nh, rev)
                 + [_full(conv_w.shape), _full(lam.shape), _full(wg.shape), _full(wgt.shape), _full(lam.shape)]
                 + list(rider.in_specs),
        out_specs=[tile(r2), tile(d)] + [_full(s) for s in grads] + list(rider.out_specs),
        out_shape=[jax.ShapeDtypeStruct((t, r2), MXU_DTYPE), jax.ShapeDtypeStruct((t, d), MXU_DTYPE)]
                  + [jax.ShapeDtypeStruct(s, F32) for s in grads] + list(rider.out_shape),
        scratch_shapes=[vm(tm)] * 2 + [vm(SUBLANES), vm(nh * SUBLANES)] + list(rider.scratch_shapes),
        compiler_params=pltpu.CompilerParams(dimension_semantics=("arbitrary",), vmem_limit_bytes=VMEM_LIMIT_WIDE_BYTES),
    )(dh, wot, p, hs, hs, *gates, *([p] * nh), conv_w, conv_b, wg, wgt, lam, *rider.inputs)


def _down_loss_head(z, w, h, g, target, n_meta, tm, name):
    t, d = h.shape
    k = z.shape[1]
    groups = tm // SUBLANES
    nt = t // tm

    def body(z_ref, w_ref, h_ref, g_ref, t_ref, dh_ref, loss_ref, dg_ref, tgt, tbuf, slabs, sems):
        i = pl.program_id(0)
        _fetch_time_tile(i, nt, t_ref, jnp.zeros((n_meta, d), F32), tbuf, sems, tm)
        _time_to_tile_order(tbuf.at[i % 2], slabs, tgt)
        x = h_ref[...] + _dot(z_ref[...], w_ref[...], _NN)
        ms = jnp.mean(x * x, axis=-1, keepdims=True)
        r = lax.rsqrt(ms + RMS_EPS)
        xhat = x * r
        gg = g_ref[...]
        row = lax.broadcasted_iota(jnp.int32, (tm, 1), 0)
        time = i * tm + jnp.right_shift(row, 3) + jnp.bitwise_and(row, SUBLANES - 1) * groups
        err = jnp.where(time >= n_meta, xhat * gg - tgt[...], 0.0)
        dout = err * (1.0 / d)
        dng = dout * gg
        c = jnp.mean(dng * xhat, axis=-1, keepdims=True)
        dh_ref[...] = r * (dng - xhat * c)
        part_dg = jnp.sum(dout * xhat, axis=0, keepdims=True)
        part_loss = jnp.broadcast_to(0.5 * jnp.sum(err * dout, keepdims=True), loss_ref.shape)

        @pl.when(i == 0)
        def _():
            dg_ref[...] = part_dg
            loss_ref[...] = part_loss

        @pl.when(i > 0)
        def _():
            dg_ref[...] += part_dg
            loss_ref[...] += part_loss

    return pl.pallas_call(
        body, name=name, grid=(nt,),
        in_specs=[pl.BlockSpec((tm, k), lambda i: (i, 0)),
                  _resident((k, d)),
                  pl.BlockSpec((tm, d), lambda i: (i, 0)),
                  pl.BlockSpec((1, d), lambda i: (0, 0)),
                  _ANY],
        out_specs=[pl.BlockSpec((tm, d), lambda i: (i, 0)),
                   pl.BlockSpec((SUBLANES, LANES), lambda i: (0, 0)),
                   pl.BlockSpec((1, d), lambda i: (0, 0))],
        out_shape=[jax.ShapeDtypeStruct((t, d), F32), jax.ShapeDtypeStruct((SUBLANES, LANES), F32),
                   jax.ShapeDtypeStruct((1, d), F32)],
        scratch_shapes=[pltpu.VMEM((tm, d), F32)] + _time_scratch(tm, d),
        compiler_params=_params("arbitrary"),
    )(z, w, h, g, target)


def _adamw(w, g, m, v, name):
    rows, cols = w.shape
    tr = rows
    if rows > 512:
        for cand in range(8, 513, 8):
            if rows % cand == 0:
                tr = cand

    def body(w_ref, g_ref, m_ref, v_ref, d_ref, nm_ref, nv_ref):
        g_ = g_ref[...]
        m_ = ADAM_B1 * m_ref[...] + (1.0 - ADAM_B1) * g_
        v_ = ADAM_B2 * v_ref[...] + (1.0 - ADAM_B2) * (g_ * g_)
        m_hat = m_ / (1.0 - ADAM_B1 ** ADAM_STEP)
        v_hat = v_ / (1.0 - ADAM_B2 ** ADAM_STEP)
        d_ref[...] = -ADAM_LR * (m_hat / (jnp.sqrt(v_hat) + ADAM_EPS) + ADAM_WD * w_ref[...])
        nm_ref[...] = m_
        nv_ref[...] = v_

    spec = pl.BlockSpec((tr, cols), lambda i: (i, 0))
    shape = jax.ShapeDtypeStruct((rows, cols), F32)
    return pl.pallas_call(
        body, name=name, grid=(rows // tr,),
        in_specs=[spec] * 4, out_specs=[spec] * 3, out_shape=[shape] * 3,
        compiler_params=_params("parallel"),
    )(w, g, m, v)


def _adamw_nd(w, g, m, v, name):
    shape = w.shape
    two_d = (-1, shape[-1]) if w.ndim > 1 else (1, -1)
    outs = _adamw(w.reshape(two_d), g.reshape(two_d), m.reshape(two_d), v.reshape(two_d), name)
    return tuple(o.reshape(shape) for o in outs)


def _ride_alone(rider, name):
    def body(*refs):
        r_in, _, r_out, _, r_scratch = _split_refs(rider, 0, 0, refs)
        now = pl.program_id(0) == 0
        rider(now, None, None, r_in, r_out, r_scratch)
        rider(None, now, None, r_in, r_out, r_scratch)
        rider(None, None, now, r_in, r_out, r_scratch)

    return pl.pallas_call(
        body, name=name, grid=(1,),
        in_specs=list(rider.in_specs), out_specs=list(rider.out_specs), out_shape=list(rider.out_shape),
        scratch_shapes=list(rider.scratch_shapes),
        compiler_params=_params("arbitrary"),
    )(*rider.inputs)


def _sum_slots(parts, name):
    slots, rows, cols = parts.shape
    tr = _row_tile(rows, 256) if rows % 16 == 0 else rows

    def body(p_ref, o_ref):
        acc = p_ref[0]
        for s in range(1, slots):
            acc = acc + p_ref[s]
        o_ref[...] = acc

    return pl.pallas_call(
        body, name=name, grid=(rows // tr,),
        in_specs=[pl.BlockSpec((slots, tr, cols), lambda i: (0, i, 0))],
        out_specs=pl.BlockSpec((tr, cols), lambda i: (i, 0)),
        out_shape=jax.ShapeDtypeStruct((rows, cols), parts.dtype),
        compiler_params=_params("parallel"),
    )(parts)


def _pad_rows(flat, cols, multiple):
    n = flat.shape[0]
    rows = -(-n // cols)
    rows = -(-rows // multiple) * multiple
    return jnp.pad(flat, (0, rows * cols - n)).reshape(rows, cols)


def _cols_to_chunks(full):
    lead = full.shape[:-1]
    c = full.shape[-1] // N_DEV
    x = full.reshape(-1, N_DEV, c)
    return jnp.transpose(x, (1, 0, 2)).reshape(N_DEV, -1)


def _chunks_to_cols(chunks, lead):
    n = 1
    for s in lead:
        n *= s
    c = chunks.shape[1] // n
    x = chunks.reshape(N_DEV, n, c)
    return jnp.transpose(x, (1, 0, 2)).reshape(tuple(lead) + (N_DEV * c,))


def kernel(x, meta_tokens, norm_mix_g, norm_ffn_g, final_norm_g, sc_w_in, sc_conv_w, sc_w_out, rg_w_in, rg_conv_w, rg_conv_b, rg_w_gate_a, rg_b_gate_a, rg_w_gate_x, rg_b_gate_x, rg_lambda, rg_w_out, ffn_w_up, ffn_conv_w, ffn_w_down, loss_target, m_meta_tokens, m_norm_mix_g, m_norm_ffn_g, m_final_norm_g, m_sc_w_in, m_sc_conv_w, m_sc_w_out, m_rg_w_in, m_rg_conv_w, m_rg_conv_b, m_rg_w_gate_a, m_rg_b_gate_a, m_rg_w_gate_x, m_rg_b_gate_x, m_rg_lambda, m_rg_w_out, m_ffn_w_up, m_ffn_conv_w, m_ffn_w_down, v_meta_tokens, v_norm_mix_g, v_norm_ffn_g, v_final_norm_g, v_sc_w_in, v_sc_conv_w, v_sc_w_out, v_rg_w_in, v_rg_conv_w, v_rg_conv_b, v_rg_w_gate_a, v_rg_b_gate_a, v_rg_w_gate_x, v_rg_b_gate_x, v_rg_lambda, v_rg_w_out, v_ffn_w_up, v_ffn_conv_w, v_ffn_w_down):
    weights = dict(meta_tokens=meta_tokens, norm_mix_g=norm_mix_g, norm_ffn_g=norm_ffn_g, final_norm_g=final_norm_g,
                   sc_w_in=sc_w_in, sc_conv_w=sc_conv_w, sc_w_out=sc_w_out, rg_w_in=rg_w_in, rg_conv_w=rg_conv_w,
                   rg_conv_b=rg_conv_b, rg_w_gate_a=rg_w_gate_a, rg_b_gate_a=rg_b_gate_a, rg_w_gate_x=rg_w_gate_x,
                   rg_b_gate_x=rg_b_gate_x, rg_lambda=rg_lambda, rg_w_out=rg_w_out, ffn_w_up=ffn_w_up,
                   ffn_conv_w=ffn_conv_w, ffn_w_down=ffn_w_down)
    m_in = dict(meta_tokens=m_meta_tokens, norm_mix_g=m_norm_mix_g, norm_ffn_g=m_norm_ffn_g, final_norm_g=m_final_norm_g,
                sc_w_in=m_sc_w_in, sc_conv_w=m_sc_conv_w, sc_w_out=m_sc_w_out, rg_w_in=m_rg_w_in, rg_conv_w=m_rg_conv_w,
                rg_conv_b=m_rg_conv_b, rg_w_gate_a=m_rg_w_gate_a, rg_b_gate_a=m_rg_b_gate_a, rg_w_gate_x=m_rg_w_gate_x,
                rg_b_gate_x=m_rg_b_gate_x, rg_lambda=m_rg_lambda, rg_w_out=m_rg_w_out, ffn_w_up=m_ffn_w_up,
                ffn_conv_w=m_ffn_conv_w, ffn_w_down=m_ffn_w_down)
    v_in = dict(meta_tokens=v_meta_tokens, norm_mix_g=v_norm_mix_g, norm_ffn_g=v_norm_ffn_g, final_norm_g=v_final_norm_g,
                sc_w_in=v_sc_w_in, sc_conv_w=v_sc_conv_w, sc_w_out=v_sc_w_out, rg_w_in=v_rg_w_in, rg_conv_w=v_rg_conv_w,
                rg_conv_b=v_rg_conv_b, rg_w_gate_a=v_rg_w_gate_a, rg_b_gate_a=v_rg_b_gate_a, rg_w_gate_x=v_rg_w_gate_x,
                rg_b_gate_x=v_rg_b_gate_x, rg_lambda=v_rg_lambda, rg_w_out=v_rg_w_out, ffn_w_up=v_ffn_w_up,
                ffn_conv_w=v_ffn_conv_w, ffn_w_down=v_ffn_w_down)
    names = list(weights)

    seq, d = x.shape[1], x.shape[2]
    n_meta = meta_tokens.shape[0]
    n_ffn = ffn_w_up.shape[0]

    assert n_ffn == 2
    def shard_rows(w_in, w_out):
        return [w_in.T.astype(MXU_DTYPE), w_out.astype(MXU_DTYPE)]

    def both_orientations(in_t_gathered, out_gathered):
        w_in_t = in_t_gathered.reshape(-1, d)
        w_out = out_gathered.reshape(-1, d)
        return w_in_t.T, w_in_t, w_out, w_out.T

    shards = {"ffn0": shard_rows(ffn_w_up[0], ffn_w_down[0]), "rg": shard_rows(rg_w_in[0], rg_w_out[0]),
              "ffn1": shard_rows(ffn_w_up[1], ffn_w_down[1])}
    ffn_w_up_f, ffn_w_up_t, ffn_w_down_f, ffn_w_down_t = [None] * 2, [None] * 2, [None] * 2, [None] * 2

    small_names = ["meta_tokens", "sc_conv_w", "rg_conv_w", "rg_conv_b", "rg_b_gate_a", "rg_b_gate_x", "rg_lambda",
                   "ffn_conv_w"]
    small_lead = {n: weights[n].shape[:-1] for n in small_names}
    small_sizes = [weights[n].size for n in small_names]
    small_flat = jnp.concatenate([weights[n].reshape(-1) for n in small_names])
    small_rows = _pad_rows(small_flat, d, SUBLANES)
    sc_in_g, sc_out_g, small_g = _ride_alone(
        _GatherRider(shard_rows(sc_w_in[0], sc_w_out[0]) + [small_rows]), "gather_first")
    sc_w_in_f, sc_w_in_t, sc_w_out_f, sc_w_out_t = both_orientations(sc_in_g, sc_out_g)
    small_g = small_g.reshape(N_DEV, -1)
    small_full = {}
    o = 0
    for n, sz in zip(small_names, small_sizes):
        small_full[n] = _chunks_to_cols(small_g[:, o:o + sz], small_lead[n])
        o += sz

    wg, wgt = _pair_gate_weights(rg_w_gate_a[0].astype(MXU_DTYPE), rg_w_gate_x[0].astype(MXU_DTYPE))
    rg_cw, rg_cb = small_full["rg_conv_w"][0], small_full["rg_conv_b"]
    rg_ba, rg_bx, rg_lam = small_full["rg_b_gate_a"], small_full["rg_b_gate_x"], small_full["rg_lambda"]
    sc_cw = small_full["sc_conv_w"][0]
    ffn_cw = small_full["ffn_conv_w"]

    tp = _row_tile(n_meta + seq, ROW_TILE_PERM)

    def ffn_fwd(h, l, rider):
        n, a, z, *gathered = _ffn_up_fused(h, norm_ffn_g[l:l + 1], ffn_w_up_f[l], ffn_cw[l], tp, f"ffn{l}_up", rider)
        return _matmul_residual(z, ffn_w_down_f[l], h, f"ffn{l}_down"), (n, a, z), gathered

    h0, n0, p0, q0, h1, *ffn0_w = _sc_in_fused(x[0], small_full["meta_tokens"], norm_mix_g[0:1], sc_w_in_f, sc_cw,
                                               sc_w_out_f, tp, "sc_mixer", _GatherRider(shards["ffn0"]))
    ffn_w_up_f[0], ffn_w_up_t[0], ffn_w_down_f[0], ffn_w_down_t[0] = both_orientations(*ffn0_w)
    h2, ffn0_saved, rg_w = ffn_fwd(h1, 0, _GatherRider(shards["rg"]))
    rg_w_in_f, rg_w_in_t, rg_w_out_f, rg_w_out_t = both_orientations(*rg_w)
    n2, p2, y2, hs2, *rest2 = _rg_in_fused(h2, norm_mix_g[1:2], rg_w_in_f, rg_cw, rg_cb, wg, rg_ba, rg_bx, rg_lam,
                                           rg_w_out_f, tp, "rg_mixer", _GatherRider(shards["ffn1"]))
    rg_gates, h3, ffn1_w = rest2[:4], rest2[4], rest2[5:]
    ffn_w_up_f[1], ffn_w_up_t[1], ffn_w_down_f[1], ffn_w_down_t[1] = both_orientations(*ffn1_w)
    ffn1_saved = tuple(_ffn_up_fused(h3, norm_ffn_g[1:2], ffn_w_up_f[1], ffn_cw[1], tp, "ffn1_up"))
    dh4, loss_tile, d_final_g = _down_loss_head(ffn1_saved[2], ffn_w_down_f[1], h3, final_norm_g.reshape(1, d),
                                                loss_target[0], n_meta, tp, "ffn1_down_loss_head")
    loss = lax.psum(loss_tile[0, 0], AXES)

    arrived = {}

    def ffn_bwd(dh_out, h_in, saved, l):
        n, a, z = saved
        da, dhb, d_cw = _ffn_down_bwd_fused(dh_out, ffn_w_down_t[l], a, ffn_cw[l], tp, f"ffn{l}_down_bwd")
        d_w_down = _wgrad(z, dhb, f"ffn{l}_down_wgrad")
        d_w_up_t, arrived[f"ffn_w_down{l}"] = _wgrad(da, n, f"ffn{l}_up_wgrad", _ScatterRider(d_w_down))
        rows = d_w_up_t.shape[0] // N_DEV
        first_rows = (rows // 2) // (2 * SUBLANES) * (2 * SUBLANES)
        dh_in, d_g, arrived[f"ffn_w_up{l}a"] = _dgrad_in_norm(
            da, ffn_w_up_t[l], h_in, norm_ffn_g[l:l + 1], dh_out, f"ffn{l}_up_dgrad",
            _ScatterRider(d_w_up_t, (0, first_rows)))
        return dh_in, d_cw, d_g, _ScatterRider(d_w_up_t, (first_rows, rows - first_rows))

    dh3, d_fcw1, d_fg1, up1_rest = ffn_bwd(dh4, h3, ffn1_saved, 1)

    dp2, dhb3, d_rg_cw, d_rg_cb, d_wg, d_rg_ba, d_rg_bx, d_rg_lam, arrived["ffn_w_up1b"] = _rg_out_bwd_fused(
        dh3, rg_w_out_t, p2, hs2, rg_gates, rg_cw, rg_cb, wg, wgt, rg_lam, tp, "rg_out_bwd", up1_rest)
    d_wa, d_wx = _unpair_gate_grads(d_wg, rg_w_gate_a.shape[2])
    d_rg_w_out = _wgrad(y2, dhb3, "rg_out_wgrad")
    d_rg_w_in_t, arrived["rg_w_out"] = _wgrad(dp2, n2, "rg_in_wgrad", _ScatterRider(d_rg_w_out))
    dh2, d_mg1, arrived["rg_w_in"] = _dgrad_in_norm(
        dp2, rg_w_in_t, h2, norm_mix_g[1:2], dh3, "rg_in_dgrad", _ScatterRider(d_rg_w_in_t))

    dh1, d_fcw0, d_fg0, up0_rest = ffn_bwd(dh2, h1, ffn0_saved, 0)

    dp0, dhb1, d_sc_cw, arrived["ffn_w_up0b"] = _sc_out_bwd_fused(dh1, sc_w_out_t, p0, sc_cw, tp, "sc_out_bwd", up0_rest)
    d_sc_w_out = _wgrad(q0, dhb1, "sc_out_wgrad")
    d_sc_w_in_t, arrived["sc_w_out"] = _wgrad(dp0, n0, "sc_in_wgrad", _ScatterRider(d_sc_w_out))
    d_mg0, d_meta, d_x, arrived["sc_w_in"] = _dgrad_in_to_input(
        dp0, sc_w_in_t, h0, norm_mix_g[0:1], dh1, n_meta, tp, "sc_in_dgrad", _ScatterRider(d_sc_w_in_t))
    grad_x = d_x[None]

    grads = {}
    for n in ("sc_w_in", "rg_w_in"):
        grads[n] = _sum_slots(arrived[n], f"sum_{n}").T[None]
    for n in ("sc_w_out", "rg_w_out"):
        grads[n] = _sum_slots(arrived[n], f"sum_{n}")[None]
    grads["ffn_w_up"] = jnp.stack([jnp.concatenate(
        [_sum_slots(arrived[f"ffn_w_up{l}{part}"], f"sum_ffn_w_up{l}{part}") for part in "ab"], axis=0).T
        for l in range(n_ffn)])
    grads["ffn_w_down"] = jnp.stack([_sum_slots(arrived[f"ffn_w_down{l}"], f"sum_ffn_w_down{l}") for l in range(n_ffn)])

    small_grads = {"meta_tokens": d_meta,"sc_conv_w": d_sc_cw[None], "rg_conv_w": d_rg_cw[None],
                   "rg_conv_b": d_rg_cb, "rg_b_gate_a": d_rg_ba, "rg_b_gate_x": d_rg_bx, "rg_lambda": d_rg_lam,
                   "ffn_conv_w": jnp.stack([d_fcw0, d_fcw1])}
    small_chunks = jnp.concatenate([_cols_to_chunks(small_grads[n]) for n in small_names], axis=1)
    pad = small_rows.shape[0] * d - small_chunks.shape[1]
    small_chunks = jnp.pad(small_chunks, ((0, 0), (0, pad))).reshape(N_DEV, small_rows.shape[0], d)
    rep_names = ["norm_mix_g", "norm_ffn_g", "final_norm_g", "rg_w_gate_a", "rg_w_gate_x"]
    rep_grads = {"norm_mix_g": jnp.concatenate([d_mg0, d_mg1], axis=0),
                 "norm_ffn_g": jnp.concatenate([d_fg0, d_fg1], axis=0),
                 "final_norm_g": d_final_g.reshape(-1), "rg_w_gate_a": d_wa[None], "rg_w_gate_x": d_wx[None]}
    rep_flat = jnp.concatenate([rep_grads[n].reshape(-1) for n in rep_names])
    rep_chunk_rows = -(-rep_flat.shape[0] // (N_DEV * d))
    rep_chunk_rows += -(small_rows.shape[0] + rep_chunk_rows) % 16
    rep_chunks = jnp.pad(rep_flat, (0, N_DEV * rep_chunk_rows * d - rep_flat.shape[0])).reshape(N_DEV, rep_chunk_rows, d)

    last_chunks = jnp.concatenate([small_chunks, rep_chunks], axis=1)
    reduced = _sum_slots(_ride_alone(_ScatterRider(last_chunks.reshape(-1, d)), "scatter_last")[0], "sum_last")

    small_red = reduced[0:small_rows.shape[0]].reshape(-1)
    o = small_rows.shape[0]
    so = 0
    for n, sz in zip(small_names, small_sizes):
        grads[n] = small_red[so:so + sz].reshape(weights[n].shape)
        so += sz
    rep_red = _ride_alone(_GatherRider([reduced[o:o + rep_chunks.shape[1]]]), "gather_replicated_grads")[0].reshape(-1)
    ro = 0
    for n in rep_names:
        sz = weights[n].size
        grads[n] = rep_red[ro:ro + sz].reshape(weights[n].shape)
        ro += sz

    delta, new_m, new_v = {}, {}, {}
    for n in names:
        delta[n], new_m[n], new_v[n] = _adamw_nd(weights[n], grads[n], m_in[n], v_in[n], f"adamw_{n}")

    return (loss, grad_x, *[grads[n] for n in names], *[delta[n] for n in names],
            *[new_m[n] for n in names], *[new_v[n] for n in names])
```

```python
import jax
import jax.numpy as jnp
from jax import lax
from jax.experimental import pallas as pl
from jax.experimental.pallas import tpu as pltpu

F32 = jnp.float32
MXU_DTYPE = jnp.bfloat16
RMS_EPS = 1e-6
RG_C = 8.0
ADAM_LR = 0.001
ADAM_B1 = 0.9
ADAM_B2 = 0.999
ADAM_EPS = 1e-08
ADAM_WD = 0.01
ADAM_STEP = 10

N_DEV = 8
AXES = ("x", "y", "c")
SUBLANES = 8
LANES = 128
VMEM_LIMIT_BYTES = 48 * 1024 * 1024
ROW_TILE_MATMUL = 700
VMEM_LIMIT_WIDE_BYTES = 58 * 1024 * 1024
ROW_TILE_WGRAD = 3300
COL_TILE_WGRAD = 1408
ROW_TILE_PERM = 400
STRIP = 256
STRIP_FFN = 4096

_TN = (((0,), (0,)), ((), ()))
_NN = (((1,), (0,)), ((), ()))


def _row_tile(t, target):
    best = None
    for tm in range(16, t + 1, 16):
        if t % tm == 0 and tm <= target:
            best = tm
    return best if best is not None else t


def _col_tile(n, target):
    best = None
    for tn in range(LANES, n + 1, LANES):
        if n % tn == 0 and tn <= target:
            best = tn
    return best if best is not None else n


def _params(*sem):
    return pltpu.CompilerParams(dimension_semantics=sem, vmem_limit_bytes=VMEM_LIMIT_BYTES)


def _dot(a, b, dims):
    return lax.dot_general(a, b, dims, preferred_element_type=F32)


def _sigmoid(x):
    return 1.0 / (1.0 + jnp.exp(-x))


def _sigmoid_tanh(x):
    return 0.5 * jnp.tanh(0.5 * x) + 0.5


def _gelu(x):
    c = 0.7978845608028654
    t = jnp.tanh(c * (x + 0.044715 * (x * x * x)))
    return 0.5 * x * (1.0 + t)


def _gelu_and_grad(x):
    c = 0.7978845608028654
    x2 = x * x
    t = jnp.tanh(c * (x + 0.044715 * (x2 * x)))
    half = 0.5 * (1.0 + t)
    return x * half, half + 0.5 * x * (1.0 - t * t) * c * (1.0 + 3.0 * 0.044715 * x2)


def _softplus(x):
    return jnp.maximum(x, 0.0) + jnp.log1p(jnp.exp(-jnp.abs(x)))


def _time_scratch(tm, d):
    return [pltpu.VMEM((2, tm, d), F32), pltpu.VMEM((d // LANES, tm, LANES), F32), pltpu.SemaphoreType.DMA((2,))]


def _fetch_time_tile(i, nt, src_ref, head, tbuf, sems, tm):
    n_head = head.shape[0]

    def tile(j, slot):
        start = pl.multiple_of(j * tm - n_head, SUBLANES)
        return pltpu.make_async_copy(src_ref.at[pl.ds(start, tm)], tbuf.at[slot], sems.at[slot])

    first = pltpu.make_async_copy(src_ref.at[pl.ds(0, tm - n_head)], tbuf.at[0, pl.ds(n_head, tm - n_head)], sems.at[0])

    @pl.when(i == 0)
    def _():
        tbuf[0, 0:n_head, :] = head
        first.start()

    @pl.when(i + 1 < nt)
    def _():
        tile(i + 1, (i + 1) % 2).start()

    @pl.when(i == 0)
    def _():
        first.wait()

    @pl.when(i > 0)
    def _():
        tile(i, i % 2).wait()


def _time_to_tile_order(t_ref, slabs, out_ref):
    tm, d = out_ref.shape
    groups = tm // SUBLANES
    for k in range(d // LANES):
        slabs[k] = t_ref[:, k * LANES:(k + 1) * LANES]
    for k in range(d // LANES):
        for g in range(groups):
            out_ref[g * SUBLANES:(g + 1) * SUBLANES, k * LANES:(k + 1) * LANES] = slabs[k, pl.ds(g, SUBLANES, stride=groups), :]


def _tile_to_time_order(p_ref, slabs, t_ref):
    tm, d = p_ref.shape
    groups = tm // SUBLANES
    for k in range(d // LANES):
        for g in range(groups):
            slabs[k, pl.ds(g, SUBLANES, stride=groups), :] = p_ref[g * SUBLANES:(g + 1) * SUBLANES, k * LANES:(k + 1) * LANES]
    for k in range(d // LANES):
        t_ref[:, k * LANES:(k + 1) * LANES] = slabs[k]


def _rows(ref, sl):
    return [ref[k:k + 1, sl] for k in range(ref.shape[0])]


def _shift_down(x, before, s):
    if s == 0:
        return x
    n = x.shape[0]
    row = lax.broadcasted_iota(jnp.int32, (SUBLANES, x.shape[1]), 0)
    heads = []
    for g in range(s):
        v = x[n - (s - g) * SUBLANES:n - (s - g - 1) * SUBLANES]
        heads.append(pltpu.roll(jnp.where(row == SUBLANES - 1, before[s - g - 1], v), 1, axis=0))
    return jnp.concatenate(heads + [x[0:n - s * SUBLANES]], axis=0)


def _shift_up(x, after, s):
    if s == 0:
        return x
    row = lax.broadcasted_iota(jnp.int32, (SUBLANES, x.shape[1]), 0)
    tails = []
    for m in range(s):
        v = x[m * SUBLANES:(m + 1) * SUBLANES]
        tails.append(pltpu.roll(jnp.where(row == 0, after[m], v), SUBLANES - 1, axis=0))
    return jnp.concatenate([x[s * SUBLANES:]] + tails, axis=0)


def _weighted(w, windows):
    y = w[0] * windows[0]
    for k in range(1, len(w)):
        y = y + w[k] * windows[k]
    return y


def _windows(x, before, k_width):
    return [_shift_down(x, before, k_width - 1 - k) for k in range(k_width)]


def _conv_t(dy, after, w):
    k_width = len(w)
    return _weighted(w, [_shift_up(dy, after, k_width - 1 - k) for k in range(k_width)])


def _blocks(ref, sl, count, newest_first):
    n = ref.shape[0] // SUBLANES
    order = range(n - 1, n - 1 - count, -1) if newest_first else range(count)
    return [ref[b * SUBLANES:(b + 1) * SUBLANES, sl] for b in order]


def _halo_specs(tm, cols, count, tile_of):
    def spec(k):
        return pl.BlockSpec((SUBLANES, cols), lambda i: (jnp.maximum(tile_of(i) * (tm // SUBLANES) - k, 0), 0))
    return [spec(k) for k in range(1, count + 1)]


def _scan_tile(coef, val, out, carry, groups, reverse):
    cols = coef.shape[1]
    row = lax.broadcasted_iota(jnp.int32, (SUBLANES, cols), 0)

    def blk(i):
        g = groups - 1 - i if reverse else i
        return pl.ds(pl.multiple_of(g * SUBLANES, SUBLANES), SUBLANES)

    def local(i, pl_):
        p_prev, l_prev = pl_
        a = coef[blk(i), :]
        p = a * p_prev
        l = a * l_prev + val[blk(i), :]
        coef[blk(i), :] = p
        val[blk(i), :] = l
        return p, l

    pf, lf = lax.fori_loop(0, groups, local, (jnp.ones((SUBLANES, cols), F32), jnp.zeros((SUBLANES, cols), F32)))
    for s in (1, 2, 4):
        keep, sh = (row < SUBLANES - s, SUBLANES - s) if reverse else (row >= s, s)
        p_s = jnp.where(keep, pltpu.roll(pf, sh, axis=0), 1.0)
        l_s = jnp.where(keep, pltpu.roll(lf, sh, axis=0), 0.0)
        lf = pf * l_s + lf
        pf = pf * p_s
    end = lf + pf * carry
    if reverse:
        init = jnp.where(row == SUBLANES - 1, carry, pltpu.roll(end, SUBLANES - 1, axis=0))
        leaving = jnp.broadcast_to(end[0:1, :], (SUBLANES, cols))
    else:
        init = jnp.where(row == 0, carry, pltpu.roll(end, 1, axis=0))
        leaving = jnp.broadcast_to(end[SUBLANES - 1:SUBLANES, :], (SUBLANES, cols))

    def fix(i, _):
        out[blk(i), :] = val[blk(i), :] + coef[blk(i), :] * init
        return 0

    lax.fori_loop(0, groups, fix, 0)
    return leaving


def _resident(shape):
    return pl.BlockSpec(shape, lambda *_: (0,) * len(shape), pipeline_mode=pl.Buffered(1))


def _full(shape):
    return pl.BlockSpec(shape, lambda *_: (0,) * len(shape))


def _rmsnorm_to(h_ref, g_ref, n_ref):
    x = h_ref[...]
    ms = jnp.mean(x * x, axis=-1, keepdims=True)
    n_ref[...] = (x * lax.rsqrt(ms + RMS_EPS) * g_ref[...]).astype(n_ref.dtype)


_ANY = pl.BlockSpec(memory_space=pl.ANY)
_MESH = pl.DeviceIdType.MESH


def _dma_sems(n):
    return pltpu.SemaphoreType.DMA((n,))


def _when_each(*phases):
    for cond, fn in phases:
        if cond is not None:
            pl.when(cond)(fn)


class _NoRider:
    inputs = in_specs = out_shape = out_specs = scratch_shapes = ()

    def __call__(self, first, middle, last, ins, outs, scratch):
        pass


_NO_RIDER = _NoRider()


def _hand_on_step(nt):
    return (3 * nt) // 4


def _split_refs(rider, n_out, n_scratch, rest):
    a = len(rider.inputs)
    b = a + n_out
    c = b + len(rider.out_shape)
    e = c + n_scratch
    return rest[:a], rest[a:b], rest[b:c], rest[c:e], rest[e:]


class _GatherRider:
    def __init__(self, blocks):
        n = len(blocks)
        self.inputs = tuple(blocks)
        self.in_specs = (_ANY,) * n
        self.out_shape = tuple(jax.ShapeDtypeStruct((N_DEV,) + b.shape, b.dtype) for b in blocks)
        self.out_specs = (_ANY,) * n
        self.scratch_shapes = (_dma_sems(7 * n), _dma_sems(7 * n), _dma_sems(n))

    def __call__(self, first, middle, last, ins, outs, scratch):
        n = len(ins)
        send_sems, recv_sems, local_sems = scratch
        x, y, c = lax.axis_index("x"), lax.axis_index("y"), lax.axis_index("c")
        me, sibling = (x, y, c), (x, y, 1 - c)
        chips = [(1 - x, y), (x, 1 - y), (1 - x, 1 - y)]

        def slot(b, px, py, pc):
            return outs[b].at[4 * px + 2 * py + pc]

        def copy(k, b, block, to, src=None):
            return pltpu.make_async_remote_copy(
                src_ref=slot(b, *block) if src is None else src, dst_ref=slot(b, *block),
                send_sem=send_sems.at[k * n + b], recv_sem=recv_sems.at[k * n + b], device_id=to, device_id_type=_MESH)

        mine = [pltpu.make_async_copy(ins[b], slot(b, *me), local_sems.at[b]) for b in range(n)]
        own = [copy(0, b, me, sibling, src=ins[b]) for b in range(n)]
        own += [copy(1 + j, b, me, (*chip, c), src=ins[b]) for j, chip in enumerate(chips) for b in range(n)]
        passed = [[copy(4 + j, b, (*chip, c), sibling) for b in range(n)] for j, chip in enumerate(chips)]

        def at_first():
            for cp in mine + own:
                cp.start()

        def at_middle():
            for j, chip in enumerate(chips):
                for b in range(n):
                    copy(1 + j, b, (*chip, c), me).wait_recv()
                    passed[j][b].start()

        def at_last():
            for b in range(n):
                copy(0, b, sibling, me).wait_recv()
                for j, chip in enumerate(chips):
                    copy(4 + j, b, (*chip, 1 - c), me).wait_recv()
            for cp in own + [cp for group in passed for cp in group]:
                cp.wait_send()
            for cp in mine:
                cp.wait()

        _when_each((first, at_first), (middle, at_middle), (last, at_last))


class _ScatterRider:
    def __init__(self, grad, rows=None):
        r = grad.shape[0] // N_DEV
        chunks = grad.reshape(N_DEV, r, grad.shape[1])
        self.rows = rows if rows is not None else (0, r)
        self.inputs = (chunks,)
        self.in_specs = (_ANY,)
        self.out_shape = (jax.ShapeDtypeStruct((N_DEV, self.rows[1], grad.shape[1]), chunks.dtype),)
        self.out_specs = (_ANY,)
        self.scratch_shapes = (_dma_sems(N_DEV - 1), _dma_sems(N_DEV - 1), pltpu.SemaphoreType.DMA(()))

    def __call__(self, first, middle, last, ins, outs, scratch):
        (g_ref,), (r_ref,) = ins, outs
        send_sems, recv_sems, local_sem = scratch
        x, y, c = lax.axis_index("x"), lax.axis_index("y"), lax.axis_index("c")
        me = 4 * x + 2 * y + c
        part = pl.ds(*self.rows)
        mine = pltpu.make_async_copy(g_ref.at[me, part], r_ref.at[me], local_sem)
        copies = []
        for k in range(1, N_DEV):
            px, py, pc = (1 - x if k & 4 else x), (1 - y if k & 2 else y), (1 - c if k & 1 else c)
            copies.append(pltpu.make_async_remote_copy(
                src_ref=g_ref.at[4 * px + 2 * py + pc, part], dst_ref=r_ref.at[me],
                send_sem=send_sems.at[k - 1], recv_sem=recv_sems.at[k - 1],
                device_id=(px, py, pc), device_id_type=_MESH))

        def at_first():
            mine.start()
            for cp in copies:
                cp.start()

        def at_last():
            for cp in copies:
                cp.wait()
            mine.wait()

        _when_each((first, at_first), (last, at_last))


def _matmul_residual(q, w, h, name):
    t, k = q.shape
    d = w.shape[1]
    tm = _row_tile(t, ROW_TILE_MATMUL)

    def body(q_ref, w_ref, h_ref, o_ref):
        o_ref[...] = h_ref[...] + _dot(q_ref[...], w_ref[...], _NN)

    return pl.pallas_call(
        body, name=name, grid=(t // tm,),
        in_specs=[pl.BlockSpec((tm, k), lambda i: (i, 0)),
                  _resident((k, d)),
                  pl.BlockSpec((tm, d), lambda i: (i, 0))],
        out_specs=pl.BlockSpec((tm, d), lambda i: (i, 0)),
        out_shape=jax.ShapeDtypeStruct((t, d), F32),
        compiler_params=_params("parallel"),
    )(q, w, h)


def _dgrad_in_norm(dp, wt, h, g, dh_next, name, rider=_NO_RIDER):
    t, n = dp.shape
    d = wt.shape[1]
    tm = _row_tile(t, ROW_TILE_MATMUL)
    nt = t // tm

    def body(dp_ref, wt_ref, h_ref, g_ref, dhn_ref, *rest):
        r_in, (dh_ref, dg_ref), r_out, _, r_scratch = _split_refs(rider, 2, 0, rest)
        i = pl.program_id(0)
        rider(i == 0, i == nt // 2, None, r_in, r_out, r_scratch)
        dn = _dot(dp_ref[...], wt_ref[...], _NN)
        x = h_ref[...]
        ms = jnp.mean(x * x, axis=-1, keepdims=True)
        r = lax.rsqrt(ms + RMS_EPS)
        xhat = x * r
        dng = dn * g_ref[...]
        c = jnp.mean(dng * xhat, axis=-1, keepdims=True)
        dh_ref[...] = dhn_ref[...] + r * (dng - xhat * c)
        part = jnp.sum(dn * xhat, axis=0, keepdims=True)

        @pl.when(i == 0)
        def _():
            dg_ref[...] = part

        @pl.when(i > 0)
        def _():
            dg_ref[...] += part

        rider(None, None, i == nt - 1, r_in, r_out, r_scratch)

    return pl.pallas_call(
        body, name=name, grid=(nt,),
        in_specs=[pl.BlockSpec((tm, n), lambda i: (i, 0)),
                  _resident((n, d)),
                  pl.BlockSpec((tm, d), lambda i: (i, 0)),
                  pl.BlockSpec((1, d), lambda i: (0, 0)),
                  pl.BlockSpec((tm, d), lambda i: (i, 0))] + list(rider.in_specs),
        out_specs=[pl.BlockSpec((tm, d), lambda i: (i, 0)),
                   pl.BlockSpec((1, d), lambda i: (0, 0))] + list(rider.out_specs),
        out_shape=[jax.ShapeDtypeStruct((t, d), F32), jax.ShapeDtypeStruct((1, d), F32)] + list(rider.out_shape),
        scratch_shapes=list(rider.scratch_shapes),
        compiler_params=_params("arbitrary"),
    )(dp, wt, h, g, dh_next, *rider.inputs)


def _dgrad_in_to_input(dp, wt, h, g, dh_next, n_head, tm, name, rider=_NO_RIDER):
    t, n = dp.shape
    d = wt.shape[1]
    nt = t // tm

    def body(dp_ref, wt_ref, h_ref, g_ref, dhn_ref, *rest):
        r_in, (dg_ref, head_ref, rest_ref), r_out, (pbuf, tout, slabs, sems), r_scratch = _split_refs(rider, 3, 4, rest)
        i = pl.program_id(0)
        rider(i == 0, i == nt // 2, None, r_in, r_out, r_scratch)
        dn = _dot(dp_ref[...], wt_ref[...], _NN)
        x = h_ref[...]
        ms = jnp.mean(x * x, axis=-1, keepdims=True)
        r = lax.rsqrt(ms + RMS_EPS)
        xhat = x * r
        dng = dn * g_ref[...]
        c = jnp.mean(dng * xhat, axis=-1, keepdims=True)
        pbuf[...] = dhn_ref[...] + r * (dng - xhat * c)
        part = jnp.sum(dn * xhat, axis=0, keepdims=True)

        @pl.when(i == 0)
        def _():
            dg_ref[...] = part

        @pl.when(i > 0)
        def _():
            dg_ref[...] += part

        def store(j):
            if isinstance(j, int) and j == 0:
                return pltpu.make_async_copy(tout.at[0, pl.ds(n_head, tm - n_head)],
                                             rest_ref.at[pl.ds(0, tm - n_head)], sems.at[0])
            start = j * tm - n_head
            start = start if isinstance(start, int) else pl.multiple_of(start, SUBLANES)
            return pltpu.make_async_copy(tout.at[j % 2], rest_ref.at[pl.ds(start, tm)], sems.at[j % 2])

        @pl.when(i == 2)
        def _():
            store(0).wait()

        @pl.when(i > 2)
        def _():
            store(i - 2).wait()

        _tile_to_time_order(pbuf, slabs, tout.at[i % 2])

        @pl.when(i == 0)
        def _():
            head_ref[...] = tout[0, 0:n_head, :]
            store(0).start()

        @pl.when(i > 0)
        def _():
            store(i).start()

        @pl.when(i == nt - 1)
        def _():
            for j in (nt - 2, nt - 1):
                if j >= 0:
                    store(j).wait()

        rider(None, None, i == nt - 1, r_in, r_out, r_scratch)

    return pl.pallas_call(
        body, name=name, grid=(nt,),
        in_specs=[pl.BlockSpec((tm, n), lambda i: (i, 0)),
                  _resident((n, d)),
                  pl.BlockSpec((tm, d), lambda i: (i, 0)),
                  pl.BlockSpec((1, d), lambda i: (0, 0)),
                  pl.BlockSpec((tm, d), lambda i: (i, 0))] + list(rider.in_specs),
        out_specs=[pl.BlockSpec((1, d), lambda i: (0, 0)), _full((n_head, d)), _ANY] + list(rider.out_specs),
        out_shape=[jax.ShapeDtypeStruct((1, d), F32), jax.ShapeDtypeStruct((n_head, d), F32),
                   jax.ShapeDtypeStruct((t - n_head, d), F32)] + list(rider.out_shape),
        scratch_shapes=[pltpu.VMEM((tm, d), F32)] + _time_scratch(tm, d) + list(rider.scratch_shapes),
        compiler_params=_params("arbitrary"),
    )(dp, wt, h, g, dh_next, *rider.inputs)


def _wgrad(a, b, name, rider=_NO_RIDER):
    t, m = a.shape
    d = b.shape[1]
    tmm, tk = _col_tile(m, COL_TILE_WGRAD), _row_tile(t, ROW_TILE_WGRAD)
    nm, nk = m // tmm, t // tk

    def body(a_ref, b_ref, *rest):
        r_in, (o_ref,), r_out, _, r_scratch = _split_refs(rider, 1, 0, rest)
        i, k = pl.program_id(0), pl.program_id(1)
        rider(jnp.logical_and(i == 0, k == 0), None, None, r_in, r_out, r_scratch)

        @pl.when(k == 0)
        def _():
            o_ref[...] = jnp.zeros_like(o_ref)

        o_ref[...] += _dot(a_ref[...], b_ref[...], _TN)
        rider(None, None, jnp.logical_and(i == nm - 1, k == nk - 1), r_in, r_out, r_scratch)

    outs = pl.pallas_call(
        body, name=name, grid=(nm, nk),
        in_specs=[pl.BlockSpec((tk, tmm), lambda i, k: (k, i)),
                  pl.BlockSpec((tk, d), lambda i, k: (k, 0))] + list(rider.in_specs),
        out_specs=[pl.BlockSpec((tmm, d), lambda i, k: (i, 0))] + list(rider.out_specs),
        out_shape=[jax.ShapeDtypeStruct((m, d), F32)] + list(rider.out_shape),
        scratch_shapes=list(rider.scratch_shapes),
        compiler_params=pltpu.CompilerParams(dimension_semantics=("arbitrary", "arbitrary"),
                                             vmem_limit_bytes=VMEM_LIMIT_WIDE_BYTES),
    )(a, b, *rider.inputs)
    return outs if rider.out_shape else outs[0]


def _ffn_up_fused(h, g, w, conv_w, tm, name, rider=_NO_RIDER):
    t, d = h.shape
    f2 = w.shape[1]
    f = f2 // 2
    cw = min(STRIP_FFN, f)
    kw = conv_w.shape[0]
    nh = kw - 1
    nt = t // tm

    def body(h_ref, g_ref, w_ref, cw_ref, *rest):
        r_in, (n_ref, a_ref, z_ref), r_out, (carry,), r_scratch = _split_refs(rider, 3, 1, rest)
        i = pl.program_id(0)
        rider(i == 0, i == _hand_on_step(nt), None, r_in, r_out, r_scratch)

        @pl.when(i == 0)
        def _():
            carry[...] = jnp.zeros_like(carry)

        _rmsnorm_to(h_ref, g_ref, n_ref)
        for c in range(0, f, cw):
            conv = []
            for sl in (slice(c, c + cw), slice(f + c, f + c + cw)):
                a = _dot(n_ref[...], w_ref[:, sl], _NN)
                a_ref[:, sl] = a
                conv.append(_weighted(_rows(cw_ref, sl), _windows(a, _blocks(carry, sl, nh, True), kw)))
                carry[:, sl] = a[tm - nh * SUBLANES:tm]
            gg, vv = conv
            z_ref[:, c:c + cw] = (gg * _sigmoid(gg) * vv).astype(z_ref.dtype)
        rider(None, None, i == nt - 1, r_in, r_out, r_scratch)

    return pl.pallas_call(
        body, name=name, grid=(nt,),
        in_specs=[pl.BlockSpec((tm, d), lambda i: (i, 0)), pl.BlockSpec((1, d), lambda i: (0, 0)),
                  _resident((d, f2)), _full(conv_w.shape)] + list(rider.in_specs),
        out_specs=[pl.BlockSpec((tm, d), lambda i: (i, 0)), pl.BlockSpec((tm, f2), lambda i: (i, 0)),
                   pl.BlockSpec((tm, f), lambda i: (i, 0))] + list(rider.out_specs),
        out_shape=[jax.ShapeDtypeStruct((t, d), MXU_DTYPE), jax.ShapeDtypeStruct((t, f2), F32),
                   jax.ShapeDtypeStruct((t, f), MXU_DTYPE)] + list(rider.out_shape),
        scratch_shapes=[pltpu.VMEM((nh * SUBLANES, f2), F32)] + list(rider.scratch_shapes),
        compiler_params=_params("arbitrary"),
    )(h, g, w, conv_w, *rider.inputs)


def _ffn_down_bwd_fused(dh, wdt, a, conv_w, tm, name):
    t, d = dh.shape
    f = wdt.shape[1]
    f2 = 2 * f
    nt = t // tm
    cw = min(STRIP_FFN, f)
    kw = conv_w.shape[0]
    nh = kw - 1
    rev = lambda i: nt - 1 - i

    def body(dh_ref, wdt_ref, a_ref, *rest):
        halo_refs = rest[:nh]
        cw_ref, da_ref, dhb_ref, dw_ref, carry = rest[nh:]
        i = pl.program_id(0)
        newest, oldest = i == 0, i == nt - 1

        @pl.when(newest)
        def _():
            carry[...] = jnp.zeros_like(carry)
            dw_ref[...] = jnp.zeros_like(dw_ref)

        dhb_ref[...] = dh_ref[...].astype(dhb_ref.dtype)
        for c in range(0, f, cw):
            gsl, vsl = slice(c, c + cw), slice(f + c, f + c + cw)
            dz = _dot(dhb_ref[...], wdt_ref[:, gsl], _NN)
            win, conv = {}, {}
            for sl in (gsl, vsl):
                before = [jnp.where(oldest, 0.0, r[:, sl]) for r in halo_refs]
                win[sl.start] = _windows(a_ref[:, sl], before, kw)
                conv[sl.start] = _weighted(_rows(cw_ref, sl), win[sl.start])
            gg, vv = conv[gsl.start], conv[vsl.start]
            s = _sigmoid(gg)
            grads = {gsl.start: dz * vv * s * (1.0 + gg * (1.0 - s)), vsl.start: dz * gg * s}
            for sl in (gsl, vsl):
                dconv = grads[sl.start]
                da_ref[:, sl] = _conv_t(dconv, _blocks(carry, sl, nh, False), _rows(cw_ref, sl)).astype(da_ref.dtype)
                for k in range(kw):
                    dw_ref[k:k + 1, sl] += jnp.sum(dconv * win[sl.start][k], axis=0, keepdims=True)
                carry[:, sl] = dconv[0:nh * SUBLANES]

    return pl.pallas_call(
        body, name=name, grid=(nt,),
        in_specs=[pl.BlockSpec((tm, d), lambda i: (rev(i), 0)), _resident((d, f)),
                  pl.BlockSpec((tm, f2), lambda i: (rev(i), 0))] + _halo_specs(tm, f2, nh, rev) + [_full(conv_w.shape)],
        out_specs=[pl.BlockSpec((tm, f2), lambda i: (rev(i), 0)), pl.BlockSpec((tm, d), lambda i: (rev(i), 0)),
                   _full(conv_w.shape)],
        out_shape=[jax.ShapeDtypeStruct((t, f2), MXU_DTYPE), jax.ShapeDtypeStruct((t, d), MXU_DTYPE),
                   jax.ShapeDtypeStruct(conv_w.shape, F32)],
        scratch_shapes=[pltpu.VMEM((nh * SUBLANES, f2), F32)],
        compiler_params=_params("arbitrary"),
    )(dh, wdt, a, *([a] * nh), conv_w)


def _sc_in_fused(x, meta, g, w, conv_w, w_out, tm, name, rider=_NO_RIDER):
    d = x.shape[1]
    t = x.shape[0] + meta.shape[0]
    cw = min(STRIP, d)
    kw = conv_w.shape[0]
    nh = kw - 1
    nt = t // tm

    def body(x_ref, meta_ref, g_ref, w_ref, cw_ref, wo_ref, *rest):
        r_in, (h_ref, n_ref, p_ref, q_ref, h1_ref), r_out, (carry, tbuf, slabs, sems), r_scratch = _split_refs(
            rider, 5, 4, rest)
        i = pl.program_id(0)
        rider(i == 0, i == _hand_on_step(nt), None, r_in, r_out, r_scratch)

        @pl.when(i == 0)
        def _():
            carry[...] = jnp.zeros_like(carry)

        _fetch_time_tile(i, nt, x_ref, meta_ref[...], tbuf, sems, tm)
        _time_to_tile_order(tbuf.at[i % 2], slabs, h_ref)
        _rmsnorm_to(h_ref, g_ref, n_ref)
        for c in range(0, d, cw):
            sl = slice(c, c + cw)
            parts = []
            for base in (0, d, 2 * d):
                psl = slice(base + c, base + c + cw)
                parts.append(_dot(n_ref[...], w_ref[:, psl], _NN))
                p_ref[:, psl] = parts[-1]
            bg, cg, v = parts
            cv = cg * v
            u = _weighted(_rows(cw_ref, sl), _windows(cv, _blocks(carry, sl, nh, True), kw))
            carry[:, sl] = cv[tm - nh * SUBLANES:tm]
            q_ref[:, sl] = (bg * u).astype(q_ref.dtype)
        h1_ref[...] = h_ref[...] + _dot(q_ref[...], wo_ref[...], _NN)
        rider(None, None, i == nt - 1, r_in, r_out, r_scratch)

    row_tile = lambda cols: pl.BlockSpec((tm, cols), lambda i: (i, 0))
    return pl.pallas_call(
        body, name=name, grid=(nt,),
        in_specs=[_ANY, _full(meta.shape), pl.BlockSpec((1, d), lambda i: (0, 0)),
                  _resident((d, 3 * d)), _full(conv_w.shape), _resident(w_out.shape)] + list(rider.in_specs),
        out_specs=[row_tile(d), row_tile(d), row_tile(3 * d), row_tile(d), row_tile(d)] + list(rider.out_specs),
        out_shape=[jax.ShapeDtypeStruct((t, d), F32), jax.ShapeDtypeStruct((t, d), MXU_DTYPE),
                   jax.ShapeDtypeStruct((t, 3 * d), F32), jax.ShapeDtypeStruct((t, d), MXU_DTYPE),
                   jax.ShapeDtypeStruct((t, d), F32)] + list(rider.out_shape),
        scratch_shapes=[pltpu.VMEM((nh * SUBLANES, d), F32)] + _time_scratch(tm, d) + list(rider.scratch_shapes),
        compiler_params=_params("arbitrary"),
    )(x, meta, g, w, conv_w, w_out, *rider.inputs)


def _sc_out_bwd_fused(dh, wot, p, conv_w, tm, name, rider=_NO_RIDER):
    t, d = dh.shape
    nt = t // tm
    cw = min(STRIP, d)
    kw = conv_w.shape[0]
    nh = kw - 1
    rev = lambda i: nt - 1 - i

    def body(dh_ref, wot_ref, p_ref, *rest):
        halo_refs, cw_ref = rest[:nh], rest[nh]
        r_in, (dp_ref, dhb_ref, dw_ref), r_out, (carry,), r_scratch = _split_refs(rider, 3, 1, rest[nh + 1:])
        i = pl.program_id(0)
        newest, oldest = i == 0, i == nt - 1
        rider(newest, i == nt // 2, None, r_in, r_out, r_scratch)

        @pl.when(newest)
        def _():
            carry[...] = jnp.zeros_like(carry)
            dw_ref[...] = jnp.zeros_like(dw_ref)

        dhb_ref[...] = dh_ref[...].astype(dhb_ref.dtype)
        for c in range(0, d, cw):
            sl, csl, vsl = slice(c, c + cw), slice(d + c, d + c + cw), slice(2 * d + c, 2 * d + c + cw)
            w = _rows(cw_ref, sl)
            dq = _dot(dhb_ref[...], wot_ref[:, sl], _NN)
            bg, cg, v = p_ref[:, sl], p_ref[:, csl], p_ref[:, vsl]
            before = [jnp.where(oldest, 0.0, r[:, csl] * r[:, vsl]) for r in halo_refs]
            win = _windows(cg * v, before, kw)
            du = dq * bg
            dcv = _conv_t(du, _blocks(carry, sl, nh, False), w)
            dp_ref[:, sl] = (dq * _weighted(w, win)).astype(dp_ref.dtype)
            dp_ref[:, csl] = (dcv * v).astype(dp_ref.dtype)
            dp_ref[:, vsl] = (dcv * cg).astype(dp_ref.dtype)
            for k in range(kw):
                dw_ref[k:k + 1, sl] += jnp.sum(du * win[k], axis=0, keepdims=True)
            carry[:, sl] = du[0:nh * SUBLANES]
        rider(None, None, oldest, r_in, r_out, r_scratch)

    tile = lambda cols: pl.BlockSpec((tm, cols), lambda i: (rev(i), 0))
    return pl.pallas_call(
        body, name=name, grid=(nt,),
        in_specs=[tile(d), _resident((d, d)), tile(3 * d)] + _halo_specs(tm, 3 * d, nh, rev) + [_full(conv_w.shape)]
                 + list(rider.in_specs),
        out_specs=[tile(3 * d), tile(d), _full(conv_w.shape)] + list(rider.out_specs),
        out_shape=[jax.ShapeDtypeStruct((t, 3 * d), MXU_DTYPE), jax.ShapeDtypeStruct((t, d), MXU_DTYPE),
                   jax.ShapeDtypeStruct(conv_w.shape, F32)] + list(rider.out_shape),
        scratch_shapes=[pltpu.VMEM((nh * SUBLANES, d), F32)] + list(rider.scratch_shapes),
        compiler_params=_params("arbitrary"),
    )(dh, wot, p, *([p] * nh), conv_w, *rider.inputs)


def _pair_gate_weights(wa, wx):
    nb, bd, _ = wa.shape
    zero = jnp.zeros((nb // 2, bd, bd), wa.dtype)

    def pair(w):
        w = w.reshape(nb // 2, 2, bd, bd)
        top = jnp.concatenate([w[:, 0], zero], axis=2)
        bottom = jnp.concatenate([zero, w[:, 1]], axis=2)
        return jnp.concatenate([top, bottom], axis=1)

    both = jnp.concatenate([pair(wa), pair(wx)], axis=2)
    return both, jnp.swapaxes(both, 1, 2)


def _unpair_gate_grads(dw, bd):
    def blocks(cols):
        d0 = dw[:, :bd, cols:cols + bd]
        d1 = dw[:, bd:, cols + bd:cols + 2 * bd]
        return jnp.stack([d0, d1], axis=1).reshape(-1, bd, bd)
    return blocks(0), blocks(2 * bd)


def _rg_gates(u, k, wg_ref, ba, bx, lam):
    ub = u.astype(MXU_DTYPE)
    z = _dot(ub, wg_ref[k], _NN)
    half = z.shape[1] // 2
    r = _sigmoid_tanh(z[:, :half] + ba)
    ig = _sigmoid_tanh(z[:, half:] + bx)
    sp = _softplus(-lam)
    la = -RG_C * r * sp
    a = jnp.exp(la)
    th = jnp.tanh(la)
    mult = jnp.sqrt(-2.0 * th / (1.0 - th))
    return r, ig, a, mult


def _rg_in_fused(h, g, w, conv_w, conv_b, wg, ba, bx, lam, w_out, tm, name, rider=_NO_RIDER):
    t, d = h.shape
    r2 = w.shape[1]
    rr = r2 // 2
    nb, bd = wg.shape[0], wg.shape[1]
    kw = conv_w.shape[0]
    nh = kw - 1
    groups = tm // SUBLANES
    nt = t // tm

    def body(h_ref, g_ref, w_ref, cw_ref, cb_ref, wg_ref, ba_ref, bx_ref, lam_ref, wo_ref, *rest):
        (r_in, (n_ref, p_ref, y_ref, hs_ref, u_ref, r_ref, ig_ref, a_ref, mult_ref, hn_ref), r_out,
         (a_scr, b_scr, carry_rb, carry_h), r_scratch) = _split_refs(rider, 10, 4, rest)
        i = pl.program_id(0)
        rider(i == 0, i == _hand_on_step(nt), None, r_in, r_out, r_scratch)

        @pl.when(i == 0)
        def _():
            carry_rb[...] = jnp.zeros_like(carry_rb)
            carry_h[...] = jnp.zeros_like(carry_h)

        _rmsnorm_to(h_ref, g_ref, n_ref)
        for k in range(nb):
            sl = slice(k * bd, (k + 1) * bd)
            rsl = slice(rr + k * bd, rr + (k + 1) * bd)
            p_ref[:, sl] = _dot(n_ref[...], w_ref[:, sl], _NN)
            rb = _dot(n_ref[...], w_ref[:, rsl], _NN)
            p_ref[:, rsl] = rb
            u = _weighted(_rows(cw_ref, sl), _windows(rb, _blocks(carry_rb, sl, nh, True), kw)) + cb_ref[:, sl]
            carry_rb[:, sl] = rb[tm - nh * SUBLANES:tm]
            r, ig, a, mult = _rg_gates(u, k, wg_ref, ba_ref[:, sl], bx_ref[:, sl], lam_ref[:, sl])
            for ref, val in ((u_ref, u), (r_ref, r), (ig_ref, ig), (a_ref, a), (mult_ref, mult)):
                ref[:, sl] = val
            a_scr[:, sl] = a
            b_scr[:, sl] = mult * (ig * u)

        carry_h[...] = _scan_tile(a_scr, b_scr, hs_ref, carry_h[...], groups, reverse=False)

        for k in range(nb):
            sl = slice(k * bd, (k + 1) * bd)
            y_ref[:, sl] = (hs_ref[:, sl] * _gelu(p_ref[:, sl])).astype(y_ref.dtype)
        hn_ref[...] = h_ref[...] + _dot(y_ref[...], wo_ref[...], _NN)
        rider(None, None, i == nt - 1, r_in, r_out, r_scratch)

    vm = lambda rows: pltpu.VMEM((rows, rr), F32)
    return pl.pallas_call(
        body, name=name, grid=(nt,),
        in_specs=[pl.BlockSpec((tm, d), lambda i: (i, 0)), pl.BlockSpec((1, d), lambda i: (0, 0)),
                  _resident((d, r2)), _full(conv_w.shape), _full(conv_b.shape), _full(wg.shape), _full(ba.shape),
                  _full(bx.shape), _full(lam.shape), _resident(w_out.shape)] + list(rider.in_specs),
        out_specs=[pl.BlockSpec((tm, d), lambda i: (i, 0)), pl.BlockSpec((tm, r2), lambda i: (i, 0))]
                  + [pl.BlockSpec((tm, rr), lambda i: (i, 0))] * 7 + [pl.BlockSpec((tm, d), lambda i: (i, 0))]
                  + list(rider.out_specs),
        out_shape=[jax.ShapeDtypeStruct((t, d), MXU_DTYPE), jax.ShapeDtypeStruct((t, r2), F32),
                   jax.ShapeDtypeStruct((t, rr), MXU_DTYPE)] + [jax.ShapeDtypeStruct((t, rr), F32)] * 6
                  + [jax.ShapeDtypeStruct((t, d), F32)] + list(rider.out_shape),
        scratch_shapes=[vm(tm), vm(tm), vm(nh * SUBLANES), vm(SUBLANES)] + list(rider.scratch_shapes),
        compiler_params=pltpu.CompilerParams(dimension_semantics=("arbitrary",), vmem_limit_bytes=VMEM_LIMIT_WIDE_BYTES),
    )(h, g, w, conv_w, conv_b, wg, ba, bx, lam, w_out, *rider.inputs)


def _rg_out_bwd_fused(dh, wot, p, hs, gates, conv_w, wg, wgt, lam, tm, name, rider=_NO_RIDER):
    t, d = dh.shape
    r2 = p.shape[1]
    rr = r2 // 2
    nb, bd = wg.shape[0], wg.shape[1]
    nt = t // tm
    kw = conv_w.shape[0]
    nh = kw - 1
    groups = tm // SUBLANES
    rev = lambda i: nt - 1 - i

    def body(dh_ref, wot_ref, p_ref, hs_ref, hsh_ref, u_ref, r_ref, ig_ref, a_ref, mult_ref, *rest):
        halo_refs = rest[:nh]
        cw_ref, wg_ref, wgt_ref, lam_ref = rest[nh:nh + 4]
        (r_in, (dp_ref, dhb_ref, dcw_ref, dcb_ref, dwg_ref, dba_ref, dbx_ref, dlam_ref), r_out,
         (as_scr, g_scr, carry_g, carry_du), r_scratch) = _split_refs(rider, 8, 4, rest[nh + 4:])
        i = pl.program_id(0)
        newest, oldest = i == 0, i == nt - 1
        rider(newest, i == nt // 2, None, r_in, r_out, r_scratch)

        @pl.when(newest)
        def _():
            carry_g[...] = jnp.zeros_like(carry_g)
            carry_du[...] = jnp.zeros_like(carry_du)
            for ref in (dcw_ref, dcb_ref, dwg_ref, dba_ref, dbx_ref, dlam_ref):
                ref[...] = jnp.zeros_like(ref)

        dhb_ref[...] = dh_ref[...].astype(dhb_ref.dtype)
        ones = jnp.ones((SUBLANES, bd), F32)

        def conv_windows(sl, rsl):
            before = [jnp.where(oldest, 0.0, r[:, rsl]) for r in halo_refs]
            return _windows(p_ref[:, rsl], before, kw)

        for k in range(nb):
            sl = slice(k * bd, (k + 1) * bd)
            rsl = slice(rr + k * bd, rr + (k + 1) * bd)
            dy = _dot(dhb_ref[...], wot_ref[:, sl], _NN)
            gate, dgate = _gelu_and_grad(p_ref[:, sl])
            g_scr[:, sl] = dy * gate
            dp_ref[:, sl] = (dy * hs_ref[:, sl] * dgate).astype(dp_ref.dtype)
            as_scr[:, sl] = _shift_up(a_ref[:, sl], [ones], 1)

        g_first = _scan_tile(as_scr, g_scr, g_scr, carry_g[...], groups, reverse=True)
        carry_g[...] = jnp.broadcast_to(a_ref[0:1, :], (SUBLANES, rr)) * g_first

        for k in range(nb):
            sl = slice(k * bd, (k + 1) * bd)
            rsl = slice(rr + k * bd, rr + (k + 1) * bd)
            cw = _rows(cw_ref, sl)
            lam_k = lam_ref[:, sl]
            sp = _softplus(-lam_k)
            g = g_scr[:, sl]
            a, u, r, ig, mult = a_ref[:, sl], u_ref[:, sl], r_ref[:, sl], ig_ref[:, sl], mult_ref[:, sl]
            da = g * _shift_down(hs_ref[:, sl], [jnp.where(oldest, 0.0, hsh_ref[:, sl])], 1)
            dmult = g * (ig * u)
            d_iu = g * mult
            dla = da * a - dmult * (a * a) / mult
            dr = dla * (-RG_C * sp)
            dsp = jnp.sum(dla * (-RG_C * r), axis=0, keepdims=True)
            dlam_ref[:, sl] += dsp * (-_sigmoid(-lam_k))
            dza = dr * r * (1.0 - r)
            dzx = (d_iu * u) * ig * (1.0 - ig)
            dba_ref[:, sl] += jnp.sum(dza, axis=0, keepdims=True)
            dbx_ref[:, sl] += jnp.sum(dzx, axis=0, keepdims=True)
            ub = u.astype(MXU_DTYPE)
            dz = jnp.concatenate([dza, dzx], axis=1).astype(MXU_DTYPE)
            dwg_ref[k] += _dot(ub, dz, _TN)
            du = d_iu * ig + _dot(dz, wgt_ref[k], _NN)
            dcb_ref[:, sl] += jnp.sum(du, axis=0, keepdims=True)
            win = conv_windows(sl, rsl)
            for kk in range(kw):
                dcw_ref[kk:kk + 1, sl] += jnp.sum(du * win[kk], axis=0, keepdims=True)
            dp_ref[:, rsl] = _conv_t(du, _blocks(carry_du, sl, nh, False), cw).astype(dp_ref.dtype)
            carry_du[:, sl] = du[0:nh * SUBLANES]
        rider(None, None, oldest, r_in, r_out, r_scratch)

    tile = lambda cols: pl.BlockSpec((tm, cols), lambda i: (rev(i), 0))
    vm = lambda rows: pltpu.VMEM((rows, rr), F32)
    grads = [conv_w.shape, lam.shape, wg.shape, lam.shape, lam.shape, lam.shape]
    return pl.pallas_call(
        body, name=name, grid=(nt,),
        in_specs=[tile(d), _resident((d, rr)), tile(r2), tile(rr)] + _halo_specs(tm, rr, 1, rev) + [tile(rr)] * 5
                 + _halo_specs(tm, r2, nh, rev)
                 + [_full(conv_w.shape), _full(wg.shape), _full(wgt.shape), _full(lam.shape)] + list(rider.in_specs),
        out_specs=[tile(r2), tile(d)] + [_full(s) for s in grads] + list(rider.out_specs),
        out_shape=[jax.ShapeDtypeStruct((t, r2), MXU_DTYPE), jax.ShapeDtypeStruct((t, d), MXU_DTYPE)]
                  + [jax.ShapeDtypeStruct(s, F32) for s in grads] + list(rider.out_shape),
        scratch_shapes=[vm(tm)] * 2 + [vm(SUBLANES), vm(nh * SUBLANES)] + list(rider.scratch_shapes),
        compiler_params=pltpu.CompilerParams(dimension_semantics=("arbitrary",), vmem_limit_bytes=VMEM_LIMIT_WIDE_BYTES),
    )(dh, wot, p, hs, hs, *gates, *([p] * nh), conv_w, wg, wgt, lam, *rider.inputs)


def _down_loss_head(z, w, h, g, target, n_meta, tm, name):
    t, d = h.shape
    k = z.shape[1]
    groups = tm // SUBLANES
    nt = t // tm

    def body(z_ref, w_ref, h_ref, g_ref, t_ref, dh_ref, loss_ref, dg_ref, tgt, tbuf, slabs, sems):
        i = pl.program_id(0)
        _fetch_time_tile(i, nt, t_ref, jnp.zeros((n_meta, d), F32), tbuf, sems, tm)
        _time_to_tile_order(tbuf.at[i % 2], slabs, tgt)
        x = h_ref[...] + _dot(z_ref[...], w_ref[...], _NN)
        ms = jnp.mean(x * x, axis=-1, keepdims=True)
        r = lax.rsqrt(ms + RMS_EPS)
        xhat = x * r
        gg = g_ref[...]
        row = lax.broadcasted_iota(jnp.int32, (tm, 1), 0)
        time = i * tm + jnp.right_shift(row, 3) + jnp.bitwise_and(row, SUBLANES - 1) * groups
        err = jnp.where(time >= n_meta, xhat * gg - tgt[...], 0.0)
        dout = err * (1.0 / d)
        dng = dout * gg
        c = jnp.mean(dng * xhat, axis=-1, keepdims=True)
        dh_ref[...] = r * (dng - xhat * c)
        part_dg = jnp.sum(dout * xhat, axis=0, keepdims=True)
        part_loss = jnp.broadcast_to(0.5 * jnp.sum(err * dout, keepdims=True), loss_ref.shape)

        @pl.when(i == 0)
        def _():
            dg_ref[...] = part_dg
            loss_ref[...] = part_loss

        @pl.when(i > 0)
        def _():
            dg_ref[...] += part_dg
            loss_ref[...] += part_loss

    return pl.pallas_call(
        body, name=name, grid=(nt,),
        in_specs=[pl.BlockSpec((tm, k), lambda i: (i, 0)),
                  _resident((k, d)),
                  pl.BlockSpec((tm, d), lambda i: (i, 0)),
                  pl.BlockSpec((1, d), lambda i: (0, 0)),
                  _ANY],
        out_specs=[pl.BlockSpec((tm, d), lambda i: (i, 0)),
                   pl.BlockSpec((SUBLANES, LANES), lambda i: (0, 0)),
                   pl.BlockSpec((1, d), lambda i: (0, 0))],
        out_shape=[jax.ShapeDtypeStruct((t, d), F32), jax.ShapeDtypeStruct((SUBLANES, LANES), F32),
                   jax.ShapeDtypeStruct((1, d), F32)],
        scratch_shapes=[pltpu.VMEM((tm, d), F32)] + _time_scratch(tm, d),
        compiler_params=_params("arbitrary"),
    )(z, w, h, g, target)


def _adamw(w, g, m, v, name):
    rows, cols = w.shape
    tr = rows
    if rows > 512:
        for cand in range(8, 513, 8):
            if rows % cand == 0:
                tr = cand

    def body(w_ref, g_ref, m_ref, v_ref, d_ref, nm_ref, nv_ref):
        g_ = g_ref[...]
        m_ = ADAM_B1 * m_ref[...] + (1.0 - ADAM_B1) * g_
        v_ = ADAM_B2 * v_ref[...] + (1.0 - ADAM_B2) * (g_ * g_)
        m_hat = m_ / (1.0 - ADAM_B1 ** ADAM_STEP)
        v_hat = v_ / (1.0 - ADAM_B2 ** ADAM_STEP)
        d_ref[...] = -ADAM_LR * (m_hat / (jnp.sqrt(v_hat) + ADAM_EPS) + ADAM_WD * w_ref[...])
        nm_ref[...] = m_
        nv_ref[...] = v_

    spec = pl.BlockSpec((tr, cols), lambda i: (i, 0))
    shape = jax.ShapeDtypeStruct((rows, cols), F32)
    return pl.pallas_call(
        body, name=name, grid=(rows // tr,),
        in_specs=[spec] * 4, out_specs=[spec] * 3, out_shape=[shape] * 3,
        compiler_params=_params("parallel"),
    )(w, g, m, v)


def _adamw_nd(w, g, m, v, name):
    shape = w.shape
    two_d = (-1, shape[-1]) if w.ndim > 1 else (1, -1)
    outs = _adamw(w.reshape(two_d), g.reshape(two_d), m.reshape(two_d), v.reshape(two_d), name)
    return tuple(o.reshape(shape) for o in outs)


def _ride_alone(rider, name):
    def body(*refs):
        r_in, _, r_out, _, r_scratch = _split_refs(rider, 0, 0, refs)
        now = pl.program_id(0) == 0
        rider(now, None, None, r_in, r_out, r_scratch)
        rider(None, now, None, r_in, r_out, r_scratch)
        rider(None, None, now, r_in, r_out, r_scratch)

    return pl.pallas_call(
        body, name=name, grid=(1,),
        in_specs=list(rider.in_specs), out_specs=list(rider.out_specs), out_shape=list(rider.out_shape),
        scratch_shapes=list(rider.scratch_shapes),
        compiler_params=_params("arbitrary"),
    )(*rider.inputs)


def _sum_slots(parts, name):
    slots, rows, cols = parts.shape
    tr = _row_tile(rows, 256) if rows % 16 == 0 else rows

    def body(p_ref, o_ref):
        acc = p_ref[0]
        for s in range(1, slots):
            acc = acc + p_ref[s]
        o_ref[...] = acc

    return pl.pallas_call(
        body, name=name, grid=(rows // tr,),
        in_specs=[pl.BlockSpec((slots, tr, cols), lambda i: (0, i, 0))],
        out_specs=pl.BlockSpec((tr, cols), lambda i: (i, 0)),
        out_shape=jax.ShapeDtypeStruct((rows, cols), parts.dtype),
        compiler_params=_params("parallel"),
    )(parts)


def _pad_rows(flat, cols, multiple):
    n = flat.shape[0]
    rows = -(-n // cols)
    rows = -(-rows // multiple) * multiple
    return jnp.pad(flat, (0, rows * cols - n)).reshape(rows, cols)


def _cols_to_chunks(full):
    lead = full.shape[:-1]
    c = full.shape[-1] // N_DEV
    x = full.reshape(-1, N_DEV, c)
    return jnp.transpose(x, (1, 0, 2)).reshape(N_DEV, -1)


def _chunks_to_cols(chunks, lead):
    n = 1
    for s in lead:
        n *= s
    c = chunks.shape[1] // n
    x = chunks.reshape(N_DEV, n, c)
    return jnp.transpose(x, (1, 0, 2)).reshape(tuple(lead) + (N_DEV * c,))


def kernel(x, meta_tokens, norm_mix_g, norm_ffn_g, final_norm_g, sc_w_in, sc_conv_w, sc_w_out, rg_w_in, rg_conv_w, rg_conv_b, rg_w_gate_a, rg_b_gate_a, rg_w_gate_x, rg_b_gate_x, rg_lambda, rg_w_out, ffn_w_up, ffn_conv_w, ffn_w_down, loss_target, m_meta_tokens, m_norm_mix_g, m_norm_ffn_g, m_final_norm_g, m_sc_w_in, m_sc_conv_w, m_sc_w_out, m_rg_w_in, m_rg_conv_w, m_rg_conv_b, m_rg_w_gate_a, m_rg_b_gate_a, m_rg_w_gate_x, m_rg_b_gate_x, m_rg_lambda, m_rg_w_out, m_ffn_w_up, m_ffn_conv_w, m_ffn_w_down, v_meta_tokens, v_norm_mix_g, v_norm_ffn_g, v_final_norm_g, v_sc_w_in, v_sc_conv_w, v_sc_w_out, v_rg_w_in, v_rg_conv_w, v_rg_conv_b, v_rg_w_gate_a, v_rg_b_gate_a, v_rg_w_gate_x, v_rg_b_gate_x, v_rg_lambda, v_rg_w_out, v_ffn_w_up, v_ffn_conv_w, v_ffn_w_down):
    weights = dict(meta_tokens=meta_tokens, norm_mix_g=norm_mix_g, norm_ffn_g=norm_ffn_g, final_norm_g=final_norm_g,
                   sc_w_in=sc_w_in, sc_conv_w=sc_conv_w, sc_w_out=sc_w_out, rg_w_in=rg_w_in, rg_conv_w=rg_conv_w,
                   rg_conv_b=rg_conv_b, rg_w_gate_a=rg_w_gate_a, rg_b_gate_a=rg_b_gate_a, rg_w_gate_x=rg_w_gate_x,
                   rg_b_gate_x=rg_b_gate_x, rg_lambda=rg_lambda, rg_w_out=rg_w_out, ffn_w_up=ffn_w_up,
                   ffn_conv_w=ffn_conv_w, ffn_w_down=ffn_w_down)
    m_in = dict(meta_tokens=m_meta_tokens, norm_mix_g=m_norm_mix_g, norm_ffn_g=m_norm_ffn_g, final_norm_g=m_final_norm_g,
                sc_w_in=m_sc_w_in, sc_conv_w=m_sc_conv_w, sc_w_out=m_sc_w_out, rg_w_in=m_rg_w_in, rg_conv_w=m_rg_conv_w,
                rg_conv_b=m_rg_conv_b, rg_w_gate_a=m_rg_w_gate_a, rg_b_gate_a=m_rg_b_gate_a, rg_w_gate_x=m_rg_w_gate_x,
                rg_b_gate_x=m_rg_b_gate_x, rg_lambda=m_rg_lambda, rg_w_out=m_rg_w_out, ffn_w_up=m_ffn_w_up,
                ffn_conv_w=m_ffn_conv_w, ffn_w_down=m_ffn_w_down)
    v_in = dict(meta_tokens=v_meta_tokens, norm_mix_g=v_norm_mix_g, norm_ffn_g=v_norm_ffn_g, final_norm_g=v_final_norm_g,
                sc_w_in=v_sc_w_in, sc_conv_w=v_sc_conv_w, sc_w_out=v_sc_w_out, rg_w_in=v_rg_w_in, rg_conv_w=v_rg_conv_w,
                rg_conv_b=v_rg_conv_b, rg_w_gate_a=v_rg_w_gate_a, rg_b_gate_a=v_rg_b_gate_a, rg_w_gate_x=v_rg_w_gate_x,
                rg_b_gate_x=v_rg_b_gate_x, rg_lambda=v_rg_lambda, rg_w_out=v_rg_w_out, ffn_w_up=v_ffn_w_up,
                ffn_conv_w=v_ffn_conv_w, ffn_w_down=v_ffn_w_down)
    names = list(weights)

    seq, d = x.shape[1], x.shape[2]
    n_meta = meta_tokens.shape[0]
    n_ffn = ffn_w_up.shape[0]

    assert n_ffn == 2
    def shard_rows(w_in, w_out):
        return [w_in.T.astype(MXU_DTYPE), w_out.astype(MXU_DTYPE)]

    def both_orientations(in_t_gathered, out_gathered):
        w_in_t = in_t_gathered.reshape(-1, d)
        w_out = out_gathered.reshape(-1, d)
        return w_in_t.T, w_in_t, w_out, w_out.T

    shards = {"ffn0": shard_rows(ffn_w_up[0], ffn_w_down[0]), "rg": shard_rows(rg_w_in[0], rg_w_out[0]),
              "ffn1": shard_rows(ffn_w_up[1], ffn_w_down[1])}
    ffn_w_up_f, ffn_w_up_t, ffn_w_down_f, ffn_w_down_t = [None] * 2, [None] * 2, [None] * 2, [None] * 2

    small_names = ["meta_tokens", "sc_conv_w", "rg_conv_w", "rg_conv_b", "rg_b_gate_a", "rg_b_gate_x", "rg_lambda",
                   "ffn_conv_w"]
    small_lead = {n: weights[n].shape[:-1] for n in small_names}
    small_sizes = [weights[n].size for n in small_names]
    small_flat = jnp.concatenate([weights[n].reshape(-1) for n in small_names])
    small_rows = _pad_rows(small_flat, d, SUBLANES)
    sc_in_g, sc_out_g, small_g = _ride_alone(
        _GatherRider(shard_rows(sc_w_in[0], sc_w_out[0]) + [small_rows]), "gather_first")
    sc_w_in_f, sc_w_in_t, sc_w_out_f, sc_w_out_t = both_orientations(sc_in_g, sc_out_g)
    small_g = small_g.reshape(N_DEV, -1)
    small_full = {}
    o = 0
    for n, sz in zip(small_names, small_sizes):
        small_full[n] = _chunks_to_cols(small_g[:, o:o + sz], small_lead[n])
        o += sz

    wg, wgt = _pair_gate_weights(rg_w_gate_a[0].astype(MXU_DTYPE), rg_w_gate_x[0].astype(MXU_DTYPE))
    rg_cw, rg_cb = small_full["rg_conv_w"][0], small_full["rg_conv_b"]
    rg_ba, rg_bx, rg_lam = small_full["rg_b_gate_a"], small_full["rg_b_gate_x"], small_full["rg_lambda"]
    sc_cw = small_full["sc_conv_w"][0]
    ffn_cw = small_full["ffn_conv_w"]

    tp = _row_tile(n_meta + seq, ROW_TILE_PERM)

    def ffn_fwd(h, l, rider):
        n, a, z, *gathered = _ffn_up_fused(h, norm_ffn_g[l:l + 1], ffn_w_up_f[l], ffn_cw[l], tp, f"ffn{l}_up", rider)
        return _matmul_residual(z, ffn_w_down_f[l], h, f"ffn{l}_down"), (n, a, z), gathered

    h0, n0, p0, q0, h1, *ffn0_w = _sc_in_fused(x[0], small_full["meta_tokens"], norm_mix_g[0:1], sc_w_in_f, sc_cw,
                                               sc_w_out_f, tp, "sc_mixer", _GatherRider(shards["ffn0"]))
    ffn_w_up_f[0], ffn_w_up_t[0], ffn_w_down_f[0], ffn_w_down_t[0] = both_orientations(*ffn0_w)
    h2, ffn0_saved, rg_w = ffn_fwd(h1, 0, _GatherRider(shards["rg"]))
    rg_w_in_f, rg_w_in_t, rg_w_out_f, rg_w_out_t = both_orientations(*rg_w)
    n2, p2, y2, hs2, *rest2 = _rg_in_fused(h2, norm_mix_g[1:2], rg_w_in_f, rg_cw, rg_cb, wg, rg_ba, rg_bx, rg_lam,
                                           rg_w_out_f, tp, "rg_mixer", _GatherRider(shards["ffn1"]))
    rg_gates, h3, ffn1_w = rest2[:5], rest2[5], rest2[6:]
    ffn_w_up_f[1], ffn_w_up_t[1], ffn_w_down_f[1], ffn_w_down_t[1] = both_orientations(*ffn1_w)
    ffn1_saved = tuple(_ffn_up_fused(h3, norm_ffn_g[1:2], ffn_w_up_f[1], ffn_cw[1], tp, "ffn1_up"))
    dh4, loss_tile, d_final_g = _down_loss_head(ffn1_saved[2], ffn_w_down_f[1], h3, final_norm_g.reshape(1, d),
                                                loss_target[0], n_meta, tp, "ffn1_down_loss_head")
    loss = lax.psum(loss_tile[0, 0], AXES)

    arrived = {}

    def ffn_bwd(dh_out, h_in, saved, l):
        n, a, z = saved
        da, dhb, d_cw = _ffn_down_bwd_fused(dh_out, ffn_w_down_t[l], a, ffn_cw[l], tp, f"ffn{l}_down_bwd")
        d_w_down = _wgrad(z, dhb, f"ffn{l}_down_wgrad")
        d_w_up_t, arrived[f"ffn_w_down{l}"] = _wgrad(da, n, f"ffn{l}_up_wgrad", _ScatterRider(d_w_down))
        rows = d_w_up_t.shape[0] // N_DEV
        first_rows = (rows // 2) // (2 * SUBLANES) * (2 * SUBLANES)
        dh_in, d_g, arrived[f"ffn_w_up{l}a"] = _dgrad_in_norm(
            da, ffn_w_up_t[l], h_in, norm_ffn_g[l:l + 1], dh_out, f"ffn{l}_up_dgrad",
            _ScatterRider(d_w_up_t, (0, first_rows)))
        return dh_in, d_cw, d_g, _ScatterRider(d_w_up_t, (first_rows, rows - first_rows))

    dh3, d_fcw1, d_fg1, up1_rest = ffn_bwd(dh4, h3, ffn1_saved, 1)

    dp2, dhb3, d_rg_cw, d_rg_cb, d_wg, d_rg_ba, d_rg_bx, d_rg_lam, arrived["ffn_w_up1b"] = _rg_out_bwd_fused(
        dh3, rg_w_out_t, p2, hs2, rg_gates, rg_cw, wg, wgt, rg_lam, tp, "rg_out_bwd", up1_rest)
    d_wa, d_wx = _unpair_gate_grads(d_wg, rg_w_gate_a.shape[2])
    d_rg_w_out = _wgrad(y2, dhb3, "rg_out_wgrad")
    d_rg_w_in_t, arrived["rg_w_out"] = _wgrad(dp2, n2, "rg_in_wgrad", _ScatterRider(d_rg_w_out))
    dh2, d_mg1, arrived["rg_w_in"] = _dgrad_in_norm(
        dp2, rg_w_in_t, h2, norm_mix_g[1:2], dh3, "rg_in_dgrad", _ScatterRider(d_rg_w_in_t))

    dh1, d_fcw0, d_fg0, up0_rest = ffn_bwd(dh2, h1, ffn0_saved, 0)

    dp0, dhb1, d_sc_cw, arrived["ffn_w_up0b"] = _sc_out_bwd_fused(dh1, sc_w_out_t, p0, sc_cw, tp, "sc_out_bwd", up0_rest)
    d_sc_w_out = _wgrad(q0, dhb1, "sc_out_wgrad")
    d_sc_w_in_t, arrived["sc_w_out"] = _wgrad(dp0, n0, "sc_in_wgrad", _ScatterRider(d_sc_w_out))
    d_mg0, d_meta, d_x = _dgrad_in_to_input(dp0, sc_w_in_t, h0, norm_mix_g[0:1], dh1, n_meta, tp, "sc_in_dgrad")
    arrived["sc_w_in"], = _ride_alone(_ScatterRider(d_sc_w_in_t), "scatter_sc_w_in")
    grad_x = d_x[None]

    grads = {}
    for n in ("sc_w_in", "rg_w_in"):
        grads[n] = _sum_slots(arrived[n], f"sum_{n}").T[None]
    for n in ("sc_w_out", "rg_w_out"):
        grads[n] = _sum_slots(arrived[n], f"sum_{n}")[None]
    grads["ffn_w_up"] = jnp.stack([jnp.concatenate(
        [_sum_slots(arrived[f"ffn_w_up{l}{part}"], f"sum_ffn_w_up{l}{part}") for part in "ab"], axis=0).T
        for l in range(n_ffn)])
    grads["ffn_w_down"] = jnp.stack([_sum_slots(arrived[f"ffn_w_down{l}"], f"sum_ffn_w_down{l}") for l in range(n_ffn)])

    small_grads = {"meta_tokens": d_meta,"sc_conv_w": d_sc_cw[None], "rg_conv_w": d_rg_cw[None],
                   "rg_conv_b": d_rg_cb, "rg_b_gate_a": d_rg_ba, "rg_b_gate_x": d_rg_bx, "rg_lambda": d_rg_lam,
                   "ffn_conv_w": jnp.stack([d_fcw0, d_fcw1])}
    small_chunks = jnp.concatenate([_cols_to_chunks(small_grads[n]) for n in small_names], axis=1)
    pad = small_rows.shape[0] * d - small_chunks.shape[1]
    small_chunks = jnp.pad(small_chunks, ((0, 0), (0, pad))).reshape(N_DEV, small_rows.shape[0], d)
    rep_names = ["norm_mix_g", "norm_ffn_g", "final_norm_g", "rg_w_gate_a", "rg_w_gate_x"]
    rep_grads = {"norm_mix_g": jnp.concatenate([d_mg0, d_mg1], axis=0),
                 "norm_ffn_g": jnp.concatenate([d_fg0, d_fg1], axis=0),
                 "final_norm_g": d_final_g.reshape(-1), "rg_w_gate_a": d_wa[None], "rg_w_gate_x": d_wx[None]}
    rep_flat = jnp.concatenate([rep_grads[n].reshape(-1) for n in rep_names])
    rep_chunk_rows = -(-rep_flat.shape[0] // (N_DEV * d))
    rep_chunk_rows += -(small_rows.shape[0] + rep_chunk_rows) % 16
    rep_chunks = jnp.pad(rep_flat, (0, N_DEV * rep_chunk_rows * d - rep_flat.shape[0])).reshape(N_DEV, rep_chunk_rows, d)

    last_chunks = jnp.concatenate([small_chunks, rep_chunks], axis=1)
    reduced = _sum_slots(_ride_alone(_ScatterRider(last_chunks.reshape(-1, d)), "scatter_last")[0], "sum_last")

    small_red = reduced[0:small_rows.shape[0]].reshape(-1)
    o = small_rows.shape[0]
    so = 0
    for n, sz in zip(small_names, small_sizes):
        grads[n] = small_red[so:so + sz].reshape(weights[n].shape)
        so += sz
    rep_red = _ride_alone(_GatherRider([reduced[o:o + rep_chunks.shape[1]]]), "gather_replicated_grads")[0].reshape(-1)
    ro = 0
    for n in rep_names:
        sz = weights[n].size
        grads[n] = rep_red[ro:ro + sz].reshape(weights[n].shape)
        ro += sz

    delta, new_m, new_v = {}, {}, {}
    for n in names:
        delta[n], new_m[n], new_v[n] = _adamw_nd(weights[n], grads[n], m_in[n], v_in[n], f"adamw_{n}")

    return (loss, grad_x, *[grads[n] for n in names], *[delta[n] for n in names],
            *[new_m[n] for n in names], *[new_v[n] for n in names])
```

```python
import jax
import jax.numpy as jnp
from jax import lax
from jax.experimental import pallas as pl
from jax.experimental.pallas import tpu as pltpu

F32 = jnp.float32
MXU_DTYPE = jnp.bfloat16
RMS_EPS = 1e-6
RG_C = 8.0
ADAM_LR = 0.001
ADAM_B1 = 0.9
ADAM_B2 = 0.999
ADAM_EPS = 1e-08
ADAM_WD = 0.01
ADAM_STEP = 10

N_DEV = 8
AXES = ("x", "y", "c")
SUBLANES = 8
LANES = 128
VMEM_LIMIT_BYTES = 48 * 1024 * 1024
ROW_TILE_MATMUL = 700
VMEM_LIMIT_WIDE_BYTES = 58 * 1024 * 1024
ROW_TILE_WGRAD = 3300
COL_TILE_WGRAD = 1408
ROW_TILE_PERM = 400
STRIP = 256
STRIP_FFN = 4096

_TN = (((0,), (0,)), ((), ()))
_NN = (((1,), (0,)), ((), ()))


def _row_tile(t, target):
    best = None
    for tm in range(16, t + 1, 16):
        if t % tm == 0 and tm <= target:
            best = tm
    return best if best is not None else t


def _col_tile(n, target):
    best = None
    for tn in range(LANES, n + 1, LANES):
        if n % tn == 0 and tn <= target:
            best = tn
    return best if best is not None else n


def _params(*sem):
    return pltpu.CompilerParams(dimension_semantics=sem, vmem_limit_bytes=VMEM_LIMIT_BYTES)


def _dot(a, b, dims):
    return lax.dot_general(a, b, dims, preferred_element_type=F32)


def _sigmoid(x):
    return 1.0 / (1.0 + jnp.exp(-x))


def _sigmoid_tanh(x):
    return 0.5 * jnp.tanh(0.5 * x) + 0.5


def _gelu(x):
    c = 0.7978845608028654
    t = jnp.tanh(c * (x + 0.044715 * (x * x * x)))
    return 0.5 * x * (1.0 + t)


def _gelu_and_grad(x):
    c = 0.7978845608028654
    x2 = x * x
    t = jnp.tanh(c * (x + 0.044715 * (x2 * x)))
    half = 0.5 * (1.0 + t)
    return x * half, half + 0.5 * x * (1.0 - t * t) * c * (1.0 + 3.0 * 0.044715 * x2)


def _softplus(x):
    return jnp.maximum(x, 0.0) + jnp.log1p(jnp.exp(-jnp.abs(x)))


def _time_scratch(tm, d):
    return [pltpu.VMEM((2, tm, d), F32), pltpu.VMEM((d // LANES, tm, LANES), F32), pltpu.SemaphoreType.DMA((2,))]


def _fetch_time_tile(i, nt, src_ref, head, tbuf, sems, tm):
    n_head = head.shape[0]

    def tile(j, slot):
        start = pl.multiple_of(j * tm - n_head, SUBLANES)
        return pltpu.make_async_copy(src_ref.at[pl.ds(start, tm)], tbuf.at[slot], sems.at[slot])

    first = pltpu.make_async_copy(src_ref.at[pl.ds(0, tm - n_head)], tbuf.at[0, pl.ds(n_head, tm - n_head)], sems.at[0])

    @pl.when(i == 0)
    def _():
        tbuf[0, 0:n_head, :] = head
        first.start()

    @pl.when(i + 1 < nt)
    def _():
        tile(i + 1, (i + 1) % 2).start()

    @pl.when(i == 0)
    def _():
        first.wait()

    @pl.when(i > 0)
    def _():
        tile(i, i % 2).wait()


def _time_to_tile_order(t_ref, slabs, out_ref):
    tm, d = out_ref.shape
    groups = tm // SUBLANES
    for k in range(d // LANES):
        slabs[k] = t_ref[:, k * LANES:(k + 1) * LANES]
    for k in range(d // LANES):
        for g in range(groups):
            out_ref[g * SUBLANES:(g + 1) * SUBLANES, k * LANES:(k + 1) * LANES] = slabs[k, pl.ds(g, SUBLANES, stride=groups), :]


def _tile_to_time_order(p_ref, slabs, t_ref):
    tm, d = p_ref.shape
    groups = tm // SUBLANES
    for k in range(d // LANES):
        for g in range(groups):
            slabs[k, pl.ds(g, SUBLANES, stride=groups), :] = p_ref[g * SUBLANES:(g + 1) * SUBLANES, k * LANES:(k + 1) * LANES]
    for k in range(d // LANES):
        t_ref[:, k * LANES:(k + 1) * LANES] = slabs[k]


def _rows(ref, sl):
    return [ref[k:k + 1, sl] for k in range(ref.shape[0])]


def _shift_down(x, before, s):
    if s == 0:
        return x
    n = x.shape[0]
    row = lax.broadcasted_iota(jnp.int32, (SUBLANES, x.shape[1]), 0)
    heads = []
    for g in range(s):
        v = x[n - (s - g) * SUBLANES:n - (s - g - 1) * SUBLANES]
        heads.append(pltpu.roll(jnp.where(row == SUBLANES - 1, before[s - g - 1], v), 1, axis=0))
    return jnp.concatenate(heads + [x[0:n - s * SUBLANES]], axis=0)


def _shift_up(x, after, s):
    if s == 0:
        return x
    row = lax.broadcasted_iota(jnp.int32, (SUBLANES, x.shape[1]), 0)
    tails = []
    for m in range(s):
        v = x[m * SUBLANES:(m + 1) * SUBLANES]
        tails.append(pltpu.roll(jnp.where(row == 0, after[m], v), SUBLANES - 1, axis=0))
    return jnp.concatenate([x[s * SUBLANES:]] + tails, axis=0)


def _weighted(w, windows):
    y = w[0] * windows[0]
    for k in range(1, len(w)):
        y = y + w[k] * windows[k]
    return y


def _windows(x, before, k_width):
    return [_shift_down(x, before, k_width - 1 - k) for k in range(k_width)]


def _conv_t(dy, after, w):
    k_width = len(w)
    return _weighted(w, [_shift_up(dy, after, k_width - 1 - k) for k in range(k_width)])


def _blocks(ref, sl, count, newest_first):
    n = ref.shape[0] // SUBLANES
    order = range(n - 1, n - 1 - count, -1) if newest_first else range(count)
    return [ref[b * SUBLANES:(b + 1) * SUBLANES, sl] for b in order]


def _halo_specs(tm, cols, count, tile_of):
    def spec(k):
        return pl.BlockSpec((SUBLANES, cols), lambda i: (jnp.maximum(tile_of(i) * (tm // SUBLANES) - k, 0), 0))
    return [spec(k) for k in range(1, count + 1)]


def _scan_tile(coef, val, out, carry, groups, reverse):
    cols = coef.shape[1]
    row = lax.broadcasted_iota(jnp.int32, (SUBLANES, cols), 0)

    def blk(i):
        g = groups - 1 - i if reverse else i
        return pl.ds(pl.multiple_of(g * SUBLANES, SUBLANES), SUBLANES)

    def local(i, pl_):
        p_prev, l_prev = pl_
        a = coef[blk(i), :]
        p = a * p_prev
        l = a * l_prev + val[blk(i), :]
        coef[blk(i), :] = p
        val[blk(i), :] = l
        return p, l

    pf, lf = lax.fori_loop(0, groups, local, (jnp.ones((SUBLANES, cols), F32), jnp.zeros((SUBLANES, cols), F32)))
    for s in (1, 2, 4):
        keep, sh = (row < SUBLANES - s, SUBLANES - s) if reverse else (row >= s, s)
        p_s = jnp.where(keep, pltpu.roll(pf, sh, axis=0), 1.0)
        l_s = jnp.where(keep, pltpu.roll(lf, sh, axis=0), 0.0)
        lf = pf * l_s + lf
        pf = pf * p_s
    end = lf + pf * carry
    if reverse:
        init = jnp.where(row == SUBLANES - 1, carry, pltpu.roll(end, SUBLANES - 1, axis=0))
        leaving = jnp.broadcast_to(end[0:1, :], (SUBLANES, cols))
    else:
        init = jnp.where(row == 0, carry, pltpu.roll(end, 1, axis=0))
        leaving = jnp.broadcast_to(end[SUBLANES - 1:SUBLANES, :], (SUBLANES, cols))

    def fix(i, _):
        out[blk(i), :] = val[blk(i), :] + coef[blk(i), :] * init
        return 0

    lax.fori_loop(0, groups, fix, 0)
    return leaving


def _resident(shape):
    return pl.BlockSpec(shape, lambda *_: (0,) * len(shape), pipeline_mode=pl.Buffered(1))


def _full(shape):
    return pl.BlockSpec(shape, lambda *_: (0,) * len(shape))


def _rmsnorm_to(h_ref, g_ref, n_ref):
    x = h_ref[...]
    ms = jnp.mean(x * x, axis=-1, keepdims=True)
    n_ref[...] = (x * lax.rsqrt(ms + RMS_EPS) * g_ref[...]).astype(n_ref.dtype)


_ANY = pl.BlockSpec(memory_space=pl.ANY)
_MESH = pl.DeviceIdType.MESH


def _dma_sems(n):
    return pltpu.SemaphoreType.DMA((n,))


def _when_each(*phases):
    for cond, fn in phases:
        if cond is not None:
            pl.when(cond)(fn)


class _NoRider:
    inputs = in_specs = out_shape = out_specs = scratch_shapes = ()

    def __call__(self, first, middle, last, ins, outs, scratch):
        pass


_NO_RIDER = _NoRider()


def _hand_on_step(nt):
    return (3 * nt) // 4


def _split_refs(rider, n_out, n_scratch, rest):
    a = len(rider.inputs)
    b = a + n_out
    c = b + len(rider.out_shape)
    e = c + n_scratch
    return rest[:a], rest[a:b], rest[b:c], rest[c:e], rest[e:]


class _GatherRider:
    def __init__(self, blocks):
        n = len(blocks)
        self.inputs = tuple(blocks)
        self.in_specs = (_ANY,) * n
        self.out_shape = tuple(jax.ShapeDtypeStruct((N_DEV,) + b.shape, b.dtype) for b in blocks)
        self.out_specs = (_ANY,) * n
        self.scratch_shapes = (_dma_sems(7 * n), _dma_sems(7 * n), _dma_sems(n))

    def __call__(self, first, middle, last, ins, outs, scratch):
        n = len(ins)
        send_sems, recv_sems, local_sems = scratch
        x, y, c = lax.axis_index("x"), lax.axis_index("y"), lax.axis_index("c")
        me, sibling = (x, y, c), (x, y, 1 - c)
        chips = [(1 - x, y), (x, 1 - y), (1 - x, 1 - y)]

        def slot(b, px, py, pc):
            return outs[b].at[4 * px + 2 * py + pc]

        def copy(k, b, block, to, src=None):
            return pltpu.make_async_remote_copy(
                src_ref=slot(b, *block) if src is None else src, dst_ref=slot(b, *block),
                send_sem=send_sems.at[k * n + b], recv_sem=recv_sems.at[k * n + b], device_id=to, device_id_type=_MESH)

        mine = [pltpu.make_async_copy(ins[b], slot(b, *me), local_sems.at[b]) for b in range(n)]
        own = [copy(0, b, me, sibling, src=ins[b]) for b in range(n)]
        own += [copy(1 + j, b, me, (*chip, c), src=ins[b]) for j, chip in enumerate(chips) for b in range(n)]
        passed = [[copy(4 + j, b, (*chip, c), sibling) for b in range(n)] for j, chip in enumerate(chips)]

        def at_first():
            for cp in mine + own:
                cp.start()

        def at_middle():
            for j, chip in enumerate(chips):
                for b in range(n):
                    copy(1 + j, b, (*chip, c), me).wait_recv()
                    passed[j][b].start()

        def at_last():
            for b in range(n):
                copy(0, b, sibling, me).wait_recv()
                for j, chip in enumerate(chips):
                    copy(4 + j, b, (*chip, 1 - c), me).wait_recv()
            for cp in own + [cp for group in passed for cp in group]:
                cp.wait_send()
            for cp in mine:
                cp.wait()

        _when_each((first, at_first), (middle, at_middle), (last, at_last))


class _ScatterRider:
    def __init__(self, grad, rows=None):
        r = grad.shape[0] // N_DEV
        chunks = grad.reshape(N_DEV, r, grad.shape[1])
        self.rows = rows if rows is not None else (0, r)
        self.inputs = (chunks,)
        self.in_specs = (_ANY,)
        self.out_shape = (jax.ShapeDtypeStruct((N_DEV, self.rows[1], grad.shape[1]), chunks.dtype),)
        self.out_specs = (_ANY,)
        self.scratch_shapes = (_dma_sems(N_DEV - 1), _dma_sems(N_DEV - 1), pltpu.SemaphoreType.DMA(()))

    def __call__(self, first, middle, last, ins, outs, scratch):
        (g_ref,), (r_ref,) = ins, outs
        send_sems, recv_sems, local_sem = scratch
        x, y, c = lax.axis_index("x"), lax.axis_index("y"), lax.axis_index("c")
        me = 4 * x + 2 * y + c
        part = pl.ds(*self.rows)
        mine = pltpu.make_async_copy(g_ref.at[me, part], r_ref.at[me], local_sem)
        copies = []
        for k in range(1, N_DEV):
            px, py, pc = (1 - x if k & 4 else x), (1 - y if k & 2 else y), (1 - c if k & 1 else c)
            copies.append(pltpu.make_async_remote_copy(
                src_ref=g_ref.at[4 * px + 2 * py + pc, part], dst_ref=r_ref.at[me],
                send_sem=send_sems.at[k - 1], recv_sem=recv_sems.at[k - 1],
                device_id=(px, py, pc), device_id_type=_MESH))

        def at_first():
            mine.start()
            for cp in copies:
                cp.start()

        def at_last():
            for cp in copies:
                cp.wait()
            mine.wait()

        _when_each((first, at_first), (last, at_last))


def _matmul_residual(q, w, h, name):
    t, k = q.shape
    d = w.shape[1]
    tm = _row_tile(t, ROW_TILE_MATMUL)

    def body(q_ref, w_ref, h_ref, o_ref):
        o_ref[...] = h_ref[...] + _dot(q_ref[...], w_ref[...], _NN)

    return pl.pallas_call(
        body, name=name, grid=(t // tm,),
        in_specs=[pl.BlockSpec((tm, k), lambda i: (i, 0)),
                  _resident((k, d)),
                  pl.BlockSpec((tm, d), lambda i: (i, 0))],
        out_specs=pl.BlockSpec((tm, d), lambda i: (i, 0)),
        out_shape=jax.ShapeDtypeStruct((t, d), F32),
        compiler_params=_params("parallel"),
    )(q, w, h)


def _dgrad_in_norm(dp, wt, h, g, dh_next, name, rider=_NO_RIDER):
    t, n = dp.shape
    d = wt.shape[1]
    tm = _row_tile(t, ROW_TILE_MATMUL)
    nt = t // tm

    def body(dp_ref, wt_ref, h_ref, g_ref, dhn_ref, *rest):
        r_in, (dh_ref, dg_ref), r_out, _, r_scratch = _split_refs(rider, 2, 0, rest)
        i = pl.program_id(0)
        rider(i == 0, i == nt // 2, None, r_in, r_out, r_scratch)
        dn = _dot(dp_ref[...], wt_ref[...], _NN)
        x = h_ref[...]
        ms = jnp.mean(x * x, axis=-1, keepdims=True)
        r = lax.rsqrt(ms + RMS_EPS)
        xhat = x * r
        dng = dn * g_ref[...]
        c = jnp.mean(dng * xhat, axis=-1, keepdims=True)
        dh_ref[...] = dhn_ref[...] + r * (dng - xhat * c)
        part = jnp.sum(dn * xhat, axis=0, keepdims=True)

        @pl.when(i == 0)
        def _():
            dg_ref[...] = part

        @pl.when(i > 0)
        def _():
            dg_ref[...] += part

        rider(None, None, i == nt - 1, r_in, r_out, r_scratch)

    return pl.pallas_call(
        body, name=name, grid=(nt,),
        in_specs=[pl.BlockSpec((tm, n), lambda i: (i, 0)),
                  _resident((n, d)),
                  pl.BlockSpec((tm, d), lambda i: (i, 0)),
                  pl.BlockSpec((1, d), lambda i: (0, 0)),
                  pl.BlockSpec((tm, d), lambda i: (i, 0))] + list(rider.in_specs),
        out_specs=[pl.BlockSpec((tm, d), lambda i: (i, 0)),
                   pl.BlockSpec((1, d), lambda i: (0, 0))] + list(rider.out_specs),
        out_shape=[jax.ShapeDtypeStruct((t, d), F32), jax.ShapeDtypeStruct((1, d), F32)] + list(rider.out_shape),
        scratch_shapes=list(rider.scratch_shapes),
        compiler_params=_params("arbitrary"),
    )(dp, wt, h, g, dh_next, *rider.inputs)


def _dgrad_in_to_input(dp, wt, h, g, dh_next, n_head, tm, name, rider=_NO_RIDER):
    t, n = dp.shape
    d = wt.shape[1]
    nt = t // tm

    def body(dp_ref, wt_ref, h_ref, g_ref, dhn_ref, *rest):
        r_in, (dg_ref, head_ref, rest_ref), r_out, (pbuf, tout, slabs, sems), r_scratch = _split_refs(rider, 3, 4, rest)
        i = pl.program_id(0)
        rider(i == 0, i == nt // 2, None, r_in, r_out, r_scratch)
        dn = _dot(dp_ref[...], wt_ref[...], _NN)
        x = h_ref[...]
        ms = jnp.mean(x * x, axis=-1, keepdims=True)
        r = lax.rsqrt(ms + RMS_EPS)
        xhat = x * r
        dng = dn * g_ref[...]
        c = jnp.mean(dng * xhat, axis=-1, keepdims=True)
        pbuf[...] = dhn_ref[...] + r * (dng - xhat * c)
        part = jnp.sum(dn * xhat, axis=0, keepdims=True)

        @pl.when(i == 0)
        def _():
            dg_ref[...] = part

        @pl.when(i > 0)
        def _():
            dg_ref[...] += part

        def store(j):
            if isinstance(j, int) and j == 0:
                return pltpu.make_async_copy(tout.at[0, pl.ds(n_head, tm - n_head)],
                                             rest_ref.at[pl.ds(0, tm - n_head)], sems.at[0])
            start = j * tm - n_head
            start = start if isinstance(start, int) else pl.multiple_of(start, SUBLANES)
            return pltpu.make_async_copy(tout.at[j % 2], rest_ref.at[pl.ds(start, tm)], sems.at[j % 2])

        @pl.when(i == 2)
        def _():
            store(0).wait()

        @pl.when(i > 2)
        def _():
            store(i - 2).wait()

        _tile_to_time_order(pbuf, slabs, tout.at[i % 2])

        @pl.when(i == 0)
        def _():
            head_ref[...] = tout[0, 0:n_head, :]
            store(0).start()

        @pl.when(i > 0)
        def _():
            store(i).start()

        @pl.when(i == nt - 1)
        def _():
            for j in (nt - 2, nt - 1):
                if j >= 0:
                    store(j).wait()

        rider(None, None, i == nt - 1, r_in, r_out, r_scratch)

    return pl.pallas_call(
        body, name=name, grid=(nt,),
        in_specs=[pl.BlockSpec((tm, n), lambda i: (i, 0)),
                  _resident((n, d)),
                  pl.BlockSpec((tm, d), lambda i: (i, 0)),
                  pl.BlockSpec((1, d), lambda i: (0, 0)),
                  pl.BlockSpec((tm, d), lambda i: (i, 0))] + list(rider.in_specs),
        out_specs=[pl.BlockSpec((1, d), lambda i: (0, 0)), _full((n_head, d)), _ANY] + list(rider.out_specs),
        out_shape=[jax.ShapeDtypeStruct((1, d), F32), jax.ShapeDtypeStruct((n_head, d), F32),
                   jax.ShapeDtypeStruct((t - n_head, d), F32)] + list(rider.out_shape),
        scratch_shapes=[pltpu.VMEM((tm, d), F32)] + _time_scratch(tm, d) + list(rider.scratch_shapes),
        compiler_params=_params("arbitrary"),
    )(dp, wt, h, g, dh_next, *rider.inputs)


def _wgrad(a, b, name, rider=_NO_RIDER):
    t, m = a.shape
    d = b.shape[1]
    tmm, tk = _col_tile(m, COL_TILE_WGRAD), _row_tile(t, ROW_TILE_WGRAD)
    nm, nk = m // tmm, t // tk

    def body(a_ref, b_ref, *rest):
        r_in, (o_ref,), r_out, _, r_scratch = _split_refs(rider, 1, 0, rest)
        i, k = pl.program_id(0), pl.program_id(1)
        rider(jnp.logical_and(i == 0, k == 0), None, None, r_in, r_out, r_scratch)

        @pl.when(k == 0)
        def _():
            o_ref[...] = jnp.zeros_like(o_ref)

        o_ref[...] += _dot(a_ref[...], b_ref[...], _TN)
        rider(None, None, jnp.logical_and(i == nm - 1, k == nk - 1), r_in, r_out, r_scratch)

    outs = pl.pallas_call(
        body, name=name, grid=(nm, nk),
        in_specs=[pl.BlockSpec((tk, tmm), lambda i, k: (k, i)),
                  pl.BlockSpec((tk, d), lambda i, k: (k, 0))] + list(rider.in_specs),
        out_specs=[pl.BlockSpec((tmm, d), lambda i, k: (i, 0))] + list(rider.out_specs),
        out_shape=[jax.ShapeDtypeStruct((m, d), F32)] + list(rider.out_shape),
        scratch_shapes=list(rider.scratch_shapes),
        compiler_params=pltpu.CompilerParams(dimension_semantics=("arbitrary", "arbitrary"),
                                             vmem_limit_bytes=VMEM_LIMIT_WIDE_BYTES),
    )(a, b, *rider.inputs)
    return outs if rider.out_shape else outs[0]


def _ffn_up_fused(h, g, w, conv_w, tm, name, rider=_NO_RIDER):
    t, d = h.shape
    f2 = w.shape[1]
    f = f2 // 2
    cw = min(STRIP_FFN, f)
    kw = conv_w.shape[0]
    nh = kw - 1
    nt = t // tm

    def body(h_ref, g_ref, w_ref, cw_ref, *rest):
        r_in, (n_ref, a_ref, z_ref), r_out, (carry,), r_scratch = _split_refs(rider, 3, 1, rest)
        i = pl.program_id(0)
        rider(i == 0, i == _hand_on_step(nt), None, r_in, r_out, r_scratch)

        @pl.when(i == 0)
        def _():
            carry[...] = jnp.zeros_like(carry)

        _rmsnorm_to(h_ref, g_ref, n_ref)
        for c in range(0, f, cw):
            conv = []
            for sl in (slice(c, c + cw), slice(f + c, f + c + cw)):
                a = _dot(n_ref[...], w_ref[:, sl], _NN)
                a_ref[:, sl] = a
                conv.append(_weighted(_rows(cw_ref, sl), _windows(a, _blocks(carry, sl, nh, True), kw)))
                carry[:, sl] = a[tm - nh * SUBLANES:tm]
            gg, vv = conv
            z_ref[:, c:c + cw] = (gg * _sigmoid(gg) * vv).astype(z_ref.dtype)
        rider(None, None, i == nt - 1, r_in, r_out, r_scratch)

    return pl.pallas_call(
        body, name=name, grid=(nt,),
        in_specs=[pl.BlockSpec((tm, d), lambda i: (i, 0)), pl.BlockSpec((1, d), lambda i: (0, 0)),
                  _resident((d, f2)), _full(conv_w.shape)] + list(rider.in_specs),
        out_specs=[pl.BlockSpec((tm, d), lambda i: (i, 0)), pl.BlockSpec((tm, f2), lambda i: (i, 0)),
                   pl.BlockSpec((tm, f), lambda i: (i, 0))] + list(rider.out_specs),
        out_shape=[jax.ShapeDtypeStruct((t, d), MXU_DTYPE), jax.ShapeDtypeStruct((t, f2), F32),
                   jax.ShapeDtypeStruct((t, f), MXU_DTYPE)] + list(rider.out_shape),
        scratch_shapes=[pltpu.VMEM((nh * SUBLANES, f2), F32)] + list(rider.scratch_shapes),
        compiler_params=_params("arbitrary"),
    )(h, g, w, conv_w, *rider.inputs)


def _ffn_down_bwd_fused(dh, wdt, a, conv_w, tm, name):
    t, d = dh.shape
    f = wdt.shape[1]
    f2 = 2 * f
    nt = t // tm
    cw = min(STRIP_FFN, f)
    kw = conv_w.shape[0]
    nh = kw - 1
    rev = lambda i: nt - 1 - i

    def body(dh_ref, wdt_ref, a_ref, *rest):
        halo_refs = rest[:nh]
        cw_ref, da_ref, dhb_ref, dw_ref, carry = rest[nh:]
        i = pl.program_id(0)
        newest, oldest = i == 0, i == nt - 1

        @pl.when(newest)
        def _():
            carry[...] = jnp.zeros_like(carry)
            dw_ref[...] = jnp.zeros_like(dw_ref)

        dhb_ref[...] = dh_ref[...].astype(dhb_ref.dtype)
        for c in range(0, f, cw):
            gsl, vsl = slice(c, c + cw), slice(f + c, f + c + cw)
            dz = _dot(dhb_ref[...], wdt_ref[:, gsl], _NN)
            win, conv = {}, {}
            for sl in (gsl, vsl):
                before = [jnp.where(oldest, 0.0, r[:, sl]) for r in halo_refs]
                win[sl.start] = _windows(a_ref[:, sl], before, kw)
                conv[sl.start] = _weighted(_rows(cw_ref, sl), win[sl.start])
            gg, vv = conv[gsl.start], conv[vsl.start]
            s = _sigmoid(gg)
            grads = {gsl.start: dz * vv * s * (1.0 + gg * (1.0 - s)), vsl.start: dz * gg * s}
            for sl in (gsl, vsl):
                dconv = grads[sl.start]
                da_ref[:, sl] = _conv_t(dconv, _blocks(carry, sl, nh, False), _rows(cw_ref, sl)).astype(da_ref.dtype)
                for k in range(kw):
                    dw_ref[k:k + 1, sl] += jnp.sum(dconv * win[sl.start][k], axis=0, keepdims=True)
                carry[:, sl] = dconv[0:nh * SUBLANES]

    return pl.pallas_call(
        body, name=name, grid=(nt,),
        in_specs=[pl.BlockSpec((tm, d), lambda i: (rev(i), 0)), _resident((d, f)),
                  pl.BlockSpec((tm, f2), lambda i: (rev(i), 0))] + _halo_specs(tm, f2, nh, rev) + [_full(conv_w.shape)],
        out_specs=[pl.BlockSpec((tm, f2), lambda i: (rev(i), 0)), pl.BlockSpec((tm, d), lambda i: (rev(i), 0)),
                   _full(conv_w.shape)],
        out_shape=[jax.ShapeDtypeStruct((t, f2), MXU_DTYPE), jax.ShapeDtypeStruct((t, d), MXU_DTYPE),
                   jax.ShapeDtypeStruct(conv_w.shape, F32)],
        scratch_shapes=[pltpu.VMEM((nh * SUBLANES, f2), F32)],
        compiler_params=_params("arbitrary"),
    )(dh, wdt, a, *([a] * nh), conv_w)


def _sc_in_fused(x, meta, g, w, conv_w, w_out, tm, name, rider=_NO_RIDER):
    d = x.shape[1]
    t = x.shape[0] + meta.shape[0]
    cw = min(STRIP, d)
    kw = conv_w.shape[0]
    nh = kw - 1
    nt = t // tm

    def body(x_ref, meta_ref, g_ref, w_ref, cw_ref, wo_ref, *rest):
        r_in, (h_ref, n_ref, p_ref, q_ref, h1_ref), r_out, (carry, tbuf, slabs, sems), r_scratch = _split_refs(
            rider, 5, 4, rest)
        i = pl.program_id(0)
        rider(i == 0, i == _hand_on_step(nt), None, r_in, r_out, r_scratch)

        @pl.when(i == 0)
        def _():
            carry[...] = jnp.zeros_like(carry)

        _fetch_time_tile(i, nt, x_ref, meta_ref[...], tbuf, sems, tm)
        _time_to_tile_order(tbuf.at[i % 2], slabs, h_ref)
        _rmsnorm_to(h_ref, g_ref, n_ref)
        for c in range(0, d, cw):
            sl = slice(c, c + cw)
            parts = []
            for base in (0, d, 2 * d):
                psl = slice(base + c, base + c + cw)
                parts.append(_dot(n_ref[...], w_ref[:, psl], _NN))
                p_ref[:, psl] = parts[-1]
            bg, cg, v = parts
            cv = cg * v
            u = _weighted(_rows(cw_ref, sl), _windows(cv, _blocks(carry, sl, nh, True), kw))
            carry[:, sl] = cv[tm - nh * SUBLANES:tm]
            q_ref[:, sl] = (bg * u).astype(q_ref.dtype)
        h1_ref[...] = h_ref[...] + _dot(q_ref[...], wo_ref[...], _NN)
        rider(None, None, i == nt - 1, r_in, r_out, r_scratch)

    row_tile = lambda cols: pl.BlockSpec((tm, cols), lambda i: (i, 0))
    return pl.pallas_call(
        body, name=name, grid=(nt,),
        in_specs=[_ANY, _full(meta.shape), pl.BlockSpec((1, d), lambda i: (0, 0)),
                  _resident((d, 3 * d)), _full(conv_w.shape), _resident(w_out.shape)] + list(rider.in_specs),
        out_specs=[row_tile(d), row_tile(d), row_tile(3 * d), row_tile(d), row_tile(d)] + list(rider.out_specs),
        out_shape=[jax.ShapeDtypeStruct((t, d), F32), jax.ShapeDtypeStruct((t, d), MXU_DTYPE),
                   jax.ShapeDtypeStruct((t, 3 * d), F32), jax.ShapeDtypeStruct((t, d), MXU_DTYPE),
                   jax.ShapeDtypeStruct((t, d), F32)] + list(rider.out_shape),
        scratch_shapes=[pltpu.VMEM((nh * SUBLANES, d), F32)] + _time_scratch(tm, d) + list(rider.scratch_shapes),
        compiler_params=_params("arbitrary"),
    )(x, meta, g, w, conv_w, w_out, *rider.inputs)


def _sc_out_bwd_fused(dh, wot, p, conv_w, tm, name, rider=_NO_RIDER):
    t, d = dh.shape
    nt = t // tm
    cw = min(STRIP, d)
    kw = conv_w.shape[0]
    nh = kw - 1
    rev = lambda i: nt - 1 - i

    def body(dh_ref, wot_ref, p_ref, *rest):
        halo_refs, cw_ref = rest[:nh], rest[nh]
        r_in, (dp_ref, dhb_ref, dw_ref), r_out, (carry,), r_scratch = _split_refs(rider, 3, 1, rest[nh + 1:])
        i = pl.program_id(0)
        newest, oldest = i == 0, i == nt - 1
        rider(newest, i == nt // 2, None, r_in, r_out, r_scratch)

        @pl.when(newest)
        def _():
            carry[...] = jnp.zeros_like(carry)
            dw_ref[...] = jnp.zeros_like(dw_ref)

        dhb_ref[...] = dh_ref[...].astype(dhb_ref.dtype)
        for c in range(0, d, cw):
            sl, csl, vsl = slice(c, c + cw), slice(d + c, d + c + cw), slice(2 * d + c, 2 * d + c + cw)
            w = _rows(cw_ref, sl)
            dq = _dot(dhb_ref[...], wot_ref[:, sl], _NN)
            bg, cg, v = p_ref[:, sl], p_ref[:, csl], p_ref[:, vsl]
            before = [jnp.where(oldest, 0.0, r[:, csl] * r[:, vsl]) for r in halo_refs]
            win = _windows(cg * v, before, kw)
            du = dq * bg
            dcv = _conv_t(du, _blocks(carry, sl, nh, False), w)
            dp_ref[:, sl] = (dq * _weighted(w, win)).astype(dp_ref.dtype)
            dp_ref[:, csl] = (dcv * v).astype(dp_ref.dtype)
            dp_ref[:, vsl] = (dcv * cg).astype(dp_ref.dtype)
            for k in range(kw):
                dw_ref[k:k + 1, sl] += jnp.sum(du * win[k], axis=0, keepdims=True)
            carry[:, sl] = du[0:nh * SUBLANES]
        rider(None, None, oldest, r_in, r_out, r_scratch)

    tile = lambda cols: pl.BlockSpec((tm, cols), lambda i: (rev(i), 0))
    return pl.pallas_call(
        body, name=name, grid=(nt,),
        in_specs=[tile(d), _resident((d, d)), tile(3 * d)] + _halo_specs(tm, 3 * d, nh, rev) + [_full(conv_w.shape)]
                 + list(rider.in_specs),
        out_specs=[tile(3 * d), tile(d), _full(conv_w.shape)] + list(rider.out_specs),
        out_shape=[jax.ShapeDtypeStruct((t, 3 * d), MXU_DTYPE), jax.ShapeDtypeStruct((t, d), MXU_DTYPE),
                   jax.ShapeDtypeStruct(conv_w.shape, F32)] + list(rider.out_shape),
        scratch_shapes=[pltpu.VMEM((nh * SUBLANES, d), F32)] + list(rider.scratch_shapes),
        compiler_params=_params("arbitrary"),
    )(dh, wot, p, *([p] * nh), conv_w, *rider.inputs)


def _pair_gate_weights(wa, wx):
    nb, bd, _ = wa.shape
    zero = jnp.zeros((nb // 2, bd, bd), wa.dtype)

    def pair(w):
        w = w.reshape(nb // 2, 2, bd, bd)
        top = jnp.concatenate([w[:, 0], zero], axis=2)
        bottom = jnp.concatenate([zero, w[:, 1]], axis=2)
        return jnp.concatenate([top, bottom], axis=1)

    both = jnp.concatenate([pair(wa), pair(wx)], axis=2)
    return both, jnp.swapaxes(both, 1, 2)


def _unpair_gate_grads(dw, bd):
    def blocks(cols):
        d0 = dw[:, :bd, cols:cols + bd]
        d1 = dw[:, bd:, cols + bd:cols + 2 * bd]
        return jnp.stack([d0, d1], axis=1).reshape(-1, bd, bd)
    return blocks(0), blocks(2 * bd)


def _rg_gates(u, k, wg_ref, ba, bx, lam):
    ub = u.astype(MXU_DTYPE)
    z = _dot(ub, wg_ref[k], _NN)
    half = z.shape[1] // 2
    r = _sigmoid_tanh(z[:, :half] + ba)
    ig = _sigmoid_tanh(z[:, half:] + bx)
    sp = _softplus(-lam)
    la = -RG_C * r * sp
    a = jnp.exp(la)
    th = jnp.tanh(la)
    mult = jnp.sqrt(-2.0 * th / (1.0 - th))
    return r, ig, a, mult


def _rg_in_fused(h, g, w, conv_w, conv_b, wg, ba, bx, lam, w_out, tm, name, rider=_NO_RIDER):
    t, d = h.shape
    r2 = w.shape[1]
    rr = r2 // 2
    nb, bd = wg.shape[0], wg.shape[1]
    kw = conv_w.shape[0]
    nh = kw - 1
    groups = tm // SUBLANES
    nt = t // tm

    def body(h_ref, g_ref, w_ref, cw_ref, cb_ref, wg_ref, ba_ref, bx_ref, lam_ref, wo_ref, *rest):
        (r_in, (n_ref, p_ref, y_ref, hs_ref, u_ref, r_ref, ig_ref, a_ref, mult_ref, hn_ref), r_out,
         (a_scr, b_scr, carry_rb, carry_h), r_scratch) = _split_refs(rider, 10, 4, rest)
        i = pl.program_id(0)
        rider(i == 0, i == _hand_on_step(nt), None, r_in, r_out, r_scratch)

        @pl.when(i == 0)
        def _():
            carry_rb[...] = jnp.zeros_like(carry_rb)
            carry_h[...] = jnp.zeros_like(carry_h)

        _rmsnorm_to(h_ref, g_ref, n_ref)
        for k in range(nb):
            sl = slice(k * bd, (k + 1) * bd)
            rsl = slice(rr + k * bd, rr + (k + 1) * bd)
            p_ref[:, sl] = _dot(n_ref[...], w_ref[:, sl], _NN)
            rb = _dot(n_ref[...], w_ref[:, rsl], _NN)
            p_ref[:, rsl] = rb
            u = _weighted(_rows(cw_ref, sl), _windows(rb, _blocks(carry_rb, sl, nh, True), kw)) + cb_ref[:, sl]
            carry_rb[:, sl] = rb[tm - nh * SUBLANES:tm]
            r, ig, a, mult = _rg_gates(u, k, wg_ref, ba_ref[:, sl], bx_ref[:, sl], lam_ref[:, sl])
            for ref, val in ((u_ref, u), (r_ref, r), (ig_ref, ig), (a_ref, a), (mult_ref, mult)):
                ref[:, sl] = val
            a_scr[:, sl] = a
            b_scr[:, sl] = mult * (ig * u)

        carry_h[...] = _scan_tile(a_scr, b_scr, hs_ref, carry_h[...], groups, reverse=False)

        for k in range(nb):
            sl = slice(k * bd, (k + 1) * bd)
            y_ref[:, sl] = (hs_ref[:, sl] * _gelu(p_ref[:, sl])).astype(y_ref.dtype)
        hn_ref[...] = h_ref[...] + _dot(y_ref[...], wo_ref[...], _NN)
        rider(None, None, i == nt - 1, r_in, r_out, r_scratch)

    vm = lambda rows: pltpu.VMEM((rows, rr), F32)
    return pl.pallas_call(
        body, name=name, grid=(nt,),
        in_specs=[pl.BlockSpec((tm, d), lambda i: (i, 0)), pl.BlockSpec((1, d), lambda i: (0, 0)),
                  _resident((d, r2)), _full(conv_w.shape), _full(conv_b.shape), _full(wg.shape), _full(ba.shape),
                  _full(bx.shape), _full(lam.shape), _resident(w_out.shape)] + list(rider.in_specs),
        out_specs=[pl.BlockSpec((tm, d), lambda i: (i, 0)), pl.BlockSpec((tm, r2), lambda i: (i, 0))]
                  + [pl.BlockSpec((tm, rr), lambda i: (i, 0))] * 7 + [pl.BlockSpec((tm, d), lambda i: (i, 0))]
                  + list(rider.out_specs),
        out_shape=[jax.ShapeDtypeStruct((t, d), MXU_DTYPE), jax.ShapeDtypeStruct((t, r2), F32),
                   jax.ShapeDtypeStruct((t, rr), MXU_DTYPE)] + [jax.ShapeDtypeStruct((t, rr), F32)] * 6
                  + [jax.ShapeDtypeStruct((t, d), F32)] + list(rider.out_shape),
        scratch_shapes=[vm(tm), vm(tm), vm(nh * SUBLANES), vm(SUBLANES)] + list(rider.scratch_shapes),
        compiler_params=pltpu.CompilerParams(dimension_semantics=("arbitrary",), vmem_limit_bytes=VMEM_LIMIT_WIDE_BYTES),
    )(h, g, w, conv_w, conv_b, wg, ba, bx, lam, w_out, *rider.inputs)


def _rg_out_bwd_fused(dh, wot, p, hs, gates, conv_w, wg, wgt, lam, tm, name, rider=_NO_RIDER):
    t, d = dh.shape
    r2 = p.shape[1]
    rr = r2 // 2
    nb, bd = wg.shape[0], wg.shape[1]
    nt = t // tm
    kw = conv_w.shape[0]
    nh = kw - 1
    groups = tm // SUBLANES
    rev = lambda i: nt - 1 - i

    def body(dh_ref, wot_ref, p_ref, hs_ref, hsh_ref, u_ref, r_ref, ig_ref, a_ref, mult_ref, *rest):
        halo_refs = rest[:nh]
        cw_ref, wg_ref, wgt_ref, lam_ref = rest[nh:nh + 4]
        (r_in, (dp_ref, dhb_ref, dcw_ref, dcb_ref, dwg_ref, dba_ref, dbx_ref, dlam_ref), r_out,
         (as_scr, g_scr, carry_g, carry_du), r_scratch) = _split_refs(rider, 8, 4, rest[nh + 4:])
        i = pl.program_id(0)
        newest, oldest = i == 0, i == nt - 1
        rider(newest, i == nt // 2, None, r_in, r_out, r_scratch)

        @pl.when(newest)
        def _():
            carry_g[...] = jnp.zeros_like(carry_g)
            carry_du[...] = jnp.zeros_like(carry_du)
            for ref in (dcw_ref, dcb_ref, dwg_ref, dba_ref, dbx_ref, dlam_ref):
                ref[...] = jnp.zeros_like(ref)

        dhb_ref[...] = dh_ref[...].astype(dhb_ref.dtype)
        ones = jnp.ones((SUBLANES, bd), F32)

        def conv_windows(sl, rsl):
            before = [jnp.where(oldest, 0.0, r[:, rsl]) for r in halo_refs]
            return _windows(p_ref[:, rsl], before, kw)

        for k in range(nb):
            sl = slice(k * bd, (k + 1) * bd)
            rsl = slice(rr + k * bd, rr + (k + 1) * bd)
            dy = _dot(dhb_ref[...], wot_ref[:, sl], _NN)
            gate, dgate = _gelu_and_grad(p_ref[:, sl])
            g_scr[:, sl] = dy * gate
            dp_ref[:, sl] = (dy * hs_ref[:, sl] * dgate).astype(dp_ref.dtype)
            as_scr[:, sl] = _shift_up(a_ref[:, sl], [ones], 1)

        g_first = _scan_tile(as_scr, g_scr, g_scr, carry_g[...], groups, reverse=True)
        carry_g[...] = jnp.broadcast_to(a_ref[0:1, :], (SUBLANES, rr)) * g_first

        for k in range(nb):
            sl = slice(k * bd, (k + 1) * bd)
            rsl = slice(rr + k * bd, rr + (k + 1) * bd)
            cw = _rows(cw_ref, sl)
            lam_k = lam_ref[:, sl]
            sp = _softplus(-lam_k)
            g = g_scr[:, sl]
            a, u, r, ig, mult = a_ref[:, sl], u_ref[:, sl], r_ref[:, sl], ig_ref[:, sl], mult_ref[:, sl]
            da = g * _shift_down(hs_ref[:, sl], [jnp.where(oldest, 0.0, hsh_ref[:, sl])], 1)
            dmult = g * (ig * u)
            d_iu = g * mult
            dla = da * a - dmult * (a * a) / mult
            dr = dla * (-RG_C * sp)
            dsp = jnp.sum(dla * (-RG_C * r), axis=0, keepdims=True)
            dlam_ref[:, sl] += dsp * (-_sigmoid(-lam_k))
            dza = dr * r * (1.0 - r)
            dzx = (d_iu * u) * ig * (1.0 - ig)
            dba_ref[:, sl] += jnp.sum(dza, axis=0, keepdims=True)
            dbx_ref[:, sl] += jnp.sum(dzx, axis=0, keepdims=True)
            ub = u.astype(MXU_DTYPE)
            dz = jnp.concatenate([dza, dzx], axis=1).astype(MXU_DTYPE)
            dwg_ref[k] += _dot(ub, dz, _TN)
            du = d_iu * ig + _dot(dz, wgt_ref[k], _NN)
            dcb_ref[:, sl] += jnp.sum(du, axis=0, keepdims=True)
            win = conv_windows(sl, rsl)
            for kk in range(kw):
                dcw_ref[kk:kk + 1, sl] += jnp.sum(du * win[kk], axis=0, keepdims=True)
            dp_ref[:, rsl] = _conv_t(du, _blocks(carry_du, sl, nh, False), cw).astype(dp_ref.dtype)
            carry_du[:, sl] = du[0:nh * SUBLANES]
        rider(None, None, oldest, r_in, r_out, r_scratch)

    tile = lambda cols: pl.BlockSpec((tm, cols), lambda i: (rev(i), 0))
    vm = lambda rows: pltpu.VMEM((rows, rr), F32)
    grads = [conv_w.shape, lam.shape, wg.shape, lam.shape, lam.shape, lam.shape]
    return pl.pallas_call(
        body, name=name, grid=(nt,),
        in_specs=[tile(d), _resident((d, rr)), tile(r2), tile(rr)] + _halo_specs(tm, rr, 1, rev) + [tile(rr)] * 5
                 + _halo_specs(tm, r2, nh, rev)
                 + [_full(conv_w.shape), _full(wg.shape), _full(wgt.shape), _full(lam.shape)] + list(rider.in_specs),
        out_specs=[tile(r2), tile(d)] + [_full(s) for s in grads] + list(rider.out_specs),
        out_shape=[jax.ShapeDtypeStruct((t, r2), MXU_DTYPE), jax.ShapeDtypeStruct((t, d), MXU_DTYPE)]
                  + [jax.ShapeDtypeStruct(s, F32) for s in grads] + list(rider.out_shape),
        scratch_shapes=[vm(tm)] * 2 + [vm(SUBLANES), vm(nh * SUBLANES)] + list(rider.scratch_shapes),
        compiler_params=pltpu.CompilerParams(dimension_semantics=("arbitrary",), vmem_limit_bytes=VMEM_LIMIT_WIDE_BYTES),
    )(dh, wot, p, hs, hs, *gates, *([p] * nh), conv_w, wg, wgt, lam, *rider.inputs)


def _down_loss_head(z, w, h, g, target, n_meta, tm, name):
    t, d = h.shape
    k = z.shape[1]
    groups = tm // SUBLANES
    nt = t // tm

    def body(z_ref, w_ref, h_ref, g_ref, t_ref, z0_ref, h0_ref, dh_ref, loss_ref, dg_ref, tgt, tbuf, slabs, sems,
             x_even, x_odd):
        i = pl.program_id(0)
        _fetch_time_tile(i, nt, t_ref, jnp.zeros((n_meta, d), F32), tbuf, sems, tm)

        @pl.when(i == 0)
        def _():
            x_even[...] = h0_ref[...] + _dot(z0_ref[...], w_ref[...], _NN)
            dg_ref[...] = jnp.zeros_like(dg_ref)
            loss_ref[...] = jnp.zeros_like(loss_ref)

        def step(x_ref, x_next_ref):
            x_next_ref[...] = h_ref[...] + _dot(z_ref[...], w_ref[...], _NN)
            _time_to_tile_order(tbuf.at[i % 2], slabs, tgt)
            x = x_ref[...]
            ms = jnp.mean(x * x, axis=-1, keepdims=True)
            r = lax.rsqrt(ms + RMS_EPS)
            xhat = x * r
            gg = g_ref[...]
            row = lax.broadcasted_iota(jnp.int32, (tm, 1), 0)
            time = i * tm + jnp.right_shift(row, 3) + jnp.bitwise_and(row, SUBLANES - 1) * groups
            err = jnp.where(time >= n_meta, xhat * gg - tgt[...], 0.0)
            dout = err * (1.0 / d)
            dng = dout * gg
            c = jnp.mean(dng * xhat, axis=-1, keepdims=True)
            dh_ref[...] = r * (dng - xhat * c)
            dg_ref[...] += jnp.sum(dout * xhat, axis=0, keepdims=True)
            loss_ref[...] += jnp.broadcast_to(0.5 * jnp.sum(err * dout, keepdims=True), loss_ref.shape)

        @pl.when(i % 2 == 0)
        def _():
            step(x_even, x_odd)

        @pl.when(i % 2 == 1)
        def _():
            step(x_odd, x_even)

    ahead = lambda i: (jnp.minimum(i + 1, nt - 1), 0)
    return pl.pallas_call(
        body, name=name, grid=(nt,),
        in_specs=[pl.BlockSpec((tm, k), ahead),
                  _resident((k, d)),
                  pl.BlockSpec((tm, d), ahead),
                  pl.BlockSpec((1, d), lambda i: (0, 0)),
                  _ANY,
                  pl.BlockSpec((tm, k), lambda i: (0, 0)),
                  pl.BlockSpec((tm, d), lambda i: (0, 0))],
        out_specs=[pl.BlockSpec((tm, d), lambda i: (i, 0)),
                   pl.BlockSpec((SUBLANES, LANES), lambda i: (0, 0)),
                   pl.BlockSpec((1, d), lambda i: (0, 0))],
        out_shape=[jax.ShapeDtypeStruct((t, d), F32), jax.ShapeDtypeStruct((SUBLANES, LANES), F32),
                   jax.ShapeDtypeStruct((1, d), F32)],
        scratch_shapes=[pltpu.VMEM((tm, d), F32)] + _time_scratch(tm, d) + [pltpu.VMEM((tm, d), F32)] * 2,
        compiler_params=_params("arbitrary"),
    )(z, w, h, g, target, z, h)


def _adamw(w, g, m, v, name):
    rows, cols = w.shape
    tr = rows
    if rows > 512:
        for cand in range(8, 513, 8):
            if rows % cand == 0:
                tr = cand

    def body(w_ref, g_ref, m_ref, v_ref, d_ref, nm_ref, nv_ref):
        g_ = g_ref[...]
        m_ = ADAM_B1 * m_ref[...] + (1.0 - ADAM_B1) * g_
        v_ = ADAM_B2 * v_ref[...] + (1.0 - ADAM_B2) * (g_ * g_)
        m_hat = m_ / (1.0 - ADAM_B1 ** ADAM_STEP)
        v_hat = v_ / (1.0 - ADAM_B2 ** ADAM_STEP)
        d_ref[...] = -ADAM_LR * (m_hat / (jnp.sqrt(v_hat) + ADAM_EPS) + ADAM_WD * w_ref[...])
        nm_ref[...] = m_
        nv_ref[...] = v_

    spec = pl.BlockSpec((tr, cols), lambda i: (i, 0))
    shape = jax.ShapeDtypeStruct((rows, cols), F32)
    return pl.pallas_call(
        body, name=name, grid=(rows // tr,),
        in_specs=[spec] * 4, out_specs=[spec] * 3, out_shape=[shape] * 3,
        compiler_params=_params("parallel"),
    )(w, g, m, v)


def _adamw_nd(w, g, m, v, name):
    shape = w.shape
    two_d = (-1, shape[-1]) if w.ndim > 1 else (1, -1)
    outs = _adamw(w.reshape(two_d), g.reshape(two_d), m.reshape(two_d), v.reshape(two_d), name)
    return tuple(o.reshape(shape) for o in outs)


def _ride_alone(rider, name):
    def body(*refs):
        r_in, _, r_out, _, r_scratch = _split_refs(rider, 0, 0, refs)
        now = pl.program_id(0) == 0
        rider(now, None, None, r_in, r_out, r_scratch)
        rider(None, now, None, r_in, r_out, r_scratch)
        rider(None, None, now, r_in, r_out, r_scratch)

    return pl.pallas_call(
        body, name=name, grid=(1,),
        in_specs=list(rider.in_specs), out_specs=list(rider.out_specs), out_shape=list(rider.out_shape),
        scratch_shapes=list(rider.scratch_shapes),
        compiler_params=_params("arbitrary"),
    )(*rider.inputs)


def _sum_slots(parts, name):
    slots, rows, cols = parts.shape
    tr = _row_tile(rows, 256) if rows % 16 == 0 else rows

    def body(p_ref, o_ref):
        acc = p_ref[0]
        for s in range(1, slots):
            acc = acc + p_ref[s]
        o_ref[...] = acc

    return pl.pallas_call(
        body, name=name, grid=(rows // tr,),
        in_specs=[pl.BlockSpec((slots, tr, cols), lambda i: (0, i, 0))],
        out_specs=pl.BlockSpec((tr, cols), lambda i: (i, 0)),
        out_shape=jax.ShapeDtypeStruct((rows, cols), parts.dtype),
        compiler_params=_params("parallel"),
    )(parts)


def _pad_rows(flat, cols, multiple):
    n = flat.shape[0]
    rows = -(-n // cols)
    rows = -(-rows // multiple) * multiple
    return jnp.pad(flat, (0, rows * cols - n)).reshape(rows, cols)


def _cols_to_chunks(full):
    lead = full.shape[:-1]
    c = full.shape[-1] // N_DEV
    x = full.reshape(-1, N_DEV, c)
    return jnp.transpose(x, (1, 0, 2)).reshape(N_DEV, -1)


def _chunks_to_cols(chunks, lead):
    n = 1
    for s in lead:
        n *= s
    c = chunks.shape[1] // n
    x = chunks.reshape(N_DEV, n, c)
    return jnp.transpose(x, (1, 0, 2)).reshape(tuple(lead) + (N_DEV * c,))


def kernel(x, meta_tokens, norm_mix_g, norm_ffn_g, final_norm_g, sc_w_in, sc_conv_w, sc_w_out, rg_w_in, rg_conv_w, rg_conv_b, rg_w_gate_a, rg_b_gate_a, rg_w_gate_x, rg_b_gate_x, rg_lambda, rg_w_out, ffn_w_up, ffn_conv_w, ffn_w_down, loss_target, m_meta_tokens, m_norm_mix_g, m_norm_ffn_g, m_final_norm_g, m_sc_w_in, m_sc_conv_w, m_sc_w_out, m_rg_w_in, m_rg_conv_w, m_rg_conv_b, m_rg_w_gate_a, m_rg_b_gate_a, m_rg_w_gate_x, m_rg_b_gate_x, m_rg_lambda, m_rg_w_out, m_ffn_w_up, m_ffn_conv_w, m_ffn_w_down, v_meta_tokens, v_norm_mix_g, v_norm_ffn_g, v_final_norm_g, v_sc_w_in, v_sc_conv_w, v_sc_w_out, v_rg_w_in, v_rg_conv_w, v_rg_conv_b, v_rg_w_gate_a, v_rg_b_gate_a, v_rg_w_gate_x, v_rg_b_gate_x, v_rg_lambda, v_rg_w_out, v_ffn_w_up, v_ffn_conv_w, v_ffn_w_down):
    weights = dict(meta_tokens=meta_tokens, norm_mix_g=norm_mix_g, norm_ffn_g=norm_ffn_g, final_norm_g=final_norm_g,
                   sc_w_in=sc_w_in, sc_conv_w=sc_conv_w, sc_w_out=sc_w_out, rg_w_in=rg_w_in, rg_conv_w=rg_conv_w,
                   rg_conv_b=rg_conv_b, rg_w_gate_a=rg_w_gate_a, rg_b_gate_a=rg_b_gate_a, rg_w_gate_x=rg_w_gate_x,
                   rg_b_gate_x=rg_b_gate_x, rg_lambda=rg_lambda, rg_w_out=rg_w_out, ffn_w_up=ffn_w_up,
                   ffn_conv_w=ffn_conv_w, ffn_w_down=ffn_w_down)
    m_in = dict(meta_tokens=m_meta_tokens, norm_mix_g=m_norm_mix_g, norm_ffn_g=m_norm_ffn_g, final_norm_g=m_final_norm_g,
                sc_w_in=m_sc_w_in, sc_conv_w=m_sc_conv_w, sc_w_out=m_sc_w_out, rg_w_in=m_rg_w_in, rg_conv_w=m_rg_conv_w,
                rg_conv_b=m_rg_conv_b, rg_w_gate_a=m_rg_w_gate_a, rg_b_gate_a=m_rg_b_gate_a, rg_w_gate_x=m_rg_w_gate_x,
                rg_b_gate_x=m_rg_b_gate_x, rg_lambda=m_rg_lambda, rg_w_out=m_rg_w_out, ffn_w_up=m_ffn_w_up,
                ffn_conv_w=m_ffn_conv_w, ffn_w_down=m_ffn_w_down)
    v_in = dict(meta_tokens=v_meta_tokens, norm_mix_g=v_norm_mix_g, norm_ffn_g=v_norm_ffn_g, final_norm_g=v_final_norm_g,
                sc_w_in=v_sc_w_in, sc_conv_w=v_sc_conv_w, sc_w_out=v_sc_w_out, rg_w_in=v_rg_w_in, rg_conv_w=v_rg_conv_w,
                rg_conv_b=v_rg_conv_b, rg_w_gate_a=v_rg_w_gate_a, rg_b_gate_a=v_rg_b_gate_a, rg_w_gate_x=v_rg_w_gate_x,
                rg_b_gate_x=v_rg_b_gate_x, rg_lambda=v_rg_lambda, rg_w_out=v_rg_w_out, ffn_w_up=v_ffn_w_up,
                ffn_conv_w=v_ffn_conv_w, ffn_w_down=v_ffn_w_down)
    names = list(weights)

    seq, d = x.shape[1], x.shape[2]
    n_meta = meta_tokens.shape[0]
    n_ffn = ffn_w_up.shape[0]

    assert n_ffn == 2
    def shard_rows(w_in, w_out):
        return [w_in.T.astype(MXU_DTYPE), w_out.astype(MXU_DTYPE)]

    def both_orientations(in_t_gathered, out_gathered):
        w_in_t = in_t_gathered.reshape(-1, d)
        w_out = out_gathered.reshape(-1, d)
        return w_in_t.T, w_in_t, w_out, w_out.T

    shards = {"ffn0": shard_rows(ffn_w_up[0], ffn_w_down[0]), "rg": shard_rows(rg_w_in[0], rg_w_out[0]),
              "ffn1": shard_rows(ffn_w_up[1], ffn_w_down[1])}
    ffn_w_up_f, ffn_w_up_t, ffn_w_down_f, ffn_w_down_t = [None] * 2, [None] * 2, [None] * 2, [None] * 2

    small_names = ["meta_tokens", "sc_conv_w", "rg_conv_w", "rg_conv_b", "rg_b_gate_a", "rg_b_gate_x", "rg_lambda",
                   "ffn_conv_w"]
    small_lead = {n: weights[n].shape[:-1] for n in small_names}
    small_sizes = [weights[n].size for n in small_names]
    small_flat = jnp.concatenate([weights[n].reshape(-1) for n in small_names])
    small_rows = _pad_rows(small_flat, d, SUBLANES)
    sc_in_g, sc_out_g, small_g = _ride_alone(
        _GatherRider(shard_rows(sc_w_in[0], sc_w_out[0]) + [small_rows]), "gather_first")
    sc_w_in_f, sc_w_in_t, sc_w_out_f, sc_w_out_t = both_orientations(sc_in_g, sc_out_g)
    small_g = small_g.reshape(N_DEV, -1)
    small_full = {}
    o = 0
    for n, sz in zip(small_names, small_sizes):
        small_full[n] = _chunks_to_cols(small_g[:, o:o + sz], small_lead[n])
        o += sz

    wg, wgt = _pair_gate_weights(rg_w_gate_a[0].astype(MXU_DTYPE), rg_w_gate_x[0].astype(MXU_DTYPE))
    rg_cw, rg_cb = small_full["rg_conv_w"][0], small_full["rg_conv_b"]
    rg_ba, rg_bx, rg_lam = small_full["rg_b_gate_a"], small_full["rg_b_gate_x"], small_full["rg_lambda"]
    sc_cw = small_full["sc_conv_w"][0]
    ffn_cw = small_full["ffn_conv_w"]

    tp = _row_tile(n_meta + seq, ROW_TILE_PERM)

    def ffn_fwd(h, l, rider):
        n, a, z, *gathered = _ffn_up_fused(h, norm_ffn_g[l:l + 1], ffn_w_up_f[l], ffn_cw[l], tp, f"ffn{l}_up", rider)
        return _matmul_residual(z, ffn_w_down_f[l], h, f"ffn{l}_down"), (n, a, z), gathered

    h0, n0, p0, q0, h1, *ffn0_w = _sc_in_fused(x[0], small_full["meta_tokens"], norm_mix_g[0:1], sc_w_in_f, sc_cw,
                                               sc_w_out_f, tp, "sc_mixer", _GatherRider(shards["ffn0"]))
    ffn_w_up_f[0], ffn_w_up_t[0], ffn_w_down_f[0], ffn_w_down_t[0] = both_orientations(*ffn0_w)
    h2, ffn0_saved, rg_w = ffn_fwd(h1, 0, _GatherRider(shards["rg"]))
    rg_w_in_f, rg_w_in_t, rg_w_out_f, rg_w_out_t = both_orientations(*rg_w)
    n2, p2, y2, hs2, *rest2 = _rg_in_fused(h2, norm_mix_g[1:2], rg_w_in_f, rg_cw, rg_cb, wg, rg_ba, rg_bx, rg_lam,
                                           rg_w_out_f, tp, "rg_mixer", _GatherRider(shards["ffn1"]))
    rg_gates, h3, ffn1_w = rest2[:5], rest2[5], rest2[6:]
    ffn_w_up_f[1], ffn_w_up_t[1], ffn_w_down_f[1], ffn_w_down_t[1] = both_orientations(*ffn1_w)
    ffn1_saved = tuple(_ffn_up_fused(h3, norm_ffn_g[1:2], ffn_w_up_f[1], ffn_cw[1], tp, "ffn1_up"))
    dh4, loss_tile, d_final_g = _down_loss_head(ffn1_saved[2], ffn_w_down_f[1], h3, final_norm_g.reshape(1, d),
                                                loss_target[0], n_meta, tp, "ffn1_down_loss_head")
    loss = lax.psum(loss_tile[0, 0], AXES)

    arrived = {}

    def ffn_bwd(dh_out, h_in, saved, l):
        n, a, z = saved
        da, dhb, d_cw = _ffn_down_bwd_fused(dh_out, ffn_w_down_t[l], a, ffn_cw[l], tp, f"ffn{l}_down_bwd")
        d_w_down = _wgrad(z, dhb, f"ffn{l}_down_wgrad")
        d_w_up_t, arrived[f"ffn_w_down{l}"] = _wgrad(da, n, f"ffn{l}_up_wgrad", _ScatterRider(d_w_down))
        rows = d_w_up_t.shape[0] // N_DEV
        first_rows = (rows // 2) // (2 * SUBLANES) * (2 * SUBLANES)
        dh_in, d_g, arrived[f"ffn_w_up{l}a"] = _dgrad_in_norm(
            da, ffn_w_up_t[l], h_in, norm_ffn_g[l:l + 1], dh_out, f"ffn{l}_up_dgrad",
            _ScatterRider(d_w_up_t, (0, first_rows)))
        return dh_in, d_cw, d_g, _ScatterRider(d_w_up_t, (first_rows, rows - first_rows))

    dh3, d_fcw1, d_fg1, up1_rest = ffn_bwd(dh4, h3, ffn1_saved, 1)

    dp2, dhb3, d_rg_cw, d_rg_cb, d_wg, d_rg_ba, d_rg_bx, d_rg_lam, arrived["ffn_w_up1b"] = _rg_out_bwd_fused(
        dh3, rg_w_out_t, p2, hs2, rg_gates, rg_cw, wg, wgt, rg_lam, tp, "rg_out_bwd", up1_rest)
    d_wa, d_wx = _unpair_gate_grads(d_wg, rg_w_gate_a.shape[2])
    d_rg_w_out = _wgrad(y2, dhb3, "rg_out_wgrad")
    d_rg_w_in_t, arrived["rg_w_out"] = _wgrad(dp2, n2, "rg_in_wgrad", _ScatterRider(d_rg_w_out))
    dh2, d_mg1, arrived["rg_w_in"] = _dgrad_in_norm(
        dp2, rg_w_in_t, h2, norm_mix_g[1:2], dh3, "rg_in_dgrad", _ScatterRider(d_rg_w_in_t))

    dh1, d_fcw0, d_fg0, up0_rest = ffn_bwd(dh2, h1, ffn0_saved, 0)

    dp0, dhb1, d_sc_cw, arrived["ffn_w_up0b"] = _sc_out_bwd_fused(dh1, sc_w_out_t, p0, sc_cw, tp, "sc_out_bwd", up0_rest)
    d_sc_w_out = _wgrad(q0, dhb1, "sc_out_wgrad")
    d_sc_w_in_t, arrived["sc_w_out"] = _wgrad(dp0, n0, "sc_in_wgrad", _ScatterRider(d_sc_w_out))
    d_mg0, d_meta, d_x, arrived["sc_w_in"] = _dgrad_in_to_input(
        dp0, sc_w_in_t, h0, norm_mix_g[0:1], dh1, n_meta, tp, "sc_in_dgrad", _ScatterRider(d_sc_w_in_t))
    grad_x = d_x[None]

    grads = {}
    for n in ("sc_w_in", "rg_w_in"):
        grads[n] = _sum_slots(arrived[n], f"sum_{n}").T[None]
    for n in ("sc_w_out", "rg_w_out"):
        grads[n] = _sum_slots(arrived[n], f"sum_{n}")[None]
    grads["ffn_w_up"] = jnp.stack([jnp.concatenate(
        [_sum_slots(arrived[f"ffn_w_up{l}{part}"], f"sum_ffn_w_up{l}{part}") for part in "ab"], axis=0).T
        for l in range(n_ffn)])
    grads["ffn_w_down"] = jnp.stack([_sum_slots(arrived[f"ffn_w_down{l}"], f"sum_ffn_w_down{l}") for l in range(n_ffn)])

    small_grads = {"meta_tokens": d_meta,"sc_conv_w": d_sc_cw[None], "rg_conv_w": d_rg_cw[None],
                   "rg_conv_b": d_rg_cb, "rg_b_gate_a": d_rg_ba, "rg_b_gate_x": d_rg_bx, "rg_lambda": d_rg_lam,
                   "ffn_conv_w": jnp.stack([d_fcw0, d_fcw1])}
    small_chunks = jnp.concatenate([_cols_to_chunks(small_grads[n]) for n in small_names], axis=1)
    pad = small_rows.shape[0] * d - small_chunks.shape[1]
    small_chunks = jnp.pad(small_chunks, ((0, 0), (0, pad))).reshape(N_DEV, small_rows.shape[0], d)
    rep_names = ["norm_mix_g", "norm_ffn_g", "final_norm_g", "rg_w_gate_a", "rg_w_gate_x"]
    rep_grads = {"norm_mix_g": jnp.concatenate([d_mg0, d_mg1], axis=0),
                 "norm_ffn_g": jnp.concatenate([d_fg0, d_fg1], axis=0),
                 "final_norm_g": d_final_g.reshape(-1), "rg_w_gate_a": d_wa[None], "rg_w_gate_x": d_wx[None]}
    rep_flat = jnp.concatenate([rep_grads[n].reshape(-1) for n in rep_names])
    rep_chunk_rows = -(-rep_flat.shape[0] // (N_DEV * d))
    rep_chunk_rows += -(small_rows.shape[0] + rep_chunk_rows) % 16
    rep_chunks = jnp.pad(rep_flat, (0, N_DEV * rep_chunk_rows * d - rep_flat.shape[0])).reshape(N_DEV, rep_chunk_rows, d)

    last_chunks = jnp.concatenate([small_chunks, rep_chunks], axis=1)
    reduced = _sum_slots(_ride_alone(_ScatterRider(last_chunks.reshape(-1, d)), "scatter_last")[0], "sum_last")

    small_red = reduced[0:small_rows.shape[0]].reshape(-1)
    o = small_rows.shape[0]
    so = 0
    for n, sz in zip(small_names, small_sizes):
        grads[n] = small_red[so:so + sz].reshape(weights[n].shape)
        so += sz
    rep_red = _ride_alone(_GatherRider([reduced[o:o + rep_chunks.shape[1]]]), "gather_replicated_grads")[0].reshape(-1)
    ro = 0
    for n in rep_names:
        sz = weights[n].size
        grads[n] = rep_red[ro:ro + sz].reshape(weights[n].shape)
        ro += sz

    delta, new_m, new_v = {}, {}, {}
    for n in names:
        delta[n], new_m[n], new_v[n] = _adamw_nd(weights[n], grads[n], m_in[n], v_in[n], f"adamw_{n}")

    return (loss, grad_x, *[grads[n] for n in names], *[delta[n] for n in names],
            *[new_m[n] for n in names], *[new_v[n] for n in names])
```

```python
import jax
import jax.numpy as jnp
from jax import lax
from jax.experimental import pallas as pl
from jax.experimental.pallas import tpu as pltpu

F32 = jnp.float32
MXU_DTYPE = jnp.bfloat16
RMS_EPS = 1e-6
RG_C = 8.0
ADAM_LR = 0.001
ADAM_B1 = 0.9
ADAM_B2 = 0.999
ADAM_EPS = 1e-08
ADAM_WD = 0.01
ADAM_STEP = 10

N_DEV = 8
AXES = ("x", "y", "c")
SUBLANES = 8
LANES = 128
VMEM_LIMIT_BYTES = 48 * 1024 * 1024
ROW_TILE_MATMUL = 700
VMEM_LIMIT_WIDE_BYTES = 58 * 1024 * 1024
ROW_TILE_WGRAD = 3300
COL_TILE_WGRAD = 1408
ROW_TILE_PERM = 400
STRIP = 256
STRIP_FFN = 4096

_TN = (((0,), (0,)), ((), ()))
_NN = (((1,), (0,)), ((), ()))


def _row_tile(t, target):
    best = None
    for tm in range(16, t + 1, 16):
        if t % tm == 0 and tm <= target:
            best = tm
    return best if best is not None else t


def _col_tile(n, target):
    best = None
    for tn in range(LANES, n + 1, LANES):
        if n % tn == 0 and tn <= target:
            best = tn
    return best if best is not None else n


def _params(*sem):
    return pltpu.CompilerParams(dimension_semantics=sem, vmem_limit_bytes=VMEM_LIMIT_BYTES)


def _dot(a, b, dims):
    return lax.dot_general(a, b, dims, preferred_element_type=F32)


def _sigmoid(x):
    return 1.0 / (1.0 + jnp.exp(-x))


def _sigmoid_tanh(x):
    return 0.5 * jnp.tanh(0.5 * x) + 0.5


def _gelu(x):
    c = 0.7978845608028654
    t = jnp.tanh(c * (x + 0.044715 * (x * x * x)))
    return 0.5 * x * (1.0 + t)


def _gelu_and_grad(x):
    c = 0.7978845608028654
    x2 = x * x
    t = jnp.tanh(c * (x + 0.044715 * (x2 * x)))
    half = 0.5 * (1.0 + t)
    return x * half, half + 0.5 * x * (1.0 - t * t) * c * (1.0 + 3.0 * 0.044715 * x2)


def _softplus(x):
    return jnp.maximum(x, 0.0) + jnp.log1p(jnp.exp(-jnp.abs(x)))


def _time_scratch(tm, d):
    return [pltpu.VMEM((2, tm, d), F32), pltpu.VMEM((d // LANES, tm, LANES), F32), pltpu.SemaphoreType.DMA((2,))]


def _fetch_time_tile(i, nt, src_ref, head, tbuf, sems, tm):
    n_head = head.shape[0]

    def tile(j, slot):
        start = pl.multiple_of(j * tm - n_head, SUBLANES)
        return pltpu.make_async_copy(src_ref.at[pl.ds(start, tm)], tbuf.at[slot], sems.at[slot])

    first = pltpu.make_async_copy(src_ref.at[pl.ds(0, tm - n_head)], tbuf.at[0, pl.ds(n_head, tm - n_head)], sems.at[0])

    @pl.when(i == 0)
    def _():
        tbuf[0, 0:n_head, :] = head
        first.start()

    @pl.when(i + 1 < nt)
    def _():
        tile(i + 1, (i + 1) % 2).start()

    @pl.when(i == 0)
    def _():
        first.wait()

    @pl.when(i > 0)
    def _():
        tile(i, i % 2).wait()


def _time_to_tile_order(t_ref, slabs, out_ref):
    tm, d = out_ref.shape
    groups = tm // SUBLANES
    for k in range(d // LANES):
        slabs[k] = t_ref[:, k * LANES:(k + 1) * LANES]
    for k in range(d // LANES):
        for g in range(groups):
            out_ref[g * SUBLANES:(g + 1) * SUBLANES, k * LANES:(k + 1) * LANES] = slabs[k, pl.ds(g, SUBLANES, stride=groups), :]


def _tile_to_time_order(p_ref, slabs, t_ref):
    tm, d = p_ref.shape
    groups = tm // SUBLANES
    for k in range(d // LANES):
        for g in range(groups):
            slabs[k, pl.ds(g, SUBLANES, stride=groups), :] = p_ref[g * SUBLANES:(g + 1) * SUBLANES, k * LANES:(k + 1) * LANES]
    for k in range(d // LANES):
        t_ref[:, k * LANES:(k + 1) * LANES] = slabs[k]


def _rows(ref, sl):
    return [ref[k:k + 1, sl] for k in range(ref.shape[0])]


def _shift_down(x, before, s):
    if s == 0:
        return x
    n = x.shape[0]
    row = lax.broadcasted_iota(jnp.int32, (SUBLANES, x.shape[1]), 0)
    heads = []
    for g in range(s):
        v = x[n - (s - g) * SUBLANES:n - (s - g - 1) * SUBLANES]
        heads.append(pltpu.roll(jnp.where(row == SUBLANES - 1, before[s - g - 1], v), 1, axis=0))
    return jnp.concatenate(heads + [x[0:n - s * SUBLANES]], axis=0)


def _shift_up(x, after, s):
    if s == 0:
        return x
    row = lax.broadcasted_iota(jnp.int32, (SUBLANES, x.shape[1]), 0)
    tails = []
    for m in range(s):
        v = x[m * SUBLANES:(m + 1) * SUBLANES]
        tails.append(pltpu.roll(jnp.where(row == 0, after[m], v), SUBLANES - 1, axis=0))
    return jnp.concatenate([x[s * SUBLANES:]] + tails, axis=0)


def _weighted(w, windows):
    y = w[0] * windows[0]
    for k in range(1, len(w)):
        y = y + w[k] * windows[k]
    return y


def _windows(x, before, k_width):
    return [_shift_down(x, before, k_width - 1 - k) for k in range(k_width)]


def _conv_t(dy, after, w):
    k_width = len(w)
    return _weighted(w, [_shift_up(dy, after, k_width - 1 - k) for k in range(k_width)])


def _blocks(ref, sl, count, newest_first):
    n = ref.shape[0] // SUBLANES
    order = range(n - 1, n - 1 - count, -1) if newest_first else range(count)
    return [ref[b * SUBLANES:(b + 1) * SUBLANES, sl] for b in order]


def _halo_specs(tm, cols, count, tile_of):
    def spec(k):
        return pl.BlockSpec((SUBLANES, cols), lambda i: (jnp.maximum(tile_of(i) * (tm // SUBLANES) - k, 0), 0))
    return [spec(k) for k in range(1, count + 1)]


def _scan_tile(coef, val, out, carry, groups, reverse):
    cols = coef.shape[1]
    row = lax.broadcasted_iota(jnp.int32, (SUBLANES, cols), 0)

    def blk(i):
        g = groups - 1 - i if reverse else i
        return pl.ds(pl.multiple_of(g * SUBLANES, SUBLANES), SUBLANES)

    def local(i, pl_):
        p_prev, l_prev = pl_
        a = coef[blk(i), :]
        p = a * p_prev
        l = a * l_prev + val[blk(i), :]
        coef[blk(i), :] = p
        val[blk(i), :] = l
        return p, l

    pf, lf = lax.fori_loop(0, groups, local, (jnp.ones((SUBLANES, cols), F32), jnp.zeros((SUBLANES, cols), F32)))
    for s in (1, 2, 4):
        keep, sh = (row < SUBLANES - s, SUBLANES - s) if reverse else (row >= s, s)
        p_s = jnp.where(keep, pltpu.roll(pf, sh, axis=0), 1.0)
        l_s = jnp.where(keep, pltpu.roll(lf, sh, axis=0), 0.0)
        lf = pf * l_s + lf
        pf = pf * p_s
    end = lf + pf * carry
    if reverse:
        init = jnp.where(row == SUBLANES - 1, carry, pltpu.roll(end, SUBLANES - 1, axis=0))
        leaving = jnp.broadcast_to(end[0:1, :], (SUBLANES, cols))
    else:
        init = jnp.where(row == 0, carry, pltpu.roll(end, 1, axis=0))
        leaving = jnp.broadcast_to(end[SUBLANES - 1:SUBLANES, :], (SUBLANES, cols))

    def fix(i, _):
        out[blk(i), :] = val[blk(i), :] + coef[blk(i), :] * init
        return 0

    lax.fori_loop(0, groups, fix, 0)
    return leaving


def _resident(shape):
    return pl.BlockSpec(shape, lambda *_: (0,) * len(shape), pipeline_mode=pl.Buffered(1))


def _full(shape):
    return pl.BlockSpec(shape, lambda *_: (0,) * len(shape))


def _rmsnorm_to(h_ref, g_ref, n_ref):
    x = h_ref[...]
    ms = jnp.mean(x * x, axis=-1, keepdims=True)
    n_ref[...] = (x * lax.rsqrt(ms + RMS_EPS) * g_ref[...]).astype(n_ref.dtype)


_ANY = pl.BlockSpec(memory_space=pl.ANY)
_MESH = pl.DeviceIdType.MESH


def _dma_sems(n):
    return pltpu.SemaphoreType.DMA((n,))


def _when_each(*phases):
    for cond, fn in phases:
        if cond is not None:
            pl.when(cond)(fn)


class _NoRider:
    inputs = in_specs = out_shape = out_specs = scratch_shapes = ()

    def __call__(self, first, middle, last, ins, outs, scratch):
        pass


_NO_RIDER = _NoRider()


def _hand_on_step(nt):
    return (3 * nt) // 4


def _split_refs(rider, n_out, n_scratch, rest):
    a = len(rider.inputs)
    b = a + n_out
    c = b + len(rider.out_shape)
    e = c + n_scratch
    return rest[:a], rest[a:b], rest[b:c], rest[c:e], rest[e:]


class _GatherRider:
    def __init__(self, blocks):
        n = len(blocks)
        self.inputs = tuple(blocks)
        self.in_specs = (_ANY,) * n
        self.out_shape = tuple(jax.ShapeDtypeStruct((N_DEV,) + b.shape, b.dtype) for b in blocks)
        self.out_specs = (_ANY,) * n
        self.scratch_shapes = (_dma_sems(7 * n), _dma_sems(7 * n), _dma_sems(n))

    def __call__(self, first, middle, last, ins, outs, scratch):
        n = len(ins)
        send_sems, recv_sems, local_sems = scratch
        x, y, c = lax.axis_index("x"), lax.axis_index("y"), lax.axis_index("c")
        me, sibling = (x, y, c), (x, y, 1 - c)
        chips = [(1 - x, y), (x, 1 - y), (1 - x, 1 - y)]

        def slot(b, px, py, pc):
            return outs[b].at[4 * px + 2 * py + pc]

        def copy(k, b, block, to, src=None):
            return pltpu.make_async_remote_copy(
                src_ref=slot(b, *block) if src is None else src, dst_ref=slot(b, *block),
                send_sem=send_sems.at[k * n + b], recv_sem=recv_sems.at[k * n + b], device_id=to, device_id_type=_MESH)

        mine = [pltpu.make_async_copy(ins[b], slot(b, *me), local_sems.at[b]) for b in range(n)]
        own = [copy(0, b, me, sibling, src=ins[b]) for b in range(n)]
        own += [copy(1 + j, b, me, (*chip, c), src=ins[b]) for j, chip in enumerate(chips) for b in range(n)]
        passed = [[copy(4 + j, b, (*chip, c), sibling) for b in range(n)] for j, chip in enumerate(chips)]

        def at_first():
            for cp in mine + own:
                cp.start()

        def at_middle():
            for j, chip in enumerate(chips):
                for b in range(n):
                    copy(1 + j, b, (*chip, c), me).wait_recv()
                    passed[j][b].start()

        def at_last():
            for b in range(n):
                copy(0, b, sibling, me).wait_recv()
                for j, chip in enumerate(chips):
                    copy(4 + j, b, (*chip, 1 - c), me).wait_recv()
            for cp in own + [cp for group in passed for cp in group]:
                cp.wait_send()
            for cp in mine:
                cp.wait()

        _when_each((first, at_first), (middle, at_middle), (last, at_last))


class _ScatterRider:
    def __init__(self, grad, rows=None):
        r = grad.shape[0] // N_DEV
        chunks = grad.reshape(N_DEV, r, grad.shape[1])
        self.rows = rows if rows is not None else (0, r)
        self.inputs = (chunks,)
        self.in_specs = (_ANY,)
        self.out_shape = (jax.ShapeDtypeStruct((N_DEV, self.rows[1], grad.shape[1]), chunks.dtype),)
        self.out_specs = (_ANY,)
        self.scratch_shapes = (_dma_sems(N_DEV - 1), _dma_sems(N_DEV - 1), pltpu.SemaphoreType.DMA(()))

    def __call__(self, first, middle, last, ins, outs, scratch):
        (g_ref,), (r_ref,) = ins, outs
        send_sems, recv_sems, local_sem = scratch
        x, y, c = lax.axis_index("x"), lax.axis_index("y"), lax.axis_index("c")
        me = 4 * x + 2 * y + c
        part = pl.ds(*self.rows)
        mine = pltpu.make_async_copy(g_ref.at[me, part], r_ref.at[me], local_sem)
        copies = []
        for k in range(1, N_DEV):
            px, py, pc = (1 - x if k & 4 else x), (1 - y if k & 2 else y), (1 - c if k & 1 else c)
            copies.append(pltpu.make_async_remote_copy(
                src_ref=g_ref.at[4 * px + 2 * py + pc, part], dst_ref=r_ref.at[me],
                send_sem=send_sems.at[k - 1], recv_sem=recv_sems.at[k - 1],
                device_id=(px, py, pc), device_id_type=_MESH))

        def at_first():
            mine.start()
            for cp in copies:
                cp.start()

        def at_last():
            for cp in copies:
                cp.wait()
            mine.wait()

        _when_each((first, at_first), (last, at_last))


def _matmul_residual(q, w, h, name):
    t, k = q.shape
    d = w.shape[1]
    tm = _row_tile(t, ROW_TILE_MATMUL)

    def body(q_ref, w_ref, h_ref, o_ref):
        o_ref[...] = h_ref[...] + _dot(q_ref[...], w_ref[...], _NN)

    return pl.pallas_call(
        body, name=name, grid=(t // tm,),
        in_specs=[pl.BlockSpec((tm, k), lambda i: (i, 0)),
                  _resident((k, d)),
                  pl.BlockSpec((tm, d), lambda i: (i, 0))],
        out_specs=pl.BlockSpec((tm, d), lambda i: (i, 0)),
        out_shape=jax.ShapeDtypeStruct((t, d), F32),
        compiler_params=_params("parallel"),
    )(q, w, h)


def _dgrad_in_norm(dp, wt, h, g, dh_next, name, rider=_NO_RIDER):
    t, n = dp.shape
    d = wt.shape[1]
    tm = _row_tile(t, ROW_TILE_MATMUL)
    nt = t // tm

    def body(dp_ref, wt_ref, h_ref, g_ref, dhn_ref, *rest):
        r_in, (dh_ref, dg_ref), r_out, _, r_scratch = _split_refs(rider, 2, 0, rest)
        i = pl.program_id(0)
        rider(i == 0, i == nt // 2, None, r_in, r_out, r_scratch)
        dn = _dot(dp_ref[...], wt_ref[...], _NN)
        x = h_ref[...]
        ms = jnp.mean(x * x, axis=-1, keepdims=True)
        r = lax.rsqrt(ms + RMS_EPS)
        xhat = x * r
        dng = dn * g_ref[...]
        c = jnp.mean(dng * xhat, axis=-1, keepdims=True)
        dh_ref[...] = dhn_ref[...] + r * (dng - xhat * c)
        part = jnp.sum(dn * xhat, axis=0, keepdims=True)

        @pl.when(i == 0)
        def _():
            dg_ref[...] = part

        @pl.when(i > 0)
        def _():
            dg_ref[...] += part

        rider(None, None, i == nt - 1, r_in, r_out, r_scratch)

    return pl.pallas_call(
        body, name=name, grid=(nt,),
        in_specs=[pl.BlockSpec((tm, n), lambda i: (i, 0)),
                  _resident((n, d)),
                  pl.BlockSpec((tm, d), lambda i: (i, 0)),
                  pl.BlockSpec((1, d), lambda i: (0, 0)),
                  pl.BlockSpec((tm, d), lambda i: (i, 0))] + list(rider.in_specs),
        out_specs=[pl.BlockSpec((tm, d), lambda i: (i, 0)),
                   pl.BlockSpec((1, d), lambda i: (0, 0))] + list(rider.out_specs),
        out_shape=[jax.ShapeDtypeStruct((t, d), F32), jax.ShapeDtypeStruct((1, d), F32)] + list(rider.out_shape),
        scratch_shapes=list(rider.scratch_shapes),
        compiler_params=_params("arbitrary"),
    )(dp, wt, h, g, dh_next, *rider.inputs)


def _dgrad_in_to_input(dp, wt, h, g, dh_next, n_head, tm, name, rider=_NO_RIDER):
    t, n = dp.shape
    d = wt.shape[1]
    nt = t // tm

    def body(dp_ref, wt_ref, h_ref, g_ref, dhn_ref, *rest):
        r_in, (dg_ref, head_ref, rest_ref), r_out, (pbuf, tout, slabs, sems), r_scratch = _split_refs(rider, 3, 4, rest)
        i = pl.program_id(0)
        rider(i == 0, i == nt // 2, None, r_in, r_out, r_scratch)
        dn = _dot(dp_ref[...], wt_ref[...], _NN)
        x = h_ref[...]
        ms = jnp.mean(x * x, axis=-1, keepdims=True)
        r = lax.rsqrt(ms + RMS_EPS)
        xhat = x * r
        dng = dn * g_ref[...]
        c = jnp.mean(dng * xhat, axis=-1, keepdims=True)
        pbuf[...] = dhn_ref[...] + r * (dng - xhat * c)
        part = jnp.sum(dn * xhat, axis=0, keepdims=True)

        @pl.when(i == 0)
        def _():
            dg_ref[...] = part

        @pl.when(i > 0)
        def _():
            dg_ref[...] += part

        def store(j):
            if isinstance(j, int) and j == 0:
                return pltpu.make_async_copy(tout.at[0, pl.ds(n_head, tm - n_head)],
                                             rest_ref.at[pl.ds(0, tm - n_head)], sems.at[0])
            start = j * tm - n_head
            start = start if isinstance(start, int) else pl.multiple_of(start, SUBLANES)
            return pltpu.make_async_copy(tout.at[j % 2], rest_ref.at[pl.ds(start, tm)], sems.at[j % 2])

        @pl.when(i == 2)
        def _():
            store(0).wait()

        @pl.when(i > 2)
        def _():
            store(i - 2).wait()

        _tile_to_time_order(pbuf, slabs, tout.at[i % 2])

        @pl.when(i == 0)
        def _():
            head_ref[...] = tout[0, 0:n_head, :]
            store(0).start()

        @pl.when(i > 0)
        def _():
            store(i).start()

        @pl.when(i == nt - 1)
        def _():
            for j in (nt - 2, nt - 1):
                if j >= 0:
                    store(j).wait()

        rider(None, None, i == nt - 1, r_in, r_out, r_scratch)

    return pl.pallas_call(
        body, name=name, grid=(nt,),
        in_specs=[pl.BlockSpec((tm, n), lambda i: (i, 0)),
                  _resident((n, d)),
                  pl.BlockSpec((tm, d), lambda i: (i, 0)),
                  pl.BlockSpec((1, d), lambda i: (0, 0)),
                  pl.BlockSpec((tm, d), lambda i: (i, 0))] + list(rider.in_specs),
        out_specs=[pl.BlockSpec((1, d), lambda i: (0, 0)), _full((n_head, d)), _ANY] + list(rider.out_specs),
        out_shape=[jax.ShapeDtypeStruct((1, d), F32), jax.ShapeDtypeStruct((n_head, d), F32),
                   jax.ShapeDtypeStruct((t - n_head, d), F32)] + list(rider.out_shape),
        scratch_shapes=[pltpu.VMEM((tm, d), F32)] + _time_scratch(tm, d) + list(rider.scratch_shapes),
        compiler_params=_params("arbitrary"),
    )(dp, wt, h, g, dh_next, *rider.inputs)


def _wgrad(a, b, name, rider=_NO_RIDER):
    t, m = a.shape
    d = b.shape[1]
    tmm, tk = _col_tile(m, COL_TILE_WGRAD), _row_tile(t, ROW_TILE_WGRAD)
    nm, nk = m // tmm, t // tk

    def body(a_ref, b_ref, *rest):
        r_in, (o_ref,), r_out, _, r_scratch = _split_refs(rider, 1, 0, rest)
        i, k = pl.program_id(0), pl.program_id(1)
        rider(jnp.logical_and(i == 0, k == 0), None, None, r_in, r_out, r_scratch)

        @pl.when(k == 0)
        def _():
            o_ref[...] = jnp.zeros_like(o_ref)

        o_ref[...] += _dot(a_ref[...], b_ref[...], _TN)
        rider(None, None, jnp.logical_and(i == nm - 1, k == nk - 1), r_in, r_out, r_scratch)

    outs = pl.pallas_call(
        body, name=name, grid=(nm, nk),
        in_specs=[pl.BlockSpec((tk, tmm), lambda i, k: (k, i)),
                  pl.BlockSpec((tk, d), lambda i, k: (k, 0))] + list(rider.in_specs),
        out_specs=[pl.BlockSpec((tmm, d), lambda i, k: (i, 0))] + list(rider.out_specs),
        out_shape=[jax.ShapeDtypeStruct((m, d), F32)] + list(rider.out_shape),
        scratch_shapes=list(rider.scratch_shapes),
        compiler_params=pltpu.CompilerParams(dimension_semantics=("arbitrary", "arbitrary"),
                                             vmem_limit_bytes=VMEM_LIMIT_WIDE_BYTES),
    )(a, b, *rider.inputs)
    return outs if rider.out_shape else outs[0]


def _ffn_up_fused(h, g, w, conv_w, tm, name, rider=_NO_RIDER):
    t, d = h.shape
    f2 = w.shape[1]
    f = f2 // 2
    cw = min(STRIP_FFN, f)
    kw = conv_w.shape[0]
    nh = kw - 1
    nt = t // tm

    def body(h_ref, g_ref, w_ref, cw_ref, *rest):
        r_in, (n_ref, a_ref, z_ref), r_out, (carry,), r_scratch = _split_refs(rider, 3, 1, rest)
        i = pl.program_id(0)
        rider(i == 0, i == _hand_on_step(nt), None, r_in, r_out, r_scratch)

        @pl.when(i == 0)
        def _():
            carry[...] = jnp.zeros_like(carry)

        _rmsnorm_to(h_ref, g_ref, n_ref)
        for c in range(0, f, cw):
            conv = []
            for sl in (slice(c, c + cw), slice(f + c, f + c + cw)):
                a = _dot(n_ref[...], w_ref[:, sl], _NN)
                a_ref[:, sl] = a
                conv.append(_weighted(_rows(cw_ref, sl), _windows(a, _blocks(carry, sl, nh, True), kw)))
                carry[:, sl] = a[tm - nh * SUBLANES:tm]
            gg, vv = conv
            z_ref[:, c:c + cw] = (gg * _sigmoid(gg) * vv).astype(z_ref.dtype)
        rider(None, None, i == nt - 1, r_in, r_out, r_scratch)

    return pl.pallas_call(
        body, name=name, grid=(nt,),
        in_specs=[pl.BlockSpec((tm, d), lambda i: (i, 0)), pl.BlockSpec((1, d), lambda i: (0, 0)),
                  _resident((d, f2)), _full(conv_w.shape)] + list(rider.in_specs),
        out_specs=[pl.BlockSpec((tm, d), lambda i: (i, 0)), pl.BlockSpec((tm, f2), lambda i: (i, 0)),
                   pl.BlockSpec((tm, f), lambda i: (i, 0))] + list(rider.out_specs),
        out_shape=[jax.ShapeDtypeStruct((t, d), MXU_DTYPE), jax.ShapeDtypeStruct((t, f2), F32),
                   jax.ShapeDtypeStruct((t, f), MXU_DTYPE)] + list(rider.out_shape),
        scratch_shapes=[pltpu.VMEM((nh * SUBLANES, f2), F32)] + list(rider.scratch_shapes),
        compiler_params=_params("arbitrary"),
    )(h, g, w, conv_w, *rider.inputs)


def _ffn_down_bwd_fused(dh, wdt, a, conv_w, tm, name):
    t, d = dh.shape
    f = wdt.shape[1]
    f2 = 2 * f
    nt = t // tm
    cw = min(STRIP_FFN, f)
    kw = conv_w.shape[0]
    nh = kw - 1
    rev = lambda i: nt - 1 - i

    def body(dh_ref, wdt_ref, a_ref, *rest):
        halo_refs = rest[:nh]
        cw_ref, da_ref, dhb_ref, dw_ref, carry = rest[nh:]
        i = pl.program_id(0)
        newest, oldest = i == 0, i == nt - 1

        @pl.when(newest)
        def _():
            carry[...] = jnp.zeros_like(carry)
            dw_ref[...] = jnp.zeros_like(dw_ref)

        dhb_ref[...] = dh_ref[...].astype(dhb_ref.dtype)
        for c in range(0, f, cw):
            gsl, vsl = slice(c, c + cw), slice(f + c, f + c + cw)
            dz = _dot(dhb_ref[...], wdt_ref[:, gsl], _NN)
            win, conv = {}, {}
            for sl in (gsl, vsl):
                before = [jnp.where(oldest, 0.0, r[:, sl]) for r in halo_refs]
                win[sl.start] = _windows(a_ref[:, sl], before, kw)
                conv[sl.start] = _weighted(_rows(cw_ref, sl), win[sl.start])
            gg, vv = conv[gsl.start], conv[vsl.start]
            s = _sigmoid(gg)
            grads = {gsl.start: dz * vv * s * (1.0 + gg * (1.0 - s)), vsl.start: dz * gg * s}
            for sl in (gsl, vsl):
                dconv = grads[sl.start]
                da_ref[:, sl] = _conv_t(dconv, _blocks(carry, sl, nh, False), _rows(cw_ref, sl)).astype(da_ref.dtype)
                for k in range(kw):
                    dw_ref[k:k + 1, sl] += jnp.sum(dconv * win[sl.start][k], axis=0, keepdims=True)
                carry[:, sl] = dconv[0:nh * SUBLANES]

    return pl.pallas_call(
        body, name=name, grid=(nt,),
        in_specs=[pl.BlockSpec((tm, d), lambda i: (rev(i), 0)), _resident((d, f)),
                  pl.BlockSpec((tm, f2), lambda i: (rev(i), 0))] + _halo_specs(tm, f2, nh, rev) + [_full(conv_w.shape)],
        out_specs=[pl.BlockSpec((tm, f2), lambda i: (rev(i), 0)), pl.BlockSpec((tm, d), lambda i: (rev(i), 0)),
                   _full(conv_w.shape)],
        out_shape=[jax.ShapeDtypeStruct((t, f2), MXU_DTYPE), jax.ShapeDtypeStruct((t, d), MXU_DTYPE),
                   jax.ShapeDtypeStruct(conv_w.shape, F32)],
        scratch_shapes=[pltpu.VMEM((nh * SUBLANES, f2), F32)],
        compiler_params=_params("arbitrary"),
    )(dh, wdt, a, *([a] * nh), conv_w)


def _sc_in_fused(x, meta, g, w, conv_w, w_out, tm, name, rider=_NO_RIDER):
    d = x.shape[1]
    t = x.shape[0] + meta.shape[0]
    cw = min(STRIP, d)
    kw = conv_w.shape[0]
    nh = kw - 1
    nt = t // tm

    def body(x_ref, meta_ref, g_ref, w_ref, cw_ref, wo_ref, *rest):
        r_in, (h_ref, n_ref, p_ref, q_ref, h1_ref), r_out, (carry, tbuf, slabs, sems), r_scratch = _split_refs(
            rider, 5, 4, rest)
        i = pl.program_id(0)
        rider(i == 0, i == _hand_on_step(nt), None, r_in, r_out, r_scratch)

        @pl.when(i == 0)
        def _():
            carry[...] = jnp.zeros_like(carry)

        _fetch_time_tile(i, nt, x_ref, meta_ref[...], tbuf, sems, tm)
        _time_to_tile_order(tbuf.at[i % 2], slabs, h_ref)
        _rmsnorm_to(h_ref, g_ref, n_ref)
        for c in range(0, d, cw):
            sl = slice(c, c + cw)
            parts = []
            for base in (0, d, 2 * d):
                psl = slice(base + c, base + c + cw)
                parts.append(_dot(n_ref[...], w_ref[:, psl], _NN))
                p_ref[:, psl] = parts[-1]
            bg, cg, v = parts
            cv = cg * v
            u = _weighted(_rows(cw_ref, sl), _windows(cv, _blocks(carry, sl, nh, True), kw))
            carry[:, sl] = cv[tm - nh * SUBLANES:tm]
            q_ref[:, sl] = (bg * u).astype(q_ref.dtype)
        h1_ref[...] = h_ref[...] + _dot(q_ref[...], wo_ref[...], _NN)
        rider(None, None, i == nt - 1, r_in, r_out, r_scratch)

    row_tile = lambda cols: pl.BlockSpec((tm, cols), lambda i: (i, 0))
    return pl.pallas_call(
        body, name=name, grid=(nt,),
        in_specs=[_ANY, _full(meta.shape), pl.BlockSpec((1, d), lambda i: (0, 0)),
                  _resident((d, 3 * d)), _full(conv_w.shape), _resident(w_out.shape)] + list(rider.in_specs),
        out_specs=[row_tile(d), row_tile(d), row_tile(3 * d), row_tile(d), row_tile(d)] + list(rider.out_specs),
        out_shape=[jax.ShapeDtypeStruct((t, d), F32), jax.ShapeDtypeStruct((t, d), MXU_DTYPE),
                   jax.ShapeDtypeStruct((t, 3 * d), F32), jax.ShapeDtypeStruct((t, d), MXU_DTYPE),
                   jax.ShapeDtypeStruct((t, d), F32)] + list(rider.out_shape),
        scratch_shapes=[pltpu.VMEM((nh * SUBLANES, d), F32)] + _time_scratch(tm, d) + list(rider.scratch_shapes),
        compiler_params=_params("arbitrary"),
    )(x, meta, g, w, conv_w, w_out, *rider.inputs)


def _sc_out_bwd_fused(dh, wot, p, conv_w, tm, name, rider=_NO_RIDER):
    t, d = dh.shape
    nt = t // tm
    cw = min(STRIP, d)
    kw = conv_w.shape[0]
    nh = kw - 1
    rev = lambda i: nt - 1 - i

    def body(dh_ref, wot_ref, p_ref, *rest):
        halo_refs, cw_ref = rest[:nh], rest[nh]
        r_in, (dp_ref, dhb_ref, dw_ref), r_out, (carry,), r_scratch = _split_refs(rider, 3, 1, rest[nh + 1:])
        i = pl.program_id(0)
        newest, oldest = i == 0, i == nt - 1
        rider(newest, i == nt // 2, None, r_in, r_out, r_scratch)

        @pl.when(newest)
        def _():
            carry[...] = jnp.zeros_like(carry)
            dw_ref[...] = jnp.zeros_like(dw_ref)

        dhb_ref[...] = dh_ref[...].astype(dhb_ref.dtype)
        for c in range(0, d, cw):
            sl, csl, vsl = slice(c, c + cw), slice(d + c, d + c + cw), slice(2 * d + c, 2 * d + c + cw)
            w = _rows(cw_ref, sl)
            dq = _dot(dhb_ref[...], wot_ref[:, sl], _NN)
            bg, cg, v = p_ref[:, sl], p_ref[:, csl], p_ref[:, vsl]
            before = [jnp.where(oldest, 0.0, r[:, csl] * r[:, vsl]) for r in halo_refs]
            win = _windows(cg * v, before, kw)
            du = dq * bg
            dcv = _conv_t(du, _blocks(carry, sl, nh, False), w)
            dp_ref[:, sl] = (dq * _weighted(w, win)).astype(dp_ref.dtype)
            dp_ref[:, csl] = (dcv * v).astype(dp_ref.dtype)
            dp_ref[:, vsl] = (dcv * cg).astype(dp_ref.dtype)
            for k in range(kw):
                dw_ref[k:k + 1, sl] += jnp.sum(du * win[k], axis=0, keepdims=True)
            carry[:, sl] = du[0:nh * SUBLANES]
        rider(None, None, oldest, r_in, r_out, r_scratch)

    tile = lambda cols: pl.BlockSpec((tm, cols), lambda i: (rev(i), 0))
    return pl.pallas_call(
        body, name=name, grid=(nt,),
        in_specs=[tile(d), _resident((d, d)), tile(3 * d)] + _halo_specs(tm, 3 * d, nh, rev) + [_full(conv_w.shape)]
                 + list(rider.in_specs),
        out_specs=[tile(3 * d), tile(d), _full(conv_w.shape)] + list(rider.out_specs),
        out_shape=[jax.ShapeDtypeStruct((t, 3 * d), MXU_DTYPE), jax.ShapeDtypeStruct((t, d), MXU_DTYPE),
                   jax.ShapeDtypeStruct(conv_w.shape, F32)] + list(rider.out_shape),
        scratch_shapes=[pltpu.VMEM((nh * SUBLANES, d), F32)] + list(rider.scratch_shapes),
        compiler_params=_params("arbitrary"),
    )(dh, wot, p, *([p] * nh), conv_w, *rider.inputs)


def _pair_gate_weights(wa, wx):
    nb, bd, _ = wa.shape
    zero = jnp.zeros((nb // 2, bd, bd), wa.dtype)

    def pair(w):
        w = w.reshape(nb // 2, 2, bd, bd)
        top = jnp.concatenate([w[:, 0], zero], axis=2)
        bottom = jnp.concatenate([zero, w[:, 1]], axis=2)
        return jnp.concatenate([top, bottom], axis=1)

    both = jnp.concatenate([pair(wa), pair(wx)], axis=2)
    return both, jnp.swapaxes(both, 1, 2)


def _unpair_gate_grads(dw, bd):
    def blocks(cols):
        d0 = dw[:, :bd, cols:cols + bd]
        d1 = dw[:, bd:, cols + bd:cols + 2 * bd]
        return jnp.stack([d0, d1], axis=1).reshape(-1, bd, bd)
    return blocks(0), blocks(2 * bd)


def _rg_gates(u, k, wg_ref, ba, bx, lam):
    ub = u.astype(MXU_DTYPE)
    z = _dot(ub, wg_ref[k], _NN)
    half = z.shape[1] // 2
    r = _sigmoid_tanh(z[:, :half] + ba)
    ig = _sigmoid_tanh(z[:, half:] + bx)
    sp = _softplus(-lam)
    la = -RG_C * r * sp
    a = jnp.exp(la)
    th = jnp.tanh(la)
    mult = jnp.sqrt(-2.0 * th / (1.0 - th))
    return r, ig, a, mult


def _rg_in_fused(h, g, w, conv_w, conv_b, wg, ba, bx, lam, w_out, tm, name, rider=_NO_RIDER):
    t, d = h.shape
    r2 = w.shape[1]
    rr = r2 // 2
    nb, bd = wg.shape[0], wg.shape[1]
    kw = conv_w.shape[0]
    nh = kw - 1
    groups = tm // SUBLANES
    nt = t // tm

    def body(h_ref, g_ref, w_ref, cw_ref, cb_ref, wg_ref, ba_ref, bx_ref, lam_ref, wo_ref, *rest):
        (r_in, (n_ref, p_ref, y_ref, hs_ref, u_ref, r_ref, ig_ref, a_ref, mult_ref, hn_ref), r_out,
         (a_scr, b_scr, carry_rb, carry_h), r_scratch) = _split_refs(rider, 10, 4, rest)
        i = pl.program_id(0)
        rider(i == 0, i == _hand_on_step(nt), None, r_in, r_out, r_scratch)

        @pl.when(i == 0)
        def _():
            carry_rb[...] = jnp.zeros_like(carry_rb)
            carry_h[...] = jnp.zeros_like(carry_h)

        _rmsnorm_to(h_ref, g_ref, n_ref)
        for k in range(nb):
            sl = slice(k * bd, (k + 1) * bd)
            rsl = slice(rr + k * bd, rr + (k + 1) * bd)
            p_ref[:, sl] = _dot(n_ref[...], w_ref[:, sl], _NN)
            rb = _dot(n_ref[...], w_ref[:, rsl], _NN)
            p_ref[:, rsl] = rb
            u = _weighted(_rows(cw_ref, sl), _windows(rb, _blocks(carry_rb, sl, nh, True), kw)) + cb_ref[:, sl]
            carry_rb[:, sl] = rb[tm - nh * SUBLANES:tm]
            r, ig, a, mult = _rg_gates(u, k, wg_ref, ba_ref[:, sl], bx_ref[:, sl], lam_ref[:, sl])
            for ref, val in ((u_ref, u), (r_ref, r), (ig_ref, ig), (a_ref, a), (mult_ref, mult)):
                ref[:, sl] = val
            a_scr[:, sl] = a
            b_scr[:, sl] = mult * (ig * u)

        carry_h[...] = _scan_tile(a_scr, b_scr, hs_ref, carry_h[...], groups, reverse=False)

        for k in range(nb):
            sl = slice(k * bd, (k + 1) * bd)
            y_ref[:, sl] = (hs_ref[:, sl] * _gelu(p_ref[:, sl])).astype(y_ref.dtype)
        hn_ref[...] = h_ref[...] + _dot(y_ref[...], wo_ref[...], _NN)
        rider(None, None, i == nt - 1, r_in, r_out, r_scratch)

    vm = lambda rows: pltpu.VMEM((rows, rr), F32)
    return pl.pallas_call(
        body, name=name, grid=(nt,),
        in_specs=[pl.BlockSpec((tm, d), lambda i: (i, 0)), pl.BlockSpec((1, d), lambda i: (0, 0)),
                  _resident((d, r2)), _full(conv_w.shape), _full(conv_b.shape), _full(wg.shape), _full(ba.shape),
                  _full(bx.shape), _full(lam.shape), _resident(w_out.shape)] + list(rider.in_specs),
        out_specs=[pl.BlockSpec((tm, d), lambda i: (i, 0)), pl.BlockSpec((tm, r2), lambda i: (i, 0))]
                  + [pl.BlockSpec((tm, rr), lambda i: (i, 0))] * 7 + [pl.BlockSpec((tm, d), lambda i: (i, 0))]
                  + list(rider.out_specs),
        out_shape=[jax.ShapeDtypeStruct((t, d), MXU_DTYPE), jax.ShapeDtypeStruct((t, r2), F32),
                   jax.ShapeDtypeStruct((t, rr), MXU_DTYPE)] + [jax.ShapeDtypeStruct((t, rr), F32)] * 6
                  + [jax.ShapeDtypeStruct((t, d), F32)] + list(rider.out_shape),
        scratch_shapes=[vm(tm), vm(tm), vm(nh * SUBLANES), vm(SUBLANES)] + list(rider.scratch_shapes),
        compiler_params=pltpu.CompilerParams(dimension_semantics=("arbitrary",), vmem_limit_bytes=VMEM_LIMIT_WIDE_BYTES),
    )(h, g, w, conv_w, conv_b, wg, ba, bx, lam, w_out, *rider.inputs)


def _rg_out_bwd_fused(dh, wot, p, hs, gates, conv_w, wg, wgt, lam, tm, name, rider=_NO_RIDER):
    t, d = dh.shape
    r2 = p.shape[1]
    rr = r2 // 2
    nb, bd = wg.shape[0], wg.shape[1]
    nt = t // tm
    kw = conv_w.shape[0]
    nh = kw - 1
    groups = tm // SUBLANES
    rev = lambda i: nt - 1 - i

    def body(dh_ref, wot_ref, p_ref, hs_ref, hsh_ref, u_ref, r_ref, ig_ref, a_ref, mult_ref, *rest):
        halo_refs = rest[:nh]
        cw_ref, wg_ref, wgt_ref, lam_ref = rest[nh:nh + 4]
        (r_in, (dp_ref, dhb_ref, dcw_ref, dcb_ref, dwg_ref, dba_ref, dbx_ref, dlam_ref), r_out,
         (as_scr, g_scr, carry_g, carry_du), r_scratch) = _split_refs(rider, 8, 4, rest[nh + 4:])
        i = pl.program_id(0)
        newest, oldest = i == 0, i == nt - 1
        rider(newest, i == nt // 2, None, r_in, r_out, r_scratch)

        @pl.when(newest)
        def _():
            carry_g[...] = jnp.zeros_like(carry_g)
            carry_du[...] = jnp.zeros_like(carry_du)
            for ref in (dcw_ref, dcb_ref, dwg_ref, dba_ref, dbx_ref, dlam_ref):
                ref[...] = jnp.zeros_like(ref)

        dhb_ref[...] = dh_ref[...].astype(dhb_ref.dtype)
        ones = jnp.ones((SUBLANES, bd), F32)

        def conv_windows(sl, rsl):
            before = [jnp.where(oldest, 0.0, r[:, rsl]) for r in halo_refs]
            return _windows(p_ref[:, rsl], before, kw)

        for k in range(nb):
            sl = slice(k * bd, (k + 1) * bd)
            rsl = slice(rr + k * bd, rr + (k + 1) * bd)
            dy = _dot(dhb_ref[...], wot_ref[:, sl], _NN)
            gate, dgate = _gelu_and_grad(p_ref[:, sl])
            g_scr[:, sl] = dy * gate
            dp_ref[:, sl] = (dy * hs_ref[:, sl] * dgate).astype(dp_ref.dtype)
            as_scr[:, sl] = _shift_up(a_ref[:, sl], [ones], 1)

        g_first = _scan_tile(as_scr, g_scr, g_scr, carry_g[...], groups, reverse=True)
        carry_g[...] = jnp.broadcast_to(a_ref[0:1, :], (SUBLANES, rr)) * g_first

        for k in range(nb):
            sl = slice(k * bd, (k + 1) * bd)
            rsl = slice(rr + k * bd, rr + (k + 1) * bd)
            cw = _rows(cw_ref, sl)
            lam_k = lam_ref[:, sl]
            sp = _softplus(-lam_k)
            g = g_scr[:, sl]
            a, u, r, ig, mult = a_ref[:, sl], u_ref[:, sl], r_ref[:, sl], ig_ref[:, sl], mult_ref[:, sl]
            da = g * _shift_down(hs_ref[:, sl], [jnp.where(oldest, 0.0, hsh_ref[:, sl])], 1)
            dmult = g * (ig * u)
            d_iu = g * mult
            dla = da * a - dmult * (a * a) / mult
            dr = dla * (-RG_C * sp)
            dsp = jnp.sum(dla * (-RG_C * r), axis=0, keepdims=True)
            dlam_ref[:, sl] += dsp * (-_sigmoid(-lam_k))
            dza = dr * r * (1.0 - r)
            dzx = (d_iu * u) * ig * (1.0 - ig)
            dba_ref[:, sl] += jnp.sum(dza, axis=0, keepdims=True)
            dbx_ref[:, sl] += jnp.sum(dzx, axis=0, keepdims=True)
            ub = u.astype(MXU_DTYPE)
            dz = jnp.concatenate([dza, dzx], axis=1).astype(MXU_DTYPE)
            dwg_ref[k] += _dot(ub, dz, _TN)
            du = d_iu * ig + _dot(dz, wgt_ref[k], _NN)
            dcb_ref[:, sl] += jnp.sum(du, axis=0, keepdims=True)
            win = conv_windows(sl, rsl)
            for kk in range(kw):
                dcw_ref[kk:kk + 1, sl] += jnp.sum(du * win[kk], axis=0, keepdims=True)
            dp_ref[:, rsl] = _conv_t(du, _blocks(carry_du, sl, nh, False), cw).astype(dp_ref.dtype)
            carry_du[:, sl] = du[0:nh * SUBLANES]
        rider(None, None, oldest, r_in, r_out, r_scratch)

    tile = lambda cols: pl.BlockSpec((tm, cols), lambda i: (rev(i), 0))
    vm = lambda rows: pltpu.VMEM((rows, rr), F32)
    grads = [conv_w.shape, lam.shape, wg.shape, lam.shape, lam.shape, lam.shape]
    return pl.pallas_call(
        body, name=name, grid=(nt,),
        in_specs=[tile(d), _resident((d, rr)), tile(r2), tile(rr)] + _halo_specs(tm, rr, 1, rev) + [tile(rr)] * 5
                 + _halo_specs(tm, r2, nh, rev)
                 + [_full(conv_w.shape), _full(wg.shape), _full(wgt.shape), _full(lam.shape)] + list(rider.in_specs),
        out_specs=[tile(r2), tile(d)] + [_full(s) for s in grads] + list(rider.out_specs),
        out_shape=[jax.ShapeDtypeStruct((t, r2), MXU_DTYPE), jax.ShapeDtypeStruct((t, d), MXU_DTYPE)]
                  + [jax.ShapeDtypeStruct(s, F32) for s in grads] + list(rider.out_shape),
        scratch_shapes=[vm(tm)] * 2 + [vm(SUBLANES), vm(nh * SUBLANES)] + list(rider.scratch_shapes),
        compiler_params=pltpu.CompilerParams(dimension_semantics=("arbitrary",), vmem_limit_bytes=VMEM_LIMIT_WIDE_BYTES),
    )(dh, wot, p, hs, hs, *gates, *([p] * nh), conv_w, wg, wgt, lam, *rider.inputs)


def _down_loss_head(z, w, h, g, target, n_meta, tm, name):
    t, d = h.shape
    k = z.shape[1]
    groups = tm // SUBLANES
    nt = t // tm

    def body(z_ref, w_ref, h_ref, g_ref, t_ref, dh_ref, loss_ref, dg_ref, tgt, tbuf, slabs, sems):
        i = pl.program_id(0)
        _fetch_time_tile(i, nt, t_ref, jnp.zeros((n_meta, d), F32), tbuf, sems, tm)
        _time_to_tile_order(tbuf.at[i % 2], slabs, tgt)
        x = h_ref[...] + _dot(z_ref[...], w_ref[...], _NN)
        ms = jnp.mean(x * x, axis=-1, keepdims=True)
        r = lax.rsqrt(ms + RMS_EPS)
        xhat = x * r
        gg = g_ref[...]
        row = lax.broadcasted_iota(jnp.int32, (tm, 1), 0)
        time = i * tm + jnp.right_shift(row, 3) + jnp.bitwise_and(row, SUBLANES - 1) * groups
        err = jnp.where(time >= n_meta, xhat * gg - tgt[...], 0.0)
        dout = err * (1.0 / d)
        dng = dout * gg
        c = jnp.mean(dng * xhat, axis=-1, keepdims=True)
        dh_ref[...] = r * (dng - xhat * c)
        part_dg = jnp.sum(dout * xhat, axis=0, keepdims=True)
        part_loss = jnp.broadcast_to(0.5 * jnp.sum(err * dout, keepdims=True), loss_ref.shape)

        @pl.when(i == 0)
        def _():
            dg_ref[...] = part_dg
            loss_ref[...] = part_loss

        @pl.when(i > 0)
        def _():
            dg_ref[...] += part_dg
            loss_ref[...] += part_loss

    return pl.pallas_call(
        body, name=name, grid=(nt,),
        in_specs=[pl.BlockSpec((tm, k), lambda i: (i, 0)),
                  _resident((k, d)),
                  pl.BlockSpec((tm, d), lambda i: (i, 0)),
                  pl.BlockSpec((1, d), lambda i: (0, 0)),
                  _ANY],
        out_specs=[pl.BlockSpec((tm, d), lambda i: (i, 0)),
                   pl.BlockSpec((SUBLANES, LANES), lambda i: (0, 0)),
                   pl.BlockSpec((1, d), lambda i: (0, 0))],
        out_shape=[jax.ShapeDtypeStruct((t, d), F32), jax.ShapeDtypeStruct((SUBLANES, LANES), F32),
                   jax.ShapeDtypeStruct((1, d), F32)],
        scratch_shapes=[pltpu.VMEM((tm, d), F32)] + _time_scratch(tm, d),
        compiler_params=_params("arbitrary"),
    )(z, w, h, g, target)


def _adamw(w, g, m, v, name):
    rows, cols = w.shape
    tr = rows
    if rows > 512:
        for cand in range(8, 513, 8):
            if rows % cand == 0:
                tr = cand

    def body(w_ref, g_ref, m_ref, v_ref, d_ref, nm_ref, nv_ref):
        g_ = g_ref[...]
        m_ = ADAM_B1 * m_ref[...] + (1.0 - ADAM_B1) * g_
        v_ = ADAM_B2 * v_ref[...] + (1.0 - ADAM_B2) * (g_ * g_)
        m_hat = m_ / (1.0 - ADAM_B1 ** ADAM_STEP)
        v_hat = v_ / (1.0 - ADAM_B2 ** ADAM_STEP)
        d_ref[...] = -ADAM_LR * (m_hat / (jnp.sqrt(v_hat) + ADAM_EPS) + ADAM_WD * w_ref[...])
        nm_ref[...] = m_
        nv_ref[...] = v_

    spec = pl.BlockSpec((tr, cols), lambda i: (i, 0))
    shape = jax.ShapeDtypeStruct((rows, cols), F32)
    return pl.pallas_call(
        body, name=name, grid=(rows // tr,),
        in_specs=[spec] * 4, out_specs=[spec] * 3, out_shape=[shape] * 3,
        compiler_params=_params("parallel"),
    )(w, g, m, v)


def _adamw_nd(w, g, m, v, name):
    shape = w.shape
    two_d = (-1, shape[-1]) if w.ndim > 1 else (1, -1)
    outs = _adamw(w.reshape(two_d), g.reshape(two_d), m.reshape(two_d), v.reshape(two_d), name)
    return tuple(o.reshape(shape) for o in outs)


def _ride_alone(rider, name):
    def body(*refs):
        r_in, _, r_out, _, r_scratch = _split_refs(rider, 0, 0, refs)
        now = pl.program_id(0) == 0
        rider(now, None, None, r_in, r_out, r_scratch)
        rider(None, now, None, r_in, r_out, r_scratch)
        rider(None, None, now, r_in, r_out, r_scratch)

    return pl.pallas_call(
        body, name=name, grid=(1,),
        in_specs=list(rider.in_specs), out_specs=list(rider.out_specs), out_shape=list(rider.out_shape),
        scratch_shapes=list(rider.scratch_shapes),
        compiler_params=_params("arbitrary"),
    )(*rider.inputs)


def _sum_slots(parts, name):
    slots, rows, cols = parts.shape
    tr = _row_tile(rows, 256) if rows % 16 == 0 else rows

    def body(p_ref, o_ref):
        acc = p_ref[0]
        for s in range(1, slots):
            acc = acc + p_ref[s]
        o_ref[...] = acc

    return pl.pallas_call(
        body, name=name, grid=(rows // tr,),
        in_specs=[pl.BlockSpec((slots, tr, cols), lambda i: (0, i, 0))],
        out_specs=pl.BlockSpec((tr, cols), lambda i: (i, 0)),
        out_shape=jax.ShapeDtypeStruct((rows, cols), parts.dtype),
        compiler_params=_params("parallel"),
    )(parts)


def _pad_rows(flat, cols, multiple):
    n = flat.shape[0]
    rows = -(-n // cols)
    rows = -(-rows // multiple) * multiple
    return jnp.pad(flat, (0, rows * cols - n)).reshape(rows, cols)


def _cols_to_chunks(full):
    lead = full.shape[:-1]
    c = full.shape[-1] // N_DEV
    x = full.reshape(-1, N_DEV, c)
    return jnp.transpose(x, (1, 0, 2)).reshape(N_DEV, -1)


def _chunks_to_cols(chunks, lead):
    n = 1
    for s in lead:
        n *= s
    c = chunks.shape[1] // n
    x = chunks.reshape(N_DEV, n, c)
    return jnp.transpose(x, (1, 0, 2)).reshape(tuple(lead) + (N_DEV * c,))


def kernel(x, meta_tokens, norm_mix_g, norm_ffn_g, final_norm_g, sc_w_in, sc_conv_w, sc_w_out, rg_w_in, rg_conv_w, rg_conv_b, rg_w_gate_a, rg_b_gate_a, rg_w_gate_x, rg_b_gate_x, rg_lambda, rg_w_out, ffn_w_up, ffn_conv_w, ffn_w_down, loss_target, m_meta_tokens, m_norm_mix_g, m_norm_ffn_g, m_final_norm_g, m_sc_w_in, m_sc_conv_w, m_sc_w_out, m_rg_w_in, m_rg_conv_w, m_rg_conv_b, m_rg_w_gate_a, m_rg_b_gate_a, m_rg_w_gate_x, m_rg_b_gate_x, m_rg_lambda, m_rg_w_out, m_ffn_w_up, m_ffn_conv_w, m_ffn_w_down, v_meta_tokens, v_norm_mix_g, v_norm_ffn_g, v_final_norm_g, v_sc_w_in, v_sc_conv_w, v_sc_w_out, v_rg_w_in, v_rg_conv_w, v_rg_conv_b, v_rg_w_gate_a, v_rg_b_gate_a, v_rg_w_gate_x, v_rg_b_gate_x, v_rg_lambda, v_rg_w_out, v_ffn_w_up, v_ffn_conv_w, v_ffn_w_down):
    weights = dict(meta_tokens=meta_tokens, norm_mix_g=norm_mix_g, norm_ffn_g=norm_ffn_g, final_norm_g=final_norm_g,
                   sc_w_in=sc_w_in, sc_conv_w=sc_conv_w, sc_w_out=sc_w_out, rg_w_in=rg_w_in, rg_conv_w=rg_conv_w,
                   rg_conv_b=rg_conv_b, rg_w_gate_a=rg_w_gate_a, rg_b_gate_a=rg_b_gate_a, rg_w_gate_x=rg_w_gate_x,
                   rg_b_gate_x=rg_b_gate_x, rg_lambda=rg_lambda, rg_w_out=rg_w_out, ffn_w_up=ffn_w_up,
                   ffn_conv_w=ffn_conv_w, ffn_w_down=ffn_w_down)
    m_in = dict(meta_tokens=m_meta_tokens, norm_mix_g=m_norm_mix_g, norm_ffn_g=m_norm_ffn_g, final_norm_g=m_final_norm_g,
                sc_w_in=m_sc_w_in, sc_conv_w=m_sc_conv_w, sc_w_out=m_sc_w_out, rg_w_in=m_rg_w_in, rg_conv_w=m_rg_conv_w,
                rg_conv_b=m_rg_conv_b, rg_w_gate_a=m_rg_w_gate_a, rg_b_gate_a=m_rg_b_gate_a, rg_w_gate_x=m_rg_w_gate_x,
                rg_b_gate_x=m_rg_b_gate_x, rg_lambda=m_rg_lambda, rg_w_out=m_rg_w_out, ffn_w_up=m_ffn_w_up,
                ffn_conv_w=m_ffn_conv_w, ffn_w_down=m_ffn_w_down)
    v_in = dict(meta_tokens=v_meta_tokens, norm_mix_g=v_norm_mix_g, norm_ffn_g=v_norm_ffn_g, final_norm_g=v_final_norm_g,
                sc_w_in=v_sc_w_in, sc_conv_w=v_sc_conv_w, sc_w_out=v_sc_w_out, rg_w_in=v_rg_w_in, rg_conv_w=v_rg_conv_w,
                rg_conv_b=v_rg_conv_b, rg_w_gate_a=v_rg_w_gate_a, rg_b_gate_a=v_rg_b_gate_a, rg_w_gate_x=v_rg_w_gate_x,
                rg_b_gate_x=v_rg_b_gate_x, rg_lambda=v_rg_lambda, rg_w_out=v_rg_w_out, ffn_w_up=v_ffn_w_up,
                ffn_conv_w=v_ffn_conv_w, ffn_w_down=v_ffn_w_down)
    names = list(weights)

    seq, d = x.shape[1], x.shape[2]
    n_meta = meta_tokens.shape[0]
    n_ffn = ffn_w_up.shape[0]

    assert n_ffn == 2
    def shard_rows(w_in, w_out):
        return [w_in.T.astype(MXU_DTYPE), w_out.astype(MXU_DTYPE)]

    def both_orientations(in_t_gathered, out_gathered):
        w_in_t = in_t_gathered.reshape(-1, d)
        w_out = out_gathered.reshape(-1, d)
        return w_in_t.T, w_in_t, w_out, w_out.T

    shards = {"ffn0": shard_rows(ffn_w_up[0], ffn_w_down[0]), "rg": shard_rows(rg_w_in[0], rg_w_out[0]),
              "ffn1": shard_rows(ffn_w_up[1], ffn_w_down[1])}
    ffn_w_up_f, ffn_w_up_t, ffn_w_down_f, ffn_w_down_t = [None] * 2, [None] * 2, [None] * 2, [None] * 2

    small_names = ["meta_tokens", "sc_conv_w", "rg_conv_w", "rg_conv_b", "rg_b_gate_a", "rg_b_gate_x", "rg_lambda",
                   "ffn_conv_w"]
    small_lead = {n: weights[n].shape[:-1] for n in small_names}
    small_sizes = [weights[n].size for n in small_names]
    small_flat = jnp.concatenate([weights[n].reshape(-1) for n in small_names])
    small_rows = _pad_rows(small_flat, d, SUBLANES)
    sc_in_g, sc_out_g, small_g = _ride_alone(
        _GatherRider(shard_rows(sc_w_in[0], sc_w_out[0]) + [small_rows]), "gather_first")
    sc_w_in_f, sc_w_in_t, sc_w_out_f, sc_w_out_t = both_orientations(sc_in_g, sc_out_g)
    small_g = small_g.reshape(N_DEV, -1)
    small_full = {}
    o = 0
    for n, sz in zip(small_names, small_sizes):
        small_full[n] = _chunks_to_cols(small_g[:, o:o + sz], small_lead[n])
        o += sz

    wg, wgt = _pair_gate_weights(rg_w_gate_a[0].astype(MXU_DTYPE), rg_w_gate_x[0].astype(MXU_DTYPE))
    rg_cw, rg_cb = small_full["rg_conv_w"][0], small_full["rg_conv_b"]
    rg_ba, rg_bx, rg_lam = small_full["rg_b_gate_a"], small_full["rg_b_gate_x"], small_full["rg_lambda"]
    sc_cw = small_full["sc_conv_w"][0]
    ffn_cw = small_full["ffn_conv_w"]

    tp = _row_tile(n_meta + seq, ROW_TILE_PERM)

    def ffn_fwd(h, l, rider):
        n, a, z, *gathered = _ffn_up_fused(h, norm_ffn_g[l:l + 1], ffn_w_up_f[l], ffn_cw[l], tp, f"ffn{l}_up", rider)
        return _matmul_residual(z, ffn_w_down_f[l], h, f"ffn{l}_down"), (n, a, z), gathered

    h0, n0, p0, q0, h1, *ffn0_w = _sc_in_fused(x[0], small_full["meta_tokens"], norm_mix_g[0:1], sc_w_in_f, sc_cw,
                                               sc_w_out_f, tp, "sc_mixer", _GatherRider(shards["ffn0"]))
    ffn_w_up_f[0], ffn_w_up_t[0], ffn_w_down_f[0], ffn_w_down_t[0] = both_orientations(*ffn0_w)
    h2, ffn0_saved, rg_w = ffn_fwd(h1, 0, _GatherRider(shards["rg"]))
    rg_w_in_f, rg_w_in_t, rg_w_out_f, rg_w_out_t = both_orientations(*rg_w)
    n2, p2, y2, hs2, *rest2 = _rg_in_fused(h2, norm_mix_g[1:2], rg_w_in_f, rg_cw, rg_cb, wg, rg_ba, rg_bx, rg_lam,
                                           rg_w_out_f, tp, "rg_mixer", _GatherRider(shards["ffn1"]))
    rg_gates, h3, ffn1_w = rest2[:5], rest2[5], rest2[6:]
    ffn_w_up_f[1], ffn_w_up_t[1], ffn_w_down_f[1], ffn_w_down_t[1] = both_orientations(*ffn1_w)
    ffn1_saved = tuple(_ffn_up_fused(h3, norm_ffn_g[1:2], ffn_w_up_f[1], ffn_cw[1], tp, "ffn1_up"))
    dh4, loss_tile, d_final_g = _down_loss_head(ffn1_saved[2], ffn_w_down_f[1], h3, final_norm_g.reshape(1, d),
                                                loss_target[0], n_meta, tp, "ffn1_down_loss_head")
    loss = lax.psum(loss_tile[0, 0], AXES)

    arrived = {}

    def ffn_bwd(dh_out, h_in, saved, l, carried=None):
        n, a, z = saved
        da, dhb, d_cw = _ffn_down_bwd_fused(dh_out, ffn_w_down_t[l], a, ffn_cw[l], tp, f"ffn{l}_down_bwd")
        if carried is None:
            d_w_down = _wgrad(z, dhb, f"ffn{l}_down_wgrad")
        else:
            d_w_down, arrived[carried[0]] = _wgrad(z, dhb, f"ffn{l}_down_wgrad", carried[1])
        d_w_up_t, arrived[f"ffn_w_down{l}"] = _wgrad(da, n, f"ffn{l}_up_wgrad", _ScatterRider(d_w_down))
        rows = d_w_up_t.shape[0] // N_DEV
        first_rows = (rows // 2) // (2 * SUBLANES) * (2 * SUBLANES)
        dh_in, d_g, arrived[f"ffn_w_up{l}a"] = _dgrad_in_norm(
            da, ffn_w_up_t[l], h_in, norm_ffn_g[l:l + 1], dh_out, f"ffn{l}_up_dgrad",
            _ScatterRider(d_w_up_t, (0, first_rows)))
        return dh_in, d_cw, d_g, _ScatterRider(d_w_up_t, (first_rows, rows - first_rows))

    dh3, d_fcw1, d_fg1, up1_rest = ffn_bwd(dh4, h3, ffn1_saved, 1)

    dp2, dhb3, d_rg_cw, d_rg_cb, d_wg, d_rg_ba, d_rg_bx, d_rg_lam, arrived["ffn_w_up1b"] = _rg_out_bwd_fused(
        dh3, rg_w_out_t, p2, hs2, rg_gates, rg_cw, wg, wgt, rg_lam, tp, "rg_out_bwd", up1_rest)
    d_wa, d_wx = _unpair_gate_grads(d_wg, rg_w_gate_a.shape[2])
    d_rg_w_out = _wgrad(y2, dhb3, "rg_out_wgrad")
    d_rg_w_in_t, arrived["rg_w_out"] = _wgrad(dp2, n2, "rg_in_wgrad", _ScatterRider(d_rg_w_out))
    rg_rows = d_rg_w_in_t.shape[0] // N_DEV
    rg_first = (rg_rows // 2) // (2 * SUBLANES) * (2 * SUBLANES)
    dh2, d_mg1, arrived["rg_w_in_a"] = _dgrad_in_norm(
        dp2, rg_w_in_t, h2, norm_mix_g[1:2], dh3, "rg_in_dgrad", _ScatterRider(d_rg_w_in_t, (0, rg_first)))

    dh1, d_fcw0, d_fg0, up0_rest = ffn_bwd(
        dh2, h1, ffn0_saved, 0, ("rg_w_in_b", _ScatterRider(d_rg_w_in_t, (rg_first, rg_rows - rg_first))))

    dp0, dhb1, d_sc_cw, arrived["ffn_w_up0b"] = _sc_out_bwd_fused(dh1, sc_w_out_t, p0, sc_cw, tp, "sc_out_bwd", up0_rest)
    d_sc_w_out = _wgrad(q0, dhb1, "sc_out_wgrad")
    d_sc_w_in_t, arrived["sc_w_out"] = _wgrad(dp0, n0, "sc_in_wgrad", _ScatterRider(d_sc_w_out))
    d_mg0, d_meta, d_x, arrived["sc_w_in"] = _dgrad_in_to_input(
        dp0, sc_w_in_t, h0, norm_mix_g[0:1], dh1, n_meta, tp, "sc_in_dgrad", _ScatterRider(d_sc_w_in_t))
    grad_x = d_x[None]

    grads = {}
    grads["sc_w_in"] = _sum_slots(arrived["sc_w_in"], "sum_sc_w_in").T[None]
    grads["rg_w_in"] = jnp.concatenate(
        [_sum_slots(arrived[f"rg_w_in_{part}"], f"sum_rg_w_in_{part}") for part in "ab"], axis=0).T[None]
    for n in ("sc_w_out", "rg_w_out"):
        grads[n] = _sum_slots(arrived[n], f"sum_{n}")[None]
    grads["ffn_w_up"] = jnp.stack([jnp.concatenate(
        [_sum_slots(arrived[f"ffn_w_up{l}{part}"], f"sum_ffn_w_up{l}{part}") for part in "ab"], axis=0).T
        for l in range(n_ffn)])
    grads["ffn_w_down"] = jnp.stack([_sum_slots(arrived[f"ffn_w_down{l}"], f"sum_ffn_w_down{l}") for l in range(n_ffn)])

    small_grads = {"meta_tokens": d_meta,"sc_conv_w": d_sc_cw[None], "rg_conv_w": d_rg_cw[None],
                   "rg_conv_b": d_rg_cb, "rg_b_gate_a": d_rg_ba, "rg_b_gate_x": d_rg_bx, "rg_lambda": d_rg_lam,
                   "ffn_conv_w": jnp.stack([d_fcw0, d_fcw1])}
    small_chunks = jnp.concatenate([_cols_to_chunks(small_grads[n]) for n in small_names], axis=1)
    pad = small_rows.shape[0] * d - small_chunks.shape[1]
    small_chunks = jnp.pad(small_chunks, ((0, 0), (0, pad))).reshape(N_DEV, small_rows.shape[0], d)
    rep_names = ["norm_mix_g", "norm_ffn_g", "final_norm_g", "rg_w_gate_a", "rg_w_gate_x"]
    rep_grads = {"norm_mix_g": jnp.concatenate([d_mg0, d_mg1], axis=0),
                 "norm_ffn_g": jnp.concatenate([d_fg0, d_fg1], axis=0),
                 "final_norm_g": d_final_g.reshape(-1), "rg_w_gate_a": d_wa[None], "rg_w_gate_x": d_wx[None]}
    rep_flat = jnp.concatenate([rep_grads[n].reshape(-1) for n in rep_names])
    rep_chunk_rows = -(-rep_flat.shape[0] // (N_DEV * d))
    rep_chunk_rows += -(small_rows.shape[0] + rep_chunk_rows) % 16
    rep_chunks = jnp.pad(rep_flat, (0, N_DEV * rep_chunk_rows * d - rep_flat.shape[0])).reshape(N_DEV, rep_chunk_rows, d)

    last_chunks = jnp.concatenate([small_chunks, rep_chunks], axis=1)
    reduced = _sum_slots(_ride_alone(_ScatterRider(last_chunks.reshape(-1, d)), "scatter_last")[0], "sum_last")

    small_red = reduced[0:small_rows.shape[0]].reshape(-1)
    o = small_rows.shape[0]
    so = 0
    for n, sz in zip(small_names, small_sizes):
        grads[n] = small_red[so:so + sz].reshape(weights[n].shape)
        so += sz
    rep_red = _ride_alone(_GatherRider([reduced[o:o + rep_chunks.shape[1]]]), "gather_replicated_grads")[0].reshape(-1)
    ro = 0
    for n in rep_names:
        sz = weights[n].size
        grads[n] = rep_red[ro:ro + sz].reshape(weights[n].shape)
        ro += sz

    delta, new_m, new_v = {}, {}, {}
    for n in names:
        delta[n], new_m[n], new_v[n] = _adamw_nd(weights[n], grads[n], m_in[n], v_in[n], f"adamw_{n}")

    return (loss, grad_x, *[grads[n] for n in names], *[delta[n] for n in names],
            *[new_m[n] for n in names], *[new_v[n] for n in names])
```

```python
import jax
import jax.numpy as jnp
from jax import lax
from jax.experimental import pallas as pl
from jax.experimental.pallas import tpu as pltpu

F32 = jnp.float32
MXU_DTYPE = jnp.bfloat16
RMS_EPS = 1e-6
RG_C = 8.0
ADAM_LR = 0.001
ADAM_B1 = 0.9
ADAM_B2 = 0.999
ADAM_EPS = 1e-08
ADAM_WD = 0.01
ADAM_STEP = 10

N_DEV = 8
AXES = ("x", "y", "c")
SUBLANES = 8
LANES = 128
VMEM_LIMIT_BYTES = 48 * 1024 * 1024
ROW_TILE_MATMUL = 700
VMEM_LIMIT_WIDE_BYTES = 58 * 1024 * 1024
ROW_TILE_WGRAD = 3300
COL_TILE_WGRAD = 1408
ROW_TILE_PERM = 400
STRIP = 256
STRIP_FFN = 4096

_TN = (((0,), (0,)), ((), ()))
_NN = (((1,), (0,)), ((), ()))


def _row_tile(t, target):
    best = None
    for tm in range(16, t + 1, 16):
        if t % tm == 0 and tm <= target:
            best = tm
    return best if best is not None else t


def _col_tile(n, target):
    best = None
    for tn in range(LANES, n + 1, LANES):
        if n % tn == 0 and tn <= target:
            best = tn
    return best if best is not None else n


def _params(*sem):
    return pltpu.CompilerParams(dimension_semantics=sem, vmem_limit_bytes=VMEM_LIMIT_BYTES)


def _dot(a, b, dims):
    return lax.dot_general(a, b, dims, preferred_element_type=F32)


def _sigmoid(x):
    return 1.0 / (1.0 + jnp.exp(-x))


def _sigmoid_tanh(x):
    return 0.5 * jnp.tanh(0.5 * x) + 0.5


def _gelu(x):
    c = 0.7978845608028654
    t = jnp.tanh(c * (x + 0.044715 * (x * x * x)))
    return 0.5 * x * (1.0 + t)


def _gelu_and_grad(x):
    c = 0.7978845608028654
    x2 = x * x
    t = jnp.tanh(c * (x + 0.044715 * (x2 * x)))
    half = 0.5 * (1.0 + t)
    return x * half, half + 0.5 * x * (1.0 - t * t) * c * (1.0 + 3.0 * 0.044715 * x2)


def _softplus(x):
    return jnp.maximum(x, 0.0) + jnp.log1p(jnp.exp(-jnp.abs(x)))


def _time_scratch(tm, d):
    return [pltpu.VMEM((2, tm, d), F32), pltpu.VMEM((d // LANES, tm, LANES), F32), pltpu.SemaphoreType.DMA((2,))]


def _fetch_time_tile(i, nt, src_ref, head, tbuf, sems, tm):
    n_head = head.shape[0]

    def tile(j, slot):
        start = pl.multiple_of(j * tm - n_head, SUBLANES)
        return pltpu.make_async_copy(src_ref.at[pl.ds(start, tm)], tbuf.at[slot], sems.at[slot])

    first = pltpu.make_async_copy(src_ref.at[pl.ds(0, tm - n_head)], tbuf.at[0, pl.ds(n_head, tm - n_head)], sems.at[0])

    @pl.when(i == 0)
    def _():
        tbuf[0, 0:n_head, :] = head
        first.start()

    @pl.when(i + 1 < nt)
    def _():
        tile(i + 1, (i + 1) % 2).start()

    @pl.when(i == 0)
    def _():
        first.wait()

    @pl.when(i > 0)
    def _():
        tile(i, i % 2).wait()


def _time_to_tile_order(t_ref, slabs, out_ref):
    tm, d = out_ref.shape
    groups = tm // SUBLANES
    for k in range(d // LANES):
        slabs[k] = t_ref[:, k * LANES:(k + 1) * LANES]
    for k in range(d // LANES):
        for g in range(groups):
            out_ref[g * SUBLANES:(g + 1) * SUBLANES, k * LANES:(k + 1) * LANES] = slabs[k, pl.ds(g, SUBLANES, stride=groups), :]


def _tile_to_time_order(p_ref, slabs, t_ref):
    tm, d = p_ref.shape
    groups = tm // SUBLANES
    for k in range(d // LANES):
        for g in range(groups):
            slabs[k, pl.ds(g, SUBLANES, stride=groups), :] = p_ref[g * SUBLANES:(g + 1) * SUBLANES, k * LANES:(k + 1) * LANES]
    for k in range(d // LANES):
        t_ref[:, k * LANES:(k + 1) * LANES] = slabs[k]


def _rows(ref, sl):
    return [ref[k:k + 1, sl] for k in range(ref.shape[0])]


def _shift_down(x, before, s):
    if s == 0:
        return x
    n = x.shape[0]
    row = lax.broadcasted_iota(jnp.int32, (SUBLANES, x.shape[1]), 0)
    heads = []
    for g in range(s):
        v = x[n - (s - g) * SUBLANES:n - (s - g - 1) * SUBLANES]
        heads.append(pltpu.roll(jnp.where(row == SUBLANES - 1, before[s - g - 1], v), 1, axis=0))
    return jnp.concatenate(heads + [x[0:n - s * SUBLANES]], axis=0)


def _shift_up(x, after, s):
    if s == 0:
        return x
    row = lax.broadcasted_iota(jnp.int32, (SUBLANES, x.shape[1]), 0)
    tails = []
    for m in range(s):
        v = x[m * SUBLANES:(m + 1) * SUBLANES]
        tails.append(pltpu.roll(jnp.where(row == 0, after[m], v), SUBLANES - 1, axis=0))
    return jnp.concatenate([x[s * SUBLANES:]] + tails, axis=0)


def _weighted(w, windows):
    y = w[0] * windows[0]
    for k in range(1, len(w)):
        y = y + w[k] * windows[k]
    return y


def _windows(x, before, k_width):
    return [_shift_down(x, before, k_width - 1 - k) for k in range(k_width)]


def _conv_t(dy, after, w):
    k_width = len(w)
    return _weighted(w, [_shift_up(dy, after, k_width - 1 - k) for k in range(k_width)])


def _blocks(ref, sl, count, newest_first):
    n = ref.shape[0] // SUBLANES
    order = range(n - 1, n - 1 - count, -1) if newest_first else range(count)
    return [ref[b * SUBLANES:(b + 1) * SUBLANES, sl] for b in order]


def _halo_specs(tm, cols, count, tile_of):
    def spec(k):
        return pl.BlockSpec((SUBLANES, cols), lambda i: (jnp.maximum(tile_of(i) * (tm // SUBLANES) - k, 0), 0))
    return [spec(k) for k in range(1, count + 1)]


def _scan_tile(coef, val, out, carry, groups, reverse):
    cols = coef.shape[1]
    row = lax.broadcasted_iota(jnp.int32, (SUBLANES, cols), 0)

    def blk(i):
        g = groups - 1 - i if reverse else i
        return pl.ds(pl.multiple_of(g * SUBLANES, SUBLANES), SUBLANES)

    def local(i, pl_):
        p_prev, l_prev = pl_
        a = coef[blk(i), :]
        p = a * p_prev
        l = a * l_prev + val[blk(i), :]
        coef[blk(i), :] = p
        val[blk(i), :] = l
        return p, l

    pf, lf = lax.fori_loop(0, groups, local, (jnp.ones((SUBLANES, cols), F32), jnp.zeros((SUBLANES, cols), F32)))
    for s in (1, 2, 4):
        keep, sh = (row < SUBLANES - s, SUBLANES - s) if reverse else (row >= s, s)
        p_s = jnp.where(keep, pltpu.roll(pf, sh, axis=0), 1.0)
        l_s = jnp.where(keep, pltpu.roll(lf, sh, axis=0), 0.0)
        lf = pf * l_s + lf
        pf = pf * p_s
    end = lf + pf * carry
    if reverse:
        init = jnp.where(row == SUBLANES - 1, carry, pltpu.roll(end, SUBLANES - 1, axis=0))
        leaving = jnp.broadcast_to(end[0:1, :], (SUBLANES, cols))
    else:
        init = jnp.where(row == 0, carry, pltpu.roll(end, 1, axis=0))
        leaving = jnp.broadcast_to(end[SUBLANES - 1:SUBLANES, :], (SUBLANES, cols))

    def fix(i, _):
        out[blk(i), :] = val[blk(i), :] + coef[blk(i), :] * init
        return 0

    lax.fori_loop(0, groups, fix, 0)
    return leaving


def _resident(shape):
    return pl.BlockSpec(shape, lambda *_: (0,) * len(shape), pipeline_mode=pl.Buffered(1))


def _full(shape):
    return pl.BlockSpec(shape, lambda *_: (0,) * len(shape))


def _rmsnorm_to(h_ref, g_ref, n_ref):
    x = h_ref[...]
    ms = jnp.mean(x * x, axis=-1, keepdims=True)
    n_ref[...] = (x * lax.rsqrt(ms + RMS_EPS) * g_ref[...]).astype(n_ref.dtype)


_ANY = pl.BlockSpec(memory_space=pl.ANY)
_MESH = pl.DeviceIdType.MESH


def _dma_sems(n):
    return pltpu.SemaphoreType.DMA((n,))


def _when_each(*phases):
    for cond, fn in phases:
        if cond is not None:
            pl.when(cond)(fn)


class _NoRider:
    inputs = in_specs = out_shape = out_specs = scratch_shapes = ()

    def __call__(self, first, middle, last, ins, outs, scratch):
        pass


_NO_RIDER = _NoRider()


def _hand_on_step(nt):
    return (3 * nt) // 4


def _split_refs(rider, n_out, n_scratch, rest):
    a = len(rider.inputs)
    b = a + n_out
    c = b + len(rider.out_shape)
    e = c + n_scratch
    return rest[:a], rest[a:b], rest[b:c], rest[c:e], rest[e:]


class _GatherRider:
    def __init__(self, blocks):
        n = len(blocks)
        self.inputs = tuple(blocks)
        self.in_specs = (_ANY,) * n
        self.out_shape = tuple(jax.ShapeDtypeStruct((N_DEV,) + b.shape, b.dtype) for b in blocks)
        self.out_specs = (_ANY,) * n
        self.scratch_shapes = (_dma_sems(7 * n), _dma_sems(7 * n), _dma_sems(n))

    def __call__(self, first, middle, last, ins, outs, scratch):
        n = len(ins)
        send_sems, recv_sems, local_sems = scratch
        x, y, c = lax.axis_index("x"), lax.axis_index("y"), lax.axis_index("c")
        me, sibling = (x, y, c), (x, y, 1 - c)
        chips = [(1 - x, y), (x, 1 - y), (1 - x, 1 - y)]

        def slot(b, px, py, pc):
            return outs[b].at[4 * px + 2 * py + pc]

        def copy(k, b, block, to, src=None):
            return pltpu.make_async_remote_copy(
                src_ref=slot(b, *block) if src is None else src, dst_ref=slot(b, *block),
                send_sem=send_sems.at[k * n + b], recv_sem=recv_sems.at[k * n + b], device_id=to, device_id_type=_MESH)

        mine = [pltpu.make_async_copy(ins[b], slot(b, *me), local_sems.at[b]) for b in range(n)]
        own = [copy(0, b, me, sibling, src=ins[b]) for b in range(n)]
        own += [copy(1 + j, b, me, (*chip, c), src=ins[b]) for j, chip in enumerate(chips) for b in range(n)]
        passed = [[copy(4 + j, b, (*chip, c), sibling) for b in range(n)] for j, chip in enumerate(chips)]

        def at_first():
            for cp in mine + own:
                cp.start()

        def at_middle():
            for j, chip in enumerate(chips):
                for b in range(n):
                    copy(1 + j, b, (*chip, c), me).wait_recv()
                    passed[j][b].start()

        def at_last():
            for b in range(n):
                copy(0, b, sibling, me).wait_recv()
                for j, chip in enumerate(chips):
                    copy(4 + j, b, (*chip, 1 - c), me).wait_recv()
            for cp in own + [cp for group in passed for cp in group]:
                cp.wait_send()
            for cp in mine:
                cp.wait()

        _when_each((first, at_first), (middle, at_middle), (last, at_last))


class _ScatterRider:
    def __init__(self, grad, rows=None):
        r = grad.shape[0] // N_DEV
        chunks = grad.reshape(N_DEV, r, grad.shape[1])
        self.rows = rows if rows is not None else (0, r)
        self.inputs = (chunks,)
        self.in_specs = (_ANY,)
        self.out_shape = (jax.ShapeDtypeStruct((N_DEV, self.rows[1], grad.shape[1]), chunks.dtype),)
        self.out_specs = (_ANY,)
        self.scratch_shapes = (_dma_sems(N_DEV - 1), _dma_sems(N_DEV - 1), pltpu.SemaphoreType.DMA(()))

    def __call__(self, first, middle, last, ins, outs, scratch):
        (g_ref,), (r_ref,) = ins, outs
        send_sems, recv_sems, local_sem = scratch
        x, y, c = lax.axis_index("x"), lax.axis_index("y"), lax.axis_index("c")
        me = 4 * x + 2 * y + c
        part = pl.ds(*self.rows)
        mine = pltpu.make_async_copy(g_ref.at[me, part], r_ref.at[me], local_sem)
        copies = []
        for k in range(1, N_DEV):
            px, py, pc = (1 - x if k & 4 else x), (1 - y if k & 2 else y), (1 - c if k & 1 else c)
            copies.append(pltpu.make_async_remote_copy(
                src_ref=g_ref.at[4 * px + 2 * py + pc, part], dst_ref=r_ref.at[me],
                send_sem=send_sems.at[k - 1], recv_sem=recv_sems.at[k - 1],
                device_id=(px, py, pc), device_id_type=_MESH))

        def at_first():
            mine.start()
            for cp in copies:
                cp.start()

        def at_last():
            for cp in copies:
                cp.wait()
            mine.wait()

        _when_each((first, at_first), (last, at_last))


def _matmul_residual(q, w, h, name):
    t, k = q.shape
    d = w.shape[1]
    tm = _row_tile(t, ROW_TILE_MATMUL)

    def body(q_ref, w_ref, h_ref, o_ref):
        o_ref[...] = h_ref[...] + _dot(q_ref[...], w_ref[...], _NN)

    return pl.pallas_call(
        body, name=name, grid=(t // tm,),
        in_specs=[pl.BlockSpec((tm, k), lambda i: (i, 0)),
                  _resident((k, d)),
                  pl.BlockSpec((tm, d), lambda i: (i, 0))],
        out_specs=pl.BlockSpec((tm, d), lambda i: (i, 0)),
        out_shape=jax.ShapeDtypeStruct((t, d), F32),
        compiler_params=_params("parallel"),
    )(q, w, h)


def _dgrad_in_norm(dp, wt, h, g, dh_next, name, rider=_NO_RIDER):
    t, n = dp.shape
    d = wt.shape[1]
    tm = _row_tile(t, ROW_TILE_MATMUL)
    nt = t // tm

    def body(dp_ref, wt_ref, h_ref, g_ref, dhn_ref, *rest):
        r_in, (dh_ref, dg_ref), r_out, _, r_scratch = _split_refs(rider, 2, 0, rest)
        i = pl.program_id(0)
        rider(i == 0, i == nt // 2, None, r_in, r_out, r_scratch)
        dn = _dot(dp_ref[...], wt_ref[...], _NN)
        x = h_ref[...]
        ms = jnp.mean(x * x, axis=-1, keepdims=True)
        r = lax.rsqrt(ms + RMS_EPS)
        xhat = x * r
        dng = dn * g_ref[...]
        c = jnp.mean(dng * xhat, axis=-1, keepdims=True)
        dh_ref[...] = dhn_ref[...] + r * (dng - xhat * c)
        part = jnp.sum(dn * xhat, axis=0, keepdims=True)

        @pl.when(i == 0)
        def _():
            dg_ref[...] = part

        @pl.when(i > 0)
        def _():
            dg_ref[...] += part

        rider(None, None, i == nt - 1, r_in, r_out, r_scratch)

    return pl.pallas_call(
        body, name=name, grid=(nt,),
        in_specs=[pl.BlockSpec((tm, n), lambda i: (i, 0)),
                  _resident((n, d)),
                  pl.BlockSpec((tm, d), lambda i: (i, 0)),
                  pl.BlockSpec((1, d), lambda i: (0, 0)),
                  pl.BlockSpec((tm, d), lambda i: (i, 0))] + list(rider.in_specs),
        out_specs=[pl.BlockSpec((tm, d), lambda i: (i, 0)),
                   pl.BlockSpec((1, d), lambda i: (0, 0))] + list(rider.out_specs),
        out_shape=[jax.ShapeDtypeStruct((t, d), F32), jax.ShapeDtypeStruct((1, d), F32)] + list(rider.out_shape),
        scratch_shapes=list(rider.scratch_shapes),
        compiler_params=_params("arbitrary"),
    )(dp, wt, h, g, dh_next, *rider.inputs)


def _dgrad_in_to_input(dp, wt, h, g, dh_next, n_head, tm, name, rider=_NO_RIDER):
    t, n = dp.shape
    d = wt.shape[1]
    nt = t // tm

    def body(dp_ref, wt_ref, h_ref, g_ref, dhn_ref, *rest):
        r_in, (dg_ref, head_ref, rest_ref), r_out, (pbuf, tout, slabs, sems), r_scratch = _split_refs(rider, 3, 4, rest)
        i = pl.program_id(0)
        rider(i == 0, i == nt // 2, None, r_in, r_out, r_scratch)
        dn = _dot(dp_ref[...], wt_ref[...], _NN)
        x = h_ref[...]
        ms = jnp.mean(x * x, axis=-1, keepdims=True)
        r = lax.rsqrt(ms + RMS_EPS)
        xhat = x * r
        dng = dn * g_ref[...]
        c = jnp.mean(dng * xhat, axis=-1, keepdims=True)
        pbuf[...] = dhn_ref[...] + r * (dng - xhat * c)
        part = jnp.sum(dn * xhat, axis=0, keepdims=True)

        @pl.when(i == 0)
        def _():
            dg_ref[...] = part

        @pl.when(i > 0)
        def _():
            dg_ref[...] += part

        def store(j):
            if isinstance(j, int) and j == 0:
                return pltpu.make_async_copy(tout.at[0, pl.ds(n_head, tm - n_head)],
                                             rest_ref.at[pl.ds(0, tm - n_head)], sems.at[0])
            start = j * tm - n_head
            start = start if isinstance(start, int) else pl.multiple_of(start, SUBLANES)
            return pltpu.make_async_copy(tout.at[j % 2], rest_ref.at[pl.ds(start, tm)], sems.at[j % 2])

        @pl.when(i == 2)
        def _():
            store(0).wait()

        @pl.when(i > 2)
        def _():
            store(i - 2).wait()

        _tile_to_time_order(pbuf, slabs, tout.at[i % 2])

        @pl.when(i == 0)
        def _():
            head_ref[...] = tout[0, 0:n_head, :]
            store(0).start()

        @pl.when(i > 0)
        def _():
            store(i).start()

        @pl.when(i == nt - 1)
        def _():
            for j in (nt - 2, nt - 1):
                if j >= 0:
                    store(j).wait()

        rider(None, None, i == nt - 1, r_in, r_out, r_scratch)

    return pl.pallas_call(
        body, name=name, grid=(nt,),
        in_specs=[pl.BlockSpec((tm, n), lambda i: (i, 0)),
                  _resident((n, d)),
                  pl.BlockSpec((tm, d), lambda i: (i, 0)),
                  pl.BlockSpec((1, d), lambda i: (0, 0)),
                  pl.BlockSpec((tm, d), lambda i: (i, 0))] + list(rider.in_specs),
        out_specs=[pl.BlockSpec((1, d), lambda i: (0, 0)), _full((n_head, d)), _ANY] + list(rider.out_specs),
        out_shape=[jax.ShapeDtypeStruct((1, d), F32), jax.ShapeDtypeStruct((n_head, d), F32),
                   jax.ShapeDtypeStruct((t - n_head, d), F32)] + list(rider.out_shape),
        scratch_shapes=[pltpu.VMEM((tm, d), F32)] + _time_scratch(tm, d) + list(rider.scratch_shapes),
        compiler_params=_params("arbitrary"),
    )(dp, wt, h, g, dh_next, *rider.inputs)


def _wgrad(a, b, name, rider=_NO_RIDER):
    t, m = a.shape
    d = b.shape[1]
    tmm, tk = _col_tile(m, COL_TILE_WGRAD), _row_tile(t, ROW_TILE_WGRAD)
    nm, nk = m // tmm, t // tk

    def body(a_ref, b_ref, *rest):
        r_in, (o_ref,), r_out, _, r_scratch = _split_refs(rider, 1, 0, rest)
        i, k = pl.program_id(0), pl.program_id(1)
        rider(jnp.logical_and(i == 0, k == 0), None, None, r_in, r_out, r_scratch)

        @pl.when(k == 0)
        def _():
            o_ref[...] = jnp.zeros_like(o_ref)

        o_ref[...] += _dot(a_ref[...], b_ref[...], _TN)
        rider(None, None, jnp.logical_and(i == nm - 1, k == nk - 1), r_in, r_out, r_scratch)

    outs = pl.pallas_call(
        body, name=name, grid=(nm, nk),
        in_specs=[pl.BlockSpec((tk, tmm), lambda i, k: (k, i)),
                  pl.BlockSpec((tk, d), lambda i, k: (k, 0))] + list(rider.in_specs),
        out_specs=[pl.BlockSpec((tmm, d), lambda i, k: (i, 0))] + list(rider.out_specs),
        out_shape=[jax.ShapeDtypeStruct((m, d), F32)] + list(rider.out_shape),
        scratch_shapes=list(rider.scratch_shapes),
        compiler_params=pltpu.CompilerParams(dimension_semantics=("arbitrary", "arbitrary"),
                                             vmem_limit_bytes=VMEM_LIMIT_WIDE_BYTES),
    )(a, b, *rider.inputs)
    return outs if rider.out_shape else outs[0]


def _ffn_up_fused(h, g, w, conv_w, tm, name, rider=_NO_RIDER):
    t, d = h.shape
    f2 = w.shape[1]
    f = f2 // 2
    cw = min(STRIP_FFN, f)
    kw = conv_w.shape[0]
    nh = kw - 1
    nt = t // tm

    def body(h_ref, g_ref, w_ref, cw_ref, *rest):
        r_in, (n_ref, a_ref, z_ref), r_out, (carry,), r_scratch = _split_refs(rider, 3, 1, rest)
        i = pl.program_id(0)
        rider(i == 0, i == _hand_on_step(nt), None, r_in, r_out, r_scratch)

        @pl.when(i == 0)
        def _():
            carry[...] = jnp.zeros_like(carry)

        _rmsnorm_to(h_ref, g_ref, n_ref)
        for c in range(0, f, cw):
            conv = []
            for sl in (slice(c, c + cw), slice(f + c, f + c + cw)):
                a = _dot(n_ref[...], w_ref[:, sl], _NN)
                a_ref[:, sl] = a
                conv.append(_weighted(_rows(cw_ref, sl), _windows(a, _blocks(carry, sl, nh, True), kw)))
                carry[:, sl] = a[tm - nh * SUBLANES:tm]
            gg, vv = conv
            z_ref[:, c:c + cw] = (gg * _sigmoid(gg) * vv).astype(z_ref.dtype)
        rider(None, None, i == nt - 1, r_in, r_out, r_scratch)

    return pl.pallas_call(
        body, name=name, grid=(nt,),
        in_specs=[pl.BlockSpec((tm, d), lambda i: (i, 0)), pl.BlockSpec((1, d), lambda i: (0, 0)),
                  _resident((d, f2)), _full(conv_w.shape)] + list(rider.in_specs),
        out_specs=[pl.BlockSpec((tm, d), lambda i: (i, 0)), pl.BlockSpec((tm, f2), lambda i: (i, 0)),
                   pl.BlockSpec((tm, f), lambda i: (i, 0))] + list(rider.out_specs),
        out_shape=[jax.ShapeDtypeStruct((t, d), MXU_DTYPE), jax.ShapeDtypeStruct((t, f2), F32),
                   jax.ShapeDtypeStruct((t, f), MXU_DTYPE)] + list(rider.out_shape),
        scratch_shapes=[pltpu.VMEM((nh * SUBLANES, f2), F32)] + list(rider.scratch_shapes),
        compiler_params=_params("arbitrary"),
    )(h, g, w, conv_w, *rider.inputs)


def _ffn_down_bwd_fused(dh, wdt, a, conv_w, tm, name):
    t, d = dh.shape
    f = wdt.shape[1]
    f2 = 2 * f
    nt = t // tm
    cw = min(STRIP_FFN, f)
    kw = conv_w.shape[0]
    nh = kw - 1
    rev = lambda i: nt - 1 - i

    def body(dh_ref, wdt_ref, a_ref, *rest):
        halo_refs = rest[:nh]
        cw_ref, da_ref, dhb_ref, dw_ref, carry = rest[nh:]
        i = pl.program_id(0)
        newest, oldest = i == 0, i == nt - 1

        @pl.when(newest)
        def _():
            carry[...] = jnp.zeros_like(carry)
            dw_ref[...] = jnp.zeros_like(dw_ref)

        dhb_ref[...] = dh_ref[...].astype(dhb_ref.dtype)
        for c in range(0, f, cw):
            gsl, vsl = slice(c, c + cw), slice(f + c, f + c + cw)
            dz = _dot(dhb_ref[...], wdt_ref[:, gsl], _NN)
            win, conv = {}, {}
            for sl in (gsl, vsl):
                before = [jnp.where(oldest, 0.0, r[:, sl]) for r in halo_refs]
                win[sl.start] = _windows(a_ref[:, sl], before, kw)
                conv[sl.start] = _weighted(_rows(cw_ref, sl), win[sl.start])
            gg, vv = conv[gsl.start], conv[vsl.start]
            s = _sigmoid(gg)
            grads = {gsl.start: dz * vv * s * (1.0 + gg * (1.0 - s)), vsl.start: dz * gg * s}
            for sl in (gsl, vsl):
                dconv = grads[sl.start]
                da_ref[:, sl] = _conv_t(dconv, _blocks(carry, sl, nh, False), _rows(cw_ref, sl)).astype(da_ref.dtype)
                for k in range(kw):
                    dw_ref[k:k + 1, sl] += jnp.sum(dconv * win[sl.start][k], axis=0, keepdims=True)
                carry[:, sl] = dconv[0:nh * SUBLANES]

    return pl.pallas_call(
        body, name=name, grid=(nt,),
        in_specs=[pl.BlockSpec((tm, d), lambda i: (rev(i), 0)), _resident((d, f)),
                  pl.BlockSpec((tm, f2), lambda i: (rev(i), 0))] + _halo_specs(tm, f2, nh, rev) + [_full(conv_w.shape)],
        out_specs=[pl.BlockSpec((tm, f2), lambda i: (rev(i), 0)), pl.BlockSpec((tm, d), lambda i: (rev(i), 0)),
                   _full(conv_w.shape)],
        out_shape=[jax.ShapeDtypeStruct((t, f2), MXU_DTYPE), jax.ShapeDtypeStruct((t, d), MXU_DTYPE),
                   jax.ShapeDtypeStruct(conv_w.shape, F32)],
        scratch_shapes=[pltpu.VMEM((nh * SUBLANES, f2), F32)],
        compiler_params=_params("arbitrary"),
    )(dh, wdt, a, *([a] * nh), conv_w)


def _sc_in_fused(x, meta, g, w, conv_w, w_out, tm, name, rider=_NO_RIDER):
    d = x.shape[1]
    t = x.shape[0] + meta.shape[0]
    cw = min(STRIP, d)
    kw = conv_w.shape[0]
    nh = kw - 1
    nt = t // tm

    def body(x_ref, meta_ref, g_ref, w_ref, cw_ref, wo_ref, *rest):
        r_in, (h_ref, n_ref, p_ref, q_ref, h1_ref), r_out, (carry, tbuf, slabs, sems), r_scratch = _split_refs(
            rider, 5, 4, rest)
        i = pl.program_id(0)
        rider(i == 0, i == _hand_on_step(nt), None, r_in, r_out, r_scratch)

        @pl.when(i == 0)
        def _():
            carry[...] = jnp.zeros_like(carry)

        _fetch_time_tile(i, nt, x_ref, meta_ref[...], tbuf, sems, tm)
        _time_to_tile_order(tbuf.at[i % 2], slabs, h_ref)
        _rmsnorm_to(h_ref, g_ref, n_ref)
        for c in range(0, d, cw):
            sl = slice(c, c + cw)
            parts = []
            for base in (0, d, 2 * d):
                psl = slice(base + c, base + c + cw)
                parts.append(_dot(n_ref[...], w_ref[:, psl], _NN))
                p_ref[:, psl] = parts[-1]
            bg, cg, v = parts
            cv = cg * v
            u = _weighted(_rows(cw_ref, sl), _windows(cv, _blocks(carry, sl, nh, True), kw))
            carry[:, sl] = cv[tm - nh * SUBLANES:tm]
            q_ref[:, sl] = (bg * u).astype(q_ref.dtype)
        h1_ref[...] = h_ref[...] + _dot(q_ref[...], wo_ref[...], _NN)
        rider(None, None, i == nt - 1, r_in, r_out, r_scratch)

    row_tile = lambda cols: pl.BlockSpec((tm, cols), lambda i: (i, 0))
    return pl.pallas_call(
        body, name=name, grid=(nt,),
        in_specs=[_ANY, _full(meta.shape), pl.BlockSpec((1, d), lambda i: (0, 0)),
                  _resident((d, 3 * d)), _full(conv_w.shape), _resident(w_out.shape)] + list(rider.in_specs),
        out_specs=[row_tile(d), row_tile(d), row_tile(3 * d), row_tile(d), row_tile(d)] + list(rider.out_specs),
        out_shape=[jax.ShapeDtypeStruct((t, d), F32), jax.ShapeDtypeStruct((t, d), MXU_DTYPE),
                   jax.ShapeDtypeStruct((t, 3 * d), F32), jax.ShapeDtypeStruct((t, d), MXU_DTYPE),
                   jax.ShapeDtypeStruct((t, d), F32)] + list(rider.out_shape),
        scratch_shapes=[pltpu.VMEM((nh * SUBLANES, d), F32)] + _time_scratch(tm, d) + list(rider.scratch_shapes),
        compiler_params=_params("arbitrary"),
    )(x, meta, g, w, conv_w, w_out, *rider.inputs)


def _sc_out_bwd_fused(dh, wot, p, conv_w, tm, name, rider=_NO_RIDER):
    t, d = dh.shape
    nt = t // tm
    cw = min(STRIP, d)
    kw = conv_w.shape[0]
    nh = kw - 1
    rev = lambda i: nt - 1 - i

    def body(dh_ref, wot_ref, p_ref, *rest):
        halo_refs, cw_ref = rest[:nh], rest[nh]
        r_in, (dp_ref, dhb_ref, dw_ref), r_out, (carry,), r_scratch = _split_refs(rider, 3, 1, rest[nh + 1:])
        i = pl.program_id(0)
        newest, oldest = i == 0, i == nt - 1
        rider(newest, i == nt // 2, None, r_in, r_out, r_scratch)

        @pl.when(newest)
        def _():
            carry[...] = jnp.zeros_like(carry)
            dw_ref[...] = jnp.zeros_like(dw_ref)

        dhb_ref[...] = dh_ref[...].astype(dhb_ref.dtype)
        for c in range(0, d, cw):
            sl, csl, vsl = slice(c, c + cw), slice(d + c, d + c + cw), slice(2 * d + c, 2 * d + c + cw)
            w = _rows(cw_ref, sl)
            dq = _dot(dhb_ref[...], wot_ref[:, sl], _NN)
            bg, cg, v = p_ref[:, sl], p_ref[:, csl], p_ref[:, vsl]
            before = [jnp.where(oldest, 0.0, r[:, csl] * r[:, vsl]) for r in halo_refs]
            win = _windows(cg * v, before, kw)
            du = dq * bg
            dcv = _conv_t(du, _blocks(carry, sl, nh, False), w)
            dp_ref[:, sl] = (dq * _weighted(w, win)).astype(dp_ref.dtype)
            dp_ref[:, csl] = (dcv * v).astype(dp_ref.dtype)
            dp_ref[:, vsl] = (dcv * cg).astype(dp_ref.dtype)
            for k in range(kw):
                dw_ref[k:k + 1, sl] += jnp.sum(du * win[k], axis=0, keepdims=True)
            carry[:, sl] = du[0:nh * SUBLANES]
        rider(None, None, oldest, r_in, r_out, r_scratch)

    tile = lambda cols: pl.BlockSpec((tm, cols), lambda i: (rev(i), 0))
    return pl.pallas_call(
        body, name=name, grid=(nt,),
        in_specs=[tile(d), _resident((d, d)), tile(3 * d)] + _halo_specs(tm, 3 * d, nh, rev) + [_full(conv_w.shape)]
                 + list(rider.in_specs),
        out_specs=[tile(3 * d), tile(d), _full(conv_w.shape)] + list(rider.out_specs),
        out_shape=[jax.ShapeDtypeStruct((t, 3 * d), MXU_DTYPE), jax.ShapeDtypeStruct((t, d), MXU_DTYPE),
                   jax.ShapeDtypeStruct(conv_w.shape, F32)] + list(rider.out_shape),
        scratch_shapes=[pltpu.VMEM((nh * SUBLANES, d), F32)] + list(rider.scratch_shapes),
        compiler_params=_params("arbitrary"),
    )(dh, wot, p, *([p] * nh), conv_w, *rider.inputs)


def _pair_gate_weights(wa, wx):
    nb, bd, _ = wa.shape
    zero = jnp.zeros((nb // 2, bd, bd), wa.dtype)

    def pair(w):
        w = w.reshape(nb // 2, 2, bd, bd)
        top = jnp.concatenate([w[:, 0], zero], axis=2)
        bottom = jnp.concatenate([zero, w[:, 1]], axis=2)
        return jnp.concatenate([top, bottom], axis=1)

    both = jnp.concatenate([pair(wa), pair(wx)], axis=2)
    return both, jnp.swapaxes(both, 1, 2)


def _unpair_gate_grads(dw, bd):
    def blocks(cols):
        d0 = dw[:, :bd, cols:cols + bd]
        d1 = dw[:, bd:, cols + bd:cols + 2 * bd]
        return jnp.stack([d0, d1], axis=1).reshape(-1, bd, bd)
    return blocks(0), blocks(2 * bd)


def _rg_gates(u, k, wg_ref, ba, bx, lam):
    ub = u.astype(MXU_DTYPE)
    z = _dot(ub, wg_ref[k], _NN)
    half = z.shape[1] // 2
    r = _sigmoid_tanh(z[:, :half] + ba)
    ig = _sigmoid_tanh(z[:, half:] + bx)
    sp = _softplus(-lam)
    la = -RG_C * r * sp
    a = jnp.exp(la)
    th = jnp.tanh(la)
    mult = jnp.sqrt(-2.0 * th / (1.0 - th))
    return r, ig, a, mult


def _rg_in_fused(h, g, w, conv_w, conv_b, wg, ba, bx, lam, w_out, tm, name, rider=_NO_RIDER):
    t, d = h.shape
    r2 = w.shape[1]
    rr = r2 // 2
    nb, bd = wg.shape[0], wg.shape[1]
    kw = conv_w.shape[0]
    nh = kw - 1
    groups = tm // SUBLANES
    nt = t // tm

    def body(h_hbm, g_ref, w_ref, cw_ref, cb_ref, wg_ref, ba_ref, bx_ref, lam_ref, wo_ref, *rest):
        (r_in, (n_ref, p_ref, y_ref, hs_ref, u_ref, r_ref, ig_ref, a_ref, mult_ref, hn_ref), r_out,
         (a_scr, b_scr, carry_rb, carry_h, h_ring, h_sems), r_scratch) = _split_refs(rider, 10, 6, rest)
        i = pl.program_id(0)
        rider(i == 0, i == _hand_on_step(nt), None, r_in, r_out, r_scratch)

        def fetch(j):
            rows = pl.ds(pl.multiple_of(j * tm, SUBLANES), tm)
            return pltpu.make_async_copy(h_hbm.at[rows], h_ring.at[j % 3], h_sems.at[j % 3])

        @pl.when(i == 0)
        def _():
            carry_rb[...] = jnp.zeros_like(carry_rb)
            carry_h[...] = jnp.zeros_like(carry_h)
            for j in range(min(2, nt)):
                fetch(j).start()

        @pl.when(i + 2 < nt)
        def _():
            fetch(i + 2).start()

        fetch(i).wait()
        h_ref = h_ring.at[i % 3]
        _rmsnorm_to(h_ref, g_ref, n_ref)
        for k in range(nb):
            sl = slice(k * bd, (k + 1) * bd)
            rsl = slice(rr + k * bd, rr + (k + 1) * bd)
            p_ref[:, sl] = _dot(n_ref[...], w_ref[:, sl], _NN)
            rb = _dot(n_ref[...], w_ref[:, rsl], _NN)
            p_ref[:, rsl] = rb
            u = _weighted(_rows(cw_ref, sl), _windows(rb, _blocks(carry_rb, sl, nh, True), kw)) + cb_ref[:, sl]
            carry_rb[:, sl] = rb[tm - nh * SUBLANES:tm]
            r, ig, a, mult = _rg_gates(u, k, wg_ref, ba_ref[:, sl], bx_ref[:, sl], lam_ref[:, sl])
            for ref, val in ((u_ref, u), (r_ref, r), (ig_ref, ig), (a_ref, a), (mult_ref, mult)):
                ref[:, sl] = val
            a_scr[:, sl] = a
            b_scr[:, sl] = mult * (ig * u)

        carry_h[...] = _scan_tile(a_scr, b_scr, hs_ref, carry_h[...], groups, reverse=False)

        for k in range(nb):
            sl = slice(k * bd, (k + 1) * bd)
            y_ref[:, sl] = (hs_ref[:, sl] * _gelu(p_ref[:, sl])).astype(y_ref.dtype)
        hn_ref[...] = h_ref[...] + _dot(y_ref[...], wo_ref[...], _NN)
        rider(None, None, i == nt - 1, r_in, r_out, r_scratch)

    vm = lambda rows: pltpu.VMEM((rows, rr), F32)
    return pl.pallas_call(
        body, name=name, grid=(nt,),
        in_specs=[_ANY, pl.BlockSpec((1, d), lambda i: (0, 0)),
                  _resident((d, r2)), _full(conv_w.shape), _full(conv_b.shape), _full(wg.shape), _full(ba.shape),
                  _full(bx.shape), _full(lam.shape), _resident(w_out.shape)] + list(rider.in_specs),
        out_specs=[pl.BlockSpec((tm, d), lambda i: (i, 0)), pl.BlockSpec((tm, r2), lambda i: (i, 0))]
                  + [pl.BlockSpec((tm, rr), lambda i: (i, 0))] * 7 + [pl.BlockSpec((tm, d), lambda i: (i, 0))]
                  + list(rider.out_specs),
        out_shape=[jax.ShapeDtypeStruct((t, d), MXU_DTYPE), jax.ShapeDtypeStruct((t, r2), F32),
                   jax.ShapeDtypeStruct((t, rr), MXU_DTYPE)] + [jax.ShapeDtypeStruct((t, rr), F32)] * 6
                  + [jax.ShapeDtypeStruct((t, d), F32)] + list(rider.out_shape),
        scratch_shapes=[vm(tm), vm(tm), vm(nh * SUBLANES), vm(SUBLANES), pltpu.VMEM((3, tm, d), F32),
                        pltpu.SemaphoreType.DMA((3,))] + list(rider.scratch_shapes),
        compiler_params=pltpu.CompilerParams(dimension_semantics=("arbitrary",), vmem_limit_bytes=VMEM_LIMIT_WIDE_BYTES),
    )(h, g, w, conv_w, conv_b, wg, ba, bx, lam, w_out, *rider.inputs)


def _rg_out_bwd_fused(dh, wot, p, hs, gates, conv_w, wg, wgt, lam, tm, name, rider=_NO_RIDER):
    t, d = dh.shape
    r2 = p.shape[1]
    rr = r2 // 2
    nb, bd = wg.shape[0], wg.shape[1]
    nt = t // tm
    kw = conv_w.shape[0]
    nh = kw - 1
    groups = tm // SUBLANES
    rev = lambda i: nt - 1 - i

    def body(dh_ref, wot_ref, p_ref, hs_ref, hsh_ref, u_ref, r_ref, ig_ref, a_ref, mult_ref, *rest):
        halo_refs = rest[:nh]
        cw_ref, wg_ref, wgt_ref, lam_ref = rest[nh:nh + 4]
        (r_in, (dp_ref, dhb_ref, dcw_ref, dcb_ref, dwg_ref, dba_ref, dbx_ref, dlam_ref), r_out,
         (as_scr, g_scr, carry_g, carry_du), r_scratch) = _split_refs(rider, 8, 4, rest[nh + 4:])
        i = pl.program_id(0)
        newest, oldest = i == 0, i == nt - 1
        rider(newest, i == nt // 2, None, r_in, r_out, r_scratch)

        @pl.when(newest)
        def _():
            carry_g[...] = jnp.zeros_like(carry_g)
            carry_du[...] = jnp.zeros_like(carry_du)
            for ref in (dcw_ref, dcb_ref, dwg_ref, dba_ref, dbx_ref, dlam_ref):
                ref[...] = jnp.zeros_like(ref)

        dhb_ref[...] = dh_ref[...].astype(dhb_ref.dtype)
        ones = jnp.ones((SUBLANES, bd), F32)

        def conv_windows(sl, rsl):
            before = [jnp.where(oldest, 0.0, r[:, rsl]) for r in halo_refs]
            return _windows(p_ref[:, rsl], before, kw)

        for k in range(nb):
            sl = slice(k * bd, (k + 1) * bd)
            rsl = slice(rr + k * bd, rr + (k + 1) * bd)
            dy = _dot(dhb_ref[...], wot_ref[:, sl], _NN)
            gate, dgate = _gelu_and_grad(p_ref[:, sl])
            g_scr[:, sl] = dy * gate
            dp_ref[:, sl] = (dy * hs_ref[:, sl] * dgate).astype(dp_ref.dtype)
            as_scr[:, sl] = _shift_up(a_ref[:, sl], [ones], 1)

        g_first = _scan_tile(as_scr, g_scr, g_scr, carry_g[...], groups, reverse=True)
        carry_g[...] = jnp.broadcast_to(a_ref[0:1, :], (SUBLANES, rr)) * g_first

        for k in range(nb):
            sl = slice(k * bd, (k + 1) * bd)
            rsl = slice(rr + k * bd, rr + (k + 1) * bd)
            cw = _rows(cw_ref, sl)
            lam_k = lam_ref[:, sl]
            sp = _softplus(-lam_k)
            g = g_scr[:, sl]
            a, u, r, ig, mult = a_ref[:, sl], u_ref[:, sl], r_ref[:, sl], ig_ref[:, sl], mult_ref[:, sl]
            da = g * _shift_down(hs_ref[:, sl], [jnp.where(oldest, 0.0, hsh_ref[:, sl])], 1)
            dmult = g * (ig * u)
            d_iu = g * mult
            dla = da * a - dmult * (a * a) / mult
            dr = dla * (-RG_C * sp)
            dsp = jnp.sum(dla * (-RG_C * r), axis=0, keepdims=True)
            dlam_ref[:, sl] += dsp * (-_sigmoid(-lam_k))
            dza = dr * r * (1.0 - r)
            dzx = (d_iu * u) * ig * (1.0 - ig)
            dba_ref[:, sl] += jnp.sum(dza, axis=0, keepdims=True)
            dbx_ref[:, sl] += jnp.sum(dzx, axis=0, keepdims=True)
            ub = u.astype(MXU_DTYPE)
            dz = jnp.concatenate([dza, dzx], axis=1).astype(MXU_DTYPE)
            dwg_ref[k] += _dot(ub, dz, _TN)
            du = d_iu * ig + _dot(dz, wgt_ref[k], _NN)
            dcb_ref[:, sl] += jnp.sum(du, axis=0, keepdims=True)
            win = conv_windows(sl, rsl)
            for kk in range(kw):
                dcw_ref[kk:kk + 1, sl] += jnp.sum(du * win[kk], axis=0, keepdims=True)
            dp_ref[:, rsl] = _conv_t(du, _blocks(carry_du, sl, nh, False), cw).astype(dp_ref.dtype)
            carry_du[:, sl] = du[0:nh * SUBLANES]
        rider(None, None, oldest, r_in, r_out, r_scratch)

    tile = lambda cols: pl.BlockSpec((tm, cols), lambda i: (rev(i), 0))
    vm = lambda rows: pltpu.VMEM((rows, rr), F32)
    grads = [conv_w.shape, lam.shape, wg.shape, lam.shape, lam.shape, lam.shape]
    return pl.pallas_call(
        body, name=name, grid=(nt,),
        in_specs=[tile(d), _resident((d, rr)), tile(r2), tile(rr)] + _halo_specs(tm, rr, 1, rev) + [tile(rr)] * 5
                 + _halo_specs(tm, r2, nh, rev)
                 + [_full(conv_w.shape), _full(wg.shape), _full(wgt.shape), _full(lam.shape)] + list(rider.in_specs),
        out_specs=[tile(r2), tile(d)] + [_full(s) for s in grads] + list(rider.out_specs),
        out_shape=[jax.ShapeDtypeStruct((t, r2), MXU_DTYPE), jax.ShapeDtypeStruct((t, d), MXU_DTYPE)]
                  + [jax.ShapeDtypeStruct(s, F32) for s in grads] + list(rider.out_shape),
        scratch_shapes=[vm(tm)] * 2 + [vm(SUBLANES), vm(nh * SUBLANES)] + list(rider.scratch_shapes),
        compiler_params=pltpu.CompilerParams(dimension_semantics=("arbitrary",), vmem_limit_bytes=VMEM_LIMIT_WIDE_BYTES),
    )(dh, wot, p, hs, hs, *gates, *([p] * nh), conv_w, wg, wgt, lam, *rider.inputs)


def _down_loss_head(z, w, h, g, target, n_meta, tm, name):
    t, d = h.shape
    k = z.shape[1]
    groups = tm // SUBLANES
    nt = t // tm

    def body(z_ref, w_ref, h_ref, g_ref, t_ref, dh_ref, loss_ref, dg_ref, tgt, tbuf, slabs, sems):
        i = pl.program_id(0)
        _fetch_time_tile(i, nt, t_ref, jnp.zeros((n_meta, d), F32), tbuf, sems, tm)
        _time_to_tile_order(tbuf.at[i % 2], slabs, tgt)
        x = h_ref[...] + _dot(z_ref[...], w_ref[...], _NN)
        ms = jnp.mean(x * x, axis=-1, keepdims=True)
        r = lax.rsqrt(ms + RMS_EPS)
        xhat = x * r
        gg = g_ref[...]
        row = lax.broadcasted_iota(jnp.int32, (tm, 1), 0)
        time = i * tm + jnp.right_shift(row, 3) + jnp.bitwise_and(row, SUBLANES - 1) * groups
        err = jnp.where(time >= n_meta, xhat * gg - tgt[...], 0.0)
        dout = err * (1.0 / d)
        dng = dout * gg
        c = jnp.mean(dng * xhat, axis=-1, keepdims=True)
        dh_ref[...] = r * (dng - xhat * c)
        part_dg = jnp.sum(dout * xhat, axis=0, keepdims=True)
        part_loss = jnp.broadcast_to(0.5 * jnp.sum(err * dout, keepdims=True), loss_ref.shape)

        @pl.when(i == 0)
        def _():
            dg_ref[...] = part_dg
            loss_ref[...] = part_loss

        @pl.when(i > 0)
        def _():
            dg_ref[...] += part_dg
            loss_ref[...] += part_loss

    return pl.pallas_call(
        body, name=name, grid=(nt,),
        in_specs=[pl.BlockSpec((tm, k), lambda i: (i, 0)),
                  _resident((k, d)),
                  pl.BlockSpec((tm, d), lambda i: (i, 0)),
                  pl.BlockSpec((1, d), lambda i: (0, 0)),
                  _ANY],
        out_specs=[pl.BlockSpec((tm, d), lambda i: (i, 0)),
                   pl.BlockSpec((SUBLANES, LANES), lambda i: (0, 0)),
                   pl.BlockSpec((1, d), lambda i: (0, 0))],
        out_shape=[jax.ShapeDtypeStruct((t, d), F32), jax.ShapeDtypeStruct((SUBLANES, LANES), F32),
                   jax.ShapeDtypeStruct((1, d), F32)],
        scratch_shapes=[pltpu.VMEM((tm, d), F32)] + _time_scratch(tm, d),
        compiler_params=_params("arbitrary"),
    )(z, w, h, g, target)


def _adamw(w, g, m, v, name):
    rows, cols = w.shape
    tr = rows
    if rows > 512:
        for cand in range(8, 513, 8):
            if rows % cand == 0:
                tr = cand

    def body(w_ref, g_ref, m_ref, v_ref, d_ref, nm_ref, nv_ref):
        g_ = g_ref[...]
        m_ = ADAM_B1 * m_ref[...] + (1.0 - ADAM_B1) * g_
        v_ = ADAM_B2 * v_ref[...] + (1.0 - ADAM_B2) * (g_ * g_)
        m_hat = m_ / (1.0 - ADAM_B1 ** ADAM_STEP)
        v_hat = v_ / (1.0 - ADAM_B2 ** ADAM_STEP)
        d_ref[...] = -ADAM_LR * (m_hat / (jnp.sqrt(v_hat) + ADAM_EPS) + ADAM_WD * w_ref[...])
        nm_ref[...] = m_
        nv_ref[...] = v_

    spec = pl.BlockSpec((tr, cols), lambda i: (i, 0))
    shape = jax.ShapeDtypeStruct((rows, cols), F32)
    return pl.pallas_call(
        body, name=name, grid=(rows // tr,),
        in_specs=[spec] * 4, out_specs=[spec] * 3, out_shape=[shape] * 3,
        compiler_params=_params("parallel"),
    )(w, g, m, v)


def _adamw_nd(w, g, m, v, name):
    shape = w.shape
    two_d = (-1, shape[-1]) if w.ndim > 1 else (1, -1)
    outs = _adamw(w.reshape(two_d), g.reshape(two_d), m.reshape(two_d), v.reshape(two_d), name)
    return tuple(o.reshape(shape) for o in outs)


def _ride_alone(rider, name):
    def body(*refs):
        r_in, _, r_out, _, r_scratch = _split_refs(rider, 0, 0, refs)
        now = pl.program_id(0) == 0
        rider(now, None, None, r_in, r_out, r_scratch)
        rider(None, now, None, r_in, r_out, r_scratch)
        rider(None, None, now, r_in, r_out, r_scratch)

    return pl.pallas_call(
        body, name=name, grid=(1,),
        in_specs=list(rider.in_specs), out_specs=list(rider.out_specs), out_shape=list(rider.out_shape),
        scratch_shapes=list(rider.scratch_shapes),
        compiler_params=_params("arbitrary"),
    )(*rider.inputs)


def _sum_slots(parts, name):
    slots, rows, cols = parts.shape
    tr = _row_tile(rows, 256) if rows % 16 == 0 else rows

    def body(p_ref, o_ref):
        acc = p_ref[0]
        for s in range(1, slots):
            acc = acc + p_ref[s]
        o_ref[...] = acc

    return pl.pallas_call(
        body, name=name, grid=(rows // tr,),
        in_specs=[pl.BlockSpec((slots, tr, cols), lambda i: (0, i, 0))],
        out_specs=pl.BlockSpec((tr, cols), lambda i: (i, 0)),
        out_shape=jax.ShapeDtypeStruct((rows, cols), parts.dtype),
        compiler_params=_params("parallel"),
    )(parts)


def _pad_rows(flat, cols, multiple):
    n = flat.shape[0]
    rows = -(-n // cols)
    rows = -(-rows // multiple) * multiple
    return jnp.pad(flat, (0, rows * cols - n)).reshape(rows, cols)


def _cols_to_chunks(full):
    lead = full.shape[:-1]
    c = full.shape[-1] // N_DEV
    x = full.reshape(-1, N_DEV, c)
    return jnp.transpose(x, (1, 0, 2)).reshape(N_DEV, -1)


def _chunks_to_cols(chunks, lead):
    n = 1
    for s in lead:
        n *= s
    c = chunks.shape[1] // n
    x = chunks.reshape(N_DEV, n, c)
    return jnp.transpose(x, (1, 0, 2)).reshape(tuple(lead) + (N_DEV * c,))


def kernel(x, meta_tokens, norm_mix_g, norm_ffn_g, final_norm_g, sc_w_in, sc_conv_w, sc_w_out, rg_w_in, rg_conv_w, rg_conv_b, rg_w_gate_a, rg_b_gate_a, rg_w_gate_x, rg_b_gate_x, rg_lambda, rg_w_out, ffn_w_up, ffn_conv_w, ffn_w_down, loss_target, m_meta_tokens, m_norm_mix_g, m_norm_ffn_g, m_final_norm_g, m_sc_w_in, m_sc_conv_w, m_sc_w_out, m_rg_w_in, m_rg_conv_w, m_rg_conv_b, m_rg_w_gate_a, m_rg_b_gate_a, m_rg_w_gate_x, m_rg_b_gate_x, m_rg_lambda, m_rg_w_out, m_ffn_w_up, m_ffn_conv_w, m_ffn_w_down, v_meta_tokens, v_norm_mix_g, v_norm_ffn_g, v_final_norm_g, v_sc_w_in, v_sc_conv_w, v_sc_w_out, v_rg_w_in, v_rg_conv_w, v_rg_conv_b, v_rg_w_gate_a, v_rg_b_gate_a, v_rg_w_gate_x, v_rg_b_gate_x, v_rg_lambda, v_rg_w_out, v_ffn_w_up, v_ffn_conv_w, v_ffn_w_down):
    weights = dict(meta_tokens=meta_tokens, norm_mix_g=norm_mix_g, norm_ffn_g=norm_ffn_g, final_norm_g=final_norm_g,
                   sc_w_in=sc_w_in, sc_conv_w=sc_conv_w, sc_w_out=sc_w_out, rg_w_in=rg_w_in, rg_conv_w=rg_conv_w,
                   rg_conv_b=rg_conv_b, rg_w_gate_a=rg_w_gate_a, rg_b_gate_a=rg_b_gate_a, rg_w_gate_x=rg_w_gate_x,
                   rg_b_gate_x=rg_b_gate_x, rg_lambda=rg_lambda, rg_w_out=rg_w_out, ffn_w_up=ffn_w_up,
                   ffn_conv_w=ffn_conv_w, ffn_w_down=ffn_w_down)
    m_in = dict(meta_tokens=m_meta_tokens, norm_mix_g=m_norm_mix_g, norm_ffn_g=m_norm_ffn_g, final_norm_g=m_final_norm_g,
                sc_w_in=m_sc_w_in, sc_conv_w=m_sc_conv_w, sc_w_out=m_sc_w_out, rg_w_in=m_rg_w_in, rg_conv_w=m_rg_conv_w,
                rg_conv_b=m_rg_conv_b, rg_w_gate_a=m_rg_w_gate_a, rg_b_gate_a=m_rg_b_gate_a, rg_w_gate_x=m_rg_w_gate_x,
                rg_b_gate_x=m_rg_b_gate_x, rg_lambda=m_rg_lambda, rg_w_out=m_rg_w_out, ffn_w_up=m_ffn_w_up,
                ffn_conv_w=m_ffn_conv_w, ffn_w_down=m_ffn_w_down)
    v_in = dict(meta_tokens=v_meta_tokens, norm_mix_g=v_norm_mix_g, norm_ffn_g=v_norm_ffn_g, final_norm_g=v_final_norm_g,
                sc_w_in=v_sc_w_in, sc_conv_w=v_sc_conv_w, sc_w_out=v_sc_w_out, rg_w_in=v_rg_w_in, rg_conv_w=v_rg_conv_w,
                rg_conv_b=v_rg_conv_b, rg_w_gate_a=v_rg_w_gate_a, rg_b_gate_a=v_rg_b_gate_a, rg_w_gate_x=v_rg_w_gate_x,
                rg_b_gate_x=v_rg_b_gate_x, rg_lambda=v_rg_lambda, rg_w_out=v_rg_w_out, ffn_w_up=v_ffn_w_up,
                ffn_conv_w=v_ffn_conv_w, ffn_w_down=v_ffn_w_down)
    names = list(weights)

    seq, d = x.shape[1], x.shape[2]
    n_meta = meta_tokens.shape[0]
    n_ffn = ffn_w_up.shape[0]

    assert n_ffn == 2
    def shard_rows(w_in, w_out):
        return [w_in.T.astype(MXU_DTYPE), w_out.astype(MXU_DTYPE)]

    def both_orientations(in_t_gathered, out_gathered):
        w_in_t = in_t_gathered.reshape(-1, d)
        w_out = out_gathered.reshape(-1, d)
        return w_in_t.T, w_in_t, w_out, w_out.T

    shards = {"ffn0": shard_rows(ffn_w_up[0], ffn_w_down[0]), "rg": shard_rows(rg_w_in[0], rg_w_out[0]),
              "ffn1": shard_rows(ffn_w_up[1], ffn_w_down[1])}
    ffn_w_up_f, ffn_w_up_t, ffn_w_down_f, ffn_w_down_t = [None] * 2, [None] * 2, [None] * 2, [None] * 2

    small_names = ["meta_tokens", "sc_conv_w", "rg_conv_w", "rg_conv_b", "rg_b_gate_a", "rg_b_gate_x", "rg_lambda",
                   "ffn_conv_w"]
    small_lead = {n: weights[n].shape[:-1] for n in small_names}
    small_sizes = [weights[n].size for n in small_names]
    small_flat = jnp.concatenate([weights[n].reshape(-1) for n in small_names])
    small_rows = _pad_rows(small_flat, d, SUBLANES)
    sc_in_g, sc_out_g, small_g = _ride_alone(
        _GatherRider(shard_rows(sc_w_in[0], sc_w_out[0]) + [small_rows]), "gather_first")
    sc_w_in_f, sc_w_in_t, sc_w_out_f, sc_w_out_t = both_orientations(sc_in_g, sc_out_g)
    small_g = small_g.reshape(N_DEV, -1)
    small_full = {}
    o = 0
    for n, sz in zip(small_names, small_sizes):
        small_full[n] = _chunks_to_cols(small_g[:, o:o + sz], small_lead[n])
        o += sz

    wg, wgt = _pair_gate_weights(rg_w_gate_a[0].astype(MXU_DTYPE), rg_w_gate_x[0].astype(MXU_DTYPE))
    rg_cw, rg_cb = small_full["rg_conv_w"][0], small_full["rg_conv_b"]
    rg_ba, rg_bx, rg_lam = small_full["rg_b_gate_a"], small_full["rg_b_gate_x"], small_full["rg_lambda"]
    sc_cw = small_full["sc_conv_w"][0]
    ffn_cw = small_full["ffn_conv_w"]

    tp = _row_tile(n_meta + seq, ROW_TILE_PERM)

    def ffn_fwd(h, l, rider):
        n, a, z, *gathered = _ffn_up_fused(h, norm_ffn_g[l:l + 1], ffn_w_up_f[l], ffn_cw[l], tp, f"ffn{l}_up", rider)
        return _matmul_residual(z, ffn_w_down_f[l], h, f"ffn{l}_down"), (n, a, z), gathered

    h0, n0, p0, q0, h1, *ffn0_w = _sc_in_fused(x[0], small_full["meta_tokens"], norm_mix_g[0:1], sc_w_in_f, sc_cw,
                                               sc_w_out_f, tp, "sc_mixer", _GatherRider(shards["ffn0"]))
    ffn_w_up_f[0], ffn_w_up_t[0], ffn_w_down_f[0], ffn_w_down_t[0] = both_orientations(*ffn0_w)
    h2, ffn0_saved, rg_w = ffn_fwd(h1, 0, _GatherRider(shards["rg"]))
    rg_w_in_f, rg_w_in_t, rg_w_out_f, rg_w_out_t = both_orientations(*rg_w)
    n2, p2, y2, hs2, *rest2 = _rg_in_fused(h2, norm_mix_g[1:2], rg_w_in_f, rg_cw, rg_cb, wg, rg_ba, rg_bx, rg_lam,
                                           rg_w_out_f, tp, "rg_mixer", _GatherRider(shards["ffn1"]))
    rg_gates, h3, ffn1_w = rest2[:5], rest2[5], rest2[6:]
    ffn_w_up_f[1], ffn_w_up_t[1], ffn_w_down_f[1], ffn_w_down_t[1] = both_orientations(*ffn1_w)
    ffn1_saved = tuple(_ffn_up_fused(h3, norm_ffn_g[1:2], ffn_w_up_f[1], ffn_cw[1], tp, "ffn1_up"))
    dh4, loss_tile, d_final_g = _down_loss_head(ffn1_saved[2], ffn_w_down_f[1], h3, final_norm_g.reshape(1, d),
                                                loss_target[0], n_meta, tp, "ffn1_down_loss_head")
    loss = lax.psum(loss_tile[0, 0], AXES)

    arrived = {}

    def ffn_bwd(dh_out, h_in, saved, l):
        n, a, z = saved
        da, dhb, d_cw = _ffn_down_bwd_fused(dh_out, ffn_w_down_t[l], a, ffn_cw[l], tp, f"ffn{l}_down_bwd")
        d_w_down = _wgrad(z, dhb, f"ffn{l}_down_wgrad")
        d_w_up_t, arrived[f"ffn_w_down{l}"] = _wgrad(da, n, f"ffn{l}_up_wgrad", _ScatterRider(d_w_down))
        rows = d_w_up_t.shape[0] // N_DEV
        first_rows = (rows // 2) // (2 * SUBLANES) * (2 * SUBLANES)
        dh_in, d_g, arrived[f"ffn_w_up{l}a"] = _dgrad_in_norm(
            da, ffn_w_up_t[l], h_in, norm_ffn_g[l:l + 1], dh_out, f"ffn{l}_up_dgrad",
            _ScatterRider(d_w_up_t, (0, first_rows)))
        return dh_in, d_cw, d_g, _ScatterRider(d_w_up_t, (first_rows, rows - first_rows))

    dh3, d_fcw1, d_fg1, up1_rest = ffn_bwd(dh4, h3, ffn1_saved, 1)

    dp2, dhb3, d_rg_cw, d_rg_cb, d_wg, d_rg_ba, d_rg_bx, d_rg_lam, arrived["ffn_w_up1b"] = _rg_out_bwd_fused(
        dh3, rg_w_out_t, p2, hs2, rg_gates, rg_cw, wg, wgt, rg_lam, tp, "rg_out_bwd", up1_rest)
    d_wa, d_wx = _unpair_gate_grads(d_wg, rg_w_gate_a.shape[2])
    d_rg_w_out = _wgrad(y2, dhb3, "rg_out_wgrad")
    d_rg_w_in_t, arrived["rg_w_out"] = _wgrad(dp2, n2, "rg_in_wgrad", _ScatterRider(d_rg_w_out))
    dh2, d_mg1, arrived["rg_w_in"] = _dgrad_in_norm(
        dp2, rg_w_in_t, h2, norm_mix_g[1:2], dh3, "rg_in_dgrad", _ScatterRider(d_rg_w_in_t))

    dh1, d_fcw0, d_fg0, up0_rest = ffn_bwd(dh2, h1, ffn0_saved, 0)

    dp0, dhb1, d_sc_cw, arrived["ffn_w_up0b"] = _sc_out_bwd_fused(dh1, sc_w_out_t, p0, sc_cw, tp, "sc_out_bwd", up0_rest)
    d_sc_w_out = _wgrad(q0, dhb1, "sc_out_wgrad")
    d_sc_w_in_t, arrived["sc_w_out"] = _wgrad(dp0, n0, "sc_in_wgrad", _ScatterRider(d_sc_w_out))
    d_mg0, d_meta, d_x, arrived["sc_w_in"] = _dgrad_in_to_input(
        dp0, sc_w_in_t, h0, norm_mix_g[0:1], dh1, n_meta, tp, "sc_in_dgrad", _ScatterRider(d_sc_w_in_t))
    grad_x = d_x[None]

    grads = {}
    for n in ("sc_w_in", "rg_w_in"):
        grads[n] = _sum_slots(arrived[n], f"sum_{n}").T[None]
    for n in ("sc_w_out", "rg_w_out"):
        grads[n] = _sum_slots(arrived[n], f"sum_{n}")[None]
    grads["ffn_w_up"] = jnp.stack([jnp.concatenate(
        [_sum_slots(arrived[f"ffn_w_up{l}{part}"], f"sum_ffn_w_up{l}{part}") for part in "ab"], axis=0).T
        for l in range(n_ffn)])
    grads["ffn_w_down"] = jnp.stack([_sum_slots(arrived[f"ffn_w_down{l}"], f"sum_ffn_w_down{l}") for l in range(n_ffn)])

    small_grads = {"meta_tokens": d_meta,"sc_conv_w": d_sc_cw[None], "rg_conv_w": d_rg_cw[None],
                   "rg_conv_b": d_rg_cb, "rg_b_gate_a": d_rg_ba, "rg_b_gate_x": d_rg_bx, "rg_lambda": d_rg_lam,
                   "ffn_conv_w": jnp.stack([d_fcw0, d_fcw1])}
    small_chunks = jnp.concatenate([_cols_to_chunks(small_grads[n]) for n in small_names], axis=1)
    pad = small_rows.shape[0] * d - small_chunks.shape[1]
    small_chunks = jnp.pad(small_chunks, ((0, 0), (0, pad))).reshape(N_DEV, small_rows.shape[0], d)
    rep_names = ["norm_mix_g", "norm_ffn_g", "final_norm_g", "rg_w_gate_a", "rg_w_gate_x"]
    rep_grads = {"norm_mix_g": jnp.concatenate([d_mg0, d_mg1], axis=0),
                 "norm_ffn_g": jnp.concatenate([d_fg0, d_fg1], axis=0),
                 "final_norm_g": d_final_g.reshape(-1), "rg_w_gate_a": d_wa[None], "rg_w_gate_x": d_wx[None]}
    rep_flat = jnp.concatenate([rep_grads[n].reshape(-1) for n in rep_names])
    rep_chunk_rows = -(-rep_flat.shape[0] // (N_DEV * d))
    rep_chunk_rows += -(small_rows.shape[0] + rep_chunk_rows) % 16
    rep_chunks = jnp.pad(rep_flat, (0, N_DEV * rep_chunk_rows * d - rep_flat.shape[0])).reshape(N_DEV, rep_chunk_rows, d)

    last_chunks = jnp.concatenate([small_chunks, rep_chunks], axis=1)
    reduced = _sum_slots(_ride_alone(_ScatterRider(last_chunks.reshape(-1, d)), "scatter_last")[0], "sum_last")

    small_red = reduced[0:small_rows.shape[0]].reshape(-1)
    o = small_rows.shape[0]
    so = 0
    for n, sz in zip(small_names, small_sizes):
        grads[n] = small_red[so:so + sz].reshape(weights[n].shape)
        so += sz
    rep_red = _ride_alone(_GatherRider([reduced[o:o + rep_chunks.shape[1]]]), "gather_replicated_grads")[0].reshape(-1)
    ro = 0
    for n in rep_names:
        sz = weights[n].size
        grads[n] = rep_red[ro:ro + sz].reshape(weights[n].shape)
        ro += sz

    delta, new_m, new_v = {}, {}, {}
    for n in names:
        delta[n], new_m[n], new_v[n] = _adamw_nd(weights[n], grads[n], m_in[n], v_in[n], f"adamw_{n}")

    return (loss, grad_x, *[grads[n] for n in names], *[delta[n] for n in names],
            *[new_m[n] for n in names], *[new_v[n] for n in names])
```
